```python
import jax, jax.numpy as jnp
from jax import lax
import numpy as np

D_MODEL = 1024
BATCH = 8
SEQ = 2048
DEPTH = 4
DEC_BATCH = 128
DEC_SEQ = 1
PAST_LEN = 16384
PAGE_SIZE = 128

MIX_WIDTH = D_MODEL
POOL_WIDTH = MIX_WIDTH // 2
POOL_GROUPS = 4
POOL_GROUP_DIM = POOL_WIDTH // POOL_GROUPS
POOL_WINDOWS = (2, 4, 8, 16)
POOL_BUF = max(POOL_WINDOWS) - 1
MLSTM_WIDTH = MIX_WIDTH - POOL_WIDTH
MLSTM_HEADS = 4
MLSTM_HEAD_DIM = MLSTM_WIDTH // MLSTM_HEADS
MLSTM_CHUNK = 64
N_MEM = 256
CA_HEADS = 4
CA_HEAD_DIM = D_MODEL // CA_HEADS
N_GROUPS = 4
EXPERTS_PER_GROUP = 4
N_EXPERTS = N_GROUPS * EXPERTS_PER_GROUP
TOP_K = 2
D_EXPERT = D_MODEL // 4
ALPHA = (2 * DEPTH) ** 0.25
BETA = (8 * DEPTH) ** -0.25
LN_EPS = 1e-5
GATE_OFF = POOL_WIDTH + 4 * MLSTM_WIDTH
IN_COLS = GATE_OFF + 2 * MLSTM_HEADS

kernel_name = "hymba_pool_mlstm_memxattn_hmoe_step"

F32 = jnp.float32


def layer_norm(x, g, b):
    xf = x.astype(F32)
    mu = xf.mean(-1, keepdims=True)
    var = jnp.mean(jnp.square(xf - mu), -1, keepdims=True)
    y = (xf - mu) * lax.rsqrt(var + LN_EPS)
    return (y * g.astype(F32) + b.astype(F32)).astype(x.dtype)


def pool_mix(u, prev, pool_w, pool_scale):
    B, T, C = u.shape
    P = prev.shape[1]
    ext = jnp.concatenate([prev.astype(F32), u.astype(F32)], axis=1)
    cs = jnp.concatenate([jnp.zeros((B, 1, C), F32), jnp.cumsum(ext, axis=1)], axis=1)
    hi = P + 1 + jnp.arange(T)
    outs = []
    for g, w in enumerate(POOL_WINDOWS):
        lo = jnp.maximum(hi - w, 0)
        sl = slice(g * POOL_GROUP_DIM, (g + 1) * POOL_GROUP_DIM)
        csg = cs[..., sl]
        cnt = (hi - lo).astype(F32)[None, :, None]
        diff = (csg[:, hi] - csg[:, lo]) / cnt - ext[:, P:, sl]
        outs.append(jnp.einsum('btc,ce->bte', diff, pool_w[g].astype(F32)))
    y = jnp.concatenate(outs, -1) * pool_scale.astype(F32)
    return y.astype(u.dtype), ext[:, -POOL_BUF:].astype(u.dtype)


def mlstm_chunk(state, inp):
    C, n, m = state
    q, k, v, ig, lf = inp
    L = q.shape[2]
    b = jnp.cumsum(lf, axis=-1)
    inter = b + m[..., None]
    causal = jnp.tril(jnp.ones((L, L), dtype=bool))
    dmat = jnp.where(causal, b[..., :, None] - b[..., None, :] + ig[..., None, :], -jnp.inf)
    m_t = jnp.maximum(inter, dmat.max(-1))
    w_intra = jnp.exp(dmat - m_t[..., None])
    w_inter = jnp.exp(inter - m_t)
    s = jnp.einsum('bhjd,bhsd->bhjs', q, k) * w_intra
    num = jnp.einsum('bhjs,bhsv->bhjv', s, v) + w_inter[..., None] * jnp.einsum('bhvd,bhjd->bhjv', C, q)
    qn = s.sum(-1) + w_inter * jnp.einsum('bhd,bhjd->bhj', n, q)
    h = num / jnp.maximum(jnp.abs(qn), jnp.exp(-m_t))[..., None]
    m_new = m_t[..., -1]
    w_state = jnp.exp(b[..., -1:] - b + ig - m_new[..., None])
    decay = jnp.exp(b[..., -1] + m - m_new)
    C_new = decay[..., None, None] * C + jnp.einsum('bhs,bhsv,bhsd->bhvd', w_state, v, k)
    n_new = decay[..., None] * n + jnp.einsum('bhs,bhsd->bhd', w_state, k)
    return (C_new, n_new, m_new), h


def mlstm_seq(q, k, v, ig, lf, C0, n0, m0):
    B, H, T, _ = q.shape
    L = MLSTM_CHUNK if T % MLSTM_CHUNK == 0 else T
    nc = T // L

    def to_chunks(a):
        a = a.reshape(a.shape[:2] + (nc, L) + a.shape[3:])
        return jnp.moveaxis(a, 2, 0)

    state, h = lax.scan(mlstm_chunk, (C0, n0, m0), tuple(to_chunks(a) for a in (q, k, v, ig, lf)))
    h = jnp.moveaxis(h, 0, 2).reshape(B, H, T, -1)
    return h, state


def hier_moe(x, p):
    B, T, D = x.shape
    xf = x.reshape(B * T, D)
    g_logits = (xf @ p['w_gr']).astype(F32) + p['b_gr'].astype(F32)
    pg = jax.nn.softmax(g_logits, -1)
    g_sel = jnp.argmax(g_logits, -1)
    e_logits = ((xf @ p['w_er']).astype(F32) + p['b_er'].astype(F32)).reshape(-1, N_GROUPS, EXPERTS_PER_GROUP)
    e_in = jnp.take_along_axis(e_logits, g_sel[:, None, None], axis=1)[:, 0]
    top_p, top_i = lax.top_k(jax.nn.softmax(e_in, -1), TOP_K)
    gate = jnp.take_along_axis(pg, g_sel[:, None], 1) * top_p / top_p.sum(-1, keepdims=True)
    e_idx = g_sel[:, None] * EXPERTS_PER_GROUP + top_i
    combine = jnp.einsum('nk,nke->ne', gate, jax.nn.one_hot(e_idx, N_EXPERTS, dtype=F32)).astype(x.dtype)
    hg = jnp.einsum('nd,edf->nef', xf, p['w_gate'])
    hu = jnp.einsum('nd,edf->nef', xf, p['w_up'])
    hh = jax.nn.silu(hg) * hu * combine[..., None]
    y = jnp.einsum('nef,efd->nd', hh, p['w_down'])
    return y.reshape(B, T, D)


def trunk_layer(x, mem_k, mem_v, pool_prev, C0, n0, m0, p):
    B, T, _ = x.shape
    h = jnp.einsum('btd,de->bte', x, p['w_in'])
    u = h[..., :POOL_WIDTH]
    q, k, v, og = [h[..., POOL_WIDTH + i * MLSTM_WIDTH: POOL_WIDTH + (i + 1) * MLSTM_WIDTH] for i in range(4)]
    ig = h[..., GATE_OFF:GATE_OFF + MLSTM_HEADS].astype(F32) + p['b_i'].astype(F32)
    fg = h[..., GATE_OFF + MLSTM_HEADS:].astype(F32) + p['b_f'].astype(F32)
    pool_out, pool_buf = pool_mix(u, pool_prev, p['pool_w'], p['pool_scale'])

    def heads(a):
        return a.reshape(B, T, MLSTM_HEADS, MLSTM_HEAD_DIM).transpose(0, 2, 1, 3).astype(F32)

    hm, (C1, n1, m1) = mlstm_seq(heads(q) * MLSTM_HEAD_DIM ** -0.5, heads(k), heads(v),
                                 ig.transpose(0, 2, 1), jax.nn.log_sigmoid(fg).transpose(0, 2, 1),
                                 C0.astype(F32), n0.astype(F32), m0.astype(F32))
    mu = hm.mean(-1, keepdims=True)
    var = jnp.mean(jnp.square(hm - mu), -1, keepdims=True)
    hn = ((hm - mu) * lax.rsqrt(var + LN_EPS)).transpose(0, 2, 1, 3).reshape(B, T, MLSTM_WIDTH)
    mlstm_out = (jax.nn.sigmoid(og.astype(F32)) * hn * p['mlstm_norm_g'].astype(F32)).astype(x.dtype)
    mix = jnp.einsum('bte,ed->btd', jnp.concatenate([pool_out, mlstm_out], -1), p['w_out'])
    x = layer_norm(ALPHA * x + mix, p['ln1_g'], p['ln1_b'])

    qc = (x @ p['ca_wq']).reshape(B, T, CA_HEADS, CA_HEAD_DIM)
    sc = jnp.einsum('bthe,bmhe->bhtm', qc.astype(F32), mem_k.astype(F32)) * CA_HEAD_DIM ** -0.5
    pa = jax.nn.softmax(sc, -1)
    ctx = jnp.einsum('bhtm,bmhe->bthe', pa, mem_v.astype(F32)).reshape(B, T, D_MODEL).astype(x.dtype)
    x = layer_norm(ALPHA * x + ctx @ p['ca_wo'], p['ln2_g'], p['ln2_b'])

    x = layer_norm(ALPHA * x + hier_moe(x, p), p['ln3_g'], p['ln3_b'])
    return x, pool_buf, C1, n1, m1


def setup_inputs(seed: int = 0) -> dict:
    key = jax.random.key(seed)
    ks = iter(jax.random.split(key, 48))

    def nrm(shape, scale):
        return jax.random.normal(next(ks), shape, F32) * scale

    def gain(shape):
        return 1.0 + nrm(shape, 0.1)

    pool_buf_len = min(POOL_BUF, PAST_LEN)
    L, D = DEPTH, D_MODEL
    return {
        'x_prompt': nrm((BATCH, SEQ, D), 1.0),
        'x_sample': nrm((DEC_BATCH, DEC_SEQ, D), 1.0),
        'mem_prompt': nrm((BATCH, N_MEM, D), 1.0),
        'cache_pool': nrm((L, DEC_BATCH, pool_buf_len, POOL_WIDTH), 1.0),
        'state_mlstm_C': nrm((L, DEC_BATCH, MLSTM_HEADS, MLSTM_HEAD_DIM, MLSTM_HEAD_DIM), 0.1),
        'state_mlstm_n': nrm((L, DEC_BATCH, MLSTM_HEADS, MLSTM_HEAD_DIM), 0.1),
        'state_mlstm_m': nrm((L, DEC_BATCH, MLSTM_HEADS), 1.0),
        'cache_mem_k': nrm((L, DEC_BATCH, N_MEM, CA_HEADS, CA_HEAD_DIM), 1.0),
        'cache_mem_v': nrm((L, DEC_BATCH, N_MEM, CA_HEADS, CA_HEAD_DIM), BETA),
        'emb_ln_g': gain((D,)),
        'emb_ln_b': nrm((D,), 0.02),
        'w_in': nrm((L, D, IN_COLS), D ** -0.5),
        'b_i': nrm((L, MLSTM_HEADS), 0.1),
        'b_f': jnp.linspace(3.0, 6.0, MLSTM_HEADS, dtype=F32)[None] + nrm((L, MLSTM_HEADS), 0.1),
        'pool_w': nrm((L, POOL_GROUPS, POOL_GROUP_DIM, POOL_GROUP_DIM), POOL_GROUP_DIM ** -0.5),
        'pool_scale': gain((L, POOL_WIDTH)),
        'mlstm_norm_g': gain((L, MLSTM_WIDTH)),
        'w_out': nrm((L, MIX_WIDTH, D), MIX_WIDTH ** -0.5 * BETA),
        'ln1_g': gain((L, D)),
        'ln1_b': nrm((L, D), 0.02),
        'ca_wq': nrm((L, D, D), D ** -0.5),
        'ca_wk': nrm((L, D, D), D ** -0.5),
        'ca_wv': nrm((L, D, D), D ** -0.5 * BETA),
        'ca_wo': nrm((L, D, D), D ** -0.5 * BETA),
        'ln2_g': gain((L, D)),
        'ln2_b': nrm((L, D), 0.02),
        'w_gr': nrm((L, D, N_GROUPS), D ** -0.5),
        'b_gr': nrm((L, N_GROUPS), 0.01),
        'w_er': nrm((L, D, N_EXPERTS), D ** -0.5),
        'b_er': nrm((L, N_EXPERTS), 0.01),
        'w_gate': nrm((L, N_EXPERTS, D, D_EXPERT), D ** -0.5),
        'w_up': nrm((L, N_EXPERTS, D, D_EXPERT), D ** -0.5),
        'w_down': nrm((L, N_EXPERTS, D_EXPERT, D), D_EXPERT ** -0.5 * BETA),
        'ln3_g': gain((L, D)),
        'ln3_b': nrm((L, D), 0.02),
    }


def reference(x_prompt, x_sample, mem_prompt, cache_pool, state_mlstm_C, state_mlstm_n, state_mlstm_m,
              cache_mem_k, cache_mem_v, emb_ln_g, emb_ln_b, w_in, b_i, b_f, pool_w, pool_scale,
              mlstm_norm_g, w_out, ln1_g, ln1_b, ca_wq, ca_wk, ca_wv, ca_wo, ln2_g, ln2_b,
              w_gr, b_gr, w_er, b_er, w_gate, w_up, w_down, ln3_g, ln3_b):
    xp = layer_norm(x_prompt, emb_ln_g, emb_ln_b)
    xs = layer_norm(x_sample, emb_ln_g, emb_ln_b)
    B = xp.shape[0]
    p_pool, p_C, p_n, p_m, p_mk, p_mv = [], [], [], [], [], []
    s_pool, s_C, s_n, s_m = [], [], [], []
    for l in range(DEPTH):
        p = dict(w_in=w_in[l], b_i=b_i[l], b_f=b_f[l], pool_w=pool_w[l], pool_scale=pool_scale[l],
                 mlstm_norm_g=mlstm_norm_g[l], w_out=w_out[l], ln1_g=ln1_g[l], ln1_b=ln1_b[l],
                 ca_wq=ca_wq[l], ca_wo=ca_wo[l], ln2_g=ln2_g[l], ln2_b=ln2_b[l],
                 w_gr=w_gr[l], b_gr=b_gr[l], w_er=w_er[l], b_er=b_er[l],
                 w_gate=w_gate[l], w_up=w_up[l], w_down=w_down[l], ln3_g=ln3_g[l], ln3_b=ln3_b[l])
        mk = (mem_prompt @ ca_wk[l]).reshape(B, N_MEM, CA_HEADS, CA_HEAD_DIM)
        mv = (mem_prompt @ ca_wv[l]).reshape(B, N_MEM, CA_HEADS, CA_HEAD_DIM)
        pool_prev = jnp.zeros((B, 0, POOL_WIDTH), xp.dtype)
        C0 = jnp.zeros((B, MLSTM_HEADS, MLSTM_HEAD_DIM, MLSTM_HEAD_DIM), F32)
        n0 = jnp.zeros((B, MLSTM_HEADS, MLSTM_HEAD_DIM), F32)
        m0 = jnp.zeros((B, MLSTM_HEADS), F32)
        xp, pb, C1, n1, m1 = trunk_layer(xp, mk, mv, pool_prev, C0, n0, m0, p)
        p_pool.append(pb)
        p_C.append(C1.astype(state_mlstm_C.dtype))
        p_n.append(n1.astype(state_mlstm_n.dtype))
        p_m.append(m1.astype(state_mlstm_m.dtype))
        p_mk.append(mk)
        p_mv.append(mv)
        xs, sb, Cs, ns, ms = trunk_layer(xs, cache_mem_k[l], cache_mem_v[l], cache_pool[l],
                                         state_mlstm_C[l], state_mlstm_n[l], state_mlstm_m[l], p)
        s_pool.append(sb.astype(cache_pool.dtype))
        s_C.append(Cs.astype(state_mlstm_C.dtype))
        s_n.append(ns.astype(state_mlstm_n.dtype))
        s_m.append(ms.astype(state_mlstm_m.dtype))
    return (xp, xs, jnp.stack(p_pool), jnp.stack(p_C), jnp.stack(p_n), jnp.stack(p_m),
            jnp.stack(p_mk), jnp.stack(p_mv), jnp.stack(s_pool), jnp.stack(s_C), jnp.stack(s_n), jnp.stack(s_m))
```

```python
import functools

import jax
import jax.numpy as jnp
from jax import lax
from jax.experimental import pallas as pl
from jax.experimental.pallas import tpu as pltpu

F32 = jnp.float32
BF16 = jnp.bfloat16

D_MODEL = 1024
DEPTH = 4
POOL_WIDTH = 512
POOL_GROUPS = 4
POOL_GROUP_DIM = 128
POOL_WINDOWS = (2, 4, 8, 16)
POOL_BUF = 15
MLSTM_WIDTH = 512
MLSTM_HEADS = 4
MLSTM_HEAD_DIM = 128
N_MEM = 256
CA_HEADS = 4
CA_HEAD_DIM = 256
N_GROUPS = 4
EXPERTS_PER_GROUP = 4
N_EXPERTS = 16
D_EXPERT = 256
ALPHA = (2 * DEPTH) ** 0.25
LN_EPS = 1e-5
GATE_OFF = POOL_WIDTH + 4 * MLSTM_WIDTH

LANES = 128
SUBLANES = 8
VMEM_LIMIT = 56 * 1024 * 1024
MLSTM_CHUNK = 256
POOL_ROWS = 2048
ROUTER_ROWS = 32


def _params(*sem):
    return pltpu.CompilerParams(dimension_semantics=sem, vmem_limit_bytes=VMEM_LIMIT)


def _dot(a, b):
    return jnp.dot(a, b, preferred_element_type=F32)


def _dot_nt(a, b, precision=None):
    return lax.dot_general(a, b, (((1,), (1,)), ((), ())), precision=precision,
                           preferred_element_type=F32)


def _dot_tn(a, b):
    return lax.dot_general(a, b, (((0,), (0,)), ((), ())), preferred_element_type=F32)


def _layer_norm(x, g, b):
    mu = jnp.mean(x, -1, keepdims=True)
    xc = x - mu
    var = jnp.mean(xc * xc, -1, keepdims=True)
    return xc * lax.rsqrt(var + LN_EPS) * g + b


def _unit_norm(x):
    mu = jnp.mean(x, -1, keepdims=True)
    xc = x - mu
    var = jnp.mean(xc * xc, -1, keepdims=True)
    return xc * lax.rsqrt(var + LN_EPS)


def _log_sigmoid(x):
    return jnp.minimum(x, 0.0) - jnp.log1p(jnp.exp(-jnp.abs(x)))


def _scan(x, axis, op, fill):
    n = x.shape[axis]
    idx = lax.broadcasted_iota(jnp.int32, x.shape, axis)
    s = 1
    while s < n:
        x = op(x, jnp.where(idx >= s, pltpu.roll(x, s, axis), fill))
        s *= 2
    return x


def _ln_kernel(x_ref, g_ref, b_ref, o_ref):
    o_ref[...] = _layer_norm(x_ref[...], g_ref[...], b_ref[...])


def _ln(x, g, b, tm):
    n, d = x.shape
    row = pl.BlockSpec((tm, d), lambda i: (i, 0))
    vec = pl.BlockSpec((1, d), lambda i: (0, 0))
    return pl.pallas_call(
        _ln_kernel, grid=(n // tm,), in_specs=[row, vec, vec], out_specs=row,
        out_shape=jax.ShapeDtypeStruct((n, d), F32), compiler_params=_params("parallel"),
        name="emb_ln")(x, g.reshape(1, d), b.reshape(1, d))


def _inproj_kernel(x_ref, w_ref, wg_ref, wgt_ref, u_ref, q_ref, k_ref, v_ref, og_ref,
                   gi_ref, gf_ref, git_ref, gft_ref):
    xb = x_ref[...].astype(BF16)
    w = MLSTM_WIDTH
    u_ref[...] = _dot(xb, w_ref[:, 0:POOL_WIDTH])
    q = _dot(xb, w_ref[:, POOL_WIDTH:POOL_WIDTH + w]) * (MLSTM_HEAD_DIM ** -0.5)
    q_ref[...] = q.astype(q_ref.dtype)
    k_ref[...] = _dot(xb, w_ref[:, POOL_WIDTH + w:POOL_WIDTH + 2 * w]).astype(k_ref.dtype)
    v_ref[...] = _dot(xb, w_ref[:, POOL_WIDTH + 2 * w:POOL_WIDTH + 3 * w]).astype(v_ref.dtype)
    og_ref[...] = _dot(xb, w_ref[:, POOL_WIDTH + 3 * w:POOL_WIDTH + 4 * w])
    g = _dot(xb, wg_ref[...])
    gi_ref[...] = g[:, 0:LANES]
    gf_ref[...] = g[:, LANES:2 * LANES]
    gt = _dot_nt(wgt_ref[...], xb)
    git_ref[...] = gt[0:SUBLANES]
    gft_ref[...] = gt[SUBLANES:2 * SUBLANES]


def _inproj(x, w_main, wg, wgt, seq, tm, qkv_dtype):
    n, d = x.shape
    nt = n // tm
    if seq % tm == 0:
        per = seq // tm
        gt_shape, gt_map = (n // seq, SUBLANES, seq), (lambda i: (i // per, 0, i % per))
    else:
        gt_shape, gt_map = (1, SUBLANES, n), (lambda i: (0, 0, i))
    row = lambda c: pl.BlockSpec((tm, c), lambda i: (i, 0))
    full = lambda a: pl.BlockSpec(a.shape, lambda i: (0, 0))
    gt_spec = pl.BlockSpec((None, SUBLANES, tm), gt_map)
    out_shape = (
        jax.ShapeDtypeStruct((n, POOL_WIDTH), F32),
        jax.ShapeDtypeStruct((n, MLSTM_WIDTH), qkv_dtype),
        jax.ShapeDtypeStruct((n, MLSTM_WIDTH), qkv_dtype),
        jax.ShapeDtypeStruct((n, MLSTM_WIDTH), qkv_dtype),
        jax.ShapeDtypeStruct((n, MLSTM_WIDTH), F32),
        jax.ShapeDtypeStruct((n, LANES), F32),
        jax.ShapeDtypeStruct((n, LANES), F32),
        jax.ShapeDtypeStruct(gt_shape, F32),
        jax.ShapeDtypeStruct(gt_shape, F32),
    )
    return pl.pallas_call(
        _inproj_kernel, grid=(nt,),
        in_specs=[row(d), full(w_main), full(wg), full(wgt)],
        out_specs=(row(POOL_WIDTH), row(MLSTM_WIDTH), row(MLSTM_WIDTH), row(MLSTM_WIDTH),
                   row(MLSTM_WIDTH), row(LANES), row(LANES), gt_spec, gt_spec),
        out_shape=out_shape, compiler_params=_params("parallel"), name="inproj")(x, w_main, wg, wgt)


def _pool_kernel(ext_ref, pw_ref, sc_ref, o_ref, *, period):
    rows = ext_ref.shape[0]
    r = lax.broadcasted_iota(jnp.int32, (rows, POOL_GROUP_DIM), 0)
    if period != rows:
        r = lax.rem(r, period)
    for g, win in enumerate(POOL_WINDOWS):
        sl = slice(g * POOL_GROUP_DIM, (g + 1) * POOL_GROUP_DIM)
        x = ext_ref[:, sl]
        acc = x
        s = 1
        while s < win:
            acc = acc + jnp.where(r >= s, pltpu.roll(acc, s, 0), 0.0)
            s *= 2
        cnt = jnp.minimum(r + 1, win).astype(F32)
        d = acc / cnt - x
        y = _dot(d.astype(BF16), pw_ref[g]) * sc_ref[:, sl]
        o_ref[:, sl] = y.astype(o_ref.dtype)


def _pool(ext2d, pw, scale, period):
    n, c = ext2d.shape
    row = pl.BlockSpec((POOL_ROWS, c), lambda i: (i, 0))
    return pl.pallas_call(
        functools.partial(_pool_kernel, period=period), grid=(n // POOL_ROWS,),
        in_specs=[row, pl.BlockSpec(pw.shape, lambda i: (0, 0, 0)), pl.BlockSpec((1, c), lambda i: (0, 0))],
        out_specs=row, out_shape=jax.ShapeDtypeStruct((n, c), BF16),
        compiler_params=_params("parallel"), name="pool")(ext2d, pw, scale.reshape(1, c))


def _mlstm_chunk_kernel(q_ref, k_ref, v_ref, gi_ref, gf_ref, git_ref, gft_ref, bi_ref, bf_ref,
                        bit_ref, bft_ref, hn_ref, c_ref, n_ref, m_ref):
    nb, L, _ = q_ref.shape

    @pl.when(pl.program_id(0) == 0)
    def _():
        c_ref[...] = jnp.zeros_like(c_ref)
        n_ref[...] = jnp.zeros_like(n_ref)
        m_ref[...] = jnp.zeros_like(m_ref)

    causal = (lax.broadcasted_iota(jnp.int32, (L, L), 0) >= lax.broadcasted_iota(jnp.int32, (L, L), 1))

    def per_batch(b, carry):
        ig = gi_ref[b] + bi_ref[...]
        lf = _log_sigmoid(gf_ref[b] + bf_ref[...])
        bc = _scan(lf, 0, jnp.add, 0.0)
        ac = ig - bc
        m_prev = m_ref[b]
        mc = jnp.maximum(_scan(ac, 0, jnp.maximum, -jnp.inf), m_prev)
        mt = bc + mc
        w_inter = jnp.exp(m_prev - mc)
        m_last = mc[L - 1:L, :]
        w_state = jnp.exp(ac - m_last)
        decay = jnp.exp(m_prev - m_last)
        floor = jnp.exp(-mt)
        m_ref[b] = mt[L - 1:L, :]
        igt = git_ref[b] + bit_ref[...]
        lft = _log_sigmoid(gft_ref[b] + bft_ref[...])
        at = igt - _scan(lft, 1, jnp.add, 0.0)
        for h in range(MLSTM_HEADS):
            sl = slice(h * MLSTM_HEAD_DIM, (h + 1) * MLSTM_HEAD_DIM)
            col = slice(h, h + 1)
            qh = q_ref[b, :, sl]
            kh = k_ref[b, :, sl]
            vh = v_ref[b, :, sl]
            s = _dot_nt(qh, kh)
            p = jnp.where(causal, s * jnp.exp(at[col, :] - mc[:, col]), 0.0)
            c_old = c_ref[b, h]
            n_old = n_ref[b, col, :]
            wi = w_inter[:, col]
            num = _dot(p.astype(BF16), vh) + wi * _dot_nt(qh, c_old.astype(BF16))
            qn = jnp.sum(p, -1, keepdims=True) + wi * jnp.sum(qh.astype(F32) * n_old, -1, keepdims=True)
            hh = num / jnp.maximum(jnp.abs(qn), floor[:, col])
            hn_ref[b, :, sl] = _unit_norm(hh)
            ws = w_state[:, col]
            dc = decay[:, col]
            c_ref[b, h] = dc * c_old + _dot_tn((vh.astype(F32) * ws).astype(BF16), kh)
            n_ref[b, col, :] = dc * n_old + jnp.sum(kh.astype(F32) * ws, 0, keepdims=True)
        return carry

    lax.fori_loop(0, nb, per_batch, 0)


def _mlstm_chunk(q, k, v, gi, gf, git, gft, bi, bf, bit, bft):
    nb, t, w = q.shape
    L = MLSTM_CHUNK
    seq = lambda c: pl.BlockSpec((nb, L, c), lambda i: (0, i, 0))
    seqt = pl.BlockSpec((nb, SUBLANES, L), lambda i: (0, 0, i))
    vec = lambda a: pl.BlockSpec(a.shape, lambda i: (0, 0))
    out_shape = (
        jax.ShapeDtypeStruct((nb, t, w), F32),
        jax.ShapeDtypeStruct((nb, MLSTM_HEADS, MLSTM_HEAD_DIM, MLSTM_HEAD_DIM), F32),
        jax.ShapeDtypeStruct((nb, MLSTM_HEADS, MLSTM_HEAD_DIM), F32),
        jax.ShapeDtypeStruct((nb, 1, LANES), F32),
    )
    out_specs = (
        seq(w),
        pl.BlockSpec(out_shape[1].shape, lambda i: (0, 0, 0, 0)),
        pl.BlockSpec(out_shape[2].shape, lambda i: (0, 0, 0)),
        pl.BlockSpec(out_shape[3].shape, lambda i: (0, 0, 0)),
    )
    return pl.pallas_call(
        _mlstm_chunk_kernel, grid=(t // L,),
        in_specs=[seq(w), seq(w), seq(w), seq(LANES), seq(LANES), seqt, seqt,
                  vec(bi), vec(bf), vec(bit), vec(bft)],
        out_specs=out_specs, out_shape=out_shape, compiler_params=_params("arbitrary"),
        name="mlstm_chunk")(q, k, v, gi, gf, git, gft, bi, bf, bit, bft)


def _mlstm_step_kernel(q_ref, k_ref, v_ref, gi_ref, gf_ref, bi_ref, bf_ref, c_ref, n_ref, m_ref,
                       hn_ref, co_ref, no_ref, mo_ref):
    nb = q_ref.shape[0]
    d = MLSTM_HEAD_DIM
    eye = (lax.broadcasted_iota(jnp.int32, (d, d), 0) == lax.broadcasted_iota(jnp.int32, (d, d), 1)).astype(F32)

    def per_batch(b, carry):
        ig = gi_ref[b] + bi_ref[...]
        lf = _log_sigmoid(gf_ref[b] + bf_ref[...])
        m_old = m_ref[b]
        mt = jnp.maximum(lf + m_old, ig)
        mo_ref[b] = mt
        wa = jnp.exp(ig - mt)
        wi = jnp.exp(lf + m_old - mt)
        fl = jnp.exp(-mt)
        for h in range(MLSTM_HEADS):
            sl = slice(h * d, (h + 1) * d)
            col = slice(h, h + 1)
            qh = q_ref[b, :, sl]
            kh = k_ref[b, :, sl]
            vh = v_ref[b, :, sl]
            c_old = c_ref[b, h]
            n_old = n_ref[b, col, :]
            s = jnp.sum(qh * kh, -1, keepdims=True) * wa[:, col]
            cq_col = jnp.sum(c_old * qh, -1, keepdims=True)
            cq = jnp.sum(eye * cq_col, 0, keepdims=True)
            num = s * vh + wi[:, col] * cq
            qn = s + wi[:, col] * jnp.sum(n_old * qh, -1, keepdims=True)
            hh = num / jnp.maximum(jnp.abs(qn), fl[:, col])
            hn_ref[b, :, sl] = _unit_norm(hh)
            v_col = jnp.sum(eye * vh, -1, keepdims=True)
            co_ref[b, h] = wi[:, col] * c_old + (wa[:, col] * v_col) * kh
            no_ref[b, col, :] = wi[:, col] * n_old + wa[:, col] * kh
        return carry

    lax.fori_loop(0, nb, per_batch, 0)


def _mlstm_step(q, k, v, gi, gf, bi, bf, c0, n0, m0, tb):
    n, _, w = q.shape
    d = MLSTM_HEAD_DIM
    row = lambda c: pl.BlockSpec((tb, 1, c), lambda i: (i, 0, 0))
    vec = lambda a: pl.BlockSpec(a.shape, lambda i: (0, 0))
    cspec = pl.BlockSpec((tb, MLSTM_HEADS, d, d), lambda i: (i, 0, 0, 0))
    nspec = pl.BlockSpec((tb, MLSTM_HEADS, d), lambda i: (i, 0, 0))
    out_shape = (
        jax.ShapeDtypeStruct((n, 1, w), F32),
        jax.ShapeDtypeStruct(c0.shape, F32),
        jax.ShapeDtypeStruct(n0.shape, F32),
        jax.ShapeDtypeStruct((n, 1, LANES), F32),
    )
    return pl.pallas_call(
        _mlstm_step_kernel, grid=(n // tb,),
        in_specs=[row(w), row(w), row(w), row(LANES), row(LANES), vec(bi), vec(bf), cspec, nspec, row(LANES)],
        out_specs=(row(w), cspec, nspec, row(LANES)), out_shape=out_shape,
        compiler_params=_params("parallel"), name="mlstm_step")(q, k, v, gi, gf, bi, bf, c0, n0, m0)


def _mixout_kernel(po_ref, hn_ref, og_ref, x_ref, ng_ref, w_ref, g_ref, b_ref, o_ref):
    mo = jax.nn.sigmoid(og_ref[...]) * hn_ref[...] * ng_ref[...]
    mix = _dot(po_ref[...], w_ref[0:POOL_WIDTH, :]) + _dot(mo.astype(BF16), w_ref[POOL_WIDTH:, :])
    o_ref[...] = _layer_norm(ALPHA * x_ref[...] + mix, g_ref[...], b_ref[...])


def _mixout(po, hn, og, x, ng, w, g, b, tm):
    n, d = x.shape
    row = lambda c: pl.BlockSpec((tm, c), lambda i: (i, 0))
    vec = lambda a: pl.BlockSpec(a.shape, lambda i: (0, 0))
    return pl.pallas_call(
        _mixout_kernel, grid=(n // tm,),
        in_specs=[row(POOL_WIDTH), row(MLSTM_WIDTH), row(MLSTM_WIDTH), row(d), vec(ng), vec(w), vec(g), vec(b)],
        out_specs=row(d), out_shape=jax.ShapeDtypeStruct((n, d), F32),
        compiler_params=_params("parallel"), name="mixout")(po, hn, og, x, ng, w, g, b)


def _mm_kernel(x_ref, w_ref, o_ref, *, scale):
    y = _dot(x_ref[...].astype(BF16), w_ref[...])
    if scale != 1.0:
        y = y * scale
    o_ref[...] = y.astype(o_ref.dtype)


def _mm(x, w, tm, out_dtype, scale=1.0, name="proj"):
    n, d = x.shape
    dout = w.shape[1]
    return pl.pallas_call(
        functools.partial(_mm_kernel, scale=scale), grid=(n // tm,),
        in_specs=[pl.BlockSpec((tm, d), lambda i: (i, 0)), pl.BlockSpec(w.shape, lambda i: (0, 0))],
        out_specs=pl.BlockSpec((tm, dout), lambda i: (i, 0)),
        out_shape=jax.ShapeDtypeStruct((n, dout), out_dtype),
        compiler_params=_params("parallel"), name=name)(x, w)


def _attn_kernel(q_ref, k_ref, v_ref, o_ref):
    tq = q_ref.shape[0]
    q = q_ref[...]
    if tq < 2 * SUBLANES:
        q = jnp.broadcast_to(q.astype(F32)[0:1], (2 * SUBLANES, q.shape[1]))
    qb = q.astype(BF16)
    kb = k_ref[...].astype(BF16)
    vb = v_ref[...].astype(BF16)
    for h in range(CA_HEADS):
        sl = slice(h * CA_HEAD_DIM, (h + 1) * CA_HEAD_DIM)
        s = _dot_nt(qb[:, sl], kb[:, sl])
        e = jnp.exp(s - jnp.max(s, -1, keepdims=True))
        ctx = _dot(e.astype(BF16), vb[:, sl]) / jnp.sum(e, -1, keepdims=True)
        o_ref[:, sl] = ctx[0:tq].astype(o_ref.dtype)


def _attn(q, k, v, tq):
    nb, t, d = q.shape
    qspec = pl.BlockSpec((None, tq, d), lambda b, i: (b, i, 0))
    kspec = pl.BlockSpec((None, N_MEM, d), lambda b, i: (b, 0, 0))
    return pl.pallas_call(
        _attn_kernel, grid=(nb, t // tq), in_specs=[qspec, kspec, kspec], out_specs=qspec,
        out_shape=jax.ShapeDtypeStruct((nb, t, d), q.dtype),
        compiler_params=_params("parallel", "parallel"), name="attn")(q, k, v)


def _mm_res_ln_kernel(a_ref, x_ref, w_ref, g_ref, b_ref, o_ref):
    y = _dot(a_ref[...].astype(BF16), w_ref[...])
    o_ref[...] = _layer_norm(ALPHA * x_ref[...] + y, g_ref[...], b_ref[...])


def _mm_res_ln(a, x, w, g, b, tm):
    n, d = x.shape
    row = lambda c: pl.BlockSpec((tm, c), lambda i: (i, 0))
    vec = lambda arr: pl.BlockSpec(arr.shape, lambda i: (0, 0))
    return pl.pallas_call(
        _mm_res_ln_kernel, grid=(n // tm,),
        in_specs=[row(a.shape[1]), row(d), vec(w), vec(g), vec(b)], out_specs=row(d),
        out_shape=jax.ShapeDtypeStruct((n, d), F32),
        compiler_params=_params("parallel"), name="proj_res_ln")(a, x, w, g, b)


def _route(lt):
    gl = [lt[g:g + 1, :] for g in range(N_GROUPS)]
    gmax = functools.reduce(jnp.maximum, gl)
    gsum = functools.reduce(jnp.add, [jnp.exp(x - gmax) for x in gl])
    pg_sel = 1.0 / gsum
    taken = jnp.zeros_like(gmax, dtype=jnp.bool_)
    g_hot = []
    for x in gl:
        hot = jnp.logical_and(x == gmax, jnp.logical_not(taken))
        taken = jnp.logical_or(taken, hot)
        g_hot.append(hot)
    el = []
    for j in range(EXPERTS_PER_GROUP):
        rows = [lt[SUBLANES + g * EXPERTS_PER_GROUP + j:SUBLANES + g * EXPERTS_PER_GROUP + j + 1, :]
                for g in range(N_GROUPS)]
        x = rows[N_GROUPS - 1]
        for g in range(N_GROUPS - 2, -1, -1):
            x = jnp.where(g_hot[g], rows[g], x)
        el.append(x)
    emax = functools.reduce(jnp.maximum, el)
    ee = [jnp.exp(x - emax) for x in el]
    esum = functools.reduce(jnp.add, ee)
    pe = [x / esum for x in ee]
    p1 = functools.reduce(jnp.maximum, pe)
    taken = jnp.zeros_like(gmax, dtype=jnp.bool_)
    hot1 = []
    for x in pe:
        hot = jnp.logical_and(x == p1, jnp.logical_not(taken))
        taken = jnp.logical_or(taken, hot)
        hot1.append(hot)
    rest = [jnp.where(h1, -jnp.inf, x) for h1, x in zip(hot1, pe)]
    p2 = functools.reduce(jnp.maximum, rest)
    taken = jnp.zeros_like(gmax, dtype=jnp.bool_)
    hot2 = []
    for x in rest:
        hot = jnp.logical_and(x == p2, jnp.logical_not(taken))
        taken = jnp.logical_or(taken, hot)
        hot2.append(hot)
    psum = p1 + p2
    gate = [jnp.where(h1, pg_sel * p1 / psum, jnp.where(h2, pg_sel * p2 / psum, 0.0))
            for h1, h2 in zip(hot1, hot2)]
    return [jnp.where(g_hot[g], gate[j], 0.0) for g in range(N_GROUPS) for j in range(EXPERTS_PER_GROUP)]


def _moe_kernel(x_ref, wr_ref, br_ref, wg_ref, wu_ref, wd_ref, g_ref, b_ref, o_ref,
                xb_ref, ct_ref, comb_ref, acc_ref):
    e = pl.program_id(1)

    @pl.when(e == 0)
    def _():
        x = x_ref[...]
        xb_ref[...] = x.astype(BF16)
        lt = _dot_nt(wr_ref[...], x, precision=lax.Precision.HIGHEST) + br_ref[...]
        ct_ref[...] = jnp.zeros_like(ct_ref)
        for i, row in enumerate(_route(lt)):
            ct_ref[i:i + 1, :] = row
        comb_ref[...] = ct_ref[...].T
        acc_ref[...] = jnp.zeros_like(acc_ref)

    xb = xb_ref[...]
    lane = lax.broadcasted_iota(jnp.int32, comb_ref.shape, 1)
    c = jnp.sum(jnp.where(lane == e, comb_ref[...], 0.0), -1, keepdims=True)
    hg = _dot(xb, wg_ref[...])
    hu = _dot(xb, wu_ref[...])
    hh = hg * jax.nn.sigmoid(hg) * hu * c
    acc_ref[...] += _dot(hh.astype(BF16), wd_ref[...])

    @pl.when(e == N_EXPERTS - 1)
    def _():
        o_ref[...] = _layer_norm(ALPHA * x_ref[...] + acc_ref[...], g_ref[...], b_ref[...])


def _moe(x, wr, br, wg, wu, wd, g, b, tm):
    n, d = x.shape
    row = pl.BlockSpec((tm, d), lambda i, e: (i, 0))
    vec = lambda a: pl.BlockSpec(a.shape, lambda i, e: (0, 0))
    scratch = [pltpu.VMEM((tm, d), BF16), pltpu.VMEM((LANES, tm), F32), pltpu.VMEM((tm, LANES), F32),
               pltpu.VMEM((tm, d), F32)]
    return pl.pallas_call(
        _moe_kernel, grid=(n // tm, N_EXPERTS),
        in_specs=[row, vec(wr), vec(br),
                  pl.BlockSpec((None, d, D_EXPERT), lambda i, e: (e, 0, 0)),
                  pl.BlockSpec((None, d, D_EXPERT), lambda i, e: (e, 0, 0)),
                  pl.BlockSpec((None, D_EXPERT, d), lambda i, e: (e, 0, 0)),
                  vec(g), vec(b)],
        out_specs=row, out_shape=jax.ShapeDtypeStruct((n, d), F32), scratch_shapes=scratch,
        compiler_params=_params("parallel", "arbitrary"), name="moe")(x, wr, br, wg, wu, wd, g, b)


def _pad_lanes(v):
    return jnp.zeros((1, LANES), F32).at[0, :v.shape[0]].set(v)


def _pad_rows(v):
    return jnp.zeros((SUBLANES, 1), F32).at[:v.shape[0], 0].set(v)


def _layer_weights(l, w_in, b_i, b_f, pool_w, pool_scale, mlstm_norm_g, w_out, ln1_g, ln1_b, ca_wq, ca_wo,
                   ln2_g, ln2_b, w_gr, b_gr, w_er, b_er, w_gate, w_up, w_down, ln3_g, ln3_b):
    d = D_MODEL
    wi = w_in[l]
    w_gate_cols = wi[:, GATE_OFF:]
    wg = jnp.zeros((d, 2 * LANES), F32)
    wg = wg.at[:, 0:MLSTM_HEADS].set(w_gate_cols[:, :MLSTM_HEADS])
    wg = wg.at[:, LANES:LANES + MLSTM_HEADS].set(w_gate_cols[:, MLSTM_HEADS:])
    wgt = jnp.zeros((2 * SUBLANES, d), F32)
    wgt = wgt.at[0:MLSTM_HEADS].set(w_gate_cols[:, :MLSTM_HEADS].T)
    wgt = wgt.at[SUBLANES:SUBLANES + MLSTM_HEADS].set(w_gate_cols[:, MLSTM_HEADS:].T)
    wr = jnp.zeros((ROUTER_ROWS, d), F32)
    wr = wr.at[0:N_GROUPS].set(w_gr[l].T).at[SUBLANES:SUBLANES + N_EXPERTS].set(w_er[l].T)
    br = jnp.zeros((ROUTER_ROWS, 1), F32)
    br = br.at[0:N_GROUPS, 0].set(b_gr[l]).at[SUBLANES:SUBLANES + N_EXPERTS, 0].set(b_er[l])
    row = lambda v: v.reshape(1, -1)
    return dict(
        w_main=wi[:, :GATE_OFF].astype(BF16), wg=wg.astype(BF16), wgt=wgt.astype(BF16),
        bi=_pad_lanes(b_i[l]), bf=_pad_lanes(b_f[l]), bit=_pad_rows(b_i[l]), bft=_pad_rows(b_f[l]),
        pool_w=pool_w[l].astype(BF16), pool_scale=pool_scale[l], norm_g=row(mlstm_norm_g[l]),
        w_out=w_out[l].astype(BF16), ln1=(row(ln1_g[l]), row(ln1_b[l])),
        wq=ca_wq[l].astype(BF16), wo=ca_wo[l].astype(BF16), ln2=(row(ln2_g[l]), row(ln2_b[l])),
        wr=wr, br=br, w_gate=w_gate[l].astype(BF16), w_up=w_up[l].astype(BF16),
        w_down=w_down[l].astype(BF16), ln3=(row(ln3_g[l]), row(ln3_b[l])))


def _trunk_layer(x, nb, seq, mem_k, mem_v, pool_prev, state, p, tm):
    chunked = state is None
    u, q, k, v, og, gi, gf, git, gft = _inproj(x, p["w_main"], p["wg"], p["wgt"], seq, tm,
                                               BF16 if chunked else F32)
    u3 = u.reshape(nb, seq, POOL_WIDTH)
    ext = u3 if pool_prev is None else jnp.concatenate([pool_prev, u3], axis=1)
    period = ext.shape[1]
    pooled = _pool(ext.reshape(nb * period, POOL_WIDTH), p["pool_w"], p["pool_scale"], period)
    pooled = pooled.reshape(nb, period, POOL_WIDTH)[:, period - seq:].reshape(nb * seq, POOL_WIDTH)
    pool_buf = ext[:, -POOL_BUF:]
    if chunked:
        r3 = lambda a: a.reshape(nb, seq, a.shape[-1])
        hn, c1, n1, m1 = _mlstm_chunk(r3(q), r3(k), r3(v), r3(gi), r3(gf), git, gft,
                                      p["bi"], p["bf"], p["bit"], p["bft"])
        hn = hn.reshape(nb * seq, MLSTM_WIDTH)
        m1 = m1[:, 0, :MLSTM_HEADS]
    else:
        c0, n0, m0 = state
        m0p = jnp.zeros((nb, 1, LANES), F32).at[:, 0, :MLSTM_HEADS].set(m0)
        r1 = lambda a: a.reshape(nb, 1, a.shape[-1])
        hn, c1, n1, m1 = _mlstm_step(r1(q), r1(k), r1(v), r1(gi), r1(gf), p["bi"], p["bf"], c0, n0, m0p,
                                     SUBLANES)
        hn = hn.reshape(nb, MLSTM_WIDTH)
        m1 = m1[:, 0, :MLSTM_HEADS]
    x = _mixout(pooled, hn, og, x, p["norm_g"], p["w_out"], *p["ln1"], tm)
    qc = _mm(x, p["wq"], tm, BF16 if chunked else F32, scale=CA_HEAD_DIM ** -0.5, name="ca_q")
    ctx = _attn(qc.reshape(nb, seq, D_MODEL), mem_k, mem_v, min(seq, tm))
    x = _mm_res_ln(ctx.reshape(nb * seq, D_MODEL), x, p["wo"], *p["ln2"], tm)
    x = _moe(x, p["wr"], p["br"], p["w_gate"], p["w_up"], p["w_down"], *p["ln3"], tm)
    return x, pool_buf, c1, n1, m1


def kernel(x_prompt, x_sample, mem_prompt, cache_pool, state_mlstm_C, state_mlstm_n, state_mlstm_m,
           cache_mem_k, cache_mem_v, emb_ln_g, emb_ln_b, w_in, b_i, b_f, pool_w, pool_scale,
           mlstm_norm_g, w_out, ln1_g, ln1_b, ca_wq, ca_wk, ca_wv, ca_wo, ln2_g, ln2_b,
           w_gr, b_gr, w_er, b_er, w_gate, w_up, w_down, ln3_g, ln3_b):
    bp, tp, d = x_prompt.shape
    bs, ts, _ = x_sample.shape
    tm_p, tm_s = 512, bs * ts
    xp = _ln(x_prompt.reshape(bp * tp, d), emb_ln_g, emb_ln_b, tm_p)
    xs = _ln(x_sample.reshape(bs * ts, d), emb_ln_g, emb_ln_b, tm_s)
    mem2d = mem_prompt.reshape(bp * N_MEM, d)
    outs = [[] for _ in range(10)]
    for l in range(DEPTH):
        p = _layer_weights(l, w_in, b_i, b_f, pool_w, pool_scale, mlstm_norm_g, w_out, ln1_g, ln1_b, ca_wq,
                           ca_wo, ln2_g, ln2_b, w_gr, b_gr, w_er, b_er, w_gate, w_up, w_down, ln3_g, ln3_b)
        mk = _mm(mem2d, ca_wk[l].astype(BF16), tm_p, F32, name="mem_k").reshape(bp, N_MEM, d)
        mv = _mm(mem2d, ca_wv[l].astype(BF16), tm_p, F32, name="mem_v").reshape(bp, N_MEM, d)
        xp, pb, c1, n1, m1 = _trunk_layer(xp, bp, tp, mk, mv, None, None, p, tm_p)
        xs, sb, cs, ns, ms = _trunk_layer(
            xs, bs, ts, cache_mem_k[l].reshape(bs, N_MEM, d), cache_mem_v[l].reshape(bs, N_MEM, d),
            cache_pool[l], (state_mlstm_C[l], state_mlstm_n[l], state_mlstm_m[l]), p, tm_s)
        for lst, val in zip(outs, (pb, c1, n1, m1, mk.reshape(bp, N_MEM, CA_HEADS, CA_HEAD_DIM),
                                   mv.reshape(bp, N_MEM, CA_HEADS, CA_HEAD_DIM), sb, cs, ns, ms)):
            lst.append(val)
    return (xp.reshape(bp, tp, d), xs.reshape(bs, ts, d)) + tuple(jnp.stack(o) for o in outs)
```

```python
import functools

import jax
import jax.numpy as jnp
from jax import lax
from jax.experimental import pallas as pl
from jax.experimental.pallas import tpu as pltpu

F32 = jnp.float32
BF16 = jnp.bfloat16

D_MODEL = 1024
DEPTH = 4
POOL_WIDTH = 512
POOL_GROUPS = 4
POOL_GROUP_DIM = 128
POOL_WINDOWS = (2, 4, 8, 16)
POOL_BUF = 15
MLSTM_WIDTH = 512
MLSTM_HEADS = 4
MLSTM_HEAD_DIM = 128
N_MEM = 256
CA_HEADS = 4
CA_HEAD_DIM = 256
N_GROUPS = 4
EXPERTS_PER_GROUP = 4
N_EXPERTS = 16
D_EXPERT = 256
ALPHA = (2 * DEPTH) ** 0.25
LN_EPS = 1e-5
GATE_OFF = POOL_WIDTH + 4 * MLSTM_WIDTH

LANES = 128
SUBLANES = 8
VMEM_LIMIT = 56 * 1024 * 1024
MLSTM_CHUNK = 256
POOL_ROWS = 2048
ROUTER_ROWS = 32


def _params(*sem):
    return pltpu.CompilerParams(dimension_semantics=sem, vmem_limit_bytes=VMEM_LIMIT)


def _dot(a, b):
    return jnp.dot(a, b, preferred_element_type=F32)


def _dot_nt(a, b, precision=None):
    return lax.dot_general(a, b, (((1,), (1,)), ((), ())), precision=precision,
                           preferred_element_type=F32)


def _dot_tn(a, b):
    return lax.dot_general(a, b, (((0,), (0,)), ((), ())), preferred_element_type=F32)


def _layer_norm(x, g, b):
    mu = jnp.mean(x, -1, keepdims=True)
    xc = x - mu
    var = jnp.mean(xc * xc, -1, keepdims=True)
    return xc * lax.rsqrt(var + LN_EPS) * g + b


def _unit_norm(x):
    mu = jnp.mean(x, -1, keepdims=True)
    xc = x - mu
    var = jnp.mean(xc * xc, -1, keepdims=True)
    return xc * lax.rsqrt(var + LN_EPS)


def _log_sigmoid(x):
    return jnp.minimum(x, 0.0) - jnp.log1p(jnp.exp(-jnp.abs(x)))


def _scan(x, axis, op, fill):
    n = x.shape[axis]
    idx = lax.broadcasted_iota(jnp.int32, x.shape, axis)
    s = 1
    while s < n:
        x = op(x, jnp.where(idx >= s, pltpu.roll(x, s, axis), fill))
        s *= 2
    return x


def _ln_kernel(x_ref, g_ref, b_ref, o_ref):
    o_ref[...] = _layer_norm(x_ref[...], g_ref[...], b_ref[...])


def _ln(x, g, b, tm):
    n, d = x.shape
    row = pl.BlockSpec((tm, d), lambda i: (i, 0))
    vec = pl.BlockSpec((1, d), lambda i: (0, 0))
    return pl.pallas_call(
        _ln_kernel, grid=(n // tm,), in_specs=[row, vec, vec], out_specs=row,
        out_shape=jax.ShapeDtypeStruct((n, d), F32), compiler_params=_params("parallel"),
        name="emb_ln")(x, g.reshape(1, d), b.reshape(1, d))


def _inproj_kernel(x_ref, w_ref, wg_ref, wgt_ref, u_ref, q_ref, k_ref, v_ref, og_ref,
                   gi_ref, gf_ref, git_ref, gft_ref):
    xb = x_ref[...].astype(BF16)
    w = MLSTM_WIDTH
    u_ref[...] = _dot(xb, w_ref[:, 0:POOL_WIDTH])
    q = _dot(xb, w_ref[:, POOL_WIDTH:POOL_WIDTH + w]) * (MLSTM_HEAD_DIM ** -0.5)
    q_ref[...] = q.astype(q_ref.dtype)
    k_ref[...] = _dot(xb, w_ref[:, POOL_WIDTH + w:POOL_WIDTH + 2 * w]).astype(k_ref.dtype)
    v_ref[...] = _dot(xb, w_ref[:, POOL_WIDTH + 2 * w:POOL_WIDTH + 3 * w]).astype(v_ref.dtype)
    og_ref[...] = _dot(xb, w_ref[:, POOL_WIDTH + 3 * w:POOL_WIDTH + 4 * w])
    g = _dot(xb, wg_ref[...])
    gi_ref[...] = g[:, 0:LANES]
    gf_ref[...] = g[:, LANES:2 * LANES]
    gt = _dot_nt(wgt_ref[...], xb)
    git_ref[...] = gt[0:SUBLANES]
    gft_ref[...] = gt[SUBLANES:2 * SUBLANES]


def _inproj(x, w_main, wg, wgt, seq, tm, qkv_dtype):
    n, d = x.shape
    nt = n // tm
    if seq % tm == 0:
        per = seq // tm
        gt_shape, gt_map = (n // seq, SUBLANES, seq), (lambda i: (i // per, 0, i % per))
    else:
        gt_shape, gt_map = (1, SUBLANES, n), (lambda i: (0, 0, i))
    row = lambda c: pl.BlockSpec((tm, c), lambda i: (i, 0))
    full = lambda a: pl.BlockSpec(a.shape, lambda i: (0, 0))
    gt_spec = pl.BlockSpec((None, SUBLANES, tm), gt_map)
    out_shape = (
        jax.ShapeDtypeStruct((n, POOL_WIDTH), F32),
        jax.ShapeDtypeStruct((n, MLSTM_WIDTH), qkv_dtype),
        jax.ShapeDtypeStruct((n, MLSTM_WIDTH), qkv_dtype),
        jax.ShapeDtypeStruct((n, MLSTM_WIDTH), qkv_dtype),
        jax.ShapeDtypeStruct((n, MLSTM_WIDTH), F32),
        jax.ShapeDtypeStruct((n, LANES), F32),
        jax.ShapeDtypeStruct((n, LANES), F32),
        jax.ShapeDtypeStruct(gt_shape, F32),
        jax.ShapeDtypeStruct(gt_shape, F32),
    )
    return pl.pallas_call(
        _inproj_kernel, grid=(nt,),
        in_specs=[row(d), full(w_main), full(wg), full(wgt)],
        out_specs=(row(POOL_WIDTH), row(MLSTM_WIDTH), row(MLSTM_WIDTH), row(MLSTM_WIDTH),
                   row(MLSTM_WIDTH), row(LANES), row(LANES), gt_spec, gt_spec),
        out_shape=out_shape, compiler_params=_params("parallel"), name="inproj")(x, w_main, wg, wgt)


def _pool_kernel(ext_ref, pw_ref, sc_ref, o_ref, *, period):
    rows = ext_ref.shape[0]
    r = lax.broadcasted_iota(jnp.int32, (rows, POOL_GROUP_DIM), 0)
    if period != rows:
        r = lax.rem(r, period)
    for g, win in enumerate(POOL_WINDOWS):
        sl = slice(g * POOL_GROUP_DIM, (g + 1) * POOL_GROUP_DIM)
        x = ext_ref[:, sl]
        acc = x
        s = 1
        while s < win:
            acc = acc + jnp.where(r >= s, pltpu.roll(acc, s, 0), 0.0)
            s *= 2
        cnt = jnp.minimum(r + 1, win).astype(F32)
        d = acc / cnt - x
        y = _dot(d.astype(BF16), pw_ref[g]) * sc_ref[:, sl]
        o_ref[:, sl] = y.astype(o_ref.dtype)


def _pool(ext2d, pw, scale, period):
    n, c = ext2d.shape
    row = pl.BlockSpec((POOL_ROWS, c), lambda i: (i, 0))
    return pl.pallas_call(
        functools.partial(_pool_kernel, period=period), grid=(n // POOL_ROWS,),
        in_specs=[row, pl.BlockSpec(pw.shape, lambda i: (0, 0, 0)), pl.BlockSpec((1, c), lambda i: (0, 0))],
        out_specs=row, out_shape=jax.ShapeDtypeStruct((n, c), BF16),
        compiler_params=_params("parallel"), name="pool")(ext2d, pw, scale.reshape(1, c))


def _mlstm_chunk_kernel(q_ref, k_ref, v_ref, gi_ref, gf_ref, git_ref, gft_ref, bi_ref, bf_ref,
                        bit_ref, bft_ref, hn_ref, c_ref, n_ref, m_ref):
    nb, L, _ = q_ref.shape

    @pl.when(pl.program_id(0) == 0)
    def _():
        c_ref[...] = jnp.zeros_like(c_ref)
        n_ref[...] = jnp.zeros_like(n_ref)
        m_ref[...] = jnp.zeros_like(m_ref)

    causal = (lax.broadcasted_iota(jnp.int32, (L, L), 0) >= lax.broadcasted_iota(jnp.int32, (L, L), 1))

    def per_batch(b, carry):
        ig = gi_ref[b] + bi_ref[...]
        lf = _log_sigmoid(gf_ref[b] + bf_ref[...])
        bc = _scan(lf, 0, jnp.add, 0.0)
        ac = ig - bc
        m_prev = m_ref[b]
        mc = jnp.maximum(_scan(ac, 0, jnp.maximum, -jnp.inf), m_prev)
        mt = bc + mc
        w_inter = jnp.exp(m_prev - mc)
        m_last = mc[L - 1:L, :]
        w_state = jnp.exp(ac - m_last)
        decay = jnp.exp(m_prev - m_last)
        floor = jnp.exp(-mt)
        m_ref[b] = mt[L - 1:L, :]
        igt = git_ref[b] + bit_ref[...]
        lft = _log_sigmoid(gft_ref[b] + bft_ref[...])
        at = igt - _scan(lft, 1, jnp.add, 0.0)
        for h in range(MLSTM_HEADS):
            sl = slice(h * MLSTM_HEAD_DIM, (h + 1) * MLSTM_HEAD_DIM)
            col = slice(h, h + 1)
            qh = q_ref[b, :, sl]
            kh = k_ref[b, :, sl]
            vh = v_ref[b, :, sl]
            s = _dot_nt(qh, kh)
            p = jnp.where(causal, s * jnp.exp(at[col, :] - mc[:, col]), 0.0)
            c_old = c_ref[b, h]
            n_old = n_ref[b, col, :]
            wi = w_inter[:, col]
            num = _dot(p.astype(BF16), vh) + wi * _dot_nt(qh, c_old.astype(BF16))
            qn = jnp.sum(p, -1, keepdims=True) + wi * jnp.sum(qh.astype(F32) * n_old, -1, keepdims=True)
            hh = num / jnp.maximum(jnp.abs(qn), floor[:, col])
            hn_ref[b, :, sl] = _unit_norm(hh)
            ws = w_state[:, col]
            dc = decay[:, col]
            c_ref[b, h] = dc * c_old + _dot_tn((vh.astype(F32) * ws).astype(BF16), kh)
            n_ref[b, col, :] = dc * n_old + jnp.sum(kh.astype(F32) * ws, 0, keepdims=True)
        return carry

    lax.fori_loop(0, nb, per_batch, 0)


def _mlstm_chunk(q, k, v, gi, gf, git, gft, bi, bf, bit, bft):
    nb, t, w = q.shape
    L = MLSTM_CHUNK
    seq = lambda c: pl.BlockSpec((nb, L, c), lambda i: (0, i, 0))
    seqt = pl.BlockSpec((nb, SUBLANES, L), lambda i: (0, 0, i))
    vec = lambda a: pl.BlockSpec(a.shape, lambda i: (0, 0))
    out_shape = (
        jax.ShapeDtypeStruct((nb, t, w), F32),
        jax.ShapeDtypeStruct((nb, MLSTM_HEADS, MLSTM_HEAD_DIM, MLSTM_HEAD_DIM), F32),
        jax.ShapeDtypeStruct((nb, MLSTM_HEADS, MLSTM_HEAD_DIM), F32),
        jax.ShapeDtypeStruct((nb, 1, LANES), F32),
    )
    out_specs = (
        seq(w),
        pl.BlockSpec(out_shape[1].shape, lambda i: (0, 0, 0, 0)),
        pl.BlockSpec(out_shape[2].shape, lambda i: (0, 0, 0)),
        pl.BlockSpec(out_shape[3].shape, lambda i: (0, 0, 0)),
    )
    return pl.pallas_call(
        _mlstm_chunk_kernel, grid=(t // L,),
        in_specs=[seq(w), seq(w), seq(w), seq(LANES), seq(LANES), seqt, seqt,
                  vec(bi), vec(bf), vec(bit), vec(bft)],
        out_specs=out_specs, out_shape=out_shape, compiler_params=_params("arbitrary"),
        name="mlstm_chunk")(q, k, v, gi, gf, git, gft, bi, bf, bit, bft)


def _mlstm_step_kernel(q_ref, k_ref, v_ref, gi_ref, gf_ref, bi_ref, bf_ref, c_ref, n_ref, m_ref,
                       hn_ref, co_ref, no_ref, mo_ref):
    tb = q_ref.shape[0]
    d = MLSTM_HEAD_DIM
    ig = gi_ref[...] + bi_ref[...]
    lf = _log_sigmoid(gf_ref[...] + bf_ref[...])
    m_old = m_ref[...]
    mt = jnp.maximum(lf + m_old, ig)
    mo_ref[...] = mt
    wa = jnp.exp(ig - mt)
    wi = jnp.exp(lf + m_old - mt)
    fl = jnp.exp(-mt)
    wa_t, wi_t, fl_t = wa.T, wi.T, fl.T
    lane = lax.broadcasted_iota(jnp.int32, (d, tb), 1)
    for h in range(MLSTM_HEADS):
        sl = slice(h * d, (h + 1) * d)
        qh, kh, vh = q_ref[:, sl], k_ref[:, sl], v_ref[:, sl]
        nh = n_ref[:, h, :]
        qt, kt, vt, nt = qh.T, kh.T, vh.T, nh.T
        wa_r, wi_r, fl_r = wa_t[h:h + 1, :], wi_t[h:h + 1, :], fl_t[h:h + 1, :]
        s = jnp.sum(qt * kt, 0, keepdims=True) * wa_r
        cq = jnp.zeros((d, tb), F32)
        for b in range(tb):
            cq = jnp.where(lane == b, jnp.sum(c_ref[b, h] * qh[b:b + 1, :], -1, keepdims=True), cq)
        num = s * vt + wi_r * cq
        qn = s + wi_r * jnp.sum(nt * qt, 0, keepdims=True)
        hh = num / jnp.maximum(jnp.abs(qn), fl_r)
        mu = jnp.mean(hh, 0, keepdims=True)
        xc = hh - mu
        var = jnp.mean(xc * xc, 0, keepdims=True)
        hn_ref[:, sl] = (xc * lax.rsqrt(var + LN_EPS)).T
        wav = wa_r * vt
        for b in range(tb):
            co_ref[b, h] = wi_r[:, b:b + 1] * c_ref[b, h] + wav[:, b:b + 1] * kh[b:b + 1, :]
        no_ref[:, h, :] = wi[:, h:h + 1] * nh + wa[:, h:h + 1] * kh


def _mlstm_step(q, k, v, gi, gf, bi, bf, c_all, n_all, m0, layer, tb):
    n, w = q.shape
    d = MLSTM_HEAD_DIM
    row = lambda c: pl.BlockSpec((tb, c), lambda i: (i, 0))
    vec = lambda a: pl.BlockSpec(a.shape, lambda i: (0, 0))
    cin = pl.BlockSpec((None, tb, MLSTM_HEADS, d, d), lambda i: (layer, i, 0, 0, 0))
    nin = pl.BlockSpec((None, tb, MLSTM_HEADS, d), lambda i: (layer, i, 0, 0))
    cout = pl.BlockSpec((tb, MLSTM_HEADS, d, d), lambda i: (i, 0, 0, 0))
    nout = pl.BlockSpec((tb, MLSTM_HEADS, d), lambda i: (i, 0, 0))
    out_shape = (
        jax.ShapeDtypeStruct((n, w), F32),
        jax.ShapeDtypeStruct(c_all.shape[1:], F32),
        jax.ShapeDtypeStruct(n_all.shape[1:], F32),
        jax.ShapeDtypeStruct((n, LANES), F32),
    )
    return pl.pallas_call(
        _mlstm_step_kernel, grid=(n // tb,),
        in_specs=[row(w), row(w), row(w), row(LANES), row(LANES), vec(bi), vec(bf), cin, nin, row(LANES)],
        out_specs=(row(w), cout, nout, row(LANES)), out_shape=out_shape,
        compiler_params=_params("parallel"), name="mlstm_step")(q, k, v, gi, gf, bi, bf, c_all, n_all, m0)


def _mixout_kernel(po_ref, hn_ref, og_ref, x_ref, ng_ref, w_ref, g_ref, b_ref, o_ref):
    mo = jax.nn.sigmoid(og_ref[...]) * hn_ref[...] * ng_ref[...]
    mix = _dot(po_ref[...], w_ref[0:POOL_WIDTH, :]) + _dot(mo.astype(BF16), w_ref[POOL_WIDTH:, :])
    o_ref[...] = _layer_norm(ALPHA * x_ref[...] + mix, g_ref[...], b_ref[...])


def _mixout(po, hn, og, x, ng, w, g, b, tm):
    n, d = x.shape
    row = lambda c: pl.BlockSpec((tm, c), lambda i: (i, 0))
    vec = lambda a: pl.BlockSpec(a.shape, lambda i: (0, 0))
    return pl.pallas_call(
        _mixout_kernel, grid=(n // tm,),
        in_specs=[row(POOL_WIDTH), row(MLSTM_WIDTH), row(MLSTM_WIDTH), row(d), vec(ng), vec(w), vec(g), vec(b)],
        out_specs=row(d), out_shape=jax.ShapeDtypeStruct((n, d), F32),
        compiler_params=_params("parallel"), name="mixout")(po, hn, og, x, ng, w, g, b)


def _mm_kernel(x_ref, w_ref, o_ref, *, scale):
    y = _dot(x_ref[...].astype(BF16), w_ref[...])
    if scale != 1.0:
        y = y * scale
    o_ref[...] = y.astype(o_ref.dtype)


def _mm(x, w, tm, out_dtype, scale=1.0, name="proj"):
    n, d = x.shape
    dout = w.shape[1]
    return pl.pallas_call(
        functools.partial(_mm_kernel, scale=scale), grid=(n // tm,),
        in_specs=[pl.BlockSpec((tm, d), lambda i: (i, 0)), pl.BlockSpec(w.shape, lambda i: (0, 0))],
        out_specs=pl.BlockSpec((tm, dout), lambda i: (i, 0)),
        out_shape=jax.ShapeDtypeStruct((n, dout), out_dtype),
        compiler_params=_params("parallel"), name=name)(x, w)


def _attn_kernel(q_ref, k_ref, v_ref, o_ref):
    qb = q_ref[...]
    kb = k_ref[...].astype(BF16)
    vb = v_ref[...].astype(BF16)
    for h in range(CA_HEADS):
        sl = slice(h * CA_HEAD_DIM, (h + 1) * CA_HEAD_DIM)
        s = _dot_nt(qb[:, sl], kb[:, sl])
        e = jnp.exp(s - jnp.max(s, -1, keepdims=True))
        ctx = _dot(e.astype(BF16), vb[:, sl]) / jnp.sum(e, -1, keepdims=True)
        o_ref[:, sl] = ctx.astype(o_ref.dtype)


def _attn(q, k, v, tq):
    nb, t, d = q.shape
    qspec = pl.BlockSpec((None, tq, d), lambda b, i: (b, i, 0))
    kspec = pl.BlockSpec((None, N_MEM, d), lambda b, i: (b, 0, 0))
    return pl.pallas_call(
        _attn_kernel, grid=(nb, t // tq), in_specs=[qspec, kspec, kspec], out_specs=qspec,
        out_shape=jax.ShapeDtypeStruct((nb, t, d), q.dtype),
        compiler_params=_params("parallel", "parallel"), name="attn")(q, k, v)


def _attn_step_kernel(q_ref, k_ref, v_ref, o_ref):
    bb = q_ref.shape[0]
    rows = N_MEM * CA_HEADS
    lane = lax.broadcasted_iota(jnp.int32, (SUBLANES, rows), 1)
    row = lax.broadcasted_iota(jnp.int32, (SUBLANES, rows), 0)
    own = lax.rem(lane, CA_HEADS) == lax.rem(row, CA_HEADS)
    for j in range(bb):
        kf = k_ref[j].reshape(rows, CA_HEAD_DIM).astype(BF16)
        vf = v_ref[j].reshape(rows, CA_HEAD_DIM).astype(BF16)
        s = jnp.where(own, _dot_nt(q_ref[j].astype(BF16), kf), -jnp.inf)
        e = jnp.exp(s - jnp.max(s, -1, keepdims=True))
        o_ref[j] = _dot(e.astype(BF16), vf) / jnp.sum(e, -1, keepdims=True)


def _attn_step(q, k_all, v_all, layer, bb):
    n = q.shape[0]
    qspec = pl.BlockSpec((bb, SUBLANES, CA_HEAD_DIM), lambda i: (i, 0, 0))
    kspec = pl.BlockSpec((None, bb, N_MEM, CA_HEADS, CA_HEAD_DIM), lambda i: (layer, i, 0, 0, 0))
    return pl.pallas_call(
        _attn_step_kernel, grid=(n // bb,), in_specs=[qspec, kspec, kspec], out_specs=qspec,
        out_shape=jax.ShapeDtypeStruct((n, SUBLANES, CA_HEAD_DIM), F32),
        compiler_params=_params("parallel"), name="attn_step")(q, k_all, v_all)


def _mm_res_ln_kernel(a_ref, x_ref, w_ref, g_ref, b_ref, o_ref):
    y = _dot(a_ref[...].astype(BF16), w_ref[...])
    o_ref[...] = _layer_norm(ALPHA * x_ref[...] + y, g_ref[...], b_ref[...])


def _mm_res_ln(a, x, w, g, b, tm):
    n, d = x.shape
    row = lambda c: pl.BlockSpec((tm, c), lambda i: (i, 0))
    vec = lambda arr: pl.BlockSpec(arr.shape, lambda i: (0, 0))
    return pl.pallas_call(
        _mm_res_ln_kernel, grid=(n // tm,),
        in_specs=[row(a.shape[1]), row(d), vec(w), vec(g), vec(b)], out_specs=row(d),
        out_shape=jax.ShapeDtypeStruct((n, d), F32),
        compiler_params=_params("parallel"), name="proj_res_ln")(a, x, w, g, b)


def _route(lt):
    gl = [lt[g:g + 1, :] for g in range(N_GROUPS)]
    gmax = functools.reduce(jnp.maximum, gl)
    gsum = functools.reduce(jnp.add, [jnp.exp(x - gmax) for x in gl])
    pg_sel = 1.0 / gsum
    taken = jnp.zeros_like(gmax, dtype=jnp.bool_)
    g_hot = []
    for x in gl:
        hot = jnp.logical_and(x == gmax, jnp.logical_not(taken))
        taken = jnp.logical_or(taken, hot)
        g_hot.append(hot)
    el = []
    for j in range(EXPERTS_PER_GROUP):
        rows = [lt[SUBLANES + g * EXPERTS_PER_GROUP + j:SUBLANES + g * EXPERTS_PER_GROUP + j + 1, :]
                for g in range(N_GROUPS)]
        x = rows[N_GROUPS - 1]
        for g in range(N_GROUPS - 2, -1, -1):
            x = jnp.where(g_hot[g], rows[g], x)
        el.append(x)
    emax = functools.reduce(jnp.maximum, el)
    ee = [jnp.exp(x - emax) for x in el]
    esum = functools.reduce(jnp.add, ee)
    pe = [x / esum for x in ee]
    p1 = functools.reduce(jnp.maximum, pe)
    taken = jnp.zeros_like(gmax, dtype=jnp.bool_)
    hot1 = []
    for x in pe:
        hot = jnp.logical_and(x == p1, jnp.logical_not(taken))
        taken = jnp.logical_or(taken, hot)
        hot1.append(hot)
    rest = [jnp.where(h1, -jnp.inf, x) for h1, x in zip(hot1, pe)]
    p2 = functools.reduce(jnp.maximum, rest)
    taken = jnp.zeros_like(gmax, dtype=jnp.bool_)
    hot2 = []
    for x in rest:
        hot = jnp.logical_and(x == p2, jnp.logical_not(taken))
        taken = jnp.logical_or(taken, hot)
        hot2.append(hot)
    psum = p1 + p2
    gate = [jnp.where(h1, pg_sel * p1 / psum, jnp.where(h2, pg_sel * p2 / psum, 0.0))
            for h1, h2 in zip(hot1, hot2)]
    return [jnp.where(g_hot[g], gate[j], 0.0) for g in range(N_GROUPS) for j in range(EXPERTS_PER_GROUP)]


def _moe_kernel(x_ref, wr_ref, br_ref, wg_ref, wu_ref, wd_ref, g_ref, b_ref, o_ref,
                xb_ref, ct_ref, comb_ref, acc_ref):
    e = pl.program_id(1)

    @pl.when(e == 0)
    def _():
        x = x_ref[...]
        xb_ref[...] = x.astype(BF16)
        lt = _dot_nt(wr_ref[...], x, precision=lax.Precision.HIGHEST) + br_ref[...]
        ct_ref[...] = jnp.zeros_like(ct_ref)
        for i, row in enumerate(_route(lt)):
            ct_ref[i:i + 1, :] = row
        comb_ref[...] = ct_ref[...].T
        acc_ref[...] = jnp.zeros_like(acc_ref)

    xb = xb_ref[...]
    lane = lax.broadcasted_iota(jnp.int32, comb_ref.shape, 1)
    c = jnp.sum(jnp.where(lane == e, comb_ref[...], 0.0), -1, keepdims=True)
    hg = _dot(xb, wg_ref[...].astype(BF16))
    hu = _dot(xb, wu_ref[...].astype(BF16))
    hh = hg * jax.nn.sigmoid(hg) * hu * c
    acc_ref[...] += _dot(hh.astype(BF16), wd_ref[...].astype(BF16))

    @pl.when(e == N_EXPERTS - 1)
    def _():
        o_ref[...] = _layer_norm(ALPHA * x_ref[...] + acc_ref[...], g_ref[...], b_ref[...])


def _moe(x, wr, br, wg, wu, wd, layer, g, b, tm):
    n, d = x.shape
    row = pl.BlockSpec((tm, d), lambda i, e: (i, 0))
    vec = lambda a: pl.BlockSpec(a.shape, lambda i, e: (0, 0))
    scratch = [pltpu.VMEM((tm, d), BF16), pltpu.VMEM((LANES, tm), F32), pltpu.VMEM((tm, LANES), F32),
               pltpu.VMEM((tm, d), F32)]
    return pl.pallas_call(
        _moe_kernel, grid=(n // tm, N_EXPERTS),
        in_specs=[row, vec(wr), vec(br),
                  pl.BlockSpec((None, None, d, D_EXPERT), lambda i, e: (layer, e, 0, 0)),
                  pl.BlockSpec((None, None, d, D_EXPERT), lambda i, e: (layer, e, 0, 0)),
                  pl.BlockSpec((None, None, D_EXPERT, d), lambda i, e: (layer, e, 0, 0)),
                  vec(g), vec(b)],
        out_specs=row, out_shape=jax.ShapeDtypeStruct((n, d), F32), scratch_shapes=scratch,
        compiler_params=_params("parallel", "arbitrary"), name="moe")(x, wr, br, wg, wu, wd, g, b)


def _pad_lanes(v):
    return jnp.zeros((1, LANES), F32).at[0, :v.shape[0]].set(v)


def _pad_rows(v):
    return jnp.zeros((SUBLANES, 1), F32).at[:v.shape[0], 0].set(v)


def _layer_weights(l, w_in, b_i, b_f, pool_w, pool_scale, mlstm_norm_g, w_out, ln1_g, ln1_b, ca_wq, ca_wo,
                   ln2_g, ln2_b, w_gr, b_gr, w_er, b_er, w_gate, w_up, w_down, ln3_g, ln3_b):
    d = D_MODEL
    wi = w_in[l]
    w_gate_cols = wi[:, GATE_OFF:]
    wg = jnp.zeros((d, 2 * LANES), F32)
    wg = wg.at[:, 0:MLSTM_HEADS].set(w_gate_cols[:, :MLSTM_HEADS])
    wg = wg.at[:, LANES:LANES + MLSTM_HEADS].set(w_gate_cols[:, MLSTM_HEADS:])
    wgt = jnp.zeros((2 * SUBLANES, d), F32)
    wgt = wgt.at[0:MLSTM_HEADS].set(w_gate_cols[:, :MLSTM_HEADS].T)
    wgt = wgt.at[SUBLANES:SUBLANES + MLSTM_HEADS].set(w_gate_cols[:, MLSTM_HEADS:].T)
    wr = jnp.zeros((ROUTER_ROWS, d), F32)
    wr = wr.at[0:N_GROUPS].set(w_gr[l].T).at[SUBLANES:SUBLANES + N_EXPERTS].set(w_er[l].T)
    br = jnp.zeros((ROUTER_ROWS, 1), F32)
    br = br.at[0:N_GROUPS, 0].set(b_gr[l]).at[SUBLANES:SUBLANES + N_EXPERTS, 0].set(b_er[l])
    row = lambda v: v.reshape(1, -1)
    return dict(
        w_main=wi[:, :GATE_OFF].astype(BF16), wg=wg.astype(BF16), wgt=wgt.astype(BF16),
        bi=_pad_lanes(b_i[l]), bf=_pad_lanes(b_f[l]), bit=_pad_rows(b_i[l]), bft=_pad_rows(b_f[l]),
        pool_w=pool_w[l].astype(BF16), pool_scale=pool_scale[l], norm_g=row(mlstm_norm_g[l]),
        w_out=w_out[l].astype(BF16), ln1=(row(ln1_g[l]), row(ln1_b[l])),
        wq=ca_wq[l].astype(BF16), wo=ca_wo[l].astype(BF16), ln2=(row(ln2_g[l]), row(ln2_b[l])),
        wr=wr, br=br, experts=(w_gate, w_up, w_down), ln3=(row(ln3_g[l]), row(ln3_b[l])))


def _trunk_layer(x, nb, seq, layer, mem, pool_prev, state, p, tm, tm_moe):
    chunked = state is None
    u, q, k, v, og, gi, gf, git, gft = _inproj(x, p["w_main"], p["wg"], p["wgt"], seq, tm,
                                               BF16 if chunked else F32)
    u3 = u.reshape(nb, seq, POOL_WIDTH)
    ext = u3 if pool_prev is None else jnp.concatenate([pool_prev, u3], axis=1)
    period = ext.shape[1]
    pooled = _pool(ext.reshape(nb * period, POOL_WIDTH), p["pool_w"], p["pool_scale"], period)
    pooled = pooled.reshape(nb, period, POOL_WIDTH)[:, period - seq:].reshape(nb * seq, POOL_WIDTH)
    pool_buf = ext[:, -POOL_BUF:]
    if chunked:
        r3 = lambda a: a.reshape(nb, seq, a.shape[-1])
        hn, c1, n1, m1 = _mlstm_chunk(r3(q), r3(k), r3(v), r3(gi), r3(gf), git, gft,
                                      p["bi"], p["bf"], p["bit"], p["bft"])
        hn = hn.reshape(nb * seq, MLSTM_WIDTH)
        m1 = m1[:, 0, :MLSTM_HEADS]
    else:
        c_all, n_all, m0 = state
        m0p = jnp.zeros((nb, LANES), F32).at[:, :MLSTM_HEADS].set(m0)
        hn, c1, n1, m1 = _mlstm_step(q, k, v, gi, gf, p["bi"], p["bf"], c_all, n_all, m0p, layer, SUBLANES)
        m1 = m1[:, :MLSTM_HEADS]
    x = _mixout(pooled, hn, og, x, p["norm_g"], p["w_out"], *p["ln1"], tm)
    qc = _mm(x, p["wq"], tm, BF16 if chunked else F32, scale=CA_HEAD_DIM ** -0.5, name="ca_q")
    if chunked:
        ctx = _attn(qc.reshape(nb, seq, D_MODEL), mem[0], mem[1], tm).reshape(nb * seq, D_MODEL)
    else:
        qh = jnp.zeros((nb, SUBLANES, CA_HEAD_DIM), F32).at[:, :CA_HEADS].set(qc.reshape(nb, CA_HEADS, CA_HEAD_DIM))
        ctx = _attn_step(qh, mem[0], mem[1], layer, 4)[:, :CA_HEADS].reshape(nb, D_MODEL)
    x = _mm_res_ln(ctx, x, p["wo"], *p["ln2"], tm)
    x = _moe(x, p["wr"], p["br"], *p["experts"], layer, *p["ln3"], tm_moe)
    return x, pool_buf, c1, n1, m1


def kernel(x_prompt, x_sample, mem_prompt, cache_pool, state_mlstm_C, state_mlstm_n, state_mlstm_m,
           cache_mem_k, cache_mem_v, emb_ln_g, emb_ln_b, w_in, b_i, b_f, pool_w, pool_scale,
           mlstm_norm_g, w_out, ln1_g, ln1_b, ca_wq, ca_wk, ca_wv, ca_wo, ln2_g, ln2_b,
           w_gr, b_gr, w_er, b_er, w_gate, w_up, w_down, ln3_g, ln3_b):
    bp, tp, d = x_prompt.shape
    bs, ts, _ = x_sample.shape
    tm_p, tm_s = 512, bs * ts
    xp = _ln(x_prompt.reshape(bp * tp, d), emb_ln_g, emb_ln_b, tm_p)
    xs = _ln(x_sample.reshape(bs * ts, d), emb_ln_g, emb_ln_b, tm_s)
    mem2d = mem_prompt.reshape(bp * N_MEM, d)
    outs = [[] for _ in range(10)]
    for l in range(DEPTH):
        p = _layer_weights(l, w_in, b_i, b_f, pool_w, pool_scale, mlstm_norm_g, w_out, ln1_g, ln1_b, ca_wq,
                           ca_wo, ln2_g, ln2_b, w_gr, b_gr, w_er, b_er, w_gate, w_up, w_down, ln3_g, ln3_b)
        mk = _mm(mem2d, ca_wk[l].astype(BF16), tm_p, F32, name="mem_k").reshape(bp, N_MEM, d)
        mv = _mm(mem2d, ca_wv[l].astype(BF16), tm_p, F32, name="mem_v").reshape(bp, N_MEM, d)
        xp, pb, c1, n1, m1 = _trunk_layer(xp, bp, tp, l, (mk, mv), None, None, p, tm_p, 2 * tm_p)
        xs, sb, cs, ns, ms = _trunk_layer(xs, bs, ts, l, (cache_mem_k, cache_mem_v), cache_pool[l],
                                          (state_mlstm_C, state_mlstm_n, state_mlstm_m[l]), p, tm_s, tm_s)
        for lst, val in zip(outs, (pb, c1, n1, m1, mk.reshape(bp, N_MEM, CA_HEADS, CA_HEAD_DIM),
                                   mv.reshape(bp, N_MEM, CA_HEADS, CA_HEAD_DIM), sb, cs, ns, ms)):
            lst.append(val)
    return (xp.reshape(bp, tp, d), xs.reshape(bs, ts, d)) + tuple(jnp.stack(o) for o in outs)
```

```python
import functools

import jax
import jax.numpy as jnp
from jax import lax
from jax.experimental import pallas as pl
from jax.experimental.pallas import tpu as pltpu

F32 = jnp.float32
BF16 = jnp.bfloat16

D_MODEL = 1024
DEPTH = 4
POOL_WIDTH = 512
POOL_GROUPS = 4
POOL_GROUP_DIM = 128
POOL_WINDOWS = (2, 4, 8, 16)
POOL_BUF = 15
MLSTM_WIDTH = 512
MLSTM_HEADS = 4
MLSTM_HEAD_DIM = 128
N_MEM = 256
CA_HEADS = 4
CA_HEAD_DIM = 256
N_GROUPS = 4
EXPERTS_PER_GROUP = 4
N_EXPERTS = 16
D_EXPERT = 256
ALPHA = (2 * DEPTH) ** 0.25
LN_EPS = 1e-5
GATE_OFF = POOL_WIDTH + 4 * MLSTM_WIDTH

LANES = 128
SUBLANES = 8
VMEM_LIMIT = 56 * 1024 * 1024
MLSTM_CHUNK = 256
POOL_ROWS = 2048
ROUTER_ROWS = 32


def _params(*sem):
    return pltpu.CompilerParams(dimension_semantics=sem, vmem_limit_bytes=VMEM_LIMIT)


def _dot(a, b):
    return jnp.dot(a, b, preferred_element_type=F32)


def _dot_nt(a, b, precision=None):
    return lax.dot_general(a, b, (((1,), (1,)), ((), ())), precision=precision,
                           preferred_element_type=F32)


def _dot_tn(a, b):
    return lax.dot_general(a, b, (((0,), (0,)), ((), ())), preferred_element_type=F32)


def _layer_norm(x, g, b):
    mu = jnp.mean(x, -1, keepdims=True)
    xc = x - mu
    var = jnp.mean(xc * xc, -1, keepdims=True)
    return xc * lax.rsqrt(var + LN_EPS) * g + b


def _unit_norm(x):
    mu = jnp.mean(x, -1, keepdims=True)
    xc = x - mu
    var = jnp.mean(xc * xc, -1, keepdims=True)
    return xc * lax.rsqrt(var + LN_EPS)


def _log_sigmoid(x):
    return jnp.minimum(x, 0.0) - jnp.log1p(jnp.exp(-jnp.abs(x)))


def _scan(x, axis, op, fill):
    n = x.shape[axis]
    idx = lax.broadcasted_iota(jnp.int32, x.shape, axis)
    s = 1
    while s < n:
        x = op(x, jnp.where(idx >= s, pltpu.roll(x, s, axis), fill))
        s *= 2
    return x


def _ln_kernel(x_ref, g_ref, b_ref, o_ref):
    o_ref[...] = _layer_norm(x_ref[...], g_ref[...], b_ref[...])


def _ln(x, g, b, tm):
    n, d = x.shape
    row = pl.BlockSpec((tm, d), lambda i: (i, 0))
    vec = pl.BlockSpec((1, d), lambda i: (0, 0))
    return pl.pallas_call(
        _ln_kernel, grid=(n // tm,), in_specs=[row, vec, vec], out_specs=row,
        out_shape=jax.ShapeDtypeStruct((n, d), F32), compiler_params=_params("parallel"),
        name="emb_ln")(x, g.reshape(1, d), b.reshape(1, d))


def _inproj_kernel(x_ref, w_ref, wg_ref, wgt_ref, u_ref, q_ref, k_ref, v_ref, og_ref,
                   gi_ref, gf_ref, git_ref, gft_ref):
    xb = x_ref[...].astype(BF16)
    w = MLSTM_WIDTH
    u_ref[...] = _dot(xb, w_ref[:, 0:POOL_WIDTH])
    q = _dot(xb, w_ref[:, POOL_WIDTH:POOL_WIDTH + w]) * (MLSTM_HEAD_DIM ** -0.5)
    q_ref[...] = q.astype(q_ref.dtype)
    k_ref[...] = _dot(xb, w_ref[:, POOL_WIDTH + w:POOL_WIDTH + 2 * w]).astype(k_ref.dtype)
    v_ref[...] = _dot(xb, w_ref[:, POOL_WIDTH + 2 * w:POOL_WIDTH + 3 * w]).astype(v_ref.dtype)
    og_ref[...] = _dot(xb, w_ref[:, POOL_WIDTH + 3 * w:POOL_WIDTH + 4 * w])
    g = _dot(xb, wg_ref[...])
    gi_ref[...] = g[:, 0:LANES]
    gf_ref[...] = g[:, LANES:2 * LANES]
    gt = _dot_nt(wgt_ref[...], xb)
    git_ref[...] = gt[0:SUBLANES]
    gft_ref[...] = gt[SUBLANES:2 * SUBLANES]


def _inproj(x, w_main, wg, wgt, seq, tm, qkv_dtype):
    n, d = x.shape
    nt = n // tm
    if seq % tm == 0:
        per = seq // tm
        gt_shape, gt_map = (n // seq, SUBLANES, seq), (lambda i: (i // per, 0, i % per))
    else:
        gt_shape, gt_map = (1, SUBLANES, n), (lambda i: (0, 0, i))
    row = lambda c: pl.BlockSpec((tm, c), lambda i: (i, 0))
    full = lambda a: pl.BlockSpec(a.shape, lambda i: (0, 0))
    gt_spec = pl.BlockSpec((None, SUBLANES, tm), gt_map)
    out_shape = (
        jax.ShapeDtypeStruct((n, POOL_WIDTH), F32),
        jax.ShapeDtypeStruct((n, MLSTM_WIDTH), qkv_dtype),
        jax.ShapeDtypeStruct((n, MLSTM_WIDTH), qkv_dtype),
        jax.ShapeDtypeStruct((n, MLSTM_WIDTH), qkv_dtype),
        jax.ShapeDtypeStruct((n, MLSTM_WIDTH), F32),
        jax.ShapeDtypeStruct((n, LANES), F32),
        jax.ShapeDtypeStruct((n, LANES), F32),
        jax.ShapeDtypeStruct(gt_shape, F32),
        jax.ShapeDtypeStruct(gt_shape, F32),
    )
    return pl.pallas_call(
        _inproj_kernel, grid=(nt,),
        in_specs=[row(d), full(w_main), full(wg), full(wgt)],
        out_specs=(row(POOL_WIDTH), row(MLSTM_WIDTH), row(MLSTM_WIDTH), row(MLSTM_WIDTH),
                   row(MLSTM_WIDTH), row(LANES), row(LANES), gt_spec, gt_spec),
        out_shape=out_shape, compiler_params=_params("parallel"), name="inproj")(x, w_main, wg, wgt)


def _pool_kernel(ext_ref, pw_ref, sc_ref, o_ref, *, period):
    rows = ext_ref.shape[0]
    r = lax.broadcasted_iota(jnp.int32, (rows, POOL_GROUP_DIM), 0)
    if period != rows:
        r = lax.rem(r, period)
    for g, win in enumerate(POOL_WINDOWS):
        sl = slice(g * POOL_GROUP_DIM, (g + 1) * POOL_GROUP_DIM)
        x = ext_ref[:, sl]
        acc = x
        s = 1
        while s < win:
            acc = acc + jnp.where(r >= s, pltpu.roll(acc, s, 0), 0.0)
            s *= 2
        cnt = jnp.minimum(r + 1, win).astype(F32)
        d = acc / cnt - x
        y = _dot(d.astype(BF16), pw_ref[g]) * sc_ref[:, sl]
        o_ref[:, sl] = y.astype(o_ref.dtype)


def _pool(ext2d, pw, scale, period):
    n, c = ext2d.shape
    row = pl.BlockSpec((POOL_ROWS, c), lambda i: (i, 0))
    return pl.pallas_call(
        functools.partial(_pool_kernel, period=period), grid=(n // POOL_ROWS,),
        in_specs=[row, pl.BlockSpec(pw.shape, lambda i: (0, 0, 0)), pl.BlockSpec((1, c), lambda i: (0, 0))],
        out_specs=row, out_shape=jax.ShapeDtypeStruct((n, c), BF16),
        compiler_params=_params("parallel"), name="pool")(ext2d, pw, scale.reshape(1, c))


def _mlstm_chunk_kernel(q_ref, k_ref, v_ref, gi_ref, gf_ref, git_ref, gft_ref, bi_ref, bf_ref,
                        bit_ref, bft_ref, hn_ref, c_ref, n_ref, m_ref):
    nb, L, _ = q_ref.shape

    @pl.when(pl.program_id(0) == 0)
    def _():
        c_ref[...] = jnp.zeros_like(c_ref)
        n_ref[...] = jnp.zeros_like(n_ref)
        m_ref[...] = jnp.zeros_like(m_ref)

    causal = (lax.broadcasted_iota(jnp.int32, (L, L), 0) >= lax.broadcasted_iota(jnp.int32, (L, L), 1))

    def per_batch(b, carry):
        ig = gi_ref[b] + bi_ref[...]
        lf = _log_sigmoid(gf_ref[b] + bf_ref[...])
        bc = _scan(lf, 0, jnp.add, 0.0)
        ac = ig - bc
        m_prev = m_ref[b]
        mc = jnp.maximum(_scan(ac, 0, jnp.maximum, -jnp.inf), m_prev)
        mt = bc + mc
        w_inter = jnp.exp(m_prev - mc)
        m_last = mc[L - 1:L, :]
        w_state = jnp.exp(ac - m_last)
        decay = jnp.exp(m_prev - m_last)
        floor = jnp.exp(-mt)
        m_ref[b] = mt[L - 1:L, :]
        igt = git_ref[b] + bit_ref[...]
        lft = _log_sigmoid(gft_ref[b] + bft_ref[...])
        at = igt - _scan(lft, 1, jnp.add, 0.0)
        for h in range(MLSTM_HEADS):
            sl = slice(h * MLSTM_HEAD_DIM, (h + 1) * MLSTM_HEAD_DIM)
            col = slice(h, h + 1)
            qh = q_ref[b, :, sl]
            kh = k_ref[b, :, sl]
            vh = v_ref[b, :, sl]
            s = _dot_nt(qh, kh)
            p = jnp.where(causal, s * jnp.exp(at[col, :] - mc[:, col]), 0.0)
            c_old = c_ref[b, h]
            n_old = n_ref[b, col, :]
            wi = w_inter[:, col]
            num = _dot(p.astype(BF16), vh) + wi * _dot_nt(qh, c_old.astype(BF16))
            qn = jnp.sum(p, -1, keepdims=True) + wi * jnp.sum(qh.astype(F32) * n_old, -1, keepdims=True)
            hh = num / jnp.maximum(jnp.abs(qn), floor[:, col])
            hn_ref[b, :, sl] = _unit_norm(hh)
            ws = w_state[:, col]
            dc = decay[:, col]
            c_ref[b, h] = dc * c_old + _dot_tn((vh.astype(F32) * ws).astype(BF16), kh)
            n_ref[b, col, :] = dc * n_old + jnp.sum(kh.astype(F32) * ws, 0, keepdims=True)
        return carry

    lax.fori_loop(0, nb, per_batch, 0)


def _mlstm_chunk(q, k, v, gi, gf, git, gft, bi, bf, bit, bft):
    nb, t, w = q.shape
    L = MLSTM_CHUNK
    seq = lambda c: pl.BlockSpec((nb, L, c), lambda i: (0, i, 0))
    seqt = pl.BlockSpec((nb, SUBLANES, L), lambda i: (0, 0, i))
    vec = lambda a: pl.BlockSpec(a.shape, lambda i: (0, 0))
    out_shape = (
        jax.ShapeDtypeStruct((nb, t, w), F32),
        jax.ShapeDtypeStruct((nb, MLSTM_HEADS, MLSTM_HEAD_DIM, MLSTM_HEAD_DIM), F32),
        jax.ShapeDtypeStruct((nb, MLSTM_HEADS, MLSTM_HEAD_DIM), F32),
        jax.ShapeDtypeStruct((nb, 1, LANES), F32),
    )
    out_specs = (
        seq(w),
        pl.BlockSpec(out_shape[1].shape, lambda i: (0, 0, 0, 0)),
        pl.BlockSpec(out_shape[2].shape, lambda i: (0, 0, 0)),
        pl.BlockSpec(out_shape[3].shape, lambda i: (0, 0, 0)),
    )
    return pl.pallas_call(
        _mlstm_chunk_kernel, grid=(t // L,),
        in_specs=[seq(w), seq(w), seq(w), seq(LANES), seq(LANES), seqt, seqt,
                  vec(bi), vec(bf), vec(bit), vec(bft)],
        out_specs=out_specs, out_shape=out_shape, compiler_params=_params("arbitrary"),
        name="mlstm_chunk")(q, k, v, gi, gf, git, gft, bi, bf, bit, bft)


def _mlstm_step_kernel(q_ref, k_ref, v_ref, gi_ref, gf_ref, bi_ref, bf_ref, c_ref, n_ref, m_ref,
                       hn_ref, co_ref, no_ref, mo_ref):
    tb = q_ref.shape[0]
    d = MLSTM_HEAD_DIM
    ig = gi_ref[...] + bi_ref[...]
    lf = _log_sigmoid(gf_ref[...] + bf_ref[...])
    m_old = m_ref[...]
    mt = jnp.maximum(lf + m_old, ig)
    mo_ref[...] = mt
    wa = jnp.exp(ig - mt)
    wi = jnp.exp(lf + m_old - mt)
    fl = jnp.exp(-mt)
    wa_t, wi_t, fl_t = wa.T, wi.T, fl.T
    lane = lax.broadcasted_iota(jnp.int32, (d, tb), 1)
    for h in range(MLSTM_HEADS):
        sl = slice(h * d, (h + 1) * d)
        qh, kh, vh = q_ref[:, sl], k_ref[:, sl], v_ref[:, sl]
        nh = n_ref[:, h, :]
        qt, kt, vt, nt = qh.T, kh.T, vh.T, nh.T
        wa_r, wi_r, fl_r = wa_t[h:h + 1, :], wi_t[h:h + 1, :], fl_t[h:h + 1, :]
        s = jnp.sum(qt * kt, 0, keepdims=True) * wa_r
        cq = jnp.zeros((d, tb), F32)
        for b in range(tb):
            cq = jnp.where(lane == b, jnp.sum(c_ref[b, h] * qh[b:b + 1, :], -1, keepdims=True), cq)
        num = s * vt + wi_r * cq
        qn = s + wi_r * jnp.sum(nt * qt, 0, keepdims=True)
        hh = num / jnp.maximum(jnp.abs(qn), fl_r)
        mu = jnp.mean(hh, 0, keepdims=True)
        xc = hh - mu
        var = jnp.mean(xc * xc, 0, keepdims=True)
        hn_ref[:, sl] = (xc * lax.rsqrt(var + LN_EPS)).T
        wav = wa_r * vt
        for b in range(tb):
            co_ref[b, h] = wi_r[:, b:b + 1] * c_ref[b, h] + wav[:, b:b + 1] * kh[b:b + 1, :]
        no_ref[:, h, :] = wi[:, h:h + 1] * nh + wa[:, h:h + 1] * kh


def _mlstm_step(q, k, v, gi, gf, bi, bf, c_all, n_all, m0, layer, tb):
    n, w = q.shape
    d = MLSTM_HEAD_DIM
    row = lambda c: pl.BlockSpec((tb, c), lambda i: (i, 0))
    vec = lambda a: pl.BlockSpec(a.shape, lambda i: (0, 0))
    cin = pl.BlockSpec((None, tb, MLSTM_HEADS, d, d), lambda i: (layer, i, 0, 0, 0))
    nin = pl.BlockSpec((None, tb, MLSTM_HEADS, d), lambda i: (layer, i, 0, 0))
    cout = pl.BlockSpec((tb, MLSTM_HEADS, d, d), lambda i: (i, 0, 0, 0))
    nout = pl.BlockSpec((tb, MLSTM_HEADS, d), lambda i: (i, 0, 0))
    out_shape = (
        jax.ShapeDtypeStruct((n, w), F32),
        jax.ShapeDtypeStruct(c_all.shape[1:], F32),
        jax.ShapeDtypeStruct(n_all.shape[1:], F32),
        jax.ShapeDtypeStruct((n, LANES), F32),
    )
    return pl.pallas_call(
        _mlstm_step_kernel, grid=(n // tb,),
        in_specs=[row(w), row(w), row(w), row(LANES), row(LANES), vec(bi), vec(bf), cin, nin, row(LANES)],
        out_specs=(row(w), cout, nout, row(LANES)), out_shape=out_shape,
        compiler_params=_params("parallel"), name="mlstm_step")(q, k, v, gi, gf, bi, bf, c_all, n_all, m0)


def _mixout_kernel(po_ref, hn_ref, og_ref, x_ref, ng_ref, w_ref, g_ref, b_ref, o_ref):
    mo = jax.nn.sigmoid(og_ref[...]) * hn_ref[...] * ng_ref[...]
    mix = _dot(po_ref[...], w_ref[0:POOL_WIDTH, :]) + _dot(mo.astype(BF16), w_ref[POOL_WIDTH:, :])
    o_ref[...] = _layer_norm(ALPHA * x_ref[...] + mix, g_ref[...], b_ref[...])


def _mixout(po, hn, og, x, ng, w, g, b, tm):
    n, d = x.shape
    row = lambda c: pl.BlockSpec((tm, c), lambda i: (i, 0))
    vec = lambda a: pl.BlockSpec(a.shape, lambda i: (0, 0))
    return pl.pallas_call(
        _mixout_kernel, grid=(n // tm,),
        in_specs=[row(POOL_WIDTH), row(MLSTM_WIDTH), row(MLSTM_WIDTH), row(d), vec(ng), vec(w), vec(g), vec(b)],
        out_specs=row(d), out_shape=jax.ShapeDtypeStruct((n, d), F32),
        compiler_params=_params("parallel"), name="mixout")(po, hn, og, x, ng, w, g, b)


def _mm_kernel(x_ref, w_ref, o_ref, *, scale):
    y = _dot(x_ref[...].astype(BF16), w_ref[...])
    if scale != 1.0:
        y = y * scale
    o_ref[...] = y.astype(o_ref.dtype)


def _mm(x, w, tm, out_dtype, scale=1.0, name="proj"):
    n, d = x.shape
    dout = w.shape[1]
    return pl.pallas_call(
        functools.partial(_mm_kernel, scale=scale), grid=(n // tm,),
        in_specs=[pl.BlockSpec((tm, d), lambda i: (i, 0)), pl.BlockSpec(w.shape, lambda i: (0, 0))],
        out_specs=pl.BlockSpec((tm, dout), lambda i: (i, 0)),
        out_shape=jax.ShapeDtypeStruct((n, dout), out_dtype),
        compiler_params=_params("parallel"), name=name)(x, w)


def _attn_block_kernel(x_ref, k_ref, v_ref, wq_ref, wo_ref, g_ref, b_ref, o_ref):
    x = x_ref[...]
    qb = (_dot(x.astype(BF16), wq_ref[...]) * (CA_HEAD_DIM ** -0.5)).astype(BF16)
    kb = k_ref[...].astype(BF16)
    vb = v_ref[...].astype(BF16)
    ctx = []
    for h in range(CA_HEADS):
        sl = slice(h * CA_HEAD_DIM, (h + 1) * CA_HEAD_DIM)
        s = _dot_nt(qb[:, sl], kb[:, sl])
        e = jnp.exp(s - jnp.max(s, -1, keepdims=True))
        ctx.append((_dot(e.astype(BF16), vb[:, sl]) / jnp.sum(e, -1, keepdims=True)).astype(BF16))
    y = _dot(jnp.concatenate(ctx, axis=1), wo_ref[...])
    o_ref[...] = _layer_norm(ALPHA * x + y, g_ref[...], b_ref[...])


def _attn_block(x, k, v, wq, wo, g, b, seq, tq):
    n, d = x.shape
    per = seq // tq
    row = pl.BlockSpec((tq, d), lambda i: (i, 0))
    kspec = pl.BlockSpec((None, N_MEM, d), lambda i: (i // per, 0, 0))
    vec = lambda a: pl.BlockSpec(a.shape, lambda i: (0, 0))
    return pl.pallas_call(
        _attn_block_kernel, grid=(n // tq,),
        in_specs=[row, kspec, kspec, vec(wq), vec(wo), vec(g), vec(b)], out_specs=row,
        out_shape=jax.ShapeDtypeStruct((n, d), F32),
        compiler_params=_params("parallel"), name="attn_block")(x, k, v, wq, wo, g, b)


def _attn_step_kernel(q_ref, k_ref, v_ref, o_ref):
    bb = q_ref.shape[0]
    rows = N_MEM * CA_HEADS
    lane = lax.broadcasted_iota(jnp.int32, (SUBLANES, rows), 1)
    row = lax.broadcasted_iota(jnp.int32, (SUBLANES, rows), 0)
    own = lax.rem(lane, CA_HEADS) == lax.rem(row, CA_HEADS)
    for j in range(bb):
        kf = k_ref[j].reshape(rows, CA_HEAD_DIM).astype(BF16)
        vf = v_ref[j].reshape(rows, CA_HEAD_DIM).astype(BF16)
        s = jnp.where(own, _dot_nt(q_ref[j].astype(BF16), kf), -jnp.inf)
        e = jnp.exp(s - jnp.max(s, -1, keepdims=True))
        o_ref[j] = _dot(e.astype(BF16), vf) / jnp.sum(e, -1, keepdims=True)


def _attn_step(q, k_all, v_all, layer, bb):
    n = q.shape[0]
    qspec = pl.BlockSpec((bb, SUBLANES, CA_HEAD_DIM), lambda i: (i, 0, 0))
    kspec = pl.BlockSpec((None, bb, N_MEM, CA_HEADS, CA_HEAD_DIM), lambda i: (layer, i, 0, 0, 0))
    return pl.pallas_call(
        _attn_step_kernel, grid=(n // bb,), in_specs=[qspec, kspec, kspec], out_specs=qspec,
        out_shape=jax.ShapeDtypeStruct((n, SUBLANES, CA_HEAD_DIM), F32),
        compiler_params=_params("parallel"), name="attn_step")(q, k_all, v_all)


def _mm_res_ln_kernel(a_ref, x_ref, w_ref, g_ref, b_ref, o_ref):
    y = _dot(a_ref[...].astype(BF16), w_ref[...])
    o_ref[...] = _layer_norm(ALPHA * x_ref[...] + y, g_ref[...], b_ref[...])


def _mm_res_ln(a, x, w, g, b, tm):
    n, d = x.shape
    row = lambda c: pl.BlockSpec((tm, c), lambda i: (i, 0))
    vec = lambda arr: pl.BlockSpec(arr.shape, lambda i: (0, 0))
    return pl.pallas_call(
        _mm_res_ln_kernel, grid=(n // tm,),
        in_specs=[row(a.shape[1]), row(d), vec(w), vec(g), vec(b)], out_specs=row(d),
        out_shape=jax.ShapeDtypeStruct((n, d), F32),
        compiler_params=_params("parallel"), name="proj_res_ln")(a, x, w, g, b)


def _route(lt):
    gl = [lt[g:g + 1, :] for g in range(N_GROUPS)]
    gmax = functools.reduce(jnp.maximum, gl)
    gsum = functools.reduce(jnp.add, [jnp.exp(x - gmax) for x in gl])
    pg_sel = 1.0 / gsum
    taken = jnp.zeros_like(gmax, dtype=jnp.bool_)
    g_hot = []
    for x in gl:
        hot = jnp.logical_and(x == gmax, jnp.logical_not(taken))
        taken = jnp.logical_or(taken, hot)
        g_hot.append(hot)
    el = []
    for j in range(EXPERTS_PER_GROUP):
        rows = [lt[SUBLANES + g * EXPERTS_PER_GROUP + j:SUBLANES + g * EXPERTS_PER_GROUP + j + 1, :]
                for g in range(N_GROUPS)]
        x = rows[N_GROUPS - 1]
        for g in range(N_GROUPS - 2, -1, -1):
            x = jnp.where(g_hot[g], rows[g], x)
        el.append(x)
    emax = functools.reduce(jnp.maximum, el)
    ee = [jnp.exp(x - emax) for x in el]
    esum = functools.reduce(jnp.add, ee)
    pe = [x / esum for x in ee]
    p1 = functools.reduce(jnp.maximum, pe)
    taken = jnp.zeros_like(gmax, dtype=jnp.bool_)
    hot1 = []
    for x in pe:
        hot = jnp.logical_and(x == p1, jnp.logical_not(taken))
        taken = jnp.logical_or(taken, hot)
        hot1.append(hot)
    rest = [jnp.where(h1, -jnp.inf, x) for h1, x in zip(hot1, pe)]
    p2 = functools.reduce(jnp.maximum, rest)
    taken = jnp.zeros_like(gmax, dtype=jnp.bool_)
    hot2 = []
    for x in rest:
        hot = jnp.logical_and(x == p2, jnp.logical_not(taken))
        taken = jnp.logical_or(taken, hot)
        hot2.append(hot)
    psum = p1 + p2
    gate = [jnp.where(h1, pg_sel * p1 / psum, jnp.where(h2, pg_sel * p2 / psum, 0.0))
            for h1, h2 in zip(hot1, hot2)]
    return [jnp.where(g_hot[g], gate[j], 0.0) for g in range(N_GROUPS) for j in range(EXPERTS_PER_GROUP)]


def _expert_cast_kernel(wg_ref, wu_ref, wd_ref, og_ref, ou_ref, od_ref):
    f = D_EXPERT
    for j in range(EXPERTS_PER_GROUP):
        og_ref[:, j * f:(j + 1) * f] = wg_ref[j].astype(BF16)
        ou_ref[:, j * f:(j + 1) * f] = wu_ref[j].astype(BF16)
        od_ref[j * f:(j + 1) * f, :] = wd_ref[j].astype(BF16)


def _expert_cast(wg, wu, wd, layer):
    d = wg.shape[2]
    e, gf = EXPERTS_PER_GROUP, EXPERTS_PER_GROUP * D_EXPERT
    cols = pl.BlockSpec((None, e, d, D_EXPERT), lambda g: (layer, g, 0, 0))
    rows = pl.BlockSpec((None, e, D_EXPERT, d), lambda g: (layer, g, 0, 0))
    return pl.pallas_call(
        _expert_cast_kernel, grid=(N_GROUPS,), in_specs=[cols, cols, rows],
        out_specs=(pl.BlockSpec((None, d, gf), lambda g: (g, 0, 0)), pl.BlockSpec((None, d, gf), lambda g: (g, 0, 0)),
                   pl.BlockSpec((None, gf, d), lambda g: (g, 0, 0))),
        out_shape=(jax.ShapeDtypeStruct((N_GROUPS, d, gf), BF16), jax.ShapeDtypeStruct((N_GROUPS, d, gf), BF16),
                   jax.ShapeDtypeStruct((N_GROUPS, gf, d), BF16)),
        compiler_params=_params("parallel"), name="expert_cast")(wg, wu, wd)


def _moe_kernel(x_ref, wr_ref, br_ref, wg_ref, wu_ref, wd_ref, g_ref, b_ref, o_ref,
                xb_ref, ct_ref, comb_ref, acc_ref):
    grp = pl.program_id(1)

    @pl.when(grp == 0)
    def _():
        x = x_ref[...]
        xb_ref[...] = x.astype(BF16)
        lt = _dot_nt(wr_ref[...], x, precision=lax.Precision.HIGHEST) + br_ref[...]
        ct_ref[...] = jnp.zeros_like(ct_ref)
        for i, row in enumerate(_route(lt)):
            ct_ref[i:i + 1, :] = row
        comb_ref[...] = ct_ref[...].T
        acc_ref[...] = jnp.zeros_like(acc_ref)

    xb = xb_ref[...]
    comb = comb_ref[...]
    lane = lax.broadcasted_iota(jnp.int32, comb.shape, 1)
    f = D_EXPERT
    hh = []
    for j in range(EXPERTS_PER_GROUP):
        c = jnp.sum(jnp.where(lane == grp * EXPERTS_PER_GROUP + j, comb, 0.0), -1, keepdims=True)
        hg = _dot(xb, wg_ref[:, j * f:(j + 1) * f])
        hu = _dot(xb, wu_ref[:, j * f:(j + 1) * f])
        hh.append((hg * jax.nn.sigmoid(hg) * hu * c).astype(BF16))
    acc_ref[...] += _dot(jnp.concatenate(hh, axis=1), wd_ref[...])

    @pl.when(grp == N_GROUPS - 1)
    def _():
        o_ref[...] = _layer_norm(ALPHA * x_ref[...] + acc_ref[...], g_ref[...], b_ref[...])


def _moe(x, wr, br, wg, wu, wd, g, b, tm):
    n, d = x.shape
    row = pl.BlockSpec((tm, d), lambda i, e: (i, 0))
    vec = lambda a: pl.BlockSpec(a.shape, lambda i, e: (0, 0))
    grp = lambda a: pl.BlockSpec((None,) + a.shape[1:], lambda i, e: (e, 0, 0))
    scratch = [pltpu.VMEM((tm, d), BF16), pltpu.VMEM((LANES, tm), F32), pltpu.VMEM((tm, LANES), F32),
               pltpu.VMEM((tm, d), F32)]
    return pl.pallas_call(
        _moe_kernel, grid=(n // tm, N_GROUPS),
        in_specs=[row, vec(wr), vec(br), grp(wg), grp(wu), grp(wd), vec(g), vec(b)],
        out_specs=row, out_shape=jax.ShapeDtypeStruct((n, d), F32), scratch_shapes=scratch,
        compiler_params=_params("parallel", "arbitrary"), name="moe")(x, wr, br, wg, wu, wd, g, b)


def _pad_lanes(v):
    return jnp.zeros((1, LANES), F32).at[0, :v.shape[0]].set(v)


def _pad_rows(v):
    return jnp.zeros((SUBLANES, 1), F32).at[:v.shape[0], 0].set(v)


def _layer_weights(l, w_in, b_i, b_f, pool_w, pool_scale, mlstm_norm_g, w_out, ln1_g, ln1_b, ca_wq, ca_wo,
                   ln2_g, ln2_b, w_gr, b_gr, w_er, b_er, w_gate, w_up, w_down, ln3_g, ln3_b):
    d = D_MODEL
    wi = w_in[l]
    w_gate_cols = wi[:, GATE_OFF:]
    wg = jnp.zeros((d, 2 * LANES), F32)
    wg = wg.at[:, 0:MLSTM_HEADS].set(w_gate_cols[:, :MLSTM_HEADS])
    wg = wg.at[:, LANES:LANES + MLSTM_HEADS].set(w_gate_cols[:, MLSTM_HEADS:])
    wgt = jnp.zeros((2 * SUBLANES, d), F32)
    wgt = wgt.at[0:MLSTM_HEADS].set(w_gate_cols[:, :MLSTM_HEADS].T)
    wgt = wgt.at[SUBLANES:SUBLANES + MLSTM_HEADS].set(w_gate_cols[:, MLSTM_HEADS:].T)
    wr = jnp.zeros((ROUTER_ROWS, d), F32)
    wr = wr.at[0:N_GROUPS].set(w_gr[l].T).at[SUBLANES:SUBLANES + N_EXPERTS].set(w_er[l].T)
    br = jnp.zeros((ROUTER_ROWS, 1), F32)
    br = br.at[0:N_GROUPS, 0].set(b_gr[l]).at[SUBLANES:SUBLANES + N_EXPERTS, 0].set(b_er[l])
    row = lambda v: v.reshape(1, -1)
    return dict(
        w_main=wi[:, :GATE_OFF].astype(BF16), wg=wg.astype(BF16), wgt=wgt.astype(BF16),
        bi=_pad_lanes(b_i[l]), bf=_pad_lanes(b_f[l]), bit=_pad_rows(b_i[l]), bft=_pad_rows(b_f[l]),
        pool_w=pool_w[l].astype(BF16), pool_scale=pool_scale[l], norm_g=row(mlstm_norm_g[l]),
        w_out=w_out[l].astype(BF16), ln1=(row(ln1_g[l]), row(ln1_b[l])),
        wq=ca_wq[l].astype(BF16), wo=ca_wo[l].astype(BF16), ln2=(row(ln2_g[l]), row(ln2_b[l])),
        wr=wr, br=br, experts=_expert_cast(w_gate, w_up, w_down, l), ln3=(row(ln3_g[l]), row(ln3_b[l])))


def _trunk_layer(x, nb, seq, layer, mem, pool_prev, state, p, tm, tm_moe):
    chunked = state is None
    u, q, k, v, og, gi, gf, git, gft = _inproj(x, p["w_main"], p["wg"], p["wgt"], seq, tm,
                                               BF16 if chunked else F32)
    u3 = u.reshape(nb, seq, POOL_WIDTH)
    ext = u3 if pool_prev is None else jnp.concatenate([pool_prev, u3], axis=1)
    period = ext.shape[1]
    pooled = _pool(ext.reshape(nb * period, POOL_WIDTH), p["pool_w"], p["pool_scale"], period)
    pooled = pooled.reshape(nb, period, POOL_WIDTH)[:, period - seq:].reshape(nb * seq, POOL_WIDTH)
    pool_buf = ext[:, -POOL_BUF:]
    if chunked:
        r3 = lambda a: a.reshape(nb, seq, a.shape[-1])
        hn, c1, n1, m1 = _mlstm_chunk(r3(q), r3(k), r3(v), r3(gi), r3(gf), git, gft,
                                      p["bi"], p["bf"], p["bit"], p["bft"])
        hn = hn.reshape(nb * seq, MLSTM_WIDTH)
        m1 = m1[:, 0, :MLSTM_HEADS]
    else:
        c_all, n_all, m0 = state
        m0p = jnp.zeros((nb, LANES), F32).at[:, :MLSTM_HEADS].set(m0)
        hn, c1, n1, m1 = _mlstm_step(q, k, v, gi, gf, p["bi"], p["bf"], c_all, n_all, m0p, layer, SUBLANES)
        m1 = m1[:, :MLSTM_HEADS]
    x = _mixout(pooled, hn, og, x, p["norm_g"], p["w_out"], *p["ln1"], tm)
    if chunked:
        x = _attn_block(x, mem[0], mem[1], p["wq"], p["wo"], *p["ln2"], seq, tm)
    else:
        qc = _mm(x, p["wq"], tm, F32, scale=CA_HEAD_DIM ** -0.5, name="ca_q")
        qh = jnp.zeros((nb, SUBLANES, CA_HEAD_DIM), F32).at[:, :CA_HEADS].set(qc.reshape(nb, CA_HEADS, CA_HEAD_DIM))
        ctx = _attn_step(qh, mem[0], mem[1], layer, 4)[:, :CA_HEADS].reshape(nb, D_MODEL)
        x = _mm_res_ln(ctx, x, p["wo"], *p["ln2"], tm)
    x = _moe(x, p["wr"], p["br"], *p["experts"], *p["ln3"], tm_moe)
    return x, pool_buf, c1, n1, m1


def kernel(x_prompt, x_sample, mem_prompt, cache_pool, state_mlstm_C, state_mlstm_n, state_mlstm_m,
           cache_mem_k, cache_mem_v, emb_ln_g, emb_ln_b, w_in, b_i, b_f, pool_w, pool_scale,
           mlstm_norm_g, w_out, ln1_g, ln1_b, ca_wq, ca_wk, ca_wv, ca_wo, ln2_g, ln2_b,
           w_gr, b_gr, w_er, b_er, w_gate, w_up, w_down, ln3_g, ln3_b):
    bp, tp, d = x_prompt.shape
    bs, ts, _ = x_sample.shape
    tm_p, tm_s = 512, bs * ts
    xp = _ln(x_prompt.reshape(bp * tp, d), emb_ln_g, emb_ln_b, tm_p)
    xs = _ln(x_sample.reshape(bs * ts, d), emb_ln_g, emb_ln_b, tm_s)
    mem2d = mem_prompt.reshape(bp * N_MEM, d)
    outs = [[] for _ in range(10)]
    for l in range(DEPTH):
        p = _layer_weights(l, w_in, b_i, b_f, pool_w, pool_scale, mlstm_norm_g, w_out, ln1_g, ln1_b, ca_wq,
                           ca_wo, ln2_g, ln2_b, w_gr, b_gr, w_er, b_er, w_gate, w_up, w_down, ln3_g, ln3_b)
        mk = _mm(mem2d, ca_wk[l].astype(BF16), tm_p, F32, name="mem_k").reshape(bp, N_MEM, d)
        mv = _mm(mem2d, ca_wv[l].astype(BF16), tm_p, F32, name="mem_v").reshape(bp, N_MEM, d)
        xp, pb, c1, n1, m1 = _trunk_layer(xp, bp, tp, l, (mk, mv), None, None, p, tm_p, 2 * tm_p)
        xs, sb, cs, ns, ms = _trunk_layer(xs, bs, ts, l, (cache_mem_k, cache_mem_v), cache_pool[l],
                                          (state_mlstm_C, state_mlstm_n, state_mlstm_m[l]), p, tm_s, tm_s)
        for lst, val in zip(outs, (pb, c1, n1, m1, mk.reshape(bp, N_MEM, CA_HEADS, CA_HEAD_DIM),
                                   mv.reshape(bp, N_MEM, CA_HEADS, CA_HEAD_DIM), sb, cs, ns, ms)):
            lst.append(val)
    return (xp.reshape(bp, tp, d), xs.reshape(bs, ts, d)) + tuple(jnp.stack(o) for o in outs)
```

```python
import functools

import jax
import jax.numpy as jnp
from jax import lax
from jax.experimental import pallas as pl
from jax.experimental.pallas import tpu as pltpu

F32 = jnp.float32
BF16 = jnp.bfloat16

D_MODEL = 1024
DEPTH = 4
POOL_WIDTH = 512
POOL_GROUPS = 4
POOL_GROUP_DIM = 128
POOL_WINDOWS = (2, 4, 8, 16)
POOL_BUF = 15
MLSTM_WIDTH = 512
MLSTM_HEADS = 4
MLSTM_HEAD_DIM = 128
N_MEM = 256
CA_HEADS = 4
CA_HEAD_DIM = 256
N_GROUPS = 4
EXPERTS_PER_GROUP = 4
N_EXPERTS = 16
D_EXPERT = 256
ALPHA = (2 * DEPTH) ** 0.25
LN_EPS = 1e-5
GATE_OFF = POOL_WIDTH + 4 * MLSTM_WIDTH

LANES = 128
SUBLANES = 8
VMEM_LIMIT = 56 * 1024 * 1024
MLSTM_CHUNK = 256
POOL_ROWS = 2048
ROUTER_ROWS = 32


def _params(*sem):
    return pltpu.CompilerParams(dimension_semantics=sem, vmem_limit_bytes=VMEM_LIMIT)


def _dot(a, b):
    return jnp.dot(a, b, preferred_element_type=F32)


def _dot_nt(a, b, precision=None):
    return lax.dot_general(a, b, (((1,), (1,)), ((), ())), precision=precision,
                           preferred_element_type=F32)


def _dot_tn(a, b):
    return lax.dot_general(a, b, (((0,), (0,)), ((), ())), preferred_element_type=F32)


def _layer_norm(x, g, b):
    mu = jnp.mean(x, -1, keepdims=True)
    xc = x - mu
    var = jnp.mean(xc * xc, -1, keepdims=True)
    return xc * lax.rsqrt(var + LN_EPS) * g + b


def _unit_norm(x):
    mu = jnp.mean(x, -1, keepdims=True)
    xc = x - mu
    var = jnp.mean(xc * xc, -1, keepdims=True)
    return xc * lax.rsqrt(var + LN_EPS)


def _log_sigmoid(x):
    return jnp.minimum(x, 0.0) - jnp.log1p(jnp.exp(-jnp.abs(x)))


def _scan(x, axis, op, fill):
    n = x.shape[axis]
    idx = lax.broadcasted_iota(jnp.int32, x.shape, axis)
    s = 1
    while s < n:
        x = op(x, jnp.where(idx >= s, pltpu.roll(x, s, axis), fill))
        s *= 2
    return x


def _ln_kernel(x_ref, g_ref, b_ref, o_ref):
    o_ref[...] = _layer_norm(x_ref[...], g_ref[...], b_ref[...])


def _ln(x, g, b, tm):
    n, d = x.shape
    row = pl.BlockSpec((tm, d), lambda i: (i, 0))
    vec = pl.BlockSpec((1, d), lambda i: (0, 0))
    return pl.pallas_call(
        _ln_kernel, grid=(n // tm,), in_specs=[row, vec, vec], out_specs=row,
        out_shape=jax.ShapeDtypeStruct((n, d), F32), compiler_params=_params("parallel"),
        name="emb_ln")(x, g.reshape(1, d), b.reshape(1, d))


def _inproj_kernel(x_ref, w_ref, wg_ref, wgt_ref, u_ref, q_ref, k_ref, v_ref, og_ref,
                   gi_ref, gf_ref, git_ref, gft_ref):
    xb = x_ref[...].astype(BF16)
    w = MLSTM_WIDTH
    u_ref[...] = _dot(xb, w_ref[:, 0:POOL_WIDTH])
    q = _dot(xb, w_ref[:, POOL_WIDTH:POOL_WIDTH + w]) * (MLSTM_HEAD_DIM ** -0.5)
    q_ref[...] = q.astype(q_ref.dtype)
    k_ref[...] = _dot(xb, w_ref[:, POOL_WIDTH + w:POOL_WIDTH + 2 * w]).astype(k_ref.dtype)
    v_ref[...] = _dot(xb, w_ref[:, POOL_WIDTH + 2 * w:POOL_WIDTH + 3 * w]).astype(v_ref.dtype)
    og_ref[...] = _dot(xb, w_ref[:, POOL_WIDTH + 3 * w:POOL_WIDTH + 4 * w])
    g = _dot(xb, wg_ref[...])
    gi_ref[...] = g[:, 0:LANES]
    gf_ref[...] = g[:, LANES:2 * LANES]
    gt = _dot_nt(wgt_ref[...], xb)
    git_ref[...] = gt[0:SUBLANES]
    gft_ref[...] = gt[SUBLANES:2 * SUBLANES]


def _inproj(x, w_main, wg, wgt, seq, tm, qkv_dtype):
    n, d = x.shape
    nt = n // tm
    if seq % tm == 0:
        per = seq // tm
        gt_shape, gt_map = (n // seq, SUBLANES, seq), (lambda i: (i // per, 0, i % per))
    else:
        gt_shape, gt_map = (1, SUBLANES, n), (lambda i: (0, 0, i))
    row = lambda c: pl.BlockSpec((tm, c), lambda i: (i, 0))
    full = lambda a: pl.BlockSpec(a.shape, lambda i: (0, 0))
    gt_spec = pl.BlockSpec((None, SUBLANES, tm), gt_map)
    out_shape = (
        jax.ShapeDtypeStruct((n, POOL_WIDTH), F32),
        jax.ShapeDtypeStruct((n, MLSTM_WIDTH), qkv_dtype),
        jax.ShapeDtypeStruct((n, MLSTM_WIDTH), qkv_dtype),
        jax.ShapeDtypeStruct((n, MLSTM_WIDTH), qkv_dtype),
        jax.ShapeDtypeStruct((n, MLSTM_WIDTH), F32),
        jax.ShapeDtypeStruct((n, LANES), F32),
        jax.ShapeDtypeStruct((n, LANES), F32),
        jax.ShapeDtypeStruct(gt_shape, F32),
        jax.ShapeDtypeStruct(gt_shape, F32),
    )
    return pl.pallas_call(
        _inproj_kernel, grid=(nt,),
        in_specs=[row(d), full(w_main), full(wg), full(wgt)],
        out_specs=(row(POOL_WIDTH), row(MLSTM_WIDTH), row(MLSTM_WIDTH), row(MLSTM_WIDTH),
                   row(MLSTM_WIDTH), row(LANES), row(LANES), gt_spec, gt_spec),
        out_shape=out_shape, compiler_params=_params("parallel"), name="inproj")(x, w_main, wg, wgt)


def _pool_kernel(ext_ref, pw_ref, sc_ref, o_ref, *, period):
    rows = ext_ref.shape[0]
    r = lax.broadcasted_iota(jnp.int32, (rows, POOL_GROUP_DIM), 0)
    if period != rows:
        r = lax.rem(r, period)
    for g, win in enumerate(POOL_WINDOWS):
        sl = slice(g * POOL_GROUP_DIM, (g + 1) * POOL_GROUP_DIM)
        x = ext_ref[:, sl]
        acc = x
        s = 1
        while s < win:
            acc = acc + jnp.where(r >= s, pltpu.roll(acc, s, 0), 0.0)
            s *= 2
        cnt = jnp.minimum(r + 1, win).astype(F32)
        d = acc / cnt - x
        y = _dot(d.astype(BF16), pw_ref[g]) * sc_ref[:, sl]
        o_ref[:, sl] = y.astype(o_ref.dtype)


def _pool(ext2d, pw, scale, period):
    n, c = ext2d.shape
    row = pl.BlockSpec((POOL_ROWS, c), lambda i: (i, 0))
    return pl.pallas_call(
        functools.partial(_pool_kernel, period=period), grid=(n // POOL_ROWS,),
        in_specs=[row, pl.BlockSpec(pw.shape, lambda i: (0, 0, 0)), pl.BlockSpec((1, c), lambda i: (0, 0))],
        out_specs=row, out_shape=jax.ShapeDtypeStruct((n, c), BF16),
        compiler_params=_params("parallel"), name="pool")(ext2d, pw, scale.reshape(1, c))


def _mlstm_chunk_kernel(q_ref, k_ref, v_ref, git_ref, gft_ref, bit_ref, bft_ref, hn_ref, c_ref, n_ref, m_ref,
                        at_s, ws_s, dc_s, mc_s, wi_s, fl_s):
    nb, L, _ = q_ref.shape
    rows = nb * SUBLANES

    @pl.when(pl.program_id(0) == 0)
    def _():
        c_ref[...] = jnp.zeros_like(c_ref)
        n_ref[...] = jnp.zeros_like(n_ref)
        m_ref[...] = jnp.zeros_like(m_ref)

    ig = git_ref[...].reshape(rows, L) + bit_ref[...]
    lf = _log_sigmoid(gft_ref[...].reshape(rows, L) + bft_ref[...])
    bc = _scan(lf, 1, jnp.add, 0.0)
    at = ig - bc
    m_prev = m_ref[...].reshape(rows, LANES)[:, 0:1]
    mc = jnp.maximum(_scan(at, 1, jnp.maximum, -jnp.inf), m_prev)
    mt = bc + mc
    m_last = mc[:, L - 1:L]
    at_s[...] = at.reshape(nb, SUBLANES, L)
    ws_s[...] = jnp.exp(at - m_last).reshape(nb, SUBLANES, L)
    dc_s[...] = jnp.broadcast_to(jnp.exp(m_prev - m_last), (rows, LANES)).reshape(nb, SUBLANES, LANES)
    m_ref[...] = jnp.broadcast_to(mt[:, L - 1:L], (rows, LANES)).reshape(nb, SUBLANES, LANES)
    wi = jnp.exp(m_prev - mc)
    fl = jnp.exp(-mt)
    for b in range(nb):
        rs = slice(b * SUBLANES, (b + 1) * SUBLANES)
        mc_s[b] = mc[rs].T
        wi_s[b] = wi[rs].T
        fl_s[b] = fl[rs].T

    causal = (lax.broadcasted_iota(jnp.int32, (L, L), 0) >= lax.broadcasted_iota(jnp.int32, (L, L), 1))

    def per_batch(b, carry):
        at = at_s[b]
        ws = ws_s[b]
        decay = dc_s[b]
        mc_c, wi_c, fl_c = mc_s[b], wi_s[b], fl_s[b]
        ws16 = jnp.concatenate([ws, ws], axis=0).astype(BF16)
        for h in range(MLSTM_HEADS):
            sl = slice(h * MLSTM_HEAD_DIM, (h + 1) * MLSTM_HEAD_DIM)
            col = slice(h, h + 1)
            qh = q_ref[b, :, sl]
            kh = k_ref[b, :, sl]
            vh = v_ref[b, :, sl]
            s = _dot_nt(qh, kh)
            p = jnp.where(causal, s * jnp.exp(at[col, :] - mc_c[:, col]), 0.0)
            c_old = c_ref[b, h]
            n_old = n_ref[b, col, :]
            wi = wi_c[:, col]
            num = _dot(p.astype(BF16), vh) + wi * _dot_nt(qh, c_old.astype(BF16))
            qn = jnp.sum(p, -1, keepdims=True) + wi * jnp.sum(qh.astype(F32) * n_old, -1, keepdims=True)
            hh = num / jnp.maximum(jnp.abs(qn), fl_c[:, col])
            hn_ref[b, :, sl] = _unit_norm(hh)
            dc = decay[col, 0:1]
            vts = (vh.astype(F32).T * ws[col, :]).astype(BF16)
            c_ref[b, h] = dc * c_old + _dot(vts, kh)
            n_ref[b, col, :] = dc * n_old + _dot(ws16, kh)[col, :]
        return carry

    lax.fori_loop(0, nb, per_batch, 0)


def _mlstm_chunk(q, k, v, git, gft, bit, bft):
    nb, t, w = q.shape
    L = MLSTM_CHUNK
    seq = lambda c: pl.BlockSpec((nb, L, c), lambda i: (0, i, 0))
    seqt = pl.BlockSpec((nb, SUBLANES, L), lambda i: (0, 0, i))
    vec = lambda a: pl.BlockSpec(a.shape, lambda i: (0, 0))
    out_shape = (
        jax.ShapeDtypeStruct((nb, t, w), F32),
        jax.ShapeDtypeStruct((nb, MLSTM_HEADS, MLSTM_HEAD_DIM, MLSTM_HEAD_DIM), F32),
        jax.ShapeDtypeStruct((nb, MLSTM_HEADS, MLSTM_HEAD_DIM), F32),
        jax.ShapeDtypeStruct((nb, SUBLANES, LANES), F32),
    )
    out_specs = (
        seq(w),
        pl.BlockSpec(out_shape[1].shape, lambda i: (0, 0, 0, 0)),
        pl.BlockSpec(out_shape[2].shape, lambda i: (0, 0, 0)),
        pl.BlockSpec(out_shape[3].shape, lambda i: (0, 0, 0)),
    )
    bit, bft = jnp.tile(bit, (nb, 1)), jnp.tile(bft, (nb, 1))
    scratch = [pltpu.VMEM((nb, SUBLANES, L), F32), pltpu.VMEM((nb, SUBLANES, L), F32),
               pltpu.VMEM((nb, SUBLANES, LANES), F32)] + [pltpu.VMEM((nb, L, SUBLANES), F32) for _ in range(3)]
    return pl.pallas_call(
        _mlstm_chunk_kernel, grid=(t // L,),
        in_specs=[seq(w), seq(w), seq(w), seqt, seqt, vec(bit), vec(bft)],
        out_specs=out_specs, out_shape=out_shape, scratch_shapes=scratch, compiler_params=_params("arbitrary"),
        name="mlstm_chunk")(q, k, v, git, gft, bit, bft)


def _mlstm_step_kernel(q_ref, k_ref, v_ref, gi_ref, gf_ref, bi_ref, bf_ref, c_ref, n_ref, m_ref,
                       hn_ref, co_ref, no_ref, mo_ref):
    tb = q_ref.shape[0]
    d = MLSTM_HEAD_DIM
    ig = gi_ref[...] + bi_ref[...]
    lf = _log_sigmoid(gf_ref[...] + bf_ref[...])
    m_old = m_ref[...]
    mt = jnp.maximum(lf + m_old, ig)
    mo_ref[...] = mt
    wa = jnp.exp(ig - mt)
    wi = jnp.exp(lf + m_old - mt)
    fl = jnp.exp(-mt)
    wa_t, wi_t, fl_t = wa.T, wi.T, fl.T
    lane = lax.broadcasted_iota(jnp.int32, (d, tb), 1)
    for h in range(MLSTM_HEADS):
        sl = slice(h * d, (h + 1) * d)
        qh, kh, vh = q_ref[:, sl], k_ref[:, sl], v_ref[:, sl]
        nh = n_ref[:, h, :]
        qt, kt, vt, nt = qh.T, kh.T, vh.T, nh.T
        wa_r, wi_r, fl_r = wa_t[h:h + 1, :], wi_t[h:h + 1, :], fl_t[h:h + 1, :]
        s = jnp.sum(qt * kt, 0, keepdims=True) * wa_r
        cq = jnp.zeros((d, tb), F32)
        for b in range(tb):
            cq = jnp.where(lane == b, jnp.sum(c_ref[b, h] * qh[b:b + 1, :], -1, keepdims=True), cq)
        num = s * vt + wi_r * cq
        qn = s + wi_r * jnp.sum(nt * qt, 0, keepdims=True)
        hh = num / jnp.maximum(jnp.abs(qn), fl_r)
        mu = jnp.mean(hh, 0, keepdims=True)
        xc = hh - mu
        var = jnp.mean(xc * xc, 0, keepdims=True)
        hn_ref[:, sl] = (xc * lax.rsqrt(var + LN_EPS)).T
        wav = wa_r * vt
        for b in range(tb):
            co_ref[b, h] = wi_r[:, b:b + 1] * c_ref[b, h] + wav[:, b:b + 1] * kh[b:b + 1, :]
        no_ref[:, h, :] = wi[:, h:h + 1] * nh + wa[:, h:h + 1] * kh


def _mlstm_step(q, k, v, gi, gf, bi, bf, c_all, n_all, m0, layer, tb):
    n, w = q.shape
    d = MLSTM_HEAD_DIM
    row = lambda c: pl.BlockSpec((tb, c), lambda i: (i, 0))
    vec = lambda a: pl.BlockSpec(a.shape, lambda i: (0, 0))
    cin = pl.BlockSpec((None, tb, MLSTM_HEADS, d, d), lambda i: (layer, i, 0, 0, 0))
    nin = pl.BlockSpec((None, tb, MLSTM_HEADS, d), lambda i: (layer, i, 0, 0))
    cout = pl.BlockSpec((tb, MLSTM_HEADS, d, d), lambda i: (i, 0, 0, 0))
    nout = pl.BlockSpec((tb, MLSTM_HEADS, d), lambda i: (i, 0, 0))
    out_shape = (
        jax.ShapeDtypeStruct((n, w), F32),
        jax.ShapeDtypeStruct(c_all.shape[1:], F32),
        jax.ShapeDtypeStruct(n_all.shape[1:], F32),
        jax.ShapeDtypeStruct((n, LANES), F32),
    )
    return pl.pallas_call(
        _mlstm_step_kernel, grid=(n // tb,),
        in_specs=[row(w), row(w), row(w), row(LANES), row(LANES), vec(bi), vec(bf), cin, nin, row(LANES)],
        out_specs=(row(w), cout, nout, row(LANES)), out_shape=out_shape,
        compiler_params=_params("parallel"), name="mlstm_step")(q, k, v, gi, gf, bi, bf, c_all, n_all, m0)


def _mixout_kernel(po_ref, hn_ref, og_ref, x_ref, ng_ref, w_ref, g_ref, b_ref, o_ref):
    mo = jax.nn.sigmoid(og_ref[...]) * hn_ref[...] * ng_ref[...]
    mix = _dot(po_ref[...], w_ref[0:POOL_WIDTH, :]) + _dot(mo.astype(BF16), w_ref[POOL_WIDTH:, :])
    o_ref[...] = _layer_norm(ALPHA * x_ref[...] + mix, g_ref[...], b_ref[...])


def _mixout(po, hn, og, x, ng, w, g, b, tm):
    n, d = x.shape
    row = lambda c: pl.BlockSpec((tm, c), lambda i: (i, 0))
    vec = lambda a: pl.BlockSpec(a.shape, lambda i: (0, 0))
    return pl.pallas_call(
        _mixout_kernel, grid=(n // tm,),
        in_specs=[row(POOL_WIDTH), row(MLSTM_WIDTH), row(MLSTM_WIDTH), row(d), vec(ng), vec(w), vec(g), vec(b)],
        out_specs=row(d), out_shape=jax.ShapeDtypeStruct((n, d), F32),
        compiler_params=_params("parallel"), name="mixout")(po, hn, og, x, ng, w, g, b)


def _mm_kernel(x_ref, w_ref, o_ref, *, scale):
    y = _dot(x_ref[...].astype(BF16), w_ref[...])
    if scale != 1.0:
        y = y * scale
    o_ref[...] = y.astype(o_ref.dtype)


def _mm(x, w, tm, out_dtype, scale=1.0, name="proj"):
    n, d = x.shape
    dout = w.shape[1]
    return pl.pallas_call(
        functools.partial(_mm_kernel, scale=scale), grid=(n // tm,),
        in_specs=[pl.BlockSpec((tm, d), lambda i: (i, 0)), pl.BlockSpec(w.shape, lambda i: (0, 0))],
        out_specs=pl.BlockSpec((tm, dout), lambda i: (i, 0)),
        out_shape=jax.ShapeDtypeStruct((n, dout), out_dtype),
        compiler_params=_params("parallel"), name=name)(x, w)


def _cast_kernel(w_ref, o_ref):
    o_ref[...] = w_ref[...].astype(o_ref.dtype)


def _cast_layer(w_all, layer, cols, tr=256):
    rows = w_all.shape[1]
    return pl.pallas_call(
        _cast_kernel, grid=(rows // tr,),
        in_specs=[pl.BlockSpec((None, tr, cols), lambda i: (layer, i, 0))],
        out_specs=pl.BlockSpec((tr, cols), lambda i: (i, 0)),
        out_shape=jax.ShapeDtypeStruct((rows, cols), BF16),
        compiler_params=_params("parallel"), name="weight_cast")(w_all)


def _mem_proj_kernel(x_ref, wk_ref, wv_ref, k4_ref, v4_ref, k2_ref, v2_ref, wkb, wvb):
    @pl.when(pl.program_id(1) == 0)
    def _():
        wkb[...] = wk_ref[...].astype(BF16)
        wvb[...] = wv_ref[...].astype(BF16)

    xb = x_ref[...].astype(BF16)
    k = _dot(xb, wkb[...])
    v = _dot(xb, wvb[...])
    k2_ref[...] = k
    v2_ref[...] = v
    k4_ref[...] = k.reshape(N_MEM, CA_HEADS, CA_HEAD_DIM)
    v4_ref[...] = v.reshape(N_MEM, CA_HEADS, CA_HEAD_DIM)


def _mem_proj(mem, wk_all, wv_all):
    nb, m, d = mem.shape
    depth = wk_all.shape[0]
    wspec = pl.BlockSpec((None, d, d), lambda l, b: (l, 0, 0))
    o4 = pl.BlockSpec((None, None, m, CA_HEADS, CA_HEAD_DIM), lambda l, b: (l, b, 0, 0, 0))
    o2 = pl.BlockSpec((None, None, m, d), lambda l, b: (l, b, 0, 0))
    s4 = jax.ShapeDtypeStruct((depth, nb, m, CA_HEADS, CA_HEAD_DIM), F32)
    s2 = jax.ShapeDtypeStruct((depth, nb, m, d), F32)
    return pl.pallas_call(
        _mem_proj_kernel, grid=(depth, nb),
        in_specs=[pl.BlockSpec((None, m, d), lambda l, b: (b, 0, 0)), wspec, wspec],
        out_specs=(o4, o4, o2, o2), out_shape=(s4, s4, s2, s2),
        scratch_shapes=[pltpu.VMEM((d, d), BF16), pltpu.VMEM((d, d), BF16)],
        compiler_params=_params("arbitrary", "arbitrary"), name="mem_proj")(mem, wk_all, wv_all)


def _attn_block_kernel(po_ref, hn_ref, og_ref, x_ref, k_ref, v_ref, ng_ref, wm_ref, g1_ref, b1_ref,
                       wq_ref, wo_ref, g_ref, b_ref, o_ref):
    mo = jax.nn.sigmoid(og_ref[...]) * hn_ref[...] * ng_ref[...]
    mix = _dot(po_ref[...], wm_ref[0:POOL_WIDTH, :]) + _dot(mo.astype(BF16), wm_ref[POOL_WIDTH:, :])
    x = _layer_norm(ALPHA * x_ref[...] + mix, g1_ref[...], b1_ref[...])
    qb = (_dot(x.astype(BF16), wq_ref[...]) * (CA_HEAD_DIM ** -0.5)).astype(BF16)
    kb = k_ref[...].astype(BF16)
    vb = v_ref[...].astype(BF16)
    ctx = []
    for h in range(CA_HEADS):
        sl = slice(h * CA_HEAD_DIM, (h + 1) * CA_HEAD_DIM)
        s = _dot_nt(qb[:, sl], kb[:, sl])
        e = jnp.exp(s - jnp.max(s, -1, keepdims=True))
        ctx.append((_dot(e.astype(BF16), vb[:, sl]) / jnp.sum(e, -1, keepdims=True)).astype(BF16))
    y = _dot(jnp.concatenate(ctx, axis=1), wo_ref[...])
    o_ref[...] = _layer_norm(ALPHA * x + y, g_ref[...], b_ref[...])


def _attn_block(po, hn, og, x, k, v, layer, ng, wm, ln1, wq, wo, ln2, seq, tq):
    n, d = x.shape
    per = seq // tq
    row = lambda c: pl.BlockSpec((tq, c), lambda i: (i, 0))
    kspec = pl.BlockSpec((None, None, N_MEM, d), lambda i: (layer, i // per, 0, 0))
    vec = lambda a: pl.BlockSpec(a.shape, lambda i: (0, 0))
    return pl.pallas_call(
        _attn_block_kernel, grid=(n // tq,),
        in_specs=[row(POOL_WIDTH), row(MLSTM_WIDTH), row(MLSTM_WIDTH), row(d), kspec, kspec, vec(ng), vec(wm),
                  vec(ln1[0]), vec(ln1[1]), vec(wq), vec(wo), vec(ln2[0]), vec(ln2[1])],
        out_specs=row(d), out_shape=jax.ShapeDtypeStruct((n, d), F32),
        compiler_params=_params("parallel"), name="attn_block")(po, hn, og, x, k, v, ng, wm, *ln1, wq, wo, *ln2)


def _attn_step_kernel(q_ref, k_ref, v_ref, o_ref):
    bb = q_ref.shape[0]
    rows = N_MEM * CA_HEADS
    lane = lax.broadcasted_iota(jnp.int32, (SUBLANES, rows), 1)
    row = lax.broadcasted_iota(jnp.int32, (SUBLANES, rows), 0)
    own = lax.rem(lane, CA_HEADS) == lax.rem(row, CA_HEADS)
    for j in range(bb):
        kf = k_ref[j].reshape(rows, CA_HEAD_DIM).astype(BF16)
        vf = v_ref[j].reshape(rows, CA_HEAD_DIM).astype(BF16)
        s = jnp.where(own, _dot_nt(q_ref[j].astype(BF16), kf), -jnp.inf)
        e = jnp.exp(s - jnp.max(s, -1, keepdims=True))
        o_ref[j] = _dot(e.astype(BF16), vf) / jnp.sum(e, -1, keepdims=True)


def _attn_step(q, k_all, v_all, layer, bb):
    n = q.shape[0]
    qspec = pl.BlockSpec((bb, SUBLANES, CA_HEAD_DIM), lambda i: (i, 0, 0))
    kspec = pl.BlockSpec((None, bb, N_MEM, CA_HEADS, CA_HEAD_DIM), lambda i: (layer, i, 0, 0, 0))
    return pl.pallas_call(
        _attn_step_kernel, grid=(n // bb,), in_specs=[qspec, kspec, kspec], out_specs=qspec,
        out_shape=jax.ShapeDtypeStruct((n, SUBLANES, CA_HEAD_DIM), F32),
        compiler_params=_params("parallel"), name="attn_step")(q, k_all, v_all)


def _mm_res_ln_kernel(a_ref, x_ref, w_ref, g_ref, b_ref, o_ref):
    y = _dot(a_ref[...].astype(BF16), w_ref[...])
    o_ref[...] = _layer_norm(ALPHA * x_ref[...] + y, g_ref[...], b_ref[...])


def _mm_res_ln(a, x, w, g, b, tm):
    n, d = x.shape
    row = lambda c: pl.BlockSpec((tm, c), lambda i: (i, 0))
    vec = lambda arr: pl.BlockSpec(arr.shape, lambda i: (0, 0))
    return pl.pallas_call(
        _mm_res_ln_kernel, grid=(n // tm,),
        in_specs=[row(a.shape[1]), row(d), vec(w), vec(g), vec(b)], out_specs=row(d),
        out_shape=jax.ShapeDtypeStruct((n, d), F32),
        compiler_params=_params("parallel"), name="proj_res_ln")(a, x, w, g, b)


def _route(lt):
    gl = [lt[g:g + 1, :] for g in range(N_GROUPS)]
    gmax = functools.reduce(jnp.maximum, gl)
    gsum = functools.reduce(jnp.add, [jnp.exp(x - gmax) for x in gl])
    pg_sel = 1.0 / gsum
    taken = jnp.zeros_like(gmax, dtype=jnp.bool_)
    g_hot = []
    for x in gl:
        hot = jnp.logical_and(x == gmax, jnp.logical_not(taken))
        taken = jnp.logical_or(taken, hot)
        g_hot.append(hot)
    el = []
    for j in range(EXPERTS_PER_GROUP):
        rows = [lt[SUBLANES + g * EXPERTS_PER_GROUP + j:SUBLANES + g * EXPERTS_PER_GROUP + j + 1, :]
                for g in range(N_GROUPS)]
        x = rows[N_GROUPS - 1]
        for g in range(N_GROUPS - 2, -1, -1):
            x = jnp.where(g_hot[g], rows[g], x)
        el.append(x)
    emax = functools.reduce(jnp.maximum, el)
    ee = [jnp.exp(x - emax) for x in el]
    esum = functools.reduce(jnp.add, ee)
    pe = [x / esum for x in ee]
    p1 = functools.reduce(jnp.maximum, pe)
    taken = jnp.zeros_like(gmax, dtype=jnp.bool_)
    hot1 = []
    for x in pe:
        hot = jnp.logical_and(x == p1, jnp.logical_not(taken))
        taken = jnp.logical_or(taken, hot)
        hot1.append(hot)
    rest = [jnp.where(h1, -jnp.inf, x) for h1, x in zip(hot1, pe)]
    p2 = functools.reduce(jnp.maximum, rest)
    taken = jnp.zeros_like(gmax, dtype=jnp.bool_)
    hot2 = []
    for x in rest:
        hot = jnp.logical_and(x == p2, jnp.logical_not(taken))
        taken = jnp.logical_or(taken, hot)
        hot2.append(hot)
    psum = p1 + p2
    gate = [jnp.where(h1, pg_sel * p1 / psum, jnp.where(h2, pg_sel * p2 / psum, 0.0))
            for h1, h2 in zip(hot1, hot2)]
    return [jnp.where(g_hot[g], gate[j], 0.0) for g in range(N_GROUPS) for j in range(EXPERTS_PER_GROUP)]


def _expert_cast_kernel(wg_ref, wu_ref, wd_ref, og_ref, ou_ref, od_ref):
    f = D_EXPERT
    for j in range(EXPERTS_PER_GROUP):
        og_ref[:, j * f:(j + 1) * f] = wg_ref[j].astype(BF16)
        ou_ref[:, j * f:(j + 1) * f] = wu_ref[j].astype(BF16)
        od_ref[j * f:(j + 1) * f, :] = wd_ref[j].astype(BF16)


def _expert_cast(wg, wu, wd, layer):
    d = wg.shape[2]
    e, gf = EXPERTS_PER_GROUP, EXPERTS_PER_GROUP * D_EXPERT
    cols = pl.BlockSpec((None, e, d, D_EXPERT), lambda g: (layer, g, 0, 0))
    rows = pl.BlockSpec((None, e, D_EXPERT, d), lambda g: (layer, g, 0, 0))
    return pl.pallas_call(
        _expert_cast_kernel, grid=(N_GROUPS,), in_specs=[cols, cols, rows],
        out_specs=(pl.BlockSpec((None, d, gf), lambda g: (g, 0, 0)), pl.BlockSpec((None, d, gf), lambda g: (g, 0, 0)),
                   pl.BlockSpec((None, gf, d), lambda g: (g, 0, 0))),
        out_shape=(jax.ShapeDtypeStruct((N_GROUPS, d, gf), BF16), jax.ShapeDtypeStruct((N_GROUPS, d, gf), BF16),
                   jax.ShapeDtypeStruct((N_GROUPS, gf, d), BF16)),
        compiler_params=_params("parallel"), name="expert_cast")(wg, wu, wd)


def _moe_kernel(x_ref, wr_ref, br_ref, wg_ref, wu_ref, wd_ref, g_ref, b_ref, o_ref,
                xb_ref, ct_ref, comb_ref, acc_ref):
    grp = pl.program_id(1)

    @pl.when(grp == 0)
    def _():
        x = x_ref[...]
        xh = x.astype(BF16)
        xb_ref[...] = xh
        xl = (x - xh.astype(F32)).astype(BF16)
        wr = wr_ref[...]
        wh = wr.astype(BF16)
        wl = (wr - wh.astype(F32)).astype(BF16)
        lt = _dot_nt(wh, xh) + (_dot_nt(wl, xh) + _dot_nt(wh, xl)) + br_ref[...]
        ct_ref[...] = jnp.zeros_like(ct_ref)
        for i, row in enumerate(_route(lt)):
            ct_ref[i:i + 1, :] = row
        comb_ref[...] = ct_ref[...].T
        acc_ref[...] = jnp.zeros_like(acc_ref)

    xb = xb_ref[...]
    comb = comb_ref[...]
    lane = lax.broadcasted_iota(jnp.int32, comb.shape, 1)
    f = D_EXPERT
    hh = []
    for j in range(EXPERTS_PER_GROUP):
        c = jnp.sum(jnp.where(lane == grp * EXPERTS_PER_GROUP + j, comb, 0.0), -1, keepdims=True)
        hg = _dot(xb, wg_ref[:, j * f:(j + 1) * f])
        hu = _dot(xb, wu_ref[:, j * f:(j + 1) * f])
        hh.append((hg * jax.nn.sigmoid(hg) * hu * c).astype(BF16))
    acc_ref[...] += _dot(jnp.concatenate(hh, axis=1), wd_ref[...])

    @pl.when(grp == N_GROUPS - 1)
    def _():
        o_ref[...] = _layer_norm(ALPHA * x_ref[...] + acc_ref[...], g_ref[...], b_ref[...])


def _moe(x, wr, br, wg, wu, wd, g, b, tm):
    n, d = x.shape
    row = pl.BlockSpec((tm, d), lambda i, e: (i, 0))
    vec = lambda a: pl.BlockSpec(a.shape, lambda i, e: (0, 0))
    grp = lambda a: pl.BlockSpec((None,) + a.shape[1:], lambda i, e: (e, 0, 0))
    scratch = [pltpu.VMEM((tm, d), BF16), pltpu.VMEM((LANES, tm), F32), pltpu.VMEM((tm, LANES), F32),
               pltpu.VMEM((tm, d), F32)]
    return pl.pallas_call(
        _moe_kernel, grid=(n // tm, N_GROUPS),
        in_specs=[row, vec(wr), vec(br), grp(wg), grp(wu), grp(wd), vec(g), vec(b)],
        out_specs=row, out_shape=jax.ShapeDtypeStruct((n, d), F32), scratch_shapes=scratch,
        compiler_params=_params("parallel", "arbitrary"), name="moe")(x, wr, br, wg, wu, wd, g, b)


def _pad_lanes(v):
    return jnp.zeros((1, LANES), F32).at[0, :v.shape[0]].set(v)


def _pad_rows(v):
    return jnp.zeros((SUBLANES, 1), F32).at[:v.shape[0], 0].set(v)


def _layer_weights(l, w_in, b_i, b_f, pool_w, pool_scale, mlstm_norm_g, w_out, ln1_g, ln1_b, ca_wq, ca_wo,
                   ln2_g, ln2_b, w_gr, b_gr, w_er, b_er, w_gate, w_up, w_down, ln3_g, ln3_b):
    d = D_MODEL
    wi = w_in[l]
    w_gate_cols = wi[:, GATE_OFF:]
    wg = jnp.zeros((d, 2 * LANES), F32)
    wg = wg.at[:, 0:MLSTM_HEADS].set(w_gate_cols[:, :MLSTM_HEADS])
    wg = wg.at[:, LANES:LANES + MLSTM_HEADS].set(w_gate_cols[:, MLSTM_HEADS:])
    wgt = jnp.zeros((2 * SUBLANES, d), F32)
    wgt = wgt.at[0:MLSTM_HEADS].set(w_gate_cols[:, :MLSTM_HEADS].T)
    wgt = wgt.at[SUBLANES:SUBLANES + MLSTM_HEADS].set(w_gate_cols[:, MLSTM_HEADS:].T)
    wr = jnp.zeros((ROUTER_ROWS, d), F32)
    wr = wr.at[0:N_GROUPS].set(w_gr[l].T).at[SUBLANES:SUBLANES + N_EXPERTS].set(w_er[l].T)
    br = jnp.zeros((ROUTER_ROWS, 1), F32)
    br = br.at[0:N_GROUPS, 0].set(b_gr[l]).at[SUBLANES:SUBLANES + N_EXPERTS, 0].set(b_er[l])
    row = lambda v: v.reshape(1, -1)
    return dict(
        w_main=_cast_layer(w_in, l, GATE_OFF), wg=wg.astype(BF16), wgt=wgt.astype(BF16),
        bi=_pad_lanes(b_i[l]), bf=_pad_lanes(b_f[l]), bit=_pad_rows(b_i[l]), bft=_pad_rows(b_f[l]),
        pool_w=pool_w[l].astype(BF16), pool_scale=pool_scale[l], norm_g=row(mlstm_norm_g[l]),
        w_out=_cast_layer(w_out, l, d), ln1=(row(ln1_g[l]), row(ln1_b[l])),
        wq=_cast_layer(ca_wq, l, d), wo=_cast_layer(ca_wo, l, d), ln2=(row(ln2_g[l]), row(ln2_b[l])),
        wr=wr, br=br, experts=_expert_cast(w_gate, w_up, w_down, l), ln3=(row(ln3_g[l]), row(ln3_b[l])))


def _trunk_layer(x, nb, seq, layer, mem, pool_prev, state, p, tm, tm_moe):
    chunked = state is None
    u, q, k, v, og, gi, gf, git, gft = _inproj(x, p["w_main"], p["wg"], p["wgt"], seq, tm,
                                               BF16 if chunked else F32)
    u3 = u.reshape(nb, seq, POOL_WIDTH)
    ext = u3 if pool_prev is None else jnp.concatenate([pool_prev, u3], axis=1)
    period = ext.shape[1]
    pooled = _pool(ext.reshape(nb * period, POOL_WIDTH), p["pool_w"], p["pool_scale"], period)
    pooled = pooled.reshape(nb, period, POOL_WIDTH)[:, period - seq:].reshape(nb * seq, POOL_WIDTH)
    pool_buf = ext[:, -POOL_BUF:]
    if chunked:
        r3 = lambda a: a.reshape(nb, seq, a.shape[-1])
        hn, c1, n1, m1 = _mlstm_chunk(r3(q), r3(k), r3(v), git, gft, p["bit"], p["bft"])
        hn = hn.reshape(nb * seq, MLSTM_WIDTH)
        m1 = m1[:, :MLSTM_HEADS, 0]
    else:
        c_all, n_all, m0 = state
        m0p = jnp.zeros((nb, LANES), F32).at[:, :MLSTM_HEADS].set(m0)
        hn, c1, n1, m1 = _mlstm_step(q, k, v, gi, gf, p["bi"], p["bf"], c_all, n_all, m0p, layer, SUBLANES)
        m1 = m1[:, :MLSTM_HEADS]
    if chunked:
        x = _attn_block(pooled, hn, og, x, mem[0], mem[1], layer, p["norm_g"], p["w_out"], p["ln1"], p["wq"],
                        p["wo"], p["ln2"], seq, tm)
    else:
        x = _mixout(pooled, hn, og, x, p["norm_g"], p["w_out"], *p["ln1"], tm)
        qc = _mm(x, p["wq"], tm, F32, scale=CA_HEAD_DIM ** -0.5, name="ca_q")
        qh = jnp.zeros((nb, SUBLANES, CA_HEAD_DIM), F32).at[:, :CA_HEADS].set(qc.reshape(nb, CA_HEADS, CA_HEAD_DIM))
        ctx = _attn_step(qh, mem[0], mem[1], layer, 4)[:, :CA_HEADS].reshape(nb, D_MODEL)
        x = _mm_res_ln(ctx, x, p["wo"], *p["ln2"], tm)
    x = _moe(x, p["wr"], p["br"], *p["experts"], *p["ln3"], tm_moe)
    return x, pool_buf, c1, n1, m1


def kernel(x_prompt, x_sample, mem_prompt, cache_pool, state_mlstm_C, state_mlstm_n, state_mlstm_m,
           cache_mem_k, cache_mem_v, emb_ln_g, emb_ln_b, w_in, b_i, b_f, pool_w, pool_scale,
           mlstm_norm_g, w_out, ln1_g, ln1_b, ca_wq, ca_wk, ca_wv, ca_wo, ln2_g, ln2_b,
           w_gr, b_gr, w_er, b_er, w_gate, w_up, w_down, ln3_g, ln3_b):
    bp, tp, d = x_prompt.shape
    bs, ts, _ = x_sample.shape
    tm_p, tm_s = 512, bs * ts
    xp = _ln(x_prompt.reshape(bp * tp, d), emb_ln_g, emb_ln_b, tm_p)
    xs = _ln(x_sample.reshape(bs * ts, d), emb_ln_g, emb_ln_b, tm_s)
    mk4, mv4, mk2, mv2 = _mem_proj(mem_prompt, ca_wk, ca_wv)
    outs = [[] for _ in range(8)]
    for l in range(DEPTH):
        p = _layer_weights(l, w_in, b_i, b_f, pool_w, pool_scale, mlstm_norm_g, w_out, ln1_g, ln1_b, ca_wq,
                           ca_wo, ln2_g, ln2_b, w_gr, b_gr, w_er, b_er, w_gate, w_up, w_down, ln3_g, ln3_b)
        xp, pb, c1, n1, m1 = _trunk_layer(xp, bp, tp, l, (mk2, mv2), None, None, p, tm_p, 2 * tm_p)
        xs, sb, cs, ns, ms = _trunk_layer(xs, bs, ts, l, (cache_mem_k, cache_mem_v), cache_pool[l],
                                          (state_mlstm_C, state_mlstm_n, state_mlstm_m[l]), p, tm_s, tm_s)
        for lst, val in zip(outs, (pb, c1, n1, m1, sb, cs, ns, ms)):
            lst.append(val)
    pp, pc, pn, pm, sp, sc, sn, sm = (jnp.stack(o) for o in outs)
    return (xp.reshape(bp, tp, d), xs.reshape(bs, ts, d), pp, pc, pn, pm, mk4, mv4, sp, sc, sn, sm)
```

```python
import functools

import jax
import jax.numpy as jnp
from jax import lax
from jax.experimental import pallas as pl
from jax.experimental.pallas import tpu as pltpu

F32 = jnp.float32
BF16 = jnp.bfloat16

D_MODEL = 1024
DEPTH = 4
POOL_WIDTH = 512
POOL_GROUPS = 4
POOL_GROUP_DIM = 128
POOL_WINDOWS = (2, 4, 8, 16)
POOL_BUF = 15
MLSTM_WIDTH = 512
MLSTM_HEADS = 4
MLSTM_HEAD_DIM = 128
N_MEM = 256
CA_HEADS = 4
CA_HEAD_DIM = 256
N_GROUPS = 4
EXPERTS_PER_GROUP = 4
N_EXPERTS = 16
D_EXPERT = 256
ALPHA = (2 * DEPTH) ** 0.25
LN_EPS = 1e-5
GATE_OFF = POOL_WIDTH + 4 * MLSTM_WIDTH

LANES = 128
SUBLANES = 8
VMEM_LIMIT = 56 * 1024 * 1024
MLSTM_CHUNK = 256
POOL_ROWS = 2048
ROUTER_ROWS = 32
MOE_CAP = 320


def _params(*sem):
    return pltpu.CompilerParams(dimension_semantics=sem, vmem_limit_bytes=VMEM_LIMIT)


def _dot(a, b):
    return jnp.dot(a, b, preferred_element_type=F32)


def _dot_nt(a, b, precision=None):
    return lax.dot_general(a, b, (((1,), (1,)), ((), ())), precision=precision,
                           preferred_element_type=F32)


def _dot_tn(a, b):
    return lax.dot_general(a, b, (((0,), (0,)), ((), ())), preferred_element_type=F32)


def _layer_norm(x, g, b):
    mu = jnp.mean(x, -1, keepdims=True)
    xc = x - mu
    var = jnp.mean(xc * xc, -1, keepdims=True)
    return xc * lax.rsqrt(var + LN_EPS) * g + b


def _unit_norm(x):
    mu = jnp.mean(x, -1, keepdims=True)
    xc = x - mu
    var = jnp.mean(xc * xc, -1, keepdims=True)
    return xc * lax.rsqrt(var + LN_EPS)


def _log_sigmoid(x):
    return jnp.minimum(x, 0.0) - jnp.log1p(jnp.exp(-jnp.abs(x)))


def _scan(x, axis, op, fill):
    n = x.shape[axis]
    idx = lax.broadcasted_iota(jnp.int32, x.shape, axis)
    s = 1
    while s < n:
        x = op(x, jnp.where(idx >= s, pltpu.roll(x, s, axis), fill))
        s *= 2
    return x


def _ln_kernel(x_ref, g_ref, b_ref, o_ref):
    o_ref[...] = _layer_norm(x_ref[...], g_ref[...], b_ref[...])


def _ln(x, g, b, tm):
    n, d = x.shape
    row = pl.BlockSpec((tm, d), lambda i: (i, 0))
    vec = pl.BlockSpec((1, d), lambda i: (0, 0))
    return pl.pallas_call(
        _ln_kernel, grid=(n // tm,), in_specs=[row, vec, vec], out_specs=row,
        out_shape=jax.ShapeDtypeStruct((n, d), F32), compiler_params=_params("parallel"),
        name="emb_ln")(x, g.reshape(1, d), b.reshape(1, d))


def _inproj_kernel(x_ref, w_ref, wg_ref, wgt_ref, u_ref, q_ref, k_ref, v_ref, og_ref,
                   gi_ref, gf_ref, git_ref, gft_ref):
    xb = x_ref[...].astype(BF16)
    w = MLSTM_WIDTH
    u_ref[...] = _dot(xb, w_ref[:, 0:POOL_WIDTH])
    q = _dot(xb, w_ref[:, POOL_WIDTH:POOL_WIDTH + w]) * (MLSTM_HEAD_DIM ** -0.5)
    q_ref[...] = q.astype(q_ref.dtype)
    k_ref[...] = _dot(xb, w_ref[:, POOL_WIDTH + w:POOL_WIDTH + 2 * w]).astype(k_ref.dtype)
    v_ref[...] = _dot(xb, w_ref[:, POOL_WIDTH + 2 * w:POOL_WIDTH + 3 * w]).astype(v_ref.dtype)
    og_ref[...] = _dot(xb, w_ref[:, POOL_WIDTH + 3 * w:POOL_WIDTH + 4 * w])
    g = _dot(xb, wg_ref[...])
    gi_ref[...] = g[:, 0:LANES]
    gf_ref[...] = g[:, LANES:2 * LANES]
    gt = _dot_nt(wgt_ref[...], xb)
    git_ref[...] = gt[0:SUBLANES]
    gft_ref[...] = gt[SUBLANES:2 * SUBLANES]


def _inproj(x, w_main, wg, wgt, seq, tm, qkv_dtype):
    n, d = x.shape
    nt = n // tm
    if seq % tm == 0:
        per = seq // tm
        gt_shape, gt_map = (n // seq, SUBLANES, seq), (lambda i: (i // per, 0, i % per))
    else:
        gt_shape, gt_map = (1, SUBLANES, n), (lambda i: (0, 0, i))
    row = lambda c: pl.BlockSpec((tm, c), lambda i: (i, 0))
    full = lambda a: pl.BlockSpec(a.shape, lambda i: (0, 0))
    gt_spec = pl.BlockSpec((None, SUBLANES, tm), gt_map)
    out_shape = (
        jax.ShapeDtypeStruct((n, POOL_WIDTH), F32),
        jax.ShapeDtypeStruct((n, MLSTM_WIDTH), qkv_dtype),
        jax.ShapeDtypeStruct((n, MLSTM_WIDTH), qkv_dtype),
        jax.ShapeDtypeStruct((n, MLSTM_WIDTH), qkv_dtype),
        jax.ShapeDtypeStruct((n, MLSTM_WIDTH), F32),
        jax.ShapeDtypeStruct((n, LANES), F32),
        jax.ShapeDtypeStruct((n, LANES), F32),
        jax.ShapeDtypeStruct(gt_shape, F32),
        jax.ShapeDtypeStruct(gt_shape, F32),
    )
    return pl.pallas_call(
        _inproj_kernel, grid=(nt,),
        in_specs=[row(d), full(w_main), full(wg), full(wgt)],
        out_specs=(row(POOL_WIDTH), row(MLSTM_WIDTH), row(MLSTM_WIDTH), row(MLSTM_WIDTH),
                   row(MLSTM_WIDTH), row(LANES), row(LANES), gt_spec, gt_spec),
        out_shape=out_shape, compiler_params=_params("parallel"), name="inproj")(x, w_main, wg, wgt)


def _pool_kernel(ext_ref, pw_ref, sc_ref, o_ref, *, period):
    rows = ext_ref.shape[0]
    r = lax.broadcasted_iota(jnp.int32, (rows, POOL_GROUP_DIM), 0)
    if period != rows:
        r = lax.rem(r, period)
    for g, win in enumerate(POOL_WINDOWS):
        sl = slice(g * POOL_GROUP_DIM, (g + 1) * POOL_GROUP_DIM)
        x = ext_ref[:, sl]
        acc = x
        s = 1
        while s < win:
            acc = acc + jnp.where(r >= s, pltpu.roll(acc, s, 0), 0.0)
            s *= 2
        cnt = jnp.minimum(r + 1, win).astype(F32)
        d = acc / cnt - x
        y = _dot(d.astype(BF16), pw_ref[g]) * sc_ref[:, sl]
        o_ref[:, sl] = y.astype(o_ref.dtype)


def _pool(ext2d, pw, scale, period):
    n, c = ext2d.shape
    row = pl.BlockSpec((POOL_ROWS, c), lambda i: (i, 0))
    return pl.pallas_call(
        functools.partial(_pool_kernel, period=period), grid=(n // POOL_ROWS,),
        in_specs=[row, pl.BlockSpec(pw.shape, lambda i: (0, 0, 0)), pl.BlockSpec((1, c), lambda i: (0, 0))],
        out_specs=row, out_shape=jax.ShapeDtypeStruct((n, c), BF16),
        compiler_params=_params("parallel"), name="pool")(ext2d, pw, scale.reshape(1, c))


def _mlstm_chunk_kernel(q_ref, k_ref, v_ref, git_ref, gft_ref, bit_ref, bft_ref, hn_ref, c_ref, n_ref, m_ref,
                        at_s, ws_s, dc_s, mc_s, wi_s, fl_s):
    nb, L, _ = q_ref.shape
    rows = nb * SUBLANES

    @pl.when(pl.program_id(0) == 0)
    def _():
        c_ref[...] = jnp.zeros_like(c_ref)
        n_ref[...] = jnp.zeros_like(n_ref)
        m_ref[...] = jnp.zeros_like(m_ref)

    ig = git_ref[...].reshape(rows, L) + bit_ref[...]
    lf = _log_sigmoid(gft_ref[...].reshape(rows, L) + bft_ref[...])
    bc = _scan(lf, 1, jnp.add, 0.0)
    at = ig - bc
    m_prev = m_ref[...].reshape(rows, LANES)[:, 0:1]
    mc = jnp.maximum(_scan(at, 1, jnp.maximum, -jnp.inf), m_prev)
    mt = bc + mc
    m_last = mc[:, L - 1:L]
    at_s[...] = at.reshape(nb, SUBLANES, L)
    ws_s[...] = jnp.exp(at - m_last).reshape(nb, SUBLANES, L)
    dc_s[...] = jnp.broadcast_to(jnp.exp(m_prev - m_last), (rows, LANES)).reshape(nb, SUBLANES, LANES)
    m_ref[...] = jnp.broadcast_to(mt[:, L - 1:L], (rows, LANES)).reshape(nb, SUBLANES, LANES)
    wi = jnp.exp(m_prev - mc)
    fl = jnp.exp(-mt)
    for b in range(nb):
        rs = slice(b * SUBLANES, (b + 1) * SUBLANES)
        mc_s[b] = mc[rs].T
        wi_s[b] = wi[rs].T
        fl_s[b] = fl[rs].T

    causal = (lax.broadcasted_iota(jnp.int32, (L, L), 0) >= lax.broadcasted_iota(jnp.int32, (L, L), 1))

    def per_batch(b, carry):
        at = at_s[b]
        ws = ws_s[b]
        decay = dc_s[b]
        mc_c, wi_c, fl_c = mc_s[b], wi_s[b], fl_s[b]
        ws16 = jnp.concatenate([ws, ws], axis=0).astype(BF16)
        for h in range(MLSTM_HEADS):
            sl = slice(h * MLSTM_HEAD_DIM, (h + 1) * MLSTM_HEAD_DIM)
            col = slice(h, h + 1)
            qh = q_ref[b, :, sl]
            kh = k_ref[b, :, sl]
            vh = v_ref[b, :, sl]
            s = _dot_nt(qh, kh)
            p = jnp.where(causal, s * jnp.exp(at[col, :] - mc_c[:, col]), 0.0)
            c_old = c_ref[b, h]
            n_old = n_ref[b, col, :]
            wi = wi_c[:, col]
            num = _dot(p.astype(BF16), vh) + wi * _dot_nt(qh, c_old.astype(BF16))
            qn = jnp.sum(p, -1, keepdims=True) + wi * jnp.sum(qh.astype(F32) * n_old, -1, keepdims=True)
            hh = num / jnp.maximum(jnp.abs(qn), fl_c[:, col])
            hn_ref[b, :, sl] = _unit_norm(hh)
            dc = decay[col, 0:1]
            vts = (vh.astype(F32).T * ws[col, :]).astype(BF16)
            c_ref[b, h] = dc * c_old + _dot(vts, kh)
            n_ref[b, col, :] = dc * n_old + _dot(ws16, kh)[col, :]
        return carry

    lax.fori_loop(0, nb, per_batch, 0)


def _mlstm_chunk(q, k, v, git, gft, bit, bft):
    nb, t, w = q.shape
    L = MLSTM_CHUNK
    seq = lambda c: pl.BlockSpec((nb, L, c), lambda i: (0, i, 0))
    seqt = pl.BlockSpec((nb, SUBLANES, L), lambda i: (0, 0, i))
    vec = lambda a: pl.BlockSpec(a.shape, lambda i: (0, 0))
    out_shape = (
        jax.ShapeDtypeStruct((nb, t, w), F32),
        jax.ShapeDtypeStruct((nb, MLSTM_HEADS, MLSTM_HEAD_DIM, MLSTM_HEAD_DIM), F32),
        jax.ShapeDtypeStruct((nb, MLSTM_HEADS, MLSTM_HEAD_DIM), F32),
        jax.ShapeDtypeStruct((nb, SUBLANES, LANES), F32),
    )
    out_specs = (
        seq(w),
        pl.BlockSpec(out_shape[1].shape, lambda i: (0, 0, 0, 0)),
        pl.BlockSpec(out_shape[2].shape, lambda i: (0, 0, 0)),
        pl.BlockSpec(out_shape[3].shape, lambda i: (0, 0, 0)),
    )
    bit, bft = jnp.tile(bit, (nb, 1)), jnp.tile(bft, (nb, 1))
    scratch = [pltpu.VMEM((nb, SUBLANES, L), F32), pltpu.VMEM((nb, SUBLANES, L), F32),
               pltpu.VMEM((nb, SUBLANES, LANES), F32)] + [pltpu.VMEM((nb, L, SUBLANES), F32) for _ in range(3)]
    return pl.pallas_call(
        _mlstm_chunk_kernel, grid=(t // L,),
        in_specs=[seq(w), seq(w), seq(w), seqt, seqt, vec(bit), vec(bft)],
        out_specs=out_specs, out_shape=out_shape, scratch_shapes=scratch, compiler_params=_params("arbitrary"),
        name="mlstm_chunk")(q, k, v, git, gft, bit, bft)


def _mlstm_step_kernel(q_ref, k_ref, v_ref, gi_ref, gf_ref, bi_ref, bf_ref, c_ref, n_ref, m_ref,
                       hn_ref, co_ref, no_ref, mo_ref):
    tb = q_ref.shape[0]
    d = MLSTM_HEAD_DIM
    ig = gi_ref[...] + bi_ref[...]
    lf = _log_sigmoid(gf_ref[...] + bf_ref[...])
    m_old = m_ref[...]
    mt = jnp.maximum(lf + m_old, ig)
    mo_ref[...] = mt
    wa = jnp.exp(ig - mt)
    wi = jnp.exp(lf + m_old - mt)
    fl = jnp.exp(-mt)
    wa_t, wi_t, fl_t = wa.T, wi.T, fl.T
    lane = lax.broadcasted_iota(jnp.int32, (d, tb), 1)
    for h in range(MLSTM_HEADS):
        sl = slice(h * d, (h + 1) * d)
        qh, kh, vh = q_ref[:, sl], k_ref[:, sl], v_ref[:, sl]
        nh = n_ref[:, h, :]
        qt, kt, vt, nt = qh.T, kh.T, vh.T, nh.T
        wa_r, wi_r, fl_r = wa_t[h:h + 1, :], wi_t[h:h + 1, :], fl_t[h:h + 1, :]
        s = jnp.sum(qt * kt, 0, keepdims=True) * wa_r
        cq = jnp.zeros((d, tb), F32)
        for b in range(tb):
            cq = jnp.where(lane == b, jnp.sum(c_ref[b, h] * qh[b:b + 1, :], -1, keepdims=True), cq)
        num = s * vt + wi_r * cq
        qn = s + wi_r * jnp.sum(nt * qt, 0, keepdims=True)
        hh = num / jnp.maximum(jnp.abs(qn), fl_r)
        mu = jnp.mean(hh, 0, keepdims=True)
        xc = hh - mu
        var = jnp.mean(xc * xc, 0, keepdims=True)
        hn_ref[:, sl] = (xc * lax.rsqrt(var + LN_EPS)).T
        wav = wa_r * vt
        for b in range(tb):
            co_ref[b, h] = wi_r[:, b:b + 1] * c_ref[b, h] + wav[:, b:b + 1] * kh[b:b + 1, :]
        no_ref[:, h, :] = wi[:, h:h + 1] * nh + wa[:, h:h + 1] * kh


def _mlstm_step(q, k, v, gi, gf, bi, bf, c_all, n_all, m0, layer, tb):
    n, w = q.shape
    d = MLSTM_HEAD_DIM
    row = lambda c: pl.BlockSpec((tb, c), lambda i: (i, 0))
    vec = lambda a: pl.BlockSpec(a.shape, lambda i: (0, 0))
    cin = pl.BlockSpec((None, tb, MLSTM_HEADS, d, d), lambda i: (layer, i, 0, 0, 0))
    nin = pl.BlockSpec((None, tb, MLSTM_HEADS, d), lambda i: (layer, i, 0, 0))
    cout = pl.BlockSpec((tb, MLSTM_HEADS, d, d), lambda i: (i, 0, 0, 0))
    nout = pl.BlockSpec((tb, MLSTM_HEADS, d), lambda i: (i, 0, 0))
    out_shape = (
        jax.ShapeDtypeStruct((n, w), F32),
        jax.ShapeDtypeStruct(c_all.shape[1:], F32),
        jax.ShapeDtypeStruct(n_all.shape[1:], F32),
        jax.ShapeDtypeStruct((n, LANES), F32),
    )
    return pl.pallas_call(
        _mlstm_step_kernel, grid=(n // tb,),
        in_specs=[row(w), row(w), row(w), row(LANES), row(LANES), vec(bi), vec(bf), cin, nin, row(LANES)],
        out_specs=(row(w), cout, nout, row(LANES)), out_shape=out_shape,
        compiler_params=_params("parallel"), name="mlstm_step")(q, k, v, gi, gf, bi, bf, c_all, n_all, m0)


def _mixout_kernel(po_ref, hn_ref, og_ref, x_ref, ng_ref, w_ref, g_ref, b_ref, o_ref):
    mo = jax.nn.sigmoid(og_ref[...]) * hn_ref[...] * ng_ref[...]
    mix = _dot(po_ref[...], w_ref[0:POOL_WIDTH, :]) + _dot(mo.astype(BF16), w_ref[POOL_WIDTH:, :])
    o_ref[...] = _layer_norm(ALPHA * x_ref[...] + mix, g_ref[...], b_ref[...])


def _mixout(po, hn, og, x, ng, w, g, b, tm):
    n, d = x.shape
    row = lambda c: pl.BlockSpec((tm, c), lambda i: (i, 0))
    vec = lambda a: pl.BlockSpec(a.shape, lambda i: (0, 0))
    return pl.pallas_call(
        _mixout_kernel, grid=(n // tm,),
        in_specs=[row(POOL_WIDTH), row(MLSTM_WIDTH), row(MLSTM_WIDTH), row(d), vec(ng), vec(w), vec(g), vec(b)],
        out_specs=row(d), out_shape=jax.ShapeDtypeStruct((n, d), F32),
        compiler_params=_params("parallel"), name="mixout")(po, hn, og, x, ng, w, g, b)


def _mm_kernel(x_ref, w_ref, o_ref, *, scale):
    y = _dot(x_ref[...].astype(BF16), w_ref[...])
    if scale != 1.0:
        y = y * scale
    o_ref[...] = y.astype(o_ref.dtype)


def _mm(x, w, tm, out_dtype, scale=1.0, name="proj"):
    n, d = x.shape
    dout = w.shape[1]
    return pl.pallas_call(
        functools.partial(_mm_kernel, scale=scale), grid=(n // tm,),
        in_specs=[pl.BlockSpec((tm, d), lambda i: (i, 0)), pl.BlockSpec(w.shape, lambda i: (0, 0))],
        out_specs=pl.BlockSpec((tm, dout), lambda i: (i, 0)),
        out_shape=jax.ShapeDtypeStruct((n, dout), out_dtype),
        compiler_params=_params("parallel"), name=name)(x, w)


def _cast_kernel(w_ref, o_ref):
    o_ref[...] = w_ref[...].astype(o_ref.dtype)


def _cast_layer(w_all, layer, cols, tr=256):
    rows = w_all.shape[1]
    return pl.pallas_call(
        _cast_kernel, grid=(rows // tr,),
        in_specs=[pl.BlockSpec((None, tr, cols), lambda i: (layer, i, 0))],
        out_specs=pl.BlockSpec((tr, cols), lambda i: (i, 0)),
        out_shape=jax.ShapeDtypeStruct((rows, cols), BF16),
        compiler_params=_params("parallel"), name="weight_cast")(w_all)


def _mem_proj_kernel(x_ref, wk_ref, wv_ref, k4_ref, v4_ref, k2_ref, v2_ref, wkb, wvb):
    @pl.when(pl.program_id(1) == 0)
    def _():
        wkb[...] = wk_ref[...].astype(BF16)
        wvb[...] = wv_ref[...].astype(BF16)

    xb = x_ref[...].astype(BF16)
    k = _dot(xb, wkb[...])
    v = _dot(xb, wvb[...])
    k2_ref[...] = k
    v2_ref[...] = v
    k4_ref[...] = k.reshape(N_MEM, CA_HEADS, CA_HEAD_DIM)
    v4_ref[...] = v.reshape(N_MEM, CA_HEADS, CA_HEAD_DIM)


def _mem_proj(mem, wk_all, wv_all):
    nb, m, d = mem.shape
    depth = wk_all.shape[0]
    wspec = pl.BlockSpec((None, d, d), lambda l, b: (l, 0, 0))
    o4 = pl.BlockSpec((None, None, m, CA_HEADS, CA_HEAD_DIM), lambda l, b: (l, b, 0, 0, 0))
    o2 = pl.BlockSpec((None, None, m, d), lambda l, b: (l, b, 0, 0))
    s4 = jax.ShapeDtypeStruct((depth, nb, m, CA_HEADS, CA_HEAD_DIM), F32)
    s2 = jax.ShapeDtypeStruct((depth, nb, m, d), F32)
    return pl.pallas_call(
        _mem_proj_kernel, grid=(depth, nb),
        in_specs=[pl.BlockSpec((None, m, d), lambda l, b: (b, 0, 0)), wspec, wspec],
        out_specs=(o4, o4, o2, o2), out_shape=(s4, s4, s2, s2),
        scratch_shapes=[pltpu.VMEM((d, d), BF16), pltpu.VMEM((d, d), BF16)],
        compiler_params=_params("arbitrary", "arbitrary"), name="mem_proj")(mem, wk_all, wv_all)


def _attn_block_kernel(po_ref, hn_ref, og_ref, x_ref, k_ref, v_ref, ng_ref, wm_ref, g1_ref, b1_ref,
                       wq_ref, wo_ref, g_ref, b_ref, o_ref):
    mo = jax.nn.sigmoid(og_ref[...]) * hn_ref[...] * ng_ref[...]
    mix = _dot(po_ref[...], wm_ref[0:POOL_WIDTH, :]) + _dot(mo.astype(BF16), wm_ref[POOL_WIDTH:, :])
    x = _layer_norm(ALPHA * x_ref[...] + mix, g1_ref[...], b1_ref[...])
    qb = (_dot(x.astype(BF16), wq_ref[...]) * (CA_HEAD_DIM ** -0.5)).astype(BF16)
    kb = k_ref[...].astype(BF16)
    vb = v_ref[...].astype(BF16)
    ctx = []
    for h in range(CA_HEADS):
        sl = slice(h * CA_HEAD_DIM, (h + 1) * CA_HEAD_DIM)
        s = _dot_nt(qb[:, sl], kb[:, sl])
        e = jnp.exp(s - jnp.max(s, -1, keepdims=True))
        ctx.append((_dot(e.astype(BF16), vb[:, sl]) / jnp.sum(e, -1, keepdims=True)).astype(BF16))
    y = _dot(jnp.concatenate(ctx, axis=1), wo_ref[...])
    o_ref[...] = _layer_norm(ALPHA * x + y, g_ref[...], b_ref[...])


def _attn_block(po, hn, og, x, k, v, layer, ng, wm, ln1, wq, wo, ln2, seq, tq):
    n, d = x.shape
    per = seq // tq
    row = lambda c: pl.BlockSpec((tq, c), lambda i: (i, 0))
    kspec = pl.BlockSpec((None, None, N_MEM, d), lambda i: (layer, i // per, 0, 0))
    vec = lambda a: pl.BlockSpec(a.shape, lambda i: (0, 0))
    return pl.pallas_call(
        _attn_block_kernel, grid=(n // tq,),
        in_specs=[row(POOL_WIDTH), row(MLSTM_WIDTH), row(MLSTM_WIDTH), row(d), kspec, kspec, vec(ng), vec(wm),
                  vec(ln1[0]), vec(ln1[1]), vec(wq), vec(wo), vec(ln2[0]), vec(ln2[1])],
        out_specs=row(d), out_shape=jax.ShapeDtypeStruct((n, d), F32),
        compiler_params=_params("parallel"), name="attn_block")(po, hn, og, x, k, v, ng, wm, *ln1, wq, wo, *ln2)


def _attn_step_kernel(q_ref, k_ref, v_ref, o_ref):
    bb = q_ref.shape[0]
    rows = N_MEM * CA_HEADS
    lane = lax.broadcasted_iota(jnp.int32, (SUBLANES, rows), 1)
    row = lax.broadcasted_iota(jnp.int32, (SUBLANES, rows), 0)
    own = lax.rem(lane, CA_HEADS) == lax.rem(row, CA_HEADS)
    for j in range(bb):
        kf = k_ref[j].reshape(rows, CA_HEAD_DIM).astype(BF16)
        vf = v_ref[j].reshape(rows, CA_HEAD_DIM).astype(BF16)
        s = jnp.where(own, _dot_nt(q_ref[j].astype(BF16), kf), -jnp.inf)
        e = jnp.exp(s - jnp.max(s, -1, keepdims=True))
        o_ref[j] = _dot(e.astype(BF16), vf) / jnp.sum(e, -1, keepdims=True)


def _attn_step(q, k_all, v_all, layer, bb):
    n = q.shape[0]
    qspec = pl.BlockSpec((bb, SUBLANES, CA_HEAD_DIM), lambda i: (i, 0, 0))
    kspec = pl.BlockSpec((None, bb, N_MEM, CA_HEADS, CA_HEAD_DIM), lambda i: (layer, i, 0, 0, 0))
    return pl.pallas_call(
        _attn_step_kernel, grid=(n // bb,), in_specs=[qspec, kspec, kspec], out_specs=qspec,
        out_shape=jax.ShapeDtypeStruct((n, SUBLANES, CA_HEAD_DIM), F32),
        compiler_params=_params("parallel"), name="attn_step")(q, k_all, v_all)


def _mm_res_ln_kernel(a_ref, x_ref, w_ref, g_ref, b_ref, o_ref):
    y = _dot(a_ref[...].astype(BF16), w_ref[...])
    o_ref[...] = _layer_norm(ALPHA * x_ref[...] + y, g_ref[...], b_ref[...])


def _mm_res_ln(a, x, w, g, b, tm):
    n, d = x.shape
    row = lambda c: pl.BlockSpec((tm, c), lambda i: (i, 0))
    vec = lambda arr: pl.BlockSpec(arr.shape, lambda i: (0, 0))
    return pl.pallas_call(
        _mm_res_ln_kernel, grid=(n // tm,),
        in_specs=[row(a.shape[1]), row(d), vec(w), vec(g), vec(b)], out_specs=row(d),
        out_shape=jax.ShapeDtypeStruct((n, d), F32),
        compiler_params=_params("parallel"), name="proj_res_ln")(a, x, w, g, b)


def _route(lt):
    gl = [lt[g:g + 1, :] for g in range(N_GROUPS)]
    gmax = functools.reduce(jnp.maximum, gl)
    gsum = functools.reduce(jnp.add, [jnp.exp(x - gmax) for x in gl])
    pg_sel = 1.0 / gsum

    def first_max(vals):
        m = functools.reduce(jnp.maximum, vals)
        taken = jnp.zeros_like(m, dtype=jnp.bool_)
        hot = []
        for x in vals:
            h = jnp.logical_and(x == m, jnp.logical_not(taken))
            taken = jnp.logical_or(taken, h)
            hot.append(h)
        return m, hot

    _, g_hot = first_max(gl)
    el = []
    for j in range(EXPERTS_PER_GROUP):
        rows = [lt[SUBLANES + g * EXPERTS_PER_GROUP + j:SUBLANES + g * EXPERTS_PER_GROUP + j + 1, :]
                for g in range(N_GROUPS)]
        x = rows[N_GROUPS - 1]
        for g in range(N_GROUPS - 2, -1, -1):
            x = jnp.where(g_hot[g], rows[g], x)
        el.append(x)
    emax = functools.reduce(jnp.maximum, el)
    ee = [jnp.exp(x - emax) for x in el]
    esum = functools.reduce(jnp.add, ee)
    pe = [x / esum for x in ee]
    p1, hot1 = first_max(pe)
    p2, hot2 = first_max([jnp.where(h, -jnp.inf, x) for h, x in zip(hot1, pe)])
    psum = p1 + p2
    gate = [jnp.where(h1, pg_sel * p1 / psum, jnp.where(h2, pg_sel * p2 / psum, 0.0)) for h1, h2 in zip(hot1, hot2)]
    return g_hot, gate


def _expert_cast_kernel(wg_ref, wu_ref, wd_ref, og_ref, ou_ref, od_ref):
    f = D_EXPERT
    for j in range(EXPERTS_PER_GROUP):
        og_ref[:, j * f:(j + 1) * f] = wg_ref[j].astype(BF16)
        ou_ref[:, j * f:(j + 1) * f] = wu_ref[j].astype(BF16)
        od_ref[j * f:(j + 1) * f, :] = wd_ref[j].astype(BF16)


def _expert_cast(wg, wu, wd, layer):
    d = wg.shape[2]
    e, gf = EXPERTS_PER_GROUP, EXPERTS_PER_GROUP * D_EXPERT
    cols = pl.BlockSpec((None, e, d, D_EXPERT), lambda g: (layer, g, 0, 0))
    rows = pl.BlockSpec((None, e, D_EXPERT, d), lambda g: (layer, g, 0, 0))
    return pl.pallas_call(
        _expert_cast_kernel, grid=(N_GROUPS,), in_specs=[cols, cols, rows],
        out_specs=(pl.BlockSpec((None, d, gf), lambda g: (g, 0, 0)), pl.BlockSpec((None, d, gf), lambda g: (g, 0, 0)),
                   pl.BlockSpec((None, gf, d), lambda g: (g, 0, 0))),
        out_shape=(jax.ShapeDtypeStruct((N_GROUPS, d, gf), BF16), jax.ShapeDtypeStruct((N_GROUPS, d, gf), BF16),
                   jax.ShapeDtypeStruct((N_GROUPS, gf, d), BF16)),
        compiler_params=_params("parallel"), name="expert_cast")(wg, wu, wd)


def _moe_kernel(x_ref, wr_ref, br_ref, tri_ref, wg_ref, wu_ref, wd_ref, g_ref, b_ref, o_ref,
                xb_ref, ct_ref, comb_ref, perm_ref, acc_ref, full_ref):
    grp = pl.program_id(1)
    tm = x_ref.shape[0]
    cap = perm_ref.shape[1]
    f = D_EXPERT

    @pl.when(grp == 0)
    def _():
        x = x_ref[...]
        xh = x.astype(BF16)
        xb_ref[...] = xh
        xl = (x - xh.astype(F32)).astype(BF16)
        wr = wr_ref[...]
        wh = wr.astype(BF16)
        wl = (wr - wh.astype(F32)).astype(BF16)
        lt = _dot_nt(wh, xh) + (_dot_nt(wl, xh) + _dot_nt(wh, xl)) + br_ref[...]
        g_hot, gate = _route(lt)
        row8 = lax.broadcasted_iota(jnp.int32, (SUBLANES, tm), 0)
        hot8 = jnp.zeros((SUBLANES, tm), F32)
        for g in range(N_GROUPS):
            hot8 = jnp.where(jnp.logical_and(row8 == g, g_hot[g]), 1.0, hot8)
        cum = _dot(hot8.astype(BF16), tri_ref[...])
        full_ref[0] = (jnp.max(cum[:, tm - 1:tm]) > cap).astype(jnp.int32)
        base = row8.astype(F32) * cap
        slot = jnp.sum(hot8 * (base + cum - 1.0), 0, keepdims=True).astype(jnp.int32)
        for g in range(N_GROUPS):
            r = lax.broadcasted_iota(jnp.int32, (cap, tm), 0) + g * cap
            perm_ref[g] = jnp.where(r == slot, 1.0, 0.0).astype(BF16)
        ct_ref[...] = jnp.zeros_like(ct_ref)
        for j in range(EXPERTS_PER_GROUP):
            ct_ref[j:j + 1, :] = gate[j]
        ct_ref[EXPERTS_PER_GROUP:EXPERTS_PER_GROUP + 1, :] = jnp.sum(hot8 * row8.astype(F32), 0, keepdims=True)
        comb_ref[...] = ct_ref[...].T
        acc_ref[...] = jnp.zeros_like(acc_ref)

    def experts(rows, gates):
        hh = []
        for j in range(EXPERTS_PER_GROUP):
            hg = _dot(rows, wg_ref[:, j * f:(j + 1) * f])
            hu = _dot(rows, wu_ref[:, j * f:(j + 1) * f])
            hh.append((hg * jax.nn.sigmoid(hg) * hu * gates[:, j:j + 1]).astype(BF16))
        return _dot(jnp.concatenate(hh, axis=1), wd_ref[...])

    @pl.when(full_ref[0] == 0)
    def _():
        p = perm_ref[grp]
        comb = comb_ref[...]
        ch = comb.astype(BF16)
        cl = (comb - ch.astype(F32)).astype(BF16)
        rows = _dot(p, xb_ref[...]).astype(BF16)
        gates = _dot(p, ch) + _dot(p, cl)
        acc_ref[...] += _dot_tn(p, experts(rows, gates).astype(BF16))

    @pl.when(full_ref[0] != 0)
    def _():
        comb = comb_ref[...]
        own = comb[:, EXPERTS_PER_GROUP:EXPERTS_PER_GROUP + 1] == grp.astype(F32)
        acc_ref[...] += experts(xb_ref[...], jnp.where(own, comb, 0.0))

    @pl.when(grp == N_GROUPS - 1)
    def _():
        o_ref[...] = _layer_norm(ALPHA * x_ref[...] + acc_ref[...], g_ref[...], b_ref[...])


def _moe(x, wr, br, wg, wu, wd, g, b, tm):
    n, d = x.shape
    cap = min(tm, MOE_CAP)
    tri = (lax.broadcasted_iota(jnp.int32, (tm, tm), 0) <= lax.broadcasted_iota(jnp.int32, (tm, tm), 1)).astype(BF16)
    row = pl.BlockSpec((tm, d), lambda i, e: (i, 0))
    vec = lambda a: pl.BlockSpec(a.shape, lambda i, e: (0, 0))
    grp = lambda a: pl.BlockSpec((None,) + a.shape[1:], lambda i, e: (e, 0, 0))
    scratch = [pltpu.VMEM((tm, d), BF16), pltpu.VMEM((LANES, tm), F32), pltpu.VMEM((tm, LANES), F32),
               pltpu.VMEM((N_GROUPS, cap, tm), BF16), pltpu.VMEM((tm, d), F32), pltpu.SMEM((1,), jnp.int32)]
    return pl.pallas_call(
        _moe_kernel, grid=(n // tm, N_GROUPS),
        in_specs=[row, vec(wr), vec(br), vec(tri), grp(wg), grp(wu), grp(wd), vec(g), vec(b)],
        out_specs=row, out_shape=jax.ShapeDtypeStruct((n, d), F32), scratch_shapes=scratch,
        compiler_params=_params("parallel", "arbitrary"), name="moe")(x, wr, br, tri, wg, wu, wd, g, b)


def _pad_lanes(v):
    return jnp.zeros((1, LANES), F32).at[0, :v.shape[0]].set(v)


def _pad_rows(v):
    return jnp.zeros((SUBLANES, 1), F32).at[:v.shape[0], 0].set(v)


def _layer_weights(l, w_in, b_i, b_f, pool_w, pool_scale, mlstm_norm_g, w_out, ln1_g, ln1_b, ca_wq, ca_wo,
                   ln2_g, ln2_b, w_gr, b_gr, w_er, b_er, w_gate, w_up, w_down, ln3_g, ln3_b):
    d = D_MODEL
    wi = w_in[l]
    w_gate_cols = wi[:, GATE_OFF:]
    wg = jnp.zeros((d, 2 * LANES), F32)
    wg = wg.at[:, 0:MLSTM_HEADS].set(w_gate_cols[:, :MLSTM_HEADS])
    wg = wg.at[:, LANES:LANES + MLSTM_HEADS].set(w_gate_cols[:, MLSTM_HEADS:])
    wgt = jnp.zeros((2 * SUBLANES, d), F32)
    wgt = wgt.at[0:MLSTM_HEADS].set(w_gate_cols[:, :MLSTM_HEADS].T)
    wgt = wgt.at[SUBLANES:SUBLANES + MLSTM_HEADS].set(w_gate_cols[:, MLSTM_HEADS:].T)
    wr = jnp.zeros((ROUTER_ROWS, d), F32)
    wr = wr.at[0:N_GROUPS].set(w_gr[l].T).at[SUBLANES:SUBLANES + N_EXPERTS].set(w_er[l].T)
    br = jnp.zeros((ROUTER_ROWS, 1), F32)
    br = br.at[0:N_GROUPS, 0].set(b_gr[l]).at[SUBLANES:SUBLANES + N_EXPERTS, 0].set(b_er[l])
    row = lambda v: v.reshape(1, -1)
    return dict(
        w_main=_cast_layer(w_in, l, GATE_OFF), wg=wg.astype(BF16), wgt=wgt.astype(BF16),
        bi=_pad_lanes(b_i[l]), bf=_pad_lanes(b_f[l]), bit=_pad_rows(b_i[l]), bft=_pad_rows(b_f[l]),
        pool_w=pool_w[l].astype(BF16), pool_scale=pool_scale[l], norm_g=row(mlstm_norm_g[l]),
        w_out=_cast_layer(w_out, l, d), ln1=(row(ln1_g[l]), row(ln1_b[l])),
        wq=_cast_layer(ca_wq, l, d), wo=_cast_layer(ca_wo, l, d), ln2=(row(ln2_g[l]), row(ln2_b[l])),
        wr=wr, br=br, experts=_expert_cast(w_gate, w_up, w_down, l), ln3=(row(ln3_g[l]), row(ln3_b[l])))


def _trunk_layer(x, nb, seq, layer, mem, pool_prev, state, p, tm, tm_moe):
    chunked = state is None
    u, q, k, v, og, gi, gf, git, gft = _inproj(x, p["w_main"], p["wg"], p["wgt"], seq, tm,
                                               BF16 if chunked else F32)
    u3 = u.reshape(nb, seq, POOL_WIDTH)
    ext = u3 if pool_prev is None else jnp.concatenate([pool_prev, u3], axis=1)
    period = ext.shape[1]
    pooled = _pool(ext.reshape(nb * period, POOL_WIDTH), p["pool_w"], p["pool_scale"], period)
    pooled = pooled.reshape(nb, period, POOL_WIDTH)[:, period - seq:].reshape(nb * seq, POOL_WIDTH)
    pool_buf = ext[:, -POOL_BUF:]
    if chunked:
        r3 = lambda a: a.reshape(nb, seq, a.shape[-1])
        hn, c1, n1, m1 = _mlstm_chunk(r3(q), r3(k), r3(v), git, gft, p["bit"], p["bft"])
        hn = hn.reshape(nb * seq, MLSTM_WIDTH)
        m1 = m1[:, :MLSTM_HEADS, 0]
    else:
        c_all, n_all, m0 = state
        m0p = jnp.zeros((nb, LANES), F32).at[:, :MLSTM_HEADS].set(m0)
        hn, c1, n1, m1 = _mlstm_step(q, k, v, gi, gf, p["bi"], p["bf"], c_all, n_all, m0p, layer, SUBLANES)
        m1 = m1[:, :MLSTM_HEADS]
    if chunked:
        x = _attn_block(pooled, hn, og, x, mem[0], mem[1], layer, p["norm_g"], p["w_out"], p["ln1"], p["wq"],
                        p["wo"], p["ln2"], seq, tm)
    else:
        x = _mixout(pooled, hn, og, x, p["norm_g"], p["w_out"], *p["ln1"], tm)
        qc = _mm(x, p["wq"], tm, F32, scale=CA_HEAD_DIM ** -0.5, name="ca_q")
        qh = jnp.zeros((nb, SUBLANES, CA_HEAD_DIM), F32).at[:, :CA_HEADS].set(qc.reshape(nb, CA_HEADS, CA_HEAD_DIM))
        ctx = _attn_step(qh, mem[0], mem[1], layer, 4)[:, :CA_HEADS].reshape(nb, D_MODEL)
        x = _mm_res_ln(ctx, x, p["wo"], *p["ln2"], tm)
    x = _moe(x, p["wr"], p["br"], *p["experts"], *p["ln3"], tm_moe)
    return x, pool_buf, c1, n1, m1


def kernel(x_prompt, x_sample, mem_prompt, cache_pool, state_mlstm_C, state_mlstm_n, state_mlstm_m,
           cache_mem_k, cache_mem_v, emb_ln_g, emb_ln_b, w_in, b_i, b_f, pool_w, pool_scale,
           mlstm_norm_g, w_out, ln1_g, ln1_b, ca_wq, ca_wk, ca_wv, ca_wo, ln2_g, ln2_b,
           w_gr, b_gr, w_er, b_er, w_gate, w_up, w_down, ln3_g, ln3_b):
    bp, tp, d = x_prompt.shape
    bs, ts, _ = x_sample.shape
    tm_p, tm_s = 512, bs * ts
    xp = _ln(x_prompt.reshape(bp * tp, d), emb_ln_g, emb_ln_b, tm_p)
    xs = _ln(x_sample.reshape(bs * ts, d), emb_ln_g, emb_ln_b, tm_s)
    mk4, mv4, mk2, mv2 = _mem_proj(mem_prompt, ca_wk, ca_wv)
    outs = [[] for _ in range(8)]
    for l in range(DEPTH):
        p = _layer_weights(l, w_in, b_i, b_f, pool_w, pool_scale, mlstm_norm_g, w_out, ln1_g, ln1_b, ca_wq,
                           ca_wo, ln2_g, ln2_b, w_gr, b_gr, w_er, b_er, w_gate, w_up, w_down, ln3_g, ln3_b)
        xp, pb, c1, n1, m1 = _trunk_layer(xp, bp, tp, l, (mk2, mv2), None, None, p, tm_p, 2 * tm_p)
        xs, sb, cs, ns, ms = _trunk_layer(xs, bs, ts, l, (cache_mem_k, cache_mem_v), cache_pool[l],
                                          (state_mlstm_C, state_mlstm_n, state_mlstm_m[l]), p, tm_s, tm_s)
        for lst, val in zip(outs, (pb, c1, n1, m1, sb, cs, ns, ms)):
            lst.append(val)
    pp, pc, pn, pm, sp, sc, sn, sm = (jnp.stack(o) for o in outs)
    return (xp.reshape(bp, tp, d), xs.reshape(bs, ts, d), pp, pc, pn, pm, mk4, mv4, sp, sc, sn, sm)
```

```python
import functools

import jax
import jax.numpy as jnp
from jax import lax
from jax.experimental import pallas as pl
from jax.experimental.pallas import tpu as pltpu

F32 = jnp.float32
BF16 = jnp.bfloat16

D_MODEL = 1024
DEPTH = 4
POOL_WIDTH = 512
POOL_GROUPS = 4
POOL_GROUP_DIM = 128
POOL_WINDOWS = (2, 4, 8, 16)
POOL_BUF = 15
MLSTM_WIDTH = 512
MLSTM_HEADS = 4
MLSTM_HEAD_DIM = 128
N_MEM = 256
CA_HEADS = 4
CA_HEAD_DIM = 256
N_GROUPS = 4
EXPERTS_PER_GROUP = 4
N_EXPERTS = 16
D_EXPERT = 256
ALPHA = (2 * DEPTH) ** 0.25
LN_EPS = 1e-5
GATE_OFF = POOL_WIDTH + 4 * MLSTM_WIDTH

LANES = 128
SUBLANES = 8
VMEM_LIMIT = 56 * 1024 * 1024
MLSTM_CHUNK = 256
POOL_ROWS = 2048
ROUTER_ROWS = 32
MOE_SEGMENT = 512
MOE_CAP = 160


def _params(*sem):
    return pltpu.CompilerParams(dimension_semantics=sem, vmem_limit_bytes=VMEM_LIMIT)


def _dot(a, b):
    return jnp.dot(a, b, preferred_element_type=F32)


def _dot_nt(a, b, precision=None):
    return lax.dot_general(a, b, (((1,), (1,)), ((), ())), precision=precision,
                           preferred_element_type=F32)


def _dot_tn(a, b):
    return lax.dot_general(a, b, (((0,), (0,)), ((), ())), preferred_element_type=F32)


def _layer_norm(x, g, b):
    mu = jnp.mean(x, -1, keepdims=True)
    xc = x - mu
    var = jnp.mean(xc * xc, -1, keepdims=True)
    return xc * lax.rsqrt(var + LN_EPS) * g + b


def _unit_norm(x):
    mu = jnp.mean(x, -1, keepdims=True)
    xc = x - mu
    var = jnp.mean(xc * xc, -1, keepdims=True)
    return xc * lax.rsqrt(var + LN_EPS)


def _log_sigmoid(x):
    return jnp.minimum(x, 0.0) - jnp.log1p(jnp.exp(-jnp.abs(x)))


def _scan(x, axis, op, fill):
    n = x.shape[axis]
    idx = lax.broadcasted_iota(jnp.int32, x.shape, axis)
    s = 1
    while s < n:
        x = op(x, jnp.where(idx >= s, pltpu.roll(x, s, axis), fill))
        s *= 2
    return x


def _ln_kernel(x_ref, g_ref, b_ref, o_ref):
    o_ref[...] = _layer_norm(x_ref[...], g_ref[...], b_ref[...])


def _ln(x, g, b, tm):
    n, d = x.shape
    row = pl.BlockSpec((tm, d), lambda i: (i, 0))
    vec = pl.BlockSpec((1, d), lambda i: (0, 0))
    return pl.pallas_call(
        _ln_kernel, grid=(n // tm,), in_specs=[row, vec, vec], out_specs=row,
        out_shape=jax.ShapeDtypeStruct((n, d), F32), compiler_params=_params("parallel"),
        name="emb_ln")(x, g.reshape(1, d), b.reshape(1, d))


def _inproj_kernel(x_ref, w_ref, wg_ref, wgt_ref, u_ref, q_ref, k_ref, v_ref, og_ref,
                   gi_ref, gf_ref, git_ref, gft_ref):
    xb = x_ref[...].astype(BF16)
    w = MLSTM_WIDTH
    u_ref[...] = _dot(xb, w_ref[:, 0:POOL_WIDTH])
    q = _dot(xb, w_ref[:, POOL_WIDTH:POOL_WIDTH + w]) * (MLSTM_HEAD_DIM ** -0.5)
    q_ref[...] = q.astype(q_ref.dtype)
    k_ref[...] = _dot(xb, w_ref[:, POOL_WIDTH + w:POOL_WIDTH + 2 * w]).astype(k_ref.dtype)
    v_ref[...] = _dot(xb, w_ref[:, POOL_WIDTH + 2 * w:POOL_WIDTH + 3 * w]).astype(v_ref.dtype)
    og_ref[...] = _dot(xb, w_ref[:, POOL_WIDTH + 3 * w:POOL_WIDTH + 4 * w])
    g = _dot(xb, wg_ref[...])
    gi_ref[...] = g[:, 0:LANES]
    gf_ref[...] = g[:, LANES:2 * LANES]
    gt = _dot_nt(wgt_ref[...], xb)
    git_ref[...] = gt[0:SUBLANES]
    gft_ref[...] = gt[SUBLANES:2 * SUBLANES]


def _inproj(x, w_main, wg, wgt, seq, tm, qkv_dtype):
    n, d = x.shape
    nt = n // tm
    if seq % tm == 0:
        per = seq // tm
        gt_shape, gt_map = (n // seq, SUBLANES, seq), (lambda i: (i // per, 0, i % per))
    else:
        gt_shape, gt_map = (1, SUBLANES, n), (lambda i: (0, 0, i))
    row = lambda c: pl.BlockSpec((tm, c), lambda i: (i, 0))
    full = lambda a: pl.BlockSpec(a.shape, lambda i: (0, 0))
    gt_spec = pl.BlockSpec((None, SUBLANES, tm), gt_map)
    out_shape = (
        jax.ShapeDtypeStruct((n, POOL_WIDTH), F32),
        jax.ShapeDtypeStruct((n, MLSTM_WIDTH), qkv_dtype),
        jax.ShapeDtypeStruct((n, MLSTM_WIDTH), qkv_dtype),
        jax.ShapeDtypeStruct((n, MLSTM_WIDTH), qkv_dtype),
        jax.ShapeDtypeStruct((n, MLSTM_WIDTH), F32),
        jax.ShapeDtypeStruct((n, LANES), F32),
        jax.ShapeDtypeStruct((n, LANES), F32),
        jax.ShapeDtypeStruct(gt_shape, F32),
        jax.ShapeDtypeStruct(gt_shape, F32),
    )
    return pl.pallas_call(
        _inproj_kernel, grid=(nt,),
        in_specs=[row(d), full(w_main), full(wg), full(wgt)],
        out_specs=(row(POOL_WIDTH), row(MLSTM_WIDTH), row(MLSTM_WIDTH), row(MLSTM_WIDTH),
                   row(MLSTM_WIDTH), row(LANES), row(LANES), gt_spec, gt_spec),
        out_shape=out_shape, compiler_params=_params("parallel"), name="inproj")(x, w_main, wg, wgt)


def _pool_kernel(ext_ref, pw_ref, sc_ref, o_ref, *, period):
    rows = ext_ref.shape[0]
    r = lax.broadcasted_iota(jnp.int32, (rows, POOL_GROUP_DIM), 0)
    if period != rows:
        r = lax.rem(r, period)
    for g, win in enumerate(POOL_WINDOWS):
        sl = slice(g * POOL_GROUP_DIM, (g + 1) * POOL_GROUP_DIM)
        x = ext_ref[:, sl]
        acc = x
        s = 1
        while s < win:
            acc = acc + jnp.where(r >= s, pltpu.roll(acc, s, 0), 0.0)
            s *= 2
        cnt = jnp.minimum(r + 1, win).astype(F32)
        d = acc / cnt - x
        y = _dot(d.astype(BF16), pw_ref[g]) * sc_ref[:, sl]
        o_ref[:, sl] = y.astype(o_ref.dtype)


def _pool(ext2d, pw, scale, period):
    n, c = ext2d.shape
    row = pl.BlockSpec((POOL_ROWS, c), lambda i: (i, 0))
    return pl.pallas_call(
        functools.partial(_pool_kernel, period=period), grid=(n // POOL_ROWS,),
        in_specs=[row, pl.BlockSpec(pw.shape, lambda i: (0, 0, 0)), pl.BlockSpec((1, c), lambda i: (0, 0))],
        out_specs=row, out_shape=jax.ShapeDtypeStruct((n, c), BF16),
        compiler_params=_params("parallel"), name="pool")(ext2d, pw, scale.reshape(1, c))


def _mlstm_chunk_kernel(q_ref, k_ref, v_ref, git_ref, gft_ref, bit_ref, bft_ref, hn_ref, c_ref, n_ref, m_ref,
                        at_s, ws_s, dc_s, mc_s, wi_s, fl_s):
    nb, L, _ = q_ref.shape
    rows = nb * SUBLANES

    @pl.when(pl.program_id(0) == 0)
    def _():
        c_ref[...] = jnp.zeros_like(c_ref)
        n_ref[...] = jnp.zeros_like(n_ref)
        m_ref[...] = jnp.zeros_like(m_ref)

    ig = git_ref[...].reshape(rows, L) + bit_ref[...]
    lf = _log_sigmoid(gft_ref[...].reshape(rows, L) + bft_ref[...])
    bc = _scan(lf, 1, jnp.add, 0.0)
    at = ig - bc
    m_prev = m_ref[...].reshape(rows, LANES)[:, 0:1]
    mc = jnp.maximum(_scan(at, 1, jnp.maximum, -jnp.inf), m_prev)
    mt = bc + mc
    m_last = mc[:, L - 1:L]
    at_s[...] = at.reshape(nb, SUBLANES, L)
    ws_s[...] = jnp.exp(at - m_last).reshape(nb, SUBLANES, L)
    dc_s[...] = jnp.broadcast_to(jnp.exp(m_prev - m_last), (rows, LANES)).reshape(nb, SUBLANES, LANES)
    m_ref[...] = jnp.broadcast_to(mt[:, L - 1:L], (rows, LANES)).reshape(nb, SUBLANES, LANES)
    wi = jnp.exp(m_prev - mc)
    fl = jnp.exp(-mt)
    for b in range(nb):
        rs = slice(b * SUBLANES, (b + 1) * SUBLANES)
        mc_s[b] = mc[rs].T
        wi_s[b] = wi[rs].T
        fl_s[b] = fl[rs].T

    causal = (lax.broadcasted_iota(jnp.int32, (L, L), 0) >= lax.broadcasted_iota(jnp.int32, (L, L), 1))

    def per_batch(b, carry):
        at = at_s[b]
        ws = ws_s[b]
        decay = dc_s[b]
        mc_c, wi_c, fl_c = mc_s[b], wi_s[b], fl_s[b]
        ws16 = jnp.concatenate([ws, ws], axis=0).astype(BF16)
        for h in range(MLSTM_HEADS):
            sl = slice(h * MLSTM_HEAD_DIM, (h + 1) * MLSTM_HEAD_DIM)
            col = slice(h, h + 1)
            qh = q_ref[b, :, sl]
            kh = k_ref[b, :, sl]
            vh = v_ref[b, :, sl]
            s = _dot_nt(qh, kh)
            p = jnp.where(causal, s * jnp.exp(at[col, :] - mc_c[:, col]), 0.0)
            c_old = c_ref[b, h]
            n_old = n_ref[b, col, :]
            wi = wi_c[:, col]
            dv = MLSTM_HEAD_DIM
            v_ext = jnp.concatenate([vh, jnp.ones_like(vh)], axis=1)
            c_ext = jnp.concatenate([c_old, jnp.broadcast_to(n_old, c_old.shape)], axis=0)
            intra = _dot(p.astype(BF16), v_ext)
            inter = _dot_nt(qh, c_ext.astype(BF16))
            num = intra[:, :dv] + wi * inter[:, :dv]
            qn = intra[:, dv:] + wi * inter[:, dv:]
            hh = num / jnp.maximum(jnp.abs(qn), fl_c[:, col])
            hn_ref[b, :, sl] = _unit_norm(hh)
            dc = decay[col, 0:1]
            vts = (vh.astype(F32).T * ws[col, :]).astype(BF16)
            c_ref[b, h] = dc * c_old + _dot(vts, kh)
            n_ref[b, col, :] = dc * n_old + _dot(ws16, kh)[col, :]
        return carry

    lax.fori_loop(0, nb, per_batch, 0)


def _mlstm_chunk(q, k, v, git, gft, bit, bft):
    nb, t, w = q.shape
    L = MLSTM_CHUNK
    seq = lambda c: pl.BlockSpec((nb, L, c), lambda i: (0, i, 0))
    seqt = pl.BlockSpec((nb, SUBLANES, L), lambda i: (0, 0, i))
    vec = lambda a: pl.BlockSpec(a.shape, lambda i: (0, 0))
    out_shape = (
        jax.ShapeDtypeStruct((nb, t, w), F32),
        jax.ShapeDtypeStruct((nb, MLSTM_HEADS, MLSTM_HEAD_DIM, MLSTM_HEAD_DIM), F32),
        jax.ShapeDtypeStruct((nb, MLSTM_HEADS, MLSTM_HEAD_DIM), F32),
        jax.ShapeDtypeStruct((nb, SUBLANES, LANES), F32),
    )
    out_specs = (
        seq(w),
        pl.BlockSpec(out_shape[1].shape, lambda i: (0, 0, 0, 0)),
        pl.BlockSpec(out_shape[2].shape, lambda i: (0, 0, 0)),
        pl.BlockSpec(out_shape[3].shape, lambda i: (0, 0, 0)),
    )
    bit, bft = jnp.tile(bit, (nb, 1)), jnp.tile(bft, (nb, 1))
    scratch = [pltpu.VMEM((nb, SUBLANES, L), F32), pltpu.VMEM((nb, SUBLANES, L), F32),
               pltpu.VMEM((nb, SUBLANES, LANES), F32)] + [pltpu.VMEM((nb, L, SUBLANES), F32) for _ in range(3)]
    return pl.pallas_call(
        _mlstm_chunk_kernel, grid=(t // L,),
        in_specs=[seq(w), seq(w), seq(w), seqt, seqt, vec(bit), vec(bft)],
        out_specs=out_specs, out_shape=out_shape, scratch_shapes=scratch, compiler_params=_params("arbitrary"),
        name="mlstm_chunk")(q, k, v, git, gft, bit, bft)


def _mlstm_step_kernel(q_ref, k_ref, v_ref, gi_ref, gf_ref, bi_ref, bf_ref, c_ref, n_ref, m_ref,
                       hn_ref, co_ref, no_ref, mo_ref):
    tb = q_ref.shape[0]
    d = MLSTM_HEAD_DIM
    ig = gi_ref[...] + bi_ref[...]
    lf = _log_sigmoid(gf_ref[...] + bf_ref[...])
    m_old = m_ref[...]
    mt = jnp.maximum(lf + m_old, ig)
    mo_ref[...] = mt
    wa = jnp.exp(ig - mt)
    wi = jnp.exp(lf + m_old - mt)
    fl = jnp.exp(-mt)
    wa_t, wi_t, fl_t = wa.T, wi.T, fl.T
    lane = lax.broadcasted_iota(jnp.int32, (d, tb), 1)
    for h in range(MLSTM_HEADS):
        sl = slice(h * d, (h + 1) * d)
        qh, kh, vh = q_ref[:, sl], k_ref[:, sl], v_ref[:, sl]
        nh = n_ref[:, h, :]
        qt, kt, vt, nt = qh.T, kh.T, vh.T, nh.T
        wa_r, wi_r, fl_r = wa_t[h:h + 1, :], wi_t[h:h + 1, :], fl_t[h:h + 1, :]
        s = jnp.sum(qt * kt, 0, keepdims=True) * wa_r
        cq = jnp.zeros((d, tb), F32)
        for b in range(tb):
            cq = jnp.where(lane == b, jnp.sum(c_ref[b, h] * qh[b:b + 1, :], -1, keepdims=True), cq)
        num = s * vt + wi_r * cq
        qn = s + wi_r * jnp.sum(nt * qt, 0, keepdims=True)
        hh = num / jnp.maximum(jnp.abs(qn), fl_r)
        mu = jnp.mean(hh, 0, keepdims=True)
        xc = hh - mu
        var = jnp.mean(xc * xc, 0, keepdims=True)
        hn_ref[:, sl] = (xc * lax.rsqrt(var + LN_EPS)).T
        wav = wa_r * vt
        for b in range(tb):
            co_ref[b, h] = wi_r[:, b:b + 1] * c_ref[b, h] + wav[:, b:b + 1] * kh[b:b + 1, :]
        no_ref[:, h, :] = wi[:, h:h + 1] * nh + wa[:, h:h + 1] * kh


def _mlstm_step(q, k, v, gi, gf, bi, bf, c_all, n_all, m0, layer, tb):
    n, w = q.shape
    d = MLSTM_HEAD_DIM
    row = lambda c: pl.BlockSpec((tb, c), lambda i: (i, 0))
    vec = lambda a: pl.BlockSpec(a.shape, lambda i: (0, 0))
    cin = pl.BlockSpec((None, tb, MLSTM_HEADS, d, d), lambda i: (layer, i, 0, 0, 0))
    nin = pl.BlockSpec((None, tb, MLSTM_HEADS, d), lambda i: (layer, i, 0, 0))
    cout = pl.BlockSpec((tb, MLSTM_HEADS, d, d), lambda i: (i, 0, 0, 0))
    nout = pl.BlockSpec((tb, MLSTM_HEADS, d), lambda i: (i, 0, 0))
    out_shape = (
        jax.ShapeDtypeStruct((n, w), F32),
        jax.ShapeDtypeStruct(c_all.shape[1:], F32),
        jax.ShapeDtypeStruct(n_all.shape[1:], F32),
        jax.ShapeDtypeStruct((n, LANES), F32),
    )
    return pl.pallas_call(
        _mlstm_step_kernel, grid=(n // tb,),
        in_specs=[row(w), row(w), row(w), row(LANES), row(LANES), vec(bi), vec(bf), cin, nin, row(LANES)],
        out_specs=(row(w), cout, nout, row(LANES)), out_shape=out_shape,
        compiler_params=_params("parallel"), name="mlstm_step")(q, k, v, gi, gf, bi, bf, c_all, n_all, m0)


def _mixout_kernel(po_ref, hn_ref, og_ref, x_ref, ng_ref, w_ref, g_ref, b_ref, o_ref):
    mo = jax.nn.sigmoid(og_ref[...]) * hn_ref[...] * ng_ref[...]
    mix = _dot(po_ref[...], w_ref[0:POOL_WIDTH, :]) + _dot(mo.astype(BF16), w_ref[POOL_WIDTH:, :])
    o_ref[...] = _layer_norm(ALPHA * x_ref[...] + mix, g_ref[...], b_ref[...])


def _mixout(po, hn, og, x, ng, w, g, b, tm):
    n, d = x.shape
    row = lambda c: pl.BlockSpec((tm, c), lambda i: (i, 0))
    vec = lambda a: pl.BlockSpec(a.shape, lambda i: (0, 0))
    return pl.pallas_call(
        _mixout_kernel, grid=(n // tm,),
        in_specs=[row(POOL_WIDTH), row(MLSTM_WIDTH), row(MLSTM_WIDTH), row(d), vec(ng), vec(w), vec(g), vec(b)],
        out_specs=row(d), out_shape=jax.ShapeDtypeStruct((n, d), F32),
        compiler_params=_params("parallel"), name="mixout")(po, hn, og, x, ng, w, g, b)


def _mm_kernel(x_ref, w_ref, o_ref, *, scale):
    y = _dot(x_ref[...].astype(BF16), w_ref[...])
    if scale != 1.0:
        y = y * scale
    o_ref[...] = y.astype(o_ref.dtype)


def _mm(x, w, tm, out_dtype, scale=1.0, name="proj"):
    n, d = x.shape
    dout = w.shape[1]
    return pl.pallas_call(
        functools.partial(_mm_kernel, scale=scale), grid=(n // tm,),
        in_specs=[pl.BlockSpec((tm, d), lambda i: (i, 0)), pl.BlockSpec(w.shape, lambda i: (0, 0))],
        out_specs=pl.BlockSpec((tm, dout), lambda i: (i, 0)),
        out_shape=jax.ShapeDtypeStruct((n, dout), out_dtype),
        compiler_params=_params("parallel"), name=name)(x, w)


def _cast_kernel(w_ref, o_ref):
    o_ref[...] = w_ref[...].astype(o_ref.dtype)


def _cast_layer(w_all, layer, cols, tr=256):
    rows = w_all.shape[1]
    return pl.pallas_call(
        _cast_kernel, grid=(rows // tr,),
        in_specs=[pl.BlockSpec((None, tr, cols), lambda i: (layer, i, 0))],
        out_specs=pl.BlockSpec((tr, cols), lambda i: (i, 0)),
        out_shape=jax.ShapeDtypeStruct((rows, cols), BF16),
        compiler_params=_params("parallel"), name="weight_cast")(w_all)


def _mem_proj_kernel(x_ref, wk_ref, wv_ref, k4_ref, v4_ref, k2_ref, v2_ref, wkb, wvb):
    @pl.when(pl.program_id(1) == 0)
    def _():
        wkb[...] = wk_ref[...].astype(BF16)
        wvb[...] = wv_ref[...].astype(BF16)

    xb = x_ref[...].astype(BF16)
    k = _dot(xb, wkb[...])
    v = _dot(xb, wvb[...])
    k2_ref[...] = k
    v2_ref[...] = v
    k4_ref[...] = k.reshape(N_MEM, CA_HEADS, CA_HEAD_DIM)
    v4_ref[...] = v.reshape(N_MEM, CA_HEADS, CA_HEAD_DIM)


def _mem_proj(mem, wk_all, wv_all):
    nb, m, d = mem.shape
    depth = wk_all.shape[0]
    wspec = pl.BlockSpec((None, d, d), lambda l, b: (l, 0, 0))
    o4 = pl.BlockSpec((None, None, m, CA_HEADS, CA_HEAD_DIM), lambda l, b: (l, b, 0, 0, 0))
    o2 = pl.BlockSpec((None, None, m, d), lambda l, b: (l, b, 0, 0))
    s4 = jax.ShapeDtypeStruct((depth, nb, m, CA_HEADS, CA_HEAD_DIM), F32)
    s2 = jax.ShapeDtypeStruct((depth, nb, m, d), F32)
    return pl.pallas_call(
        _mem_proj_kernel, grid=(depth, nb),
        in_specs=[pl.BlockSpec((None, m, d), lambda l, b: (b, 0, 0)), wspec, wspec],
        out_specs=(o4, o4, o2, o2), out_shape=(s4, s4, s2, s2),
        scratch_shapes=[pltpu.VMEM((d, d), BF16), pltpu.VMEM((d, d), BF16)],
        compiler_params=_params("arbitrary", "arbitrary"), name="mem_proj")(mem, wk_all, wv_all)


def _attn_block_kernel(po_ref, hn_ref, og_ref, x_ref, k_ref, v_ref, ng_ref, wm_ref, g1_ref, b1_ref,
                       wq_ref, wo_ref, g_ref, b_ref, o_ref):
    mo = jax.nn.sigmoid(og_ref[...]) * hn_ref[...] * ng_ref[...]
    mix = _dot(po_ref[...], wm_ref[0:POOL_WIDTH, :]) + _dot(mo.astype(BF16), wm_ref[POOL_WIDTH:, :])
    x = _layer_norm(ALPHA * x_ref[...] + mix, g1_ref[...], b1_ref[...])
    qb = (_dot(x.astype(BF16), wq_ref[...]) * (CA_HEAD_DIM ** -0.5)).astype(BF16)
    kb = k_ref[...].astype(BF16)
    vb = v_ref[...].astype(BF16)
    ctx = []
    for h in range(CA_HEADS):
        sl = slice(h * CA_HEAD_DIM, (h + 1) * CA_HEAD_DIM)
        s = _dot_nt(qb[:, sl], kb[:, sl])
        e = jnp.exp(s - jnp.max(s, -1, keepdims=True))
        ctx.append((_dot(e.astype(BF16), vb[:, sl]) / jnp.sum(e, -1, keepdims=True)).astype(BF16))
    y = _dot(jnp.concatenate(ctx, axis=1), wo_ref[...])
    o_ref[...] = _layer_norm(ALPHA * x + y, g_ref[...], b_ref[...])


def _attn_block(po, hn, og, x, k, v, layer, ng, wm, ln1, wq, wo, ln2, seq, tq):
    n, d = x.shape
    per = seq // tq
    row = lambda c: pl.BlockSpec((tq, c), lambda i: (i, 0))
    kspec = pl.BlockSpec((None, None, N_MEM, d), lambda i: (layer, i // per, 0, 0))
    vec = lambda a: pl.BlockSpec(a.shape, lambda i: (0, 0))
    return pl.pallas_call(
        _attn_block_kernel, grid=(n // tq,),
        in_specs=[row(POOL_WIDTH), row(MLSTM_WIDTH), row(MLSTM_WIDTH), row(d), kspec, kspec, vec(ng), vec(wm),
                  vec(ln1[0]), vec(ln1[1]), vec(wq), vec(wo), vec(ln2[0]), vec(ln2[1])],
        out_specs=row(d), out_shape=jax.ShapeDtypeStruct((n, d), F32),
        compiler_params=_params("parallel"), name="attn_block")(po, hn, og, x, k, v, ng, wm, *ln1, wq, wo, *ln2)


def _attn_step_kernel(q_ref, k_ref, v_ref, o_ref):
    bb = q_ref.shape[0]
    rows = N_MEM * CA_HEADS
    lane = lax.broadcasted_iota(jnp.int32, (SUBLANES, rows), 1)
    row = lax.broadcasted_iota(jnp.int32, (SUBLANES, rows), 0)
    own = lax.rem(lane, CA_HEADS) == lax.rem(row, CA_HEADS)
    for j in range(bb):
        kf = k_ref[j].reshape(rows, CA_HEAD_DIM).astype(BF16)
        vf = v_ref[j].reshape(rows, CA_HEAD_DIM).astype(BF16)
        s = jnp.where(own, _dot_nt(q_ref[j].astype(BF16), kf), -jnp.inf)
        e = jnp.exp(s - jnp.max(s, -1, keepdims=True))
        o_ref[j] = _dot(e.astype(BF16), vf) / jnp.sum(e, -1, keepdims=True)


def _attn_step(q, k_all, v_all, layer, bb):
    n = q.shape[0]
    qspec = pl.BlockSpec((bb, SUBLANES, CA_HEAD_DIM), lambda i: (i, 0, 0))
    kspec = pl.BlockSpec((None, bb, N_MEM, CA_HEADS, CA_HEAD_DIM), lambda i: (layer, i, 0, 0, 0))
    return pl.pallas_call(
        _attn_step_kernel, grid=(n // bb,), in_specs=[qspec, kspec, kspec], out_specs=qspec,
        out_shape=jax.ShapeDtypeStruct((n, SUBLANES, CA_HEAD_DIM), F32),
        compiler_params=_params("parallel"), name="attn_step")(q, k_all, v_all)


def _mm_res_ln_kernel(a_ref, x_ref, w_ref, g_ref, b_ref, o_ref):
    y = _dot(a_ref[...].astype(BF16), w_ref[...])
    o_ref[...] = _layer_norm(ALPHA * x_ref[...] + y, g_ref[...], b_ref[...])


def _mm_res_ln(a, x, w, g, b, tm):
    n, d = x.shape
    row = lambda c: pl.BlockSpec((tm, c), lambda i: (i, 0))
    vec = lambda arr: pl.BlockSpec(arr.shape, lambda i: (0, 0))
    return pl.pallas_call(
        _mm_res_ln_kernel, grid=(n // tm,),
        in_specs=[row(a.shape[1]), row(d), vec(w), vec(g), vec(b)], out_specs=row(d),
        out_shape=jax.ShapeDtypeStruct((n, d), F32),
        compiler_params=_params("parallel"), name="proj_res_ln")(a, x, w, g, b)


def _route(lt):
    gl = [lt[g:g + 1, :] for g in range(N_GROUPS)]
    gmax = functools.reduce(jnp.maximum, gl)
    gsum = functools.reduce(jnp.add, [jnp.exp(x - gmax) for x in gl])
    pg_sel = 1.0 / gsum

    def first_max(vals):
        m = functools.reduce(jnp.maximum, vals)
        taken = jnp.zeros_like(m, dtype=jnp.bool_)
        hot = []
        for x in vals:
            h = jnp.logical_and(x == m, jnp.logical_not(taken))
            taken = jnp.logical_or(taken, h)
            hot.append(h)
        return m, hot

    _, g_hot = first_max(gl)
    el = []
    for j in range(EXPERTS_PER_GROUP):
        rows = [lt[SUBLANES + g * EXPERTS_PER_GROUP + j:SUBLANES + g * EXPERTS_PER_GROUP + j + 1, :]
                for g in range(N_GROUPS)]
        x = rows[N_GROUPS - 1]
        for g in range(N_GROUPS - 2, -1, -1):
            x = jnp.where(g_hot[g], rows[g], x)
        el.append(x)
    emax = functools.reduce(jnp.maximum, el)
    ee = [jnp.exp(x - emax) for x in el]
    esum = functools.reduce(jnp.add, ee)
    pe = [x / esum for x in ee]
    p1, hot1 = first_max(pe)
    p2, hot2 = first_max([jnp.where(h, -jnp.inf, x) for h, x in zip(hot1, pe)])
    psum = p1 + p2
    gate = [jnp.where(h1, pg_sel * p1 / psum, jnp.where(h2, pg_sel * p2 / psum, 0.0)) for h1, h2 in zip(hot1, hot2)]
    return g_hot, gate


def _expert_cast_kernel(wg_ref, wu_ref, wd_ref, og_ref, ou_ref, od_ref):
    f = D_EXPERT
    for j in range(EXPERTS_PER_GROUP):
        og_ref[:, j * f:(j + 1) * f] = wg_ref[j].astype(BF16)
        ou_ref[:, j * f:(j + 1) * f] = wu_ref[j].astype(BF16)
        od_ref[j * f:(j + 1) * f, :] = wd_ref[j].astype(BF16)


def _expert_cast(wg, wu, wd, layer):
    d = wg.shape[2]
    e, gf = EXPERTS_PER_GROUP, EXPERTS_PER_GROUP * D_EXPERT
    cols = pl.BlockSpec((None, e, d, D_EXPERT), lambda g: (layer, g, 0, 0))
    rows = pl.BlockSpec((None, e, D_EXPERT, d), lambda g: (layer, g, 0, 0))
    return pl.pallas_call(
        _expert_cast_kernel, grid=(N_GROUPS,), in_specs=[cols, cols, rows],
        out_specs=(pl.BlockSpec((None, d, gf), lambda g: (g, 0, 0)), pl.BlockSpec((None, d, gf), lambda g: (g, 0, 0)),
                   pl.BlockSpec((None, gf, d), lambda g: (g, 0, 0))),
        out_shape=(jax.ShapeDtypeStruct((N_GROUPS, d, gf), BF16), jax.ShapeDtypeStruct((N_GROUPS, d, gf), BF16),
                   jax.ShapeDtypeStruct((N_GROUPS, gf, d), BF16)),
        compiler_params=_params("parallel"), name="expert_cast")(wg, wu, wd)


def _moe_kernel(x_ref, wr_ref, br_ref, tri_ref, wg_ref, wu_ref, wd_ref, g_ref, b_ref, o_ref,
                xb_ref, ct_ref, comb_ref, perm_ref, acc_ref, full_ref):
    grp = pl.program_id(1)
    tm = x_ref.shape[0]
    ns, cap, sw = perm_ref.shape[1:]
    f = D_EXPERT

    @pl.when(grp == 0)
    def _():
        x = x_ref[...]
        xh = x.astype(BF16)
        xb_ref[...] = xh
        xl = (x - xh.astype(F32)).astype(BF16)
        wr = wr_ref[...]
        wh = wr.astype(BF16)
        wl = (wr - wh.astype(F32)).astype(BF16)
        lt = _dot_nt(wh, xh) + (_dot_nt(wl, xh) + _dot_nt(wh, xl)) + br_ref[...]
        g_hot, gate = _route(lt)
        row8 = lax.broadcasted_iota(jnp.int32, (SUBLANES, tm), 0)
        hot8 = jnp.zeros((SUBLANES, tm), F32)
        for g in range(N_GROUPS):
            hot8 = jnp.where(jnp.logical_and(row8 == g, g_hot[g]), 1.0, hot8)
        most = None
        for s in range(ns):
            hs = hot8[:, s * sw:(s + 1) * sw]
            cum = _dot(hs.astype(BF16), tri_ref[...])
            seg_max = jnp.max(cum[:, sw - 1:sw])
            most = seg_max if most is None else jnp.maximum(most, seg_max)
            base = lax.broadcasted_iota(jnp.int32, (SUBLANES, sw), 0).astype(F32) * cap
            code = jnp.sum(hs * (base + cum - 1.0), 0, keepdims=True).astype(jnp.int32)
            for g in range(N_GROUPS):
                r = lax.broadcasted_iota(jnp.int32, (cap, sw), 0) + g * cap
                perm_ref[g, s] = jnp.where(r == code, 1.0, 0.0).astype(BF16)
        full_ref[0] = (most > cap).astype(jnp.int32)
        ct_ref[...] = jnp.zeros_like(ct_ref)
        for j in range(EXPERTS_PER_GROUP):
            ct_ref[j:j + 1, :] = gate[j]
        ct_ref[EXPERTS_PER_GROUP:EXPERTS_PER_GROUP + 1, :] = jnp.sum(hot8 * row8.astype(F32), 0, keepdims=True)
        comb_ref[...] = ct_ref[...].T
        acc_ref[...] = jnp.zeros_like(acc_ref)

    def experts(rows, gates):
        hh = []
        for j in range(EXPERTS_PER_GROUP):
            hg = _dot(rows, wg_ref[:, j * f:(j + 1) * f])
            hu = _dot(rows, wu_ref[:, j * f:(j + 1) * f])
            hh.append((hg * jax.nn.sigmoid(hg) * hu * gates[:, j:j + 1]).astype(BF16))
        return _dot(jnp.concatenate(hh, axis=1), wd_ref[...])

    @pl.when(full_ref[0] == 0)
    def _():
        comb = comb_ref[...]
        ch = comb.astype(BF16)
        cl = (comb - ch.astype(F32)).astype(BF16)
        xb = xb_ref[...]
        perm = [perm_ref[grp, s] for s in range(ns)]
        seg = lambda a, s: a[s * sw:(s + 1) * sw]
        rows = jnp.concatenate([_dot(perm[s], seg(xb, s)) for s in range(ns)], axis=0).astype(BF16)
        gates = jnp.concatenate([_dot(perm[s], seg(ch, s)) + _dot(perm[s], seg(cl, s)) for s in range(ns)], axis=0)
        y = experts(rows, gates).astype(BF16)
        for s in range(ns):
            acc_ref[s * sw:(s + 1) * sw, :] += _dot_tn(perm[s], y[s * cap:(s + 1) * cap])

    @pl.when(full_ref[0] != 0)
    def _():
        comb = comb_ref[...]
        own = comb[:, EXPERTS_PER_GROUP:EXPERTS_PER_GROUP + 1] == grp.astype(F32)
        acc_ref[...] += experts(xb_ref[...], jnp.where(own, comb, 0.0))

    @pl.when(grp == N_GROUPS - 1)
    def _():
        o_ref[...] = _layer_norm(ALPHA * x_ref[...] + acc_ref[...], g_ref[...], b_ref[...])


def _moe(x, wr, br, wg, wu, wd, g, b, tm):
    n, d = x.shape
    ns = max(1, tm // MOE_SEGMENT)
    sw = tm // ns
    cap = min(sw, MOE_CAP)
    tri = (lax.broadcasted_iota(jnp.int32, (sw, sw), 0) <= lax.broadcasted_iota(jnp.int32, (sw, sw), 1)).astype(BF16)
    row = pl.BlockSpec((tm, d), lambda i, e: (i, 0))
    vec = lambda a: pl.BlockSpec(a.shape, lambda i, e: (0, 0))
    grp = lambda a: pl.BlockSpec((None,) + a.shape[1:], lambda i, e: (e, 0, 0))
    scratch = [pltpu.VMEM((tm, d), BF16), pltpu.VMEM((LANES, tm), F32), pltpu.VMEM((tm, LANES), F32),
               pltpu.VMEM((N_GROUPS, ns, cap, sw), BF16), pltpu.VMEM((tm, d), F32), pltpu.SMEM((1,), jnp.int32)]
    return pl.pallas_call(
        _moe_kernel, grid=(n // tm, N_GROUPS),
        in_specs=[row, vec(wr), vec(br), vec(tri), grp(wg), grp(wu), grp(wd), vec(g), vec(b)],
        out_specs=row, out_shape=jax.ShapeDtypeStruct((n, d), F32), scratch_shapes=scratch,
        compiler_params=_params("parallel", "arbitrary"), name="moe")(x, wr, br, tri, wg, wu, wd, g, b)


def _pad_lanes(v):
    return jnp.zeros((1, LANES), F32).at[0, :v.shape[0]].set(v)


def _pad_rows(v):
    return jnp.zeros((SUBLANES, 1), F32).at[:v.shape[0], 0].set(v)


def _layer_weights(l, w_in, b_i, b_f, pool_w, pool_scale, mlstm_norm_g, w_out, ln1_g, ln1_b, ca_wq, ca_wo,
                   ln2_g, ln2_b, w_gr, b_gr, w_er, b_er, w_gate, w_up, w_down, ln3_g, ln3_b):
    d = D_MODEL
    wi = w_in[l]
    w_gate_cols = wi[:, GATE_OFF:]
    wg = jnp.zeros((d, 2 * LANES), F32)
    wg = wg.at[:, 0:MLSTM_HEADS].set(w_gate_cols[:, :MLSTM_HEADS])
    wg = wg.at[:, LANES:LANES + MLSTM_HEADS].set(w_gate_cols[:, MLSTM_HEADS:])
    wgt = jnp.zeros((2 * SUBLANES, d), F32)
    wgt = wgt.at[0:MLSTM_HEADS].set(w_gate_cols[:, :MLSTM_HEADS].T)
    wgt = wgt.at[SUBLANES:SUBLANES + MLSTM_HEADS].set(w_gate_cols[:, MLSTM_HEADS:].T)
    wr = jnp.zeros((ROUTER_ROWS, d), F32)
    wr = wr.at[0:N_GROUPS].set(w_gr[l].T).at[SUBLANES:SUBLANES + N_EXPERTS].set(w_er[l].T)
    br = jnp.zeros((ROUTER_ROWS, 1), F32)
    br = br.at[0:N_GROUPS, 0].set(b_gr[l]).at[SUBLANES:SUBLANES + N_EXPERTS, 0].set(b_er[l])
    row = lambda v: v.reshape(1, -1)
    return dict(
        w_main=_cast_layer(w_in, l, GATE_OFF), wg=wg.astype(BF16), wgt=wgt.astype(BF16),
        bi=_pad_lanes(b_i[l]), bf=_pad_lanes(b_f[l]), bit=_pad_rows(b_i[l]), bft=_pad_rows(b_f[l]),
        pool_w=pool_w[l].astype(BF16), pool_scale=pool_scale[l], norm_g=row(mlstm_norm_g[l]),
        w_out=_cast_layer(w_out, l, d), ln1=(row(ln1_g[l]), row(ln1_b[l])),
        wq=_cast_layer(ca_wq, l, d), wo=_cast_layer(ca_wo, l, d), ln2=(row(ln2_g[l]), row(ln2_b[l])),
        wr=wr, br=br, experts=_expert_cast(w_gate, w_up, w_down, l), ln3=(row(ln3_g[l]), row(ln3_b[l])))


def _trunk_layer(x, nb, seq, layer, mem, pool_prev, state, p, tm, tm_moe):
    chunked = state is None
    u, q, k, v, og, gi, gf, git, gft = _inproj(x, p["w_main"], p["wg"], p["wgt"], seq, tm,
                                               BF16 if chunked else F32)
    u3 = u.reshape(nb, seq, POOL_WIDTH)
    ext = u3 if pool_prev is None else jnp.concatenate([pool_prev, u3], axis=1)
    period = ext.shape[1]
    pooled = _pool(ext.reshape(nb * period, POOL_WIDTH), p["pool_w"], p["pool_scale"], period)
    pooled = pooled.reshape(nb, period, POOL_WIDTH)[:, period - seq:].reshape(nb * seq, POOL_WIDTH)
    pool_buf = ext[:, -POOL_BUF:]
    if chunked:
        r3 = lambda a: a.reshape(nb, seq, a.shape[-1])
        hn, c1, n1, m1 = _mlstm_chunk(r3(q), r3(k), r3(v), git, gft, p["bit"], p["bft"])
        hn = hn.reshape(nb * seq, MLSTM_WIDTH)
        m1 = m1[:, :MLSTM_HEADS, 0]
    else:
        c_all, n_all, m0 = state
        m0p = jnp.zeros((nb, LANES), F32).at[:, :MLSTM_HEADS].set(m0)
        hn, c1, n1, m1 = _mlstm_step(q, k, v, gi, gf, p["bi"], p["bf"], c_all, n_all, m0p, layer, SUBLANES)
        m1 = m1[:, :MLSTM_HEADS]
    if chunked:
        x = _attn_block(pooled, hn, og, x, mem[0], mem[1], layer, p["norm_g"], p["w_out"], p["ln1"], p["wq"],
                        p["wo"], p["ln2"], seq, tm)
    else:
        x = _mixout(pooled, hn, og, x, p["norm_g"], p["w_out"], *p["ln1"], tm)
        qc = _mm(x, p["wq"], tm, F32, scale=CA_HEAD_DIM ** -0.5, name="ca_q")
        qh = jnp.zeros((nb, SUBLANES, CA_HEAD_DIM), F32).at[:, :CA_HEADS].set(qc.reshape(nb, CA_HEADS, CA_HEAD_DIM))
        ctx = _attn_step(qh, mem[0], mem[1], layer, 4)[:, :CA_HEADS].reshape(nb, D_MODEL)
        x = _mm_res_ln(ctx, x, p["wo"], *p["ln2"], tm)
    x = _moe(x, p["wr"], p["br"], *p["experts"], *p["ln3"], tm_moe)
    return x, pool_buf, c1, n1, m1


def kernel(x_prompt, x_sample, mem_prompt, cache_pool, state_mlstm_C, state_mlstm_n, state_mlstm_m,
           cache_mem_k, cache_mem_v, emb_ln_g, emb_ln_b, w_in, b_i, b_f, pool_w, pool_scale,
           mlstm_norm_g, w_out, ln1_g, ln1_b, ca_wq, ca_wk, ca_wv, ca_wo, ln2_g, ln2_b,
           w_gr, b_gr, w_er, b_er, w_gate, w_up, w_down, ln3_g, ln3_b):
    bp, tp, d = x_prompt.shape
    bs, ts, _ = x_sample.shape
    tm_p, tm_s = 512, bs * ts
    xp = _ln(x_prompt.reshape(bp * tp, d), emb_ln_g, emb_ln_b, tm_p)
    xs = _ln(x_sample.reshape(bs * ts, d), emb_ln_g, emb_ln_b, tm_s)
    mk4, mv4, mk2, mv2 = _mem_proj(mem_prompt, ca_wk, ca_wv)
    outs = [[] for _ in range(8)]
    for l in range(DEPTH):
        p = _layer_weights(l, w_in, b_i, b_f, pool_w, pool_scale, mlstm_norm_g, w_out, ln1_g, ln1_b, ca_wq,
                           ca_wo, ln2_g, ln2_b, w_gr, b_gr, w_er, b_er, w_gate, w_up, w_down, ln3_g, ln3_b)
        xp, pb, c1, n1, m1 = _trunk_layer(xp, bp, tp, l, (mk2, mv2), None, None, p, tm_p, 2 * tm_p)
        xs, sb, cs, ns, ms = _trunk_layer(xs, bs, ts, l, (cache_mem_k, cache_mem_v), cache_pool[l],
                                          (state_mlstm_C, state_mlstm_n, state_mlstm_m[l]), p, tm_s, tm_s)
        for lst, val in zip(outs, (pb, c1, n1, m1, sb, cs, ns, ms)):
            lst.append(val)
    pp, pc, pn, pm, sp, sc, sn, sm = (jnp.stack(o) for o in outs)
    return (xp.reshape(bp, tp, d), xs.reshape(bs, ts, d), pp, pc, pn, pm, mk4, mv4, sp, sc, sn, sm)
```

```python
import functools

import jax
import jax.numpy as jnp
from jax import lax
from jax.experimental import pallas as pl
from jax.experimental.pallas import tpu as pltpu

F32 = jnp.float32
BF16 = jnp.bfloat16

D_MODEL = 1024
DEPTH = 4
POOL_WIDTH = 512
POOL_GROUPS = 4
POOL_GROUP_DIM = 128
POOL_WINDOWS = (2, 4, 8, 16)
POOL_BUF = 15
MLSTM_WIDTH = 512
MLSTM_HEADS = 4
MLSTM_HEAD_DIM = 128
N_MEM = 256
CA_HEADS = 4
CA_HEAD_DIM = 256
N_GROUPS = 4
EXPERTS_PER_GROUP = 4
N_EXPERTS = 16
D_EXPERT = 256
ALPHA = (2 * DEPTH) ** 0.25
LN_EPS = 1e-5
GATE_OFF = POOL_WIDTH + 4 * MLSTM_WIDTH

LANES = 128
SUBLANES = 8
VMEM_LIMIT = 56 * 1024 * 1024
MLSTM_CHUNK = 256
POOL_ROWS = 2048
ROUTER_ROWS = 32
MOE_SEGMENT = 512
MOE_CAP = 160


def _params(*sem):
    return pltpu.CompilerParams(dimension_semantics=sem, vmem_limit_bytes=VMEM_LIMIT)


def _dot(a, b):
    return jnp.dot(a, b, preferred_element_type=F32)


def _dot_nt(a, b, precision=None):
    return lax.dot_general(a, b, (((1,), (1,)), ((), ())), precision=precision,
                           preferred_element_type=F32)


def _dot_tn(a, b):
    return lax.dot_general(a, b, (((0,), (0,)), ((), ())), preferred_element_type=F32)


def _layer_norm(x, g, b):
    mu = jnp.mean(x, -1, keepdims=True)
    xc = x - mu
    var = jnp.mean(xc * xc, -1, keepdims=True)
    return xc * lax.rsqrt(var + LN_EPS) * g + b


def _unit_norm(x):
    mu = jnp.mean(x, -1, keepdims=True)
    xc = x - mu
    var = jnp.mean(xc * xc, -1, keepdims=True)
    return xc * lax.rsqrt(var + LN_EPS)


def _log_sigmoid(x):
    return jnp.minimum(x, 0.0) - jnp.log1p(jnp.exp(-jnp.abs(x)))


def _scan(x, axis, op, fill):
    n = x.shape[axis]
    idx = lax.broadcasted_iota(jnp.int32, x.shape, axis)
    s = 1
    while s < n:
        x = op(x, jnp.where(idx >= s, pltpu.roll(x, s, axis), fill))
        s *= 2
    return x


def _ln_kernel(x_ref, g_ref, b_ref, o_ref):
    o_ref[...] = _layer_norm(x_ref[...], g_ref[...], b_ref[...])


def _ln(x, g, b, tm):
    n, d = x.shape
    row = pl.BlockSpec((tm, d), lambda i: (i, 0))
    vec = pl.BlockSpec((1, d), lambda i: (0, 0))
    return pl.pallas_call(
        _ln_kernel, grid=(n // tm,), in_specs=[row, vec, vec], out_specs=row,
        out_shape=jax.ShapeDtypeStruct((n, d), F32), compiler_params=_params("parallel"),
        name="emb_ln")(x, g.reshape(1, d), b.reshape(1, d))


def _inproj_kernel(x_ref, w_ref, wg_ref, wgt_ref, u_ref, q_ref, k_ref, v_ref, og_ref,
                   gi_ref, gf_ref, git_ref, gft_ref):
    xb = x_ref[...].astype(BF16)
    w = MLSTM_WIDTH
    u_ref[...] = _dot(xb, w_ref[:, 0:POOL_WIDTH])
    q = _dot(xb, w_ref[:, POOL_WIDTH:POOL_WIDTH + w]) * (MLSTM_HEAD_DIM ** -0.5)
    q_ref[...] = q.astype(q_ref.dtype)
    k_ref[...] = _dot(xb, w_ref[:, POOL_WIDTH + w:POOL_WIDTH + 2 * w]).astype(k_ref.dtype)
    v_ref[...] = _dot(xb, w_ref[:, POOL_WIDTH + 2 * w:POOL_WIDTH + 3 * w]).astype(v_ref.dtype)
    og_ref[...] = _dot(xb, w_ref[:, POOL_WIDTH + 3 * w:POOL_WIDTH + 4 * w])
    g = _dot(xb, wg_ref[...])
    gi_ref[...] = g[:, 0:LANES]
    gf_ref[...] = g[:, LANES:2 * LANES]
    gt = _dot_nt(wgt_ref[...], xb)
    git_ref[...] = gt[0:SUBLANES]
    gft_ref[...] = gt[SUBLANES:2 * SUBLANES]


def _inproj(x, w_main, wg, wgt, seq, tm, qkv_dtype):
    n, d = x.shape
    nt = n // tm
    if seq % tm == 0:
        per = seq // tm
        gt_shape, gt_map = (n // seq, SUBLANES, seq), (lambda i: (i // per, 0, i % per))
    else:
        gt_shape, gt_map = (1, SUBLANES, n), (lambda i: (0, 0, i))
    row = lambda c: pl.BlockSpec((tm, c), lambda i: (i, 0))
    full = lambda a: pl.BlockSpec(a.shape, lambda i: (0, 0))
    gt_spec = pl.BlockSpec((None, SUBLANES, tm), gt_map)
    out_shape = (
        jax.ShapeDtypeStruct((n, POOL_WIDTH), F32),
        jax.ShapeDtypeStruct((n, MLSTM_WIDTH), qkv_dtype),
        jax.ShapeDtypeStruct((n, MLSTM_WIDTH), qkv_dtype),
        jax.ShapeDtypeStruct((n, MLSTM_WIDTH), qkv_dtype),
        jax.ShapeDtypeStruct((n, MLSTM_WIDTH), F32),
        jax.ShapeDtypeStruct((n, LANES), F32),
        jax.ShapeDtypeStruct((n, LANES), F32),
        jax.ShapeDtypeStruct(gt_shape, F32),
        jax.ShapeDtypeStruct(gt_shape, F32),
    )
    return pl.pallas_call(
        _inproj_kernel, grid=(nt,),
        in_specs=[row(d), full(w_main), full(wg), full(wgt)],
        out_specs=(row(POOL_WIDTH), row(MLSTM_WIDTH), row(MLSTM_WIDTH), row(MLSTM_WIDTH),
                   row(MLSTM_WIDTH), row(LANES), row(LANES), gt_spec, gt_spec),
        out_shape=out_shape, compiler_params=_params("parallel"), name="inproj")(x, w_main, wg, wgt)


def _pool_kernel(ext_ref, pw_ref, sc_ref, o_ref, *, period):
    rows = ext_ref.shape[0]
    r = lax.broadcasted_iota(jnp.int32, (rows, POOL_GROUP_DIM), 0)
    if period != rows:
        r = lax.rem(r, period)
    for g, win in enumerate(POOL_WINDOWS):
        sl = slice(g * POOL_GROUP_DIM, (g + 1) * POOL_GROUP_DIM)
        x = ext_ref[:, sl]
        acc = x
        s = 1
        while s < win:
            acc = acc + jnp.where(r >= s, pltpu.roll(acc, s, 0), 0.0)
            s *= 2
        cnt = jnp.minimum(r + 1, win).astype(F32)
        d = acc / cnt - x
        y = _dot(d.astype(BF16), pw_ref[g]) * sc_ref[:, sl]
        o_ref[:, sl] = y.astype(o_ref.dtype)


def _pool(ext2d, pw, scale, period):
    n, c = ext2d.shape
    row = pl.BlockSpec((POOL_ROWS, c), lambda i: (i, 0))
    return pl.pallas_call(
        functools.partial(_pool_kernel, period=period), grid=(n // POOL_ROWS,),
        in_specs=[row, pl.BlockSpec(pw.shape, lambda i: (0, 0, 0)), pl.BlockSpec((1, c), lambda i: (0, 0))],
        out_specs=row, out_shape=jax.ShapeDtypeStruct((n, c), BF16),
        compiler_params=_params("parallel"), name="pool")(ext2d, pw, scale.reshape(1, c))


def _mlstm_chunk_kernel(q_ref, k_ref, v_ref, git_ref, gft_ref, bit_ref, bft_ref, hn_ref, c_ref, n_ref, m_ref,
                        at_s, ws_s, dc_s, mc_s, wi_s, fl_s):
    nb, L, _ = q_ref.shape
    rows = nb * SUBLANES

    @pl.when(pl.program_id(0) == 0)
    def _():
        c_ref[...] = jnp.zeros_like(c_ref)
        n_ref[...] = jnp.zeros_like(n_ref)
        m_ref[...] = jnp.zeros_like(m_ref)

    ig = git_ref[...].reshape(rows, L) + bit_ref[...]
    lf = _log_sigmoid(gft_ref[...].reshape(rows, L) + bft_ref[...])
    bc = _scan(lf, 1, jnp.add, 0.0)
    at = ig - bc
    m_prev = m_ref[...].reshape(rows, LANES)[:, 0:1]
    mc = jnp.maximum(_scan(at, 1, jnp.maximum, -jnp.inf), m_prev)
    mt = bc + mc
    m_last = mc[:, L - 1:L]
    at_s[...] = at.reshape(nb, SUBLANES, L)
    ws_s[...] = jnp.exp(at - m_last).reshape(nb, SUBLANES, L)
    dc_s[...] = jnp.broadcast_to(jnp.exp(m_prev - m_last), (rows, LANES)).reshape(nb, SUBLANES, LANES)
    m_ref[...] = jnp.broadcast_to(mt[:, L - 1:L], (rows, LANES)).reshape(nb, SUBLANES, LANES)
    wi = jnp.exp(m_prev - mc)
    fl = jnp.exp(-mt)
    for b in range(nb):
        rs = slice(b * SUBLANES, (b + 1) * SUBLANES)
        mc_s[b] = mc[rs].T
        wi_s[b] = wi[rs].T
        fl_s[b] = fl[rs].T

    causal = (lax.broadcasted_iota(jnp.int32, (L, L), 0) >= lax.broadcasted_iota(jnp.int32, (L, L), 1))

    def per_batch(b, carry):
        at = at_s[b]
        ws = ws_s[b]
        decay = dc_s[b]
        mc_c, wi_c, fl_c = mc_s[b], wi_s[b], fl_s[b]
        ws16 = jnp.concatenate([ws, ws], axis=0).astype(BF16)
        for h in range(MLSTM_HEADS):
            sl = slice(h * MLSTM_HEAD_DIM, (h + 1) * MLSTM_HEAD_DIM)
            col = slice(h, h + 1)
            qh = q_ref[b, :, sl]
            kh = k_ref[b, :, sl]
            vh = v_ref[b, :, sl]
            s = _dot_nt(qh, kh)
            p = jnp.where(causal, s * jnp.exp(at[col, :] - mc_c[:, col]), 0.0)
            c_old = c_ref[b, h]
            n_old = n_ref[b, col, :]
            wi = wi_c[:, col]
            dv = MLSTM_HEAD_DIM
            v_ext = jnp.concatenate([vh, jnp.ones_like(vh)], axis=1)
            c_ext = jnp.concatenate([c_old, jnp.broadcast_to(n_old, c_old.shape)], axis=0)
            intra = _dot(p.astype(BF16), v_ext)
            inter = _dot_nt(qh, c_ext.astype(BF16))
            num = intra[:, :dv] + wi * inter[:, :dv]
            qn = intra[:, dv:] + wi * inter[:, dv:]
            hh = num / jnp.maximum(jnp.abs(qn), fl_c[:, col])
            hn_ref[b, :, sl] = _unit_norm(hh)
            dc = decay[col, 0:1]
            vts = (vh.astype(F32).T * ws[col, :]).astype(BF16)
            c_ref[b, h] = dc * c_old + _dot(vts, kh)
            n_ref[b, col, :] = dc * n_old + _dot(ws16, kh)[col, :]
        return carry

    lax.fori_loop(0, nb, per_batch, 0)


def _mlstm_chunk(q, k, v, git, gft, bit, bft):
    nb, t, w = q.shape
    L = MLSTM_CHUNK
    seq = lambda c: pl.BlockSpec((nb, L, c), lambda i: (0, i, 0))
    seqt = pl.BlockSpec((nb, SUBLANES, L), lambda i: (0, 0, i))
    vec = lambda a: pl.BlockSpec(a.shape, lambda i: (0, 0))
    out_shape = (
        jax.ShapeDtypeStruct((nb, t, w), F32),
        jax.ShapeDtypeStruct((nb, MLSTM_HEADS, MLSTM_HEAD_DIM, MLSTM_HEAD_DIM), F32),
        jax.ShapeDtypeStruct((nb, MLSTM_HEADS, MLSTM_HEAD_DIM), F32),
        jax.ShapeDtypeStruct((nb, SUBLANES, LANES), F32),
    )
    out_specs = (
        seq(w),
        pl.BlockSpec(out_shape[1].shape, lambda i: (0, 0, 0, 0)),
        pl.BlockSpec(out_shape[2].shape, lambda i: (0, 0, 0)),
        pl.BlockSpec(out_shape[3].shape, lambda i: (0, 0, 0)),
    )
    bit, bft = jnp.tile(bit, (nb, 1)), jnp.tile(bft, (nb, 1))
    scratch = [pltpu.VMEM((nb, SUBLANES, L), F32), pltpu.VMEM((nb, SUBLANES, L), F32),
               pltpu.VMEM((nb, SUBLANES, LANES), F32)] + [pltpu.VMEM((nb, L, SUBLANES), F32) for _ in range(3)]
    return pl.pallas_call(
        _mlstm_chunk_kernel, grid=(t // L,),
        in_specs=[seq(w), seq(w), seq(w), seqt, seqt, vec(bit), vec(bft)],
        out_specs=out_specs, out_shape=out_shape, scratch_shapes=scratch, compiler_params=_params("arbitrary"),
        name="mlstm_chunk")(q, k, v, git, gft, bit, bft)


def _mlstm_step_kernel(q_ref, k_ref, v_ref, gi_ref, gf_ref, bi_ref, bf_ref, c_ref, n_ref, m_ref, c_new_ref,
                       hn_ref, co_ref, no_ref, mo_ref):
    del c_new_ref
    tb = q_ref.shape[0]
    d = MLSTM_HEAD_DIM
    ig = gi_ref[...] + bi_ref[...]
    lf = _log_sigmoid(gf_ref[...] + bf_ref[...])
    m_old = m_ref[...]
    mt = jnp.maximum(lf + m_old, ig)
    mo_ref[...] = mt
    wa = jnp.exp(ig - mt)
    wi = jnp.exp(lf + m_old - mt)
    fl = jnp.exp(-mt)
    wa_t, wi_t, fl_t = wa.T, wi.T, fl.T
    lane = lax.broadcasted_iota(jnp.int32, (d, tb), 1)
    for h in range(MLSTM_HEADS):
        sl = slice(h * d, (h + 1) * d)
        qh, kh, vh = q_ref[:, sl], k_ref[:, sl], v_ref[:, sl]
        nh = n_ref[:, h, :]
        qt, kt, vt, nt = qh.T, kh.T, vh.T, nh.T
        wa_r, wi_r, fl_r = wa_t[h:h + 1, :], wi_t[h:h + 1, :], fl_t[h:h + 1, :]
        s = jnp.sum(qt * kt, 0, keepdims=True) * wa_r
        cq = jnp.zeros((d, tb), F32)
        for b in range(tb):
            cq = jnp.where(lane == b, jnp.sum(c_ref[b, h] * qh[b:b + 1, :], -1, keepdims=True), cq)
        num = s * vt + wi_r * cq
        qn = s + wi_r * jnp.sum(nt * qt, 0, keepdims=True)
        hh = num / jnp.maximum(jnp.abs(qn), fl_r)
        mu = jnp.mean(hh, 0, keepdims=True)
        xc = hh - mu
        var = jnp.mean(xc * xc, 0, keepdims=True)
        hn_ref[:, sl] = (xc * lax.rsqrt(var + LN_EPS)).T
        wav = wa_r * vt
        for b in range(tb):
            co_ref[b, h] = wi_r[:, b:b + 1] * c_ref[b, h] + wav[:, b:b + 1] * kh[b:b + 1, :]
        no_ref[:, h, :] = wi[:, h:h + 1] * nh + wa[:, h:h + 1] * kh


def _mlstm_step(q, k, v, gi, gf, bi, bf, c_all, n_all, m0, c_new, layer, tb):
    n, w = q.shape
    d = MLSTM_HEAD_DIM
    row = lambda c: pl.BlockSpec((tb, c), lambda i: (i, 0))
    vec = lambda a: pl.BlockSpec(a.shape, lambda i: (0, 0))
    cin = pl.BlockSpec((None, tb, MLSTM_HEADS, d, d), lambda i: (layer, i, 0, 0, 0))
    nin = pl.BlockSpec((None, tb, MLSTM_HEADS, d), lambda i: (layer, i, 0, 0))
    nout = pl.BlockSpec((tb, MLSTM_HEADS, d), lambda i: (i, 0, 0))
    out_shape = (
        jax.ShapeDtypeStruct((n, w), F32),
        jax.ShapeDtypeStruct(c_new.shape, F32),
        jax.ShapeDtypeStruct(n_all.shape[1:], F32),
        jax.ShapeDtypeStruct((n, LANES), F32),
    )
    return pl.pallas_call(
        _mlstm_step_kernel, grid=(n // tb,),
        in_specs=[row(w), row(w), row(w), row(LANES), row(LANES), vec(bi), vec(bf), cin, nin, row(LANES),
                  pl.BlockSpec(memory_space=pl.ANY)],
        out_specs=(row(w), cin, nout, row(LANES)), out_shape=out_shape, input_output_aliases={10: 1},
        compiler_params=_params("parallel"), name="mlstm_step")(q, k, v, gi, gf, bi, bf, c_all, n_all, m0, c_new)


def _mixout_kernel(po_ref, hn_ref, og_ref, x_ref, ng_ref, w_ref, g_ref, b_ref, o_ref):
    mo = jax.nn.sigmoid(og_ref[...]) * hn_ref[...] * ng_ref[...]
    mix = _dot(po_ref[...], w_ref[0:POOL_WIDTH, :]) + _dot(mo.astype(BF16), w_ref[POOL_WIDTH:, :])
    o_ref[...] = _layer_norm(ALPHA * x_ref[...] + mix, g_ref[...], b_ref[...])


def _mixout(po, hn, og, x, ng, w, g, b, tm):
    n, d = x.shape
    row = lambda c: pl.BlockSpec((tm, c), lambda i: (i, 0))
    vec = lambda a: pl.BlockSpec(a.shape, lambda i: (0, 0))
    return pl.pallas_call(
        _mixout_kernel, grid=(n // tm,),
        in_specs=[row(POOL_WIDTH), row(MLSTM_WIDTH), row(MLSTM_WIDTH), row(d), vec(ng), vec(w), vec(g), vec(b)],
        out_specs=row(d), out_shape=jax.ShapeDtypeStruct((n, d), F32),
        compiler_params=_params("parallel"), name="mixout")(po, hn, og, x, ng, w, g, b)


def _mm_kernel(x_ref, w_ref, o_ref, *, scale):
    y = _dot(x_ref[...].astype(BF16), w_ref[...])
    if scale != 1.0:
        y = y * scale
    o_ref[...] = y.astype(o_ref.dtype)


def _mm(x, w, tm, out_dtype, scale=1.0, name="proj"):
    n, d = x.shape
    dout = w.shape[1]
    return pl.pallas_call(
        functools.partial(_mm_kernel, scale=scale), grid=(n // tm,),
        in_specs=[pl.BlockSpec((tm, d), lambda i: (i, 0)), pl.BlockSpec(w.shape, lambda i: (0, 0))],
        out_specs=pl.BlockSpec((tm, dout), lambda i: (i, 0)),
        out_shape=jax.ShapeDtypeStruct((n, dout), out_dtype),
        compiler_params=_params("parallel"), name=name)(x, w)


def _cast_kernel(w_ref, o_ref):
    o_ref[...] = w_ref[...].astype(o_ref.dtype)


def _cast_layer(w_all, layer, cols, tr=256):
    rows = w_all.shape[1]
    return pl.pallas_call(
        _cast_kernel, grid=(rows // tr,),
        in_specs=[pl.BlockSpec((None, tr, cols), lambda i: (layer, i, 0))],
        out_specs=pl.BlockSpec((tr, cols), lambda i: (i, 0)),
        out_shape=jax.ShapeDtypeStruct((rows, cols), BF16),
        compiler_params=_params("parallel"), name="weight_cast")(w_all)


def _cast_inproj_kernel(w_ref, o_ref, og_ref):
    w = w_ref[...]
    o_ref[...] = w[:, :GATE_OFF].astype(BF16)
    og_ref[...] = jnp.zeros_like(og_ref)
    og_ref[:, 0:2 * MLSTM_HEADS] = w[:, GATE_OFF:]


def _cast_inproj(w_in, layer, tr=256):
    _, rows, cols = w_in.shape
    return pl.pallas_call(
        _cast_inproj_kernel, grid=(rows // tr,),
        in_specs=[pl.BlockSpec((None, tr, cols), lambda i: (layer, i, 0))],
        out_specs=(pl.BlockSpec((tr, GATE_OFF), lambda i: (i, 0)), pl.BlockSpec((tr, LANES), lambda i: (i, 0))),
        out_shape=(jax.ShapeDtypeStruct((rows, GATE_OFF), BF16), jax.ShapeDtypeStruct((rows, LANES), F32)),
        compiler_params=_params("parallel"), name="inproj_cast")(w_in)


def _mem_proj_kernel(x_ref, wk_ref, wv_ref, k4_ref, v4_ref, k2_ref, v2_ref, wkb, wvb):
    @pl.when(pl.program_id(1) == 0)
    def _():
        wkb[...] = wk_ref[...].astype(BF16)
        wvb[...] = wv_ref[...].astype(BF16)

    xb = x_ref[...].astype(BF16)
    k = _dot(xb, wkb[...])
    v = _dot(xb, wvb[...])
    k2_ref[...] = k
    v2_ref[...] = v
    k4_ref[...] = k.reshape(N_MEM, CA_HEADS, CA_HEAD_DIM)
    v4_ref[...] = v.reshape(N_MEM, CA_HEADS, CA_HEAD_DIM)


def _mem_proj(mem, wk_all, wv_all):
    nb, m, d = mem.shape
    depth = wk_all.shape[0]
    wspec = pl.BlockSpec((None, d, d), lambda l, b: (l, 0, 0))
    o4 = pl.BlockSpec((None, None, m, CA_HEADS, CA_HEAD_DIM), lambda l, b: (l, b, 0, 0, 0))
    o2 = pl.BlockSpec((None, None, m, d), lambda l, b: (l, b, 0, 0))
    s4 = jax.ShapeDtypeStruct((depth, nb, m, CA_HEADS, CA_HEAD_DIM), F32)
    s2 = jax.ShapeDtypeStruct((depth, nb, m, d), F32)
    return pl.pallas_call(
        _mem_proj_kernel, grid=(depth, nb),
        in_specs=[pl.BlockSpec((None, m, d), lambda l, b: (b, 0, 0)), wspec, wspec],
        out_specs=(o4, o4, o2, o2), out_shape=(s4, s4, s2, s2),
        scratch_shapes=[pltpu.VMEM((d, d), BF16), pltpu.VMEM((d, d), BF16)],
        compiler_params=_params("arbitrary", "arbitrary"), name="mem_proj")(mem, wk_all, wv_all)


def _attn_block_kernel(po_ref, hn_ref, og_ref, x_ref, k_ref, v_ref, ng_ref, wm_ref, g1_ref, b1_ref,
                       wq_ref, wo_ref, g_ref, b_ref, o_ref):
    mo = jax.nn.sigmoid(og_ref[...]) * hn_ref[...] * ng_ref[...]
    mix = _dot(po_ref[...], wm_ref[0:POOL_WIDTH, :]) + _dot(mo.astype(BF16), wm_ref[POOL_WIDTH:, :])
    x = _layer_norm(ALPHA * x_ref[...] + mix, g1_ref[...], b1_ref[...])
    qb = (_dot(x.astype(BF16), wq_ref[...]) * (CA_HEAD_DIM ** -0.5)).astype(BF16)
    kb = k_ref[...].astype(BF16)
    vb = v_ref[...].astype(BF16)
    ctx = []
    for h in range(CA_HEADS):
        sl = slice(h * CA_HEAD_DIM, (h + 1) * CA_HEAD_DIM)
        s = _dot_nt(qb[:, sl], kb[:, sl])
        e = jnp.exp(s - jnp.max(s, -1, keepdims=True))
        ctx.append((_dot(e.astype(BF16), vb[:, sl]) / jnp.sum(e, -1, keepdims=True)).astype(BF16))
    y = _dot(jnp.concatenate(ctx, axis=1), wo_ref[...])
    o_ref[...] = _layer_norm(ALPHA * x + y, g_ref[...], b_ref[...])


def _attn_block(po, hn, og, x, k, v, layer, ng, wm, ln1, wq, wo, ln2, seq, tq):
    n, d = x.shape
    per = seq // tq
    row = lambda c: pl.BlockSpec((tq, c), lambda i: (i, 0))
    kspec = pl.BlockSpec((None, None, N_MEM, d), lambda i: (layer, i // per, 0, 0))
    vec = lambda a: pl.BlockSpec(a.shape, lambda i: (0, 0))
    return pl.pallas_call(
        _attn_block_kernel, grid=(n // tq,),
        in_specs=[row(POOL_WIDTH), row(MLSTM_WIDTH), row(MLSTM_WIDTH), row(d), kspec, kspec, vec(ng), vec(wm),
                  vec(ln1[0]), vec(ln1[1]), vec(wq), vec(wo), vec(ln2[0]), vec(ln2[1])],
        out_specs=row(d), out_shape=jax.ShapeDtypeStruct((n, d), F32),
        compiler_params=_params("parallel"), name="attn_block")(po, hn, og, x, k, v, ng, wm, *ln1, wq, wo, *ln2)


def _attn_step_kernel(q_ref, k_ref, v_ref, o_ref):
    bb = q_ref.shape[0]
    rows = N_MEM * CA_HEADS
    lane = lax.broadcasted_iota(jnp.int32, (SUBLANES, rows), 1)
    row = lax.broadcasted_iota(jnp.int32, (SUBLANES, rows), 0)
    own = lax.rem(lane, CA_HEADS) == lax.rem(row, CA_HEADS)
    for j in range(bb):
        kf = k_ref[j].reshape(rows, CA_HEAD_DIM).astype(BF16)
        vf = v_ref[j].reshape(rows, CA_HEAD_DIM).astype(BF16)
        s = jnp.where(own, _dot_nt(q_ref[j].astype(BF16), kf), -jnp.inf)
        e = jnp.exp(s - jnp.max(s, -1, keepdims=True))
        o_ref[j] = _dot(e.astype(BF16), vf) / jnp.sum(e, -1, keepdims=True)


def _attn_step(q, k_all, v_all, layer, bb):
    n = q.shape[0]
    qspec = pl.BlockSpec((bb, SUBLANES, CA_HEAD_DIM), lambda i: (i, 0, 0))
    kspec = pl.BlockSpec((None, bb, N_MEM, CA_HEADS, CA_HEAD_DIM), lambda i: (layer, i, 0, 0, 0))
    return pl.pallas_call(
        _attn_step_kernel, grid=(n // bb,), in_specs=[qspec, kspec, kspec], out_specs=qspec,
        out_shape=jax.ShapeDtypeStruct((n, SUBLANES, CA_HEAD_DIM), F32),
        compiler_params=_params("parallel"), name="attn_step")(q, k_all, v_all)


def _mm_res_ln_kernel(a_ref, x_ref, w_ref, g_ref, b_ref, o_ref):
    y = _dot(a_ref[...].astype(BF16), w_ref[...])
    o_ref[...] = _layer_norm(ALPHA * x_ref[...] + y, g_ref[...], b_ref[...])


def _mm_res_ln(a, x, w, g, b, tm):
    n, d = x.shape
    row = lambda c: pl.BlockSpec((tm, c), lambda i: (i, 0))
    vec = lambda arr: pl.BlockSpec(arr.shape, lambda i: (0, 0))
    return pl.pallas_call(
        _mm_res_ln_kernel, grid=(n // tm,),
        in_specs=[row(a.shape[1]), row(d), vec(w), vec(g), vec(b)], out_specs=row(d),
        out_shape=jax.ShapeDtypeStruct((n, d), F32),
        compiler_params=_params("parallel"), name="proj_res_ln")(a, x, w, g, b)


def _route(lt):
    gl = [lt[g:g + 1, :] for g in range(N_GROUPS)]
    gmax = functools.reduce(jnp.maximum, gl)
    gsum = functools.reduce(jnp.add, [jnp.exp(x - gmax) for x in gl])
    pg_sel = 1.0 / gsum

    def first_max(vals):
        m = functools.reduce(jnp.maximum, vals)
        taken = jnp.zeros_like(m, dtype=jnp.bool_)
        hot = []
        for x in vals:
            h = jnp.logical_and(x == m, jnp.logical_not(taken))
            taken = jnp.logical_or(taken, h)
            hot.append(h)
        return m, hot

    _, g_hot = first_max(gl)
    el = []
    for j in range(EXPERTS_PER_GROUP):
        rows = [lt[SUBLANES + g * EXPERTS_PER_GROUP + j:SUBLANES + g * EXPERTS_PER_GROUP + j + 1, :]
                for g in range(N_GROUPS)]
        x = rows[N_GROUPS - 1]
        for g in range(N_GROUPS - 2, -1, -1):
            x = jnp.where(g_hot[g], rows[g], x)
        el.append(x)
    emax = functools.reduce(jnp.maximum, el)
    ee = [jnp.exp(x - emax) for x in el]
    esum = functools.reduce(jnp.add, ee)
    pe = [x / esum for x in ee]
    p1, hot1 = first_max(pe)
    p2, hot2 = first_max([jnp.where(h, -jnp.inf, x) for h, x in zip(hot1, pe)])
    psum = p1 + p2
    gate = [jnp.where(h1, pg_sel * p1 / psum, jnp.where(h2, pg_sel * p2 / psum, 0.0)) for h1, h2 in zip(hot1, hot2)]
    return g_hot, gate


def _expert_cast_kernel(wg_ref, wu_ref, wd_ref, og_ref, ou_ref, od_ref):
    f = D_EXPERT
    for j in range(EXPERTS_PER_GROUP):
        og_ref[:, j * f:(j + 1) * f] = wg_ref[j].astype(BF16)
        ou_ref[:, j * f:(j + 1) * f] = wu_ref[j].astype(BF16)
        od_ref[j * f:(j + 1) * f, :] = wd_ref[j].astype(BF16)


def _expert_cast(wg, wu, wd, layer):
    d = wg.shape[2]
    e, gf = EXPERTS_PER_GROUP, EXPERTS_PER_GROUP * D_EXPERT
    cols = pl.BlockSpec((None, e, d, D_EXPERT), lambda g: (layer, g, 0, 0))
    rows = pl.BlockSpec((None, e, D_EXPERT, d), lambda g: (layer, g, 0, 0))
    return pl.pallas_call(
        _expert_cast_kernel, grid=(N_GROUPS,), in_specs=[cols, cols, rows],
        out_specs=(pl.BlockSpec((None, d, gf), lambda g: (g, 0, 0)), pl.BlockSpec((None, d, gf), lambda g: (g, 0, 0)),
                   pl.BlockSpec((None, gf, d), lambda g: (g, 0, 0))),
        out_shape=(jax.ShapeDtypeStruct((N_GROUPS, d, gf), BF16), jax.ShapeDtypeStruct((N_GROUPS, d, gf), BF16),
                   jax.ShapeDtypeStruct((N_GROUPS, gf, d), BF16)),
        compiler_params=_params("parallel"), name="expert_cast")(wg, wu, wd)


def _moe_kernel(x_ref, wr_ref, br_ref, tri_ref, wg_ref, wu_ref, wd_ref, g_ref, b_ref, o_ref,
                xb_ref, ct_ref, comb_ref, perm_ref, acc_ref, full_ref):
    grp = pl.program_id(1)
    tm = x_ref.shape[0]
    ns, cap, sw = perm_ref.shape[1:]
    f = D_EXPERT

    @pl.when(grp == 0)
    def _():
        x = x_ref[...]
        xh = x.astype(BF16)
        xb_ref[...] = xh
        xl = (x - xh.astype(F32)).astype(BF16)
        wr = wr_ref[...]
        wh = wr.astype(BF16)
        wl = (wr - wh.astype(F32)).astype(BF16)
        lt = _dot_nt(wh, xh) + (_dot_nt(wl, xh) + _dot_nt(wh, xl)) + br_ref[...]
        g_hot, gate = _route(lt)
        row8 = lax.broadcasted_iota(jnp.int32, (SUBLANES, tm), 0)
        hot8 = jnp.zeros((SUBLANES, tm), F32)
        for g in range(N_GROUPS):
            hot8 = jnp.where(jnp.logical_and(row8 == g, g_hot[g]), 1.0, hot8)
        most = None
        for s in range(ns):
            hs = hot8[:, s * sw:(s + 1) * sw]
            cum = _dot(hs.astype(BF16), tri_ref[...])
            seg_max = jnp.max(cum[:, sw - 1:sw])
            most = seg_max if most is None else jnp.maximum(most, seg_max)
            base = lax.broadcasted_iota(jnp.int32, (SUBLANES, sw), 0).astype(F32) * cap
            code = jnp.sum(hs * (base + cum - 1.0), 0, keepdims=True).astype(jnp.int32)
            for g in range(N_GROUPS):
                r = lax.broadcasted_iota(jnp.int32, (cap, sw), 0) + g * cap
                perm_ref[g, s] = jnp.where(r == code, 1.0, 0.0).astype(BF16)
        full_ref[0] = (most > cap).astype(jnp.int32)
        ct_ref[...] = jnp.zeros_like(ct_ref)
        for j in range(EXPERTS_PER_GROUP):
            ct_ref[j:j + 1, :] = gate[j]
        ct_ref[EXPERTS_PER_GROUP:EXPERTS_PER_GROUP + 1, :] = jnp.sum(hot8 * row8.astype(F32), 0, keepdims=True)
        comb_ref[...] = ct_ref[...].T
        acc_ref[...] = jnp.zeros_like(acc_ref)

    def experts(rows, gates):
        hh = []
        for j in range(EXPERTS_PER_GROUP):
            hg = _dot(rows, wg_ref[:, j * f:(j + 1) * f])
            hu = _dot(rows, wu_ref[:, j * f:(j + 1) * f])
            hh.append((hg * jax.nn.sigmoid(hg) * hu * gates[:, j:j + 1]).astype(BF16))
        return _dot(jnp.concatenate(hh, axis=1), wd_ref[...])

    @pl.when(full_ref[0] == 0)
    def _():
        comb = comb_ref[...]
        ch = comb.astype(BF16)
        cl = (comb - ch.astype(F32)).astype(BF16)
        xb = xb_ref[...]
        perm = [perm_ref[grp, s] for s in range(ns)]
        seg = lambda a, s: a[s * sw:(s + 1) * sw]
        rows = jnp.concatenate([_dot(perm[s], seg(xb, s)) for s in range(ns)], axis=0).astype(BF16)
        gates = jnp.concatenate([_dot(perm[s], seg(ch, s)) + _dot(perm[s], seg(cl, s)) for s in range(ns)], axis=0)
        y = experts(rows, gates).astype(BF16)
        for s in range(ns):
            acc_ref[s * sw:(s + 1) * sw, :] += _dot_tn(perm[s], y[s * cap:(s + 1) * cap])

    @pl.when(full_ref[0] != 0)
    def _():
        comb = comb_ref[...]
        own = comb[:, EXPERTS_PER_GROUP:EXPERTS_PER_GROUP + 1] == grp.astype(F32)
        acc_ref[...] += experts(xb_ref[...], jnp.where(own, comb, 0.0))

    @pl.when(grp == N_GROUPS - 1)
    def _():
        o_ref[...] = _layer_norm(ALPHA * x_ref[...] + acc_ref[...], g_ref[...], b_ref[...])


def _moe(x, wr, br, wg, wu, wd, g, b, tm):
    n, d = x.shape
    ns = max(1, tm // MOE_SEGMENT)
    sw = tm // ns
    cap = min(sw, MOE_CAP)
    tri = (lax.broadcasted_iota(jnp.int32, (sw, sw), 0) <= lax.broadcasted_iota(jnp.int32, (sw, sw), 1)).astype(BF16)
    row = pl.BlockSpec((tm, d), lambda i, e: (i, 0))
    vec = lambda a: pl.BlockSpec(a.shape, lambda i, e: (0, 0))
    grp = lambda a: pl.BlockSpec((None,) + a.shape[1:], lambda i, e: (e, 0, 0))
    scratch = [pltpu.VMEM((tm, d), BF16), pltpu.VMEM((LANES, tm), F32), pltpu.VMEM((tm, LANES), F32),
               pltpu.VMEM((N_GROUPS, ns, cap, sw), BF16), pltpu.VMEM((tm, d), F32), pltpu.SMEM((1,), jnp.int32)]
    return pl.pallas_call(
        _moe_kernel, grid=(n // tm, N_GROUPS),
        in_specs=[row, vec(wr), vec(br), vec(tri), grp(wg), grp(wu), grp(wd), vec(g), vec(b)],
        out_specs=row, out_shape=jax.ShapeDtypeStruct((n, d), F32), scratch_shapes=scratch,
        compiler_params=_params("parallel", "arbitrary"), name="moe")(x, wr, br, tri, wg, wu, wd, g, b)


def _pad_lanes(v):
    return jnp.zeros((1, LANES), F32).at[0, :v.shape[0]].set(v)


def _pad_rows(v):
    return jnp.zeros((SUBLANES, 1), F32).at[:v.shape[0], 0].set(v)


def _layer_weights(l, w_in, b_i, b_f, pool_w, pool_scale, mlstm_norm_g, w_out, ln1_g, ln1_b, ca_wq, ca_wo,
                   ln2_g, ln2_b, w_gr, b_gr, w_er, b_er, w_gate, w_up, w_down, ln3_g, ln3_b):
    d = D_MODEL
    w_main, w_gate_lanes = _cast_inproj(w_in, l)
    w_gate_cols = w_gate_lanes[:, :2 * MLSTM_HEADS]
    wg = jnp.zeros((d, 2 * LANES), F32)
    wg = wg.at[:, 0:MLSTM_HEADS].set(w_gate_cols[:, :MLSTM_HEADS])
    wg = wg.at[:, LANES:LANES + MLSTM_HEADS].set(w_gate_cols[:, MLSTM_HEADS:])
    wgt = jnp.zeros((2 * SUBLANES, d), F32)
    wgt = wgt.at[0:MLSTM_HEADS].set(w_gate_cols[:, :MLSTM_HEADS].T)
    wgt = wgt.at[SUBLANES:SUBLANES + MLSTM_HEADS].set(w_gate_cols[:, MLSTM_HEADS:].T)
    wr = jnp.zeros((ROUTER_ROWS, d), F32)
    wr = wr.at[0:N_GROUPS].set(w_gr[l].T).at[SUBLANES:SUBLANES + N_EXPERTS].set(w_er[l].T)
    br = jnp.zeros((ROUTER_ROWS, 1), F32)
    br = br.at[0:N_GROUPS, 0].set(b_gr[l]).at[SUBLANES:SUBLANES + N_EXPERTS, 0].set(b_er[l])
    row = lambda v: v.reshape(1, -1)
    return dict(
        w_main=w_main, wg=wg.astype(BF16), wgt=wgt.astype(BF16),
        bi=_pad_lanes(b_i[l]), bf=_pad_lanes(b_f[l]), bit=_pad_rows(b_i[l]), bft=_pad_rows(b_f[l]),
        pool_w=pool_w[l].astype(BF16), pool_scale=pool_scale[l], norm_g=row(mlstm_norm_g[l]),
        w_out=_cast_layer(w_out, l, d), ln1=(row(ln1_g[l]), row(ln1_b[l])),
        wq=_cast_layer(ca_wq, l, d), wo=_cast_layer(ca_wo, l, d), ln2=(row(ln2_g[l]), row(ln2_b[l])),
        wr=wr, br=br, experts=_expert_cast(w_gate, w_up, w_down, l), ln3=(row(ln3_g[l]), row(ln3_b[l])))


def _trunk_layer(x, nb, seq, layer, mem, pool_prev, state, p, tm, tm_moe):
    chunked = state is None
    u, q, k, v, og, gi, gf, git, gft = _inproj(x, p["w_main"], p["wg"], p["wgt"], seq, tm,
                                               BF16 if chunked else F32)
    u3 = u.reshape(nb, seq, POOL_WIDTH)
    ext = u3 if pool_prev is None else jnp.concatenate([pool_prev, u3], axis=1)
    period = ext.shape[1]
    pooled = _pool(ext.reshape(nb * period, POOL_WIDTH), p["pool_w"], p["pool_scale"], period)
    pooled = pooled.reshape(nb, period, POOL_WIDTH)[:, period - seq:].reshape(nb * seq, POOL_WIDTH)
    pool_buf = ext[:, -POOL_BUF:]
    if chunked:
        r3 = lambda a: a.reshape(nb, seq, a.shape[-1])
        hn, c1, n1, m1 = _mlstm_chunk(r3(q), r3(k), r3(v), git, gft, p["bit"], p["bft"])
        hn = hn.reshape(nb * seq, MLSTM_WIDTH)
        m1 = m1[:, :MLSTM_HEADS, 0]
    else:
        c_all, n_all, m0, c_new = state
        m0p = jnp.zeros((nb, LANES), F32).at[:, :MLSTM_HEADS].set(m0)
        hn, c1, n1, m1 = _mlstm_step(q, k, v, gi, gf, p["bi"], p["bf"], c_all, n_all, m0p, c_new, layer, SUBLANES)
        m1 = m1[:, :MLSTM_HEADS]
    if chunked:
        x = _attn_block(pooled, hn, og, x, mem[0], mem[1], layer, p["norm_g"], p["w_out"], p["ln1"], p["wq"],
                        p["wo"], p["ln2"], seq, tm_moe)
    else:
        x = _mixout(pooled, hn, og, x, p["norm_g"], p["w_out"], *p["ln1"], tm)
        qc = _mm(x, p["wq"], tm, F32, scale=CA_HEAD_DIM ** -0.5, name="ca_q")
        qh = jnp.zeros((nb, SUBLANES, CA_HEAD_DIM), F32).at[:, :CA_HEADS].set(qc.reshape(nb, CA_HEADS, CA_HEAD_DIM))
        ctx = _attn_step(qh, mem[0], mem[1], layer, 4)[:, :CA_HEADS].reshape(nb, D_MODEL)
        x = _mm_res_ln(ctx, x, p["wo"], *p["ln2"], tm)
    x = _moe(x, p["wr"], p["br"], *p["experts"], *p["ln3"], tm_moe)
    return x, pool_buf, c1, n1, m1


def kernel(x_prompt, x_sample, mem_prompt, cache_pool, state_mlstm_C, state_mlstm_n, state_mlstm_m,
           cache_mem_k, cache_mem_v, emb_ln_g, emb_ln_b, w_in, b_i, b_f, pool_w, pool_scale,
           mlstm_norm_g, w_out, ln1_g, ln1_b, ca_wq, ca_wk, ca_wv, ca_wo, ln2_g, ln2_b,
           w_gr, b_gr, w_er, b_er, w_gate, w_up, w_down, ln3_g, ln3_b):
    bp, tp, d = x_prompt.shape
    bs, ts, _ = x_sample.shape
    tm_p, tm_s = 512, bs * ts
    xp = _ln(x_prompt.reshape(bp * tp, d), emb_ln_g, emb_ln_b, tm_p)
    xs = _ln(x_sample.reshape(bs * ts, d), emb_ln_g, emb_ln_b, tm_s)
    mk4, mv4, mk2, mv2 = _mem_proj(mem_prompt, ca_wk, ca_wv)
    outs = [[] for _ in range(7)]
    sc = jnp.zeros_like(state_mlstm_C)
    for l in range(DEPTH):
        p = _layer_weights(l, w_in, b_i, b_f, pool_w, pool_scale, mlstm_norm_g, w_out, ln1_g, ln1_b, ca_wq,
                           ca_wo, ln2_g, ln2_b, w_gr, b_gr, w_er, b_er, w_gate, w_up, w_down, ln3_g, ln3_b)
        xp, pb, c1, n1, m1 = _trunk_layer(xp, bp, tp, l, (mk2, mv2), None, None, p, tm_p, 2 * tm_p)
        xs, sb, sc, ns, ms = _trunk_layer(xs, bs, ts, l, (cache_mem_k, cache_mem_v), cache_pool[l],
                                          (state_mlstm_C, state_mlstm_n, state_mlstm_m[l], sc), p, tm_s, tm_s)
        for lst, val in zip(outs, (pb, c1, n1, m1, sb, ns, ms)):
            lst.append(val)
    pp, pc, pn, pm, sp, sn, sm = (jnp.stack(o) for o in outs)
    return (xp.reshape(bp, tp, d), xs.reshape(bs, ts, d), pp, pc, pn, pm, mk4, mv4, sp, sc, sn, sm)
```

```python
import functools

import jax
import jax.numpy as jnp
from jax import lax
from jax.experimental import pallas as pl
from jax.experimental.pallas import tpu as pltpu

F32 = jnp.float32
BF16 = jnp.bfloat16

D_MODEL = 1024
DEPTH = 4
POOL_WIDTH = 512
POOL_GROUPS = 4
POOL_GROUP_DIM = 128
POOL_WINDOWS = (2, 4, 8, 16)
POOL_BUF = 15
MLSTM_WIDTH = 512
MLSTM_HEADS = 4
MLSTM_HEAD_DIM = 128
N_MEM = 256
CA_HEADS = 4
CA_HEAD_DIM = 256
N_GROUPS = 4
EXPERTS_PER_GROUP = 4
N_EXPERTS = 16
D_EXPERT = 256
ALPHA = (2 * DEPTH) ** 0.25
LN_EPS = 1e-5
GATE_OFF = POOL_WIDTH + 4 * MLSTM_WIDTH

LANES = 128
SUBLANES = 8
VMEM_LIMIT = 56 * 1024 * 1024
MLSTM_CHUNK = 256
POOL_ROWS = 2048
POOL_HALO = 16
ROUTER_ROWS = 32
MOE_SEGMENT = 512
MOE_CAP = 160


def _params(*sem):
    return pltpu.CompilerParams(dimension_semantics=sem, vmem_limit_bytes=VMEM_LIMIT)


def _dot(a, b):
    return jnp.dot(a, b, preferred_element_type=F32)


def _dot_nt(a, b, precision=None):
    return lax.dot_general(a, b, (((1,), (1,)), ((), ())), precision=precision,
                           preferred_element_type=F32)


def _dot_tn(a, b):
    return lax.dot_general(a, b, (((0,), (0,)), ((), ())), preferred_element_type=F32)


def _layer_norm(x, g, b):
    mu = jnp.mean(x, -1, keepdims=True)
    xc = x - mu
    var = jnp.mean(xc * xc, -1, keepdims=True)
    return xc * lax.rsqrt(var + LN_EPS) * g + b


def _unit_norm(x):
    mu = jnp.mean(x, -1, keepdims=True)
    xc = x - mu
    var = jnp.mean(xc * xc, -1, keepdims=True)
    return xc * lax.rsqrt(var + LN_EPS)


def _log_sigmoid(x):
    return jnp.minimum(x, 0.0) - jnp.log1p(jnp.exp(-jnp.abs(x)))


def _scan(x, axis, op, fill):
    n = x.shape[axis]
    idx = lax.broadcasted_iota(jnp.int32, x.shape, axis)
    s = 1
    while s < n:
        x = op(x, jnp.where(idx >= s, pltpu.roll(x, s, axis), fill))
        s *= 2
    return x


def _ln_kernel(x_ref, g_ref, b_ref, o_ref):
    o_ref[...] = _layer_norm(x_ref[...], g_ref[...], b_ref[...])


def _ln(x, g, b, tm):
    n, d = x.shape
    row = pl.BlockSpec((tm, d), lambda i: (i, 0))
    vec = pl.BlockSpec((1, d), lambda i: (0, 0))
    return pl.pallas_call(
        _ln_kernel, grid=(n // tm,), in_specs=[row, vec, vec], out_specs=row,
        out_shape=jax.ShapeDtypeStruct((n, d), F32), compiler_params=_params("parallel"),
        name="emb_ln")(x, g.reshape(1, d), b.reshape(1, d))


def _qkvo(xb, w_ref, q_ref, k_ref, v_ref, og_ref):
    w = MLSTM_WIDTH
    q = _dot(xb, w_ref[:, POOL_WIDTH:POOL_WIDTH + w]) * (MLSTM_HEAD_DIM ** -0.5)
    q_ref[...] = q.astype(q_ref.dtype)
    k_ref[...] = _dot(xb, w_ref[:, POOL_WIDTH + w:POOL_WIDTH + 2 * w]).astype(k_ref.dtype)
    v_ref[...] = _dot(xb, w_ref[:, POOL_WIDTH + 2 * w:POOL_WIDTH + 3 * w]).astype(v_ref.dtype)
    og_ref[...] = _dot(xb, w_ref[:, POOL_WIDTH + 3 * w:POOL_WIDTH + 4 * w])


def _inproj_kernel(x_ref, w_ref, wg_ref, u_ref, q_ref, k_ref, v_ref, og_ref, gi_ref, gf_ref):
    xb = x_ref[...].astype(BF16)
    u_ref[...] = _dot(xb, w_ref[:, 0:POOL_WIDTH])
    _qkvo(xb, w_ref, q_ref, k_ref, v_ref, og_ref)
    g = _dot(xb, wg_ref[...])
    gi_ref[...] = g[:, 0:LANES]
    gf_ref[...] = g[:, LANES:2 * LANES]


def _inproj(x, w_main, wg, tm):
    n, d = x.shape
    row = lambda c: pl.BlockSpec((tm, c), lambda i: (i, 0))
    full = lambda a: pl.BlockSpec(a.shape, lambda i: (0, 0))
    widths = (POOL_WIDTH, MLSTM_WIDTH, MLSTM_WIDTH, MLSTM_WIDTH, MLSTM_WIDTH, LANES, LANES)
    return pl.pallas_call(
        _inproj_kernel, grid=(n // tm,), in_specs=[row(d), full(w_main), full(wg)],
        out_specs=tuple(row(c) for c in widths),
        out_shape=tuple(jax.ShapeDtypeStruct((n, c), F32) for c in widths),
        compiler_params=_params("parallel"), name="inproj")(x, w_main, wg)


def _inproj_seq_kernel(x_ref, w_ref, wgt_ref, pw_ref, sc_ref, po_ref, tail_ref, q_ref, k_ref, v_ref, og_ref,
                       git_ref, gft_ref, carry_ref, *, per):
    tm = x_ref.shape[0]
    start = lax.rem(pl.program_id(0), per) * tm

    @pl.when(start == 0)
    def _():
        carry_ref[...] = jnp.zeros_like(carry_ref)

    xb = x_ref[...].astype(BF16)
    u = _dot(xb, w_ref[:, 0:POOL_WIDTH])
    halo = carry_ref.shape[0]
    ext = jnp.concatenate([carry_ref[...], u], axis=0)
    tail = ext[tm:, :]
    carry_ref[...] = tail
    tail_ref[...] = tail
    pos = start + lax.broadcasted_iota(jnp.int32, (tm, POOL_GROUP_DIM), 0)
    for g, win in enumerate(POOL_WINDOWS):
        sl = slice(g * POOL_GROUP_DIM, (g + 1) * POOL_GROUP_DIM)
        acc = ext[:, sl]
        s = 1
        while s < win:
            acc = acc + pltpu.roll(acc, s, 0)
            s *= 2
        cnt = jnp.minimum(pos + 1, win).astype(F32)
        dlt = acc[halo:, :] / cnt - u[:, sl]
        po_ref[:, sl] = (_dot(dlt.astype(BF16), pw_ref[g]) * sc_ref[:, sl]).astype(po_ref.dtype)
    _qkvo(xb, w_ref, q_ref, k_ref, v_ref, og_ref)
    gt = _dot_nt(wgt_ref[...], xb)
    git_ref[...] = gt[0:SUBLANES]
    gft_ref[...] = gt[SUBLANES:2 * SUBLANES]


def _inproj_seq(x, w_main, wgt, pw, scale, seq, tm):
    n, d = x.shape
    per = seq // tm
    row = lambda c: pl.BlockSpec((tm, c), lambda i: (i, 0))
    full = lambda a: pl.BlockSpec(a.shape, lambda i: (0,) * a.ndim)
    gt_spec = pl.BlockSpec((None, SUBLANES, tm), lambda i: (i // per, 0, i % per))
    gt_shape = jax.ShapeDtypeStruct((n // seq, SUBLANES, seq), F32)
    mw = MLSTM_WIDTH
    out_shape = (
        jax.ShapeDtypeStruct((n, POOL_WIDTH), BF16),
        jax.ShapeDtypeStruct((n // tm, POOL_HALO, POOL_WIDTH), F32),
        jax.ShapeDtypeStruct((n, mw), BF16), jax.ShapeDtypeStruct((n, mw), BF16), jax.ShapeDtypeStruct((n, mw), BF16),
        jax.ShapeDtypeStruct((n, mw), F32), gt_shape, gt_shape,
    )
    out_specs = (row(POOL_WIDTH), pl.BlockSpec((None, POOL_HALO, POOL_WIDTH), lambda i: (i, 0, 0)),
                 row(mw), row(mw), row(mw), row(mw), gt_spec, gt_spec)
    scale = scale.reshape(1, POOL_WIDTH)
    return pl.pallas_call(
        functools.partial(_inproj_seq_kernel, per=per), grid=(n // tm,),
        in_specs=[row(d), full(w_main), full(wgt), full(pw), full(scale)],
        out_specs=out_specs, out_shape=out_shape, scratch_shapes=[pltpu.VMEM((POOL_HALO, POOL_WIDTH), F32)],
        compiler_params=_params("arbitrary"), name="inproj_seq")(x, w_main, wgt, pw, scale)


def _pool_kernel(ext_ref, pw_ref, sc_ref, o_ref, *, period):
    rows = ext_ref.shape[0]
    r = lax.broadcasted_iota(jnp.int32, (rows, POOL_GROUP_DIM), 0)
    if period != rows:
        r = lax.rem(r, period)
    for g, win in enumerate(POOL_WINDOWS):
        sl = slice(g * POOL_GROUP_DIM, (g + 1) * POOL_GROUP_DIM)
        x = ext_ref[:, sl]
        acc = x
        s = 1
        while s < win:
            acc = acc + jnp.where(r >= s, pltpu.roll(acc, s, 0), 0.0)
            s *= 2
        cnt = jnp.minimum(r + 1, win).astype(F32)
        d = acc / cnt - x
        y = _dot(d.astype(BF16), pw_ref[g]) * sc_ref[:, sl]
        o_ref[:, sl] = y.astype(o_ref.dtype)


def _pool(ext2d, pw, scale, period):
    n, c = ext2d.shape
    row = pl.BlockSpec((POOL_ROWS, c), lambda i: (i, 0))
    return pl.pallas_call(
        functools.partial(_pool_kernel, period=period), grid=(n // POOL_ROWS,),
        in_specs=[row, pl.BlockSpec(pw.shape, lambda i: (0, 0, 0)), pl.BlockSpec((1, c), lambda i: (0, 0))],
        out_specs=row, out_shape=jax.ShapeDtypeStruct((n, c), BF16),
        compiler_params=_params("parallel"), name="pool")(ext2d, pw, scale.reshape(1, c))


def _mlstm_chunk_kernel(q_ref, k_ref, v_ref, git_ref, gft_ref, bit_ref, bft_ref, hn_ref, c_ref, n_ref, m_ref,
                        at_s, ws_s, dc_s, mc_s, wi_s, fl_s):
    nb, L, _ = q_ref.shape
    rows = nb * SUBLANES

    @pl.when(pl.program_id(0) == 0)
    def _():
        c_ref[...] = jnp.zeros_like(c_ref)
        n_ref[...] = jnp.zeros_like(n_ref)
        m_ref[...] = jnp.zeros_like(m_ref)

    ig = git_ref[...].reshape(rows, L) + bit_ref[...]
    lf = _log_sigmoid(gft_ref[...].reshape(rows, L) + bft_ref[...])
    bc = _scan(lf, 1, jnp.add, 0.0)
    at = ig - bc
    m_prev = m_ref[...].reshape(rows, LANES)[:, 0:1]
    mc = jnp.maximum(_scan(at, 1, jnp.maximum, -jnp.inf), m_prev)
    mt = bc + mc
    m_last = mc[:, L - 1:L]
    at_s[...] = at.reshape(nb, SUBLANES, L)
    ws_s[...] = jnp.exp(at - m_last).reshape(nb, SUBLANES, L)
    dc_s[...] = jnp.broadcast_to(jnp.exp(m_prev - m_last), (rows, LANES)).reshape(nb, SUBLANES, LANES)
    m_ref[...] = jnp.broadcast_to(mt[:, L - 1:L], (rows, LANES)).reshape(nb, SUBLANES, LANES)
    wi = jnp.exp(m_prev - mc)
    fl = jnp.exp(-mt)
    for b in range(nb):
        rs = slice(b * SUBLANES, (b + 1) * SUBLANES)
        mc_s[b] = mc[rs].T
        wi_s[b] = wi[rs].T
        fl_s[b] = fl[rs].T

    causal = (lax.broadcasted_iota(jnp.int32, (L, L), 0) >= lax.broadcasted_iota(jnp.int32, (L, L), 1))

    def per_batch(b, carry):
        at = at_s[b]
        ws = ws_s[b]
        decay = dc_s[b]
        mc_c, wi_c, fl_c = mc_s[b], wi_s[b], fl_s[b]
        ws16 = jnp.concatenate([ws, ws], axis=0).astype(BF16)
        for h in range(MLSTM_HEADS):
            sl = slice(h * MLSTM_HEAD_DIM, (h + 1) * MLSTM_HEAD_DIM)
            col = slice(h, h + 1)
            qh = q_ref[b, :, sl]
            kh = k_ref[b, :, sl]
            vh = v_ref[b, :, sl]
            s = _dot_nt(qh, kh)
            p = jnp.where(causal, s * jnp.exp(at[col, :] - mc_c[:, col]), 0.0)
            c_old = c_ref[b, h]
            n_old = n_ref[b, col, :]
            wi = wi_c[:, col]
            dv = MLSTM_HEAD_DIM
            v_ext = jnp.concatenate([vh, jnp.ones_like(vh)], axis=1)
            c_ext = jnp.concatenate([c_old, jnp.broadcast_to(n_old, c_old.shape)], axis=0)
            intra = _dot(p.astype(BF16), v_ext)
            inter = _dot_nt(qh, c_ext.astype(BF16))
            num = intra[:, :dv] + wi * inter[:, :dv]
            qn = intra[:, dv:] + wi * inter[:, dv:]
            hh = num / jnp.maximum(jnp.abs(qn), fl_c[:, col])
            hn_ref[b, :, sl] = _unit_norm(hh)
            dc = decay[col, 0:1]
            vts = (vh.astype(F32).T * ws[col, :]).astype(BF16)
            c_ref[b, h] = dc * c_old + _dot(vts, kh)
            n_ref[b, col, :] = dc * n_old + _dot(ws16, kh)[col, :]
        return carry

    lax.fori_loop(0, nb, per_batch, 0)


def _mlstm_chunk(q, k, v, git, gft, bit, bft):
    nb, t, w = q.shape
    L = MLSTM_CHUNK
    seq = lambda c: pl.BlockSpec((nb, L, c), lambda i: (0, i, 0))
    seqt = pl.BlockSpec((nb, SUBLANES, L), lambda i: (0, 0, i))
    vec = lambda a: pl.BlockSpec(a.shape, lambda i: (0, 0))
    out_shape = (
        jax.ShapeDtypeStruct((nb, t, w), F32),
        jax.ShapeDtypeStruct((nb, MLSTM_HEADS, MLSTM_HEAD_DIM, MLSTM_HEAD_DIM), F32),
        jax.ShapeDtypeStruct((nb, MLSTM_HEADS, MLSTM_HEAD_DIM), F32),
        jax.ShapeDtypeStruct((nb, SUBLANES, LANES), F32),
    )
    out_specs = (
        seq(w),
        pl.BlockSpec(out_shape[1].shape, lambda i: (0, 0, 0, 0)),
        pl.BlockSpec(out_shape[2].shape, lambda i: (0, 0, 0)),
        pl.BlockSpec(out_shape[3].shape, lambda i: (0, 0, 0)),
    )
    bit, bft = jnp.tile(bit, (nb, 1)), jnp.tile(bft, (nb, 1))
    scratch = [pltpu.VMEM((nb, SUBLANES, L), F32), pltpu.VMEM((nb, SUBLANES, L), F32),
               pltpu.VMEM((nb, SUBLANES, LANES), F32)] + [pltpu.VMEM((nb, L, SUBLANES), F32) for _ in range(3)]
    return pl.pallas_call(
        _mlstm_chunk_kernel, grid=(t // L,),
        in_specs=[seq(w), seq(w), seq(w), seqt, seqt, vec(bit), vec(bft)],
        out_specs=out_specs, out_shape=out_shape, scratch_shapes=scratch, compiler_params=_params("arbitrary"),
        name="mlstm_chunk")(q, k, v, git, gft, bit, bft)


def _mlstm_step_kernel(q_ref, k_ref, v_ref, gi_ref, gf_ref, bi_ref, bf_ref, c_ref, n_ref, m_ref, c_new_ref,
                       hn_ref, co_ref, no_ref, mo_ref):
    del c_new_ref
    tb = q_ref.shape[0]
    d = MLSTM_HEAD_DIM
    ig = gi_ref[...] + bi_ref[...]
    lf = _log_sigmoid(gf_ref[...] + bf_ref[...])
    m_old = m_ref[...]
    mt = jnp.maximum(lf + m_old, ig)
    mo_ref[...] = mt
    wa = jnp.exp(ig - mt)
    wi = jnp.exp(lf + m_old - mt)
    fl = jnp.exp(-mt)
    wa_t, wi_t, fl_t = wa.T, wi.T, fl.T
    lane = lax.broadcasted_iota(jnp.int32, (d, tb), 1)
    for h in range(MLSTM_HEADS):
        sl = slice(h * d, (h + 1) * d)
        qh, kh, vh = q_ref[:, sl], k_ref[:, sl], v_ref[:, sl]
        nh = n_ref[:, h, :]
        qt, kt, vt, nt = qh.T, kh.T, vh.T, nh.T
        wa_r, wi_r, fl_r = wa_t[h:h + 1, :], wi_t[h:h + 1, :], fl_t[h:h + 1, :]
        s = jnp.sum(qt * kt, 0, keepdims=True) * wa_r
        cq = jnp.zeros((d, tb), F32)
        for b in range(tb):
            cq = jnp.where(lane == b, jnp.sum(c_ref[b, h] * qh[b:b + 1, :], -1, keepdims=True), cq)
        num = s * vt + wi_r * cq
        qn = s + wi_r * jnp.sum(nt * qt, 0, keepdims=True)
        hh = num / jnp.maximum(jnp.abs(qn), fl_r)
        mu = jnp.mean(hh, 0, keepdims=True)
        xc = hh - mu
        var = jnp.mean(xc * xc, 0, keepdims=True)
        hn_ref[:, sl] = (xc * lax.rsqrt(var + LN_EPS)).T
        wav = wa_r * vt
        for b in range(tb):
            co_ref[b, h] = wi_r[:, b:b + 1] * c_ref[b, h] + wav[:, b:b + 1] * kh[b:b + 1, :]
        no_ref[:, h, :] = wi[:, h:h + 1] * nh + wa[:, h:h + 1] * kh


def _mlstm_step(q, k, v, gi, gf, bi, bf, c_all, n_all, m0, c_new, layer, tb):
    n, w = q.shape
    d = MLSTM_HEAD_DIM
    row = lambda c: pl.BlockSpec((tb, c), lambda i: (i, 0))
    vec = lambda a: pl.BlockSpec(a.shape, lambda i: (0, 0))
    cin = pl.BlockSpec((None, tb, MLSTM_HEADS, d, d), lambda i: (layer, i, 0, 0, 0))
    nin = pl.BlockSpec((None, tb, MLSTM_HEADS, d), lambda i: (layer, i, 0, 0))
    nout = pl.BlockSpec((tb, MLSTM_HEADS, d), lambda i: (i, 0, 0))
    out_shape = (
        jax.ShapeDtypeStruct((n, w), F32),
        jax.ShapeDtypeStruct(c_new.shape, F32),
        jax.ShapeDtypeStruct(n_all.shape[1:], F32),
        jax.ShapeDtypeStruct((n, LANES), F32),
    )
    return pl.pallas_call(
        _mlstm_step_kernel, grid=(n // tb,),
        in_specs=[row(w), row(w), row(w), row(LANES), row(LANES), vec(bi), vec(bf), cin, nin, row(LANES),
                  pl.BlockSpec(memory_space=pl.ANY)],
        out_specs=(row(w), cin, nout, row(LANES)), out_shape=out_shape, input_output_aliases={10: 1},
        compiler_params=_params("parallel"), name="mlstm_step")(q, k, v, gi, gf, bi, bf, c_all, n_all, m0, c_new)


def _mixout_kernel(po_ref, hn_ref, og_ref, x_ref, ng_ref, w_ref, g_ref, b_ref, o_ref):
    mo = jax.nn.sigmoid(og_ref[...]) * hn_ref[...] * ng_ref[...]
    mix = _dot(po_ref[...], w_ref[0:POOL_WIDTH, :]) + _dot(mo.astype(BF16), w_ref[POOL_WIDTH:, :])
    o_ref[...] = _layer_norm(ALPHA * x_ref[...] + mix, g_ref[...], b_ref[...])


def _mixout(po, hn, og, x, ng, w, g, b, tm):
    n, d = x.shape
    row = lambda c: pl.BlockSpec((tm, c), lambda i: (i, 0))
    vec = lambda a: pl.BlockSpec(a.shape, lambda i: (0, 0))
    return pl.pallas_call(
        _mixout_kernel, grid=(n // tm,),
        in_specs=[row(POOL_WIDTH), row(MLSTM_WIDTH), row(MLSTM_WIDTH), row(d), vec(ng), vec(w), vec(g), vec(b)],
        out_specs=row(d), out_shape=jax.ShapeDtypeStruct((n, d), F32),
        compiler_params=_params("parallel"), name="mixout")(po, hn, og, x, ng, w, g, b)


def _mm_kernel(x_ref, w_ref, o_ref, *, scale):
    y = _dot(x_ref[...].astype(BF16), w_ref[...])
    if scale != 1.0:
        y = y * scale
    o_ref[...] = y.astype(o_ref.dtype)


def _mm(x, w, tm, out_dtype, scale=1.0, name="proj"):
    n, d = x.shape
    dout = w.shape[1]
    return pl.pallas_call(
        functools.partial(_mm_kernel, scale=scale), grid=(n // tm,),
        in_specs=[pl.BlockSpec((tm, d), lambda i: (i, 0)), pl.BlockSpec(w.shape, lambda i: (0, 0))],
        out_specs=pl.BlockSpec((tm, dout), lambda i: (i, 0)),
        out_shape=jax.ShapeDtypeStruct((n, dout), out_dtype),
        compiler_params=_params("parallel"), name=name)(x, w)


def _cast_kernel(w_ref, o_ref):
    o_ref[...] = w_ref[...].astype(o_ref.dtype)


def _cast_layer(w_all, layer, cols, tr=256):
    rows = w_all.shape[1]
    return pl.pallas_call(
        _cast_kernel, grid=(rows // tr,),
        in_specs=[pl.BlockSpec((None, tr, cols), lambda i: (layer, i, 0))],
        out_specs=pl.BlockSpec((tr, cols), lambda i: (i, 0)),
        out_shape=jax.ShapeDtypeStruct((rows, cols), BF16),
        compiler_params=_params("parallel"), name="weight_cast")(w_all)


def _cast_inproj_kernel(w_ref, o_ref, og_ref):
    w = w_ref[...]
    o_ref[...] = w[:, :GATE_OFF].astype(BF16)
    og_ref[...] = jnp.zeros_like(og_ref)
    og_ref[:, 0:2 * MLSTM_HEADS] = w[:, GATE_OFF:]


def _cast_inproj(w_in, layer, tr=256):
    _, rows, cols = w_in.shape
    return pl.pallas_call(
        _cast_inproj_kernel, grid=(rows // tr,),
        in_specs=[pl.BlockSpec((None, tr, cols), lambda i: (layer, i, 0))],
        out_specs=(pl.BlockSpec((tr, GATE_OFF), lambda i: (i, 0)), pl.BlockSpec((tr, LANES), lambda i: (i, 0))),
        out_shape=(jax.ShapeDtypeStruct((rows, GATE_OFF), BF16), jax.ShapeDtypeStruct((rows, LANES), F32)),
        compiler_params=_params("parallel"), name="inproj_cast")(w_in)


def _mem_proj_kernel(x_ref, wk_ref, wv_ref, k4_ref, v4_ref, k2_ref, v2_ref, wkb, wvb):
    @pl.when(pl.program_id(1) == 0)
    def _():
        wkb[...] = wk_ref[...].astype(BF16)
        wvb[...] = wv_ref[...].astype(BF16)

    xb = x_ref[...].astype(BF16)
    k = _dot(xb, wkb[...])
    v = _dot(xb, wvb[...])
    k2_ref[...] = k
    v2_ref[...] = v
    k4_ref[...] = k.reshape(N_MEM, CA_HEADS, CA_HEAD_DIM)
    v4_ref[...] = v.reshape(N_MEM, CA_HEADS, CA_HEAD_DIM)


def _mem_proj(mem, wk_all, wv_all):
    nb, m, d = mem.shape
    depth = wk_all.shape[0]
    wspec = pl.BlockSpec((None, d, d), lambda l, b: (l, 0, 0))
    o4 = pl.BlockSpec((None, None, m, CA_HEADS, CA_HEAD_DIM), lambda l, b: (l, b, 0, 0, 0))
    o2 = pl.BlockSpec((None, None, m, d), lambda l, b: (l, b, 0, 0))
    s4 = jax.ShapeDtypeStruct((depth, nb, m, CA_HEADS, CA_HEAD_DIM), F32)
    s2 = jax.ShapeDtypeStruct((depth, nb, m, d), F32)
    return pl.pallas_call(
        _mem_proj_kernel, grid=(depth, nb),
        in_specs=[pl.BlockSpec((None, m, d), lambda l, b: (b, 0, 0)), wspec, wspec],
        out_specs=(o4, o4, o2, o2), out_shape=(s4, s4, s2, s2),
        scratch_shapes=[pltpu.VMEM((d, d), BF16), pltpu.VMEM((d, d), BF16)],
        compiler_params=_params("arbitrary", "arbitrary"), name="mem_proj")(mem, wk_all, wv_all)


def _attn_block_kernel(po_ref, hn_ref, og_ref, x_ref, k_ref, v_ref, ng_ref, wm_ref, g1_ref, b1_ref,
                       wq_ref, wo_ref, g_ref, b_ref, o_ref):
    mo = jax.nn.sigmoid(og_ref[...]) * hn_ref[...] * ng_ref[...]
    mix = _dot(po_ref[...], wm_ref[0:POOL_WIDTH, :]) + _dot(mo.astype(BF16), wm_ref[POOL_WIDTH:, :])
    x = _layer_norm(ALPHA * x_ref[...] + mix, g1_ref[...], b1_ref[...])
    qb = (_dot(x.astype(BF16), wq_ref[...]) * (CA_HEAD_DIM ** -0.5)).astype(BF16)
    kb = k_ref[...].astype(BF16)
    vb = v_ref[...].astype(BF16)
    ctx = []
    for h in range(CA_HEADS):
        sl = slice(h * CA_HEAD_DIM, (h + 1) * CA_HEAD_DIM)
        s = _dot_nt(qb[:, sl], kb[:, sl])
        e = jnp.exp(s - jnp.max(s, -1, keepdims=True))
        ctx.append((_dot(e.astype(BF16), vb[:, sl]) / jnp.sum(e, -1, keepdims=True)).astype(BF16))
    y = _dot(jnp.concatenate(ctx, axis=1), wo_ref[...])
    o_ref[...] = _layer_norm(ALPHA * x + y, g_ref[...], b_ref[...])


def _attn_block(po, hn, og, x, k, v, layer, ng, wm, ln1, wq, wo, ln2, seq, tq):
    n, d = x.shape
    per = seq // tq
    row = lambda c: pl.BlockSpec((tq, c), lambda i: (i, 0))
    kspec = pl.BlockSpec((None, None, N_MEM, d), lambda i: (layer, i // per, 0, 0))
    vec = lambda a: pl.BlockSpec(a.shape, lambda i: (0, 0))
    return pl.pallas_call(
        _attn_block_kernel, grid=(n // tq,),
        in_specs=[row(POOL_WIDTH), row(MLSTM_WIDTH), row(MLSTM_WIDTH), row(d), kspec, kspec, vec(ng), vec(wm),
                  vec(ln1[0]), vec(ln1[1]), vec(wq), vec(wo), vec(ln2[0]), vec(ln2[1])],
        out_specs=row(d), out_shape=jax.ShapeDtypeStruct((n, d), F32),
        compiler_params=_params("parallel"), name="attn_block")(po, hn, og, x, k, v, ng, wm, *ln1, wq, wo, *ln2)


def _attn_step_kernel(q_ref, k_ref, v_ref, o_ref):
    bb = q_ref.shape[0]
    rows = N_MEM * CA_HEADS
    lane = lax.broadcasted_iota(jnp.int32, (SUBLANES, rows), 1)
    row = lax.broadcasted_iota(jnp.int32, (SUBLANES, rows), 0)
    own = lax.rem(lane, CA_HEADS) == lax.rem(row, CA_HEADS)
    for j in range(bb):
        kf = k_ref[j].reshape(rows, CA_HEAD_DIM).astype(BF16)
        vf = v_ref[j].reshape(rows, CA_HEAD_DIM).astype(BF16)
        s = jnp.where(own, _dot_nt(q_ref[j].astype(BF16), kf), -jnp.inf)
        e = jnp.exp(s - jnp.max(s, -1, keepdims=True))
        o_ref[j] = _dot(e.astype(BF16), vf) / jnp.sum(e, -1, keepdims=True)


def _attn_step(q, k_all, v_all, layer, bb):
    n = q.shape[0]
    qspec = pl.BlockSpec((bb, SUBLANES, CA_HEAD_DIM), lambda i: (i, 0, 0))
    kspec = pl.BlockSpec((None, bb, N_MEM, CA_HEADS, CA_HEAD_DIM), lambda i: (layer, i, 0, 0, 0))
    return pl.pallas_call(
        _attn_step_kernel, grid=(n // bb,), in_specs=[qspec, kspec, kspec], out_specs=qspec,
        out_shape=jax.ShapeDtypeStruct((n, SUBLANES, CA_HEAD_DIM), F32),
        compiler_params=_params("parallel"), name="attn_step")(q, k_all, v_all)


def _mm_res_ln_kernel(a_ref, x_ref, w_ref, g_ref, b_ref, o_ref):
    y = _dot(a_ref[...].astype(BF16), w_ref[...])
    o_ref[...] = _layer_norm(ALPHA * x_ref[...] + y, g_ref[...], b_ref[...])


def _mm_res_ln(a, x, w, g, b, tm):
    n, d = x.shape
    row = lambda c: pl.BlockSpec((tm, c), lambda i: (i, 0))
    vec = lambda arr: pl.BlockSpec(arr.shape, lambda i: (0, 0))
    return pl.pallas_call(
        _mm_res_ln_kernel, grid=(n // tm,),
        in_specs=[row(a.shape[1]), row(d), vec(w), vec(g), vec(b)], out_specs=row(d),
        out_shape=jax.ShapeDtypeStruct((n, d), F32),
        compiler_params=_params("parallel"), name="proj_res_ln")(a, x, w, g, b)


def _route(lt):
    gl = [lt[g:g + 1, :] for g in range(N_GROUPS)]
    gmax = functools.reduce(jnp.maximum, gl)
    gsum = functools.reduce(jnp.add, [jnp.exp(x - gmax) for x in gl])
    pg_sel = 1.0 / gsum

    def first_max(vals):
        m = functools.reduce(jnp.maximum, vals)
        taken = jnp.zeros_like(m, dtype=jnp.bool_)
        hot = []
        for x in vals:
            h = jnp.logical_and(x == m, jnp.logical_not(taken))
            taken = jnp.logical_or(taken, h)
            hot.append(h)
        return m, hot

    _, g_hot = first_max(gl)
    el = []
    for j in range(EXPERTS_PER_GROUP):
        rows = [lt[SUBLANES + g * EXPERTS_PER_GROUP + j:SUBLANES + g * EXPERTS_PER_GROUP + j + 1, :]
                for g in range(N_GROUPS)]
        x = rows[N_GROUPS - 1]
        for g in range(N_GROUPS - 2, -1, -1):
            x = jnp.where(g_hot[g], rows[g], x)
        el.append(x)
    emax = functools.reduce(jnp.maximum, el)
    ee = [jnp.exp(x - emax) for x in el]
    esum = functools.reduce(jnp.add, ee)
    pe = [x / esum for x in ee]
    p1, hot1 = first_max(pe)
    p2, hot2 = first_max([jnp.where(h, -jnp.inf, x) for h, x in zip(hot1, pe)])
    psum = p1 + p2
    gate = [jnp.where(h1, pg_sel * p1 / psum, jnp.where(h2, pg_sel * p2 / psum, 0.0)) for h1, h2 in zip(hot1, hot2)]
    return g_hot, gate


def _expert_cast_kernel(wg_ref, wu_ref, wd_ref, og_ref, ou_ref, od_ref):
    f = D_EXPERT
    for j in range(EXPERTS_PER_GROUP):
        og_ref[:, j * f:(j + 1) * f] = wg_ref[j].astype(BF16)
        ou_ref[:, j * f:(j + 1) * f] = wu_ref[j].astype(BF16)
        od_ref[j * f:(j + 1) * f, :] = wd_ref[j].astype(BF16)


def _expert_cast(wg, wu, wd, layer):
    d = wg.shape[2]
    e, gf = EXPERTS_PER_GROUP, EXPERTS_PER_GROUP * D_EXPERT
    cols = pl.BlockSpec((None, e, d, D_EXPERT), lambda g: (layer, g, 0, 0))
    rows = pl.BlockSpec((None, e, D_EXPERT, d), lambda g: (layer, g, 0, 0))
    return pl.pallas_call(
        _expert_cast_kernel, grid=(N_GROUPS,), in_specs=[cols, cols, rows],
        out_specs=(pl.BlockSpec((None, d, gf), lambda g: (g, 0, 0)), pl.BlockSpec((None, d, gf), lambda g: (g, 0, 0)),
                   pl.BlockSpec((None, gf, d), lambda g: (g, 0, 0))),
        out_shape=(jax.ShapeDtypeStruct((N_GROUPS, d, gf), BF16), jax.ShapeDtypeStruct((N_GROUPS, d, gf), BF16),
                   jax.ShapeDtypeStruct((N_GROUPS, gf, d), BF16)),
        compiler_params=_params("parallel"), name="expert_cast")(wg, wu, wd)


def _moe_kernel(x_ref, wr_ref, br_ref, tri_ref, wg_ref, wu_ref, wd_ref, g_ref, b_ref, o_ref,
                xb_ref, ct_ref, comb_ref, perm_ref, acc_ref, full_ref):
    grp = pl.program_id(1)
    tm = x_ref.shape[0]
    ns, cap, sw = perm_ref.shape[1:]
    f = D_EXPERT

    @pl.when(grp == 0)
    def _():
        x = x_ref[...]
        xh = x.astype(BF16)
        xb_ref[...] = xh
        xl = (x - xh.astype(F32)).astype(BF16)
        wr = wr_ref[...]
        wh = wr.astype(BF16)
        wl = (wr - wh.astype(F32)).astype(BF16)
        lt = _dot_nt(wh, xh) + (_dot_nt(wl, xh) + _dot_nt(wh, xl)) + br_ref[...]
        g_hot, gate = _route(lt)
        row8 = lax.broadcasted_iota(jnp.int32, (SUBLANES, tm), 0)
        hot8 = jnp.zeros((SUBLANES, tm), F32)
        for g in range(N_GROUPS):
            hot8 = jnp.where(jnp.logical_and(row8 == g, g_hot[g]), 1.0, hot8)
        most = None
        for s in range(ns):
            hs = hot8[:, s * sw:(s + 1) * sw]
            cum = _dot(hs.astype(BF16), tri_ref[...])
            seg_max = jnp.max(cum[:, sw - 1:sw])
            most = seg_max if most is None else jnp.maximum(most, seg_max)
            base = lax.broadcasted_iota(jnp.int32, (SUBLANES, sw), 0).astype(F32) * cap
            code = jnp.sum(hs * (base + cum - 1.0), 0, keepdims=True).astype(jnp.int32)
            for g in range(N_GROUPS):
                r = lax.broadcasted_iota(jnp.int32, (cap, sw), 0) + g * cap
                perm_ref[g, s] = jnp.where(r == code, 1.0, 0.0).astype(BF16)
        full_ref[0] = (most > cap).astype(jnp.int32)
        ct_ref[...] = jnp.zeros_like(ct_ref)
        for j in range(EXPERTS_PER_GROUP):
            ct_ref[j:j + 1, :] = gate[j]
        ct_ref[EXPERTS_PER_GROUP:EXPERTS_PER_GROUP + 1, :] = jnp.sum(hot8 * row8.astype(F32), 0, keepdims=True)
        comb_ref[...] = ct_ref[...].T
        acc_ref[...] = jnp.zeros_like(acc_ref)

    def experts(rows, gates):
        hh = []
        for j in range(EXPERTS_PER_GROUP):
            hg = _dot(rows, wg_ref[:, j * f:(j + 1) * f])
            hu = _dot(rows, wu_ref[:, j * f:(j + 1) * f])
            hh.append((hg * jax.nn.sigmoid(hg) * hu * gates[:, j:j + 1]).astype(BF16))
        return _dot(jnp.concatenate(hh, axis=1), wd_ref[...])

    @pl.when(full_ref[0] == 0)
    def _():
        comb = comb_ref[...]
        ch = comb.astype(BF16)
        cl = (comb - ch.astype(F32)).astype(BF16)
        xb = xb_ref[...]
        perm = [perm_ref[grp, s] for s in range(ns)]
        seg = lambda a, s: a[s * sw:(s + 1) * sw]
        rows = jnp.concatenate([_dot(perm[s], seg(xb, s)) for s in range(ns)], axis=0).astype(BF16)
        gates = jnp.concatenate([_dot(perm[s], seg(ch, s)) + _dot(perm[s], seg(cl, s)) for s in range(ns)], axis=0)
        y = experts(rows, gates).astype(BF16)
        for s in range(ns):
            acc_ref[s * sw:(s + 1) * sw, :] += _dot_tn(perm[s], y[s * cap:(s + 1) * cap])

    @pl.when(full_ref[0] != 0)
    def _():
        comb = comb_ref[...]
        own = comb[:, EXPERTS_PER_GROUP:EXPERTS_PER_GROUP + 1] == grp.astype(F32)
        acc_ref[...] += experts(xb_ref[...], jnp.where(own, comb, 0.0))

    @pl.when(grp == N_GROUPS - 1)
    def _():
        o_ref[...] = _layer_norm(ALPHA * x_ref[...] + acc_ref[...], g_ref[...], b_ref[...])


def _moe(x, wr, br, wg, wu, wd, g, b, tm):
    n, d = x.shape
    ns = max(1, tm // MOE_SEGMENT)
    sw = tm // ns
    cap = min(sw, MOE_CAP)
    tri = (lax.broadcasted_iota(jnp.int32, (sw, sw), 0) <= lax.broadcasted_iota(jnp.int32, (sw, sw), 1)).astype(BF16)
    row = pl.BlockSpec((tm, d), lambda i, e: (i, 0))
    vec = lambda a: pl.BlockSpec(a.shape, lambda i, e: (0, 0))
    grp = lambda a: pl.BlockSpec((None,) + a.shape[1:], lambda i, e: (e, 0, 0))
    scratch = [pltpu.VMEM((tm, d), BF16), pltpu.VMEM((LANES, tm), F32), pltpu.VMEM((tm, LANES), F32),
               pltpu.VMEM((N_GROUPS, ns, cap, sw), BF16), pltpu.VMEM((tm, d), F32), pltpu.SMEM((1,), jnp.int32)]
    return pl.pallas_call(
        _moe_kernel, grid=(n // tm, N_GROUPS),
        in_specs=[row, vec(wr), vec(br), vec(tri), grp(wg), grp(wu), grp(wd), vec(g), vec(b)],
        out_specs=row, out_shape=jax.ShapeDtypeStruct((n, d), F32), scratch_shapes=scratch,
        compiler_params=_params("parallel", "arbitrary"), name="moe")(x, wr, br, tri, wg, wu, wd, g, b)


def _pad_lanes(v):
    return jnp.zeros((1, LANES), F32).at[0, :v.shape[0]].set(v)


def _pad_rows(v):
    return jnp.zeros((SUBLANES, 1), F32).at[:v.shape[0], 0].set(v)


def _layer_weights(l, w_in, b_i, b_f, pool_w, pool_scale, mlstm_norm_g, w_out, ln1_g, ln1_b, ca_wq, ca_wo,
                   ln2_g, ln2_b, w_gr, b_gr, w_er, b_er, w_gate, w_up, w_down, ln3_g, ln3_b):
    d = D_MODEL
    w_main, w_gate_lanes = _cast_inproj(w_in, l)
    w_gate_cols = w_gate_lanes[:, :2 * MLSTM_HEADS]
    wg = jnp.zeros((d, 2 * LANES), F32)
    wg = wg.at[:, 0:MLSTM_HEADS].set(w_gate_cols[:, :MLSTM_HEADS])
    wg = wg.at[:, LANES:LANES + MLSTM_HEADS].set(w_gate_cols[:, MLSTM_HEADS:])
    wgt = jnp.zeros((2 * SUBLANES, d), F32)
    wgt = wgt.at[0:MLSTM_HEADS].set(w_gate_cols[:, :MLSTM_HEADS].T)
    wgt = wgt.at[SUBLANES:SUBLANES + MLSTM_HEADS].set(w_gate_cols[:, MLSTM_HEADS:].T)
    wr = jnp.zeros((ROUTER_ROWS, d), F32)
    wr = wr.at[0:N_GROUPS].set(w_gr[l].T).at[SUBLANES:SUBLANES + N_EXPERTS].set(w_er[l].T)
    br = jnp.zeros((ROUTER_ROWS, 1), F32)
    br = br.at[0:N_GROUPS, 0].set(b_gr[l]).at[SUBLANES:SUBLANES + N_EXPERTS, 0].set(b_er[l])
    row = lambda v: v.reshape(1, -1)
    return dict(
        w_main=w_main, wg=wg.astype(BF16), wgt=wgt.astype(BF16),
        bi=_pad_lanes(b_i[l]), bf=_pad_lanes(b_f[l]), bit=_pad_rows(b_i[l]), bft=_pad_rows(b_f[l]),
        pool_w=pool_w[l].astype(BF16), pool_scale=pool_scale[l], norm_g=row(mlstm_norm_g[l]),
        w_out=_cast_layer(w_out, l, d), ln1=(row(ln1_g[l]), row(ln1_b[l])),
        wq=_cast_layer(ca_wq, l, d), wo=_cast_layer(ca_wo, l, d), ln2=(row(ln2_g[l]), row(ln2_b[l])),
        wr=wr, br=br, experts=_expert_cast(w_gate, w_up, w_down, l), ln3=(row(ln3_g[l]), row(ln3_b[l])))


def _trunk_layer(x, nb, seq, layer, mem, pool_prev, state, p, tm, tm_moe):
    chunked = state is None
    if chunked:
        per = seq // tm
        pooled, tails, q, k, v, og, git, gft = _inproj_seq(x, p["w_main"], p["wgt"], p["pool_w"], p["pool_scale"],
                                                           seq, tm)
        pool_buf = tails.reshape(nb, per, POOL_HALO, POOL_WIDTH)[:, per - 1, POOL_HALO - POOL_BUF:]
    else:
        u, q, k, v, og, gi, gf = _inproj(x, p["w_main"], p["wg"], tm)
        ext = jnp.concatenate([pool_prev, u.reshape(nb, seq, POOL_WIDTH)], axis=1)
        period = ext.shape[1]
        pooled = _pool(ext.reshape(nb * period, POOL_WIDTH), p["pool_w"], p["pool_scale"], period)
        pooled = pooled.reshape(nb, period, POOL_WIDTH)[:, period - seq:].reshape(nb * seq, POOL_WIDTH)
        pool_buf = ext[:, -POOL_BUF:]
    if chunked:
        r3 = lambda a: a.reshape(nb, seq, a.shape[-1])
        hn, c1, n1, m1 = _mlstm_chunk(r3(q), r3(k), r3(v), git, gft, p["bit"], p["bft"])
        hn = hn.reshape(nb * seq, MLSTM_WIDTH)
        m1 = m1[:, :MLSTM_HEADS, 0]
    else:
        c_all, n_all, m0, c_new = state
        m0p = jnp.zeros((nb, LANES), F32).at[:, :MLSTM_HEADS].set(m0)
        hn, c1, n1, m1 = _mlstm_step(q, k, v, gi, gf, p["bi"], p["bf"], c_all, n_all, m0p, c_new, layer, SUBLANES)
        m1 = m1[:, :MLSTM_HEADS]
    if chunked:
        x = _attn_block(pooled, hn, og, x, mem[0], mem[1], layer, p["norm_g"], p["w_out"], p["ln1"], p["wq"],
                        p["wo"], p["ln2"], seq, tm_moe)
    else:
        x = _mixout(pooled, hn, og, x, p["norm_g"], p["w_out"], *p["ln1"], tm)
        qc = _mm(x, p["wq"], tm, F32, scale=CA_HEAD_DIM ** -0.5, name="ca_q")
        qh = jnp.zeros((nb, SUBLANES, CA_HEAD_DIM), F32).at[:, :CA_HEADS].set(qc.reshape(nb, CA_HEADS, CA_HEAD_DIM))
        ctx = _attn_step(qh, mem[0], mem[1], layer, 4)[:, :CA_HEADS].reshape(nb, D_MODEL)
        x = _mm_res_ln(ctx, x, p["wo"], *p["ln2"], tm)
    x = _moe(x, p["wr"], p["br"], *p["experts"], *p["ln3"], tm_moe)
    return x, pool_buf, c1, n1, m1


def kernel(x_prompt, x_sample, mem_prompt, cache_pool, state_mlstm_C, state_mlstm_n, state_mlstm_m,
           cache_mem_k, cache_mem_v, emb_ln_g, emb_ln_b, w_in, b_i, b_f, pool_w, pool_scale,
           mlstm_norm_g, w_out, ln1_g, ln1_b, ca_wq, ca_wk, ca_wv, ca_wo, ln2_g, ln2_b,
           w_gr, b_gr, w_er, b_er, w_gate, w_up, w_down, ln3_g, ln3_b):
    bp, tp, d = x_prompt.shape
    bs, ts, _ = x_sample.shape
    tm_p, tm_s = 512, bs * ts
    xp = _ln(x_prompt.reshape(bp * tp, d), emb_ln_g, emb_ln_b, tm_p)
    xs = _ln(x_sample.reshape(bs * ts, d), emb_ln_g, emb_ln_b, tm_s)
    mk4, mv4, mk2, mv2 = _mem_proj(mem_prompt, ca_wk, ca_wv)
    outs = [[] for _ in range(7)]
    sc = jnp.zeros_like(state_mlstm_C)
    for l in range(DEPTH):
        p = _layer_weights(l, w_in, b_i, b_f, pool_w, pool_scale, mlstm_norm_g, w_out, ln1_g, ln1_b, ca_wq,
                           ca_wo, ln2_g, ln2_b, w_gr, b_gr, w_er, b_er, w_gate, w_up, w_down, ln3_g, ln3_b)
        xp, pb, c1, n1, m1 = _trunk_layer(xp, bp, tp, l, (mk2, mv2), None, None, p, tm_p, 2 * tm_p)
        xs, sb, sc, ns, ms = _trunk_layer(xs, bs, ts, l, (cache_mem_k, cache_mem_v), cache_pool[l],
                                          (state_mlstm_C, state_mlstm_n, state_mlstm_m[l], sc), p, tm_s, tm_s)
        for lst, val in zip(outs, (pb, c1, n1, m1, sb, ns, ms)):
            lst.append(val)
    pp, pc, pn, pm, sp, sn, sm = (jnp.stack(o) for o in outs)
    return (xp.reshape(bp, tp, d), xs.reshape(bs, ts, d), pp, pc, pn, pm, mk4, mv4, sp, sc, sn, sm)
```

```python
import functools

import jax
import jax.numpy as jnp
from jax import lax
from jax.experimental import pallas as pl
from jax.experimental.pallas import tpu as pltpu

F32 = jnp.float32
BF16 = jnp.bfloat16

D_MODEL = 1024
DEPTH = 4
POOL_WIDTH = 512
POOL_GROUPS = 4
POOL_GROUP_DIM = 128
POOL_WINDOWS = (2, 4, 8, 16)
POOL_BUF = 15
MLSTM_WIDTH = 512
MLSTM_HEADS = 4
MLSTM_HEAD_DIM = 128
N_MEM = 256
CA_HEADS = 4
CA_HEAD_DIM = 256
N_GROUPS = 4
EXPERTS_PER_GROUP = 4
N_EXPERTS = 16
D_EXPERT = 256
ALPHA = (2 * DEPTH) ** 0.25
LN_EPS = 1e-5
GATE_OFF = POOL_WIDTH + 4 * MLSTM_WIDTH

LANES = 128
SUBLANES = 8
VMEM_LIMIT = 56 * 1024 * 1024
MLSTM_CHUNK = 256
POOL_ROWS = 2048
POOL_HALO = 16
ROUTER_ROWS = 32
MOE_SEGMENT = 512
MOE_CAP = 160


def _params(*sem):
    return pltpu.CompilerParams(dimension_semantics=sem, vmem_limit_bytes=VMEM_LIMIT)


def _dot(a, b):
    return jnp.dot(a, b, preferred_element_type=F32)


def _dot_nt(a, b, precision=None):
    return lax.dot_general(a, b, (((1,), (1,)), ((), ())), precision=precision,
                           preferred_element_type=F32)


def _dot_tn(a, b):
    return lax.dot_general(a, b, (((0,), (0,)), ((), ())), preferred_element_type=F32)


def _layer_norm(x, g, b):
    mu = jnp.mean(x, -1, keepdims=True)
    xc = x - mu
    var = jnp.mean(xc * xc, -1, keepdims=True)
    return xc * lax.rsqrt(var + LN_EPS) * g + b


def _unit_norm(x):
    mu = jnp.mean(x, -1, keepdims=True)
    xc = x - mu
    var = jnp.mean(xc * xc, -1, keepdims=True)
    return xc * lax.rsqrt(var + LN_EPS)


def _log_sigmoid(x):
    return jnp.minimum(x, 0.0) - jnp.log1p(jnp.exp(-jnp.abs(x)))


def _scan(x, axis, op, fill):
    n = x.shape[axis]
    idx = lax.broadcasted_iota(jnp.int32, x.shape, axis)
    s = 1
    while s < n:
        x = op(x, jnp.where(idx >= s, pltpu.roll(x, s, axis), fill))
        s *= 2
    return x


def _ln_kernel(x_ref, g_ref, b_ref, o_ref):
    o_ref[...] = _layer_norm(x_ref[...], g_ref[...], b_ref[...])


def _ln(x, g, b, tm):
    n, d = x.shape
    row = pl.BlockSpec((tm, d), lambda i: (i, 0))
    vec = pl.BlockSpec((1, d), lambda i: (0, 0))
    return pl.pallas_call(
        _ln_kernel, grid=(n // tm,), in_specs=[row, vec, vec], out_specs=row,
        out_shape=jax.ShapeDtypeStruct((n, d), F32), compiler_params=_params("parallel"),
        name="emb_ln")(x, g.reshape(1, d), b.reshape(1, d))


def _qkvo(xb, w_ref, q_ref, k_ref, v_ref, og_ref):
    w = MLSTM_WIDTH
    q = _dot(xb, w_ref[:, POOL_WIDTH:POOL_WIDTH + w]) * (MLSTM_HEAD_DIM ** -0.5)
    q_ref[...] = q.astype(q_ref.dtype)
    k_ref[...] = _dot(xb, w_ref[:, POOL_WIDTH + w:POOL_WIDTH + 2 * w]).astype(k_ref.dtype)
    v_ref[...] = _dot(xb, w_ref[:, POOL_WIDTH + 2 * w:POOL_WIDTH + 3 * w]).astype(v_ref.dtype)
    og_ref[...] = _dot(xb, w_ref[:, POOL_WIDTH + 3 * w:POOL_WIDTH + 4 * w])


def _inproj_kernel(x_ref, w_ref, wg_ref, u_ref, q_ref, k_ref, v_ref, og_ref, gi_ref, gf_ref):
    xb = x_ref[...].astype(BF16)
    u_ref[...] = _dot(xb, w_ref[:, 0:POOL_WIDTH])
    _qkvo(xb, w_ref, q_ref, k_ref, v_ref, og_ref)
    g = _dot(xb, wg_ref[...])
    gi_ref[...] = g[:, 0:LANES]
    gf_ref[...] = g[:, LANES:2 * LANES]


def _inproj(x, w_main, wg, tm):
    n, d = x.shape
    row = lambda c: pl.BlockSpec((tm, c), lambda i: (i, 0))
    full = lambda a: pl.BlockSpec(a.shape, lambda i: (0, 0))
    widths = (POOL_WIDTH, MLSTM_WIDTH, MLSTM_WIDTH, MLSTM_WIDTH, MLSTM_WIDTH, LANES, LANES)
    return pl.pallas_call(
        _inproj_kernel, grid=(n // tm,), in_specs=[row(d), full(w_main), full(wg)],
        out_specs=tuple(row(c) for c in widths),
        out_shape=tuple(jax.ShapeDtypeStruct((n, c), F32) for c in widths),
        compiler_params=_params("parallel"), name="inproj")(x, w_main, wg)


def _inproj_seq_kernel(x_ref, w_ref, wgt_ref, pw_ref, sc_ref, po_ref, tail_ref, q_ref, k_ref, v_ref, og_ref,
                       git_ref, gft_ref, carry_ref, *, per):
    tm = x_ref.shape[0]
    start = lax.rem(pl.program_id(0), per) * tm

    @pl.when(start == 0)
    def _():
        carry_ref[...] = jnp.zeros_like(carry_ref)

    xb = x_ref[...].astype(BF16)
    u = _dot(xb, w_ref[:, 0:POOL_WIDTH])
    halo = carry_ref.shape[0]
    ext = jnp.concatenate([carry_ref[...], u], axis=0)
    tail = ext[tm:, :]
    carry_ref[...] = tail
    tail_ref[...] = tail
    pos = start + lax.broadcasted_iota(jnp.int32, (tm, POOL_GROUP_DIM), 0)
    for g, win in enumerate(POOL_WINDOWS):
        sl = slice(g * POOL_GROUP_DIM, (g + 1) * POOL_GROUP_DIM)
        acc = ext[:, sl]
        s = 1
        while s < win:
            acc = acc + pltpu.roll(acc, s, 0)
            s *= 2
        cnt = jnp.minimum(pos + 1, win).astype(F32)
        dlt = acc[halo:, :] / cnt - u[:, sl]
        po_ref[:, sl] = (_dot(dlt.astype(BF16), pw_ref[g]) * sc_ref[:, sl]).astype(po_ref.dtype)
    _qkvo(xb, w_ref, q_ref, k_ref, v_ref, og_ref)
    gt = _dot_nt(wgt_ref[...], xb)
    git_ref[...] = gt[0:SUBLANES]
    gft_ref[...] = gt[SUBLANES:2 * SUBLANES]


def _inproj_seq(x, w_main, wgt, pw, scale, seq, tm):
    n, d = x.shape
    per = seq // tm
    row = lambda c: pl.BlockSpec((tm, c), lambda i: (i, 0))
    full = lambda a: pl.BlockSpec(a.shape, lambda i: (0,) * a.ndim)
    gt_spec = pl.BlockSpec((None, SUBLANES, tm), lambda i: (i // per, 0, i % per))
    gt_shape = jax.ShapeDtypeStruct((n // seq, SUBLANES, seq), F32)
    mw = MLSTM_WIDTH
    out_shape = (
        jax.ShapeDtypeStruct((n, POOL_WIDTH), BF16),
        jax.ShapeDtypeStruct((n // tm, POOL_HALO, POOL_WIDTH), F32),
        jax.ShapeDtypeStruct((n, mw), BF16), jax.ShapeDtypeStruct((n, mw), BF16), jax.ShapeDtypeStruct((n, mw), BF16),
        jax.ShapeDtypeStruct((n, mw), F32), gt_shape, gt_shape,
    )
    out_specs = (row(POOL_WIDTH), pl.BlockSpec((None, POOL_HALO, POOL_WIDTH), lambda i: (i, 0, 0)),
                 row(mw), row(mw), row(mw), row(mw), gt_spec, gt_spec)
    scale = scale.reshape(1, POOL_WIDTH)
    return pl.pallas_call(
        functools.partial(_inproj_seq_kernel, per=per), grid=(n // tm,),
        in_specs=[row(d), full(w_main), full(wgt), full(pw), full(scale)],
        out_specs=out_specs, out_shape=out_shape, scratch_shapes=[pltpu.VMEM((POOL_HALO, POOL_WIDTH), F32)],
        compiler_params=_params("arbitrary"), name="inproj_seq")(x, w_main, wgt, pw, scale)


def _pool_kernel(ext_ref, pw_ref, sc_ref, o_ref, *, period):
    rows = ext_ref.shape[0]
    r = lax.broadcasted_iota(jnp.int32, (rows, POOL_GROUP_DIM), 0)
    if period != rows:
        r = lax.rem(r, period)
    for g, win in enumerate(POOL_WINDOWS):
        sl = slice(g * POOL_GROUP_DIM, (g + 1) * POOL_GROUP_DIM)
        x = ext_ref[:, sl]
        acc = x
        s = 1
        while s < win:
            acc = acc + jnp.where(r >= s, pltpu.roll(acc, s, 0), 0.0)
            s *= 2
        cnt = jnp.minimum(r + 1, win).astype(F32)
        d = acc / cnt - x
        y = _dot(d.astype(BF16), pw_ref[g]) * sc_ref[:, sl]
        o_ref[:, sl] = y.astype(o_ref.dtype)


def _pool(ext2d, pw, scale, period):
    n, c = ext2d.shape
    row = pl.BlockSpec((POOL_ROWS, c), lambda i: (i, 0))
    return pl.pallas_call(
        functools.partial(_pool_kernel, period=period), grid=(n // POOL_ROWS,),
        in_specs=[row, pl.BlockSpec(pw.shape, lambda i: (0, 0, 0)), pl.BlockSpec((1, c), lambda i: (0, 0))],
        out_specs=row, out_shape=jax.ShapeDtypeStruct((n, c), BF16),
        compiler_params=_params("parallel"), name="pool")(ext2d, pw, scale.reshape(1, c))


def _mlstm_chunk_kernel(q_ref, k_ref, v_ref, git_ref, gft_ref, bit_ref, bft_ref, hn_ref, c_ref, n_ref, m_ref,
                        at_s, ws_s, dc_s, mc_s, wi_s, fl_s):
    nb, L, _ = q_ref.shape
    rows = nb * SUBLANES

    @pl.when(pl.program_id(0) == 0)
    def _():
        c_ref[...] = jnp.zeros_like(c_ref)
        n_ref[...] = jnp.zeros_like(n_ref)
        m_ref[...] = jnp.zeros_like(m_ref)

    ig = git_ref[...].reshape(rows, L) + bit_ref[...]
    lf = _log_sigmoid(gft_ref[...].reshape(rows, L) + bft_ref[...])
    bc = _scan(lf, 1, jnp.add, 0.0)
    at = ig - bc
    m_prev = m_ref[...].reshape(rows, LANES)[:, 0:1]
    mc = jnp.maximum(_scan(at, 1, jnp.maximum, -jnp.inf), m_prev)
    mt = bc + mc
    m_last = mc[:, L - 1:L]
    at_s[...] = at.reshape(nb, SUBLANES, L)
    ws_s[...] = jnp.exp(at - m_last).reshape(nb, SUBLANES, L)
    dc_s[...] = jnp.broadcast_to(jnp.exp(m_prev - m_last), (rows, LANES)).reshape(nb, SUBLANES, LANES)
    m_ref[...] = jnp.broadcast_to(mt[:, L - 1:L], (rows, LANES)).reshape(nb, SUBLANES, LANES)
    wi = jnp.exp(m_prev - mc)
    fl = jnp.exp(-mt)
    for b in range(nb):
        rs = slice(b * SUBLANES, (b + 1) * SUBLANES)
        mc_s[b] = mc[rs].T
        wi_s[b] = wi[rs].T
        fl_s[b] = fl[rs].T

    causal = (lax.broadcasted_iota(jnp.int32, (L, L), 0) >= lax.broadcasted_iota(jnp.int32, (L, L), 1))

    def per_batch(b, carry):
        at = at_s[b]
        ws = ws_s[b]
        decay = dc_s[b]
        mc_c, wi_c, fl_c = mc_s[b], wi_s[b], fl_s[b]
        ws16 = jnp.concatenate([ws, ws], axis=0).astype(BF16)
        for h in range(MLSTM_HEADS):
            sl = slice(h * MLSTM_HEAD_DIM, (h + 1) * MLSTM_HEAD_DIM)
            col = slice(h, h + 1)
            qh = q_ref[b, :, sl]
            kh = k_ref[b, :, sl]
            vh = v_ref[b, :, sl]
            s = _dot_nt(qh, kh)
            p = jnp.where(causal, s * jnp.exp(at[col, :] - mc_c[:, col]), 0.0)
            c_old = c_ref[b, h]
            n_old = n_ref[b, col, :]
            wi = wi_c[:, col]
            dv = MLSTM_HEAD_DIM
            v_ext = jnp.concatenate([vh, jnp.ones_like(vh)], axis=1)
            c_ext = jnp.concatenate([c_old, jnp.broadcast_to(n_old, c_old.shape)], axis=0)
            intra = _dot(p.astype(BF16), v_ext)
            inter = _dot_nt(qh, c_ext.astype(BF16))
            num = intra[:, :dv] + wi * inter[:, :dv]
            qn = intra[:, dv:] + wi * inter[:, dv:]
            hh = num / jnp.maximum(jnp.abs(qn), fl_c[:, col])
            hn_ref[b, :, sl] = _unit_norm(hh)
            dc = decay[col, 0:1]
            vts = (vh.astype(F32).T * ws[col, :]).astype(BF16)
            c_ref[b, h] = dc * c_old + _dot(vts, kh)
            n_ref[b, col, :] = dc * n_old + _dot(ws16, kh)[col, :]
        return carry

    lax.fori_loop(0, nb, per_batch, 0)


def _mlstm_chunk(q, k, v, git, gft, bit, bft):
    nb, t, w = q.shape
    L = MLSTM_CHUNK
    seq = lambda c: pl.BlockSpec((nb, L, c), lambda i: (0, i, 0))
    seqt = pl.BlockSpec((nb, SUBLANES, L), lambda i: (0, 0, i))
    vec = lambda a: pl.BlockSpec(a.shape, lambda i: (0, 0))
    out_shape = (
        jax.ShapeDtypeStruct((nb, t, w), F32),
        jax.ShapeDtypeStruct((nb, MLSTM_HEADS, MLSTM_HEAD_DIM, MLSTM_HEAD_DIM), F32),
        jax.ShapeDtypeStruct((nb, MLSTM_HEADS, MLSTM_HEAD_DIM), F32),
        jax.ShapeDtypeStruct((nb, SUBLANES, LANES), F32),
    )
    out_specs = (
        seq(w),
        pl.BlockSpec(out_shape[1].shape, lambda i: (0, 0, 0, 0)),
        pl.BlockSpec(out_shape[2].shape, lambda i: (0, 0, 0)),
        pl.BlockSpec(out_shape[3].shape, lambda i: (0, 0, 0)),
    )
    bit, bft = jnp.tile(bit, (nb, 1)), jnp.tile(bft, (nb, 1))
    scratch = [pltpu.VMEM((nb, SUBLANES, L), F32), pltpu.VMEM((nb, SUBLANES, L), F32),
               pltpu.VMEM((nb, SUBLANES, LANES), F32)] + [pltpu.VMEM((nb, L, SUBLANES), F32) for _ in range(3)]
    return pl.pallas_call(
        _mlstm_chunk_kernel, grid=(t // L,),
        in_specs=[seq(w), seq(w), seq(w), seqt, seqt, vec(bit), vec(bft)],
        out_specs=out_specs, out_shape=out_shape, scratch_shapes=scratch, compiler_params=_params("arbitrary"),
        name="mlstm_chunk")(q, k, v, git, gft, bit, bft)


def _mlstm_step_kernel(q_ref, k_ref, v_ref, gi_ref, gf_ref, bi_ref, bf_ref, c_ref, n_ref, m_ref, c_new_ref,
                       hn_ref, co_ref, no_ref, mo_ref):
    del c_new_ref
    tb = q_ref.shape[0]
    d = MLSTM_HEAD_DIM
    ig = gi_ref[...] + bi_ref[...]
    lf = _log_sigmoid(gf_ref[...] + bf_ref[...])
    m_old = m_ref[...]
    mt = jnp.maximum(lf + m_old, ig)
    mo_ref[...] = mt
    wa = jnp.exp(ig - mt)
    wi = jnp.exp(lf + m_old - mt)
    fl = jnp.exp(-mt)
    wa_t, wi_t, fl_t = wa.T, wi.T, fl.T
    lane = lax.broadcasted_iota(jnp.int32, (d, tb), 1)
    for h in range(MLSTM_HEADS):
        sl = slice(h * d, (h + 1) * d)
        qh, kh, vh = q_ref[:, sl], k_ref[:, sl], v_ref[:, sl]
        nh = n_ref[:, h, :]
        qt, kt, vt, nt = qh.T, kh.T, vh.T, nh.T
        wa_r, wi_r, fl_r = wa_t[h:h + 1, :], wi_t[h:h + 1, :], fl_t[h:h + 1, :]
        s = jnp.sum(qt * kt, 0, keepdims=True) * wa_r
        cq = jnp.zeros((d, tb), F32)
        for b in range(tb):
            cq = jnp.where(lane == b, jnp.sum(c_ref[b, h] * qh[b:b + 1, :], -1, keepdims=True), cq)
        num = s * vt + wi_r * cq
        qn = s + wi_r * jnp.sum(nt * qt, 0, keepdims=True)
        hh = num / jnp.maximum(jnp.abs(qn), fl_r)
        mu = jnp.mean(hh, 0, keepdims=True)
        xc = hh - mu
        var = jnp.mean(xc * xc, 0, keepdims=True)
        hn_ref[:, sl] = (xc * lax.rsqrt(var + LN_EPS)).T
        wav = wa_r * vt
        for b in range(tb):
            co_ref[b, h] = wi_r[:, b:b + 1] * c_ref[b, h] + wav[:, b:b + 1] * kh[b:b + 1, :]
        no_ref[:, h, :] = wi[:, h:h + 1] * nh + wa[:, h:h + 1] * kh


def _mlstm_step(q, k, v, gi, gf, bi, bf, c_all, n_all, m0, c_new, layer, tb):
    n, w = q.shape
    d = MLSTM_HEAD_DIM
    row = lambda c: pl.BlockSpec((tb, c), lambda i: (i, 0))
    vec = lambda a: pl.BlockSpec(a.shape, lambda i: (0, 0))
    cin = pl.BlockSpec((None, tb, MLSTM_HEADS, d, d), lambda i: (layer, i, 0, 0, 0))
    nin = pl.BlockSpec((None, tb, MLSTM_HEADS, d), lambda i: (layer, i, 0, 0))
    nout = pl.BlockSpec((tb, MLSTM_HEADS, d), lambda i: (i, 0, 0))
    out_shape = (
        jax.ShapeDtypeStruct((n, w), F32),
        jax.ShapeDtypeStruct(c_new.shape, F32),
        jax.ShapeDtypeStruct(n_all.shape[1:], F32),
        jax.ShapeDtypeStruct((n, LANES), F32),
    )
    return pl.pallas_call(
        _mlstm_step_kernel, grid=(n // tb,),
        in_specs=[row(w), row(w), row(w), row(LANES), row(LANES), vec(bi), vec(bf), cin, nin, row(LANES),
                  pl.BlockSpec(memory_space=pl.ANY)],
        out_specs=(row(w), cin, nout, row(LANES)), out_shape=out_shape, input_output_aliases={10: 1},
        compiler_params=_params("parallel"), name="mlstm_step")(q, k, v, gi, gf, bi, bf, c_all, n_all, m0, c_new)


def _mixout_kernel(po_ref, hn_ref, og_ref, x_ref, ng_ref, w_ref, g_ref, b_ref, o_ref):
    mo = jax.nn.sigmoid(og_ref[...]) * hn_ref[...] * ng_ref[...]
    mix = _dot(po_ref[...], w_ref[0:POOL_WIDTH, :]) + _dot(mo.astype(BF16), w_ref[POOL_WIDTH:, :])
    o_ref[...] = _layer_norm(ALPHA * x_ref[...] + mix, g_ref[...], b_ref[...])


def _mixout(po, hn, og, x, ng, w, g, b, tm):
    n, d = x.shape
    row = lambda c: pl.BlockSpec((tm, c), lambda i: (i, 0))
    vec = lambda a: pl.BlockSpec(a.shape, lambda i: (0, 0))
    return pl.pallas_call(
        _mixout_kernel, grid=(n // tm,),
        in_specs=[row(POOL_WIDTH), row(MLSTM_WIDTH), row(MLSTM_WIDTH), row(d), vec(ng), vec(w), vec(g), vec(b)],
        out_specs=row(d), out_shape=jax.ShapeDtypeStruct((n, d), F32),
        compiler_params=_params("parallel"), name="mixout")(po, hn, og, x, ng, w, g, b)


def _mm_kernel(x_ref, w_ref, o_ref, *, scale):
    y = _dot(x_ref[...].astype(BF16), w_ref[...])
    if scale != 1.0:
        y = y * scale
    o_ref[...] = y.astype(o_ref.dtype)


def _mm(x, w, tm, out_dtype, scale=1.0, name="proj"):
    n, d = x.shape
    dout = w.shape[1]
    return pl.pallas_call(
        functools.partial(_mm_kernel, scale=scale), grid=(n // tm,),
        in_specs=[pl.BlockSpec((tm, d), lambda i: (i, 0)), pl.BlockSpec(w.shape, lambda i: (0, 0))],
        out_specs=pl.BlockSpec((tm, dout), lambda i: (i, 0)),
        out_shape=jax.ShapeDtypeStruct((n, dout), out_dtype),
        compiler_params=_params("parallel"), name=name)(x, w)


def _cast_kernel(w_ref, o_ref):
    o_ref[...] = w_ref[...].astype(o_ref.dtype)


def _cast_layer(w_all, layer, cols, tr=256):
    rows = w_all.shape[1]
    return pl.pallas_call(
        _cast_kernel, grid=(rows // tr,),
        in_specs=[pl.BlockSpec((None, tr, cols), lambda i: (layer, i, 0))],
        out_specs=pl.BlockSpec((tr, cols), lambda i: (i, 0)),
        out_shape=jax.ShapeDtypeStruct((rows, cols), BF16),
        compiler_params=_params("parallel"), name="weight_cast")(w_all)


def _cast_inproj_kernel(wt_ref, o_ref, og_ref, *, n_main):
    i = pl.program_id(0)

    @pl.when(i < n_main)
    def _():
        o_ref[...] = wt_ref[...].T.astype(BF16)

    @pl.when(i == n_main)
    def _():
        og_ref[...] = wt_ref[0:2 * MLSTM_HEADS, :]


def _cast_inproj(w_in, layer, tc=256):
    wt = jnp.swapaxes(w_in, 1, 2)
    d = wt.shape[2]
    n_main = GATE_OFF // tc
    return pl.pallas_call(
        functools.partial(_cast_inproj_kernel, n_main=n_main), grid=(n_main + 1,),
        in_specs=[pl.BlockSpec((None, tc, d), lambda i: (layer, i, 0))],
        out_specs=(pl.BlockSpec((d, tc), lambda i: (0, jnp.minimum(i, n_main - 1))),
                   pl.BlockSpec((2 * MLSTM_HEADS, d), lambda i: (0, 0))),
        out_shape=(jax.ShapeDtypeStruct((d, GATE_OFF), BF16), jax.ShapeDtypeStruct((2 * MLSTM_HEADS, d), F32)),
        compiler_params=_params("arbitrary"), name="inproj_cast")(wt)


def _mem_proj_kernel(x_ref, wk_ref, wv_ref, k4_ref, v4_ref, k2_ref, v2_ref, wkb, wvb):
    @pl.when(pl.program_id(1) == 0)
    def _():
        wkb[...] = wk_ref[...].astype(BF16)
        wvb[...] = wv_ref[...].astype(BF16)

    bb, m, d = x_ref.shape
    xb = x_ref[...].reshape(bb * m, d).astype(BF16)
    k = _dot(xb, wkb[...])
    v = _dot(xb, wvb[...])
    k2_ref[...] = k.reshape(bb, m, d)
    v2_ref[...] = v.reshape(bb, m, d)
    k4_ref[...] = k.reshape(bb, m, CA_HEADS, CA_HEAD_DIM)
    v4_ref[...] = v.reshape(bb, m, CA_HEADS, CA_HEAD_DIM)


def _mem_proj(mem, wk_all, wv_all, bb=2):
    nb, m, d = mem.shape
    depth = wk_all.shape[0]
    wspec = pl.BlockSpec((None, d, d), lambda l, b: (l, 0, 0))
    o4 = pl.BlockSpec((None, bb, m, CA_HEADS, CA_HEAD_DIM), lambda l, b: (l, b, 0, 0, 0))
    o2 = pl.BlockSpec((None, bb, m, d), lambda l, b: (l, b, 0, 0))
    s4 = jax.ShapeDtypeStruct((depth, nb, m, CA_HEADS, CA_HEAD_DIM), F32)
    s2 = jax.ShapeDtypeStruct((depth, nb, m, d), F32)
    return pl.pallas_call(
        _mem_proj_kernel, grid=(depth, nb // bb),
        in_specs=[pl.BlockSpec((bb, m, d), lambda l, b: (b, 0, 0)), wspec, wspec],
        out_specs=(o4, o4, o2, o2), out_shape=(s4, s4, s2, s2),
        scratch_shapes=[pltpu.VMEM((d, d), BF16), pltpu.VMEM((d, d), BF16)],
        compiler_params=_params("arbitrary", "arbitrary"), name="mem_proj")(mem, wk_all, wv_all)


def _attn_block_kernel(po_ref, hn_ref, og_ref, x_ref, k_ref, v_ref, ng_ref, wm_ref, g1_ref, b1_ref,
                       wq_ref, wo_ref, g_ref, b_ref, o_ref):
    mo = jax.nn.sigmoid(og_ref[...]) * hn_ref[...] * ng_ref[...]
    mix = _dot(po_ref[...], wm_ref[0:POOL_WIDTH, :]) + _dot(mo.astype(BF16), wm_ref[POOL_WIDTH:, :])
    x = _layer_norm(ALPHA * x_ref[...] + mix, g1_ref[...], b1_ref[...])
    qb = (_dot(x.astype(BF16), wq_ref[...]) * (CA_HEAD_DIM ** -0.5)).astype(BF16)
    kb = k_ref[...].astype(BF16)
    vb = v_ref[...].astype(BF16)
    ctx = []
    for h in range(CA_HEADS):
        sl = slice(h * CA_HEAD_DIM, (h + 1) * CA_HEAD_DIM)
        s = _dot_nt(qb[:, sl], kb[:, sl])
        e = jnp.exp(s - jnp.max(s, -1, keepdims=True))
        ctx.append((_dot(e.astype(BF16), vb[:, sl]) / jnp.sum(e, -1, keepdims=True)).astype(BF16))
    y = _dot(jnp.concatenate(ctx, axis=1), wo_ref[...])
    o_ref[...] = _layer_norm(ALPHA * x + y, g_ref[...], b_ref[...])


def _attn_block(po, hn, og, x, k, v, layer, ng, wm, ln1, wq, wo, ln2, seq, tq):
    n, d = x.shape
    per = seq // tq
    row = lambda c: pl.BlockSpec((tq, c), lambda i: (i, 0))
    kspec = pl.BlockSpec((None, None, N_MEM, d), lambda i: (layer, i // per, 0, 0))
    vec = lambda a: pl.BlockSpec(a.shape, lambda i: (0, 0))
    return pl.pallas_call(
        _attn_block_kernel, grid=(n // tq,),
        in_specs=[row(POOL_WIDTH), row(MLSTM_WIDTH), row(MLSTM_WIDTH), row(d), kspec, kspec, vec(ng), vec(wm),
                  vec(ln1[0]), vec(ln1[1]), vec(wq), vec(wo), vec(ln2[0]), vec(ln2[1])],
        out_specs=row(d), out_shape=jax.ShapeDtypeStruct((n, d), F32),
        compiler_params=_params("parallel"), name="attn_block")(po, hn, og, x, k, v, ng, wm, *ln1, wq, wo, *ln2)


def _attn_step_kernel(q_ref, k_ref, v_ref, o_ref):
    bb = q_ref.shape[0]
    rows = N_MEM * CA_HEADS
    lane = lax.broadcasted_iota(jnp.int32, (SUBLANES, rows), 1)
    row = lax.broadcasted_iota(jnp.int32, (SUBLANES, rows), 0)
    own = lax.rem(lane, CA_HEADS) == lax.rem(row, CA_HEADS)
    for j in range(bb):
        kf = k_ref[j].reshape(rows, CA_HEAD_DIM).astype(BF16)
        vf = v_ref[j].reshape(rows, CA_HEAD_DIM).astype(BF16)
        s = jnp.where(own, _dot_nt(q_ref[j].astype(BF16), kf), -jnp.inf)
        e = jnp.exp(s - jnp.max(s, -1, keepdims=True))
        o_ref[j] = _dot(e.astype(BF16), vf) / jnp.sum(e, -1, keepdims=True)


def _attn_step(q, k_all, v_all, layer, bb):
    n = q.shape[0]
    qspec = pl.BlockSpec((bb, SUBLANES, CA_HEAD_DIM), lambda i: (i, 0, 0))
    kspec = pl.BlockSpec((None, bb, N_MEM, CA_HEADS, CA_HEAD_DIM), lambda i: (layer, i, 0, 0, 0))
    return pl.pallas_call(
        _attn_step_kernel, grid=(n // bb,), in_specs=[qspec, kspec, kspec], out_specs=qspec,
        out_shape=jax.ShapeDtypeStruct((n, SUBLANES, CA_HEAD_DIM), F32),
        compiler_params=_params("parallel"), name="attn_step")(q, k_all, v_all)


def _mm_res_ln_kernel(a_ref, x_ref, w_ref, g_ref, b_ref, o_ref):
    y = _dot(a_ref[...].astype(BF16), w_ref[...])
    o_ref[...] = _layer_norm(ALPHA * x_ref[...] + y, g_ref[...], b_ref[...])


def _mm_res_ln(a, x, w, g, b, tm):
    n, d = x.shape
    row = lambda c: pl.BlockSpec((tm, c), lambda i: (i, 0))
    vec = lambda arr: pl.BlockSpec(arr.shape, lambda i: (0, 0))
    return pl.pallas_call(
        _mm_res_ln_kernel, grid=(n // tm,),
        in_specs=[row(a.shape[1]), row(d), vec(w), vec(g), vec(b)], out_specs=row(d),
        out_shape=jax.ShapeDtypeStruct((n, d), F32),
        compiler_params=_params("parallel"), name="proj_res_ln")(a, x, w, g, b)


def _route(lt):
    gl = [lt[g:g + 1, :] for g in range(N_GROUPS)]
    gmax = functools.reduce(jnp.maximum, gl)
    gsum = functools.reduce(jnp.add, [jnp.exp(x - gmax) for x in gl])
    pg_sel = 1.0 / gsum

    def first_max(vals):
        m = functools.reduce(jnp.maximum, vals)
        taken = jnp.zeros_like(m, dtype=jnp.bool_)
        hot = []
        for x in vals:
            h = jnp.logical_and(x == m, jnp.logical_not(taken))
            taken = jnp.logical_or(taken, h)
            hot.append(h)
        return m, hot

    _, g_hot = first_max(gl)
    el = []
    for j in range(EXPERTS_PER_GROUP):
        rows = [lt[SUBLANES + g * EXPERTS_PER_GROUP + j:SUBLANES + g * EXPERTS_PER_GROUP + j + 1, :]
                for g in range(N_GROUPS)]
        x = rows[N_GROUPS - 1]
        for g in range(N_GROUPS - 2, -1, -1):
            x = jnp.where(g_hot[g], rows[g], x)
        el.append(x)
    emax = functools.reduce(jnp.maximum, el)
    ee = [jnp.exp(x - emax) for x in el]
    esum = functools.reduce(jnp.add, ee)
    pe = [x / esum for x in ee]
    p1, hot1 = first_max(pe)
    p2, hot2 = first_max([jnp.where(h, -jnp.inf, x) for h, x in zip(hot1, pe)])
    psum = p1 + p2
    gate = [jnp.where(h1, pg_sel * p1 / psum, jnp.where(h2, pg_sel * p2 / psum, 0.0)) for h1, h2 in zip(hot1, hot2)]
    return g_hot, gate


def _expert_cast_kernel(wg_ref, wu_ref, wd_ref, og_ref, ou_ref, od_ref):
    f = D_EXPERT
    for j in range(EXPERTS_PER_GROUP):
        og_ref[:, j * f:(j + 1) * f] = wg_ref[j].astype(BF16)
        ou_ref[:, j * f:(j + 1) * f] = wu_ref[j].astype(BF16)
        od_ref[j * f:(j + 1) * f, :] = wd_ref[j].astype(BF16)


def _expert_cast(wg, wu, wd, layer):
    d = wg.shape[2]
    e, gf = EXPERTS_PER_GROUP, EXPERTS_PER_GROUP * D_EXPERT
    cols = pl.BlockSpec((None, e, d, D_EXPERT), lambda g: (layer, g, 0, 0))
    rows = pl.BlockSpec((None, e, D_EXPERT, d), lambda g: (layer, g, 0, 0))
    return pl.pallas_call(
        _expert_cast_kernel, grid=(N_GROUPS,), in_specs=[cols, cols, rows],
        out_specs=(pl.BlockSpec((None, d, gf), lambda g: (g, 0, 0)), pl.BlockSpec((None, d, gf), lambda g: (g, 0, 0)),
                   pl.BlockSpec((None, gf, d), lambda g: (g, 0, 0))),
        out_shape=(jax.ShapeDtypeStruct((N_GROUPS, d, gf), BF16), jax.ShapeDtypeStruct((N_GROUPS, d, gf), BF16),
                   jax.ShapeDtypeStruct((N_GROUPS, gf, d), BF16)),
        compiler_params=_params("parallel"), name="expert_cast")(wg, wu, wd)


def _moe_kernel(x_ref, wr_ref, br_ref, tri_ref, wg_ref, wu_ref, wd_ref, g_ref, b_ref, o_ref,
                xb_ref, ct_ref, comb_ref, perm_ref, acc_ref, full_ref):
    grp = pl.program_id(1)
    tm = x_ref.shape[0]
    ns, cap, sw = perm_ref.shape[1:]
    f = D_EXPERT

    @pl.when(grp == 0)
    def _():
        x = x_ref[...]
        xh = x.astype(BF16)
        xb_ref[...] = xh
        xl = (x - xh.astype(F32)).astype(BF16)
        wr = wr_ref[...]
        wh = wr.astype(BF16)
        wl = (wr - wh.astype(F32)).astype(BF16)
        lt = _dot_nt(wh, xh) + (_dot_nt(wl, xh) + _dot_nt(wh, xl)) + br_ref[...]
        g_hot, gate = _route(lt)
        row8 = lax.broadcasted_iota(jnp.int32, (SUBLANES, tm), 0)
        hot8 = jnp.zeros((SUBLANES, tm), F32)
        for g in range(N_GROUPS):
            hot8 = jnp.where(jnp.logical_and(row8 == g, g_hot[g]), 1.0, hot8)
        most = None
        for s in range(ns):
            hs = hot8[:, s * sw:(s + 1) * sw]
            cum = _dot(hs.astype(BF16), tri_ref[...])
            seg_max = jnp.max(cum[:, sw - 1:sw])
            most = seg_max if most is None else jnp.maximum(most, seg_max)
            base = lax.broadcasted_iota(jnp.int32, (SUBLANES, sw), 0).astype(F32) * cap
            code = jnp.sum(hs * (base + cum - 1.0), 0, keepdims=True).astype(jnp.int32)
            for g in range(N_GROUPS):
                r = lax.broadcasted_iota(jnp.int32, (cap, sw), 0) + g * cap
                perm_ref[g, s] = jnp.where(r == code, 1.0, 0.0).astype(BF16)
        full_ref[0] = (most > cap).astype(jnp.int32)
        ct_ref[...] = jnp.zeros_like(ct_ref)
        for j in range(EXPERTS_PER_GROUP):
            ct_ref[j:j + 1, :] = gate[j]
        ct_ref[EXPERTS_PER_GROUP:EXPERTS_PER_GROUP + 1, :] = jnp.sum(hot8 * row8.astype(F32), 0, keepdims=True)
        comb_ref[...] = ct_ref[...].T
        acc_ref[...] = jnp.zeros_like(acc_ref)

    def experts(rows, gates):
        hh = []
        for j in range(EXPERTS_PER_GROUP):
            hg = _dot(rows, wg_ref[:, j * f:(j + 1) * f])
            hu = _dot(rows, wu_ref[:, j * f:(j + 1) * f])
            hh.append((hg * jax.nn.sigmoid(hg) * hu * gates[:, j:j + 1]).astype(BF16))
        return _dot(jnp.concatenate(hh, axis=1), wd_ref[...])

    @pl.when(full_ref[0] == 0)
    def _():
        comb = comb_ref[...]
        ch = comb.astype(BF16)
        cl = (comb - ch.astype(F32)).astype(BF16)
        xb = xb_ref[...]
        perm = [perm_ref[grp, s] for s in range(ns)]
        seg = lambda a, s: a[s * sw:(s + 1) * sw]
        rows = jnp.concatenate([_dot(perm[s], seg(xb, s)) for s in range(ns)], axis=0).astype(BF16)
        gates = jnp.concatenate([_dot(perm[s], seg(ch, s)) + _dot(perm[s], seg(cl, s)) for s in range(ns)], axis=0)
        y = experts(rows, gates).astype(BF16)
        for s in range(ns):
            acc_ref[s * sw:(s + 1) * sw, :] += _dot_tn(perm[s], y[s * cap:(s + 1) * cap])

    @pl.when(full_ref[0] != 0)
    def _():
        comb = comb_ref[...]
        own = comb[:, EXPERTS_PER_GROUP:EXPERTS_PER_GROUP + 1] == grp.astype(F32)
        acc_ref[...] += experts(xb_ref[...], jnp.where(own, comb, 0.0))

    @pl.when(grp == N_GROUPS - 1)
    def _():
        o_ref[...] = _layer_norm(ALPHA * x_ref[...] + acc_ref[...], g_ref[...], b_ref[...])


def _moe(x, wr, br, wg, wu, wd, g, b, tm):
    n, d = x.shape
    ns = max(1, tm // MOE_SEGMENT)
    sw = tm // ns
    cap = min(sw, MOE_CAP)
    tri = (lax.broadcasted_iota(jnp.int32, (sw, sw), 0) <= lax.broadcasted_iota(jnp.int32, (sw, sw), 1)).astype(BF16)
    row = pl.BlockSpec((tm, d), lambda i, e: (i, 0))
    vec = lambda a: pl.BlockSpec(a.shape, lambda i, e: (0, 0))
    grp = lambda a: pl.BlockSpec((None,) + a.shape[1:], lambda i, e: (e, 0, 0))
    scratch = [pltpu.VMEM((tm, d), BF16), pltpu.VMEM((LANES, tm), F32), pltpu.VMEM((tm, LANES), F32),
               pltpu.VMEM((N_GROUPS, ns, cap, sw), BF16), pltpu.VMEM((tm, d), F32), pltpu.SMEM((1,), jnp.int32)]
    return pl.pallas_call(
        _moe_kernel, grid=(n // tm, N_GROUPS),
        in_specs=[row, vec(wr), vec(br), vec(tri), grp(wg), grp(wu), grp(wd), vec(g), vec(b)],
        out_specs=row, out_shape=jax.ShapeDtypeStruct((n, d), F32), scratch_shapes=scratch,
        compiler_params=_params("parallel", "arbitrary"), name="moe")(x, wr, br, tri, wg, wu, wd, g, b)


def _pad_lanes(v):
    return jnp.zeros((1, LANES), F32).at[0, :v.shape[0]].set(v)


def _pad_rows(v):
    return jnp.zeros((SUBLANES, 1), F32).at[:v.shape[0], 0].set(v)


def _layer_weights(l, w_in, b_i, b_f, pool_w, pool_scale, mlstm_norm_g, w_out, ln1_g, ln1_b, ca_wq, ca_wo,
                   ln2_g, ln2_b, w_gr, b_gr, w_er, b_er, w_gate, w_up, w_down, ln3_g, ln3_b):
    d = D_MODEL
    w_main, w_gate_rows = _cast_inproj(w_in, l)
    w_gate_cols = w_gate_rows.T
    wg = jnp.zeros((d, 2 * LANES), F32)
    wg = wg.at[:, 0:MLSTM_HEADS].set(w_gate_cols[:, :MLSTM_HEADS])
    wg = wg.at[:, LANES:LANES + MLSTM_HEADS].set(w_gate_cols[:, MLSTM_HEADS:])
    wgt = jnp.zeros((2 * SUBLANES, d), F32)
    wgt = wgt.at[0:MLSTM_HEADS].set(w_gate_cols[:, :MLSTM_HEADS].T)
    wgt = wgt.at[SUBLANES:SUBLANES + MLSTM_HEADS].set(w_gate_cols[:, MLSTM_HEADS:].T)
    wr = jnp.zeros((ROUTER_ROWS, d), F32)
    wr = wr.at[0:N_GROUPS].set(w_gr[l].T).at[SUBLANES:SUBLANES + N_EXPERTS].set(w_er[l].T)
    br = jnp.zeros((ROUTER_ROWS, 1), F32)
    br = br.at[0:N_GROUPS, 0].set(b_gr[l]).at[SUBLANES:SUBLANES + N_EXPERTS, 0].set(b_er[l])
    row = lambda v: v.reshape(1, -1)
    return dict(
        w_main=w_main, wg=wg.astype(BF16), wgt=wgt.astype(BF16),
        bi=_pad_lanes(b_i[l]), bf=_pad_lanes(b_f[l]), bit=_pad_rows(b_i[l]), bft=_pad_rows(b_f[l]),
        pool_w=pool_w[l].astype(BF16), pool_scale=pool_scale[l], norm_g=row(mlstm_norm_g[l]),
        w_out=_cast_layer(w_out, l, d), ln1=(row(ln1_g[l]), row(ln1_b[l])),
        wq=_cast_layer(ca_wq, l, d), wo=_cast_layer(ca_wo, l, d), ln2=(row(ln2_g[l]), row(ln2_b[l])),
        wr=wr, br=br, experts=_expert_cast(w_gate, w_up, w_down, l), ln3=(row(ln3_g[l]), row(ln3_b[l])))


def _trunk_layer(x, nb, seq, layer, mem, pool_prev, state, p, tm, tm_moe):
    chunked = state is None
    if chunked:
        per = seq // tm
        pooled, tails, q, k, v, og, git, gft = _inproj_seq(x, p["w_main"], p["wgt"], p["pool_w"], p["pool_scale"],
                                                           seq, tm)
        pool_buf = tails.reshape(nb, per, POOL_HALO, POOL_WIDTH)[:, per - 1, POOL_HALO - POOL_BUF:]
    else:
        u, q, k, v, og, gi, gf = _inproj(x, p["w_main"], p["wg"], tm)
        ext = jnp.concatenate([pool_prev, u.reshape(nb, seq, POOL_WIDTH)], axis=1)
        period = ext.shape[1]
        pooled = _pool(ext.reshape(nb * period, POOL_WIDTH), p["pool_w"], p["pool_scale"], period)
        pooled = pooled.reshape(nb, period, POOL_WIDTH)[:, period - seq:].reshape(nb * seq, POOL_WIDTH)
        pool_buf = ext[:, -POOL_BUF:]
    if chunked:
        r3 = lambda a: a.reshape(nb, seq, a.shape[-1])
        hn, c1, n1, m1 = _mlstm_chunk(r3(q), r3(k), r3(v), git, gft, p["bit"], p["bft"])
        hn = hn.reshape(nb * seq, MLSTM_WIDTH)
        m1 = m1[:, :MLSTM_HEADS, 0]
    else:
        c_all, n_all, m0, c_new = state
        m0p = jnp.zeros((nb, LANES), F32).at[:, :MLSTM_HEADS].set(m0)
        hn, c1, n1, m1 = _mlstm_step(q, k, v, gi, gf, p["bi"], p["bf"], c_all, n_all, m0p, c_new, layer, SUBLANES)
        m1 = m1[:, :MLSTM_HEADS]
    if chunked:
        x = _attn_block(pooled, hn, og, x, mem[0], mem[1], layer, p["norm_g"], p["w_out"], p["ln1"], p["wq"],
                        p["wo"], p["ln2"], seq, tm_moe)
    else:
        x = _mixout(pooled, hn, og, x, p["norm_g"], p["w_out"], *p["ln1"], tm)
        qc = _mm(x, p["wq"], tm, F32, scale=CA_HEAD_DIM ** -0.5, name="ca_q")
        qh = jnp.zeros((nb, SUBLANES, CA_HEAD_DIM), F32).at[:, :CA_HEADS].set(qc.reshape(nb, CA_HEADS, CA_HEAD_DIM))
        ctx = _attn_step(qh, mem[0], mem[1], layer, 4)[:, :CA_HEADS].reshape(nb, D_MODEL)
        x = _mm_res_ln(ctx, x, p["wo"], *p["ln2"], tm)
    x = _moe(x, p["wr"], p["br"], *p["experts"], *p["ln3"], tm_moe)
    return x, pool_buf, c1, n1, m1


def kernel(x_prompt, x_sample, mem_prompt, cache_pool, state_mlstm_C, state_mlstm_n, state_mlstm_m,
           cache_mem_k, cache_mem_v, emb_ln_g, emb_ln_b, w_in, b_i, b_f, pool_w, pool_scale,
           mlstm_norm_g, w_out, ln1_g, ln1_b, ca_wq, ca_wk, ca_wv, ca_wo, ln2_g, ln2_b,
           w_gr, b_gr, w_er, b_er, w_gate, w_up, w_down, ln3_g, ln3_b):
    bp, tp, d = x_prompt.shape
    bs, ts, _ = x_sample.shape
    tm_p, tm_s = 512, bs * ts
    xp = _ln(x_prompt.reshape(bp * tp, d), emb_ln_g, emb_ln_b, tm_p)
    xs = _ln(x_sample.reshape(bs * ts, d), emb_ln_g, emb_ln_b, tm_s)
    mk4, mv4, mk2, mv2 = _mem_proj(mem_prompt, ca_wk, ca_wv)
    outs = [[] for _ in range(7)]
    sc = jnp.zeros_like(state_mlstm_C)
    for l in range(DEPTH):
        p = _layer_weights(l, w_in, b_i, b_f, pool_w, pool_scale, mlstm_norm_g, w_out, ln1_g, ln1_b, ca_wq,
                           ca_wo, ln2_g, ln2_b, w_gr, b_gr, w_er, b_er, w_gate, w_up, w_down, ln3_g, ln3_b)
        xp, pb, c1, n1, m1 = _trunk_layer(xp, bp, tp, l, (mk2, mv2), None, None, p, tm_p, 2 * tm_p)
        xs, sb, sc, ns, ms = _trunk_layer(xs, bs, ts, l, (cache_mem_k, cache_mem_v), cache_pool[l],
                                          (state_mlstm_C, state_mlstm_n, state_mlstm_m[l], sc), p, tm_s, tm_s)
        for lst, val in zip(outs, (pb, c1, n1, m1, sb, ns, ms)):
            lst.append(val)
    pp, pc, pn, pm, sp, sn, sm = (jnp.stack(o) for o in outs)
    return (xp.reshape(bp, tp, d), xs.reshape(bs, ts, d), pp, pc, pn, pm, mk4, mv4, sp, sc, sn, sm)
```

```python
import functools

import jax
import jax.numpy as jnp
from jax import lax
from jax.experimental import pallas as pl
from jax.experimental.pallas import tpu as pltpu

F32 = jnp.float32
BF16 = jnp.bfloat16

D_MODEL = 1024
DEPTH = 4
POOL_WIDTH = 512
POOL_GROUPS = 4
POOL_GROUP_DIM = 128
POOL_WINDOWS = (2, 4, 8, 16)
POOL_BUF = 15
MLSTM_WIDTH = 512
MLSTM_HEADS = 4
MLSTM_HEAD_DIM = 128
N_MEM = 256
CA_HEADS = 4
CA_HEAD_DIM = 256
N_GROUPS = 4
EXPERTS_PER_GROUP = 4
N_EXPERTS = 16
D_EXPERT = 256
ALPHA = (2 * DEPTH) ** 0.25
LN_EPS = 1e-5
GATE_OFF = POOL_WIDTH + 4 * MLSTM_WIDTH

LANES = 128
SUBLANES = 8
VMEM_LIMIT = 56 * 1024 * 1024
MLSTM_CHUNK = 256
POOL_ROWS = 2048
POOL_HALO = 16
ROUTER_ROWS = 32
MOE_SEGMENT = 512
MOE_CAP = 160


def _params(*sem):
    return pltpu.CompilerParams(dimension_semantics=sem, vmem_limit_bytes=VMEM_LIMIT)


def _dot(a, b):
    return jnp.dot(a, b, preferred_element_type=F32)


def _dot_nt(a, b, precision=None):
    return lax.dot_general(a, b, (((1,), (1,)), ((), ())), precision=precision,
                           preferred_element_type=F32)


def _dot_tn(a, b):
    return lax.dot_general(a, b, (((0,), (0,)), ((), ())), preferred_element_type=F32)


def _layer_norm(x, g, b):
    mu = jnp.mean(x, -1, keepdims=True)
    xc = x - mu
    var = jnp.mean(xc * xc, -1, keepdims=True)
    return xc * lax.rsqrt(var + LN_EPS) * g + b


def _unit_norm(x):
    mu = jnp.mean(x, -1, keepdims=True)
    xc = x - mu
    var = jnp.mean(xc * xc, -1, keepdims=True)
    return xc * lax.rsqrt(var + LN_EPS)


def _log_sigmoid(x):
    return jnp.minimum(x, 0.0) - jnp.log1p(jnp.exp(-jnp.abs(x)))


def _scan(x, axis, op, fill):
    n = x.shape[axis]
    idx = lax.broadcasted_iota(jnp.int32, x.shape, axis)
    s = 1
    while s < n:
        x = op(x, jnp.where(idx >= s, pltpu.roll(x, s, axis), fill))
        s *= 2
    return x


def _ln_kernel(x_ref, g_ref, b_ref, o_ref):
    o_ref[...] = _layer_norm(x_ref[...], g_ref[...], b_ref[...])


def _ln(x, g, b, tm):
    n, d = x.shape
    row = pl.BlockSpec((tm, d), lambda i: (i, 0))
    vec = pl.BlockSpec((1, d), lambda i: (0, 0))
    return pl.pallas_call(
        _ln_kernel, grid=(n // tm,), in_specs=[row, vec, vec], out_specs=row,
        out_shape=jax.ShapeDtypeStruct((n, d), F32), compiler_params=_params("parallel"),
        name="emb_ln")(x, g.reshape(1, d), b.reshape(1, d))


def _qkvo(xb, w_ref, q_ref, k_ref, v_ref, og_ref):
    w = MLSTM_WIDTH
    q = _dot(xb, w_ref[:, POOL_WIDTH:POOL_WIDTH + w]) * (MLSTM_HEAD_DIM ** -0.5)
    q_ref[...] = q.astype(q_ref.dtype)
    k_ref[...] = _dot(xb, w_ref[:, POOL_WIDTH + w:POOL_WIDTH + 2 * w]).astype(k_ref.dtype)
    v_ref[...] = _dot(xb, w_ref[:, POOL_WIDTH + 2 * w:POOL_WIDTH + 3 * w]).astype(v_ref.dtype)
    og_ref[...] = _dot(xb, w_ref[:, POOL_WIDTH + 3 * w:POOL_WIDTH + 4 * w])


def _inproj_kernel(x_ref, w_ref, wg_ref, u_ref, q_ref, k_ref, v_ref, og_ref, gi_ref, gf_ref):
    xb = x_ref[...].astype(BF16)
    u_ref[...] = _dot(xb, w_ref[:, 0:POOL_WIDTH])
    _qkvo(xb, w_ref, q_ref, k_ref, v_ref, og_ref)
    g = _dot(xb, wg_ref[...])
    gi_ref[...] = g[:, 0:LANES]
    gf_ref[...] = g[:, LANES:2 * LANES]


def _inproj(x, w_main, wg, tm):
    n, d = x.shape
    row = lambda c: pl.BlockSpec((tm, c), lambda i: (i, 0))
    full = lambda a: pl.BlockSpec(a.shape, lambda i: (0, 0))
    widths = (POOL_WIDTH, MLSTM_WIDTH, MLSTM_WIDTH, MLSTM_WIDTH, MLSTM_WIDTH, LANES, LANES)
    return pl.pallas_call(
        _inproj_kernel, grid=(n // tm,), in_specs=[row(d), full(w_main), full(wg)],
        out_specs=tuple(row(c) for c in widths),
        out_shape=tuple(jax.ShapeDtypeStruct((n, c), F32) for c in widths),
        compiler_params=_params("parallel"), name="inproj")(x, w_main, wg)


def _inproj_seq_kernel(x_ref, w_ref, wgt_ref, pw_ref, sc_ref, po_ref, tail_ref, q_ref, k_ref, v_ref, og_ref,
                       git_ref, gft_ref, carry_ref, *, per):
    tm = x_ref.shape[0]
    start = lax.rem(pl.program_id(0), per) * tm

    @pl.when(start == 0)
    def _():
        carry_ref[...] = jnp.zeros_like(carry_ref)

    xb = x_ref[...].astype(BF16)
    u = _dot(xb, w_ref[:, 0:POOL_WIDTH])
    halo = carry_ref.shape[0]
    ext = jnp.concatenate([carry_ref[...], u], axis=0)
    tail = ext[tm:, :]
    carry_ref[...] = tail
    tail_ref[...] = tail
    pos = start + lax.broadcasted_iota(jnp.int32, (tm, POOL_GROUP_DIM), 0)
    for g, win in enumerate(POOL_WINDOWS):
        sl = slice(g * POOL_GROUP_DIM, (g + 1) * POOL_GROUP_DIM)
        acc = ext[:, sl]
        s = 1
        while s < win:
            acc = acc + pltpu.roll(acc, s, 0)
            s *= 2
        cnt = jnp.minimum(pos + 1, win).astype(F32)
        dlt = acc[halo:, :] / cnt - u[:, sl]
        po_ref[:, sl] = (_dot(dlt.astype(BF16), pw_ref[g]) * sc_ref[:, sl]).astype(po_ref.dtype)
    _qkvo(xb, w_ref, q_ref, k_ref, v_ref, og_ref)
    gt = _dot_nt(wgt_ref[...], xb)
    git_ref[...] = gt[0:SUBLANES]
    gft_ref[...] = gt[SUBLANES:2 * SUBLANES]


def _inproj_seq(x, w_main, wgt, pw, scale, seq, tm):
    n, d = x.shape
    per = seq // tm
    row = lambda c: pl.BlockSpec((tm, c), lambda i: (i, 0))
    full = lambda a: pl.BlockSpec(a.shape, lambda i: (0,) * a.ndim)
    gt_spec = pl.BlockSpec((None, SUBLANES, tm), lambda i: (i // per, 0, i % per))
    gt_shape = jax.ShapeDtypeStruct((n // seq, SUBLANES, seq), F32)
    mw = MLSTM_WIDTH
    out_shape = (
        jax.ShapeDtypeStruct((n, POOL_WIDTH), BF16),
        jax.ShapeDtypeStruct((n // tm, POOL_HALO, POOL_WIDTH), F32),
        jax.ShapeDtypeStruct((n, mw), BF16), jax.ShapeDtypeStruct((n, mw), BF16), jax.ShapeDtypeStruct((n, mw), BF16),
        jax.ShapeDtypeStruct((n, mw), F32), gt_shape, gt_shape,
    )
    out_specs = (row(POOL_WIDTH), pl.BlockSpec((None, POOL_HALO, POOL_WIDTH), lambda i: (i, 0, 0)),
                 row(mw), row(mw), row(mw), row(mw), gt_spec, gt_spec)
    scale = scale.reshape(1, POOL_WIDTH)
    return pl.pallas_call(
        functools.partial(_inproj_seq_kernel, per=per), grid=(n // tm,),
        in_specs=[row(d), full(w_main), full(wgt), full(pw), full(scale)],
        out_specs=out_specs, out_shape=out_shape, scratch_shapes=[pltpu.VMEM((POOL_HALO, POOL_WIDTH), F32)],
        compiler_params=_params("arbitrary"), name="inproj_seq")(x, w_main, wgt, pw, scale)


def _pool_kernel(ext_ref, pw_ref, sc_ref, o_ref, *, period):
    rows = ext_ref.shape[0]
    r = lax.broadcasted_iota(jnp.int32, (rows, POOL_GROUP_DIM), 0)
    if period != rows:
        r = lax.rem(r, period)
    for g, win in enumerate(POOL_WINDOWS):
        sl = slice(g * POOL_GROUP_DIM, (g + 1) * POOL_GROUP_DIM)
        x = ext_ref[:, sl]
        acc = x
        s = 1
        while s < win:
            acc = acc + jnp.where(r >= s, pltpu.roll(acc, s, 0), 0.0)
            s *= 2
        cnt = jnp.minimum(r + 1, win).astype(F32)
        d = acc / cnt - x
        y = _dot(d.astype(BF16), pw_ref[g]) * sc_ref[:, sl]
        o_ref[:, sl] = y.astype(o_ref.dtype)


def _pool(ext2d, pw, scale, period):
    n, c = ext2d.shape
    row = pl.BlockSpec((POOL_ROWS, c), lambda i: (i, 0))
    return pl.pallas_call(
        functools.partial(_pool_kernel, period=period), grid=(n // POOL_ROWS,),
        in_specs=[row, pl.BlockSpec(pw.shape, lambda i: (0, 0, 0)), pl.BlockSpec((1, c), lambda i: (0, 0))],
        out_specs=row, out_shape=jax.ShapeDtypeStruct((n, c), BF16),
        compiler_params=_params("parallel"), name="pool")(ext2d, pw, scale.reshape(1, c))


def _mlstm_chunk_kernel(q_ref, k_ref, v_ref, git_ref, gft_ref, bit_ref, bft_ref, hn_ref, c_ref, n_ref, m_ref,
                        at_s, ws_s, dc_s, mc_s, wi_s, fl_s):
    nb, L, _ = q_ref.shape
    rows = nb * SUBLANES

    @pl.when(pl.program_id(0) == 0)
    def _():
        c_ref[...] = jnp.zeros_like(c_ref)
        n_ref[...] = jnp.zeros_like(n_ref)
        m_ref[...] = jnp.zeros_like(m_ref)

    ig = git_ref[...].reshape(rows, L) + bit_ref[...]
    lf = _log_sigmoid(gft_ref[...].reshape(rows, L) + bft_ref[...])
    tri = (lax.broadcasted_iota(jnp.int32, (L, L), 0) <= lax.broadcasted_iota(jnp.int32, (L, L), 1)).astype(BF16)
    lf_hi = lf.astype(BF16)
    lf_mid = (lf - lf_hi.astype(F32)).astype(BF16)
    lf_lo = (lf - lf_hi.astype(F32) - lf_mid.astype(F32)).astype(BF16)
    bc = _dot(lf_hi, tri) + (_dot(lf_mid, tri) + _dot(lf_lo, tri))
    at = ig - bc
    m_prev = m_ref[...].reshape(rows, LANES)[:, 0:1]
    mc = jnp.maximum(_scan(at, 1, jnp.maximum, -jnp.inf), m_prev)
    mt = bc + mc
    m_last = mc[:, L - 1:L]
    at_s[...] = at.reshape(nb, SUBLANES, L)
    ws_s[...] = jnp.exp(at - m_last).reshape(nb, SUBLANES, L)
    dc_s[...] = jnp.broadcast_to(jnp.exp(m_prev - m_last), (rows, LANES)).reshape(nb, SUBLANES, LANES)
    m_ref[...] = jnp.broadcast_to(mt[:, L - 1:L], (rows, LANES)).reshape(nb, SUBLANES, LANES)
    wi = jnp.exp(m_prev - mc)
    fl = jnp.exp(-mt)
    for b in range(nb):
        rs = slice(b * SUBLANES, (b + 1) * SUBLANES)
        mc_s[b] = mc[rs].T
        wi_s[b] = wi[rs].T
        fl_s[b] = fl[rs].T

    causal = (lax.broadcasted_iota(jnp.int32, (L, L), 0) >= lax.broadcasted_iota(jnp.int32, (L, L), 1))

    def per_batch(heads, b, carry):
        at = at_s[b]
        ws = ws_s[b]
        decay = dc_s[b]
        mc_c, wi_c, fl_c = mc_s[b], wi_s[b], fl_s[b]
        ws16 = jnp.concatenate([ws, ws], axis=0).astype(BF16)
        for h in heads:
            sl = slice(h * MLSTM_HEAD_DIM, (h + 1) * MLSTM_HEAD_DIM)
            col = slice(h, h + 1)
            qh = q_ref[b, :, sl]
            kh = k_ref[b, :, sl]
            vh = v_ref[b, :, sl]
            s = _dot_nt(qh, kh)
            p = jnp.where(causal, s * jnp.exp(at[col, :] - mc_c[:, col]), 0.0)
            c_old = c_ref[b, h]
            n_old = n_ref[b, col, :]
            wi = wi_c[:, col]
            dv = MLSTM_HEAD_DIM
            v_ext = jnp.concatenate([vh, jnp.ones_like(vh)], axis=1)
            c_ext = jnp.concatenate([c_old, jnp.broadcast_to(n_old, c_old.shape)], axis=0)
            intra = _dot(p.astype(BF16), v_ext)
            inter = _dot_nt(qh, c_ext.astype(BF16))
            num = intra[:, :dv] + wi * inter[:, :dv]
            qn = intra[:, dv:] + wi * inter[:, dv:]
            hh = num / jnp.maximum(jnp.abs(qn), fl_c[:, col])
            hn_ref[b, :, sl] = _unit_norm(hh)
            dc = decay[col, 0:1]
            vts = (vh.astype(F32).T * ws[col, :]).astype(BF16)
            c_ref[b, h] = dc * c_old + _dot(vts, kh)
            n_ref[b, col, :] = dc * n_old + _dot(ws16, kh)[col, :]
        return carry

    for h0 in range(0, MLSTM_HEADS, 2):
        lax.fori_loop(0, nb, functools.partial(per_batch, (h0, h0 + 1)), 0)


def _mlstm_chunk(q, k, v, git, gft, bit, bft):
    nb, t, w = q.shape
    L = MLSTM_CHUNK
    seq = lambda c: pl.BlockSpec((nb, L, c), lambda i: (0, i, 0))
    seqt = pl.BlockSpec((nb, SUBLANES, L), lambda i: (0, 0, i))
    vec = lambda a: pl.BlockSpec(a.shape, lambda i: (0, 0))
    out_shape = (
        jax.ShapeDtypeStruct((nb, t, w), F32),
        jax.ShapeDtypeStruct((nb, MLSTM_HEADS, MLSTM_HEAD_DIM, MLSTM_HEAD_DIM), F32),
        jax.ShapeDtypeStruct((nb, MLSTM_HEADS, MLSTM_HEAD_DIM), F32),
        jax.ShapeDtypeStruct((nb, SUBLANES, LANES), F32),
    )
    out_specs = (
        seq(w),
        pl.BlockSpec(out_shape[1].shape, lambda i: (0, 0, 0, 0)),
        pl.BlockSpec(out_shape[2].shape, lambda i: (0, 0, 0)),
        pl.BlockSpec(out_shape[3].shape, lambda i: (0, 0, 0)),
    )
    bit, bft = jnp.tile(bit, (nb, 1)), jnp.tile(bft, (nb, 1))
    scratch = [pltpu.VMEM((nb, SUBLANES, L), F32), pltpu.VMEM((nb, SUBLANES, L), F32),
               pltpu.VMEM((nb, SUBLANES, LANES), F32)] + [pltpu.VMEM((nb, L, SUBLANES), F32) for _ in range(3)]
    return pl.pallas_call(
        _mlstm_chunk_kernel, grid=(t // L,),
        in_specs=[seq(w), seq(w), seq(w), seqt, seqt, vec(bit), vec(bft)],
        out_specs=out_specs, out_shape=out_shape, scratch_shapes=scratch, compiler_params=_params("arbitrary"),
        name="mlstm_chunk")(q, k, v, git, gft, bit, bft)


def _mlstm_step_kernel(q_ref, k_ref, v_ref, gi_ref, gf_ref, bi_ref, bf_ref, c_ref, n_ref, m_ref, c_new_ref,
                       hn_ref, co_ref, no_ref, mo_ref):
    del c_new_ref
    tb = q_ref.shape[0]
    d = MLSTM_HEAD_DIM
    ig = gi_ref[...] + bi_ref[...]
    lf = _log_sigmoid(gf_ref[...] + bf_ref[...])
    m_old = m_ref[...]
    mt = jnp.maximum(lf + m_old, ig)
    mo_ref[...] = mt
    wa = jnp.exp(ig - mt)
    wi = jnp.exp(lf + m_old - mt)
    fl = jnp.exp(-mt)
    wa_t, wi_t, fl_t = wa.T, wi.T, fl.T
    lane = lax.broadcasted_iota(jnp.int32, (d, tb), 1)
    for h in range(MLSTM_HEADS):
        sl = slice(h * d, (h + 1) * d)
        qh, kh, vh = q_ref[:, sl], k_ref[:, sl], v_ref[:, sl]
        nh = n_ref[:, h, :]
        qt, kt, vt, nt = qh.T, kh.T, vh.T, nh.T
        wa_r, wi_r, fl_r = wa_t[h:h + 1, :], wi_t[h:h + 1, :], fl_t[h:h + 1, :]
        s = jnp.sum(qt * kt, 0, keepdims=True) * wa_r
        cq = jnp.zeros((d, tb), F32)
        qtb = qt.astype(BF16)
        for b in range(tb):
            cq = jnp.where(lane == b, _dot(c_ref[b, h].astype(BF16), qtb), cq)
        num = s * vt + wi_r * cq
        qn = s + wi_r * jnp.sum(nt * qt, 0, keepdims=True)
        hh = num / jnp.maximum(jnp.abs(qn), fl_r)
        mu = jnp.mean(hh, 0, keepdims=True)
        xc = hh - mu
        var = jnp.mean(xc * xc, 0, keepdims=True)
        hn_ref[:, sl] = (xc * lax.rsqrt(var + LN_EPS)).T
        wav = wa_r * vt
        for b in range(tb):
            co_ref[b, h] = wi_r[:, b:b + 1] * c_ref[b, h] + wav[:, b:b + 1] * kh[b:b + 1, :]
        no_ref[:, h, :] = wi[:, h:h + 1] * nh + wa[:, h:h + 1] * kh


def _mlstm_step(q, k, v, gi, gf, bi, bf, c_all, n_all, m0, c_new, layer, tb):
    n, w = q.shape
    d = MLSTM_HEAD_DIM
    row = lambda c: pl.BlockSpec((tb, c), lambda i: (i, 0))
    vec = lambda a: pl.BlockSpec(a.shape, lambda i: (0, 0))
    cin = pl.BlockSpec((None, tb, MLSTM_HEADS, d, d), lambda i: (layer, i, 0, 0, 0))
    nin = pl.BlockSpec((None, tb, MLSTM_HEADS, d), lambda i: (layer, i, 0, 0))
    nout = pl.BlockSpec((tb, MLSTM_HEADS, d), lambda i: (i, 0, 0))
    out_shape = (
        jax.ShapeDtypeStruct((n, w), F32),
        jax.ShapeDtypeStruct(c_new.shape, F32),
        jax.ShapeDtypeStruct(n_all.shape[1:], F32),
        jax.ShapeDtypeStruct((n, LANES), F32),
    )
    return pl.pallas_call(
        _mlstm_step_kernel, grid=(n // tb,),
        in_specs=[row(w), row(w), row(w), row(LANES), row(LANES), vec(bi), vec(bf), cin, nin, row(LANES),
                  pl.BlockSpec(memory_space=pl.ANY)],
        out_specs=(row(w), cin, nout, row(LANES)), out_shape=out_shape, input_output_aliases={10: 1},
        compiler_params=_params("parallel"), name="mlstm_step")(q, k, v, gi, gf, bi, bf, c_all, n_all, m0, c_new)


def _mixout_kernel(po_ref, hn_ref, og_ref, x_ref, ng_ref, w_ref, g_ref, b_ref, o_ref):
    mo = jax.nn.sigmoid(og_ref[...]) * hn_ref[...] * ng_ref[...]
    mix = _dot(po_ref[...], w_ref[0:POOL_WIDTH, :]) + _dot(mo.astype(BF16), w_ref[POOL_WIDTH:, :])
    o_ref[...] = _layer_norm(ALPHA * x_ref[...] + mix, g_ref[...], b_ref[...])


def _mixout(po, hn, og, x, ng, w, g, b, tm):
    n, d = x.shape
    row = lambda c: pl.BlockSpec((tm, c), lambda i: (i, 0))
    vec = lambda a: pl.BlockSpec(a.shape, lambda i: (0, 0))
    return pl.pallas_call(
        _mixout_kernel, grid=(n // tm,),
        in_specs=[row(POOL_WIDTH), row(MLSTM_WIDTH), row(MLSTM_WIDTH), row(d), vec(ng), vec(w), vec(g), vec(b)],
        out_specs=row(d), out_shape=jax.ShapeDtypeStruct((n, d), F32),
        compiler_params=_params("parallel"), name="mixout")(po, hn, og, x, ng, w, g, b)


def _mm_kernel(x_ref, w_ref, o_ref, *, scale):
    y = _dot(x_ref[...].astype(BF16), w_ref[...])
    if scale != 1.0:
        y = y * scale
    o_ref[...] = y.astype(o_ref.dtype)


def _mm(x, w, tm, out_dtype, scale=1.0, name="proj"):
    n, d = x.shape
    dout = w.shape[1]
    return pl.pallas_call(
        functools.partial(_mm_kernel, scale=scale), grid=(n // tm,),
        in_specs=[pl.BlockSpec((tm, d), lambda i: (i, 0)), pl.BlockSpec(w.shape, lambda i: (0, 0))],
        out_specs=pl.BlockSpec((tm, dout), lambda i: (i, 0)),
        out_shape=jax.ShapeDtypeStruct((n, dout), out_dtype),
        compiler_params=_params("parallel"), name=name)(x, w)


def _cast_kernel(w_ref, o_ref):
    o_ref[...] = w_ref[...].astype(o_ref.dtype)


def _cast_layer(w_all, layer, cols, tr=256):
    rows = w_all.shape[1]
    return pl.pallas_call(
        _cast_kernel, grid=(rows // tr,),
        in_specs=[pl.BlockSpec((None, tr, cols), lambda i: (layer, i, 0))],
        out_specs=pl.BlockSpec((tr, cols), lambda i: (i, 0)),
        out_shape=jax.ShapeDtypeStruct((rows, cols), BF16),
        compiler_params=_params("parallel"), name="weight_cast")(w_all)


def _cast_inproj_kernel(wt_ref, o_ref, og_ref, *, n_main):
    i = pl.program_id(0)

    @pl.when(i < n_main)
    def _():
        o_ref[...] = wt_ref[...].T.astype(BF16)

    @pl.when(i == n_main)
    def _():
        og_ref[...] = wt_ref[0:2 * MLSTM_HEADS, :]


def _cast_inproj(w_in, layer, tc=256):
    wt = jnp.swapaxes(w_in, 1, 2)
    d = wt.shape[2]
    n_main = GATE_OFF // tc
    return pl.pallas_call(
        functools.partial(_cast_inproj_kernel, n_main=n_main), grid=(n_main + 1,),
        in_specs=[pl.BlockSpec((None, tc, d), lambda i: (layer, i, 0))],
        out_specs=(pl.BlockSpec((d, tc), lambda i: (0, jnp.minimum(i, n_main - 1))),
                   pl.BlockSpec((2 * MLSTM_HEADS, d), lambda i: (0, 0))),
        out_shape=(jax.ShapeDtypeStruct((d, GATE_OFF), BF16), jax.ShapeDtypeStruct((2 * MLSTM_HEADS, d), F32)),
        compiler_params=_params("arbitrary"), name="inproj_cast")(wt)


def _mem_proj_kernel(x_ref, wk_ref, wv_ref, k4_ref, v4_ref, k2_ref, v2_ref, wkb, wvb):
    @pl.when(pl.program_id(1) == 0)
    def _():
        wkb[...] = wk_ref[...].astype(BF16)
        wvb[...] = wv_ref[...].astype(BF16)

    bb, m, d = x_ref.shape
    xb = x_ref[...].reshape(bb * m, d).astype(BF16)
    k = _dot(xb, wkb[...])
    v = _dot(xb, wvb[...])
    k2_ref[...] = k.reshape(bb, m, d)
    v2_ref[...] = v.reshape(bb, m, d)
    k4_ref[...] = k.reshape(bb, m, CA_HEADS, CA_HEAD_DIM)
    v4_ref[...] = v.reshape(bb, m, CA_HEADS, CA_HEAD_DIM)


def _mem_proj(mem, wk_all, wv_all, bb=2):
    nb, m, d = mem.shape
    depth = wk_all.shape[0]
    wspec = pl.BlockSpec((None, d, d), lambda l, b: (l, 0, 0))
    o4 = pl.BlockSpec((None, bb, m, CA_HEADS, CA_HEAD_DIM), lambda l, b: (l, b, 0, 0, 0))
    o2 = pl.BlockSpec((None, bb, m, d), lambda l, b: (l, b, 0, 0))
    s4 = jax.ShapeDtypeStruct((depth, nb, m, CA_HEADS, CA_HEAD_DIM), F32)
    s2 = jax.ShapeDtypeStruct((depth, nb, m, d), F32)
    return pl.pallas_call(
        _mem_proj_kernel, grid=(depth, nb // bb),
        in_specs=[pl.BlockSpec((bb, m, d), lambda l, b: (b, 0, 0)), wspec, wspec],
        out_specs=(o4, o4, o2, o2), out_shape=(s4, s4, s2, s2),
        scratch_shapes=[pltpu.VMEM((d, d), BF16), pltpu.VMEM((d, d), BF16)],
        compiler_params=_params("arbitrary", "arbitrary"), name="mem_proj")(mem, wk_all, wv_all)


def _attn_block_kernel(po_ref, hn_ref, og_ref, x_ref, k_ref, v_ref, ng_ref, wm_ref, g1_ref, b1_ref,
                       wq_ref, wo_ref, g_ref, b_ref, o_ref):
    mo = jax.nn.sigmoid(og_ref[...]) * hn_ref[...] * ng_ref[...]
    mix = _dot(po_ref[...], wm_ref[0:POOL_WIDTH, :]) + _dot(mo.astype(BF16), wm_ref[POOL_WIDTH:, :])
    x = _layer_norm(ALPHA * x_ref[...] + mix, g1_ref[...], b1_ref[...])
    qb = (_dot(x.astype(BF16), wq_ref[...]) * (CA_HEAD_DIM ** -0.5)).astype(BF16)
    kb = k_ref[...].astype(BF16)
    vb = v_ref[...].astype(BF16)
    ctx = []
    for h in range(CA_HEADS):
        sl = slice(h * CA_HEAD_DIM, (h + 1) * CA_HEAD_DIM)
        s = _dot_nt(qb[:, sl], kb[:, sl])
        e = jnp.exp(s - jnp.max(s, -1, keepdims=True))
        ctx.append((_dot(e.astype(BF16), vb[:, sl]) / jnp.sum(e, -1, keepdims=True)).astype(BF16))
    y = _dot(jnp.concatenate(ctx, axis=1), wo_ref[...])
    o_ref[...] = _layer_norm(ALPHA * x + y, g_ref[...], b_ref[...])


def _attn_block(po, hn, og, x, k, v, layer, ng, wm, ln1, wq, wo, ln2, seq, tq):
    n, d = x.shape
    per = seq // tq
    row = lambda c: pl.BlockSpec((tq, c), lambda i: (i, 0))
    kspec = pl.BlockSpec((None, None, N_MEM, d), lambda i: (layer, i // per, 0, 0))
    vec = lambda a: pl.BlockSpec(a.shape, lambda i: (0, 0))
    return pl.pallas_call(
        _attn_block_kernel, grid=(n // tq,),
        in_specs=[row(POOL_WIDTH), row(MLSTM_WIDTH), row(MLSTM_WIDTH), row(d), kspec, kspec, vec(ng), vec(wm),
                  vec(ln1[0]), vec(ln1[1]), vec(wq), vec(wo), vec(ln2[0]), vec(ln2[1])],
        out_specs=row(d), out_shape=jax.ShapeDtypeStruct((n, d), F32),
        compiler_params=_params("parallel"), name="attn_block")(po, hn, og, x, k, v, ng, wm, *ln1, wq, wo, *ln2)


def _attn_step_kernel(q_ref, k_ref, v_ref, o_ref):
    bb = q_ref.shape[0]
    rows = N_MEM * CA_HEADS
    lane = lax.broadcasted_iota(jnp.int32, (SUBLANES, rows), 1)
    row = lax.broadcasted_iota(jnp.int32, (SUBLANES, rows), 0)
    own = lax.rem(lane, CA_HEADS) == lax.rem(row, CA_HEADS)
    for j in range(bb):
        kf = k_ref[j].reshape(rows, CA_HEAD_DIM).astype(BF16)
        vf = v_ref[j].reshape(rows, CA_HEAD_DIM).astype(BF16)
        s = jnp.where(own, _dot_nt(q_ref[j].astype(BF16), kf), -jnp.inf)
        e = jnp.exp(s - jnp.max(s, -1, keepdims=True))
        o_ref[j] = _dot(e.astype(BF16), vf) / jnp.sum(e, -1, keepdims=True)


def _attn_step(q, k_all, v_all, layer, bb):
    n = q.shape[0]
    qspec = pl.BlockSpec((bb, SUBLANES, CA_HEAD_DIM), lambda i: (i, 0, 0))
    kspec = pl.BlockSpec((None, bb, N_MEM, CA_HEADS, CA_HEAD_DIM), lambda i: (layer, i, 0, 0, 0))
    return pl.pallas_call(
        _attn_step_kernel, grid=(n // bb,), in_specs=[qspec, kspec, kspec], out_specs=qspec,
        out_shape=jax.ShapeDtypeStruct((n, SUBLANES, CA_HEAD_DIM), F32),
        compiler_params=_params("parallel"), name="attn_step")(q, k_all, v_all)


def _mm_res_ln_kernel(a_ref, x_ref, w_ref, g_ref, b_ref, o_ref):
    y = _dot(a_ref[...].astype(BF16), w_ref[...])
    o_ref[...] = _layer_norm(ALPHA * x_ref[...] + y, g_ref[...], b_ref[...])


def _mm_res_ln(a, x, w, g, b, tm):
    n, d = x.shape
    row = lambda c: pl.BlockSpec((tm, c), lambda i: (i, 0))
    vec = lambda arr: pl.BlockSpec(arr.shape, lambda i: (0, 0))
    return pl.pallas_call(
        _mm_res_ln_kernel, grid=(n // tm,),
        in_specs=[row(a.shape[1]), row(d), vec(w), vec(g), vec(b)], out_specs=row(d),
        out_shape=jax.ShapeDtypeStruct((n, d), F32),
        compiler_params=_params("parallel"), name="proj_res_ln")(a, x, w, g, b)


def _route(lt):
    gl = [lt[g:g + 1, :] for g in range(N_GROUPS)]
    gmax = functools.reduce(jnp.maximum, gl)
    gsum = functools.reduce(jnp.add, [jnp.exp(x - gmax) for x in gl])
    pg_sel = 1.0 / gsum

    def first_max(vals):
        m = functools.reduce(jnp.maximum, vals)
        taken = jnp.zeros_like(m, dtype=jnp.bool_)
        hot = []
        for x in vals:
            h = jnp.logical_and(x == m, jnp.logical_not(taken))
            taken = jnp.logical_or(taken, h)
            hot.append(h)
        return m, hot

    _, g_hot = first_max(gl)
    el = []
    for j in range(EXPERTS_PER_GROUP):
        rows = [lt[SUBLANES + g * EXPERTS_PER_GROUP + j:SUBLANES + g * EXPERTS_PER_GROUP + j + 1, :]
                for g in range(N_GROUPS)]
        x = rows[N_GROUPS - 1]
        for g in range(N_GROUPS - 2, -1, -1):
            x = jnp.where(g_hot[g], rows[g], x)
        el.append(x)
    emax = functools.reduce(jnp.maximum, el)
    ee = [jnp.exp(x - emax) for x in el]
    esum = functools.reduce(jnp.add, ee)
    pe = [x / esum for x in ee]
    p1, hot1 = first_max(pe)
    p2, hot2 = first_max([jnp.where(h, -jnp.inf, x) for h, x in zip(hot1, pe)])
    psum = p1 + p2
    gate = [jnp.where(h1, pg_sel * p1 / psum, jnp.where(h2, pg_sel * p2 / psum, 0.0)) for h1, h2 in zip(hot1, hot2)]
    return g_hot, gate


def _expert_cast_kernel(wg_ref, wu_ref, wd_ref, og_ref, ou_ref, od_ref):
    f = D_EXPERT
    for j in range(EXPERTS_PER_GROUP):
        og_ref[:, j * f:(j + 1) * f] = wg_ref[j].astype(BF16)
        ou_ref[:, j * f:(j + 1) * f] = wu_ref[j].astype(BF16)
        od_ref[j * f:(j + 1) * f, :] = wd_ref[j].astype(BF16)


def _expert_cast(wg, wu, wd, layer):
    d = wg.shape[2]
    e, gf = EXPERTS_PER_GROUP, EXPERTS_PER_GROUP * D_EXPERT
    cols = pl.BlockSpec((None, e, d, D_EXPERT), lambda g: (layer, g, 0, 0))
    rows = pl.BlockSpec((None, e, D_EXPERT, d), lambda g: (layer, g, 0, 0))
    return pl.pallas_call(
        _expert_cast_kernel, grid=(N_GROUPS,), in_specs=[cols, cols, rows],
        out_specs=(pl.BlockSpec((None, d, gf), lambda g: (g, 0, 0)), pl.BlockSpec((None, d, gf), lambda g: (g, 0, 0)),
                   pl.BlockSpec((None, gf, d), lambda g: (g, 0, 0))),
        out_shape=(jax.ShapeDtypeStruct((N_GROUPS, d, gf), BF16), jax.ShapeDtypeStruct((N_GROUPS, d, gf), BF16),
                   jax.ShapeDtypeStruct((N_GROUPS, gf, d), BF16)),
        compiler_params=_params("parallel"), name="expert_cast")(wg, wu, wd)


def _moe_kernel(x_ref, wr_ref, br_ref, tri_ref, wg_ref, wu_ref, wd_ref, g_ref, b_ref, o_ref,
                xb_ref, ct_ref, comb_ref, perm_ref, acc_ref, full_ref):
    grp = pl.program_id(1)
    tm = x_ref.shape[0]
    ns, cap, sw = perm_ref.shape[1:]
    f = D_EXPERT

    @pl.when(grp == 0)
    def _():
        x = x_ref[...]
        xh = x.astype(BF16)
        xb_ref[...] = xh
        xl = (x - xh.astype(F32)).astype(BF16)
        wr = wr_ref[...]
        wh = wr.astype(BF16)
        wl = (wr - wh.astype(F32)).astype(BF16)
        lt = _dot_nt(wh, xh) + (_dot_nt(wl, xh) + _dot_nt(wh, xl)) + br_ref[...]
        g_hot, gate = _route(lt)
        row8 = lax.broadcasted_iota(jnp.int32, (SUBLANES, tm), 0)
        hot8 = jnp.zeros((SUBLANES, tm), F32)
        for g in range(N_GROUPS):
            hot8 = jnp.where(jnp.logical_and(row8 == g, g_hot[g]), 1.0, hot8)
        most = None
        for s in range(ns):
            hs = hot8[:, s * sw:(s + 1) * sw]
            cum = _dot(hs.astype(BF16), tri_ref[...])
            seg_max = jnp.max(cum[:, sw - 1:sw])
            most = seg_max if most is None else jnp.maximum(most, seg_max)
            base = lax.broadcasted_iota(jnp.int32, (SUBLANES, sw), 0).astype(F32) * cap
            code = jnp.sum(hs * (base + cum - 1.0), 0, keepdims=True).astype(jnp.int32)
            for g in range(N_GROUPS):
                r = lax.broadcasted_iota(jnp.int32, (cap, sw), 0) + g * cap
                perm_ref[g, s] = jnp.where(r == code, 1.0, 0.0).astype(BF16)
        full_ref[0] = (most > cap).astype(jnp.int32)
        ct_ref[...] = jnp.zeros_like(ct_ref)
        for j in range(EXPERTS_PER_GROUP):
            ct_ref[j:j + 1, :] = gate[j]
        ct_ref[EXPERTS_PER_GROUP:EXPERTS_PER_GROUP + 1, :] = jnp.sum(hot8 * row8.astype(F32), 0, keepdims=True)
        comb_ref[...] = ct_ref[...].T
        acc_ref[...] = jnp.zeros_like(acc_ref)

    def experts(rows, gates):
        hh = []
        for j in range(EXPERTS_PER_GROUP):
            hg = _dot(rows, wg_ref[:, j * f:(j + 1) * f])
            hu = _dot(rows, wu_ref[:, j * f:(j + 1) * f])
            hh.append((hg * jax.nn.sigmoid(hg) * hu * gates[:, j:j + 1]).astype(BF16))
        return _dot(jnp.concatenate(hh, axis=1), wd_ref[...])

    @pl.when(full_ref[0] == 0)
    def _():
        comb = comb_ref[...]
        ch = comb.astype(BF16)
        cl = (comb - ch.astype(F32)).astype(BF16)
        xb = xb_ref[...]
        perm = [perm_ref[grp, s] for s in range(ns)]
        seg = lambda a, s: a[s * sw:(s + 1) * sw]
        rows = jnp.concatenate([_dot(perm[s], seg(xb, s)) for s in range(ns)], axis=0).astype(BF16)
        gates = jnp.concatenate([_dot(perm[s], seg(ch, s)) + _dot(perm[s], seg(cl, s)) for s in range(ns)], axis=0)
        y = experts(rows, gates).astype(BF16)
        for s in range(ns):
            acc_ref[s * sw:(s + 1) * sw, :] += _dot_tn(perm[s], y[s * cap:(s + 1) * cap])

    @pl.when(full_ref[0] != 0)
    def _():
        comb = comb_ref[...]
        own = comb[:, EXPERTS_PER_GROUP:EXPERTS_PER_GROUP + 1] == grp.astype(F32)
        acc_ref[...] += experts(xb_ref[...], jnp.where(own, comb, 0.0))

    @pl.when(grp == N_GROUPS - 1)
    def _():
        o_ref[...] = _layer_norm(ALPHA * x_ref[...] + acc_ref[...], g_ref[...], b_ref[...])


def _moe(x, wr, br, wg, wu, wd, g, b, tm):
    n, d = x.shape
    ns = max(1, tm // MOE_SEGMENT)
    sw = tm // ns
    cap = min(sw, MOE_CAP)
    tri = (lax.broadcasted_iota(jnp.int32, (sw, sw), 0) <= lax.broadcasted_iota(jnp.int32, (sw, sw), 1)).astype(BF16)
    row = pl.BlockSpec((tm, d), lambda i, e: (i, 0))
    vec = lambda a: pl.BlockSpec(a.shape, lambda i, e: (0, 0))
    grp = lambda a: pl.BlockSpec((None,) + a.shape[1:], lambda i, e: (e, 0, 0))
    scratch = [pltpu.VMEM((tm, d), BF16), pltpu.VMEM((LANES, tm), F32), pltpu.VMEM((tm, LANES), F32),
               pltpu.VMEM((N_GROUPS, ns, cap, sw), BF16), pltpu.VMEM((tm, d), F32), pltpu.SMEM((1,), jnp.int32)]
    return pl.pallas_call(
        _moe_kernel, grid=(n // tm, N_GROUPS),
        in_specs=[row, vec(wr), vec(br), vec(tri), grp(wg), grp(wu), grp(wd), vec(g), vec(b)],
        out_specs=row, out_shape=jax.ShapeDtypeStruct((n, d), F32), scratch_shapes=scratch,
        compiler_params=_params("parallel", "arbitrary"), name="moe")(x, wr, br, tri, wg, wu, wd, g, b)


def _pad_lanes(v):
    return jnp.zeros((1, LANES), F32).at[0, :v.shape[0]].set(v)


def _pad_rows(v):
    return jnp.zeros((SUBLANES, 1), F32).at[:v.shape[0], 0].set(v)


def _layer_weights(l, w_in, b_i, b_f, pool_w, pool_scale, mlstm_norm_g, w_out, ln1_g, ln1_b, ca_wq, ca_wo,
                   ln2_g, ln2_b, w_gr, b_gr, w_er, b_er, w_gate, w_up, w_down, ln3_g, ln3_b):
    d = D_MODEL
    w_main, w_gate_rows = _cast_inproj(w_in, l)
    w_gate_cols = w_gate_rows.T
    wg = jnp.zeros((d, 2 * LANES), F32)
    wg = wg.at[:, 0:MLSTM_HEADS].set(w_gate_cols[:, :MLSTM_HEADS])
    wg = wg.at[:, LANES:LANES + MLSTM_HEADS].set(w_gate_cols[:, MLSTM_HEADS:])
    wgt = jnp.zeros((2 * SUBLANES, d), F32)
    wgt = wgt.at[0:MLSTM_HEADS].set(w_gate_cols[:, :MLSTM_HEADS].T)
    wgt = wgt.at[SUBLANES:SUBLANES + MLSTM_HEADS].set(w_gate_cols[:, MLSTM_HEADS:].T)
    wr = jnp.zeros((ROUTER_ROWS, d), F32)
    wr = wr.at[0:N_GROUPS].set(w_gr[l].T).at[SUBLANES:SUBLANES + N_EXPERTS].set(w_er[l].T)
    br = jnp.zeros((ROUTER_ROWS, 1), F32)
    br = br.at[0:N_GROUPS, 0].set(b_gr[l]).at[SUBLANES:SUBLANES + N_EXPERTS, 0].set(b_er[l])
    row = lambda v: v.reshape(1, -1)
    return dict(
        w_main=w_main, wg=wg.astype(BF16), wgt=wgt.astype(BF16),
        bi=_pad_lanes(b_i[l]), bf=_pad_lanes(b_f[l]), bit=_pad_rows(b_i[l]), bft=_pad_rows(b_f[l]),
        pool_w=pool_w[l].astype(BF16), pool_scale=pool_scale[l], norm_g=row(mlstm_norm_g[l]),
        w_out=_cast_layer(w_out, l, d), ln1=(row(ln1_g[l]), row(ln1_b[l])),
        wq=_cast_layer(ca_wq, l, d), wo=_cast_layer(ca_wo, l, d), ln2=(row(ln2_g[l]), row(ln2_b[l])),
        wr=wr, br=br, experts=_expert_cast(w_gate, w_up, w_down, l), ln3=(row(ln3_g[l]), row(ln3_b[l])))


def _trunk_layer(x, nb, seq, layer, mem, pool_prev, state, p, tm, tm_moe):
    chunked = state is None
    if chunked:
        per = seq // tm
        pooled, tails, q, k, v, og, git, gft = _inproj_seq(x, p["w_main"], p["wgt"], p["pool_w"], p["pool_scale"],
                                                           seq, tm)
        pool_buf = tails.reshape(nb, per, POOL_HALO, POOL_WIDTH)[:, per - 1, POOL_HALO - POOL_BUF:]
    else:
        u, q, k, v, og, gi, gf = _inproj(x, p["w_main"], p["wg"], tm)
        ext = jnp.concatenate([pool_prev, u.reshape(nb, seq, POOL_WIDTH)], axis=1)
        period = ext.shape[1]
        pooled = _pool(ext.reshape(nb * period, POOL_WIDTH), p["pool_w"], p["pool_scale"], period)
        pooled = pooled.reshape(nb, period, POOL_WIDTH)[:, period - seq:].reshape(nb * seq, POOL_WIDTH)
        pool_buf = ext[:, -POOL_BUF:]
    if chunked:
        r3 = lambda a: a.reshape(nb, seq, a.shape[-1])
        hn, c1, n1, m1 = _mlstm_chunk(r3(q), r3(k), r3(v), git, gft, p["bit"], p["bft"])
        hn = hn.reshape(nb * seq, MLSTM_WIDTH)
        m1 = m1[:, :MLSTM_HEADS, 0]
    else:
        c_all, n_all, m0, c_new = state
        m0p = jnp.zeros((nb, LANES), F32).at[:, :MLSTM_HEADS].set(m0)
        hn, c1, n1, m1 = _mlstm_step(q, k, v, gi, gf, p["bi"], p["bf"], c_all, n_all, m0p, c_new, layer, SUBLANES)
        m1 = m1[:, :MLSTM_HEADS]
    if chunked:
        x = _attn_block(pooled, hn, og, x, mem[0], mem[1], layer, p["norm_g"], p["w_out"], p["ln1"], p["wq"],
                        p["wo"], p["ln2"], seq, tm_moe)
    else:
        x = _mixout(pooled, hn, og, x, p["norm_g"], p["w_out"], *p["ln1"], tm)
        qc = _mm(x, p["wq"], tm, F32, scale=CA_HEAD_DIM ** -0.5, name="ca_q")
        qh = jnp.zeros((nb, SUBLANES, CA_HEAD_DIM), F32).at[:, :CA_HEADS].set(qc.reshape(nb, CA_HEADS, CA_HEAD_DIM))
        ctx = _attn_step(qh, mem[0], mem[1], layer, 4)[:, :CA_HEADS].reshape(nb, D_MODEL)
        x = _mm_res_ln(ctx, x, p["wo"], *p["ln2"], tm)
    x = _moe(x, p["wr"], p["br"], *p["experts"], *p["ln3"], tm_moe)
    return x, pool_buf, c1, n1, m1


def kernel(x_prompt, x_sample, mem_prompt, cache_pool, state_mlstm_C, state_mlstm_n, state_mlstm_m,
           cache_mem_k, cache_mem_v, emb_ln_g, emb_ln_b, w_in, b_i, b_f, pool_w, pool_scale,
           mlstm_norm_g, w_out, ln1_g, ln1_b, ca_wq, ca_wk, ca_wv, ca_wo, ln2_g, ln2_b,
           w_gr, b_gr, w_er, b_er, w_gate, w_up, w_down, ln3_g, ln3_b):
    bp, tp, d = x_prompt.shape
    bs, ts, _ = x_sample.shape
    tm_p, tm_s = 512, bs * ts
    xp = _ln(x_prompt.reshape(bp * tp, d), emb_ln_g, emb_ln_b, tm_p)
    xs = _ln(x_sample.reshape(bs * ts, d), emb_ln_g, emb_ln_b, tm_s)
    mk4, mv4, mk2, mv2 = _mem_proj(mem_prompt, ca_wk, ca_wv)
    outs = [[] for _ in range(7)]
    sc = jnp.zeros_like(state_mlstm_C)
    for l in range(DEPTH):
        p = _layer_weights(l, w_in, b_i, b_f, pool_w, pool_scale, mlstm_norm_g, w_out, ln1_g, ln1_b, ca_wq,
                           ca_wo, ln2_g, ln2_b, w_gr, b_gr, w_er, b_er, w_gate, w_up, w_down, ln3_g, ln3_b)
        xp, pb, c1, n1, m1 = _trunk_layer(xp, bp, tp, l, (mk2, mv2), None, None, p, tm_p, 2 * tm_p)
        xs, sb, sc, ns, ms = _trunk_layer(xs, bs, ts, l, (cache_mem_k, cache_mem_v), cache_pool[l],
                                          (state_mlstm_C, state_mlstm_n, state_mlstm_m[l], sc), p, tm_s, tm_s)
        for lst, val in zip(outs, (pb, c1, n1, m1, sb, ns, ms)):
            lst.append(val)
    pp, pc, pn, pm, sp, sn, sm = (jnp.stack(o) for o in outs)
    return (xp.reshape(bp, tp, d), xs.reshape(bs, ts, d), pp, pc, pn, pm, mk4, mv4, sp, sc, sn, sm)
```

```python
import functools

import jax
import jax.numpy as jnp
from jax import lax
from jax.experimental import pallas as pl
from jax.experimental.pallas import tpu as pltpu

F32 = jnp.float32
BF16 = jnp.bfloat16

D_MODEL = 1024
DEPTH = 4
POOL_WIDTH = 512
POOL_GROUPS = 4
POOL_GROUP_DIM = 128
POOL_WINDOWS = (2, 4, 8, 16)
POOL_BUF = 15
MLSTM_WIDTH = 512
MLSTM_HEADS = 4
MLSTM_HEAD_DIM = 128
N_MEM = 256
CA_HEADS = 4
CA_HEAD_DIM = 256
N_GROUPS = 4
EXPERTS_PER_GROUP = 4
N_EXPERTS = 16
D_EXPERT = 256
ALPHA = (2 * DEPTH) ** 0.25
LN_EPS = 1e-5
GATE_OFF = POOL_WIDTH + 4 * MLSTM_WIDTH

LANES = 128
SUBLANES = 8
VMEM_LIMIT = 56 * 1024 * 1024
MLSTM_CHUNK = 256
POOL_ROWS = 2048
MLSTM_STEP_SEQS = 16
ATTN_STEP_SEQS = 8
POOL_HALO = 16
ROUTER_ROWS = 32
MOE_SEGMENT = 512
MOE_CAP = 160


def _params(*sem):
    return pltpu.CompilerParams(dimension_semantics=sem, vmem_limit_bytes=VMEM_LIMIT)


def _dot(a, b):
    return jnp.dot(a, b, preferred_element_type=F32)


def _dot_nt(a, b, precision=None):
    return lax.dot_general(a, b, (((1,), (1,)), ((), ())), precision=precision,
                           preferred_element_type=F32)


def _dot_tn(a, b):
    return lax.dot_general(a, b, (((0,), (0,)), ((), ())), preferred_element_type=F32)


def _layer_norm(x, g, b):
    mu = jnp.mean(x, -1, keepdims=True)
    xc = x - mu
    var = jnp.mean(xc * xc, -1, keepdims=True)
    return xc * lax.rsqrt(var + LN_EPS) * g + b


def _unit_norm(x):
    mu = jnp.mean(x, -1, keepdims=True)
    xc = x - mu
    var = jnp.mean(xc * xc, -1, keepdims=True)
    return xc * lax.rsqrt(var + LN_EPS)


def _log_sigmoid(x):
    return jnp.minimum(x, 0.0) - jnp.log1p(jnp.exp(-jnp.abs(x)))


def _scan(x, axis, op, fill):
    n = x.shape[axis]
    idx = lax.broadcasted_iota(jnp.int32, x.shape, axis)
    s = 1
    while s < n:
        x = op(x, jnp.where(idx >= s, pltpu.roll(x, s, axis), fill))
        s *= 2
    return x


def _ln_kernel(x_ref, g_ref, b_ref, o_ref):
    o_ref[...] = _layer_norm(x_ref[...], g_ref[...], b_ref[...])


def _ln(x, g, b, tm):
    n, d = x.shape
    row = pl.BlockSpec((tm, d), lambda i: (i, 0))
    vec = pl.BlockSpec((1, d), lambda i: (0, 0))
    return pl.pallas_call(
        _ln_kernel, grid=(n // tm,), in_specs=[row, vec, vec], out_specs=row,
        out_shape=jax.ShapeDtypeStruct((n, d), F32), compiler_params=_params("parallel"),
        name="emb_ln")(x, g.reshape(1, d), b.reshape(1, d))


def _qkvo(xb, w_ref, q_ref, k_ref, v_ref, og_ref):
    w = MLSTM_WIDTH
    q = _dot(xb, w_ref[:, POOL_WIDTH:POOL_WIDTH + w]) * (MLSTM_HEAD_DIM ** -0.5)
    q_ref[...] = q.astype(q_ref.dtype)
    k_ref[...] = _dot(xb, w_ref[:, POOL_WIDTH + w:POOL_WIDTH + 2 * w]).astype(k_ref.dtype)
    v_ref[...] = _dot(xb, w_ref[:, POOL_WIDTH + 2 * w:POOL_WIDTH + 3 * w]).astype(v_ref.dtype)
    og_ref[...] = _dot(xb, w_ref[:, POOL_WIDTH + 3 * w:POOL_WIDTH + 4 * w])


def _inproj_kernel(x_ref, w_ref, wg_ref, u_ref, q_ref, k_ref, v_ref, og_ref, gi_ref, gf_ref):
    xb = x_ref[...].astype(BF16)
    u_ref[...] = _dot(xb, w_ref[:, 0:POOL_WIDTH])
    _qkvo(xb, w_ref, q_ref, k_ref, v_ref, og_ref)
    g = _dot(xb, wg_ref[...])
    gi_ref[...] = g[:, 0:LANES]
    gf_ref[...] = g[:, LANES:2 * LANES]


def _inproj(x, w_main, wg, tm):
    n, d = x.shape
    row = lambda c: pl.BlockSpec((tm, c), lambda i: (i, 0))
    full = lambda a: pl.BlockSpec(a.shape, lambda i: (0, 0))
    widths = (POOL_WIDTH, MLSTM_WIDTH, MLSTM_WIDTH, MLSTM_WIDTH, MLSTM_WIDTH, LANES, LANES)
    return pl.pallas_call(
        _inproj_kernel, grid=(n // tm,), in_specs=[row(d), full(w_main), full(wg)],
        out_specs=tuple(row(c) for c in widths),
        out_shape=tuple(jax.ShapeDtypeStruct((n, c), F32) for c in widths),
        compiler_params=_params("parallel"), name="inproj")(x, w_main, wg)


def _inproj_seq_kernel(x_ref, w_ref, wgt_ref, pw_ref, sc_ref, po_ref, tail_ref, q_ref, k_ref, v_ref, og_ref,
                       git_ref, gft_ref, carry_ref, *, per):
    tm = x_ref.shape[0]
    start = lax.rem(pl.program_id(0), per) * tm

    @pl.when(start == 0)
    def _():
        carry_ref[...] = jnp.zeros_like(carry_ref)

    xb = x_ref[...].astype(BF16)
    u = _dot(xb, w_ref[:, 0:POOL_WIDTH])
    halo = carry_ref.shape[0]
    ext = jnp.concatenate([carry_ref[...], u], axis=0)
    tail = ext[tm:, :]
    carry_ref[...] = tail
    tail_ref[...] = tail
    pos = start + lax.broadcasted_iota(jnp.int32, (tm, POOL_GROUP_DIM), 0)
    for g, win in enumerate(POOL_WINDOWS):
        sl = slice(g * POOL_GROUP_DIM, (g + 1) * POOL_GROUP_DIM)
        acc = ext[:, sl]
        s = 1
        while s < win:
            acc = acc + pltpu.roll(acc, s, 0)
            s *= 2
        cnt = jnp.minimum(pos + 1, win).astype(F32)
        dlt = acc[halo:, :] / cnt - u[:, sl]
        po_ref[:, sl] = (_dot(dlt.astype(BF16), pw_ref[g]) * sc_ref[:, sl]).astype(po_ref.dtype)
    _qkvo(xb, w_ref, q_ref, k_ref, v_ref, og_ref)
    gt = _dot_nt(wgt_ref[...], xb)
    git_ref[...] = gt[0:SUBLANES]
    gft_ref[...] = gt[SUBLANES:2 * SUBLANES]


def _inproj_seq(x, w_main, wgt, pw, scale, seq, tm):
    n, d = x.shape
    per = seq // tm
    row = lambda c: pl.BlockSpec((tm, c), lambda i: (i, 0))
    full = lambda a: pl.BlockSpec(a.shape, lambda i: (0,) * a.ndim)
    gt_spec = pl.BlockSpec((None, SUBLANES, tm), lambda i: (i // per, 0, i % per))
    gt_shape = jax.ShapeDtypeStruct((n // seq, SUBLANES, seq), F32)
    mw = MLSTM_WIDTH
    out_shape = (
        jax.ShapeDtypeStruct((n, POOL_WIDTH), BF16),
        jax.ShapeDtypeStruct((n // tm, POOL_HALO, POOL_WIDTH), F32),
        jax.ShapeDtypeStruct((n, mw), BF16), jax.ShapeDtypeStruct((n, mw), BF16), jax.ShapeDtypeStruct((n, mw), BF16),
        jax.ShapeDtypeStruct((n, mw), F32), gt_shape, gt_shape,
    )
    out_specs = (row(POOL_WIDTH), pl.BlockSpec((None, POOL_HALO, POOL_WIDTH), lambda i: (i, 0, 0)),
                 row(mw), row(mw), row(mw), row(mw), gt_spec, gt_spec)
    scale = scale.reshape(1, POOL_WIDTH)
    return pl.pallas_call(
        functools.partial(_inproj_seq_kernel, per=per), grid=(n // tm,),
        in_specs=[row(d), full(w_main), full(wgt), full(pw), full(scale)],
        out_specs=out_specs, out_shape=out_shape, scratch_shapes=[pltpu.VMEM((POOL_HALO, POOL_WIDTH), F32)],
        compiler_params=_params("arbitrary"), name="inproj_seq")(x, w_main, wgt, pw, scale)


def _pool_kernel(ext_ref, pw_ref, sc_ref, o_ref, *, period):
    rows = ext_ref.shape[0]
    r = lax.broadcasted_iota(jnp.int32, (rows, POOL_GROUP_DIM), 0)
    if period != rows:
        r = lax.rem(r, period)
    for g, win in enumerate(POOL_WINDOWS):
        sl = slice(g * POOL_GROUP_DIM, (g + 1) * POOL_GROUP_DIM)
        x = ext_ref[:, sl]
        acc = x
        s = 1
        while s < win:
            acc = acc + jnp.where(r >= s, pltpu.roll(acc, s, 0), 0.0)
            s *= 2
        cnt = jnp.minimum(r + 1, win).astype(F32)
        d = acc / cnt - x
        y = _dot(d.astype(BF16), pw_ref[g]) * sc_ref[:, sl]
        o_ref[:, sl] = y.astype(o_ref.dtype)


def _pool(ext2d, pw, scale, period):
    n, c = ext2d.shape
    row = pl.BlockSpec((POOL_ROWS, c), lambda i: (i, 0))
    return pl.pallas_call(
        functools.partial(_pool_kernel, period=period), grid=(n // POOL_ROWS,),
        in_specs=[row, pl.BlockSpec(pw.shape, lambda i: (0, 0, 0)), pl.BlockSpec((1, c), lambda i: (0, 0))],
        out_specs=row, out_shape=jax.ShapeDtypeStruct((n, c), BF16),
        compiler_params=_params("parallel"), name="pool")(ext2d, pw, scale.reshape(1, c))


def _mlstm_chunk_kernel(q_ref, k_ref, v_ref, git_ref, gft_ref, bit_ref, bft_ref, hn_ref, c_ref, n_ref, m_ref,
                        at_s, ws_s, dc_s, mc_s, wi_s, fl_s):
    nb, L, _ = q_ref.shape
    rows = nb * SUBLANES

    @pl.when(pl.program_id(0) == 0)
    def _():
        c_ref[...] = jnp.zeros_like(c_ref)
        n_ref[...] = jnp.zeros_like(n_ref)
        m_ref[...] = jnp.zeros_like(m_ref)

    ig = git_ref[...].reshape(rows, L) + bit_ref[...]
    lf = _log_sigmoid(gft_ref[...].reshape(rows, L) + bft_ref[...])
    tri = (lax.broadcasted_iota(jnp.int32, (L, L), 0) <= lax.broadcasted_iota(jnp.int32, (L, L), 1)).astype(BF16)
    lf_hi = lf.astype(BF16)
    lf_mid = (lf - lf_hi.astype(F32)).astype(BF16)
    lf_lo = (lf - lf_hi.astype(F32) - lf_mid.astype(F32)).astype(BF16)
    bc = _dot(lf_hi, tri) + (_dot(lf_mid, tri) + _dot(lf_lo, tri))
    at = ig - bc
    m_prev = m_ref[...].reshape(rows, LANES)[:, 0:1]
    mc = jnp.maximum(_scan(at, 1, jnp.maximum, -jnp.inf), m_prev)
    mt = bc + mc
    m_last = mc[:, L - 1:L]
    at_s[...] = at.reshape(nb, SUBLANES, L)
    ws_s[...] = jnp.exp(at - m_last).reshape(nb, SUBLANES, L)
    dc_s[...] = jnp.broadcast_to(jnp.exp(m_prev - m_last), (rows, LANES)).reshape(nb, SUBLANES, LANES)
    m_ref[...] = jnp.broadcast_to(mt[:, L - 1:L], (rows, LANES)).reshape(nb, SUBLANES, LANES)
    wi = jnp.exp(m_prev - mc)
    fl = jnp.exp(-mt)
    for b in range(nb):
        rs = slice(b * SUBLANES, (b + 1) * SUBLANES)
        mc_s[b] = mc[rs].T
        wi_s[b] = wi[rs].T
        fl_s[b] = fl[rs].T

    causal = (lax.broadcasted_iota(jnp.int32, (L, L), 0) >= lax.broadcasted_iota(jnp.int32, (L, L), 1))

    def per_batch(heads, b, carry):
        at = at_s[b]
        ws = ws_s[b]
        decay = dc_s[b]
        mc_c, wi_c, fl_c = mc_s[b], wi_s[b], fl_s[b]
        ws16 = jnp.concatenate([ws, ws], axis=0).astype(BF16)
        for h in heads:
            sl = slice(h * MLSTM_HEAD_DIM, (h + 1) * MLSTM_HEAD_DIM)
            col = slice(h, h + 1)
            qh = q_ref[b, :, sl]
            kh = k_ref[b, :, sl]
            vh = v_ref[b, :, sl]
            s = _dot_nt(qh, kh)
            p = jnp.where(causal, s * jnp.exp(at[col, :] - mc_c[:, col]), 0.0)
            c_old = c_ref[b, h]
            n_old = n_ref[b, col, :]
            wi = wi_c[:, col]
            dv = MLSTM_HEAD_DIM
            v_ext = jnp.concatenate([vh, jnp.ones_like(vh)], axis=1)
            c_ext = jnp.concatenate([c_old, jnp.broadcast_to(n_old, c_old.shape)], axis=0)
            intra = _dot(p.astype(BF16), v_ext)
            inter = _dot_nt(qh, c_ext.astype(BF16))
            num = intra[:, :dv] + wi * inter[:, :dv]
            qn = intra[:, dv:] + wi * inter[:, dv:]
            hh = num / jnp.maximum(jnp.abs(qn), fl_c[:, col])
            hn_ref[b, :, sl] = _unit_norm(hh)
            dc = decay[col, 0:1]
            vts = (vh.astype(F32).T * ws[col, :]).astype(BF16)
            c_ref[b, h] = dc * c_old + _dot(vts, kh)
            n_ref[b, col, :] = dc * n_old + _dot(ws16, kh)[col, :]
        return carry

    for h0 in range(0, MLSTM_HEADS, 2):
        lax.fori_loop(0, nb, functools.partial(per_batch, (h0, h0 + 1)), 0)


def _mlstm_chunk(q, k, v, git, gft, bit, bft):
    nb, t, w = q.shape
    L = MLSTM_CHUNK
    seq = lambda c: pl.BlockSpec((nb, L, c), lambda i: (0, i, 0))
    seqt = pl.BlockSpec((nb, SUBLANES, L), lambda i: (0, 0, i))
    vec = lambda a: pl.BlockSpec(a.shape, lambda i: (0, 0))
    out_shape = (
        jax.ShapeDtypeStruct((nb, t, w), F32),
        jax.ShapeDtypeStruct((nb, MLSTM_HEADS, MLSTM_HEAD_DIM, MLSTM_HEAD_DIM), F32),
        jax.ShapeDtypeStruct((nb, MLSTM_HEADS, MLSTM_HEAD_DIM), F32),
        jax.ShapeDtypeStruct((nb, SUBLANES, LANES), F32),
    )
    out_specs = (
        seq(w),
        pl.BlockSpec(out_shape[1].shape, lambda i: (0, 0, 0, 0)),
        pl.BlockSpec(out_shape[2].shape, lambda i: (0, 0, 0)),
        pl.BlockSpec(out_shape[3].shape, lambda i: (0, 0, 0)),
    )
    bit, bft = jnp.tile(bit, (nb, 1)), jnp.tile(bft, (nb, 1))
    scratch = [pltpu.VMEM((nb, SUBLANES, L), F32), pltpu.VMEM((nb, SUBLANES, L), F32),
               pltpu.VMEM((nb, SUBLANES, LANES), F32)] + [pltpu.VMEM((nb, L, SUBLANES), F32) for _ in range(3)]
    return pl.pallas_call(
        _mlstm_chunk_kernel, grid=(t // L,),
        in_specs=[seq(w), seq(w), seq(w), seqt, seqt, vec(bit), vec(bft)],
        out_specs=out_specs, out_shape=out_shape, scratch_shapes=scratch, compiler_params=_params("arbitrary"),
        name="mlstm_chunk")(q, k, v, git, gft, bit, bft)


def _mlstm_step_kernel(q_ref, k_ref, v_ref, gi_ref, gf_ref, bi_ref, bf_ref, c_ref, n_ref, m_ref, c_new_ref,
                       hn_ref, co_ref, no_ref, mo_ref):
    del c_new_ref
    tb = q_ref.shape[0]
    d = MLSTM_HEAD_DIM
    ig = gi_ref[...] + bi_ref[...]
    lf = _log_sigmoid(gf_ref[...] + bf_ref[...])
    m_old = m_ref[...]
    mt = jnp.maximum(lf + m_old, ig)
    mo_ref[...] = mt
    wa = jnp.exp(ig - mt)
    wi = jnp.exp(lf + m_old - mt)
    fl = jnp.exp(-mt)
    wa_t, wi_t, fl_t = wa.T, wi.T, fl.T
    lane = lax.broadcasted_iota(jnp.int32, (d, tb), 1)
    for h in range(MLSTM_HEADS):
        sl = slice(h * d, (h + 1) * d)
        qh, kh, vh = q_ref[:, sl], k_ref[:, sl], v_ref[:, sl]
        nh = n_ref[:, h, :]
        qt, kt, vt, nt = qh.T, kh.T, vh.T, nh.T
        wa_r, wi_r, fl_r = wa_t[h:h + 1, :], wi_t[h:h + 1, :], fl_t[h:h + 1, :]
        s = jnp.sum(qt * kt, 0, keepdims=True) * wa_r
        cq = jnp.zeros((d, tb), F32)
        qtb = qt.astype(BF16)
        for b in range(tb):
            cq = jnp.where(lane == b, _dot(c_ref[b, h].astype(BF16), qtb), cq)
        num = s * vt + wi_r * cq
        qn = s + wi_r * jnp.sum(nt * qt, 0, keepdims=True)
        hh = num / jnp.maximum(jnp.abs(qn), fl_r)
        mu = jnp.mean(hh, 0, keepdims=True)
        xc = hh - mu
        var = jnp.mean(xc * xc, 0, keepdims=True)
        hn_ref[:, sl] = (xc * lax.rsqrt(var + LN_EPS)).T
        wav = wa_r * vt
        for b in range(tb):
            co_ref[b, h] = wi_r[:, b:b + 1] * c_ref[b, h] + wav[:, b:b + 1] * kh[b:b + 1, :]
        no_ref[:, h, :] = wi[:, h:h + 1] * nh + wa[:, h:h + 1] * kh


def _mlstm_step(q, k, v, gi, gf, bi, bf, c_all, n_all, m0, c_new, layer, tb):
    n, w = q.shape
    d = MLSTM_HEAD_DIM
    row = lambda c: pl.BlockSpec((tb, c), lambda i: (i, 0))
    vec = lambda a: pl.BlockSpec(a.shape, lambda i: (0, 0))
    cin = pl.BlockSpec((None, tb, MLSTM_HEADS, d, d), lambda i: (layer, i, 0, 0, 0))
    nin = pl.BlockSpec((None, tb, MLSTM_HEADS, d), lambda i: (layer, i, 0, 0))
    nout = pl.BlockSpec((tb, MLSTM_HEADS, d), lambda i: (i, 0, 0))
    out_shape = (
        jax.ShapeDtypeStruct((n, w), F32),
        jax.ShapeDtypeStruct(c_new.shape, F32),
        jax.ShapeDtypeStruct(n_all.shape[1:], F32),
        jax.ShapeDtypeStruct((n, LANES), F32),
    )
    return pl.pallas_call(
        _mlstm_step_kernel, grid=(n // tb,),
        in_specs=[row(w), row(w), row(w), row(LANES), row(LANES), vec(bi), vec(bf), cin, nin, row(LANES),
                  pl.BlockSpec(memory_space=pl.ANY)],
        out_specs=(row(w), cin, nout, row(LANES)), out_shape=out_shape, input_output_aliases={10: 1},
        compiler_params=_params("parallel"), name="mlstm_step")(q, k, v, gi, gf, bi, bf, c_all, n_all, m0, c_new)


def _mixout_kernel(po_ref, hn_ref, og_ref, x_ref, ng_ref, w_ref, g_ref, b_ref, o_ref):
    mo = jax.nn.sigmoid(og_ref[...]) * hn_ref[...] * ng_ref[...]
    mix = _dot(po_ref[...], w_ref[0:POOL_WIDTH, :]) + _dot(mo.astype(BF16), w_ref[POOL_WIDTH:, :])
    o_ref[...] = _layer_norm(ALPHA * x_ref[...] + mix, g_ref[...], b_ref[...])


def _mixout(po, hn, og, x, ng, w, g, b, tm):
    n, d = x.shape
    row = lambda c: pl.BlockSpec((tm, c), lambda i: (i, 0))
    vec = lambda a: pl.BlockSpec(a.shape, lambda i: (0, 0))
    return pl.pallas_call(
        _mixout_kernel, grid=(n // tm,),
        in_specs=[row(POOL_WIDTH), row(MLSTM_WIDTH), row(MLSTM_WIDTH), row(d), vec(ng), vec(w), vec(g), vec(b)],
        out_specs=row(d), out_shape=jax.ShapeDtypeStruct((n, d), F32),
        compiler_params=_params("parallel"), name="mixout")(po, hn, og, x, ng, w, g, b)


def _mm_kernel(x_ref, w_ref, o_ref, *, scale):
    y = _dot(x_ref[...].astype(BF16), w_ref[...])
    if scale != 1.0:
        y = y * scale
    o_ref[...] = y.astype(o_ref.dtype)


def _mm(x, w, tm, out_dtype, scale=1.0, name="proj"):
    n, d = x.shape
    dout = w.shape[1]
    return pl.pallas_call(
        functools.partial(_mm_kernel, scale=scale), grid=(n // tm,),
        in_specs=[pl.BlockSpec((tm, d), lambda i: (i, 0)), pl.BlockSpec(w.shape, lambda i: (0, 0))],
        out_specs=pl.BlockSpec((tm, dout), lambda i: (i, 0)),
        out_shape=jax.ShapeDtypeStruct((n, dout), out_dtype),
        compiler_params=_params("parallel"), name=name)(x, w)


def _cast_kernel(w_ref, o_ref):
    o_ref[...] = w_ref[...].astype(o_ref.dtype)


def _cast_layer(w_all, layer, cols, tr=256):
    rows = w_all.shape[1]
    return pl.pallas_call(
        _cast_kernel, grid=(rows // tr,),
        in_specs=[pl.BlockSpec((None, tr, cols), lambda i: (layer, i, 0))],
        out_specs=pl.BlockSpec((tr, cols), lambda i: (i, 0)),
        out_shape=jax.ShapeDtypeStruct((rows, cols), BF16),
        compiler_params=_params("parallel"), name="weight_cast")(w_all)


def _cast_inproj_kernel(wt_ref, o_ref, og_ref, *, n_main):
    i = pl.program_id(0)

    @pl.when(i < n_main)
    def _():
        o_ref[...] = wt_ref[...].T.astype(BF16)

    @pl.when(i == n_main)
    def _():
        og_ref[...] = wt_ref[0:2 * MLSTM_HEADS, :]


def _cast_inproj(w_in, layer, tc=256):
    wt = jnp.swapaxes(w_in, 1, 2)
    d = wt.shape[2]
    n_main = GATE_OFF // tc
    return pl.pallas_call(
        functools.partial(_cast_inproj_kernel, n_main=n_main), grid=(n_main + 1,),
        in_specs=[pl.BlockSpec((None, tc, d), lambda i: (layer, i, 0))],
        out_specs=(pl.BlockSpec((d, tc), lambda i: (0, jnp.minimum(i, n_main - 1))),
                   pl.BlockSpec((2 * MLSTM_HEADS, d), lambda i: (0, 0))),
        out_shape=(jax.ShapeDtypeStruct((d, GATE_OFF), BF16), jax.ShapeDtypeStruct((2 * MLSTM_HEADS, d), F32)),
        compiler_params=_params("arbitrary"), name="inproj_cast")(wt)


def _mem_proj_kernel(x_ref, wk_ref, wv_ref, k4_ref, v4_ref, k2_ref, v2_ref, wkb, wvb):
    @pl.when(pl.program_id(1) == 0)
    def _():
        wkb[...] = wk_ref[...].astype(BF16)
        wvb[...] = wv_ref[...].astype(BF16)

    bb, m, d = x_ref.shape
    xb = x_ref[...].reshape(bb * m, d).astype(BF16)
    k = _dot(xb, wkb[...])
    v = _dot(xb, wvb[...])
    k2_ref[...] = k.reshape(bb, m, d)
    v2_ref[...] = v.reshape(bb, m, d)
    k4_ref[...] = k.reshape(bb, m, CA_HEADS, CA_HEAD_DIM)
    v4_ref[...] = v.reshape(bb, m, CA_HEADS, CA_HEAD_DIM)


def _mem_proj(mem, wk_all, wv_all, bb=2):
    nb, m, d = mem.shape
    depth = wk_all.shape[0]
    wspec = pl.BlockSpec((None, d, d), lambda l, b: (l, 0, 0))
    o4 = pl.BlockSpec((None, bb, m, CA_HEADS, CA_HEAD_DIM), lambda l, b: (l, b, 0, 0, 0))
    o2 = pl.BlockSpec((None, bb, m, d), lambda l, b: (l, b, 0, 0))
    s4 = jax.ShapeDtypeStruct((depth, nb, m, CA_HEADS, CA_HEAD_DIM), F32)
    s2 = jax.ShapeDtypeStruct((depth, nb, m, d), F32)
    return pl.pallas_call(
        _mem_proj_kernel, grid=(depth, nb // bb),
        in_specs=[pl.BlockSpec((bb, m, d), lambda l, b: (b, 0, 0)), wspec, wspec],
        out_specs=(o4, o4, o2, o2), out_shape=(s4, s4, s2, s2),
        scratch_shapes=[pltpu.VMEM((d, d), BF16), pltpu.VMEM((d, d), BF16)],
        compiler_params=_params("arbitrary", "arbitrary"), name="mem_proj")(mem, wk_all, wv_all)


def _attn_block_kernel(po_ref, hn_ref, og_ref, x_ref, k_ref, v_ref, ng_ref, wm_ref, g1_ref, b1_ref,
                       wq_ref, wo_ref, g_ref, b_ref, o_ref):
    mo = jax.nn.sigmoid(og_ref[...]) * hn_ref[...] * ng_ref[...]
    mix = _dot(po_ref[...], wm_ref[0:POOL_WIDTH, :]) + _dot(mo.astype(BF16), wm_ref[POOL_WIDTH:, :])
    x = _layer_norm(ALPHA * x_ref[...] + mix, g1_ref[...], b1_ref[...])
    qb = (_dot(x.astype(BF16), wq_ref[...]) * (CA_HEAD_DIM ** -0.5)).astype(BF16)
    kb = k_ref[...].astype(BF16)
    vb = v_ref[...].astype(BF16)
    ctx = []
    for h in range(CA_HEADS):
        sl = slice(h * CA_HEAD_DIM, (h + 1) * CA_HEAD_DIM)
        s = _dot_nt(qb[:, sl], kb[:, sl])
        e = jnp.exp(s - jnp.max(s, -1, keepdims=True))
        ctx.append((_dot(e.astype(BF16), vb[:, sl]) / jnp.sum(e, -1, keepdims=True)).astype(BF16))
    y = _dot(jnp.concatenate(ctx, axis=1), wo_ref[...])
    o_ref[...] = _layer_norm(ALPHA * x + y, g_ref[...], b_ref[...])


def _attn_block(po, hn, og, x, k, v, layer, ng, wm, ln1, wq, wo, ln2, seq, tq):
    n, d = x.shape
    per = seq // tq
    row = lambda c: pl.BlockSpec((tq, c), lambda i: (i, 0))
    kspec = pl.BlockSpec((None, None, N_MEM, d), lambda i: (layer, i // per, 0, 0))
    vec = lambda a: pl.BlockSpec(a.shape, lambda i: (0, 0))
    return pl.pallas_call(
        _attn_block_kernel, grid=(n // tq,),
        in_specs=[row(POOL_WIDTH), row(MLSTM_WIDTH), row(MLSTM_WIDTH), row(d), kspec, kspec, vec(ng), vec(wm),
                  vec(ln1[0]), vec(ln1[1]), vec(wq), vec(wo), vec(ln2[0]), vec(ln2[1])],
        out_specs=row(d), out_shape=jax.ShapeDtypeStruct((n, d), F32),
        compiler_params=_params("parallel"), name="attn_block")(po, hn, og, x, k, v, ng, wm, *ln1, wq, wo, *ln2)


def _attn_step_kernel(q_ref, k_ref, v_ref, o_ref):
    bb = q_ref.shape[0]
    rows = N_MEM * CA_HEADS
    lane = lax.broadcasted_iota(jnp.int32, (SUBLANES, rows), 1)
    row = lax.broadcasted_iota(jnp.int32, (SUBLANES, rows), 0)
    own = lax.rem(lane, CA_HEADS) == lax.rem(row, CA_HEADS)
    for j in range(bb):
        kf = k_ref[j].reshape(rows, CA_HEAD_DIM).astype(BF16)
        vf = v_ref[j].reshape(rows, CA_HEAD_DIM).astype(BF16)
        s = jnp.where(own, _dot_nt(q_ref[j].astype(BF16), kf), -jnp.inf)
        e = jnp.exp(s - jnp.max(s, -1, keepdims=True))
        o_ref[j] = _dot(e.astype(BF16), vf) / jnp.sum(e, -1, keepdims=True)


def _attn_step(q, k_all, v_all, layer, bb):
    n = q.shape[0]
    qspec = pl.BlockSpec((bb, SUBLANES, CA_HEAD_DIM), lambda i: (i, 0, 0))
    kspec = pl.BlockSpec((None, bb, N_MEM, CA_HEADS, CA_HEAD_DIM), lambda i: (layer, i, 0, 0, 0))
    return pl.pallas_call(
        _attn_step_kernel, grid=(n // bb,), in_specs=[qspec, kspec, kspec], out_specs=qspec,
        out_shape=jax.ShapeDtypeStruct((n, SUBLANES, CA_HEAD_DIM), F32),
        compiler_params=_params("parallel"), name="attn_step")(q, k_all, v_all)


def _mm_res_ln_kernel(a_ref, x_ref, w_ref, g_ref, b_ref, o_ref):
    y = _dot(a_ref[...].astype(BF16), w_ref[...])
    o_ref[...] = _layer_norm(ALPHA * x_ref[...] + y, g_ref[...], b_ref[...])


def _mm_res_ln(a, x, w, g, b, tm):
    n, d = x.shape
    row = lambda c: pl.BlockSpec((tm, c), lambda i: (i, 0))
    vec = lambda arr: pl.BlockSpec(arr.shape, lambda i: (0, 0))
    return pl.pallas_call(
        _mm_res_ln_kernel, grid=(n // tm,),
        in_specs=[row(a.shape[1]), row(d), vec(w), vec(g), vec(b)], out_specs=row(d),
        out_shape=jax.ShapeDtypeStruct((n, d), F32),
        compiler_params=_params("parallel"), name="proj_res_ln")(a, x, w, g, b)


def _route(lt):
    gl = [lt[g:g + 1, :] for g in range(N_GROUPS)]
    gmax = functools.reduce(jnp.maximum, gl)
    gsum = functools.reduce(jnp.add, [jnp.exp(x - gmax) for x in gl])
    pg_sel = 1.0 / gsum

    def first_max(vals):
        m = functools.reduce(jnp.maximum, vals)
        taken = jnp.zeros_like(m, dtype=jnp.bool_)
        hot = []
        for x in vals:
            h = jnp.logical_and(x == m, jnp.logical_not(taken))
            taken = jnp.logical_or(taken, h)
            hot.append(h)
        return m, hot

    _, g_hot = first_max(gl)
    el = []
    for j in range(EXPERTS_PER_GROUP):
        rows = [lt[SUBLANES + g * EXPERTS_PER_GROUP + j:SUBLANES + g * EXPERTS_PER_GROUP + j + 1, :]
                for g in range(N_GROUPS)]
        x = rows[N_GROUPS - 1]
        for g in range(N_GROUPS - 2, -1, -1):
            x = jnp.where(g_hot[g], rows[g], x)
        el.append(x)
    emax = functools.reduce(jnp.maximum, el)
    ee = [jnp.exp(x - emax) for x in el]
    esum = functools.reduce(jnp.add, ee)
    pe = [x / esum for x in ee]
    p1, hot1 = first_max(pe)
    p2, hot2 = first_max([jnp.where(h, -jnp.inf, x) for h, x in zip(hot1, pe)])
    psum = p1 + p2
    gate = [jnp.where(h1, pg_sel * p1 / psum, jnp.where(h2, pg_sel * p2 / psum, 0.0)) for h1, h2 in zip(hot1, hot2)]
    return g_hot, gate


def _expert_cast_kernel(wg_ref, wu_ref, wd_ref, og_ref, ou_ref, od_ref):
    f = D_EXPERT
    for j in range(EXPERTS_PER_GROUP):
        og_ref[:, j * f:(j + 1) * f] = wg_ref[j].astype(BF16)
        ou_ref[:, j * f:(j + 1) * f] = wu_ref[j].astype(BF16)
        od_ref[j * f:(j + 1) * f, :] = wd_ref[j].astype(BF16)


def _expert_cast(wg, wu, wd, layer):
    d = wg.shape[2]
    e, gf = EXPERTS_PER_GROUP, EXPERTS_PER_GROUP * D_EXPERT
    cols = pl.BlockSpec((None, e, d, D_EXPERT), lambda g: (layer, g, 0, 0))
    rows = pl.BlockSpec((None, e, D_EXPERT, d), lambda g: (layer, g, 0, 0))
    return pl.pallas_call(
        _expert_cast_kernel, grid=(N_GROUPS,), in_specs=[cols, cols, rows],
        out_specs=(pl.BlockSpec((None, d, gf), lambda g: (g, 0, 0)), pl.BlockSpec((None, d, gf), lambda g: (g, 0, 0)),
                   pl.BlockSpec((None, gf, d), lambda g: (g, 0, 0))),
        out_shape=(jax.ShapeDtypeStruct((N_GROUPS, d, gf), BF16), jax.ShapeDtypeStruct((N_GROUPS, d, gf), BF16),
                   jax.ShapeDtypeStruct((N_GROUPS, gf, d), BF16)),
        compiler_params=_params("parallel"), name="expert_cast")(wg, wu, wd)


def _moe_kernel(x_ref, wr_ref, br_ref, tri_ref, wg_ref, wu_ref, wd_ref, g_ref, b_ref, o_ref,
                xb_ref, ct_ref, comb_ref, perm_ref, acc_ref, full_ref):
    grp = pl.program_id(1)
    tm = x_ref.shape[0]
    ns, cap, sw = perm_ref.shape[1:]
    f = D_EXPERT

    @pl.when(grp == 0)
    def _():
        x = x_ref[...]
        xh = x.astype(BF16)
        xb_ref[...] = xh
        xl = (x - xh.astype(F32)).astype(BF16)
        wr = wr_ref[...]
        wh = wr.astype(BF16)
        wl = (wr - wh.astype(F32)).astype(BF16)
        lt = _dot_nt(wh, xh) + (_dot_nt(wl, xh) + _dot_nt(wh, xl)) + br_ref[...]
        g_hot, gate = _route(lt)
        row8 = lax.broadcasted_iota(jnp.int32, (SUBLANES, tm), 0)
        hot8 = jnp.zeros((SUBLANES, tm), F32)
        for g in range(N_GROUPS):
            hot8 = jnp.where(jnp.logical_and(row8 == g, g_hot[g]), 1.0, hot8)
        most = None
        for s in range(ns):
            hs = hot8[:, s * sw:(s + 1) * sw]
            cum = _dot(hs.astype(BF16), tri_ref[...])
            seg_max = jnp.max(cum[:, sw - 1:sw])
            most = seg_max if most is None else jnp.maximum(most, seg_max)
            base = lax.broadcasted_iota(jnp.int32, (SUBLANES, sw), 0).astype(F32) * cap
            code = jnp.sum(hs * (base + cum - 1.0), 0, keepdims=True).astype(jnp.int32)
            for g in range(N_GROUPS):
                r = lax.broadcasted_iota(jnp.int32, (cap, sw), 0) + g * cap
                perm_ref[g, s] = jnp.where(r == code, 1.0, 0.0).astype(BF16)
        full_ref[0] = (most > cap).astype(jnp.int32)
        ct_ref[...] = jnp.zeros_like(ct_ref)
        for j in range(EXPERTS_PER_GROUP):
            ct_ref[j:j + 1, :] = gate[j]
        ct_ref[EXPERTS_PER_GROUP:EXPERTS_PER_GROUP + 1, :] = jnp.sum(hot8 * row8.astype(F32), 0, keepdims=True)
        comb_ref[...] = ct_ref[...].T
        acc_ref[...] = jnp.zeros_like(acc_ref)

    def experts(rows, gates):
        hh = []
        for j in range(EXPERTS_PER_GROUP):
            hg = _dot(rows, wg_ref[:, j * f:(j + 1) * f])
            hu = _dot(rows, wu_ref[:, j * f:(j + 1) * f])
            hh.append((hg * jax.nn.sigmoid(hg) * hu * gates[:, j:j + 1]).astype(BF16))
        return _dot(jnp.concatenate(hh, axis=1), wd_ref[...])

    @pl.when(full_ref[0] == 0)
    def _():
        comb = comb_ref[...]
        ch = comb.astype(BF16)
        cl = (comb - ch.astype(F32)).astype(BF16)
        xb = xb_ref[...]
        perm = [perm_ref[grp, s] for s in range(ns)]
        seg = lambda a, s: a[s * sw:(s + 1) * sw]
        rows = jnp.concatenate([_dot(perm[s], seg(xb, s)) for s in range(ns)], axis=0).astype(BF16)
        gates = jnp.concatenate([_dot(perm[s], seg(ch, s)) + _dot(perm[s], seg(cl, s)) for s in range(ns)], axis=0)
        y = experts(rows, gates).astype(BF16)
        for s in range(ns):
            acc_ref[s * sw:(s + 1) * sw, :] += _dot_tn(perm[s], y[s * cap:(s + 1) * cap])

    @pl.when(full_ref[0] != 0)
    def _():
        comb = comb_ref[...]
        own = comb[:, EXPERTS_PER_GROUP:EXPERTS_PER_GROUP + 1] == grp.astype(F32)
        acc_ref[...] += experts(xb_ref[...], jnp.where(own, comb, 0.0))

    @pl.when(grp == N_GROUPS - 1)
    def _():
        o_ref[...] = _layer_norm(ALPHA * x_ref[...] + acc_ref[...], g_ref[...], b_ref[...])


def _moe(x, wr, br, wg, wu, wd, g, b, tm):
    n, d = x.shape
    ns = max(1, tm // MOE_SEGMENT)
    sw = tm // ns
    cap = min(sw, MOE_CAP)
    tri = (lax.broadcasted_iota(jnp.int32, (sw, sw), 0) <= lax.broadcasted_iota(jnp.int32, (sw, sw), 1)).astype(BF16)
    row = pl.BlockSpec((tm, d), lambda i, e: (i, 0))
    vec = lambda a: pl.BlockSpec(a.shape, lambda i, e: (0, 0))
    grp = lambda a: pl.BlockSpec((None,) + a.shape[1:], lambda i, e: (e, 0, 0))
    scratch = [pltpu.VMEM((tm, d), BF16), pltpu.VMEM((LANES, tm), F32), pltpu.VMEM((tm, LANES), F32),
               pltpu.VMEM((N_GROUPS, ns, cap, sw), BF16), pltpu.VMEM((tm, d), F32), pltpu.SMEM((1,), jnp.int32)]
    return pl.pallas_call(
        _moe_kernel, grid=(n // tm, N_GROUPS),
        in_specs=[row, vec(wr), vec(br), vec(tri), grp(wg), grp(wu), grp(wd), vec(g), vec(b)],
        out_specs=row, out_shape=jax.ShapeDtypeStruct((n, d), F32), scratch_shapes=scratch,
        compiler_params=_params("parallel", "arbitrary"), name="moe")(x, wr, br, tri, wg, wu, wd, g, b)


def _pad_lanes(v):
    return jnp.zeros((1, LANES), F32).at[0, :v.shape[0]].set(v)


def _pad_rows(v):
    return jnp.zeros((SUBLANES, 1), F32).at[:v.shape[0], 0].set(v)


def _layer_weights(l, w_in, b_i, b_f, pool_w, pool_scale, mlstm_norm_g, w_out, ln1_g, ln1_b, ca_wq, ca_wo,
                   ln2_g, ln2_b, w_gr, b_gr, w_er, b_er, w_gate, w_up, w_down, ln3_g, ln3_b):
    d = D_MODEL
    w_main, w_gate_rows = _cast_inproj(w_in, l)
    w_gate_cols = w_gate_rows.T
    wg = jnp.zeros((d, 2 * LANES), F32)
    wg = wg.at[:, 0:MLSTM_HEADS].set(w_gate_cols[:, :MLSTM_HEADS])
    wg = wg.at[:, LANES:LANES + MLSTM_HEADS].set(w_gate_cols[:, MLSTM_HEADS:])
    wgt = jnp.zeros((2 * SUBLANES, d), F32)
    wgt = wgt.at[0:MLSTM_HEADS].set(w_gate_cols[:, :MLSTM_HEADS].T)
    wgt = wgt.at[SUBLANES:SUBLANES + MLSTM_HEADS].set(w_gate_cols[:, MLSTM_HEADS:].T)
    wr = jnp.zeros((ROUTER_ROWS, d), F32)
    wr = wr.at[0:N_GROUPS].set(w_gr[l].T).at[SUBLANES:SUBLANES + N_EXPERTS].set(w_er[l].T)
    br = jnp.zeros((ROUTER_ROWS, 1), F32)
    br = br.at[0:N_GROUPS, 0].set(b_gr[l]).at[SUBLANES:SUBLANES + N_EXPERTS, 0].set(b_er[l])
    row = lambda v: v.reshape(1, -1)
    return dict(
        w_main=w_main, wg=wg.astype(BF16), wgt=wgt.astype(BF16),
        bi=_pad_lanes(b_i[l]), bf=_pad_lanes(b_f[l]), bit=_pad_rows(b_i[l]), bft=_pad_rows(b_f[l]),
        pool_w=pool_w[l].astype(BF16), pool_scale=pool_scale[l], norm_g=row(mlstm_norm_g[l]),
        w_out=_cast_layer(w_out, l, d), ln1=(row(ln1_g[l]), row(ln1_b[l])),
        wq=_cast_layer(ca_wq, l, d), wo=_cast_layer(ca_wo, l, d), ln2=(row(ln2_g[l]), row(ln2_b[l])),
        wr=wr, br=br, experts=_expert_cast(w_gate, w_up, w_down, l), ln3=(row(ln3_g[l]), row(ln3_b[l])))


def _trunk_layer(x, nb, seq, layer, mem, pool_prev, state, p, tm, tm_moe):
    chunked = state is None
    if chunked:
        per = seq // tm
        pooled, tails, q, k, v, og, git, gft = _inproj_seq(x, p["w_main"], p["wgt"], p["pool_w"], p["pool_scale"],
                                                           seq, tm)
        pool_buf = tails.reshape(nb, per, POOL_HALO, POOL_WIDTH)[:, per - 1, POOL_HALO - POOL_BUF:]
    else:
        u, q, k, v, og, gi, gf = _inproj(x, p["w_main"], p["wg"], tm)
        ext = jnp.concatenate([pool_prev, u.reshape(nb, seq, POOL_WIDTH)], axis=1)
        period = ext.shape[1]
        pooled = _pool(ext.reshape(nb * period, POOL_WIDTH), p["pool_w"], p["pool_scale"], period)
        pooled = pooled.reshape(nb, period, POOL_WIDTH)[:, period - seq:].reshape(nb * seq, POOL_WIDTH)
        pool_buf = ext[:, -POOL_BUF:]
    if chunked:
        r3 = lambda a: a.reshape(nb, seq, a.shape[-1])
        hn, c1, n1, m1 = _mlstm_chunk(r3(q), r3(k), r3(v), git, gft, p["bit"], p["bft"])
        hn = hn.reshape(nb * seq, MLSTM_WIDTH)
        m1 = m1[:, :MLSTM_HEADS, 0]
    else:
        c_all, n_all, m0, c_new = state
        m0p = jnp.zeros((nb, LANES), F32).at[:, :MLSTM_HEADS].set(m0)
        hn, c1, n1, m1 = _mlstm_step(q, k, v, gi, gf, p["bi"], p["bf"], c_all, n_all, m0p, c_new, layer, MLSTM_STEP_SEQS)
        m1 = m1[:, :MLSTM_HEADS]
    if chunked:
        x = _attn_block(pooled, hn, og, x, mem[0], mem[1], layer, p["norm_g"], p["w_out"], p["ln1"], p["wq"],
                        p["wo"], p["ln2"], seq, tm_moe)
    else:
        x = _mixout(pooled, hn, og, x, p["norm_g"], p["w_out"], *p["ln1"], tm)
        qc = _mm(x, p["wq"], tm, F32, scale=CA_HEAD_DIM ** -0.5, name="ca_q")
        qh = jnp.zeros((nb, SUBLANES, CA_HEAD_DIM), F32).at[:, :CA_HEADS].set(qc.reshape(nb, CA_HEADS, CA_HEAD_DIM))
        ctx = _attn_step(qh, mem[0], mem[1], layer, ATTN_STEP_SEQS)[:, :CA_HEADS].reshape(nb, D_MODEL)
        x = _mm_res_ln(ctx, x, p["wo"], *p["ln2"], tm)
    x = _moe(x, p["wr"], p["br"], *p["experts"], *p["ln3"], tm_moe)
    return x, pool_buf, c1, n1, m1


def kernel(x_prompt, x_sample, mem_prompt, cache_pool, state_mlstm_C, state_mlstm_n, state_mlstm_m,
           cache_mem_k, cache_mem_v, emb_ln_g, emb_ln_b, w_in, b_i, b_f, pool_w, pool_scale,
           mlstm_norm_g, w_out, ln1_g, ln1_b, ca_wq, ca_wk, ca_wv, ca_wo, ln2_g, ln2_b,
           w_gr, b_gr, w_er, b_er, w_gate, w_up, w_down, ln3_g, ln3_b):
    bp, tp, d = x_prompt.shape
    bs, ts, _ = x_sample.shape
    tm_p, tm_s = 512, bs * ts
    xp = _ln(x_prompt.reshape(bp * tp, d), emb_ln_g, emb_ln_b, tm_p)
    xs = _ln(x_sample.reshape(bs * ts, d), emb_ln_g, emb_ln_b, tm_s)
    mk4, mv4, mk2, mv2 = _mem_proj(mem_prompt, ca_wk, ca_wv)
    outs = [[] for _ in range(7)]
    sc = jnp.zeros_like(state_mlstm_C)
    for l in range(DEPTH):
        p = _layer_weights(l, w_in, b_i, b_f, pool_w, pool_scale, mlstm_norm_g, w_out, ln1_g, ln1_b, ca_wq,
                           ca_wo, ln2_g, ln2_b, w_gr, b_gr, w_er, b_er, w_gate, w_up, w_down, ln3_g, ln3_b)
        xp, pb, c1, n1, m1 = _trunk_layer(xp, bp, tp, l, (mk2, mv2), None, None, p, tm_p, 2 * tm_p)
        xs, sb, sc, ns, ms = _trunk_layer(xs, bs, ts, l, (cache_mem_k, cache_mem_v), cache_pool[l],
                                          (state_mlstm_C, state_mlstm_n, state_mlstm_m[l], sc), p, tm_s, tm_s)
        for lst, val in zip(outs, (pb, c1, n1, m1, sb, ns, ms)):
            lst.append(val)
    pp, pc, pn, pm, sp, sn, sm = (jnp.stack(o) for o in outs)
    return (xp.reshape(bp, tp, d), xs.reshape(bs, ts, d), pp, pc, pn, pm, mk4, mv4, sp, sc, sn, sm)
```

```python
import functools

import jax
import jax.numpy as jnp
from jax import lax
from jax.experimental import pallas as pl
from jax.experimental.pallas import tpu as pltpu

F32 = jnp.float32
BF16 = jnp.bfloat16

D_MODEL = 1024
DEPTH = 4
POOL_WIDTH = 512
POOL_GROUPS = 4
POOL_GROUP_DIM = 128
POOL_WINDOWS = (2, 4, 8, 16)
POOL_BUF = 15
MLSTM_WIDTH = 512
MLSTM_HEADS = 4
MLSTM_HEAD_DIM = 128
N_MEM = 256
CA_HEADS = 4
CA_HEAD_DIM = 256
N_GROUPS = 4
EXPERTS_PER_GROUP = 4
N_EXPERTS = 16
D_EXPERT = 256
ALPHA = (2 * DEPTH) ** 0.25
LN_EPS = 1e-5
GATE_OFF = POOL_WIDTH + 4 * MLSTM_WIDTH

LANES = 128
SUBLANES = 8
VMEM_LIMIT = 56 * 1024 * 1024
MLSTM_CHUNK = 256
POOL_ROWS = 2048
MLSTM_STEP_SEQS = 16
ATTN_STEP_SEQS = 8
POOL_HALO = 16
ROUTER_ROWS = 32
MOE_SEGMENT = 512
MOE_CAP = 160


def _params(*sem):
    return pltpu.CompilerParams(dimension_semantics=sem, vmem_limit_bytes=VMEM_LIMIT)


def _dot(a, b):
    return jnp.dot(a, b, preferred_element_type=F32)


def _dot_nt(a, b, precision=None):
    return lax.dot_general(a, b, (((1,), (1,)), ((), ())), precision=precision,
                           preferred_element_type=F32)


def _dot_tn(a, b):
    return lax.dot_general(a, b, (((0,), (0,)), ((), ())), preferred_element_type=F32)


def _layer_norm(x, g, b):
    mu = jnp.mean(x, -1, keepdims=True)
    xc = x - mu
    var = jnp.mean(xc * xc, -1, keepdims=True)
    return xc * lax.rsqrt(var + LN_EPS) * g + b


def _unit_norm(x):
    mu = jnp.mean(x, -1, keepdims=True)
    xc = x - mu
    var = jnp.mean(xc * xc, -1, keepdims=True)
    return xc * lax.rsqrt(var + LN_EPS)


def _log_sigmoid(x):
    return jnp.minimum(x, 0.0) - jnp.log1p(jnp.exp(-jnp.abs(x)))


def _scan(x, axis, op, fill):
    n = x.shape[axis]
    idx = lax.broadcasted_iota(jnp.int32, x.shape, axis)
    s = 1
    while s < n:
        x = op(x, jnp.where(idx >= s, pltpu.roll(x, s, axis), fill))
        s *= 2
    return x


def _ln_kernel(x_ref, g_ref, b_ref, o_ref):
    o_ref[...] = _layer_norm(x_ref[...], g_ref[...], b_ref[...])


def _ln(x, g, b, tm):
    n, d = x.shape
    row = pl.BlockSpec((tm, d), lambda i: (i, 0))
    vec = pl.BlockSpec((1, d), lambda i: (0, 0))
    return pl.pallas_call(
        _ln_kernel, grid=(n // tm,), in_specs=[row, vec, vec], out_specs=row,
        out_shape=jax.ShapeDtypeStruct((n, d), F32), compiler_params=_params("parallel"),
        name="emb_ln")(x, g.reshape(1, d), b.reshape(1, d))


def _qkvo(xb, w_ref, q_ref, k_ref, v_ref, og_ref):
    w = MLSTM_WIDTH
    q = _dot(xb, w_ref[:, POOL_WIDTH:POOL_WIDTH + w]) * (MLSTM_HEAD_DIM ** -0.5)
    q_ref[...] = q.astype(q_ref.dtype)
    k_ref[...] = _dot(xb, w_ref[:, POOL_WIDTH + w:POOL_WIDTH + 2 * w]).astype(k_ref.dtype)
    v_ref[...] = _dot(xb, w_ref[:, POOL_WIDTH + 2 * w:POOL_WIDTH + 3 * w]).astype(v_ref.dtype)
    og_ref[...] = _dot(xb, w_ref[:, POOL_WIDTH + 3 * w:POOL_WIDTH + 4 * w])


def _inproj_kernel(x_ref, w_ref, wg_ref, u_ref, q_ref, k_ref, v_ref, og_ref, gi_ref, gf_ref):
    xb = x_ref[...].astype(BF16)
    u_ref[...] = _dot(xb, w_ref[:, 0:POOL_WIDTH])
    _qkvo(xb, w_ref, q_ref, k_ref, v_ref, og_ref)
    g = _dot(xb, wg_ref[...])
    gi_ref[...] = g[:, 0:LANES]
    gf_ref[...] = g[:, LANES:2 * LANES]


def _inproj(x, w_main, wg, tm):
    n, d = x.shape
    row = lambda c: pl.BlockSpec((tm, c), lambda i: (i, 0))
    full = lambda a: pl.BlockSpec(a.shape, lambda i: (0, 0))
    widths = (POOL_WIDTH, MLSTM_WIDTH, MLSTM_WIDTH, MLSTM_WIDTH, MLSTM_WIDTH, LANES, LANES)
    return pl.pallas_call(
        _inproj_kernel, grid=(n // tm,), in_specs=[row(d), full(w_main), full(wg)],
        out_specs=tuple(row(c) for c in widths),
        out_shape=tuple(jax.ShapeDtypeStruct((n, c), F32) for c in widths),
        compiler_params=_params("parallel"), name="inproj")(x, w_main, wg)


def _inproj_seq_kernel(x_ref, w_ref, wgt_ref, pw_ref, sc_ref, wg_ref, wu_ref, wd_ref, wo_ref, wq_ref, wa_ref,
                       po_ref, tail_ref, q_ref, k_ref, v_ref, og_ref, git_ref, gft_ref,
                       cg_ref, cu_ref, cd_ref, co_ref, cq_ref, ca_ref, *rest, per):
    carry_ref = rest[-1]
    tm = x_ref.shape[0]
    start = lax.rem(pl.program_id(0), per) * tm

    @pl.when(start == 0)
    def _():
        carry_ref[...] = jnp.zeros_like(carry_ref)

    for src, dst in ((wg_ref, cg_ref), (wu_ref, cu_ref), (wd_ref, cd_ref), (wo_ref, co_ref), (wq_ref, cq_ref),
                     (wa_ref, ca_ref)):
        dst[...] = src[...].astype(BF16)
    if len(rest) == 2:
        rest[0][...] = jnp.zeros_like(rest[0])

    xb = x_ref[...].astype(BF16)
    u = _dot(xb, w_ref[:, 0:POOL_WIDTH])
    halo = carry_ref.shape[0]
    ext = jnp.concatenate([carry_ref[...], u], axis=0)
    tail = ext[tm:, :]
    carry_ref[...] = tail
    tail_ref[...] = tail
    pos = start + lax.broadcasted_iota(jnp.int32, (tm, POOL_GROUP_DIM), 0)
    for g, win in enumerate(POOL_WINDOWS):
        sl = slice(g * POOL_GROUP_DIM, (g + 1) * POOL_GROUP_DIM)
        acc = ext[:, sl]
        s = 1
        while s < win:
            acc = acc + pltpu.roll(acc, s, 0)
            s *= 2
        cnt = jnp.minimum(pos + 1, win).astype(F32)
        dlt = acc[halo:, :] / cnt - u[:, sl]
        po_ref[:, sl] = (_dot(dlt.astype(BF16), pw_ref[g]) * sc_ref[:, sl]).astype(po_ref.dtype)
    _qkvo(xb, w_ref, q_ref, k_ref, v_ref, og_ref)
    gt = _dot_nt(wgt_ref[...], xb)
    git_ref[...] = gt[0:SUBLANES]
    gft_ref[...] = gt[SUBLANES:2 * SUBLANES]


def _inproj_seq(x, w_main, wgt, pw, scale, experts, dense, layer, seq, tm, zeros_like=None):
    n, d = x.shape
    per = seq // tm
    steps = n // tm
    split = steps // N_EXPERTS
    assert split * N_EXPERTS == steps and d % steps == 0 and d % split == 0 and D_EXPERT % split == 0
    row = lambda c: pl.BlockSpec((tm, c), lambda i: (i, 0))
    full = lambda a: pl.BlockSpec(a.shape, lambda i: (0,) * a.ndim)
    gt_spec = pl.BlockSpec((None, SUBLANES, tm), lambda i: (i // per, 0, i % per))
    gt_shape = jax.ShapeDtypeStruct((n // seq, SUBLANES, seq), F32)
    mw, e, gf = MLSTM_WIDTH, EXPERTS_PER_GROUP, EXPERTS_PER_GROUP * D_EXPERT
    dr, fr, wr = d // split, D_EXPERT // split, d // steps
    col_in = pl.BlockSpec((None, None, dr, D_EXPERT), lambda i: (layer, i // split, i % split, 0))
    row_in = pl.BlockSpec((None, None, fr, d), lambda i: (layer, i // split, i % split, 0))
    col_out = pl.BlockSpec((None, dr, D_EXPERT), lambda i: (i // split // e, i % split, (i // split) % e))
    row_out = pl.BlockSpec((None, fr, d), lambda i: (i // split // e, ((i // split) % e) * split + i % split, 0))
    dense_in = pl.BlockSpec((None, wr, d), lambda i: (layer, i, 0))
    dense_out = pl.BlockSpec((wr, d), lambda i: (i, 0))
    out_shape = [
        jax.ShapeDtypeStruct((n, POOL_WIDTH), BF16),
        jax.ShapeDtypeStruct((steps, POOL_HALO, POOL_WIDTH), F32),
        jax.ShapeDtypeStruct((n, mw), BF16), jax.ShapeDtypeStruct((n, mw), BF16), jax.ShapeDtypeStruct((n, mw), BF16),
        jax.ShapeDtypeStruct((n, mw), F32), gt_shape, gt_shape,
        jax.ShapeDtypeStruct((N_GROUPS, d, gf), BF16), jax.ShapeDtypeStruct((N_GROUPS, d, gf), BF16),
        jax.ShapeDtypeStruct((N_GROUPS, gf, d), BF16),
        jax.ShapeDtypeStruct((d, d), BF16), jax.ShapeDtypeStruct((d, d), BF16), jax.ShapeDtypeStruct((d, d), BF16),
    ]
    out_specs = [row(POOL_WIDTH), pl.BlockSpec((None, POOL_HALO, POOL_WIDTH), lambda i: (i, 0, 0)),
                 row(mw), row(mw), row(mw), row(mw), gt_spec, gt_spec,
                 col_out, col_out, row_out, dense_out, dense_out, dense_out]
    if zeros_like is not None:
        lead = zeros_like.shape[0] * zeros_like.shape[1]
        assert lead % steps == 0 and zeros_like.shape[1] % (lead // steps) == 0
        zb = lead // steps
        zper = zeros_like.shape[1] // zb
        out_shape.append(jax.ShapeDtypeStruct(zeros_like.shape, zeros_like.dtype))
        out_specs.append(pl.BlockSpec((None, zb) + zeros_like.shape[2:],
                                      lambda i: (i // zper, i % zper) + (0,) * (zeros_like.ndim - 2)))
    scale = scale.reshape(1, POOL_WIDTH)
    return pl.pallas_call(
        functools.partial(_inproj_seq_kernel, per=per), grid=(steps,),
        in_specs=[row(d), full(w_main), full(wgt), full(pw), full(scale), col_in, col_in, row_in,
                  dense_in, dense_in, dense_in],
        out_specs=tuple(out_specs), out_shape=tuple(out_shape),
        scratch_shapes=[pltpu.VMEM((POOL_HALO, POOL_WIDTH), F32)],
        compiler_params=_params("arbitrary"), name="inproj_seq")(x, w_main, wgt, pw, scale, *experts, *dense)


def _pool_kernel(ext_ref, pw_ref, sc_ref, o_ref, *, period):
    rows = ext_ref.shape[0]
    r = lax.broadcasted_iota(jnp.int32, (rows, POOL_GROUP_DIM), 0)
    if period != rows:
        r = lax.rem(r, period)
    for g, win in enumerate(POOL_WINDOWS):
        sl = slice(g * POOL_GROUP_DIM, (g + 1) * POOL_GROUP_DIM)
        x = ext_ref[:, sl]
        acc = x
        s = 1
        while s < win:
            acc = acc + jnp.where(r >= s, pltpu.roll(acc, s, 0), 0.0)
            s *= 2
        cnt = jnp.minimum(r + 1, win).astype(F32)
        d = acc / cnt - x
        y = _dot(d.astype(BF16), pw_ref[g]) * sc_ref[:, sl]
        o_ref[:, sl] = y.astype(o_ref.dtype)


def _pool(ext2d, pw, scale, period):
    n, c = ext2d.shape
    row = pl.BlockSpec((POOL_ROWS, c), lambda i: (i, 0))
    return pl.pallas_call(
        functools.partial(_pool_kernel, period=period), grid=(n // POOL_ROWS,),
        in_specs=[row, pl.BlockSpec(pw.shape, lambda i: (0, 0, 0)), pl.BlockSpec((1, c), lambda i: (0, 0))],
        out_specs=row, out_shape=jax.ShapeDtypeStruct((n, c), BF16),
        compiler_params=_params("parallel"), name="pool")(ext2d, pw, scale.reshape(1, c))


def _mlstm_chunk_kernel(q_ref, k_ref, v_ref, git_ref, gft_ref, bit_ref, bft_ref, hn_ref, c_ref, n_ref, m_ref,
                        at_s, ws_s, dc_s, mc_s, wi_s, fl_s):
    nb, L, _ = q_ref.shape
    rows = nb * SUBLANES

    @pl.when(pl.program_id(0) == 0)
    def _():
        c_ref[...] = jnp.zeros_like(c_ref)
        n_ref[...] = jnp.zeros_like(n_ref)
        m_ref[...] = jnp.zeros_like(m_ref)

    ig = git_ref[...].reshape(rows, L) + bit_ref[...]
    lf = _log_sigmoid(gft_ref[...].reshape(rows, L) + bft_ref[...])
    tri = (lax.broadcasted_iota(jnp.int32, (L, L), 0) <= lax.broadcasted_iota(jnp.int32, (L, L), 1)).astype(BF16)
    lf_hi = lf.astype(BF16)
    lf_mid = (lf - lf_hi.astype(F32)).astype(BF16)
    lf_lo = (lf - lf_hi.astype(F32) - lf_mid.astype(F32)).astype(BF16)
    bc = _dot(lf_hi, tri) + (_dot(lf_mid, tri) + _dot(lf_lo, tri))
    at = ig - bc
    m_prev = m_ref[...].reshape(rows, LANES)[:, 0:1]
    mc = jnp.maximum(_scan(at, 1, jnp.maximum, -jnp.inf), m_prev)
    mt = bc + mc
    m_last = mc[:, L - 1:L]
    at_s[...] = at.reshape(nb, SUBLANES, L)
    ws_s[...] = jnp.exp(at - m_last).reshape(nb, SUBLANES, L)
    dc_s[...] = jnp.broadcast_to(jnp.exp(m_prev - m_last), (rows, LANES)).reshape(nb, SUBLANES, LANES)
    m_ref[...] = jnp.broadcast_to(mt[:, L - 1:L], (rows, LANES)).reshape(nb, SUBLANES, LANES)
    wi = jnp.exp(m_prev - mc)
    fl = jnp.exp(-mt)
    for b in range(nb):
        rs = slice(b * SUBLANES, (b + 1) * SUBLANES)
        mc_s[b] = mc[rs].T
        wi_s[b] = wi[rs].T
        fl_s[b] = fl[rs].T

    causal = (lax.broadcasted_iota(jnp.int32, (L, L), 0) >= lax.broadcasted_iota(jnp.int32, (L, L), 1))

    def per_batch(heads, b, carry):
        at = at_s[b]
        ws = ws_s[b]
        decay = dc_s[b]
        mc_c, wi_c, fl_c = mc_s[b], wi_s[b], fl_s[b]
        ws16 = jnp.concatenate([ws, ws], axis=0).astype(BF16)
        for h in heads:
            sl = slice(h * MLSTM_HEAD_DIM, (h + 1) * MLSTM_HEAD_DIM)
            col = slice(h, h + 1)
            qh = q_ref[b, :, sl]
            kh = k_ref[b, :, sl]
            vh = v_ref[b, :, sl]
            s = _dot_nt(qh, kh)
            p = jnp.where(causal, s * jnp.exp(at[col, :] - mc_c[:, col]), 0.0)
            c_old = c_ref[b, h]
            n_old = n_ref[b, col, :]
            wi = wi_c[:, col]
            dv = MLSTM_HEAD_DIM
            v_ext = jnp.concatenate([vh, jnp.ones_like(vh)], axis=1)
            c_ext = jnp.concatenate([c_old, jnp.broadcast_to(n_old, c_old.shape)], axis=0)
            intra = _dot(p.astype(BF16), v_ext)
            inter = _dot_nt(qh, c_ext.astype(BF16))
            num = intra[:, :dv] + wi * inter[:, :dv]
            qn = intra[:, dv:] + wi * inter[:, dv:]
            hh = num / jnp.maximum(jnp.abs(qn), fl_c[:, col])
            hn_ref[b, :, sl] = _unit_norm(hh)
            dc = decay[col, 0:1]
            vts = (vh.astype(F32).T * ws[col, :]).astype(BF16)
            c_ref[b, h] = dc * c_old + _dot(vts, kh)
            n_ref[b, col, :] = dc * n_old + _dot(ws16, kh)[col, :]
        return carry

    for h0 in range(0, MLSTM_HEADS, 2):
        lax.fori_loop(0, nb, functools.partial(per_batch, (h0, h0 + 1)), 0)


def _mlstm_chunk(q, k, v, git, gft, bit, bft):
    nb, t, w = q.shape
    L = MLSTM_CHUNK
    seq = lambda c: pl.BlockSpec((nb, L, c), lambda i: (0, i, 0))
    seqt = pl.BlockSpec((nb, SUBLANES, L), lambda i: (0, 0, i))
    vec = lambda a: pl.BlockSpec(a.shape, lambda i: (0, 0))
    out_shape = (
        jax.ShapeDtypeStruct((nb, t, w), F32),
        jax.ShapeDtypeStruct((nb, MLSTM_HEADS, MLSTM_HEAD_DIM, MLSTM_HEAD_DIM), F32),
        jax.ShapeDtypeStruct((nb, MLSTM_HEADS, MLSTM_HEAD_DIM), F32),
        jax.ShapeDtypeStruct((nb, SUBLANES, LANES), F32),
    )
    out_specs = (
        seq(w),
        pl.BlockSpec(out_shape[1].shape, lambda i: (0, 0, 0, 0)),
        pl.BlockSpec(out_shape[2].shape, lambda i: (0, 0, 0)),
        pl.BlockSpec(out_shape[3].shape, lambda i: (0, 0, 0)),
    )
    bit, bft = jnp.tile(bit, (nb, 1)), jnp.tile(bft, (nb, 1))
    scratch = [pltpu.VMEM((nb, SUBLANES, L), F32), pltpu.VMEM((nb, SUBLANES, L), F32),
               pltpu.VMEM((nb, SUBLANES, LANES), F32)] + [pltpu.VMEM((nb, L, SUBLANES), F32) for _ in range(3)]
    return pl.pallas_call(
        _mlstm_chunk_kernel, grid=(t // L,),
        in_specs=[seq(w), seq(w), seq(w), seqt, seqt, vec(bit), vec(bft)],
        out_specs=out_specs, out_shape=out_shape, scratch_shapes=scratch, compiler_params=_params("arbitrary"),
        name="mlstm_chunk")(q, k, v, git, gft, bit, bft)


def _mlstm_step_kernel(q_ref, k_ref, v_ref, gi_ref, gf_ref, bi_ref, bf_ref, c_ref, n_ref, m_ref, c_new_ref,
                       hn_ref, co_ref, no_ref, mo_ref):
    del c_new_ref
    tb = q_ref.shape[0]
    d = MLSTM_HEAD_DIM
    ig = gi_ref[...] + bi_ref[...]
    lf = _log_sigmoid(gf_ref[...] + bf_ref[...])
    m_old = m_ref[...]
    mt = jnp.maximum(lf + m_old, ig)
    mo_ref[...] = mt
    wa = jnp.exp(ig - mt)
    wi = jnp.exp(lf + m_old - mt)
    fl = jnp.exp(-mt)
    wa_t, wi_t, fl_t = wa.T, wi.T, fl.T
    lane = lax.broadcasted_iota(jnp.int32, (d, tb), 1)
    for h in range(MLSTM_HEADS):
        sl = slice(h * d, (h + 1) * d)
        qh, kh, vh = q_ref[:, sl], k_ref[:, sl], v_ref[:, sl]
        nh = n_ref[:, h, :]
        qt, kt, vt, nt = qh.T, kh.T, vh.T, nh.T
        wa_r, wi_r, fl_r = wa_t[h:h + 1, :], wi_t[h:h + 1, :], fl_t[h:h + 1, :]
        s = jnp.sum(qt * kt, 0, keepdims=True) * wa_r
        cq = jnp.zeros((d, tb), F32)
        qtb = qt.astype(BF16)
        for b in range(tb):
            cq = jnp.where(lane == b, _dot(c_ref[b, h].astype(BF16), qtb), cq)
        num = s * vt + wi_r * cq
        qn = s + wi_r * jnp.sum(nt * qt, 0, keepdims=True)
        hh = num / jnp.maximum(jnp.abs(qn), fl_r)
        mu = jnp.mean(hh, 0, keepdims=True)
        xc = hh - mu
        var = jnp.mean(xc * xc, 0, keepdims=True)
        hn_ref[:, sl] = (xc * lax.rsqrt(var + LN_EPS)).T
        wav = wa_r * vt
        for b in range(tb):
            co_ref[b, h] = wi_r[:, b:b + 1] * c_ref[b, h] + wav[:, b:b + 1] * kh[b:b + 1, :]
        no_ref[:, h, :] = wi[:, h:h + 1] * nh + wa[:, h:h + 1] * kh


def _mlstm_step(q, k, v, gi, gf, bi, bf, c_all, n_all, m0, c_new, layer, tb):
    n, w = q.shape
    d = MLSTM_HEAD_DIM
    row = lambda c: pl.BlockSpec((tb, c), lambda i: (i, 0))
    vec = lambda a: pl.BlockSpec(a.shape, lambda i: (0, 0))
    cin = pl.BlockSpec((None, tb, MLSTM_HEADS, d, d), lambda i: (layer, i, 0, 0, 0))
    nin = pl.BlockSpec((None, tb, MLSTM_HEADS, d), lambda i: (layer, i, 0, 0))
    nout = pl.BlockSpec((tb, MLSTM_HEADS, d), lambda i: (i, 0, 0))
    out_shape = (
        jax.ShapeDtypeStruct((n, w), F32),
        jax.ShapeDtypeStruct(c_new.shape, F32),
        jax.ShapeDtypeStruct(n_all.shape[1:], F32),
        jax.ShapeDtypeStruct((n, LANES), F32),
    )
    return pl.pallas_call(
        _mlstm_step_kernel, grid=(n // tb,),
        in_specs=[row(w), row(w), row(w), row(LANES), row(LANES), vec(bi), vec(bf), cin, nin, row(LANES),
                  pl.BlockSpec(memory_space=pl.ANY)],
        out_specs=(row(w), cin, nout, row(LANES)), out_shape=out_shape, input_output_aliases={10: 1},
        compiler_params=_params("parallel"), name="mlstm_step")(q, k, v, gi, gf, bi, bf, c_all, n_all, m0, c_new)


def _mixout_kernel(po_ref, hn_ref, og_ref, x_ref, ng_ref, w_ref, g_ref, b_ref, o_ref):
    mo = jax.nn.sigmoid(og_ref[...]) * hn_ref[...] * ng_ref[...]
    mix = _dot(po_ref[...], w_ref[0:POOL_WIDTH, :]) + _dot(mo.astype(BF16), w_ref[POOL_WIDTH:, :])
    o_ref[...] = _layer_norm(ALPHA * x_ref[...] + mix, g_ref[...], b_ref[...])


def _mixout(po, hn, og, x, ng, w, g, b, tm):
    n, d = x.shape
    row = lambda c: pl.BlockSpec((tm, c), lambda i: (i, 0))
    vec = lambda a: pl.BlockSpec(a.shape, lambda i: (0, 0))
    return pl.pallas_call(
        _mixout_kernel, grid=(n // tm,),
        in_specs=[row(POOL_WIDTH), row(MLSTM_WIDTH), row(MLSTM_WIDTH), row(d), vec(ng), vec(w), vec(g), vec(b)],
        out_specs=row(d), out_shape=jax.ShapeDtypeStruct((n, d), F32),
        compiler_params=_params("parallel"), name="mixout")(po, hn, og, x, ng, w, g, b)


def _mm_kernel(x_ref, w_ref, o_ref, *, scale):
    y = _dot(x_ref[...].astype(BF16), w_ref[...])
    if scale != 1.0:
        y = y * scale
    o_ref[...] = y.astype(o_ref.dtype)


def _mm(x, w, tm, out_dtype, scale=1.0, name="proj"):
    n, d = x.shape
    dout = w.shape[1]
    return pl.pallas_call(
        functools.partial(_mm_kernel, scale=scale), grid=(n // tm,),
        in_specs=[pl.BlockSpec((tm, d), lambda i: (i, 0)), pl.BlockSpec(w.shape, lambda i: (0, 0))],
        out_specs=pl.BlockSpec((tm, dout), lambda i: (i, 0)),
        out_shape=jax.ShapeDtypeStruct((n, dout), out_dtype),
        compiler_params=_params("parallel"), name=name)(x, w)


def _cast_inproj_kernel(wt_ref, o_ref, og_ref, *, n_main):
    i = pl.program_id(0)

    @pl.when(i < n_main)
    def _():
        o_ref[...] = wt_ref[...].T.astype(BF16)

    @pl.when(i == n_main)
    def _():
        og_ref[...] = wt_ref[0:2 * MLSTM_HEADS, :]


def _cast_inproj(w_in, layer, tc=256):
    wt = jnp.swapaxes(w_in, 1, 2)
    d = wt.shape[2]
    n_main = GATE_OFF // tc
    return pl.pallas_call(
        functools.partial(_cast_inproj_kernel, n_main=n_main), grid=(n_main + 1,),
        in_specs=[pl.BlockSpec((None, tc, d), lambda i: (layer, i, 0))],
        out_specs=(pl.BlockSpec((d, tc), lambda i: (0, jnp.minimum(i, n_main - 1))),
                   pl.BlockSpec((2 * MLSTM_HEADS, d), lambda i: (0, 0))),
        out_shape=(jax.ShapeDtypeStruct((d, GATE_OFF), BF16), jax.ShapeDtypeStruct((2 * MLSTM_HEADS, d), F32)),
        compiler_params=_params("arbitrary"), name="inproj_cast")(wt)


def _mem_proj_kernel(x_ref, wk_ref, wv_ref, k4_ref, v4_ref, k2_ref, v2_ref, wkb, wvb):
    @pl.when(pl.program_id(1) == 0)
    def _():
        wkb[...] = wk_ref[...].astype(BF16)
        wvb[...] = wv_ref[...].astype(BF16)

    bb, m, d = x_ref.shape
    xb = x_ref[...].reshape(bb * m, d).astype(BF16)
    k = _dot(xb, wkb[...])
    v = _dot(xb, wvb[...])
    k2_ref[...] = k.reshape(bb, m, d)
    v2_ref[...] = v.reshape(bb, m, d)
    k4_ref[...] = k.reshape(bb, m, CA_HEADS, CA_HEAD_DIM)
    v4_ref[...] = v.reshape(bb, m, CA_HEADS, CA_HEAD_DIM)


def _mem_proj(mem, wk_all, wv_all, bb=2):
    nb, m, d = mem.shape
    depth = wk_all.shape[0]
    wspec = pl.BlockSpec((None, d, d), lambda l, b: (l, 0, 0))
    o4 = pl.BlockSpec((None, bb, m, CA_HEADS, CA_HEAD_DIM), lambda l, b: (l, b, 0, 0, 0))
    o2 = pl.BlockSpec((None, bb, m, d), lambda l, b: (l, b, 0, 0))
    s4 = jax.ShapeDtypeStruct((depth, nb, m, CA_HEADS, CA_HEAD_DIM), F32)
    s2 = jax.ShapeDtypeStruct((depth, nb, m, d), F32)
    return pl.pallas_call(
        _mem_proj_kernel, grid=(depth, nb // bb),
        in_specs=[pl.BlockSpec((bb, m, d), lambda l, b: (b, 0, 0)), wspec, wspec],
        out_specs=(o4, o4, o2, o2), out_shape=(s4, s4, s2, s2),
        scratch_shapes=[pltpu.VMEM((d, d), BF16), pltpu.VMEM((d, d), BF16)],
        compiler_params=_params("arbitrary", "arbitrary"), name="mem_proj")(mem, wk_all, wv_all)


def _attn_block_kernel(po_ref, hn_ref, og_ref, x_ref, k_ref, v_ref, ng_ref, wm_ref, g1_ref, b1_ref,
                       wq_ref, wo_ref, g_ref, b_ref, o_ref):
    mo = jax.nn.sigmoid(og_ref[...]) * hn_ref[...] * ng_ref[...]
    mix = _dot(po_ref[...], wm_ref[0:POOL_WIDTH, :]) + _dot(mo.astype(BF16), wm_ref[POOL_WIDTH:, :])
    x = _layer_norm(ALPHA * x_ref[...] + mix, g1_ref[...], b1_ref[...])
    qb = (_dot(x.astype(BF16), wq_ref[...]) * (CA_HEAD_DIM ** -0.5)).astype(BF16)
    kb = k_ref[...].astype(BF16)
    vb = v_ref[...].astype(BF16)
    ctx = []
    for h in range(CA_HEADS):
        sl = slice(h * CA_HEAD_DIM, (h + 1) * CA_HEAD_DIM)
        s = _dot_nt(qb[:, sl], kb[:, sl])
        e = jnp.exp(s - jnp.max(s, -1, keepdims=True))
        ctx.append((_dot(e.astype(BF16), vb[:, sl]) / jnp.sum(e, -1, keepdims=True)).astype(BF16))
    y = _dot(jnp.concatenate(ctx, axis=1), wo_ref[...])
    o_ref[...] = _layer_norm(ALPHA * x + y, g_ref[...], b_ref[...])


def _attn_block(po, hn, og, x, k, v, layer, ng, wm, ln1, wq, wo, ln2, seq, tq):
    n, d = x.shape
    per = seq // tq
    row = lambda c: pl.BlockSpec((tq, c), lambda i: (i, 0))
    kspec = pl.BlockSpec((None, None, N_MEM, d), lambda i: (layer, i // per, 0, 0))
    vec = lambda a: pl.BlockSpec(a.shape, lambda i: (0, 0))
    return pl.pallas_call(
        _attn_block_kernel, grid=(n // tq,),
        in_specs=[row(POOL_WIDTH), row(MLSTM_WIDTH), row(MLSTM_WIDTH), row(d), kspec, kspec, vec(ng), vec(wm),
                  vec(ln1[0]), vec(ln1[1]), vec(wq), vec(wo), vec(ln2[0]), vec(ln2[1])],
        out_specs=row(d), out_shape=jax.ShapeDtypeStruct((n, d), F32),
        compiler_params=_params("parallel"), name="attn_block")(po, hn, og, x, k, v, ng, wm, *ln1, wq, wo, *ln2)


def _attn_step_kernel(q_ref, k_ref, v_ref, o_ref):
    bb = q_ref.shape[0]
    rows = N_MEM * CA_HEADS
    lane = lax.broadcasted_iota(jnp.int32, (SUBLANES, rows), 1)
    row = lax.broadcasted_iota(jnp.int32, (SUBLANES, rows), 0)
    own = lax.rem(lane, CA_HEADS) == lax.rem(row, CA_HEADS)
    for j in range(bb):
        kf = k_ref[j].reshape(rows, CA_HEAD_DIM).astype(BF16)
        vf = v_ref[j].reshape(rows, CA_HEAD_DIM).astype(BF16)
        s = jnp.where(own, _dot_nt(q_ref[j].astype(BF16), kf), -jnp.inf)
        e = jnp.exp(s - jnp.max(s, -1, keepdims=True))
        o_ref[j] = _dot(e.astype(BF16), vf) / jnp.sum(e, -1, keepdims=True)


def _attn_step(q, k_all, v_all, layer, bb):
    n = q.shape[0]
    qspec = pl.BlockSpec((bb, SUBLANES, CA_HEAD_DIM), lambda i: (i, 0, 0))
    kspec = pl.BlockSpec((None, bb, N_MEM, CA_HEADS, CA_HEAD_DIM), lambda i: (layer, i, 0, 0, 0))
    return pl.pallas_call(
        _attn_step_kernel, grid=(n // bb,), in_specs=[qspec, kspec, kspec], out_specs=qspec,
        out_shape=jax.ShapeDtypeStruct((n, SUBLANES, CA_HEAD_DIM), F32),
        compiler_params=_params("parallel"), name="attn_step")(q, k_all, v_all)


def _mm_res_ln_kernel(a_ref, x_ref, w_ref, g_ref, b_ref, o_ref):
    y = _dot(a_ref[...].astype(BF16), w_ref[...])
    o_ref[...] = _layer_norm(ALPHA * x_ref[...] + y, g_ref[...], b_ref[...])


def _mm_res_ln(a, x, w, g, b, tm):
    n, d = x.shape
    row = lambda c: pl.BlockSpec((tm, c), lambda i: (i, 0))
    vec = lambda arr: pl.BlockSpec(arr.shape, lambda i: (0, 0))
    return pl.pallas_call(
        _mm_res_ln_kernel, grid=(n // tm,),
        in_specs=[row(a.shape[1]), row(d), vec(w), vec(g), vec(b)], out_specs=row(d),
        out_shape=jax.ShapeDtypeStruct((n, d), F32),
        compiler_params=_params("parallel"), name="proj_res_ln")(a, x, w, g, b)


def _route(lt):
    gl = [lt[g:g + 1, :] for g in range(N_GROUPS)]
    gmax = functools.reduce(jnp.maximum, gl)
    gsum = functools.reduce(jnp.add, [jnp.exp(x - gmax) for x in gl])
    pg_sel = 1.0 / gsum

    def first_max(vals):
        m = functools.reduce(jnp.maximum, vals)
        taken = jnp.zeros_like(m, dtype=jnp.bool_)
        hot = []
        for x in vals:
            h = jnp.logical_and(x == m, jnp.logical_not(taken))
            taken = jnp.logical_or(taken, h)
            hot.append(h)
        return m, hot

    _, g_hot = first_max(gl)
    el = []
    for j in range(EXPERTS_PER_GROUP):
        rows = [lt[SUBLANES + g * EXPERTS_PER_GROUP + j:SUBLANES + g * EXPERTS_PER_GROUP + j + 1, :]
                for g in range(N_GROUPS)]
        x = rows[N_GROUPS - 1]
        for g in range(N_GROUPS - 2, -1, -1):
            x = jnp.where(g_hot[g], rows[g], x)
        el.append(x)
    emax = functools.reduce(jnp.maximum, el)
    ee = [jnp.exp(x - emax) for x in el]
    esum = functools.reduce(jnp.add, ee)
    pe = [x / esum for x in ee]
    p1, hot1 = first_max(pe)
    p2, hot2 = first_max([jnp.where(h, -jnp.inf, x) for h, x in zip(hot1, pe)])
    psum = p1 + p2
    gate = [jnp.where(h1, pg_sel * p1 / psum, jnp.where(h2, pg_sel * p2 / psum, 0.0)) for h1, h2 in zip(hot1, hot2)]
    return g_hot, gate


def _moe_kernel(x_ref, wr_ref, br_ref, tri_ref, wg_ref, wu_ref, wd_ref, g_ref, b_ref, o_ref,
                xb_ref, ct_ref, comb_ref, perm_ref, acc_ref, full_ref):
    grp = pl.program_id(1)
    tm = x_ref.shape[0]
    ns, cap, sw = perm_ref.shape[1:]
    f = D_EXPERT

    @pl.when(grp == 0)
    def _():
        x = x_ref[...]
        xh = x.astype(BF16)
        xb_ref[...] = xh
        xl = (x - xh.astype(F32)).astype(BF16)
        wr = wr_ref[...]
        wh = wr.astype(BF16)
        wl = (wr - wh.astype(F32)).astype(BF16)
        lt = _dot_nt(wh, xh) + (_dot_nt(wl, xh) + _dot_nt(wh, xl)) + br_ref[...]
        g_hot, gate = _route(lt)
        row8 = lax.broadcasted_iota(jnp.int32, (SUBLANES, tm), 0)
        hot8 = jnp.zeros((SUBLANES, tm), F32)
        for g in range(N_GROUPS):
            hot8 = jnp.where(jnp.logical_and(row8 == g, g_hot[g]), 1.0, hot8)
        most = None
        for s in range(ns):
            hs = hot8[:, s * sw:(s + 1) * sw]
            cum = _dot(hs.astype(BF16), tri_ref[...])
            seg_max = jnp.max(cum[:, sw - 1:sw])
            most = seg_max if most is None else jnp.maximum(most, seg_max)
            base = lax.broadcasted_iota(jnp.int32, (SUBLANES, sw), 0).astype(F32) * cap
            code = jnp.sum(hs * (base + cum - 1.0), 0, keepdims=True).astype(jnp.int32)
            for g in range(N_GROUPS):
                r = lax.broadcasted_iota(jnp.int32, (cap, sw), 0) + g * cap
                perm_ref[g, s] = jnp.where(r == code, 1.0, 0.0).astype(BF16)
        full_ref[0] = (most > cap).astype(jnp.int32)
        ct_ref[...] = jnp.zeros_like(ct_ref)
        for j in range(EXPERTS_PER_GROUP):
            ct_ref[j:j + 1, :] = gate[j]
        ct_ref[EXPERTS_PER_GROUP:EXPERTS_PER_GROUP + 1, :] = jnp.sum(hot8 * row8.astype(F32), 0, keepdims=True)
        comb_ref[...] = ct_ref[...].T
        acc_ref[...] = jnp.zeros_like(acc_ref)

    def experts(rows, gates):
        hh = []
        for j in range(EXPERTS_PER_GROUP):
            hg = _dot(rows, wg_ref[:, j * f:(j + 1) * f])
            hu = _dot(rows, wu_ref[:, j * f:(j + 1) * f])
            hh.append((hg * jax.nn.sigmoid(hg) * hu * gates[:, j:j + 1]).astype(BF16))
        return _dot(jnp.concatenate(hh, axis=1), wd_ref[...])

    @pl.when(full_ref[0] == 0)
    def _():
        comb = comb_ref[...]
        ch = comb.astype(BF16)
        cl = (comb - ch.astype(F32)).astype(BF16)
        xb = xb_ref[...]
        perm = [perm_ref[grp, s] for s in range(ns)]
        seg = lambda a, s: a[s * sw:(s + 1) * sw]
        rows = jnp.concatenate([_dot(perm[s], seg(xb, s)) for s in range(ns)], axis=0).astype(BF16)
        gates = jnp.concatenate([_dot(perm[s], seg(ch, s)) + _dot(perm[s], seg(cl, s)) for s in range(ns)], axis=0)
        y = experts(rows, gates).astype(BF16)
        for s in range(ns):
            acc_ref[s * sw:(s + 1) * sw, :] += _dot_tn(perm[s], y[s * cap:(s + 1) * cap])

    @pl.when(full_ref[0] != 0)
    def _():
        comb = comb_ref[...]
        own = comb[:, EXPERTS_PER_GROUP:EXPERTS_PER_GROUP + 1] == grp.astype(F32)
        acc_ref[...] += experts(xb_ref[...], jnp.where(own, comb, 0.0))

    @pl.when(grp == N_GROUPS - 1)
    def _():
        o_ref[...] = _layer_norm(ALPHA * x_ref[...] + acc_ref[...], g_ref[...], b_ref[...])


def _moe(x, wr, br, wg, wu, wd, g, b, tm):
    n, d = x.shape
    ns = max(1, tm // MOE_SEGMENT)
    sw = tm // ns
    cap = min(sw, MOE_CAP)
    tri = (lax.broadcasted_iota(jnp.int32, (sw, sw), 0) <= lax.broadcasted_iota(jnp.int32, (sw, sw), 1)).astype(BF16)
    row = pl.BlockSpec((tm, d), lambda i, e: (i, 0))
    vec = lambda a: pl.BlockSpec(a.shape, lambda i, e: (0, 0))
    grp = lambda a: pl.BlockSpec((None,) + a.shape[1:], lambda i, e: (e, 0, 0))
    scratch = [pltpu.VMEM((tm, d), BF16), pltpu.VMEM((LANES, tm), F32), pltpu.VMEM((tm, LANES), F32),
               pltpu.VMEM((N_GROUPS, ns, cap, sw), BF16), pltpu.VMEM((tm, d), F32), pltpu.SMEM((1,), jnp.int32)]
    return pl.pallas_call(
        _moe_kernel, grid=(n // tm, N_GROUPS),
        in_specs=[row, vec(wr), vec(br), vec(tri), grp(wg), grp(wu), grp(wd), vec(g), vec(b)],
        out_specs=row, out_shape=jax.ShapeDtypeStruct((n, d), F32), scratch_shapes=scratch,
        compiler_params=_params("parallel", "arbitrary"), name="moe")(x, wr, br, tri, wg, wu, wd, g, b)


def _pad_lanes(v):
    return jnp.zeros((1, LANES), F32).at[0, :v.shape[0]].set(v)


def _pad_rows(v):
    return jnp.zeros((SUBLANES, 1), F32).at[:v.shape[0], 0].set(v)


def _layer_weights(l, w_in, b_i, b_f, pool_w, pool_scale, mlstm_norm_g, w_out, ln1_g, ln1_b, ca_wq, ca_wo,
                   ln2_g, ln2_b, w_gr, b_gr, w_er, b_er, w_gate, w_up, w_down, ln3_g, ln3_b):
    d = D_MODEL
    w_main, w_gate_rows = _cast_inproj(w_in, l)
    w_gate_cols = w_gate_rows.T
    wg = jnp.zeros((d, 2 * LANES), F32)
    wg = wg.at[:, 0:MLSTM_HEADS].set(w_gate_cols[:, :MLSTM_HEADS])
    wg = wg.at[:, LANES:LANES + MLSTM_HEADS].set(w_gate_cols[:, MLSTM_HEADS:])
    wgt = jnp.zeros((2 * SUBLANES, d), F32)
    wgt = wgt.at[0:MLSTM_HEADS].set(w_gate_cols[:, :MLSTM_HEADS].T)
    wgt = wgt.at[SUBLANES:SUBLANES + MLSTM_HEADS].set(w_gate_cols[:, MLSTM_HEADS:].T)
    wr = jnp.zeros((ROUTER_ROWS, d), F32)
    wr = wr.at[0:N_GROUPS].set(w_gr[l].T).at[SUBLANES:SUBLANES + N_EXPERTS].set(w_er[l].T)
    br = jnp.zeros((ROUTER_ROWS, 1), F32)
    br = br.at[0:N_GROUPS, 0].set(b_gr[l]).at[SUBLANES:SUBLANES + N_EXPERTS, 0].set(b_er[l])
    row = lambda v: v.reshape(1, -1)
    return dict(
        w_main=w_main, wg=wg.astype(BF16), wgt=wgt.astype(BF16),
        bi=_pad_lanes(b_i[l]), bf=_pad_lanes(b_f[l]), bit=_pad_rows(b_i[l]), bft=_pad_rows(b_f[l]),
        pool_w=pool_w[l].astype(BF16), pool_scale=pool_scale[l], norm_g=row(mlstm_norm_g[l]),
        ln1=(row(ln1_g[l]), row(ln1_b[l])), ln2=(row(ln2_g[l]), row(ln2_b[l])), ln3=(row(ln3_g[l]), row(ln3_b[l])),
        wr=wr, br=br, raw_experts=(w_gate, w_up, w_down), raw_dense=(w_out, ca_wq, ca_wo))


def _trunk_layer(x, nb, seq, layer, mem, pool_prev, state, p, tm, tm_moe):
    chunked = state is None
    if chunked:
        per = seq // tm
        outs = _inproj_seq(x, p["w_main"], p["wgt"], p["pool_w"], p["pool_scale"], p["raw_experts"], p["raw_dense"],
                           layer, seq, tm, zeros_like=p.get("zeros_like"))
        pooled, tails, q, k, v, og, git, gft = outs[:8]
        p["experts"] = outs[8:11]
        p["w_out"], p["wq"], p["wo"] = outs[11:14]
        p["zeros"] = outs[14] if len(outs) > 14 else None
        pool_buf = tails.reshape(nb, per, POOL_HALO, POOL_WIDTH)[:, per - 1, POOL_HALO - POOL_BUF:]
    else:
        u, q, k, v, og, gi, gf = _inproj(x, p["w_main"], p["wg"], tm)
        ext = jnp.concatenate([pool_prev, u.reshape(nb, seq, POOL_WIDTH)], axis=1)
        period = ext.shape[1]
        pooled = _pool(ext.reshape(nb * period, POOL_WIDTH), p["pool_w"], p["pool_scale"], period)
        pooled = pooled.reshape(nb, period, POOL_WIDTH)[:, period - seq:].reshape(nb * seq, POOL_WIDTH)
        pool_buf = ext[:, -POOL_BUF:]
    if chunked:
        r3 = lambda a: a.reshape(nb, seq, a.shape[-1])
        hn, c1, n1, m1 = _mlstm_chunk(r3(q), r3(k), r3(v), git, gft, p["bit"], p["bft"])
        hn = hn.reshape(nb * seq, MLSTM_WIDTH)
        m1 = m1[:, :MLSTM_HEADS, 0]
    else:
        c_all, n_all, m0, c_new = state
        m0p = jnp.zeros((nb, LANES), F32).at[:, :MLSTM_HEADS].set(m0)
        hn, c1, n1, m1 = _mlstm_step(q, k, v, gi, gf, p["bi"], p["bf"], c_all, n_all, m0p, c_new, layer, MLSTM_STEP_SEQS)
        m1 = m1[:, :MLSTM_HEADS]
    if chunked:
        x = _attn_block(pooled, hn, og, x, mem[0], mem[1], layer, p["norm_g"], p["w_out"], p["ln1"], p["wq"],
                        p["wo"], p["ln2"], seq, tm_moe)
    else:
        x = _mixout(pooled, hn, og, x, p["norm_g"], p["w_out"], *p["ln1"], tm)
        qc = _mm(x, p["wq"], tm, F32, scale=CA_HEAD_DIM ** -0.5, name="ca_q")
        qh = jnp.zeros((nb, SUBLANES, CA_HEAD_DIM), F32).at[:, :CA_HEADS].set(qc.reshape(nb, CA_HEADS, CA_HEAD_DIM))
        ctx = _attn_step(qh, mem[0], mem[1], layer, ATTN_STEP_SEQS)[:, :CA_HEADS].reshape(nb, D_MODEL)
        x = _mm_res_ln(ctx, x, p["wo"], *p["ln2"], tm)
    x = _moe(x, p["wr"], p["br"], *p["experts"], *p["ln3"], tm_moe)
    return x, pool_buf, c1, n1, m1


def kernel(x_prompt, x_sample, mem_prompt, cache_pool, state_mlstm_C, state_mlstm_n, state_mlstm_m,
           cache_mem_k, cache_mem_v, emb_ln_g, emb_ln_b, w_in, b_i, b_f, pool_w, pool_scale,
           mlstm_norm_g, w_out, ln1_g, ln1_b, ca_wq, ca_wk, ca_wv, ca_wo, ln2_g, ln2_b,
           w_gr, b_gr, w_er, b_er, w_gate, w_up, w_down, ln3_g, ln3_b):
    bp, tp, d = x_prompt.shape
    bs, ts, _ = x_sample.shape
    tm_p, tm_s = 512, bs * ts
    xp = _ln(x_prompt.reshape(bp * tp, d), emb_ln_g, emb_ln_b, tm_p)
    xs = _ln(x_sample.reshape(bs * ts, d), emb_ln_g, emb_ln_b, tm_s)
    mk4, mv4, mk2, mv2 = _mem_proj(mem_prompt, ca_wk, ca_wv)
    outs = [[] for _ in range(7)]
    sc = None
    for l in range(DEPTH):
        p = _layer_weights(l, w_in, b_i, b_f, pool_w, pool_scale, mlstm_norm_g, w_out, ln1_g, ln1_b, ca_wq,
                           ca_wo, ln2_g, ln2_b, w_gr, b_gr, w_er, b_er, w_gate, w_up, w_down, ln3_g, ln3_b)
        if l == 0:
            p["zeros_like"] = state_mlstm_C
        xp, pb, c1, n1, m1 = _trunk_layer(xp, bp, tp, l, (mk2, mv2), None, None, p, tm_p, 2 * tm_p)
        if l == 0:
            sc = p["zeros"]
        xs, sb, sc, ns, ms = _trunk_layer(xs, bs, ts, l, (cache_mem_k, cache_mem_v), cache_pool[l],
                                          (state_mlstm_C, state_mlstm_n, state_mlstm_m[l], sc), p, tm_s, tm_s)
        for lst, val in zip(outs, (pb, c1, n1, m1, sb, ns, ms)):
            lst.append(val)
    pp, pc, pn, pm, sp, sn, sm = (jnp.stack(o) for o in outs)
    return (xp.reshape(bp, tp, d), xs.reshape(bs, ts, d), pp, pc, pn, pm, mk4, mv4, sp, sc, sn, sm)
```

```python
import functools

import jax
import jax.numpy as jnp
from jax import lax
from jax.experimental import pallas as pl
from jax.experimental.pallas import tpu as pltpu

F32 = jnp.float32
BF16 = jnp.bfloat16

D_MODEL = 1024
DEPTH = 4
POOL_WIDTH = 512
POOL_GROUPS = 4
POOL_GROUP_DIM = 128
POOL_WINDOWS = (2, 4, 8, 16)
POOL_BUF = 15
MLSTM_WIDTH = 512
MLSTM_HEADS = 4
MLSTM_HEAD_DIM = 128
N_MEM = 256
CA_HEADS = 4
CA_HEAD_DIM = 256
N_GROUPS = 4
EXPERTS_PER_GROUP = 4
N_EXPERTS = 16
D_EXPERT = 256
ALPHA = (2 * DEPTH) ** 0.25
LN_EPS = 1e-5
GATE_OFF = POOL_WIDTH + 4 * MLSTM_WIDTH

LANES = 128
SUBLANES = 8
VMEM_LIMIT = 56 * 1024 * 1024
MLSTM_CHUNK = 256
POOL_ROWS = 2048
MLSTM_STEP_SEQS = 16
POOL_HALO = 16
ROUTER_ROWS = 32
MOE_SEGMENT = 512
MOE_CAP = 160


def _params(*sem):
    return pltpu.CompilerParams(dimension_semantics=sem, vmem_limit_bytes=VMEM_LIMIT)


def _dot(a, b):
    return jnp.dot(a, b, preferred_element_type=F32)


def _dot_nt(a, b, precision=None):
    return lax.dot_general(a, b, (((1,), (1,)), ((), ())), precision=precision,
                           preferred_element_type=F32)


def _dot_tn(a, b):
    return lax.dot_general(a, b, (((0,), (0,)), ((), ())), preferred_element_type=F32)


def _layer_norm(x, g, b):
    mu = jnp.mean(x, -1, keepdims=True)
    xc = x - mu
    var = jnp.mean(xc * xc, -1, keepdims=True)
    return xc * lax.rsqrt(var + LN_EPS) * g + b


def _unit_norm(x):
    mu = jnp.mean(x, -1, keepdims=True)
    xc = x - mu
    var = jnp.mean(xc * xc, -1, keepdims=True)
    return xc * lax.rsqrt(var + LN_EPS)


def _log_sigmoid(x):
    return jnp.minimum(x, 0.0) - jnp.log1p(jnp.exp(-jnp.abs(x)))


def _scan(x, axis, op, fill):
    n = x.shape[axis]
    idx = lax.broadcasted_iota(jnp.int32, x.shape, axis)
    s = 1
    while s < n:
        x = op(x, jnp.where(idx >= s, pltpu.roll(x, s, axis), fill))
        s *= 2
    return x


def _ln_kernel(x_ref, g_ref, b_ref, o_ref):
    o_ref[...] = _layer_norm(x_ref[...], g_ref[...], b_ref[...])


def _ln(x, g, b, tm):
    n, d = x.shape
    row = pl.BlockSpec((tm, d), lambda i: (i, 0))
    vec = pl.BlockSpec((1, d), lambda i: (0, 0))
    return pl.pallas_call(
        _ln_kernel, grid=(n // tm,), in_specs=[row, vec, vec], out_specs=row,
        out_shape=jax.ShapeDtypeStruct((n, d), F32), compiler_params=_params("parallel"),
        name="emb_ln")(x, g.reshape(1, d), b.reshape(1, d))


def _qkvo(xb, w_ref, q_ref, k_ref, v_ref, og_ref):
    w = MLSTM_WIDTH
    q = _dot(xb, w_ref[:, POOL_WIDTH:POOL_WIDTH + w]) * (MLSTM_HEAD_DIM ** -0.5)
    q_ref[...] = q.astype(q_ref.dtype)
    k_ref[...] = _dot(xb, w_ref[:, POOL_WIDTH + w:POOL_WIDTH + 2 * w]).astype(k_ref.dtype)
    v_ref[...] = _dot(xb, w_ref[:, POOL_WIDTH + 2 * w:POOL_WIDTH + 3 * w]).astype(v_ref.dtype)
    og_ref[...] = _dot(xb, w_ref[:, POOL_WIDTH + 3 * w:POOL_WIDTH + 4 * w])


def _inproj_kernel(x_ref, w_ref, wg_ref, u_ref, q_ref, k_ref, v_ref, og_ref, gi_ref, gf_ref):
    xb = x_ref[...].astype(BF16)
    u_ref[...] = _dot(xb, w_ref[:, 0:POOL_WIDTH])
    _qkvo(xb, w_ref, q_ref, k_ref, v_ref, og_ref)
    g = _dot(xb, wg_ref[...])
    gi_ref[...] = g[:, 0:LANES]
    gf_ref[...] = g[:, LANES:2 * LANES]


def _inproj(x, w_main, wg, tm):
    n, d = x.shape
    row = lambda c: pl.BlockSpec((tm, c), lambda i: (i, 0))
    full = lambda a: pl.BlockSpec(a.shape, lambda i: (0, 0))
    widths = (POOL_WIDTH, MLSTM_WIDTH, MLSTM_WIDTH, MLSTM_WIDTH, MLSTM_WIDTH, LANES, LANES)
    return pl.pallas_call(
        _inproj_kernel, grid=(n // tm,), in_specs=[row(d), full(w_main), full(wg)],
        out_specs=tuple(row(c) for c in widths),
        out_shape=tuple(jax.ShapeDtypeStruct((n, c), F32) for c in widths),
        compiler_params=_params("parallel"), name="inproj")(x, w_main, wg)


def _inproj_seq_kernel(x_ref, w_ref, wgt_ref, pw_ref, sc_ref, wg_ref, wu_ref, wd_ref, wo_ref, wq_ref, wa_ref,
                       po_ref, tail_ref, q_ref, k_ref, v_ref, og_ref, git_ref, gft_ref,
                       cg_ref, cu_ref, cd_ref, co_ref, cq_ref, ca_ref, *rest, per):
    carry_ref = rest[-1]
    tm = x_ref.shape[0]
    start = lax.rem(pl.program_id(0), per) * tm

    @pl.when(start == 0)
    def _():
        carry_ref[...] = jnp.zeros_like(carry_ref)

    for src, dst in ((wg_ref, cg_ref), (wu_ref, cu_ref), (wd_ref, cd_ref), (wo_ref, co_ref), (wq_ref, cq_ref),
                     (wa_ref, ca_ref)):
        dst[...] = src[...].astype(BF16)
    if len(rest) == 2:
        rest[0][...] = jnp.zeros_like(rest[0])

    xb = x_ref[...].astype(BF16)
    u = _dot(xb, w_ref[:, 0:POOL_WIDTH])
    halo = carry_ref.shape[0]
    ext = jnp.concatenate([carry_ref[...], u], axis=0)
    tail = ext[tm:, :]
    carry_ref[...] = tail
    tail_ref[...] = tail
    pos = start + lax.broadcasted_iota(jnp.int32, (tm, POOL_GROUP_DIM), 0)
    for g, win in enumerate(POOL_WINDOWS):
        sl = slice(g * POOL_GROUP_DIM, (g + 1) * POOL_GROUP_DIM)
        acc = ext[:, sl]
        s = 1
        while s < win:
            acc = acc + pltpu.roll(acc, s, 0)
            s *= 2
        cnt = jnp.minimum(pos + 1, win).astype(F32)
        dlt = acc[halo:, :] / cnt - u[:, sl]
        po_ref[:, sl] = (_dot(dlt.astype(BF16), pw_ref[g]) * sc_ref[:, sl]).astype(po_ref.dtype)
    _qkvo(xb, w_ref, q_ref, k_ref, v_ref, og_ref)
    gt = _dot_nt(wgt_ref[...], xb)
    git_ref[...] = gt[0:SUBLANES]
    gft_ref[...] = gt[SUBLANES:2 * SUBLANES]


def _inproj_seq(x, w_main, wgt, pw, scale, experts, dense, layer, seq, tm, zeros_like=None):
    n, d = x.shape
    per = seq // tm
    steps = n // tm
    split = steps // N_EXPERTS
    assert split * N_EXPERTS == steps and d % steps == 0 and d % split == 0 and D_EXPERT % split == 0
    row = lambda c: pl.BlockSpec((tm, c), lambda i: (i, 0))
    full = lambda a: pl.BlockSpec(a.shape, lambda i: (0,) * a.ndim)
    gt_spec = pl.BlockSpec((None, SUBLANES, tm), lambda i: (i // per, 0, i % per))
    gt_shape = jax.ShapeDtypeStruct((n // seq, SUBLANES, seq), F32)
    mw, e, gf = MLSTM_WIDTH, EXPERTS_PER_GROUP, EXPERTS_PER_GROUP * D_EXPERT
    dr, fr, wr = d // split, D_EXPERT // split, d // steps
    col_in = pl.BlockSpec((None, None, dr, D_EXPERT), lambda i: (layer, i // split, i % split, 0))
    row_in = pl.BlockSpec((None, None, fr, d), lambda i: (layer, i // split, i % split, 0))
    col_out = pl.BlockSpec((None, dr, D_EXPERT), lambda i: (i // split // e, i % split, (i // split) % e))
    row_out = pl.BlockSpec((None, fr, d), lambda i: (i // split // e, ((i // split) % e) * split + i % split, 0))
    dense_in = pl.BlockSpec((None, wr, d), lambda i: (layer, i, 0))
    dense_out = pl.BlockSpec((wr, d), lambda i: (i, 0))
    out_shape = [
        jax.ShapeDtypeStruct((n, POOL_WIDTH), BF16),
        jax.ShapeDtypeStruct((steps, POOL_HALO, POOL_WIDTH), F32),
        jax.ShapeDtypeStruct((n, mw), BF16), jax.ShapeDtypeStruct((n, mw), BF16), jax.ShapeDtypeStruct((n, mw), BF16),
        jax.ShapeDtypeStruct((n, mw), F32), gt_shape, gt_shape,
        jax.ShapeDtypeStruct((N_GROUPS, d, gf), BF16), jax.ShapeDtypeStruct((N_GROUPS, d, gf), BF16),
        jax.ShapeDtypeStruct((N_GROUPS, gf, d), BF16),
        jax.ShapeDtypeStruct((d, d), BF16), jax.ShapeDtypeStruct((d, d), BF16), jax.ShapeDtypeStruct((d, d), BF16),
    ]
    out_specs = [row(POOL_WIDTH), pl.BlockSpec((None, POOL_HALO, POOL_WIDTH), lambda i: (i, 0, 0)),
                 row(mw), row(mw), row(mw), row(mw), gt_spec, gt_spec,
                 col_out, col_out, row_out, dense_out, dense_out, dense_out]
    if zeros_like is not None:
        lead = zeros_like.shape[0] * zeros_like.shape[1]
        assert lead % steps == 0 and zeros_like.shape[1] % (lead // steps) == 0
        zb = lead // steps
        zper = zeros_like.shape[1] // zb
        out_shape.append(jax.ShapeDtypeStruct(zeros_like.shape, zeros_like.dtype))
        out_specs.append(pl.BlockSpec((None, zb) + zeros_like.shape[2:],
                                      lambda i: (i // zper, i % zper) + (0,) * (zeros_like.ndim - 2)))
    scale = scale.reshape(1, POOL_WIDTH)
    return pl.pallas_call(
        functools.partial(_inproj_seq_kernel, per=per), grid=(steps,),
        in_specs=[row(d), full(w_main), full(wgt), full(pw), full(scale), col_in, col_in, row_in,
                  dense_in, dense_in, dense_in],
        out_specs=tuple(out_specs), out_shape=tuple(out_shape),
        scratch_shapes=[pltpu.VMEM((POOL_HALO, POOL_WIDTH), F32)],
        compiler_params=_params("arbitrary"), name="inproj_seq")(x, w_main, wgt, pw, scale, *experts, *dense)


def _pool_kernel(ext_ref, pw_ref, sc_ref, o_ref, *, period):
    rows = ext_ref.shape[0]
    r = lax.broadcasted_iota(jnp.int32, (rows, POOL_GROUP_DIM), 0)
    if period != rows:
        r = lax.rem(r, period)
    for g, win in enumerate(POOL_WINDOWS):
        sl = slice(g * POOL_GROUP_DIM, (g + 1) * POOL_GROUP_DIM)
        x = ext_ref[:, sl]
        acc = x
        s = 1
        while s < win:
            acc = acc + jnp.where(r >= s, pltpu.roll(acc, s, 0), 0.0)
            s *= 2
        cnt = jnp.minimum(r + 1, win).astype(F32)
        d = acc / cnt - x
        y = _dot(d.astype(BF16), pw_ref[g]) * sc_ref[:, sl]
        o_ref[:, sl] = y.astype(o_ref.dtype)


def _pool(ext2d, pw, scale, period):
    n, c = ext2d.shape
    row = pl.BlockSpec((POOL_ROWS, c), lambda i: (i, 0))
    return pl.pallas_call(
        functools.partial(_pool_kernel, period=period), grid=(n // POOL_ROWS,),
        in_specs=[row, pl.BlockSpec(pw.shape, lambda i: (0, 0, 0)), pl.BlockSpec((1, c), lambda i: (0, 0))],
        out_specs=row, out_shape=jax.ShapeDtypeStruct((n, c), BF16),
        compiler_params=_params("parallel"), name="pool")(ext2d, pw, scale.reshape(1, c))


def _mlstm_chunk_kernel(q_ref, k_ref, v_ref, git_ref, gft_ref, bit_ref, bft_ref, hn_ref, c_ref, n_ref, m_ref,
                        at_s, ws_s, dc_s, mc_s, wi_s, fl_s):
    nb, L, _ = q_ref.shape
    rows = nb * SUBLANES

    @pl.when(pl.program_id(0) == 0)
    def _():
        c_ref[...] = jnp.zeros_like(c_ref)
        n_ref[...] = jnp.zeros_like(n_ref)
        m_ref[...] = jnp.zeros_like(m_ref)

    ig = git_ref[...].reshape(rows, L) + bit_ref[...]
    lf = _log_sigmoid(gft_ref[...].reshape(rows, L) + bft_ref[...])
    tri = (lax.broadcasted_iota(jnp.int32, (L, L), 0) <= lax.broadcasted_iota(jnp.int32, (L, L), 1)).astype(BF16)
    lf_hi = lf.astype(BF16)
    lf_mid = (lf - lf_hi.astype(F32)).astype(BF16)
    lf_lo = (lf - lf_hi.astype(F32) - lf_mid.astype(F32)).astype(BF16)
    bc = _dot(lf_hi, tri) + (_dot(lf_mid, tri) + _dot(lf_lo, tri))
    at = ig - bc
    m_prev = m_ref[...].reshape(rows, LANES)[:, 0:1]
    mc = jnp.maximum(_scan(at, 1, jnp.maximum, -jnp.inf), m_prev)
    mt = bc + mc
    m_last = mc[:, L - 1:L]
    at_s[...] = at.reshape(nb, SUBLANES, L)
    ws_s[...] = jnp.exp(at - m_last).reshape(nb, SUBLANES, L)
    dc_s[...] = jnp.broadcast_to(jnp.exp(m_prev - m_last), (rows, LANES)).reshape(nb, SUBLANES, LANES)
    m_ref[...] = jnp.broadcast_to(mt[:, L - 1:L], (rows, LANES)).reshape(nb, SUBLANES, LANES)
    wi = jnp.exp(m_prev - mc)
    fl = jnp.exp(-mt)
    for b in range(nb):
        rs = slice(b * SUBLANES, (b + 1) * SUBLANES)
        mc_s[b] = mc[rs].T
        wi_s[b] = wi[rs].T
        fl_s[b] = fl[rs].T

    causal = (lax.broadcasted_iota(jnp.int32, (L, L), 0) >= lax.broadcasted_iota(jnp.int32, (L, L), 1))

    def per_batch(heads, b, carry):
        at = at_s[b]
        ws = ws_s[b]
        decay = dc_s[b]
        mc_c, wi_c, fl_c = mc_s[b], wi_s[b], fl_s[b]
        ws16 = jnp.concatenate([ws, ws], axis=0).astype(BF16)
        for h in heads:
            sl = slice(h * MLSTM_HEAD_DIM, (h + 1) * MLSTM_HEAD_DIM)
            col = slice(h, h + 1)
            qh = q_ref[b, :, sl]
            kh = k_ref[b, :, sl]
            vh = v_ref[b, :, sl]
            s = _dot_nt(qh, kh)
            p = jnp.where(causal, s * jnp.exp(at[col, :] - mc_c[:, col]), 0.0)
            c_old = c_ref[b, h]
            n_old = n_ref[b, col, :]
            wi = wi_c[:, col]
            dv = MLSTM_HEAD_DIM
            v_ext = jnp.concatenate([vh, jnp.ones_like(vh)], axis=1)
            c_ext = jnp.concatenate([c_old, jnp.broadcast_to(n_old, c_old.shape)], axis=0)
            intra = _dot(p.astype(BF16), v_ext)
            inter = _dot_nt(qh, c_ext.astype(BF16))
            num = intra[:, :dv] + wi * inter[:, :dv]
            qn = intra[:, dv:] + wi * inter[:, dv:]
            hh = num / jnp.maximum(jnp.abs(qn), fl_c[:, col])
            hn_ref[b, :, sl] = _unit_norm(hh)
            dc = decay[col, 0:1]
            vts = (vh.astype(F32).T * ws[col, :]).astype(BF16)
            c_ref[b, h] = dc * c_old + _dot(vts, kh)
            n_ref[b, col, :] = dc * n_old + _dot(ws16, kh)[col, :]
        return carry

    for h0 in range(0, MLSTM_HEADS, 2):
        lax.fori_loop(0, nb, functools.partial(per_batch, (h0, h0 + 1)), 0)


def _mlstm_chunk(q, k, v, git, gft, bit, bft):
    nb, t, w = q.shape
    L = MLSTM_CHUNK
    seq = lambda c: pl.BlockSpec((nb, L, c), lambda i: (0, i, 0))
    seqt = pl.BlockSpec((nb, SUBLANES, L), lambda i: (0, 0, i))
    vec = lambda a: pl.BlockSpec(a.shape, lambda i: (0, 0))
    out_shape = (
        jax.ShapeDtypeStruct((nb, t, w), F32),
        jax.ShapeDtypeStruct((nb, MLSTM_HEADS, MLSTM_HEAD_DIM, MLSTM_HEAD_DIM), F32),
        jax.ShapeDtypeStruct((nb, MLSTM_HEADS, MLSTM_HEAD_DIM), F32),
        jax.ShapeDtypeStruct((nb, SUBLANES, LANES), F32),
    )
    out_specs = (
        seq(w),
        pl.BlockSpec(out_shape[1].shape, lambda i: (0, 0, 0, 0)),
        pl.BlockSpec(out_shape[2].shape, lambda i: (0, 0, 0)),
        pl.BlockSpec(out_shape[3].shape, lambda i: (0, 0, 0)),
    )
    bit, bft = jnp.tile(bit, (nb, 1)), jnp.tile(bft, (nb, 1))
    scratch = [pltpu.VMEM((nb, SUBLANES, L), F32), pltpu.VMEM((nb, SUBLANES, L), F32),
               pltpu.VMEM((nb, SUBLANES, LANES), F32)] + [pltpu.VMEM((nb, L, SUBLANES), F32) for _ in range(3)]
    return pl.pallas_call(
        _mlstm_chunk_kernel, grid=(t // L,),
        in_specs=[seq(w), seq(w), seq(w), seqt, seqt, vec(bit), vec(bft)],
        out_specs=out_specs, out_shape=out_shape, scratch_shapes=scratch, compiler_params=_params("arbitrary"),
        name="mlstm_chunk")(q, k, v, git, gft, bit, bft)


def _mlstm_step_kernel(q_ref, k_ref, v_ref, gi_ref, gf_ref, bi_ref, bf_ref, c_ref, n_ref, m_ref, c_new_ref,
                       hn_ref, co_ref, no_ref, mo_ref):
    del c_new_ref
    tb = q_ref.shape[0]
    d = MLSTM_HEAD_DIM
    ig = gi_ref[...] + bi_ref[...]
    lf = _log_sigmoid(gf_ref[...] + bf_ref[...])
    m_old = m_ref[...]
    mt = jnp.maximum(lf + m_old, ig)
    mo_ref[...] = mt
    wa = jnp.exp(ig - mt)
    wi = jnp.exp(lf + m_old - mt)
    fl = jnp.exp(-mt)
    wa_t, wi_t, fl_t = wa.T, wi.T, fl.T
    lane = lax.broadcasted_iota(jnp.int32, (d, tb), 1)
    for h in range(MLSTM_HEADS):
        sl = slice(h * d, (h + 1) * d)
        qh, kh, vh = q_ref[:, sl], k_ref[:, sl], v_ref[:, sl]
        nh = n_ref[:, h, :]
        qt, kt, vt, nt = qh.T, kh.T, vh.T, nh.T
        wa_r, wi_r, fl_r = wa_t[h:h + 1, :], wi_t[h:h + 1, :], fl_t[h:h + 1, :]
        s = jnp.sum(qt * kt, 0, keepdims=True) * wa_r
        cq = jnp.zeros((d, tb), F32)
        qtb = qt.astype(BF16)
        for b in range(tb):
            cq = jnp.where(lane == b, _dot(c_ref[b, h].astype(BF16), qtb), cq)
        num = s * vt + wi_r * cq
        qn = s + wi_r * jnp.sum(nt * qt, 0, keepdims=True)
        hh = num / jnp.maximum(jnp.abs(qn), fl_r)
        mu = jnp.mean(hh, 0, keepdims=True)
        xc = hh - mu
        var = jnp.mean(xc * xc, 0, keepdims=True)
        hn_ref[:, sl] = (xc * lax.rsqrt(var + LN_EPS)).T
        wav = wa_r * vt
        for b in range(tb):
            co_ref[b, h] = wi_r[:, b:b + 1] * c_ref[b, h] + wav[:, b:b + 1] * kh[b:b + 1, :]
        no_ref[:, h, :] = wi[:, h:h + 1] * nh + wa[:, h:h + 1] * kh


def _mlstm_step(q, k, v, gi, gf, bi, bf, c_all, n_all, m0, c_new, layer, tb):
    n, w = q.shape
    d = MLSTM_HEAD_DIM
    row = lambda c: pl.BlockSpec((tb, c), lambda i: (i, 0))
    vec = lambda a: pl.BlockSpec(a.shape, lambda i: (0, 0))
    cin = pl.BlockSpec((None, tb, MLSTM_HEADS, d, d), lambda i: (layer, i, 0, 0, 0))
    nin = pl.BlockSpec((None, tb, MLSTM_HEADS, d), lambda i: (layer, i, 0, 0))
    nout = pl.BlockSpec((tb, MLSTM_HEADS, d), lambda i: (i, 0, 0))
    out_shape = (
        jax.ShapeDtypeStruct((n, w), F32),
        jax.ShapeDtypeStruct(c_new.shape, F32),
        jax.ShapeDtypeStruct(n_all.shape[1:], F32),
        jax.ShapeDtypeStruct((n, LANES), F32),
    )
    return pl.pallas_call(
        _mlstm_step_kernel, grid=(n // tb,),
        in_specs=[row(w), row(w), row(w), row(LANES), row(LANES), vec(bi), vec(bf), cin, nin, row(LANES),
                  pl.BlockSpec(memory_space=pl.ANY)],
        out_specs=(row(w), cin, nout, row(LANES)), out_shape=out_shape, input_output_aliases={10: 1},
        compiler_params=_params("parallel"), name="mlstm_step")(q, k, v, gi, gf, bi, bf, c_all, n_all, m0, c_new)


def _mixout_kernel(po_ref, hn_ref, og_ref, x_ref, ng_ref, w_ref, g_ref, b_ref, o_ref):
    mo = jax.nn.sigmoid(og_ref[...]) * hn_ref[...] * ng_ref[...]
    mix = _dot(po_ref[...], w_ref[0:POOL_WIDTH, :]) + _dot(mo.astype(BF16), w_ref[POOL_WIDTH:, :])
    o_ref[...] = _layer_norm(ALPHA * x_ref[...] + mix, g_ref[...], b_ref[...])


def _mixout(po, hn, og, x, ng, w, g, b, tm):
    n, d = x.shape
    row = lambda c: pl.BlockSpec((tm, c), lambda i: (i, 0))
    vec = lambda a: pl.BlockSpec(a.shape, lambda i: (0, 0))
    return pl.pallas_call(
        _mixout_kernel, grid=(n // tm,),
        in_specs=[row(POOL_WIDTH), row(MLSTM_WIDTH), row(MLSTM_WIDTH), row(d), vec(ng), vec(w), vec(g), vec(b)],
        out_specs=row(d), out_shape=jax.ShapeDtypeStruct((n, d), F32),
        compiler_params=_params("parallel"), name="mixout")(po, hn, og, x, ng, w, g, b)


def _mm_kernel(x_ref, w_ref, o_ref, *, scale):
    y = _dot(x_ref[...].astype(BF16), w_ref[...])
    if scale != 1.0:
        y = y * scale
    o_ref[...] = y.astype(o_ref.dtype)


def _mm(x, w, tm, out_dtype, scale=1.0, name="proj"):
    n, d = x.shape
    dout = w.shape[1]
    return pl.pallas_call(
        functools.partial(_mm_kernel, scale=scale), grid=(n // tm,),
        in_specs=[pl.BlockSpec((tm, d), lambda i: (i, 0)), pl.BlockSpec(w.shape, lambda i: (0, 0))],
        out_specs=pl.BlockSpec((tm, dout), lambda i: (i, 0)),
        out_shape=jax.ShapeDtypeStruct((n, dout), out_dtype),
        compiler_params=_params("parallel"), name=name)(x, w)


def _cast_inproj_kernel(wt_ref, o_ref, og_ref, *, n_main):
    i = pl.program_id(0)

    @pl.when(i < n_main)
    def _():
        o_ref[...] = wt_ref[...].T.astype(BF16)

    @pl.when(i == n_main)
    def _():
        og_ref[...] = wt_ref[0:2 * MLSTM_HEADS, :]


def _cast_inproj(w_in, layer, tc=256):
    wt = jnp.swapaxes(w_in, 1, 2)
    d = wt.shape[2]
    n_main = GATE_OFF // tc
    return pl.pallas_call(
        functools.partial(_cast_inproj_kernel, n_main=n_main), grid=(n_main + 1,),
        in_specs=[pl.BlockSpec((None, tc, d), lambda i: (layer, i, 0))],
        out_specs=(pl.BlockSpec((d, tc), lambda i: (0, jnp.minimum(i, n_main - 1))),
                   pl.BlockSpec((2 * MLSTM_HEADS, d), lambda i: (0, 0))),
        out_shape=(jax.ShapeDtypeStruct((d, GATE_OFF), BF16), jax.ShapeDtypeStruct((2 * MLSTM_HEADS, d), F32)),
        compiler_params=_params("arbitrary"), name="inproj_cast")(wt)


def _mem_proj_kernel(x_ref, wk_ref, wv_ref, k4_ref, v4_ref, k2_ref, v2_ref, wkb, wvb):
    @pl.when(pl.program_id(1) == 0)
    def _():
        wkb[...] = wk_ref[...].astype(BF16)
        wvb[...] = wv_ref[...].astype(BF16)

    bb, m, d = x_ref.shape
    xb = x_ref[...].reshape(bb * m, d).astype(BF16)
    k = _dot(xb, wkb[...])
    v = _dot(xb, wvb[...])
    k2_ref[...] = k.reshape(bb, m, d)
    v2_ref[...] = v.reshape(bb, m, d)
    k4_ref[...] = k.reshape(bb, m, CA_HEADS, CA_HEAD_DIM)
    v4_ref[...] = v.reshape(bb, m, CA_HEADS, CA_HEAD_DIM)


def _mem_proj(mem, wk_all, wv_all, bb=2):
    nb, m, d = mem.shape
    depth = wk_all.shape[0]
    wspec = pl.BlockSpec((None, d, d), lambda l, b: (l, 0, 0))
    o4 = pl.BlockSpec((None, bb, m, CA_HEADS, CA_HEAD_DIM), lambda l, b: (l, b, 0, 0, 0))
    o2 = pl.BlockSpec((None, bb, m, d), lambda l, b: (l, b, 0, 0))
    s4 = jax.ShapeDtypeStruct((depth, nb, m, CA_HEADS, CA_HEAD_DIM), F32)
    s2 = jax.ShapeDtypeStruct((depth, nb, m, d), F32)
    return pl.pallas_call(
        _mem_proj_kernel, grid=(depth, nb // bb),
        in_specs=[pl.BlockSpec((bb, m, d), lambda l, b: (b, 0, 0)), wspec, wspec],
        out_specs=(o4, o4, o2, o2), out_shape=(s4, s4, s2, s2),
        scratch_shapes=[pltpu.VMEM((d, d), BF16), pltpu.VMEM((d, d), BF16)],
        compiler_params=_params("arbitrary", "arbitrary"), name="mem_proj")(mem, wk_all, wv_all)


def _attn_block_kernel(po_ref, hn_ref, og_ref, x_ref, k_ref, v_ref, ng_ref, wm_ref, g1_ref, b1_ref,
                       wq_ref, wo_ref, g_ref, b_ref, o_ref):
    mo = jax.nn.sigmoid(og_ref[...]) * hn_ref[...] * ng_ref[...]
    mix = _dot(po_ref[...], wm_ref[0:POOL_WIDTH, :]) + _dot(mo.astype(BF16), wm_ref[POOL_WIDTH:, :])
    x = _layer_norm(ALPHA * x_ref[...] + mix, g1_ref[...], b1_ref[...])
    qb = (_dot(x.astype(BF16), wq_ref[...]) * (CA_HEAD_DIM ** -0.5)).astype(BF16)
    kb = k_ref[...].astype(BF16)
    vb = v_ref[...].astype(BF16)
    ctx = []
    for h in range(CA_HEADS):
        sl = slice(h * CA_HEAD_DIM, (h + 1) * CA_HEAD_DIM)
        s = _dot_nt(qb[:, sl], kb[:, sl])
        e = jnp.exp(s - jnp.max(s, -1, keepdims=True))
        ctx.append((_dot(e.astype(BF16), vb[:, sl]) / jnp.sum(e, -1, keepdims=True)).astype(BF16))
    y = _dot(jnp.concatenate(ctx, axis=1), wo_ref[...])
    o_ref[...] = _layer_norm(ALPHA * x + y, g_ref[...], b_ref[...])


def _attn_block(po, hn, og, x, k, v, layer, ng, wm, ln1, wq, wo, ln2, seq, tq):
    n, d = x.shape
    per = seq // tq
    row = lambda c: pl.BlockSpec((tq, c), lambda i: (i, 0))
    kspec = pl.BlockSpec((None, None, N_MEM, d), lambda i: (layer, i // per, 0, 0))
    vec = lambda a: pl.BlockSpec(a.shape, lambda i: (0, 0))
    return pl.pallas_call(
        _attn_block_kernel, grid=(n // tq,),
        in_specs=[row(POOL_WIDTH), row(MLSTM_WIDTH), row(MLSTM_WIDTH), row(d), kspec, kspec, vec(ng), vec(wm),
                  vec(ln1[0]), vec(ln1[1]), vec(wq), vec(wo), vec(ln2[0]), vec(ln2[1])],
        out_specs=row(d), out_shape=jax.ShapeDtypeStruct((n, d), F32),
        compiler_params=_params("parallel"), name="attn_block")(po, hn, og, x, k, v, ng, wm, *ln1, wq, wo, *ln2)


def _attn_step_kernel(q_ref, k_ref, v_ref, o_ref):
    bb = q_ref.shape[0]
    rows = N_MEM * CA_HEADS
    lane = lax.broadcasted_iota(jnp.int32, (SUBLANES, rows), 1)
    row = lax.broadcasted_iota(jnp.int32, (SUBLANES, rows), 0)
    own = lax.rem(lane, CA_HEADS) == lax.rem(row, CA_HEADS)
    for j in range(bb):
        kf = k_ref[j].reshape(rows, CA_HEAD_DIM).astype(BF16)
        vf = v_ref[j].reshape(rows, CA_HEAD_DIM).astype(BF16)
        s = jnp.where(own, _dot_nt(q_ref[j].astype(BF16), kf), -jnp.inf)
        e = jnp.exp(s - jnp.max(s, -1, keepdims=True))
        o_ref[j] = _dot(e.astype(BF16), vf) / jnp.sum(e, -1, keepdims=True)


def _mm_res_ln_kernel(a_ref, x_ref, w_ref, g_ref, b_ref, o_ref):
    y = _dot(a_ref[...].astype(BF16), w_ref[...])
    o_ref[...] = _layer_norm(ALPHA * x_ref[...] + y, g_ref[...], b_ref[...])


def _mm_res_ln(a, x, w, g, b, tm):
    n, d = x.shape
    row = lambda c: pl.BlockSpec((tm, c), lambda i: (i, 0))
    vec = lambda arr: pl.BlockSpec(arr.shape, lambda i: (0, 0))
    return pl.pallas_call(
        _mm_res_ln_kernel, grid=(n // tm,),
        in_specs=[row(a.shape[1]), row(d), vec(w), vec(g), vec(b)], out_specs=row(d),
        out_shape=jax.ShapeDtypeStruct((n, d), F32),
        compiler_params=_params("parallel"), name="proj_res_ln")(a, x, w, g, b)


def _route(lt):
    gl = [lt[g:g + 1, :] for g in range(N_GROUPS)]
    gmax = functools.reduce(jnp.maximum, gl)
    gsum = functools.reduce(jnp.add, [jnp.exp(x - gmax) for x in gl])
    pg_sel = 1.0 / gsum

    def first_max(vals):
        m = functools.reduce(jnp.maximum, vals)
        taken = jnp.zeros_like(m, dtype=jnp.bool_)
        hot = []
        for x in vals:
            h = jnp.logical_and(x == m, jnp.logical_not(taken))
            taken = jnp.logical_or(taken, h)
            hot.append(h)
        return m, hot

    _, g_hot = first_max(gl)
    el = []
    for j in range(EXPERTS_PER_GROUP):
        rows = [lt[SUBLANES + g * EXPERTS_PER_GROUP + j:SUBLANES + g * EXPERTS_PER_GROUP + j + 1, :]
                for g in range(N_GROUPS)]
        x = rows[N_GROUPS - 1]
        for g in range(N_GROUPS - 2, -1, -1):
            x = jnp.where(g_hot[g], rows[g], x)
        el.append(x)
    emax = functools.reduce(jnp.maximum, el)
    ee = [jnp.exp(x - emax) for x in el]
    esum = functools.reduce(jnp.add, ee)
    pe = [x / esum for x in ee]
    p1, hot1 = first_max(pe)
    p2, hot2 = first_max([jnp.where(h, -jnp.inf, x) for h, x in zip(hot1, pe)])
    psum = p1 + p2
    gate = [jnp.where(h1, pg_sel * p1 / psum, jnp.where(h2, pg_sel * p2 / psum, 0.0)) for h1, h2 in zip(hot1, hot2)]
    return g_hot, gate


def _moe_kernel(x_ref, wr_ref, br_ref, tri_ref, wg_ref, wu_ref, wd_ref, g_ref, b_ref, *rest):
    ride = None
    if len(rest) > 7:
        ride = functools.partial(_attn_step_kernel, rest[0], rest[1], rest[2], rest[4])
        rest = rest[3:4] + rest[5:]
    o_ref, xb_ref, ct_ref, comb_ref, perm_ref, acc_ref, full_ref = rest
    grp = pl.program_id(1)
    tm = x_ref.shape[0]
    ns, cap, sw = perm_ref.shape[1:]
    f = D_EXPERT

    @pl.when(grp == 0)
    def _():
        x = x_ref[...]
        xh = x.astype(BF16)
        xb_ref[...] = xh
        xl = (x - xh.astype(F32)).astype(BF16)
        wr = wr_ref[...]
        wh = wr.astype(BF16)
        wl = (wr - wh.astype(F32)).astype(BF16)
        lt = _dot_nt(wh, xh) + (_dot_nt(wl, xh) + _dot_nt(wh, xl)) + br_ref[...]
        g_hot, gate = _route(lt)
        row8 = lax.broadcasted_iota(jnp.int32, (SUBLANES, tm), 0)
        hot8 = jnp.zeros((SUBLANES, tm), F32)
        for g in range(N_GROUPS):
            hot8 = jnp.where(jnp.logical_and(row8 == g, g_hot[g]), 1.0, hot8)
        most = None
        for s in range(ns):
            hs = hot8[:, s * sw:(s + 1) * sw]
            cum = _dot(hs.astype(BF16), tri_ref[...])
            seg_max = jnp.max(cum[:, sw - 1:sw])
            most = seg_max if most is None else jnp.maximum(most, seg_max)
            base = lax.broadcasted_iota(jnp.int32, (SUBLANES, sw), 0).astype(F32) * cap
            code = jnp.sum(hs * (base + cum - 1.0), 0, keepdims=True).astype(jnp.int32)
            for g in range(N_GROUPS):
                r = lax.broadcasted_iota(jnp.int32, (cap, sw), 0) + g * cap
                perm_ref[g, s] = jnp.where(r == code, 1.0, 0.0).astype(BF16)
        full_ref[0] = (most > cap).astype(jnp.int32)
        ct_ref[...] = jnp.zeros_like(ct_ref)
        for j in range(EXPERTS_PER_GROUP):
            ct_ref[j:j + 1, :] = gate[j]
        ct_ref[EXPERTS_PER_GROUP:EXPERTS_PER_GROUP + 1, :] = jnp.sum(hot8 * row8.astype(F32), 0, keepdims=True)
        comb_ref[...] = ct_ref[...].T
        acc_ref[...] = jnp.zeros_like(acc_ref)

    def experts(rows, gates):
        hh = []
        for j in range(EXPERTS_PER_GROUP):
            hg = _dot(rows, wg_ref[:, j * f:(j + 1) * f])
            hu = _dot(rows, wu_ref[:, j * f:(j + 1) * f])
            hh.append((hg * jax.nn.sigmoid(hg) * hu * gates[:, j:j + 1]).astype(BF16))
        return _dot(jnp.concatenate(hh, axis=1), wd_ref[...])

    @pl.when(full_ref[0] == 0)
    def _():
        comb = comb_ref[...]
        ch = comb.astype(BF16)
        cl = (comb - ch.astype(F32)).astype(BF16)
        xb = xb_ref[...]
        perm = [perm_ref[grp, s] for s in range(ns)]
        seg = lambda a, s: a[s * sw:(s + 1) * sw]
        rows = jnp.concatenate([_dot(perm[s], seg(xb, s)) for s in range(ns)], axis=0).astype(BF16)
        gates = jnp.concatenate([_dot(perm[s], seg(ch, s)) + _dot(perm[s], seg(cl, s)) for s in range(ns)], axis=0)
        y = experts(rows, gates).astype(BF16)
        for s in range(ns):
            acc_ref[s * sw:(s + 1) * sw, :] += _dot_tn(perm[s], y[s * cap:(s + 1) * cap])
        if ride is not None:
            ride()

    @pl.when(full_ref[0] != 0)
    def _():
        comb = comb_ref[...]
        own = comb[:, EXPERTS_PER_GROUP:EXPERTS_PER_GROUP + 1] == grp.astype(F32)
        acc_ref[...] += experts(xb_ref[...], jnp.where(own, comb, 0.0))
        if ride is not None:
            ride()

    @pl.when(grp == N_GROUPS - 1)
    def _():
        o_ref[...] = _layer_norm(ALPHA * x_ref[...] + acc_ref[...], g_ref[...], b_ref[...])


def _moe(x, wr, br, wg, wu, wd, g, b, tm, attn=None):
    n, d = x.shape
    ns = max(1, tm // MOE_SEGMENT)
    sw = tm // ns
    cap = min(sw, MOE_CAP)
    tri = (lax.broadcasted_iota(jnp.int32, (sw, sw), 0) <= lax.broadcasted_iota(jnp.int32, (sw, sw), 1)).astype(BF16)
    row = pl.BlockSpec((tm, d), lambda i, e: (i, 0))
    vec = lambda a: pl.BlockSpec(a.shape, lambda i, e: (0, 0))
    grp = lambda a: pl.BlockSpec((None,) + a.shape[1:], lambda i, e: (e, 0, 0))
    scratch = [pltpu.VMEM((tm, d), BF16), pltpu.VMEM((LANES, tm), F32), pltpu.VMEM((tm, LANES), F32),
               pltpu.VMEM((N_GROUPS, ns, cap, sw), BF16), pltpu.VMEM((tm, d), F32), pltpu.SMEM((1,), jnp.int32)]
    in_specs = [row, vec(wr), vec(br), vec(tri), grp(wg), grp(wu), grp(wd), vec(g), vec(b)]
    args = [x, wr, br, tri, wg, wu, wd, g, b]
    out_specs, out_shape = row, jax.ShapeDtypeStruct((n, d), F32)
    if attn is not None:
        q, k_all, v_all, layer = attn
        steps = (n // tm) * N_GROUPS
        bb = q.shape[0] // steps
        assert bb * steps == q.shape[0]
        qspec = pl.BlockSpec((bb, SUBLANES, CA_HEAD_DIM), lambda i, e: (i * N_GROUPS + e, 0, 0))
        kspec = pl.BlockSpec((None, bb, N_MEM, CA_HEADS, CA_HEAD_DIM), lambda i, e: (layer, i * N_GROUPS + e, 0, 0, 0))
        in_specs += [qspec, kspec, kspec]
        args += [q, k_all, v_all]
        out_specs, out_shape = (row, qspec), (out_shape, jax.ShapeDtypeStruct(q.shape, F32))
    return pl.pallas_call(
        _moe_kernel, grid=(n // tm, N_GROUPS), in_specs=in_specs, out_specs=out_specs, out_shape=out_shape,
        scratch_shapes=scratch, compiler_params=_params("arbitrary", "arbitrary"), name="moe")(*args)


def _pad_lanes(v):
    return jnp.zeros((1, LANES), F32).at[0, :v.shape[0]].set(v)


def _pad_rows(v):
    return jnp.zeros((SUBLANES, 1), F32).at[:v.shape[0], 0].set(v)


def _layer_weights(l, w_in, b_i, b_f, pool_w, pool_scale, mlstm_norm_g, w_out, ln1_g, ln1_b, ca_wq, ca_wo,
                   ln2_g, ln2_b, w_gr, b_gr, w_er, b_er, w_gate, w_up, w_down, ln3_g, ln3_b):
    d = D_MODEL
    w_main, w_gate_rows = _cast_inproj(w_in, l)
    w_gate_cols = w_gate_rows.T
    wg = jnp.zeros((d, 2 * LANES), F32)
    wg = wg.at[:, 0:MLSTM_HEADS].set(w_gate_cols[:, :MLSTM_HEADS])
    wg = wg.at[:, LANES:LANES + MLSTM_HEADS].set(w_gate_cols[:, MLSTM_HEADS:])
    wgt = jnp.zeros((2 * SUBLANES, d), F32)
    wgt = wgt.at[0:MLSTM_HEADS].set(w_gate_cols[:, :MLSTM_HEADS].T)
    wgt = wgt.at[SUBLANES:SUBLANES + MLSTM_HEADS].set(w_gate_cols[:, MLSTM_HEADS:].T)
    wr = jnp.zeros((ROUTER_ROWS, d), F32)
    wr = wr.at[0:N_GROUPS].set(w_gr[l].T).at[SUBLANES:SUBLANES + N_EXPERTS].set(w_er[l].T)
    br = jnp.zeros((ROUTER_ROWS, 1), F32)
    br = br.at[0:N_GROUPS, 0].set(b_gr[l]).at[SUBLANES:SUBLANES + N_EXPERTS, 0].set(b_er[l])
    row = lambda v: v.reshape(1, -1)
    return dict(
        w_main=w_main, wg=wg.astype(BF16), wgt=wgt.astype(BF16),
        bi=_pad_lanes(b_i[l]), bf=_pad_lanes(b_f[l]), bit=_pad_rows(b_i[l]), bft=_pad_rows(b_f[l]),
        pool_w=pool_w[l].astype(BF16), pool_scale=pool_scale[l], norm_g=row(mlstm_norm_g[l]),
        ln1=(row(ln1_g[l]), row(ln1_b[l])), ln2=(row(ln2_g[l]), row(ln2_b[l])), ln3=(row(ln3_g[l]), row(ln3_b[l])),
        wr=wr, br=br, raw_experts=(w_gate, w_up, w_down), raw_dense=(w_out, ca_wq, ca_wo))


def _prompt_mixers(x, nb, seq, layer, mem, p, tm, tm_big):
    per = seq // tm
    outs = _inproj_seq(x, p["w_main"], p["wgt"], p["pool_w"], p["pool_scale"], p["raw_experts"], p["raw_dense"],
                       layer, seq, tm, zeros_like=p.get("zeros_like"))
    pooled, tails, q, k, v, og, git, gft = outs[:8]
    p["experts"] = outs[8:11]
    p["w_out"], p["wq"], p["wo"] = outs[11:14]
    p["zeros"] = outs[14] if len(outs) > 14 else None
    pool_buf = tails.reshape(nb, per, POOL_HALO, POOL_WIDTH)[:, per - 1, POOL_HALO - POOL_BUF:]
    r3 = lambda a: a.reshape(nb, seq, a.shape[-1])
    hn, c1, n1, m1 = _mlstm_chunk(r3(q), r3(k), r3(v), git, gft, p["bit"], p["bft"])
    x = _attn_block(pooled, hn.reshape(nb * seq, MLSTM_WIDTH), og, x, mem[0], mem[1], layer, p["norm_g"], p["w_out"],
                    p["ln1"], p["wq"], p["wo"], p["ln2"], seq, tm_big)
    return x, pool_buf, c1, n1, m1[:, :MLSTM_HEADS, 0]


def _sample_mixers(x, nb, layer, pool_prev, state, p):
    u, q, k, v, og, gi, gf = _inproj(x, p["w_main"], p["wg"], nb)
    ext = jnp.concatenate([pool_prev, u.reshape(nb, 1, POOL_WIDTH)], axis=1)
    period = ext.shape[1]
    pooled = _pool(ext.reshape(nb * period, POOL_WIDTH), p["pool_w"], p["pool_scale"], period)
    pooled = pooled.reshape(nb, period, POOL_WIDTH)[:, period - 1]
    c_all, n_all, m0, c_new = state
    m0p = jnp.zeros((nb, LANES), F32).at[:, :MLSTM_HEADS].set(m0)
    hn, c1, n1, m1 = _mlstm_step(q, k, v, gi, gf, p["bi"], p["bf"], c_all, n_all, m0p, c_new, layer, MLSTM_STEP_SEQS)
    x = _mixout(pooled, hn, og, x, p["norm_g"], p["w_out"], *p["ln1"], nb)
    qc = _mm(x, p["wq"], nb, F32, scale=CA_HEAD_DIM ** -0.5, name="ca_q")
    qh = jnp.zeros((nb, SUBLANES, CA_HEAD_DIM), F32).at[:, :CA_HEADS].set(qc.reshape(nb, CA_HEADS, CA_HEAD_DIM))
    return x, qh, ext[:, -POOL_BUF:], c1, n1, m1[:, :MLSTM_HEADS]


def kernel(x_prompt, x_sample, mem_prompt, cache_pool, state_mlstm_C, state_mlstm_n, state_mlstm_m,
           cache_mem_k, cache_mem_v, emb_ln_g, emb_ln_b, w_in, b_i, b_f, pool_w, pool_scale,
           mlstm_norm_g, w_out, ln1_g, ln1_b, ca_wq, ca_wk, ca_wv, ca_wo, ln2_g, ln2_b,
           w_gr, b_gr, w_er, b_er, w_gate, w_up, w_down, ln3_g, ln3_b):
    bp, tp, d = x_prompt.shape
    bs, ts, _ = x_sample.shape
    tm_p, tm_s = 512, bs * ts
    xp = _ln(x_prompt.reshape(bp * tp, d), emb_ln_g, emb_ln_b, tm_p)
    xs = _ln(x_sample.reshape(bs * ts, d), emb_ln_g, emb_ln_b, tm_s)
    mk4, mv4, mk2, mv2 = _mem_proj(mem_prompt, ca_wk, ca_wv)
    outs = [[] for _ in range(7)]
    sc = None
    for l in range(DEPTH):
        p = _layer_weights(l, w_in, b_i, b_f, pool_w, pool_scale, mlstm_norm_g, w_out, ln1_g, ln1_b, ca_wq,
                           ca_wo, ln2_g, ln2_b, w_gr, b_gr, w_er, b_er, w_gate, w_up, w_down, ln3_g, ln3_b)
        if l == 0:
            p["zeros_like"] = state_mlstm_C
        xp, pb, c1, n1, m1 = _prompt_mixers(xp, bp, tp, l, (mk2, mv2), p, tm_p, 2 * tm_p)
        if l == 0:
            sc = p["zeros"]
        xs, qh, sb, sc, ns, ms = _sample_mixers(xs, bs * ts, l, cache_pool[l],
                                                (state_mlstm_C, state_mlstm_n, state_mlstm_m[l], sc), p)
        xp, ctx = _moe(xp, p["wr"], p["br"], *p["experts"], *p["ln3"], 2 * tm_p, attn=(qh, cache_mem_k, cache_mem_v, l))
        xs = _mm_res_ln(ctx[:, :CA_HEADS].reshape(bs * ts, d), xs, p["wo"], *p["ln2"], tm_s)
        xs = _moe(xs, p["wr"], p["br"], *p["experts"], *p["ln3"], tm_s)
        for lst, val in zip(outs, (pb, c1, n1, m1, sb, ns, ms)):
            lst.append(val)
    pp, pc, pn, pm, sp, sn, sm = (jnp.stack(o) for o in outs)
    return (xp.reshape(bp, tp, d), xs.reshape(bs, ts, d), pp, pc, pn, pm, mk4, mv4, sp, sc, sn, sm)
```

```python
import functools

import jax
import jax.numpy as jnp
from jax import lax
from jax.experimental import pallas as pl
from jax.experimental.pallas import tpu as pltpu

F32 = jnp.float32
BF16 = jnp.bfloat16

D_MODEL = 1024
DEPTH = 4
POOL_WIDTH = 512
POOL_GROUPS = 4
POOL_GROUP_DIM = 128
POOL_WINDOWS = (2, 4, 8, 16)
POOL_BUF = 15
MLSTM_WIDTH = 512
MLSTM_HEADS = 4
MLSTM_HEAD_DIM = 128
N_MEM = 256
CA_HEADS = 4
CA_HEAD_DIM = 256
N_GROUPS = 4
EXPERTS_PER_GROUP = 4
N_EXPERTS = 16
D_EXPERT = 256
ALPHA = (2 * DEPTH) ** 0.25
LN_EPS = 1e-5
GATE_OFF = POOL_WIDTH + 4 * MLSTM_WIDTH

LANES = 128
SUBLANES = 8
VMEM_LIMIT = 56 * 1024 * 1024
MLSTM_CHUNK = 256
POOL_ROWS = 2048
MLSTM_STEP_SEQS = 16
POOL_HALO = 16
ROUTER_ROWS = 32
MOE_SEGMENT = 512
MOE_CAP = 160


def _params(*sem):
    return pltpu.CompilerParams(dimension_semantics=sem, vmem_limit_bytes=VMEM_LIMIT)


def _dot(a, b):
    return jnp.dot(a, b, preferred_element_type=F32)


def _dot_nt(a, b, precision=None):
    return lax.dot_general(a, b, (((1,), (1,)), ((), ())), precision=precision,
                           preferred_element_type=F32)


def _dot_tn(a, b):
    return lax.dot_general(a, b, (((0,), (0,)), ((), ())), preferred_element_type=F32)


def _layer_norm(x, g, b):
    mu = jnp.mean(x, -1, keepdims=True)
    xc = x - mu
    var = jnp.mean(xc * xc, -1, keepdims=True)
    return xc * lax.rsqrt(var + LN_EPS) * g + b


def _unit_norm(x):
    mu = jnp.mean(x, -1, keepdims=True)
    xc = x - mu
    var = jnp.mean(xc * xc, -1, keepdims=True)
    return xc * lax.rsqrt(var + LN_EPS)


def _log_sigmoid(x):
    return jnp.minimum(x, 0.0) - jnp.log1p(jnp.exp(-jnp.abs(x)))


def _scan(x, axis, op, fill):
    n = x.shape[axis]
    idx = lax.broadcasted_iota(jnp.int32, x.shape, axis)
    s = 1
    while s < n:
        x = op(x, jnp.where(idx >= s, pltpu.roll(x, s, axis), fill))
        s *= 2
    return x


def _ln_kernel(x_ref, g_ref, b_ref, o_ref):
    o_ref[...] = _layer_norm(x_ref[...], g_ref[...], b_ref[...])


def _ln(x, g, b, tm):
    n, d = x.shape
    row = pl.BlockSpec((tm, d), lambda i: (i, 0))
    vec = pl.BlockSpec((1, d), lambda i: (0, 0))
    return pl.pallas_call(
        _ln_kernel, grid=(n // tm,), in_specs=[row, vec, vec], out_specs=row,
        out_shape=jax.ShapeDtypeStruct((n, d), F32), compiler_params=_params("parallel"),
        name="emb_ln")(x, g.reshape(1, d), b.reshape(1, d))


def _qkvo(xb, w_ref, q_ref, k_ref, v_ref, og_ref):
    w = MLSTM_WIDTH
    q = _dot(xb, w_ref[:, POOL_WIDTH:POOL_WIDTH + w]) * (MLSTM_HEAD_DIM ** -0.5)
    q_ref[...] = q.astype(q_ref.dtype)
    k_ref[...] = _dot(xb, w_ref[:, POOL_WIDTH + w:POOL_WIDTH + 2 * w]).astype(k_ref.dtype)
    v_ref[...] = _dot(xb, w_ref[:, POOL_WIDTH + 2 * w:POOL_WIDTH + 3 * w]).astype(v_ref.dtype)
    og_ref[...] = _dot(xb, w_ref[:, POOL_WIDTH + 3 * w:POOL_WIDTH + 4 * w])


def _inproj_kernel(x_ref, w_ref, wg_ref, u_ref, q_ref, k_ref, v_ref, og_ref, gi_ref, gf_ref):
    xb = x_ref[...].astype(BF16)
    u_ref[...] = _dot(xb, w_ref[:, 0:POOL_WIDTH])
    _qkvo(xb, w_ref, q_ref, k_ref, v_ref, og_ref)
    g = _dot(xb, wg_ref[...])
    gi_ref[...] = g[:, 0:LANES]
    gf_ref[...] = g[:, LANES:2 * LANES]


def _inproj(x, w_main, wg, tm):
    n, d = x.shape
    row = lambda c: pl.BlockSpec((tm, c), lambda i: (i, 0))
    full = lambda a: pl.BlockSpec(a.shape, lambda i: (0, 0))
    widths = (POOL_WIDTH, MLSTM_WIDTH, MLSTM_WIDTH, MLSTM_WIDTH, MLSTM_WIDTH, LANES, LANES)
    return pl.pallas_call(
        _inproj_kernel, grid=(n // tm,), in_specs=[row(d), full(w_main), full(wg)],
        out_specs=tuple(row(c) for c in widths),
        out_shape=tuple(jax.ShapeDtypeStruct((n, c), F32) for c in widths),
        compiler_params=_params("parallel"), name="inproj")(x, w_main, wg)


def _inproj_seq_kernel(*refs, per, first):
    n_in = 13 if first else 11
    x_ref, w_ref, wgt_ref, pw_ref, sc_ref, wg_ref, wu_ref, wd_ref, wo_ref, wq_ref, wa_ref = refs[:11]
    outs, carry_ref = refs[n_in:-1], refs[-1]
    (po_ref, tail_ref, q_ref, k_ref, v_ref, og_ref, git_ref, gft_ref,
     cg_ref, cu_ref, cd_ref, co_ref, cq_ref, ca_ref) = outs[:14]
    tm = x_ref.shape[0]
    start = lax.rem(pl.program_id(0), per) * tm

    @pl.when(start == 0)
    def _():
        carry_ref[...] = jnp.zeros_like(carry_ref)

    for src, dst in ((wg_ref, cg_ref), (wu_ref, cu_ref), (wd_ref, cd_ref), (wo_ref, co_ref), (wq_ref, cq_ref),
                     (wa_ref, ca_ref)):
        dst[...] = src[...].astype(BF16)
    x = x_ref[...]
    if first:
        x = _layer_norm(x, refs[11][...], refs[12][...])
        outs[14][...] = x
        outs[15][...] = jnp.zeros_like(outs[15])

    xb = x.astype(BF16)
    u = _dot(xb, w_ref[:, 0:POOL_WIDTH])
    halo = carry_ref.shape[0]
    ext = jnp.concatenate([carry_ref[...], u], axis=0)
    tail = ext[tm:, :]
    carry_ref[...] = tail
    tail_ref[...] = tail
    pos = start + lax.broadcasted_iota(jnp.int32, (tm, POOL_GROUP_DIM), 0)
    for g, win in enumerate(POOL_WINDOWS):
        sl = slice(g * POOL_GROUP_DIM, (g + 1) * POOL_GROUP_DIM)
        acc = ext[:, sl]
        s = 1
        while s < win:
            acc = acc + pltpu.roll(acc, s, 0)
            s *= 2
        cnt = jnp.minimum(pos + 1, win).astype(F32)
        dlt = acc[halo:, :] / cnt - u[:, sl]
        po_ref[:, sl] = (_dot(dlt.astype(BF16), pw_ref[g]) * sc_ref[:, sl]).astype(po_ref.dtype)
    _qkvo(xb, w_ref, q_ref, k_ref, v_ref, og_ref)
    gt = _dot_nt(wgt_ref[...], xb)
    git_ref[...] = gt[0:SUBLANES]
    gft_ref[...] = gt[SUBLANES:2 * SUBLANES]


def _inproj_seq(x, w_main, wgt, pw, scale, experts, dense, layer, seq, tm, first=None):
    n, d = x.shape
    per = seq // tm
    steps = n // tm
    split = steps // N_EXPERTS
    assert split * N_EXPERTS == steps and d % steps == 0 and d % split == 0 and D_EXPERT % split == 0
    row = lambda c: pl.BlockSpec((tm, c), lambda i: (i, 0))
    full = lambda a: pl.BlockSpec(a.shape, lambda i: (0,) * a.ndim)
    gt_spec = pl.BlockSpec((None, SUBLANES, tm), lambda i: (i // per, 0, i % per))
    gt_shape = jax.ShapeDtypeStruct((n // seq, SUBLANES, seq), F32)
    mw, e, gf = MLSTM_WIDTH, EXPERTS_PER_GROUP, EXPERTS_PER_GROUP * D_EXPERT
    dr, fr, wr = d // split, D_EXPERT // split, d // steps
    col_in = pl.BlockSpec((None, None, dr, D_EXPERT), lambda i: (layer, i // split, i % split, 0))
    row_in = pl.BlockSpec((None, None, fr, d), lambda i: (layer, i // split, i % split, 0))
    col_out = pl.BlockSpec((None, dr, D_EXPERT), lambda i: (i // split // e, i % split, (i // split) % e))
    row_out = pl.BlockSpec((None, fr, d), lambda i: (i // split // e, ((i // split) % e) * split + i % split, 0))
    dense_in = pl.BlockSpec((None, wr, d), lambda i: (layer, i, 0))
    dense_out = pl.BlockSpec((wr, d), lambda i: (i, 0))
    out_shape = [
        jax.ShapeDtypeStruct((n, POOL_WIDTH), BF16),
        jax.ShapeDtypeStruct((steps, POOL_HALO, POOL_WIDTH), F32),
        jax.ShapeDtypeStruct((n, mw), BF16), jax.ShapeDtypeStruct((n, mw), BF16), jax.ShapeDtypeStruct((n, mw), BF16),
        jax.ShapeDtypeStruct((n, mw), F32), gt_shape, gt_shape,
        jax.ShapeDtypeStruct((N_GROUPS, d, gf), BF16), jax.ShapeDtypeStruct((N_GROUPS, d, gf), BF16),
        jax.ShapeDtypeStruct((N_GROUPS, gf, d), BF16),
        jax.ShapeDtypeStruct((d, d), BF16), jax.ShapeDtypeStruct((d, d), BF16), jax.ShapeDtypeStruct((d, d), BF16),
    ]
    out_specs = [row(POOL_WIDTH), pl.BlockSpec((None, POOL_HALO, POOL_WIDTH), lambda i: (i, 0, 0)),
                 row(mw), row(mw), row(mw), row(mw), gt_spec, gt_spec,
                 col_out, col_out, row_out, dense_out, dense_out, dense_out]
    extra_in, extra_args = [], []
    if first is not None:
        ln_g, ln_b, zeros_like = first
        lead = zeros_like.shape[0] * zeros_like.shape[1]
        assert lead % steps == 0 and zeros_like.shape[1] % (lead // steps) == 0
        zb = lead // steps
        zper = zeros_like.shape[1] // zb
        out_shape += [jax.ShapeDtypeStruct((n, d), F32), jax.ShapeDtypeStruct(zeros_like.shape, zeros_like.dtype)]
        out_specs += [row(d), pl.BlockSpec((None, zb) + zeros_like.shape[2:],
                                           lambda i: (i // zper, i % zper) + (0,) * (zeros_like.ndim - 2))]
        extra_args = [ln_g.reshape(1, d), ln_b.reshape(1, d)]
        extra_in = [full(a) for a in extra_args]
    scale = scale.reshape(1, POOL_WIDTH)
    return pl.pallas_call(
        functools.partial(_inproj_seq_kernel, per=per, first=first is not None), grid=(steps,),
        in_specs=[row(d), full(w_main), full(wgt), full(pw), full(scale), col_in, col_in, row_in,
                  dense_in, dense_in, dense_in] + extra_in,
        out_specs=tuple(out_specs), out_shape=tuple(out_shape),
        scratch_shapes=[pltpu.VMEM((POOL_HALO, POOL_WIDTH), F32)],
        compiler_params=_params("arbitrary"), name="inproj_seq")(x, w_main, wgt, pw, scale, *experts, *dense,
                                                                 *extra_args)


def _pool_kernel(ext_ref, pw_ref, sc_ref, o_ref, *, period):
    rows = ext_ref.shape[0]
    r = lax.broadcasted_iota(jnp.int32, (rows, POOL_GROUP_DIM), 0)
    if period != rows:
        r = lax.rem(r, period)
    for g, win in enumerate(POOL_WINDOWS):
        sl = slice(g * POOL_GROUP_DIM, (g + 1) * POOL_GROUP_DIM)
        x = ext_ref[:, sl]
        acc = x
        s = 1
        while s < win:
            acc = acc + jnp.where(r >= s, pltpu.roll(acc, s, 0), 0.0)
            s *= 2
        cnt = jnp.minimum(r + 1, win).astype(F32)
        d = acc / cnt - x
        y = _dot(d.astype(BF16), pw_ref[g]) * sc_ref[:, sl]
        o_ref[:, sl] = y.astype(o_ref.dtype)


def _pool(ext2d, pw, scale, period):
    n, c = ext2d.shape
    row = pl.BlockSpec((POOL_ROWS, c), lambda i: (i, 0))
    return pl.pallas_call(
        functools.partial(_pool_kernel, period=period), grid=(n // POOL_ROWS,),
        in_specs=[row, pl.BlockSpec(pw.shape, lambda i: (0, 0, 0)), pl.BlockSpec((1, c), lambda i: (0, 0))],
        out_specs=row, out_shape=jax.ShapeDtypeStruct((n, c), BF16),
        compiler_params=_params("parallel"), name="pool")(ext2d, pw, scale.reshape(1, c))


def _mlstm_chunk_kernel(q_ref, k_ref, v_ref, git_ref, gft_ref, bit_ref, bft_ref, hn_ref, c_ref, n_ref, m_ref,
                        at_s, ws_s, dc_s, mc_s, wi_s, fl_s):
    nb, L, _ = q_ref.shape
    rows = nb * SUBLANES

    @pl.when(pl.program_id(0) == 0)
    def _():
        c_ref[...] = jnp.zeros_like(c_ref)
        n_ref[...] = jnp.zeros_like(n_ref)
        m_ref[...] = jnp.zeros_like(m_ref)

    ig = git_ref[...].reshape(rows, L) + bit_ref[...]
    lf = _log_sigmoid(gft_ref[...].reshape(rows, L) + bft_ref[...])
    tri = (lax.broadcasted_iota(jnp.int32, (L, L), 0) <= lax.broadcasted_iota(jnp.int32, (L, L), 1)).astype(BF16)
    lf_hi = lf.astype(BF16)
    lf_mid = (lf - lf_hi.astype(F32)).astype(BF16)
    lf_lo = (lf - lf_hi.astype(F32) - lf_mid.astype(F32)).astype(BF16)
    bc = _dot(lf_hi, tri) + (_dot(lf_mid, tri) + _dot(lf_lo, tri))
    at = ig - bc
    m_prev = m_ref[...].reshape(rows, LANES)[:, 0:1]
    mc = jnp.maximum(_scan(at, 1, jnp.maximum, -jnp.inf), m_prev)
    mt = bc + mc
    m_last = mc[:, L - 1:L]
    at_s[...] = at.reshape(nb, SUBLANES, L)
    ws_s[...] = jnp.exp(at - m_last).reshape(nb, SUBLANES, L)
    dc_s[...] = jnp.broadcast_to(jnp.exp(m_prev - m_last), (rows, LANES)).reshape(nb, SUBLANES, LANES)
    m_ref[...] = jnp.broadcast_to(mt[:, L - 1:L], (rows, LANES)).reshape(nb, SUBLANES, LANES)
    wi = jnp.exp(m_prev - mc)
    fl = jnp.exp(-mt)
    for b in range(nb):
        rs = slice(b * SUBLANES, (b + 1) * SUBLANES)
        mc_s[b] = mc[rs].T
        wi_s[b] = wi[rs].T
        fl_s[b] = fl[rs].T

    causal = (lax.broadcasted_iota(jnp.int32, (L, L), 0) >= lax.broadcasted_iota(jnp.int32, (L, L), 1))

    def per_batch(heads, b, carry):
        at = at_s[b]
        ws = ws_s[b]
        decay = dc_s[b]
        mc_c, wi_c, fl_c = mc_s[b], wi_s[b], fl_s[b]
        ws16 = jnp.concatenate([ws, ws], axis=0).astype(BF16)
        for h in heads:
            sl = slice(h * MLSTM_HEAD_DIM, (h + 1) * MLSTM_HEAD_DIM)
            col = slice(h, h + 1)
            qh = q_ref[b, :, sl]
            kh = k_ref[b, :, sl]
            vh = v_ref[b, :, sl]
            s = _dot_nt(qh, kh)
            p = jnp.where(causal, s * jnp.exp(at[col, :] - mc_c[:, col]), 0.0)
            c_old = c_ref[b, h]
            n_old = n_ref[b, col, :]
            wi = wi_c[:, col]
            dv = MLSTM_HEAD_DIM
            v_ext = jnp.concatenate([vh, jnp.ones_like(vh)], axis=1)
            c_ext = jnp.concatenate([c_old, jnp.broadcast_to(n_old, c_old.shape)], axis=0)
            intra = _dot(p.astype(BF16), v_ext)
            inter = _dot_nt(qh, c_ext.astype(BF16))
            num = intra[:, :dv] + wi * inter[:, :dv]
            qn = intra[:, dv:] + wi * inter[:, dv:]
            hh = num / jnp.maximum(jnp.abs(qn), fl_c[:, col])
            hn_ref[b, :, sl] = _unit_norm(hh)
            dc = decay[col, 0:1]
            vts = (vh.astype(F32).T * ws[col, :]).astype(BF16)
            c_ref[b, h] = dc * c_old + _dot(vts, kh)
            n_ref[b, col, :] = dc * n_old + _dot(ws16, kh)[col, :]
        return carry

    for h0 in range(0, MLSTM_HEADS, 2):
        lax.fori_loop(0, nb, functools.partial(per_batch, (h0, h0 + 1)), 0)


def _mlstm_chunk(q, k, v, git, gft, bit, bft):
    nb, t, w = q.shape
    L = MLSTM_CHUNK
    seq = lambda c: pl.BlockSpec((nb, L, c), lambda i: (0, i, 0))
    seqt = pl.BlockSpec((nb, SUBLANES, L), lambda i: (0, 0, i))
    vec = lambda a: pl.BlockSpec(a.shape, lambda i: (0, 0))
    out_shape = (
        jax.ShapeDtypeStruct((nb, t, w), F32),
        jax.ShapeDtypeStruct((nb, MLSTM_HEADS, MLSTM_HEAD_DIM, MLSTM_HEAD_DIM), F32),
        jax.ShapeDtypeStruct((nb, MLSTM_HEADS, MLSTM_HEAD_DIM), F32),
        jax.ShapeDtypeStruct((nb, SUBLANES, LANES), F32),
    )
    out_specs = (
        seq(w),
        pl.BlockSpec(out_shape[1].shape, lambda i: (0, 0, 0, 0)),
        pl.BlockSpec(out_shape[2].shape, lambda i: (0, 0, 0)),
        pl.BlockSpec(out_shape[3].shape, lambda i: (0, 0, 0)),
    )
    bit, bft = jnp.tile(bit, (nb, 1)), jnp.tile(bft, (nb, 1))
    scratch = [pltpu.VMEM((nb, SUBLANES, L), F32), pltpu.VMEM((nb, SUBLANES, L), F32),
               pltpu.VMEM((nb, SUBLANES, LANES), F32)] + [pltpu.VMEM((nb, L, SUBLANES), F32) for _ in range(3)]
    return pl.pallas_call(
        _mlstm_chunk_kernel, grid=(t // L,),
        in_specs=[seq(w), seq(w), seq(w), seqt, seqt, vec(bit), vec(bft)],
        out_specs=out_specs, out_shape=out_shape, scratch_shapes=scratch, compiler_params=_params("arbitrary"),
        name="mlstm_chunk")(q, k, v, git, gft, bit, bft)


def _mlstm_step_kernel(q_ref, k_ref, v_ref, gi_ref, gf_ref, bi_ref, bf_ref, c_ref, n_ref, m_ref, c_new_ref,
                       hn_ref, co_ref, no_ref, mo_ref):
    del c_new_ref
    tb = q_ref.shape[0]
    d = MLSTM_HEAD_DIM
    ig = gi_ref[...] + bi_ref[...]
    lf = _log_sigmoid(gf_ref[...] + bf_ref[...])
    m_old = m_ref[...]
    mt = jnp.maximum(lf + m_old, ig)
    mo_ref[...] = mt
    wa = jnp.exp(ig - mt)
    wi = jnp.exp(lf + m_old - mt)
    fl = jnp.exp(-mt)
    wa_t, wi_t, fl_t = wa.T, wi.T, fl.T
    lane = lax.broadcasted_iota(jnp.int32, (d, tb), 1)
    for h in range(MLSTM_HEADS):
        sl = slice(h * d, (h + 1) * d)
        qh, kh, vh = q_ref[:, sl], k_ref[:, sl], v_ref[:, sl]
        nh = n_ref[:, h, :]
        qt, kt, vt, nt = qh.T, kh.T, vh.T, nh.T
        wa_r, wi_r, fl_r = wa_t[h:h + 1, :], wi_t[h:h + 1, :], fl_t[h:h + 1, :]
        s = jnp.sum(qt * kt, 0, keepdims=True) * wa_r
        cq = jnp.zeros((d, tb), F32)
        qtb = qt.astype(BF16)
        for b in range(tb):
            cq = jnp.where(lane == b, _dot(c_ref[b, h].astype(BF16), qtb), cq)
        num = s * vt + wi_r * cq
        qn = s + wi_r * jnp.sum(nt * qt, 0, keepdims=True)
        hh = num / jnp.maximum(jnp.abs(qn), fl_r)
        mu = jnp.mean(hh, 0, keepdims=True)
        xc = hh - mu
        var = jnp.mean(xc * xc, 0, keepdims=True)
        hn_ref[:, sl] = (xc * lax.rsqrt(var + LN_EPS)).T
        wav = wa_r * vt
        for b in range(tb):
            co_ref[b, h] = wi_r[:, b:b + 1] * c_ref[b, h] + wav[:, b:b + 1] * kh[b:b + 1, :]
        no_ref[:, h, :] = wi[:, h:h + 1] * nh + wa[:, h:h + 1] * kh


def _mlstm_step(q, k, v, gi, gf, bi, bf, c_all, n_all, m0, c_new, layer, tb):
    n, w = q.shape
    d = MLSTM_HEAD_DIM
    row = lambda c: pl.BlockSpec((tb, c), lambda i: (i, 0))
    vec = lambda a: pl.BlockSpec(a.shape, lambda i: (0, 0))
    cin = pl.BlockSpec((None, tb, MLSTM_HEADS, d, d), lambda i: (layer, i, 0, 0, 0))
    nin = pl.BlockSpec((None, tb, MLSTM_HEADS, d), lambda i: (layer, i, 0, 0))
    nout = pl.BlockSpec((tb, MLSTM_HEADS, d), lambda i: (i, 0, 0))
    out_shape = (
        jax.ShapeDtypeStruct((n, w), F32),
        jax.ShapeDtypeStruct(c_new.shape, F32),
        jax.ShapeDtypeStruct(n_all.shape[1:], F32),
        jax.ShapeDtypeStruct((n, LANES), F32),
    )
    return pl.pallas_call(
        _mlstm_step_kernel, grid=(n // tb,),
        in_specs=[row(w), row(w), row(w), row(LANES), row(LANES), vec(bi), vec(bf), cin, nin, row(LANES),
                  pl.BlockSpec(memory_space=pl.ANY)],
        out_specs=(row(w), cin, nout, row(LANES)), out_shape=out_shape, input_output_aliases={10: 1},
        compiler_params=_params("parallel"), name="mlstm_step")(q, k, v, gi, gf, bi, bf, c_all, n_all, m0, c_new)


def _mixout_kernel(po_ref, hn_ref, og_ref, x_ref, ng_ref, w_ref, g_ref, b_ref, o_ref):
    mo = jax.nn.sigmoid(og_ref[...]) * hn_ref[...] * ng_ref[...]
    mix = _dot(po_ref[...], w_ref[0:POOL_WIDTH, :]) + _dot(mo.astype(BF16), w_ref[POOL_WIDTH:, :])
    o_ref[...] = _layer_norm(ALPHA * x_ref[...] + mix, g_ref[...], b_ref[...])


def _mixout(po, hn, og, x, ng, w, g, b, tm):
    n, d = x.shape
    row = lambda c: pl.BlockSpec((tm, c), lambda i: (i, 0))
    vec = lambda a: pl.BlockSpec(a.shape, lambda i: (0, 0))
    return pl.pallas_call(
        _mixout_kernel, grid=(n // tm,),
        in_specs=[row(POOL_WIDTH), row(MLSTM_WIDTH), row(MLSTM_WIDTH), row(d), vec(ng), vec(w), vec(g), vec(b)],
        out_specs=row(d), out_shape=jax.ShapeDtypeStruct((n, d), F32),
        compiler_params=_params("parallel"), name="mixout")(po, hn, og, x, ng, w, g, b)


def _mm_kernel(x_ref, w_ref, o_ref, *, scale):
    y = _dot(x_ref[...].astype(BF16), w_ref[...])
    if scale != 1.0:
        y = y * scale
    o_ref[...] = y.astype(o_ref.dtype)


def _mm(x, w, tm, out_dtype, scale=1.0, name="proj"):
    n, d = x.shape
    dout = w.shape[1]
    return pl.pallas_call(
        functools.partial(_mm_kernel, scale=scale), grid=(n // tm,),
        in_specs=[pl.BlockSpec((tm, d), lambda i: (i, 0)), pl.BlockSpec(w.shape, lambda i: (0, 0))],
        out_specs=pl.BlockSpec((tm, dout), lambda i: (i, 0)),
        out_shape=jax.ShapeDtypeStruct((n, dout), out_dtype),
        compiler_params=_params("parallel"), name=name)(x, w)


def _cast_inproj_kernel(wt_ref, o_ref, og_ref, *, n_main):
    i = pl.program_id(0)

    @pl.when(i < n_main)
    def _():
        o_ref[...] = wt_ref[...].T.astype(BF16)

    @pl.when(i == n_main)
    def _():
        og_ref[...] = wt_ref[0:2 * MLSTM_HEADS, :]


def _cast_inproj(w_in, layer, tc=256):
    wt = jnp.swapaxes(w_in, 1, 2)
    d = wt.shape[2]
    n_main = GATE_OFF // tc
    return pl.pallas_call(
        functools.partial(_cast_inproj_kernel, n_main=n_main), grid=(n_main + 1,),
        in_specs=[pl.BlockSpec((None, tc, d), lambda i: (layer, i, 0))],
        out_specs=(pl.BlockSpec((d, tc), lambda i: (0, jnp.minimum(i, n_main - 1))),
                   pl.BlockSpec((2 * MLSTM_HEADS, d), lambda i: (0, 0))),
        out_shape=(jax.ShapeDtypeStruct((d, GATE_OFF), BF16), jax.ShapeDtypeStruct((2 * MLSTM_HEADS, d), F32)),
        compiler_params=_params("arbitrary"), name="inproj_cast")(wt)


def _mem_proj_kernel(x_ref, wk_ref, wv_ref, k4_ref, v4_ref, k2_ref, v2_ref, wkb, wvb):
    @pl.when(pl.program_id(1) == 0)
    def _():
        wkb[...] = wk_ref[...].astype(BF16)
        wvb[...] = wv_ref[...].astype(BF16)

    bb, m, d = x_ref.shape
    xb = x_ref[...].reshape(bb * m, d).astype(BF16)
    k = _dot(xb, wkb[...])
    v = _dot(xb, wvb[...])
    k2_ref[...] = k.reshape(bb, m, d)
    v2_ref[...] = v.reshape(bb, m, d)
    k4_ref[...] = k.reshape(bb, m, CA_HEADS, CA_HEAD_DIM)
    v4_ref[...] = v.reshape(bb, m, CA_HEADS, CA_HEAD_DIM)


def _mem_proj(mem, wk_all, wv_all, bb=2):
    nb, m, d = mem.shape
    depth = wk_all.shape[0]
    wspec = pl.BlockSpec((None, d, d), lambda l, b: (l, 0, 0))
    o4 = pl.BlockSpec((None, bb, m, CA_HEADS, CA_HEAD_DIM), lambda l, b: (l, b, 0, 0, 0))
    o2 = pl.BlockSpec((None, bb, m, d), lambda l, b: (l, b, 0, 0))
    s4 = jax.ShapeDtypeStruct((depth, nb, m, CA_HEADS, CA_HEAD_DIM), F32)
    s2 = jax.ShapeDtypeStruct((depth, nb, m, d), F32)
    return pl.pallas_call(
        _mem_proj_kernel, grid=(depth, nb // bb),
        in_specs=[pl.BlockSpec((bb, m, d), lambda l, b: (b, 0, 0)), wspec, wspec],
        out_specs=(o4, o4, o2, o2), out_shape=(s4, s4, s2, s2),
        scratch_shapes=[pltpu.VMEM((d, d), BF16), pltpu.VMEM((d, d), BF16)],
        compiler_params=_params("arbitrary", "arbitrary"), name="mem_proj")(mem, wk_all, wv_all)


def _attn_block_kernel(po_ref, hn_ref, og_ref, x_ref, k_ref, v_ref, ng_ref, wm_ref, g1_ref, b1_ref,
                       wq_ref, wo_ref, g_ref, b_ref, o_ref):
    mo = jax.nn.sigmoid(og_ref[...]) * hn_ref[...] * ng_ref[...]
    mix = _dot(po_ref[...], wm_ref[0:POOL_WIDTH, :]) + _dot(mo.astype(BF16), wm_ref[POOL_WIDTH:, :])
    x = _layer_norm(ALPHA * x_ref[...] + mix, g1_ref[...], b1_ref[...])
    qb = (_dot(x.astype(BF16), wq_ref[...]) * (CA_HEAD_DIM ** -0.5)).astype(BF16)
    kb = k_ref[...].astype(BF16)
    vb = v_ref[...].astype(BF16)
    ctx = []
    for h in range(CA_HEADS):
        sl = slice(h * CA_HEAD_DIM, (h + 1) * CA_HEAD_DIM)
        s = _dot_nt(qb[:, sl], kb[:, sl])
        e = jnp.exp(s - jnp.max(s, -1, keepdims=True))
        ctx.append((_dot(e.astype(BF16), vb[:, sl]) / jnp.sum(e, -1, keepdims=True)).astype(BF16))
    y = _dot(jnp.concatenate(ctx, axis=1), wo_ref[...])
    o_ref[...] = _layer_norm(ALPHA * x + y, g_ref[...], b_ref[...])


def _attn_block(po, hn, og, x, k, v, layer, ng, wm, ln1, wq, wo, ln2, seq, tq):
    n, d = x.shape
    per = seq // tq
    row = lambda c: pl.BlockSpec((tq, c), lambda i: (i, 0))
    kspec = pl.BlockSpec((None, None, N_MEM, d), lambda i: (layer, i // per, 0, 0))
    vec = lambda a: pl.BlockSpec(a.shape, lambda i: (0, 0))
    return pl.pallas_call(
        _attn_block_kernel, grid=(n // tq,),
        in_specs=[row(POOL_WIDTH), row(MLSTM_WIDTH), row(MLSTM_WIDTH), row(d), kspec, kspec, vec(ng), vec(wm),
                  vec(ln1[0]), vec(ln1[1]), vec(wq), vec(wo), vec(ln2[0]), vec(ln2[1])],
        out_specs=row(d), out_shape=jax.ShapeDtypeStruct((n, d), F32),
        compiler_params=_params("parallel"), name="attn_block")(po, hn, og, x, k, v, ng, wm, *ln1, wq, wo, *ln2)


def _attn_step_kernel(q_ref, k_ref, v_ref, o_ref):
    bb = q_ref.shape[0]
    rows = N_MEM * CA_HEADS
    lane = lax.broadcasted_iota(jnp.int32, (SUBLANES, rows), 1)
    row = lax.broadcasted_iota(jnp.int32, (SUBLANES, rows), 0)
    own = lax.rem(lane, CA_HEADS) == lax.rem(row, CA_HEADS)
    for j in range(bb):
        kf = k_ref[j].reshape(rows, CA_HEAD_DIM).astype(BF16)
        vf = v_ref[j].reshape(rows, CA_HEAD_DIM).astype(BF16)
        s = jnp.where(own, _dot_nt(q_ref[j].astype(BF16), kf), -jnp.inf)
        e = jnp.exp(s - jnp.max(s, -1, keepdims=True))
        o_ref[j] = _dot(e.astype(BF16), vf) / jnp.sum(e, -1, keepdims=True)


def _mm_res_ln_kernel(a_ref, x_ref, w_ref, g_ref, b_ref, o_ref):
    y = _dot(a_ref[...].astype(BF16), w_ref[...])
    o_ref[...] = _layer_norm(ALPHA * x_ref[...] + y, g_ref[...], b_ref[...])


def _mm_res_ln(a, x, w, g, b, tm):
    n, d = x.shape
    row = lambda c: pl.BlockSpec((tm, c), lambda i: (i, 0))
    vec = lambda arr: pl.BlockSpec(arr.shape, lambda i: (0, 0))
    return pl.pallas_call(
        _mm_res_ln_kernel, grid=(n // tm,),
        in_specs=[row(a.shape[1]), row(d), vec(w), vec(g), vec(b)], out_specs=row(d),
        out_shape=jax.ShapeDtypeStruct((n, d), F32),
        compiler_params=_params("parallel"), name="proj_res_ln")(a, x, w, g, b)


def _route(lt):
    gl = [lt[g:g + 1, :] for g in range(N_GROUPS)]
    gmax = functools.reduce(jnp.maximum, gl)
    gsum = functools.reduce(jnp.add, [jnp.exp(x - gmax) for x in gl])
    pg_sel = 1.0 / gsum

    def first_max(vals):
        m = functools.reduce(jnp.maximum, vals)
        taken = jnp.zeros_like(m, dtype=jnp.bool_)
        hot = []
        for x in vals:
            h = jnp.logical_and(x == m, jnp.logical_not(taken))
            taken = jnp.logical_or(taken, h)
            hot.append(h)
        return m, hot

    _, g_hot = first_max(gl)
    el = []
    for j in range(EXPERTS_PER_GROUP):
        rows = [lt[SUBLANES + g * EXPERTS_PER_GROUP + j:SUBLANES + g * EXPERTS_PER_GROUP + j + 1, :]
                for g in range(N_GROUPS)]
        x = rows[N_GROUPS - 1]
        for g in range(N_GROUPS - 2, -1, -1):
            x = jnp.where(g_hot[g], rows[g], x)
        el.append(x)
    emax = functools.reduce(jnp.maximum, el)
    ee = [jnp.exp(x - emax) for x in el]
    esum = functools.reduce(jnp.add, ee)
    pe = [x / esum for x in ee]
    p1, hot1 = first_max(pe)
    p2, hot2 = first_max([jnp.where(h, -jnp.inf, x) for h, x in zip(hot1, pe)])
    psum = p1 + p2
    gate = [jnp.where(h1, pg_sel * p1 / psum, jnp.where(h2, pg_sel * p2 / psum, 0.0)) for h1, h2 in zip(hot1, hot2)]
    return g_hot, gate


def _moe_kernel(x_ref, wr_ref, br_ref, tri_ref, wg_ref, wu_ref, wd_ref, g_ref, b_ref, *rest):
    ride = None
    if len(rest) > 7:
        ride = functools.partial(_attn_step_kernel, rest[0], rest[1], rest[2], rest[4])
        rest = rest[3:4] + rest[5:]
    o_ref, xb_ref, ct_ref, comb_ref, perm_ref, acc_ref, full_ref = rest
    grp = pl.program_id(1)
    tm = x_ref.shape[0]
    ns, cap, sw = perm_ref.shape[1:]
    f = D_EXPERT

    @pl.when(grp == 0)
    def _():
        x = x_ref[...]
        xh = x.astype(BF16)
        xb_ref[...] = xh
        xl = (x - xh.astype(F32)).astype(BF16)
        wr = wr_ref[...]
        wh = wr.astype(BF16)
        wl = (wr - wh.astype(F32)).astype(BF16)
        lt = _dot_nt(wh, xh) + (_dot_nt(wl, xh) + _dot_nt(wh, xl)) + br_ref[...]
        g_hot, gate = _route(lt)
        row8 = lax.broadcasted_iota(jnp.int32, (SUBLANES, tm), 0)
        hot8 = jnp.zeros((SUBLANES, tm), F32)
        for g in range(N_GROUPS):
            hot8 = jnp.where(jnp.logical_and(row8 == g, g_hot[g]), 1.0, hot8)
        most = None
        for s in range(ns):
            hs = hot8[:, s * sw:(s + 1) * sw]
            cum = _dot(hs.astype(BF16), tri_ref[...])
            seg_max = jnp.max(cum[:, sw - 1:sw])
            most = seg_max if most is None else jnp.maximum(most, seg_max)
            base = lax.broadcasted_iota(jnp.int32, (SUBLANES, sw), 0).astype(F32) * cap
            code = jnp.sum(hs * (base + cum - 1.0), 0, keepdims=True).astype(jnp.int32)
            for g in range(N_GROUPS):
                r = lax.broadcasted_iota(jnp.int32, (cap, sw), 0) + g * cap
                perm_ref[g, s] = jnp.where(r == code, 1.0, 0.0).astype(BF16)
        full_ref[0] = (most > cap).astype(jnp.int32)
        ct_ref[...] = jnp.zeros_like(ct_ref)
        for j in range(EXPERTS_PER_GROUP):
            ct_ref[j:j + 1, :] = gate[j]
        ct_ref[EXPERTS_PER_GROUP:EXPERTS_PER_GROUP + 1, :] = jnp.sum(hot8 * row8.astype(F32), 0, keepdims=True)
        comb_ref[...] = ct_ref[...].T
        acc_ref[...] = jnp.zeros_like(acc_ref)

    def experts(rows, gates):
        hh = []
        for j in range(EXPERTS_PER_GROUP):
            hg = _dot(rows, wg_ref[:, j * f:(j + 1) * f])
            hu = _dot(rows, wu_ref[:, j * f:(j + 1) * f])
            hh.append((hg * jax.nn.sigmoid(hg) * hu * gates[:, j:j + 1]).astype(BF16))
        return _dot(jnp.concatenate(hh, axis=1), wd_ref[...])

    @pl.when(full_ref[0] == 0)
    def _():
        comb = comb_ref[...]
        ch = comb.astype(BF16)
        cl = (comb - ch.astype(F32)).astype(BF16)
        xb = xb_ref[...]
        perm = [perm_ref[grp, s] for s in range(ns)]
        seg = lambda a, s: a[s * sw:(s + 1) * sw]
        rows = jnp.concatenate([_dot(perm[s], seg(xb, s)) for s in range(ns)], axis=0).astype(BF16)
        gates = jnp.concatenate([_dot(perm[s], seg(ch, s)) + _dot(perm[s], seg(cl, s)) for s in range(ns)], axis=0)
        y = experts(rows, gates).astype(BF16)
        for s in range(ns):
            acc_ref[s * sw:(s + 1) * sw, :] += _dot_tn(perm[s], y[s * cap:(s + 1) * cap])
        if ride is not None:
            ride()

    @pl.when(full_ref[0] != 0)
    def _():
        comb = comb_ref[...]
        own = comb[:, EXPERTS_PER_GROUP:EXPERTS_PER_GROUP + 1] == grp.astype(F32)
        acc_ref[...] += experts(xb_ref[...], jnp.where(own, comb, 0.0))
        if ride is not None:
            ride()

    @pl.when(grp == N_GROUPS - 1)
    def _():
        o_ref[...] = _layer_norm(ALPHA * x_ref[...] + acc_ref[...], g_ref[...], b_ref[...])


def _moe(x, wr, br, wg, wu, wd, g, b, tm, attn=None):
    n, d = x.shape
    ns = max(1, tm // MOE_SEGMENT)
    sw = tm // ns
    cap = min(sw, MOE_CAP)
    tri = (lax.broadcasted_iota(jnp.int32, (sw, sw), 0) <= lax.broadcasted_iota(jnp.int32, (sw, sw), 1)).astype(BF16)
    row = pl.BlockSpec((tm, d), lambda i, e: (i, 0))
    vec = lambda a: pl.BlockSpec(a.shape, lambda i, e: (0, 0))
    grp = lambda a: pl.BlockSpec((None,) + a.shape[1:], lambda i, e: (e, 0, 0))
    scratch = [pltpu.VMEM((tm, d), BF16), pltpu.VMEM((LANES, tm), F32), pltpu.VMEM((tm, LANES), F32),
               pltpu.VMEM((N_GROUPS, ns, cap, sw), BF16), pltpu.VMEM((tm, d), F32), pltpu.SMEM((1,), jnp.int32)]
    in_specs = [row, vec(wr), vec(br), vec(tri), grp(wg), grp(wu), grp(wd), vec(g), vec(b)]
    args = [x, wr, br, tri, wg, wu, wd, g, b]
    out_specs, out_shape = row, jax.ShapeDtypeStruct((n, d), F32)
    if attn is not None:
        q, k_all, v_all, layer = attn
        steps = (n // tm) * N_GROUPS
        bb = q.shape[0] // steps
        assert bb * steps == q.shape[0]
        qspec = pl.BlockSpec((bb, SUBLANES, CA_HEAD_DIM), lambda i, e: (i * N_GROUPS + e, 0, 0))
        kspec = pl.BlockSpec((None, bb, N_MEM, CA_HEADS, CA_HEAD_DIM), lambda i, e: (layer, i * N_GROUPS + e, 0, 0, 0))
        in_specs += [qspec, kspec, kspec]
        args += [q, k_all, v_all]
        out_specs, out_shape = (row, qspec), (out_shape, jax.ShapeDtypeStruct(q.shape, F32))
    return pl.pallas_call(
        _moe_kernel, grid=(n // tm, N_GROUPS), in_specs=in_specs, out_specs=out_specs, out_shape=out_shape,
        scratch_shapes=scratch, compiler_params=_params("arbitrary", "arbitrary"), name="moe")(*args)


def _pad_lanes(v):
    return jnp.zeros((1, LANES), F32).at[0, :v.shape[0]].set(v)


def _pad_rows(v):
    return jnp.zeros((SUBLANES, 1), F32).at[:v.shape[0], 0].set(v)


def _layer_weights(l, w_in, b_i, b_f, pool_w, pool_scale, mlstm_norm_g, w_out, ln1_g, ln1_b, ca_wq, ca_wo,
                   ln2_g, ln2_b, w_gr, b_gr, w_er, b_er, w_gate, w_up, w_down, ln3_g, ln3_b):
    d = D_MODEL
    w_main, w_gate_rows = _cast_inproj(w_in, l)
    w_gate_cols = w_gate_rows.T
    wg = jnp.zeros((d, 2 * LANES), F32)
    wg = wg.at[:, 0:MLSTM_HEADS].set(w_gate_cols[:, :MLSTM_HEADS])
    wg = wg.at[:, LANES:LANES + MLSTM_HEADS].set(w_gate_cols[:, MLSTM_HEADS:])
    wgt = jnp.zeros((2 * SUBLANES, d), F32)
    wgt = wgt.at[0:MLSTM_HEADS].set(w_gate_cols[:, :MLSTM_HEADS].T)
    wgt = wgt.at[SUBLANES:SUBLANES + MLSTM_HEADS].set(w_gate_cols[:, MLSTM_HEADS:].T)
    wr = jnp.zeros((ROUTER_ROWS, d), F32)
    wr = wr.at[0:N_GROUPS].set(w_gr[l].T).at[SUBLANES:SUBLANES + N_EXPERTS].set(w_er[l].T)
    br = jnp.zeros((ROUTER_ROWS, 1), F32)
    br = br.at[0:N_GROUPS, 0].set(b_gr[l]).at[SUBLANES:SUBLANES + N_EXPERTS, 0].set(b_er[l])
    row = lambda v: v.reshape(1, -1)
    return dict(
        w_main=w_main, wg=wg.astype(BF16), wgt=wgt.astype(BF16),
        bi=_pad_lanes(b_i[l]), bf=_pad_lanes(b_f[l]), bit=_pad_rows(b_i[l]), bft=_pad_rows(b_f[l]),
        pool_w=pool_w[l].astype(BF16), pool_scale=pool_scale[l], norm_g=row(mlstm_norm_g[l]),
        ln1=(row(ln1_g[l]), row(ln1_b[l])), ln2=(row(ln2_g[l]), row(ln2_b[l])), ln3=(row(ln3_g[l]), row(ln3_b[l])),
        wr=wr, br=br, raw_experts=(w_gate, w_up, w_down), raw_dense=(w_out, ca_wq, ca_wo))


def _prompt_mixers(x, nb, seq, layer, mem, p, tm, tm_big):
    per = seq // tm
    outs = _inproj_seq(x, p["w_main"], p["wgt"], p["pool_w"], p["pool_scale"], p["raw_experts"], p["raw_dense"],
                       layer, seq, tm, first=p.get("first"))
    pooled, tails, q, k, v, og, git, gft = outs[:8]
    p["experts"] = outs[8:11]
    p["w_out"], p["wq"], p["wo"] = outs[11:14]
    if len(outs) > 14:
        x, p["zeros"] = outs[14:16]
    pool_buf = tails.reshape(nb, per, POOL_HALO, POOL_WIDTH)[:, per - 1, POOL_HALO - POOL_BUF:]
    r3 = lambda a: a.reshape(nb, seq, a.shape[-1])
    hn, c1, n1, m1 = _mlstm_chunk(r3(q), r3(k), r3(v), git, gft, p["bit"], p["bft"])
    x = _attn_block(pooled, hn.reshape(nb * seq, MLSTM_WIDTH), og, x, mem[0], mem[1], layer, p["norm_g"], p["w_out"],
                    p["ln1"], p["wq"], p["wo"], p["ln2"], seq, tm_big)
    return x, pool_buf, c1, n1, m1[:, :MLSTM_HEADS, 0]


def _sample_mixers(x, nb, layer, pool_prev, state, p):
    u, q, k, v, og, gi, gf = _inproj(x, p["w_main"], p["wg"], nb)
    ext = jnp.concatenate([pool_prev, u.reshape(nb, 1, POOL_WIDTH)], axis=1)
    period = ext.shape[1]
    pooled = _pool(ext.reshape(nb * period, POOL_WIDTH), p["pool_w"], p["pool_scale"], period)
    pooled = pooled.reshape(nb, period, POOL_WIDTH)[:, period - 1]
    c_all, n_all, m0, c_new = state
    m0p = jnp.zeros((nb, LANES), F32).at[:, :MLSTM_HEADS].set(m0)
    hn, c1, n1, m1 = _mlstm_step(q, k, v, gi, gf, p["bi"], p["bf"], c_all, n_all, m0p, c_new, layer, MLSTM_STEP_SEQS)
    x = _mixout(pooled, hn, og, x, p["norm_g"], p["w_out"], *p["ln1"], nb)
    qc = _mm(x, p["wq"], nb, F32, scale=CA_HEAD_DIM ** -0.5, name="ca_q")
    qh = jnp.zeros((nb, SUBLANES, CA_HEAD_DIM), F32).at[:, :CA_HEADS].set(qc.reshape(nb, CA_HEADS, CA_HEAD_DIM))
    return x, qh, ext[:, -POOL_BUF:], c1, n1, m1[:, :MLSTM_HEADS]


def kernel(x_prompt, x_sample, mem_prompt, cache_pool, state_mlstm_C, state_mlstm_n, state_mlstm_m,
           cache_mem_k, cache_mem_v, emb_ln_g, emb_ln_b, w_in, b_i, b_f, pool_w, pool_scale,
           mlstm_norm_g, w_out, ln1_g, ln1_b, ca_wq, ca_wk, ca_wv, ca_wo, ln2_g, ln2_b,
           w_gr, b_gr, w_er, b_er, w_gate, w_up, w_down, ln3_g, ln3_b):
    bp, tp, d = x_prompt.shape
    bs, ts, _ = x_sample.shape
    tm_p, tm_s = 512, bs * ts
    xp = x_prompt.reshape(bp * tp, d)
    xs = _ln(x_sample.reshape(bs * ts, d), emb_ln_g, emb_ln_b, tm_s)
    mk4, mv4, mk2, mv2 = _mem_proj(mem_prompt, ca_wk, ca_wv)
    outs = [[] for _ in range(7)]
    sc = None
    for l in range(DEPTH):
        p = _layer_weights(l, w_in, b_i, b_f, pool_w, pool_scale, mlstm_norm_g, w_out, ln1_g, ln1_b, ca_wq,
                           ca_wo, ln2_g, ln2_b, w_gr, b_gr, w_er, b_er, w_gate, w_up, w_down, ln3_g, ln3_b)
        if l == 0:
            p["first"] = (emb_ln_g, emb_ln_b, state_mlstm_C)
        xp, pb, c1, n1, m1 = _prompt_mixers(xp, bp, tp, l, (mk2, mv2), p, tm_p, 2 * tm_p)
        if l == 0:
            sc = p["zeros"]
        xs, qh, sb, sc, ns, ms = _sample_mixers(xs, bs * ts, l, cache_pool[l],
                                                (state_mlstm_C, state_mlstm_n, state_mlstm_m[l], sc), p)
        xp, ctx = _moe(xp, p["wr"], p["br"], *p["experts"], *p["ln3"], 2 * tm_p, attn=(qh, cache_mem_k, cache_mem_v, l))
        xs = _mm_res_ln(ctx[:, :CA_HEADS].reshape(bs * ts, d), xs, p["wo"], *p["ln2"], tm_s)
        xs = _moe(xs, p["wr"], p["br"], *p["experts"], *p["ln3"], tm_s)
        for lst, val in zip(outs, (pb, c1, n1, m1, sb, ns, ms)):
            lst.append(val)
    pp, pc, pn, pm, sp, sn, sm = (jnp.stack(o) for o in outs)
    return (xp.reshape(bp, tp, d), xs.reshape(bs, ts, d), pp, pc, pn, pm, mk4, mv4, sp, sc, sn, sm)
```

```python
import functools

import jax
import jax.numpy as jnp
from jax import lax
from jax.experimental import pallas as pl
from jax.experimental.pallas import tpu as pltpu

F32 = jnp.float32
BF16 = jnp.bfloat16

D_MODEL = 1024
DEPTH = 4
POOL_WIDTH = 512
POOL_GROUPS = 4
POOL_GROUP_DIM = 128
POOL_WINDOWS = (2, 4, 8, 16)
POOL_BUF = 15
MLSTM_WIDTH = 512
MLSTM_HEADS = 4
MLSTM_HEAD_DIM = 128
N_MEM = 256
CA_HEADS = 4
CA_HEAD_DIM = 256
N_GROUPS = 4
EXPERTS_PER_GROUP = 4
N_EXPERTS = 16
D_EXPERT = 256
ALPHA = (2 * DEPTH) ** 0.25
LN_EPS = 1e-5
GATE_OFF = POOL_WIDTH + 4 * MLSTM_WIDTH

LANES = 128
SUBLANES = 8
VMEM_LIMIT = 56 * 1024 * 1024
MLSTM_CHUNK = 256
POOL_ROWS = 2048
MLSTM_STEP_SEQS = 16
STEP_INPUTS = 11
POOL_HALO = 16
ROUTER_ROWS = 32
MOE_SEGMENT = 512
MOE_CAP = 160


def _params(*sem):
    return pltpu.CompilerParams(dimension_semantics=sem, vmem_limit_bytes=VMEM_LIMIT)


def _dot(a, b):
    return jnp.dot(a, b, preferred_element_type=F32)


def _dot_nt(a, b, precision=None):
    return lax.dot_general(a, b, (((1,), (1,)), ((), ())), precision=precision,
                           preferred_element_type=F32)


def _dot_tn(a, b):
    return lax.dot_general(a, b, (((0,), (0,)), ((), ())), preferred_element_type=F32)


def _layer_norm(x, g, b):
    mu = jnp.mean(x, -1, keepdims=True)
    xc = x - mu
    var = jnp.mean(xc * xc, -1, keepdims=True)
    return xc * lax.rsqrt(var + LN_EPS) * g + b


def _unit_norm(x):
    mu = jnp.mean(x, -1, keepdims=True)
    xc = x - mu
    var = jnp.mean(xc * xc, -1, keepdims=True)
    return xc * lax.rsqrt(var + LN_EPS)


def _log_sigmoid(x):
    return jnp.minimum(x, 0.0) - jnp.log1p(jnp.exp(-jnp.abs(x)))


def _scan(x, axis, op, fill):
    n = x.shape[axis]
    idx = lax.broadcasted_iota(jnp.int32, x.shape, axis)
    s = 1
    while s < n:
        x = op(x, jnp.where(idx >= s, pltpu.roll(x, s, axis), fill))
        s *= 2
    return x


def _ln_kernel(x_ref, g_ref, b_ref, o_ref):
    o_ref[...] = _layer_norm(x_ref[...], g_ref[...], b_ref[...])


def _ln(x, g, b, tm):
    n, d = x.shape
    row = pl.BlockSpec((tm, d), lambda i: (i, 0))
    vec = pl.BlockSpec((1, d), lambda i: (0, 0))
    return pl.pallas_call(
        _ln_kernel, grid=(n // tm,), in_specs=[row, vec, vec], out_specs=row,
        out_shape=jax.ShapeDtypeStruct((n, d), F32), compiler_params=_params("parallel"),
        name="emb_ln")(x, g.reshape(1, d), b.reshape(1, d))


def _qkvo(xb, w_ref, q_ref, k_ref, v_ref, og_ref):
    w = MLSTM_WIDTH
    q = _dot(xb, w_ref[:, POOL_WIDTH:POOL_WIDTH + w]) * (MLSTM_HEAD_DIM ** -0.5)
    q_ref[...] = q.astype(q_ref.dtype)
    k_ref[...] = _dot(xb, w_ref[:, POOL_WIDTH + w:POOL_WIDTH + 2 * w]).astype(k_ref.dtype)
    v_ref[...] = _dot(xb, w_ref[:, POOL_WIDTH + 2 * w:POOL_WIDTH + 3 * w]).astype(v_ref.dtype)
    og_ref[...] = _dot(xb, w_ref[:, POOL_WIDTH + 3 * w:POOL_WIDTH + 4 * w])


def _inproj_kernel(x_ref, w_ref, wg_ref, u_ref, q_ref, k_ref, v_ref, og_ref, gi_ref, gf_ref):
    xb = x_ref[...].astype(BF16)
    u_ref[...] = _dot(xb, w_ref[:, 0:POOL_WIDTH])
    _qkvo(xb, w_ref, q_ref, k_ref, v_ref, og_ref)
    g = _dot(xb, wg_ref[...])
    gi_ref[...] = g[:, 0:LANES]
    gf_ref[...] = g[:, LANES:2 * LANES]


def _inproj(x, w_main, wg, tm):
    n, d = x.shape
    row = lambda c: pl.BlockSpec((tm, c), lambda i: (i, 0))
    full = lambda a: pl.BlockSpec(a.shape, lambda i: (0, 0))
    widths = (POOL_WIDTH, MLSTM_WIDTH, MLSTM_WIDTH, MLSTM_WIDTH, MLSTM_WIDTH, LANES, LANES)
    return pl.pallas_call(
        _inproj_kernel, grid=(n // tm,), in_specs=[row(d), full(w_main), full(wg)],
        out_specs=tuple(row(c) for c in widths),
        out_shape=tuple(jax.ShapeDtypeStruct((n, c), F32) for c in widths),
        compiler_params=_params("parallel"), name="inproj")(x, w_main, wg)


def _inproj_seq_kernel(*refs, per, first, ride):
    n_in = 11 + (2 if first else 0) + (STEP_INPUTS if ride else 0)
    x_ref, w_ref, wgt_ref, pw_ref, sc_ref, wg_ref, wu_ref, wd_ref, wo_ref, wq_ref, wa_ref = refs[:11]
    outs, carry_ref = refs[n_in:-1], refs[-1]
    (po_ref, tail_ref, q_ref, k_ref, v_ref, og_ref, git_ref, gft_ref,
     cg_ref, cu_ref, cd_ref, co_ref, cq_ref, ca_ref) = outs[:14]
    tm = x_ref.shape[0]
    start = lax.rem(pl.program_id(0), per) * tm

    @pl.when(start == 0)
    def _():
        carry_ref[...] = jnp.zeros_like(carry_ref)

    for src, dst in ((wg_ref, cg_ref), (wu_ref, cu_ref), (wd_ref, cd_ref), (wo_ref, co_ref), (wq_ref, cq_ref),
                     (wa_ref, ca_ref)):
        dst[...] = src[...].astype(BF16)
    if ride:
        _mlstm_step_kernel(*refs[11:11 + STEP_INPUTS], *outs[14:18])
    x = x_ref[...]
    if first:
        x = _layer_norm(x, refs[11][...], refs[12][...])
        outs[14][...] = x
        outs[15][...] = jnp.zeros_like(outs[15])

    xb = x.astype(BF16)
    u = _dot(xb, w_ref[:, 0:POOL_WIDTH])
    halo = carry_ref.shape[0]
    ext = jnp.concatenate([carry_ref[...], u], axis=0)
    tail = ext[tm:, :]
    carry_ref[...] = tail
    tail_ref[...] = tail
    pos = start + lax.broadcasted_iota(jnp.int32, (tm, POOL_GROUP_DIM), 0)
    for g, win in enumerate(POOL_WINDOWS):
        sl = slice(g * POOL_GROUP_DIM, (g + 1) * POOL_GROUP_DIM)
        acc = ext[:, sl]
        s = 1
        while s < win:
            acc = acc + pltpu.roll(acc, s, 0)
            s *= 2
        cnt = jnp.minimum(pos + 1, win).astype(F32)
        dlt = acc[halo:, :] / cnt - u[:, sl]
        po_ref[:, sl] = (_dot(dlt.astype(BF16), pw_ref[g]) * sc_ref[:, sl]).astype(po_ref.dtype)
    _qkvo(xb, w_ref, q_ref, k_ref, v_ref, og_ref)
    gt = _dot_nt(wgt_ref[...], xb)
    git_ref[...] = gt[0:SUBLANES]
    gft_ref[...] = gt[SUBLANES:2 * SUBLANES]


def _inproj_seq(x, w_main, wgt, pw, scale, experts, dense, layer, seq, tm, first=None, ride=None):
    n, d = x.shape
    per = seq // tm
    steps = n // tm
    split = steps // N_EXPERTS
    assert split * N_EXPERTS == steps and d % steps == 0 and d % split == 0 and D_EXPERT % split == 0
    row = lambda c: pl.BlockSpec((tm, c), lambda i: (i, 0))
    full = lambda a: pl.BlockSpec(a.shape, lambda i: (0,) * a.ndim)
    gt_spec = pl.BlockSpec((None, SUBLANES, tm), lambda i: (i // per, 0, i % per))
    gt_shape = jax.ShapeDtypeStruct((n // seq, SUBLANES, seq), F32)
    mw, e, gf = MLSTM_WIDTH, EXPERTS_PER_GROUP, EXPERTS_PER_GROUP * D_EXPERT
    dr, fr, wr = d // split, D_EXPERT // split, d // steps
    col_in = pl.BlockSpec((None, None, dr, D_EXPERT), lambda i: (layer, i // split, i % split, 0))
    row_in = pl.BlockSpec((None, None, fr, d), lambda i: (layer, i // split, i % split, 0))
    col_out = pl.BlockSpec((None, dr, D_EXPERT), lambda i: (i // split // e, i % split, (i // split) % e))
    row_out = pl.BlockSpec((None, fr, d), lambda i: (i // split // e, ((i // split) % e) * split + i % split, 0))
    dense_in = pl.BlockSpec((None, wr, d), lambda i: (layer, i, 0))
    dense_out = pl.BlockSpec((wr, d), lambda i: (i, 0))
    out_shape = [
        jax.ShapeDtypeStruct((n, POOL_WIDTH), BF16),
        jax.ShapeDtypeStruct((steps, POOL_HALO, POOL_WIDTH), F32),
        jax.ShapeDtypeStruct((n, mw), BF16), jax.ShapeDtypeStruct((n, mw), BF16), jax.ShapeDtypeStruct((n, mw), BF16),
        jax.ShapeDtypeStruct((n, mw), F32), gt_shape, gt_shape,
        jax.ShapeDtypeStruct((N_GROUPS, d, gf), BF16), jax.ShapeDtypeStruct((N_GROUPS, d, gf), BF16),
        jax.ShapeDtypeStruct((N_GROUPS, gf, d), BF16),
        jax.ShapeDtypeStruct((d, d), BF16), jax.ShapeDtypeStruct((d, d), BF16), jax.ShapeDtypeStruct((d, d), BF16),
    ]
    out_specs = [row(POOL_WIDTH), pl.BlockSpec((None, POOL_HALO, POOL_WIDTH), lambda i: (i, 0, 0)),
                 row(mw), row(mw), row(mw), row(mw), gt_spec, gt_spec,
                 col_out, col_out, row_out, dense_out, dense_out, dense_out]
    extra_in, extra_args = [], []
    if first is not None:
        ln_g, ln_b, zeros_like = first
        lead = zeros_like.shape[0] * zeros_like.shape[1]
        assert lead % steps == 0 and zeros_like.shape[1] % (lead // steps) == 0
        zb = lead // steps
        zper = zeros_like.shape[1] // zb
        out_shape += [jax.ShapeDtypeStruct((n, d), F32), jax.ShapeDtypeStruct(zeros_like.shape, zeros_like.dtype)]
        out_specs += [row(d), pl.BlockSpec((None, zb) + zeros_like.shape[2:],
                                           lambda i: (i // zper, i % zper) + (0,) * (zeros_like.ndim - 2))]
        extra_args = [ln_g.reshape(1, d), ln_b.reshape(1, d)]
        extra_in = [full(a) for a in extra_args]
    aliases = {}
    if ride is not None:
        assert first is None
        rq, rk, rv, rgi, rgf, rbi, rbf, c_all, n_all, rm, c_new = ride
        n_s = rq.shape[0]
        sb = n_s // steps
        assert sb * steps == n_s
        hd = MLSTM_HEAD_DIM
        r3 = lambda a: a.reshape(steps, sb, a.shape[-1])
        s3 = lambda c: pl.BlockSpec((None, sb, c), lambda i: (i, 0, 0))
        cspec = pl.BlockSpec((None, sb, MLSTM_HEADS, hd, hd), lambda i: (layer, i, 0, 0, 0))
        nspec = pl.BlockSpec((None, sb, MLSTM_HEADS, hd), lambda i: (layer, i, 0, 0))
        extra_args = [r3(rq), r3(rk), r3(rv), r3(rgi), r3(rgf), rbi, rbf, c_all, n_all, r3(rm), c_new]
        assert len(extra_args) == STEP_INPUTS
        extra_in = [s3(mw), s3(mw), s3(mw), s3(LANES), s3(LANES), full(rbi), full(rbf), cspec, nspec, s3(LANES),
                    pl.BlockSpec(memory_space=pl.ANY)]
        aliases = {11 + STEP_INPUTS - 1: len(out_shape) + 1}
        out_shape += [jax.ShapeDtypeStruct((steps, sb, mw), F32), jax.ShapeDtypeStruct(c_new.shape, F32),
                      jax.ShapeDtypeStruct(n_all.shape[1:], F32), jax.ShapeDtypeStruct((steps, sb, LANES), F32)]
        out_specs += [s3(mw), cspec, pl.BlockSpec((sb, MLSTM_HEADS, hd), lambda i: (i, 0, 0)), s3(LANES)]
    scale = scale.reshape(1, POOL_WIDTH)
    return pl.pallas_call(
        functools.partial(_inproj_seq_kernel, per=per, first=first is not None, ride=ride is not None), grid=(steps,),
        in_specs=[row(d), full(w_main), full(wgt), full(pw), full(scale), col_in, col_in, row_in,
                  dense_in, dense_in, dense_in] + extra_in,
        out_specs=tuple(out_specs), out_shape=tuple(out_shape),
        scratch_shapes=[pltpu.VMEM((POOL_HALO, POOL_WIDTH), F32)], input_output_aliases=aliases,
        compiler_params=_params("arbitrary"), name="inproj_seq")(x, w_main, wgt, pw, scale, *experts, *dense,
                                                                 *extra_args)


def _pool_kernel(ext_ref, pw_ref, sc_ref, o_ref, *, period):
    rows = ext_ref.shape[0]
    r = lax.broadcasted_iota(jnp.int32, (rows, POOL_GROUP_DIM), 0)
    if period != rows:
        r = lax.rem(r, period)
    for g, win in enumerate(POOL_WINDOWS):
        sl = slice(g * POOL_GROUP_DIM, (g + 1) * POOL_GROUP_DIM)
        x = ext_ref[:, sl]
        acc = x
        s = 1
        while s < win:
            acc = acc + jnp.where(r >= s, pltpu.roll(acc, s, 0), 0.0)
            s *= 2
        cnt = jnp.minimum(r + 1, win).astype(F32)
        d = acc / cnt - x
        y = _dot(d.astype(BF16), pw_ref[g]) * sc_ref[:, sl]
        o_ref[:, sl] = y.astype(o_ref.dtype)


def _pool(ext2d, pw, scale, period):
    n, c = ext2d.shape
    row = pl.BlockSpec((POOL_ROWS, c), lambda i: (i, 0))
    return pl.pallas_call(
        functools.partial(_pool_kernel, period=period), grid=(n // POOL_ROWS,),
        in_specs=[row, pl.BlockSpec(pw.shape, lambda i: (0, 0, 0)), pl.BlockSpec((1, c), lambda i: (0, 0))],
        out_specs=row, out_shape=jax.ShapeDtypeStruct((n, c), BF16),
        compiler_params=_params("parallel"), name="pool")(ext2d, pw, scale.reshape(1, c))


def _mlstm_chunk_kernel(q_ref, k_ref, v_ref, git_ref, gft_ref, bit_ref, bft_ref, hn_ref, c_ref, n_ref, m_ref,
                        at_s, ws_s, dc_s, mc_s, wi_s, fl_s):
    nb, L, _ = q_ref.shape
    rows = nb * SUBLANES

    @pl.when(pl.program_id(0) == 0)
    def _():
        c_ref[...] = jnp.zeros_like(c_ref)
        n_ref[...] = jnp.zeros_like(n_ref)
        m_ref[...] = jnp.zeros_like(m_ref)

    ig = git_ref[...].reshape(rows, L) + bit_ref[...]
    lf = _log_sigmoid(gft_ref[...].reshape(rows, L) + bft_ref[...])
    tri = (lax.broadcasted_iota(jnp.int32, (L, L), 0) <= lax.broadcasted_iota(jnp.int32, (L, L), 1)).astype(BF16)
    lf_hi = lf.astype(BF16)
    lf_mid = (lf - lf_hi.astype(F32)).astype(BF16)
    lf_lo = (lf - lf_hi.astype(F32) - lf_mid.astype(F32)).astype(BF16)
    bc = _dot(lf_hi, tri) + (_dot(lf_mid, tri) + _dot(lf_lo, tri))
    at = ig - bc
    m_prev = m_ref[...].reshape(rows, LANES)[:, 0:1]
    mc = jnp.maximum(_scan(at, 1, jnp.maximum, -jnp.inf), m_prev)
    mt = bc + mc
    m_last = mc[:, L - 1:L]
    at_s[...] = at.reshape(nb, SUBLANES, L)
    ws_s[...] = jnp.exp(at - m_last).reshape(nb, SUBLANES, L)
    dc_s[...] = jnp.broadcast_to(jnp.exp(m_prev - m_last), (rows, LANES)).reshape(nb, SUBLANES, LANES)
    m_ref[...] = jnp.broadcast_to(mt[:, L - 1:L], (rows, LANES)).reshape(nb, SUBLANES, LANES)
    wi = jnp.exp(m_prev - mc)
    fl = jnp.exp(-mt)
    for b in range(nb):
        rs = slice(b * SUBLANES, (b + 1) * SUBLANES)
        mc_s[b] = mc[rs].T
        wi_s[b] = wi[rs].T
        fl_s[b] = fl[rs].T

    causal = (lax.broadcasted_iota(jnp.int32, (L, L), 0) >= lax.broadcasted_iota(jnp.int32, (L, L), 1))

    def per_batch(heads, b, carry):
        at = at_s[b]
        ws = ws_s[b]
        decay = dc_s[b]
        mc_c, wi_c, fl_c = mc_s[b], wi_s[b], fl_s[b]
        ws16 = jnp.concatenate([ws, ws], axis=0).astype(BF16)
        for h in heads:
            sl = slice(h * MLSTM_HEAD_DIM, (h + 1) * MLSTM_HEAD_DIM)
            col = slice(h, h + 1)
            qh = q_ref[b, :, sl]
            kh = k_ref[b, :, sl]
            vh = v_ref[b, :, sl]
            s = _dot_nt(qh, kh)
            p = jnp.where(causal, s * jnp.exp(at[col, :] - mc_c[:, col]), 0.0)
            c_old = c_ref[b, h]
            n_old = n_ref[b, col, :]
            wi = wi_c[:, col]
            dv = MLSTM_HEAD_DIM
            v_ext = jnp.concatenate([vh, jnp.ones_like(vh)], axis=1)
            c_ext = jnp.concatenate([c_old, jnp.broadcast_to(n_old, c_old.shape)], axis=0)
            intra = _dot(p.astype(BF16), v_ext)
            inter = _dot_nt(qh, c_ext.astype(BF16))
            num = intra[:, :dv] + wi * inter[:, :dv]
            qn = intra[:, dv:] + wi * inter[:, dv:]
            hh = num / jnp.maximum(jnp.abs(qn), fl_c[:, col])
            hn_ref[b, :, sl] = _unit_norm(hh)
            dc = decay[col, 0:1]
            vts = (vh.astype(F32).T * ws[col, :]).astype(BF16)
            c_ref[b, h] = dc * c_old + _dot(vts, kh)
            n_ref[b, col, :] = dc * n_old + _dot(ws16, kh)[col, :]
        return carry

    for h0 in range(0, MLSTM_HEADS, 2):
        lax.fori_loop(0, nb, functools.partial(per_batch, (h0, h0 + 1)), 0)


def _mlstm_chunk(q, k, v, git, gft, bit, bft):
    nb, t, w = q.shape
    L = MLSTM_CHUNK
    seq = lambda c: pl.BlockSpec((nb, L, c), lambda i: (0, i, 0))
    seqt = pl.BlockSpec((nb, SUBLANES, L), lambda i: (0, 0, i))
    vec = lambda a: pl.BlockSpec(a.shape, lambda i: (0, 0))
    out_shape = (
        jax.ShapeDtypeStruct((nb, t, w), F32),
        jax.ShapeDtypeStruct((nb, MLSTM_HEADS, MLSTM_HEAD_DIM, MLSTM_HEAD_DIM), F32),
        jax.ShapeDtypeStruct((nb, MLSTM_HEADS, MLSTM_HEAD_DIM), F32),
        jax.ShapeDtypeStruct((nb, SUBLANES, LANES), F32),
    )
    out_specs = (
        seq(w),
        pl.BlockSpec(out_shape[1].shape, lambda i: (0, 0, 0, 0)),
        pl.BlockSpec(out_shape[2].shape, lambda i: (0, 0, 0)),
        pl.BlockSpec(out_shape[3].shape, lambda i: (0, 0, 0)),
    )
    bit, bft = jnp.tile(bit, (nb, 1)), jnp.tile(bft, (nb, 1))
    scratch = [pltpu.VMEM((nb, SUBLANES, L), F32), pltpu.VMEM((nb, SUBLANES, L), F32),
               pltpu.VMEM((nb, SUBLANES, LANES), F32)] + [pltpu.VMEM((nb, L, SUBLANES), F32) for _ in range(3)]
    return pl.pallas_call(
        _mlstm_chunk_kernel, grid=(t // L,),
        in_specs=[seq(w), seq(w), seq(w), seqt, seqt, vec(bit), vec(bft)],
        out_specs=out_specs, out_shape=out_shape, scratch_shapes=scratch, compiler_params=_params("arbitrary"),
        name="mlstm_chunk")(q, k, v, git, gft, bit, bft)


def _mlstm_step_kernel(q_ref, k_ref, v_ref, gi_ref, gf_ref, bi_ref, bf_ref, c_ref, n_ref, m_ref, c_new_ref,
                       hn_ref, co_ref, no_ref, mo_ref):
    del c_new_ref
    tb = q_ref.shape[0]
    d = MLSTM_HEAD_DIM
    ig = gi_ref[...] + bi_ref[...]
    lf = _log_sigmoid(gf_ref[...] + bf_ref[...])
    m_old = m_ref[...]
    mt = jnp.maximum(lf + m_old, ig)
    mo_ref[...] = mt
    wa = jnp.exp(ig - mt)
    wi = jnp.exp(lf + m_old - mt)
    fl = jnp.exp(-mt)
    wa_t, wi_t, fl_t = wa.T, wi.T, fl.T
    lane = lax.broadcasted_iota(jnp.int32, (d, tb), 1)
    for h in range(MLSTM_HEADS):
        sl = slice(h * d, (h + 1) * d)
        qh, kh, vh = q_ref[:, sl], k_ref[:, sl], v_ref[:, sl]
        nh = n_ref[:, h, :]
        qt, kt, vt, nt = qh.T, kh.T, vh.T, nh.T
        wa_r, wi_r, fl_r = wa_t[h:h + 1, :], wi_t[h:h + 1, :], fl_t[h:h + 1, :]
        s = jnp.sum(qt * kt, 0, keepdims=True) * wa_r
        cq = jnp.zeros((d, tb), F32)
        qtb = qt.astype(BF16)
        for b in range(tb):
            cq = jnp.where(lane == b, _dot(c_ref[b, h].astype(BF16), qtb), cq)
        num = s * vt + wi_r * cq
        qn = s + wi_r * jnp.sum(nt * qt, 0, keepdims=True)
        hh = num / jnp.maximum(jnp.abs(qn), fl_r)
        mu = jnp.mean(hh, 0, keepdims=True)
        xc = hh - mu
        var = jnp.mean(xc * xc, 0, keepdims=True)
        hn_ref[:, sl] = (xc * lax.rsqrt(var + LN_EPS)).T
        wav = wa_r * vt
        for b in range(tb):
            co_ref[b, h] = wi_r[:, b:b + 1] * c_ref[b, h] + wav[:, b:b + 1] * kh[b:b + 1, :]
        no_ref[:, h, :] = wi[:, h:h + 1] * nh + wa[:, h:h + 1] * kh


def _mlstm_step(q, k, v, gi, gf, bi, bf, c_all, n_all, m0, c_new, layer, tb):
    n, w = q.shape
    d = MLSTM_HEAD_DIM
    row = lambda c: pl.BlockSpec((tb, c), lambda i: (i, 0))
    vec = lambda a: pl.BlockSpec(a.shape, lambda i: (0, 0))
    cin = pl.BlockSpec((None, tb, MLSTM_HEADS, d, d), lambda i: (layer, i, 0, 0, 0))
    nin = pl.BlockSpec((None, tb, MLSTM_HEADS, d), lambda i: (layer, i, 0, 0))
    nout = pl.BlockSpec((tb, MLSTM_HEADS, d), lambda i: (i, 0, 0))
    out_shape = (
        jax.ShapeDtypeStruct((n, w), F32),
        jax.ShapeDtypeStruct(c_new.shape, F32),
        jax.ShapeDtypeStruct(n_all.shape[1:], F32),
        jax.ShapeDtypeStruct((n, LANES), F32),
    )
    return pl.pallas_call(
        _mlstm_step_kernel, grid=(n // tb,),
        in_specs=[row(w), row(w), row(w), row(LANES), row(LANES), vec(bi), vec(bf), cin, nin, row(LANES),
                  pl.BlockSpec(memory_space=pl.ANY)],
        out_specs=(row(w), cin, nout, row(LANES)), out_shape=out_shape, input_output_aliases={10: 1},
        compiler_params=_params("parallel"), name="mlstm_step")(q, k, v, gi, gf, bi, bf, c_all, n_all, m0, c_new)


def _mixout_kernel(po_ref, hn_ref, og_ref, x_ref, ng_ref, w_ref, g_ref, b_ref, o_ref):
    mo = jax.nn.sigmoid(og_ref[...]) * hn_ref[...] * ng_ref[...]
    mix = _dot(po_ref[...], w_ref[0:POOL_WIDTH, :]) + _dot(mo.astype(BF16), w_ref[POOL_WIDTH:, :])
    o_ref[...] = _layer_norm(ALPHA * x_ref[...] + mix, g_ref[...], b_ref[...])


def _mixout(po, hn, og, x, ng, w, g, b, tm):
    n, d = x.shape
    row = lambda c: pl.BlockSpec((tm, c), lambda i: (i, 0))
    vec = lambda a: pl.BlockSpec(a.shape, lambda i: (0, 0))
    return pl.pallas_call(
        _mixout_kernel, grid=(n // tm,),
        in_specs=[row(POOL_WIDTH), row(MLSTM_WIDTH), row(MLSTM_WIDTH), row(d), vec(ng), vec(w), vec(g), vec(b)],
        out_specs=row(d), out_shape=jax.ShapeDtypeStruct((n, d), F32),
        compiler_params=_params("parallel"), name="mixout")(po, hn, og, x, ng, w, g, b)


def _mm_kernel(x_ref, w_ref, o_ref, *, scale):
    y = _dot(x_ref[...].astype(BF16), w_ref[...])
    if scale != 1.0:
        y = y * scale
    o_ref[...] = y.astype(o_ref.dtype)


def _mm(x, w, tm, out_dtype, scale=1.0, name="proj"):
    n, d = x.shape
    dout = w.shape[1]
    return pl.pallas_call(
        functools.partial(_mm_kernel, scale=scale), grid=(n // tm,),
        in_specs=[pl.BlockSpec((tm, d), lambda i: (i, 0)), pl.BlockSpec(w.shape, lambda i: (0, 0))],
        out_specs=pl.BlockSpec((tm, dout), lambda i: (i, 0)),
        out_shape=jax.ShapeDtypeStruct((n, dout), out_dtype),
        compiler_params=_params("parallel"), name=name)(x, w)


def _cast_inproj_kernel(wt_ref, o_ref, og_ref, *, n_main):
    i = pl.program_id(0)

    @pl.when(i < n_main)
    def _():
        o_ref[...] = wt_ref[...].T.astype(BF16)

    @pl.when(i == n_main)
    def _():
        og_ref[...] = wt_ref[0:2 * MLSTM_HEADS, :]


def _cast_inproj(w_in, layer, tc=256):
    wt = jnp.swapaxes(w_in, 1, 2)
    d = wt.shape[2]
    n_main = GATE_OFF // tc
    return pl.pallas_call(
        functools.partial(_cast_inproj_kernel, n_main=n_main), grid=(n_main + 1,),
        in_specs=[pl.BlockSpec((None, tc, d), lambda i: (layer, i, 0))],
        out_specs=(pl.BlockSpec((d, tc), lambda i: (0, jnp.minimum(i, n_main - 1))),
                   pl.BlockSpec((2 * MLSTM_HEADS, d), lambda i: (0, 0))),
        out_shape=(jax.ShapeDtypeStruct((d, GATE_OFF), BF16), jax.ShapeDtypeStruct((2 * MLSTM_HEADS, d), F32)),
        compiler_params=_params("arbitrary"), name="inproj_cast")(wt)


def _mem_proj_kernel(x_ref, wk_ref, wv_ref, k4_ref, v4_ref, k2_ref, v2_ref, wkb, wvb):
    @pl.when(pl.program_id(1) == 0)
    def _():
        wkb[...] = wk_ref[...].astype(BF16)
        wvb[...] = wv_ref[...].astype(BF16)

    bb, m, d = x_ref.shape
    xb = x_ref[...].reshape(bb * m, d).astype(BF16)
    k = _dot(xb, wkb[...])
    v = _dot(xb, wvb[...])
    k2_ref[...] = k.reshape(bb, m, d)
    v2_ref[...] = v.reshape(bb, m, d)
    k4_ref[...] = k.reshape(bb, m, CA_HEADS, CA_HEAD_DIM)
    v4_ref[...] = v.reshape(bb, m, CA_HEADS, CA_HEAD_DIM)


def _mem_proj(mem, wk_all, wv_all, bb=2):
    nb, m, d = mem.shape
    depth = wk_all.shape[0]
    wspec = pl.BlockSpec((None, d, d), lambda l, b: (l, 0, 0))
    o4 = pl.BlockSpec((None, bb, m, CA_HEADS, CA_HEAD_DIM), lambda l, b: (l, b, 0, 0, 0))
    o2 = pl.BlockSpec((None, bb, m, d), lambda l, b: (l, b, 0, 0))
    s4 = jax.ShapeDtypeStruct((depth, nb, m, CA_HEADS, CA_HEAD_DIM), F32)
    s2 = jax.ShapeDtypeStruct((depth, nb, m, d), F32)
    return pl.pallas_call(
        _mem_proj_kernel, grid=(depth, nb // bb),
        in_specs=[pl.BlockSpec((bb, m, d), lambda l, b: (b, 0, 0)), wspec, wspec],
        out_specs=(o4, o4, o2, o2), out_shape=(s4, s4, s2, s2),
        scratch_shapes=[pltpu.VMEM((d, d), BF16), pltpu.VMEM((d, d), BF16)],
        compiler_params=_params("arbitrary", "arbitrary"), name="mem_proj")(mem, wk_all, wv_all)


def _attn_block_kernel(po_ref, hn_ref, og_ref, x_ref, k_ref, v_ref, ng_ref, wm_ref, g1_ref, b1_ref,
                       wq_ref, wo_ref, g_ref, b_ref, o_ref):
    mo = jax.nn.sigmoid(og_ref[...]) * hn_ref[...] * ng_ref[...]
    mix = _dot(po_ref[...], wm_ref[0:POOL_WIDTH, :]) + _dot(mo.astype(BF16), wm_ref[POOL_WIDTH:, :])
    x = _layer_norm(ALPHA * x_ref[...] + mix, g1_ref[...], b1_ref[...])
    qb = (_dot(x.astype(BF16), wq_ref[...]) * (CA_HEAD_DIM ** -0.5)).astype(BF16)
    kb = k_ref[...].astype(BF16)
    vb = v_ref[...].astype(BF16)
    ctx = []
    for h in range(CA_HEADS):
        sl = slice(h * CA_HEAD_DIM, (h + 1) * CA_HEAD_DIM)
        s = _dot_nt(qb[:, sl], kb[:, sl])
        e = jnp.exp(s - jnp.max(s, -1, keepdims=True))
        ctx.append((_dot(e.astype(BF16), vb[:, sl]) / jnp.sum(e, -1, keepdims=True)).astype(BF16))
    y = _dot(jnp.concatenate(ctx, axis=1), wo_ref[...])
    o_ref[...] = _layer_norm(ALPHA * x + y, g_ref[...], b_ref[...])


def _attn_block(po, hn, og, x, k, v, layer, ng, wm, ln1, wq, wo, ln2, seq, tq):
    n, d = x.shape
    per = seq // tq
    row = lambda c: pl.BlockSpec((tq, c), lambda i: (i, 0))
    kspec = pl.BlockSpec((None, None, N_MEM, d), lambda i: (layer, i // per, 0, 0))
    vec = lambda a: pl.BlockSpec(a.shape, lambda i: (0, 0))
    return pl.pallas_call(
        _attn_block_kernel, grid=(n // tq,),
        in_specs=[row(POOL_WIDTH), row(MLSTM_WIDTH), row(MLSTM_WIDTH), row(d), kspec, kspec, vec(ng), vec(wm),
                  vec(ln1[0]), vec(ln1[1]), vec(wq), vec(wo), vec(ln2[0]), vec(ln2[1])],
        out_specs=row(d), out_shape=jax.ShapeDtypeStruct((n, d), F32),
        compiler_params=_params("parallel"), name="attn_block")(po, hn, og, x, k, v, ng, wm, *ln1, wq, wo, *ln2)


def _attn_step_kernel(q_ref, k_ref, v_ref, o_ref):
    bb = q_ref.shape[0]
    rows = N_MEM * CA_HEADS
    lane = lax.broadcasted_iota(jnp.int32, (SUBLANES, rows), 1)
    row = lax.broadcasted_iota(jnp.int32, (SUBLANES, rows), 0)
    own = lax.rem(lane, CA_HEADS) == lax.rem(row, CA_HEADS)
    for j in range(bb):
        kf = k_ref[j].reshape(rows, CA_HEAD_DIM).astype(BF16)
        vf = v_ref[j].reshape(rows, CA_HEAD_DIM).astype(BF16)
        s = jnp.where(own, _dot_nt(q_ref[j].astype(BF16), kf), -jnp.inf)
        e = jnp.exp(s - jnp.max(s, -1, keepdims=True))
        o_ref[j] = _dot(e.astype(BF16), vf) / jnp.sum(e, -1, keepdims=True)


def _mm_res_ln_kernel(a_ref, x_ref, w_ref, g_ref, b_ref, o_ref):
    y = _dot(a_ref[...].astype(BF16), w_ref[...])
    o_ref[...] = _layer_norm(ALPHA * x_ref[...] + y, g_ref[...], b_ref[...])


def _mm_res_ln(a, x, w, g, b, tm):
    n, d = x.shape
    row = lambda c: pl.BlockSpec((tm, c), lambda i: (i, 0))
    vec = lambda arr: pl.BlockSpec(arr.shape, lambda i: (0, 0))
    return pl.pallas_call(
        _mm_res_ln_kernel, grid=(n // tm,),
        in_specs=[row(a.shape[1]), row(d), vec(w), vec(g), vec(b)], out_specs=row(d),
        out_shape=jax.ShapeDtypeStruct((n, d), F32),
        compiler_params=_params("parallel"), name="proj_res_ln")(a, x, w, g, b)


def _route(lt):
    gl = [lt[g:g + 1, :] for g in range(N_GROUPS)]
    gmax = functools.reduce(jnp.maximum, gl)
    gsum = functools.reduce(jnp.add, [jnp.exp(x - gmax) for x in gl])
    pg_sel = 1.0 / gsum

    def first_max(vals):
        m = functools.reduce(jnp.maximum, vals)
        taken = jnp.zeros_like(m, dtype=jnp.bool_)
        hot = []
        for x in vals:
            h = jnp.logical_and(x == m, jnp.logical_not(taken))
            taken = jnp.logical_or(taken, h)
            hot.append(h)
        return m, hot

    _, g_hot = first_max(gl)
    el = []
    for j in range(EXPERTS_PER_GROUP):
        rows = [lt[SUBLANES + g * EXPERTS_PER_GROUP + j:SUBLANES + g * EXPERTS_PER_GROUP + j + 1, :]
                for g in range(N_GROUPS)]
        x = rows[N_GROUPS - 1]
        for g in range(N_GROUPS - 2, -1, -1):
            x = jnp.where(g_hot[g], rows[g], x)
        el.append(x)
    emax = functools.reduce(jnp.maximum, el)
    ee = [jnp.exp(x - emax) for x in el]
    esum = functools.reduce(jnp.add, ee)
    pe = [x / esum for x in ee]
    p1, hot1 = first_max(pe)
    p2, hot2 = first_max([jnp.where(h, -jnp.inf, x) for h, x in zip(hot1, pe)])
    psum = p1 + p2
    gate = [jnp.where(h1, pg_sel * p1 / psum, jnp.where(h2, pg_sel * p2 / psum, 0.0)) for h1, h2 in zip(hot1, hot2)]
    return g_hot, gate


def _moe_kernel(x_ref, wr_ref, br_ref, tri_ref, wg_ref, wu_ref, wd_ref, g_ref, b_ref, *rest):
    ride = None
    if len(rest) > 7:
        ride = functools.partial(_attn_step_kernel, rest[0], rest[1], rest[2], rest[4])
        rest = rest[3:4] + rest[5:]
    o_ref, xb_ref, ct_ref, comb_ref, perm_ref, acc_ref, full_ref = rest
    grp = pl.program_id(1)
    tm = x_ref.shape[0]
    ns, cap, sw = perm_ref.shape[1:]
    f = D_EXPERT

    @pl.when(grp == 0)
    def _():
        x = x_ref[...]
        xh = x.astype(BF16)
        xb_ref[...] = xh
        xl = (x - xh.astype(F32)).astype(BF16)
        wr = wr_ref[...]
        wh = wr.astype(BF16)
        wl = (wr - wh.astype(F32)).astype(BF16)
        lt = _dot_nt(wh, xh) + (_dot_nt(wl, xh) + _dot_nt(wh, xl)) + br_ref[...]
        g_hot, gate = _route(lt)
        row8 = lax.broadcasted_iota(jnp.int32, (SUBLANES, tm), 0)
        hot8 = jnp.zeros((SUBLANES, tm), F32)
        for g in range(N_GROUPS):
            hot8 = jnp.where(jnp.logical_and(row8 == g, g_hot[g]), 1.0, hot8)
        most = None
        for s in range(ns):
            hs = hot8[:, s * sw:(s + 1) * sw]
            cum = _dot(hs.astype(BF16), tri_ref[...])
            seg_max = jnp.max(cum[:, sw - 1:sw])
            most = seg_max if most is None else jnp.maximum(most, seg_max)
            base = lax.broadcasted_iota(jnp.int32, (SUBLANES, sw), 0).astype(F32) * cap
            code = jnp.sum(hs * (base + cum - 1.0), 0, keepdims=True).astype(jnp.int32)
            for g in range(N_GROUPS):
                r = lax.broadcasted_iota(jnp.int32, (cap, sw), 0) + g * cap
                perm_ref[g, s] = jnp.where(r == code, 1.0, 0.0).astype(BF16)
        full_ref[0] = (most > cap).astype(jnp.int32)
        ct_ref[...] = jnp.zeros_like(ct_ref)
        for j in range(EXPERTS_PER_GROUP):
            ct_ref[j:j + 1, :] = gate[j]
        ct_ref[EXPERTS_PER_GROUP:EXPERTS_PER_GROUP + 1, :] = jnp.sum(hot8 * row8.astype(F32), 0, keepdims=True)
        comb_ref[...] = ct_ref[...].T
        acc_ref[...] = jnp.zeros_like(acc_ref)

    def experts(rows, gates):
        hh = []
        for j in range(EXPERTS_PER_GROUP):
            hg = _dot(rows, wg_ref[:, j * f:(j + 1) * f])
            hu = _dot(rows, wu_ref[:, j * f:(j + 1) * f])
            hh.append((hg * jax.nn.sigmoid(hg) * hu * gates[:, j:j + 1]).astype(BF16))
        return _dot(jnp.concatenate(hh, axis=1), wd_ref[...])

    @pl.when(full_ref[0] == 0)
    def _():
        comb = comb_ref[...]
        ch = comb.astype(BF16)
        cl = (comb - ch.astype(F32)).astype(BF16)
        xb = xb_ref[...]
        perm = [perm_ref[grp, s] for s in range(ns)]
        seg = lambda a, s: a[s * sw:(s + 1) * sw]
        rows = jnp.concatenate([_dot(perm[s], seg(xb, s)) for s in range(ns)], axis=0).astype(BF16)
        gates = jnp.concatenate([_dot(perm[s], seg(ch, s)) + _dot(perm[s], seg(cl, s)) for s in range(ns)], axis=0)
        y = experts(rows, gates).astype(BF16)
        for s in range(ns):
            acc_ref[s * sw:(s + 1) * sw, :] += _dot_tn(perm[s], y[s * cap:(s + 1) * cap])
        if ride is not None:
            ride()

    @pl.when(full_ref[0] != 0)
    def _():
        comb = comb_ref[...]
        own = comb[:, EXPERTS_PER_GROUP:EXPERTS_PER_GROUP + 1] == grp.astype(F32)
        acc_ref[...] += experts(xb_ref[...], jnp.where(own, comb, 0.0))
        if ride is not None:
            ride()

    @pl.when(grp == N_GROUPS - 1)
    def _():
        o_ref[...] = _layer_norm(ALPHA * x_ref[...] + acc_ref[...], g_ref[...], b_ref[...])


def _moe(x, wr, br, wg, wu, wd, g, b, tm, attn=None):
    n, d = x.shape
    ns = max(1, tm // MOE_SEGMENT)
    sw = tm // ns
    cap = min(sw, MOE_CAP)
    tri = (lax.broadcasted_iota(jnp.int32, (sw, sw), 0) <= lax.broadcasted_iota(jnp.int32, (sw, sw), 1)).astype(BF16)
    row = pl.BlockSpec((tm, d), lambda i, e: (i, 0))
    vec = lambda a: pl.BlockSpec(a.shape, lambda i, e: (0, 0))
    grp = lambda a: pl.BlockSpec((None,) + a.shape[1:], lambda i, e: (e, 0, 0))
    scratch = [pltpu.VMEM((tm, d), BF16), pltpu.VMEM((LANES, tm), F32), pltpu.VMEM((tm, LANES), F32),
               pltpu.VMEM((N_GROUPS, ns, cap, sw), BF16), pltpu.VMEM((tm, d), F32), pltpu.SMEM((1,), jnp.int32)]
    in_specs = [row, vec(wr), vec(br), vec(tri), grp(wg), grp(wu), grp(wd), vec(g), vec(b)]
    args = [x, wr, br, tri, wg, wu, wd, g, b]
    out_specs, out_shape = row, jax.ShapeDtypeStruct((n, d), F32)
    if attn is not None:
        q, k_all, v_all, layer = attn
        steps = (n // tm) * N_GROUPS
        bb = q.shape[0] // steps
        assert bb * steps == q.shape[0]
        qspec = pl.BlockSpec((bb, SUBLANES, CA_HEAD_DIM), lambda i, e: (i * N_GROUPS + e, 0, 0))
        kspec = pl.BlockSpec((None, bb, N_MEM, CA_HEADS, CA_HEAD_DIM), lambda i, e: (layer, i * N_GROUPS + e, 0, 0, 0))
        in_specs += [qspec, kspec, kspec]
        args += [q, k_all, v_all]
        out_specs, out_shape = (row, qspec), (out_shape, jax.ShapeDtypeStruct(q.shape, F32))
    return pl.pallas_call(
        _moe_kernel, grid=(n // tm, N_GROUPS), in_specs=in_specs, out_specs=out_specs, out_shape=out_shape,
        scratch_shapes=scratch, compiler_params=_params("arbitrary", "arbitrary"), name="moe")(*args)


def _pad_lanes(v):
    return jnp.zeros((1, LANES), F32).at[0, :v.shape[0]].set(v)


def _pad_rows(v):
    return jnp.zeros((SUBLANES, 1), F32).at[:v.shape[0], 0].set(v)


def _layer_weights(l, w_in, b_i, b_f, pool_w, pool_scale, mlstm_norm_g, w_out, ln1_g, ln1_b, ca_wq, ca_wo,
                   ln2_g, ln2_b, w_gr, b_gr, w_er, b_er, w_gate, w_up, w_down, ln3_g, ln3_b):
    d = D_MODEL
    w_main, w_gate_rows = _cast_inproj(w_in, l)
    w_gate_cols = w_gate_rows.T
    wg = jnp.zeros((d, 2 * LANES), F32)
    wg = wg.at[:, 0:MLSTM_HEADS].set(w_gate_cols[:, :MLSTM_HEADS])
    wg = wg.at[:, LANES:LANES + MLSTM_HEADS].set(w_gate_cols[:, MLSTM_HEADS:])
    wgt = jnp.zeros((2 * SUBLANES, d), F32)
    wgt = wgt.at[0:MLSTM_HEADS].set(w_gate_cols[:, :MLSTM_HEADS].T)
    wgt = wgt.at[SUBLANES:SUBLANES + MLSTM_HEADS].set(w_gate_cols[:, MLSTM_HEADS:].T)
    wr = jnp.zeros((ROUTER_ROWS, d), F32)
    wr = wr.at[0:N_GROUPS].set(w_gr[l].T).at[SUBLANES:SUBLANES + N_EXPERTS].set(w_er[l].T)
    br = jnp.zeros((ROUTER_ROWS, 1), F32)
    br = br.at[0:N_GROUPS, 0].set(b_gr[l]).at[SUBLANES:SUBLANES + N_EXPERTS, 0].set(b_er[l])
    row = lambda v: v.reshape(1, -1)
    return dict(
        w_main=w_main, wg=wg.astype(BF16), wgt=wgt.astype(BF16),
        bi=_pad_lanes(b_i[l]), bf=_pad_lanes(b_f[l]), bit=_pad_rows(b_i[l]), bft=_pad_rows(b_f[l]),
        pool_w=pool_w[l].astype(BF16), pool_scale=pool_scale[l], norm_g=row(mlstm_norm_g[l]),
        ln1=(row(ln1_g[l]), row(ln1_b[l])), ln2=(row(ln2_g[l]), row(ln2_b[l])), ln3=(row(ln3_g[l]), row(ln3_b[l])),
        wr=wr, br=br, raw_experts=(w_gate, w_up, w_down), raw_dense=(w_out, ca_wq, ca_wo))


def _prompt_mixers(x, nb, seq, layer, mem, p, tm, tm_big, ride=None):
    per = seq // tm
    outs = _inproj_seq(x, p["w_main"], p["wgt"], p["pool_w"], p["pool_scale"], p["raw_experts"], p["raw_dense"],
                       layer, seq, tm, first=p.get("first"), ride=ride)
    if ride is not None:
        hn_s, c_s, n_s, m_s = outs[14:18]
        p["step"] = (hn_s.reshape(-1, MLSTM_WIDTH), c_s, n_s, m_s.reshape(-1, LANES))
    pooled, tails, q, k, v, og, git, gft = outs[:8]
    p["experts"] = outs[8:11]
    p["w_out"], p["wq"], p["wo"] = outs[11:14]
    if p.get("first") is not None:
        x, p["zeros"] = outs[14:16]
    pool_buf = tails.reshape(nb, per, POOL_HALO, POOL_WIDTH)[:, per - 1, POOL_HALO - POOL_BUF:]
    r3 = lambda a: a.reshape(nb, seq, a.shape[-1])
    hn, c1, n1, m1 = _mlstm_chunk(r3(q), r3(k), r3(v), git, gft, p["bit"], p["bft"])
    x = _attn_block(pooled, hn.reshape(nb * seq, MLSTM_WIDTH), og, x, mem[0], mem[1], layer, p["norm_g"], p["w_out"],
                    p["ln1"], p["wq"], p["wo"], p["ln2"], seq, tm_big)
    return x, pool_buf, c1, n1, m1[:, :MLSTM_HEADS, 0]


def _sample_inproj(x, nb, state, p):
    u, q, k, v, og, gi, gf = _inproj(x, p["w_main"], p["wg"], nb)
    c_all, n_all, m0, c_new = state
    m0p = jnp.zeros((nb, LANES), F32).at[:, :MLSTM_HEADS].set(m0)
    return u, og, (q, k, v, gi, gf, p["bi"], p["bf"], c_all, n_all, m0p, c_new)


def _sample_mixers(x, nb, layer, pool_prev, u, og, step, p):
    ext = jnp.concatenate([pool_prev, u.reshape(nb, 1, POOL_WIDTH)], axis=1)
    period = ext.shape[1]
    pooled = _pool(ext.reshape(nb * period, POOL_WIDTH), p["pool_w"], p["pool_scale"], period)
    pooled = pooled.reshape(nb, period, POOL_WIDTH)[:, period - 1]
    hn, c1, n1, m1 = p["step"] if "step" in p else _mlstm_step(*step, layer, MLSTM_STEP_SEQS)
    x = _mixout(pooled, hn, og, x, p["norm_g"], p["w_out"], *p["ln1"], nb)
    qc = _mm(x, p["wq"], nb, F32, scale=CA_HEAD_DIM ** -0.5, name="ca_q")
    qh = jnp.zeros((nb, SUBLANES, CA_HEAD_DIM), F32).at[:, :CA_HEADS].set(qc.reshape(nb, CA_HEADS, CA_HEAD_DIM))
    return x, qh, ext[:, -POOL_BUF:], c1, n1, m1[:, :MLSTM_HEADS]


def kernel(x_prompt, x_sample, mem_prompt, cache_pool, state_mlstm_C, state_mlstm_n, state_mlstm_m,
           cache_mem_k, cache_mem_v, emb_ln_g, emb_ln_b, w_in, b_i, b_f, pool_w, pool_scale,
           mlstm_norm_g, w_out, ln1_g, ln1_b, ca_wq, ca_wk, ca_wv, ca_wo, ln2_g, ln2_b,
           w_gr, b_gr, w_er, b_er, w_gate, w_up, w_down, ln3_g, ln3_b):
    bp, tp, d = x_prompt.shape
    bs, ts, _ = x_sample.shape
    tm_p, tm_s = 512, bs * ts
    xp = x_prompt.reshape(bp * tp, d)
    xs = _ln(x_sample.reshape(bs * ts, d), emb_ln_g, emb_ln_b, tm_s)
    mk4, mv4, mk2, mv2 = _mem_proj(mem_prompt, ca_wk, ca_wv)
    outs = [[] for _ in range(7)]
    sc = None
    for l in range(DEPTH):
        p = _layer_weights(l, w_in, b_i, b_f, pool_w, pool_scale, mlstm_norm_g, w_out, ln1_g, ln1_b, ca_wq,
                           ca_wo, ln2_g, ln2_b, w_gr, b_gr, w_er, b_er, w_gate, w_up, w_down, ln3_g, ln3_b)
        if l == 0:
            p["first"] = (emb_ln_g, emb_ln_b, state_mlstm_C)
        if l > 0:
            u_s, og_s, step = _sample_inproj(xs, bs * ts, (state_mlstm_C, state_mlstm_n, state_mlstm_m[l], sc), p)
        xp, pb, c1, n1, m1 = _prompt_mixers(xp, bp, tp, l, (mk2, mv2), p, tm_p, 2 * tm_p, ride=step if l > 0 else None)
        if l == 0:
            sc = p["zeros"]
            u_s, og_s, step = _sample_inproj(xs, bs * ts, (state_mlstm_C, state_mlstm_n, state_mlstm_m[l], sc), p)
        xs, qh, sb, sc, ns, ms = _sample_mixers(xs, bs * ts, l, cache_pool[l], u_s, og_s, step, p)
        xp, ctx = _moe(xp, p["wr"], p["br"], *p["experts"], *p["ln3"], 2 * tm_p, attn=(qh, cache_mem_k, cache_mem_v, l))
        xs = _mm_res_ln(ctx[:, :CA_HEADS].reshape(bs * ts, d), xs, p["wo"], *p["ln2"], tm_s)
        xs = _moe(xs, p["wr"], p["br"], *p["experts"], *p["ln3"], tm_s)
        for lst, val in zip(outs, (pb, c1, n1, m1, sb, ns, ms)):
            lst.append(val)
    pp, pc, pn, pm, sp, sn, sm = (jnp.stack(o) for o in outs)
    return (xp.reshape(bp, tp, d), xs.reshape(bs, ts, d), pp, pc, pn, pm, mk4, mv4, sp, sc, sn, sm)
```

```python
import functools

import jax
import jax.numpy as jnp
from jax import lax
from jax.experimental import pallas as pl
from jax.experimental.pallas import tpu as pltpu

F32 = jnp.float32
BF16 = jnp.bfloat16

D_MODEL = 1024
DEPTH = 4
POOL_WIDTH = 512
POOL_GROUPS = 4
POOL_GROUP_DIM = 128
POOL_WINDOWS = (2, 4, 8, 16)
POOL_BUF = 15
MLSTM_WIDTH = 512
MLSTM_HEADS = 4
MLSTM_HEAD_DIM = 128
N_MEM = 256
CA_HEADS = 4
CA_HEAD_DIM = 256
N_GROUPS = 4
EXPERTS_PER_GROUP = 4
N_EXPERTS = 16
D_EXPERT = 256
ALPHA = (2 * DEPTH) ** 0.25
LN_EPS = 1e-5
GATE_OFF = POOL_WIDTH + 4 * MLSTM_WIDTH

LANES = 128
SUBLANES = 8
VMEM_LIMIT = 56 * 1024 * 1024
MOE_VMEM_LIMIT = 60 * 1024 * 1024
MLSTM_CHUNK = 256
POOL_ROWS = 2048
MLSTM_STEP_SEQS = 16
STEP_INPUTS = 11
POOL_HALO = 16
ROUTER_ROWS = 32
MOE_SEGMENT = 512
MOE_CAP = 160


def _params(*sem):
    return pltpu.CompilerParams(dimension_semantics=sem, vmem_limit_bytes=VMEM_LIMIT)


def _dot(a, b):
    return jnp.dot(a, b, preferred_element_type=F32)


def _dot_nt(a, b, precision=None):
    return lax.dot_general(a, b, (((1,), (1,)), ((), ())), precision=precision,
                           preferred_element_type=F32)


def _dot_tn(a, b):
    return lax.dot_general(a, b, (((0,), (0,)), ((), ())), preferred_element_type=F32)


def _layer_norm(x, g, b):
    mu = jnp.mean(x, -1, keepdims=True)
    xc = x - mu
    var = jnp.mean(xc * xc, -1, keepdims=True)
    return xc * lax.rsqrt(var + LN_EPS) * g + b


def _unit_norm(x):
    mu = jnp.mean(x, -1, keepdims=True)
    xc = x - mu
    var = jnp.mean(xc * xc, -1, keepdims=True)
    return xc * lax.rsqrt(var + LN_EPS)


def _log_sigmoid(x):
    return jnp.minimum(x, 0.0) - jnp.log1p(jnp.exp(-jnp.abs(x)))


def _scan(x, axis, op, fill):
    n = x.shape[axis]
    idx = lax.broadcasted_iota(jnp.int32, x.shape, axis)
    s = 1
    while s < n:
        x = op(x, jnp.where(idx >= s, pltpu.roll(x, s, axis), fill))
        s *= 2
    return x


def _ln_kernel(x_ref, g_ref, b_ref, o_ref):
    o_ref[...] = _layer_norm(x_ref[...], g_ref[...], b_ref[...])


def _ln(x, g, b, tm):
    n, d = x.shape
    row = pl.BlockSpec((tm, d), lambda i: (i, 0))
    vec = pl.BlockSpec((1, d), lambda i: (0, 0))
    return pl.pallas_call(
        _ln_kernel, grid=(n // tm,), in_specs=[row, vec, vec], out_specs=row,
        out_shape=jax.ShapeDtypeStruct((n, d), F32), compiler_params=_params("parallel"),
        name="emb_ln")(x, g.reshape(1, d), b.reshape(1, d))


def _qkvo(xb, w_ref, q_ref, k_ref, v_ref, og_ref):
    w = MLSTM_WIDTH
    q = _dot(xb, w_ref[:, POOL_WIDTH:POOL_WIDTH + w]) * (MLSTM_HEAD_DIM ** -0.5)
    q_ref[...] = q.astype(q_ref.dtype)
    k_ref[...] = _dot(xb, w_ref[:, POOL_WIDTH + w:POOL_WIDTH + 2 * w]).astype(k_ref.dtype)
    v_ref[...] = _dot(xb, w_ref[:, POOL_WIDTH + 2 * w:POOL_WIDTH + 3 * w]).astype(v_ref.dtype)
    og_ref[...] = _dot(xb, w_ref[:, POOL_WIDTH + 3 * w:POOL_WIDTH + 4 * w])


def _inproj_kernel(x_ref, w_ref, wg_ref, u_ref, q_ref, k_ref, v_ref, og_ref, gi_ref, gf_ref):
    xb = x_ref[...].astype(BF16)
    u_ref[...] = _dot(xb, w_ref[:, 0:POOL_WIDTH])
    _qkvo(xb, w_ref, q_ref, k_ref, v_ref, og_ref)
    g = _dot(xb, wg_ref[...])
    gi_ref[...] = g[:, 0:LANES]
    gf_ref[...] = g[:, LANES:2 * LANES]


def _inproj(x, w_main, wg, tm):
    n, d = x.shape
    row = lambda c: pl.BlockSpec((tm, c), lambda i: (i, 0))
    full = lambda a: pl.BlockSpec(a.shape, lambda i: (0, 0))
    widths = (POOL_WIDTH, MLSTM_WIDTH, MLSTM_WIDTH, MLSTM_WIDTH, MLSTM_WIDTH, LANES, LANES)
    return pl.pallas_call(
        _inproj_kernel, grid=(n // tm,), in_specs=[row(d), full(w_main), full(wg)],
        out_specs=tuple(row(c) for c in widths),
        out_shape=tuple(jax.ShapeDtypeStruct((n, c), F32) for c in widths),
        compiler_params=_params("parallel"), name="inproj")(x, w_main, wg)


def _inproj_seq_kernel(*refs, per, first, ride):
    n_in = 11 + (2 if first else 0) + (STEP_INPUTS if ride else 0)
    x_ref, w_ref, wgt_ref, pw_ref, sc_ref, wg_ref, wu_ref, wd_ref, wo_ref, wq_ref, wa_ref = refs[:11]
    outs, carry_ref = refs[n_in:-1], refs[-1]
    (po_ref, tail_ref, q_ref, k_ref, v_ref, og_ref, git_ref, gft_ref,
     cg_ref, cu_ref, cd_ref, co_ref, cq_ref, ca_ref) = outs[:14]
    tm = x_ref.shape[0]
    start = lax.rem(pl.program_id(0), per) * tm

    @pl.when(start == 0)
    def _():
        carry_ref[...] = jnp.zeros_like(carry_ref)

    for src, dst in ((wg_ref, cg_ref), (wu_ref, cu_ref), (wd_ref, cd_ref), (wo_ref, co_ref), (wq_ref, cq_ref),
                     (wa_ref, ca_ref)):
        dst[...] = src[...].astype(BF16)
    if ride:
        _mlstm_step_kernel(*refs[11:11 + STEP_INPUTS], *outs[14:18])
    x = x_ref[...]
    if first:
        x = _layer_norm(x, refs[11][...], refs[12][...])
        outs[14][...] = x
        outs[15][...] = jnp.zeros_like(outs[15])

    xb = x.astype(BF16)
    u = _dot(xb, w_ref[:, 0:POOL_WIDTH])
    halo = carry_ref.shape[0]
    ext = jnp.concatenate([carry_ref[...], u], axis=0)
    tail = ext[tm:, :]
    carry_ref[...] = tail
    tail_ref[...] = tail
    pos = start + lax.broadcasted_iota(jnp.int32, (tm, POOL_GROUP_DIM), 0)
    for g, win in enumerate(POOL_WINDOWS):
        sl = slice(g * POOL_GROUP_DIM, (g + 1) * POOL_GROUP_DIM)
        acc = ext[:, sl]
        s = 1
        while s < win:
            acc = acc + pltpu.roll(acc, s, 0)
            s *= 2
        cnt = jnp.minimum(pos + 1, win).astype(F32)
        dlt = acc[halo:, :] / cnt - u[:, sl]
        po_ref[:, sl] = (_dot(dlt.astype(BF16), pw_ref[g]) * sc_ref[:, sl]).astype(po_ref.dtype)
    _qkvo(xb, w_ref, q_ref, k_ref, v_ref, og_ref)
    gt = _dot_nt(wgt_ref[...], xb)
    git_ref[...] = gt[0:SUBLANES]
    gft_ref[...] = gt[SUBLANES:2 * SUBLANES]


def _inproj_seq(x, w_main, wgt, pw, scale, experts, dense, layer, seq, tm, first=None, ride=None):
    n, d = x.shape
    per = seq // tm
    steps = n // tm
    split = steps // N_EXPERTS
    assert split * N_EXPERTS == steps and d % steps == 0 and d % split == 0 and D_EXPERT % split == 0
    row = lambda c: pl.BlockSpec((tm, c), lambda i: (i, 0))
    full = lambda a: pl.BlockSpec(a.shape, lambda i: (0,) * a.ndim)
    gt_spec = pl.BlockSpec((None, SUBLANES, tm), lambda i: (i // per, 0, i % per))
    gt_shape = jax.ShapeDtypeStruct((n // seq, SUBLANES, seq), F32)
    mw, e, gf = MLSTM_WIDTH, EXPERTS_PER_GROUP, EXPERTS_PER_GROUP * D_EXPERT
    dr, fr, wr = d // split, D_EXPERT // split, d // steps
    col_in = pl.BlockSpec((None, None, dr, D_EXPERT), lambda i: (layer, i // split, i % split, 0))
    row_in = pl.BlockSpec((None, None, fr, d), lambda i: (layer, i // split, i % split, 0))
    col_out = pl.BlockSpec((None, dr, D_EXPERT), lambda i: (i // split // e, i % split, (i // split) % e))
    row_out = pl.BlockSpec((None, fr, d), lambda i: (i // split // e, ((i // split) % e) * split + i % split, 0))
    dense_in = pl.BlockSpec((None, wr, d), lambda i: (layer, i, 0))
    dense_out = pl.BlockSpec((wr, d), lambda i: (i, 0))
    out_shape = [
        jax.ShapeDtypeStruct((n, POOL_WIDTH), BF16),
        jax.ShapeDtypeStruct((steps, POOL_HALO, POOL_WIDTH), F32),
        jax.ShapeDtypeStruct((n, mw), BF16), jax.ShapeDtypeStruct((n, mw), BF16), jax.ShapeDtypeStruct((n, mw), BF16),
        jax.ShapeDtypeStruct((n, mw), F32), gt_shape, gt_shape,
        jax.ShapeDtypeStruct((N_GROUPS, d, gf), BF16), jax.ShapeDtypeStruct((N_GROUPS, d, gf), BF16),
        jax.ShapeDtypeStruct((N_GROUPS, gf, d), BF16),
        jax.ShapeDtypeStruct((d, d), BF16), jax.ShapeDtypeStruct((d, d), BF16), jax.ShapeDtypeStruct((d, d), BF16),
    ]
    out_specs = [row(POOL_WIDTH), pl.BlockSpec((None, POOL_HALO, POOL_WIDTH), lambda i: (i, 0, 0)),
                 row(mw), row(mw), row(mw), row(mw), gt_spec, gt_spec,
                 col_out, col_out, row_out, dense_out, dense_out, dense_out]
    extra_in, extra_args = [], []
    if first is not None:
        ln_g, ln_b, zeros_like = first
        lead = zeros_like.shape[0] * zeros_like.shape[1]
        assert lead % steps == 0 and zeros_like.shape[1] % (lead // steps) == 0
        zb = lead // steps
        zper = zeros_like.shape[1] // zb
        out_shape += [jax.ShapeDtypeStruct((n, d), F32), jax.ShapeDtypeStruct(zeros_like.shape, zeros_like.dtype)]
        out_specs += [row(d), pl.BlockSpec((None, zb) + zeros_like.shape[2:],
                                           lambda i: (i // zper, i % zper) + (0,) * (zeros_like.ndim - 2))]
        extra_args = [ln_g.reshape(1, d), ln_b.reshape(1, d)]
        extra_in = [full(a) for a in extra_args]
    aliases = {}
    if ride is not None:
        assert first is None
        rq, rk, rv, rgi, rgf, rbi, rbf, c_all, n_all, rm, c_new = ride
        n_s = rq.shape[0]
        sb = n_s // steps
        assert sb * steps == n_s
        hd = MLSTM_HEAD_DIM
        r3 = lambda a: a.reshape(steps, sb, a.shape[-1])
        s3 = lambda c: pl.BlockSpec((None, sb, c), lambda i: (i, 0, 0))
        cspec = pl.BlockSpec((None, sb, MLSTM_HEADS, hd, hd), lambda i: (layer, i, 0, 0, 0))
        nspec = pl.BlockSpec((None, sb, MLSTM_HEADS, hd), lambda i: (layer, i, 0, 0))
        extra_args = [r3(rq), r3(rk), r3(rv), r3(rgi), r3(rgf), rbi, rbf, c_all, n_all, r3(rm), c_new]
        assert len(extra_args) == STEP_INPUTS
        extra_in = [s3(mw), s3(mw), s3(mw), s3(LANES), s3(LANES), full(rbi), full(rbf), cspec, nspec, s3(LANES),
                    pl.BlockSpec(memory_space=pl.ANY)]
        aliases = {11 + STEP_INPUTS - 1: len(out_shape) + 1}
        out_shape += [jax.ShapeDtypeStruct((steps, sb, mw), F32), jax.ShapeDtypeStruct(c_new.shape, F32),
                      jax.ShapeDtypeStruct(n_all.shape[1:], F32), jax.ShapeDtypeStruct((steps, sb, LANES), F32)]
        out_specs += [s3(mw), cspec, pl.BlockSpec((sb, MLSTM_HEADS, hd), lambda i: (i, 0, 0)), s3(LANES)]
    scale = scale.reshape(1, POOL_WIDTH)
    return pl.pallas_call(
        functools.partial(_inproj_seq_kernel, per=per, first=first is not None, ride=ride is not None), grid=(steps,),
        in_specs=[row(d), full(w_main), full(wgt), full(pw), full(scale), col_in, col_in, row_in,
                  dense_in, dense_in, dense_in] + extra_in,
        out_specs=tuple(out_specs), out_shape=tuple(out_shape),
        scratch_shapes=[pltpu.VMEM((POOL_HALO, POOL_WIDTH), F32)], input_output_aliases=aliases,
        compiler_params=_params("arbitrary"), name="inproj_seq")(x, w_main, wgt, pw, scale, *experts, *dense,
                                                                 *extra_args)


def _pool_kernel(ext_ref, pw_ref, sc_ref, o_ref, *, period):
    rows = ext_ref.shape[0]
    r = lax.broadcasted_iota(jnp.int32, (rows, POOL_GROUP_DIM), 0)
    if period != rows:
        r = lax.rem(r, period)
    for g, win in enumerate(POOL_WINDOWS):
        sl = slice(g * POOL_GROUP_DIM, (g + 1) * POOL_GROUP_DIM)
        x = ext_ref[:, sl]
        acc = x
        s = 1
        while s < win:
            acc = acc + jnp.where(r >= s, pltpu.roll(acc, s, 0), 0.0)
            s *= 2
        cnt = jnp.minimum(r + 1, win).astype(F32)
        d = acc / cnt - x
        y = _dot(d.astype(BF16), pw_ref[g]) * sc_ref[:, sl]
        o_ref[:, sl] = y.astype(o_ref.dtype)


def _pool(ext2d, pw, scale, period):
    n, c = ext2d.shape
    row = pl.BlockSpec((POOL_ROWS, c), lambda i: (i, 0))
    return pl.pallas_call(
        functools.partial(_pool_kernel, period=period), grid=(n // POOL_ROWS,),
        in_specs=[row, pl.BlockSpec(pw.shape, lambda i: (0, 0, 0)), pl.BlockSpec((1, c), lambda i: (0, 0))],
        out_specs=row, out_shape=jax.ShapeDtypeStruct((n, c), BF16),
        compiler_params=_params("parallel"), name="pool")(ext2d, pw, scale.reshape(1, c))


def _mlstm_chunk_kernel(q_ref, k_ref, v_ref, git_ref, gft_ref, bit_ref, bft_ref, hn_ref, c_ref, n_ref, m_ref,
                        at_s, ws_s, dc_s, mc_s, wi_s, fl_s):
    nb, L, _ = q_ref.shape
    rows = nb * SUBLANES

    @pl.when(pl.program_id(0) == 0)
    def _():
        c_ref[...] = jnp.zeros_like(c_ref)
        n_ref[...] = jnp.zeros_like(n_ref)
        m_ref[...] = jnp.zeros_like(m_ref)

    ig = git_ref[...].reshape(rows, L) + bit_ref[...]
    lf = _log_sigmoid(gft_ref[...].reshape(rows, L) + bft_ref[...])
    tri = (lax.broadcasted_iota(jnp.int32, (L, L), 0) <= lax.broadcasted_iota(jnp.int32, (L, L), 1)).astype(BF16)
    lf_hi = lf.astype(BF16)
    lf_mid = (lf - lf_hi.astype(F32)).astype(BF16)
    lf_lo = (lf - lf_hi.astype(F32) - lf_mid.astype(F32)).astype(BF16)
    bc = _dot(lf_hi, tri) + (_dot(lf_mid, tri) + _dot(lf_lo, tri))
    at = ig - bc
    m_prev = m_ref[...].reshape(rows, LANES)[:, 0:1]
    mc = jnp.maximum(_scan(at, 1, jnp.maximum, -jnp.inf), m_prev)
    mt = bc + mc
    m_last = mc[:, L - 1:L]
    at_s[...] = at.reshape(nb, SUBLANES, L)
    ws_s[...] = jnp.exp(at - m_last).reshape(nb, SUBLANES, L)
    dc_s[...] = jnp.broadcast_to(jnp.exp(m_prev - m_last), (rows, LANES)).reshape(nb, SUBLANES, LANES)
    m_ref[...] = jnp.broadcast_to(mt[:, L - 1:L], (rows, LANES)).reshape(nb, SUBLANES, LANES)
    wi = jnp.exp(m_prev - mc)
    fl = jnp.exp(-mt)
    for b in range(nb):
        rs = slice(b * SUBLANES, (b + 1) * SUBLANES)
        mc_s[b] = mc[rs].T
        wi_s[b] = wi[rs].T
        fl_s[b] = fl[rs].T

    causal = (lax.broadcasted_iota(jnp.int32, (L, L), 0) >= lax.broadcasted_iota(jnp.int32, (L, L), 1))

    def per_batch(heads, b, carry):
        at = at_s[b]
        ws = ws_s[b]
        decay = dc_s[b]
        mc_c, wi_c, fl_c = mc_s[b], wi_s[b], fl_s[b]
        ws16 = jnp.concatenate([ws, ws], axis=0).astype(BF16)
        for h in heads:
            sl = slice(h * MLSTM_HEAD_DIM, (h + 1) * MLSTM_HEAD_DIM)
            col = slice(h, h + 1)
            qh = q_ref[b, :, sl]
            kh = k_ref[b, :, sl]
            vh = v_ref[b, :, sl]
            s = _dot_nt(qh, kh)
            p = jnp.where(causal, s * jnp.exp(at[col, :] - mc_c[:, col]), 0.0)
            c_old = c_ref[b, h]
            n_old = n_ref[b, col, :]
            wi = wi_c[:, col]
            dv = MLSTM_HEAD_DIM
            v_ext = jnp.concatenate([vh, jnp.ones_like(vh)], axis=1)
            c_ext = jnp.concatenate([c_old, jnp.broadcast_to(n_old, c_old.shape)], axis=0)
            intra = _dot(p.astype(BF16), v_ext)
            inter = _dot_nt(qh, c_ext.astype(BF16))
            num = intra[:, :dv] + wi * inter[:, :dv]
            qn = intra[:, dv:] + wi * inter[:, dv:]
            hh = num / jnp.maximum(jnp.abs(qn), fl_c[:, col])
            hn_ref[b, :, sl] = _unit_norm(hh)
            dc = decay[col, 0:1]
            vts = (vh.astype(F32).T * ws[col, :]).astype(BF16)
            c_ref[b, h] = dc * c_old + _dot(vts, kh)
            n_ref[b, col, :] = dc * n_old + _dot(ws16, kh)[col, :]
        return carry

    for h0 in range(0, MLSTM_HEADS, 2):
        lax.fori_loop(0, nb, functools.partial(per_batch, (h0, h0 + 1)), 0)


def _mlstm_chunk(q, k, v, git, gft, bit, bft):
    nb, t, w = q.shape
    L = MLSTM_CHUNK
    seq = lambda c: pl.BlockSpec((nb, L, c), lambda i: (0, i, 0))
    seqt = pl.BlockSpec((nb, SUBLANES, L), lambda i: (0, 0, i))
    vec = lambda a: pl.BlockSpec(a.shape, lambda i: (0, 0))
    out_shape = (
        jax.ShapeDtypeStruct((nb, t, w), F32),
        jax.ShapeDtypeStruct((nb, MLSTM_HEADS, MLSTM_HEAD_DIM, MLSTM_HEAD_DIM), F32),
        jax.ShapeDtypeStruct((nb, MLSTM_HEADS, MLSTM_HEAD_DIM), F32),
        jax.ShapeDtypeStruct((nb, SUBLANES, LANES), F32),
    )
    out_specs = (
        seq(w),
        pl.BlockSpec(out_shape[1].shape, lambda i: (0, 0, 0, 0)),
        pl.BlockSpec(out_shape[2].shape, lambda i: (0, 0, 0)),
        pl.BlockSpec(out_shape[3].shape, lambda i: (0, 0, 0)),
    )
    bit, bft = jnp.tile(bit, (nb, 1)), jnp.tile(bft, (nb, 1))
    scratch = [pltpu.VMEM((nb, SUBLANES, L), F32), pltpu.VMEM((nb, SUBLANES, L), F32),
               pltpu.VMEM((nb, SUBLANES, LANES), F32)] + [pltpu.VMEM((nb, L, SUBLANES), F32) for _ in range(3)]
    return pl.pallas_call(
        _mlstm_chunk_kernel, grid=(t // L,),
        in_specs=[seq(w), seq(w), seq(w), seqt, seqt, vec(bit), vec(bft)],
        out_specs=out_specs, out_shape=out_shape, scratch_shapes=scratch, compiler_params=_params("arbitrary"),
        name="mlstm_chunk")(q, k, v, git, gft, bit, bft)


def _mlstm_step_kernel(q_ref, k_ref, v_ref, gi_ref, gf_ref, bi_ref, bf_ref, c_ref, n_ref, m_ref, c_new_ref,
                       hn_ref, co_ref, no_ref, mo_ref):
    del c_new_ref
    tb = q_ref.shape[0]
    d = MLSTM_HEAD_DIM
    ig = gi_ref[...] + bi_ref[...]
    lf = _log_sigmoid(gf_ref[...] + bf_ref[...])
    m_old = m_ref[...]
    mt = jnp.maximum(lf + m_old, ig)
    mo_ref[...] = mt
    wa = jnp.exp(ig - mt)
    wi = jnp.exp(lf + m_old - mt)
    fl = jnp.exp(-mt)
    wa_t, wi_t, fl_t = wa.T, wi.T, fl.T
    lane = lax.broadcasted_iota(jnp.int32, (d, tb), 1)
    for h in range(MLSTM_HEADS):
        sl = slice(h * d, (h + 1) * d)
        qh, kh, vh = q_ref[:, sl], k_ref[:, sl], v_ref[:, sl]
        nh = n_ref[:, h, :]
        qt, kt, vt, nt = qh.T, kh.T, vh.T, nh.T
        wa_r, wi_r, fl_r = wa_t[h:h + 1, :], wi_t[h:h + 1, :], fl_t[h:h + 1, :]
        s = jnp.sum(qt * kt, 0, keepdims=True) * wa_r
        cq = jnp.zeros((d, tb), F32)
        qtb = qt.astype(BF16)
        for b in range(tb):
            cq = jnp.where(lane == b, _dot(c_ref[b, h].astype(BF16), qtb), cq)
        num = s * vt + wi_r * cq
        qn = s + wi_r * jnp.sum(nt * qt, 0, keepdims=True)
        hh = num / jnp.maximum(jnp.abs(qn), fl_r)
        mu = jnp.mean(hh, 0, keepdims=True)
        xc = hh - mu
        var = jnp.mean(xc * xc, 0, keepdims=True)
        hn_ref[:, sl] = (xc * lax.rsqrt(var + LN_EPS)).T
        wav = wa_r * vt
        for b in range(tb):
            co_ref[b, h] = wi_r[:, b:b + 1] * c_ref[b, h] + wav[:, b:b + 1] * kh[b:b + 1, :]
        no_ref[:, h, :] = wi[:, h:h + 1] * nh + wa[:, h:h + 1] * kh


def _mlstm_step(q, k, v, gi, gf, bi, bf, c_all, n_all, m0, c_new, layer, tb):
    n, w = q.shape
    d = MLSTM_HEAD_DIM
    row = lambda c: pl.BlockSpec((tb, c), lambda i: (i, 0))
    vec = lambda a: pl.BlockSpec(a.shape, lambda i: (0, 0))
    cin = pl.BlockSpec((None, tb, MLSTM_HEADS, d, d), lambda i: (layer, i, 0, 0, 0))
    nin = pl.BlockSpec((None, tb, MLSTM_HEADS, d), lambda i: (layer, i, 0, 0))
    nout = pl.BlockSpec((tb, MLSTM_HEADS, d), lambda i: (i, 0, 0))
    out_shape = (
        jax.ShapeDtypeStruct((n, w), F32),
        jax.ShapeDtypeStruct(c_new.shape, F32),
        jax.ShapeDtypeStruct(n_all.shape[1:], F32),
        jax.ShapeDtypeStruct((n, LANES), F32),
    )
    return pl.pallas_call(
        _mlstm_step_kernel, grid=(n // tb,),
        in_specs=[row(w), row(w), row(w), row(LANES), row(LANES), vec(bi), vec(bf), cin, nin, row(LANES),
                  pl.BlockSpec(memory_space=pl.ANY)],
        out_specs=(row(w), cin, nout, row(LANES)), out_shape=out_shape, input_output_aliases={10: 1},
        compiler_params=_params("parallel"), name="mlstm_step")(q, k, v, gi, gf, bi, bf, c_all, n_all, m0, c_new)


def _mixout_kernel(po_ref, hn_ref, og_ref, x_ref, ng_ref, w_ref, g_ref, b_ref, o_ref):
    mo = jax.nn.sigmoid(og_ref[...]) * hn_ref[...] * ng_ref[...]
    mix = _dot(po_ref[...], w_ref[0:POOL_WIDTH, :]) + _dot(mo.astype(BF16), w_ref[POOL_WIDTH:, :])
    o_ref[...] = _layer_norm(ALPHA * x_ref[...] + mix, g_ref[...], b_ref[...])


def _mixout(po, hn, og, x, ng, w, g, b, tm):
    n, d = x.shape
    row = lambda c: pl.BlockSpec((tm, c), lambda i: (i, 0))
    vec = lambda a: pl.BlockSpec(a.shape, lambda i: (0, 0))
    return pl.pallas_call(
        _mixout_kernel, grid=(n // tm,),
        in_specs=[row(POOL_WIDTH), row(MLSTM_WIDTH), row(MLSTM_WIDTH), row(d), vec(ng), vec(w), vec(g), vec(b)],
        out_specs=row(d), out_shape=jax.ShapeDtypeStruct((n, d), F32),
        compiler_params=_params("parallel"), name="mixout")(po, hn, og, x, ng, w, g, b)


def _mm_kernel(x_ref, w_ref, o_ref, *, scale):
    y = _dot(x_ref[...].astype(BF16), w_ref[...])
    if scale != 1.0:
        y = y * scale
    o_ref[...] = y.astype(o_ref.dtype)


def _mm(x, w, tm, out_dtype, scale=1.0, name="proj"):
    n, d = x.shape
    dout = w.shape[1]
    return pl.pallas_call(
        functools.partial(_mm_kernel, scale=scale), grid=(n // tm,),
        in_specs=[pl.BlockSpec((tm, d), lambda i: (i, 0)), pl.BlockSpec(w.shape, lambda i: (0, 0))],
        out_specs=pl.BlockSpec((tm, dout), lambda i: (i, 0)),
        out_shape=jax.ShapeDtypeStruct((n, dout), out_dtype),
        compiler_params=_params("parallel"), name=name)(x, w)


def _cast_inproj_kernel(wt_ref, o_ref, og_ref, *, n_main):
    i = pl.program_id(0)

    @pl.when(i < n_main)
    def _():
        o_ref[...] = wt_ref[...].T.astype(BF16)

    @pl.when(i == n_main)
    def _():
        og_ref[...] = wt_ref[0:2 * MLSTM_HEADS, :]


def _cast_inproj(w_in, layer, tc=256):
    wt = jnp.swapaxes(w_in, 1, 2)
    d = wt.shape[2]
    n_main = GATE_OFF // tc
    return pl.pallas_call(
        functools.partial(_cast_inproj_kernel, n_main=n_main), grid=(n_main + 1,),
        in_specs=[pl.BlockSpec((None, tc, d), lambda i: (layer, i, 0))],
        out_specs=(pl.BlockSpec((d, tc), lambda i: (0, jnp.minimum(i, n_main - 1))),
                   pl.BlockSpec((2 * MLSTM_HEADS, d), lambda i: (0, 0))),
        out_shape=(jax.ShapeDtypeStruct((d, GATE_OFF), BF16), jax.ShapeDtypeStruct((2 * MLSTM_HEADS, d), F32)),
        compiler_params=_params("arbitrary"), name="inproj_cast")(wt)


def _mem_proj_kernel(x_ref, wk_ref, wv_ref, k4_ref, v4_ref, k2_ref, v2_ref, wkb, wvb):
    @pl.when(pl.program_id(1) == 0)
    def _():
        wkb[...] = wk_ref[...].astype(BF16)
        wvb[...] = wv_ref[...].astype(BF16)

    bb, m, d = x_ref.shape
    xb = x_ref[...].reshape(bb * m, d).astype(BF16)
    k = _dot(xb, wkb[...])
    v = _dot(xb, wvb[...])
    k2_ref[...] = k.reshape(bb, m, d)
    v2_ref[...] = v.reshape(bb, m, d)
    k4_ref[...] = k.reshape(bb, m, CA_HEADS, CA_HEAD_DIM)
    v4_ref[...] = v.reshape(bb, m, CA_HEADS, CA_HEAD_DIM)


def _mem_proj(mem, wk_all, wv_all, bb=2):
    nb, m, d = mem.shape
    depth = wk_all.shape[0]
    wspec = pl.BlockSpec((None, d, d), lambda l, b: (l, 0, 0))
    o4 = pl.BlockSpec((None, bb, m, CA_HEADS, CA_HEAD_DIM), lambda l, b: (l, b, 0, 0, 0))
    o2 = pl.BlockSpec((None, bb, m, d), lambda l, b: (l, b, 0, 0))
    s4 = jax.ShapeDtypeStruct((depth, nb, m, CA_HEADS, CA_HEAD_DIM), F32)
    s2 = jax.ShapeDtypeStruct((depth, nb, m, d), F32)
    return pl.pallas_call(
        _mem_proj_kernel, grid=(depth, nb // bb),
        in_specs=[pl.BlockSpec((bb, m, d), lambda l, b: (b, 0, 0)), wspec, wspec],
        out_specs=(o4, o4, o2, o2), out_shape=(s4, s4, s2, s2),
        scratch_shapes=[pltpu.VMEM((d, d), BF16), pltpu.VMEM((d, d), BF16)],
        compiler_params=_params("arbitrary", "arbitrary"), name="mem_proj")(mem, wk_all, wv_all)


def _attn_block_kernel(po_ref, hn_ref, og_ref, x_ref, k_ref, v_ref, ng_ref, wm_ref, g1_ref, b1_ref,
                       wq_ref, wo_ref, g_ref, b_ref, o_ref):
    mo = jax.nn.sigmoid(og_ref[...]) * hn_ref[...] * ng_ref[...]
    mix = _dot(po_ref[...], wm_ref[0:POOL_WIDTH, :]) + _dot(mo.astype(BF16), wm_ref[POOL_WIDTH:, :])
    x = _layer_norm(ALPHA * x_ref[...] + mix, g1_ref[...], b1_ref[...])
    qb = (_dot(x.astype(BF16), wq_ref[...]) * (CA_HEAD_DIM ** -0.5)).astype(BF16)
    kb = k_ref[...].astype(BF16)
    vb = v_ref[...].astype(BF16)
    ctx = []
    for h in range(CA_HEADS):
        sl = slice(h * CA_HEAD_DIM, (h + 1) * CA_HEAD_DIM)
        s = _dot_nt(qb[:, sl], kb[:, sl])
        e = jnp.exp(s - jnp.max(s, -1, keepdims=True))
        ctx.append((_dot(e.astype(BF16), vb[:, sl]) / jnp.sum(e, -1, keepdims=True)).astype(BF16))
    y = _dot(jnp.concatenate(ctx, axis=1), wo_ref[...])
    o_ref[...] = _layer_norm(ALPHA * x + y, g_ref[...], b_ref[...])


def _attn_block(po, hn, og, x, k, v, layer, ng, wm, ln1, wq, wo, ln2, seq, tq):
    n, d = x.shape
    per = seq // tq
    row = lambda c: pl.BlockSpec((tq, c), lambda i: (i, 0))
    kspec = pl.BlockSpec((None, None, N_MEM, d), lambda i: (layer, i // per, 0, 0))
    vec = lambda a: pl.BlockSpec(a.shape, lambda i: (0, 0))
    return pl.pallas_call(
        _attn_block_kernel, grid=(n // tq,),
        in_specs=[row(POOL_WIDTH), row(MLSTM_WIDTH), row(MLSTM_WIDTH), row(d), kspec, kspec, vec(ng), vec(wm),
                  vec(ln1[0]), vec(ln1[1]), vec(wq), vec(wo), vec(ln2[0]), vec(ln2[1])],
        out_specs=row(d), out_shape=jax.ShapeDtypeStruct((n, d), F32),
        compiler_params=_params("parallel"), name="attn_block")(po, hn, og, x, k, v, ng, wm, *ln1, wq, wo, *ln2)


def _attn_step_kernel(q_ref, k_ref, v_ref, o_ref):
    bb = q_ref.shape[0]
    rows = N_MEM * CA_HEADS
    lane = lax.broadcasted_iota(jnp.int32, (SUBLANES, rows), 1)
    row = lax.broadcasted_iota(jnp.int32, (SUBLANES, rows), 0)
    own = lax.rem(lane, CA_HEADS) == lax.rem(row, CA_HEADS)
    for j in range(bb):
        kf = k_ref[j].reshape(rows, CA_HEAD_DIM).astype(BF16)
        vf = v_ref[j].reshape(rows, CA_HEAD_DIM).astype(BF16)
        s = jnp.where(own, _dot_nt(q_ref[j].astype(BF16), kf), -jnp.inf)
        e = jnp.exp(s - jnp.max(s, -1, keepdims=True))
        o_ref[j] = _dot(e.astype(BF16), vf) / jnp.sum(e, -1, keepdims=True)


def _mm_res_ln_kernel(a_ref, x_ref, w_ref, g_ref, b_ref, o_ref):
    y = _dot(a_ref[...].astype(BF16), w_ref[...])
    o_ref[...] = _layer_norm(ALPHA * x_ref[...] + y, g_ref[...], b_ref[...])


def _mm_res_ln(a, x, w, g, b, tm):
    n, d = x.shape
    row = lambda c: pl.BlockSpec((tm, c), lambda i: (i, 0))
    vec = lambda arr: pl.BlockSpec(arr.shape, lambda i: (0, 0))
    return pl.pallas_call(
        _mm_res_ln_kernel, grid=(n // tm,),
        in_specs=[row(a.shape[1]), row(d), vec(w), vec(g), vec(b)], out_specs=row(d),
        out_shape=jax.ShapeDtypeStruct((n, d), F32),
        compiler_params=_params("parallel"), name="proj_res_ln")(a, x, w, g, b)


def _route(lt):
    gl = [lt[g:g + 1, :] for g in range(N_GROUPS)]
    gmax = functools.reduce(jnp.maximum, gl)
    gsum = functools.reduce(jnp.add, [jnp.exp(x - gmax) for x in gl])
    pg_sel = 1.0 / gsum

    def first_max(vals):
        m = functools.reduce(jnp.maximum, vals)
        taken = jnp.zeros_like(m, dtype=jnp.bool_)
        hot = []
        for x in vals:
            h = jnp.logical_and(x == m, jnp.logical_not(taken))
            taken = jnp.logical_or(taken, h)
            hot.append(h)
        return m, hot

    _, g_hot = first_max(gl)
    el = []
    for j in range(EXPERTS_PER_GROUP):
        rows = [lt[SUBLANES + g * EXPERTS_PER_GROUP + j:SUBLANES + g * EXPERTS_PER_GROUP + j + 1, :]
                for g in range(N_GROUPS)]
        x = rows[N_GROUPS - 1]
        for g in range(N_GROUPS - 2, -1, -1):
            x = jnp.where(g_hot[g], rows[g], x)
        el.append(x)
    emax = functools.reduce(jnp.maximum, el)
    ee = [jnp.exp(x - emax) for x in el]
    esum = functools.reduce(jnp.add, ee)
    pe = [x / esum for x in ee]
    p1, hot1 = first_max(pe)
    p2, hot2 = first_max([jnp.where(h, -jnp.inf, x) for h, x in zip(hot1, pe)])
    psum = p1 + p2
    gate = [jnp.where(h1, pg_sel * p1 / psum, jnp.where(h2, pg_sel * p2 / psum, 0.0)) for h1, h2 in zip(hot1, hot2)]
    return g_hot, gate


def _moe_kernel(x_ref, wr_ref, br_ref, tri_ref, wg_ref, wu_ref, wd_ref, g_ref, b_ref, *rest):
    ride = None
    if len(rest) > 6:
        ride = functools.partial(_attn_step_kernel, rest[0], rest[1], rest[2], rest[4])
        rest = rest[3:4] + rest[5:]
    o_ref, xb_ref, ct_ref, comb_ref, perm_ref, full_ref = rest
    acc_ref = o_ref
    wg_ref, wu_ref, wd_ref = (r.at[pl.program_id(1)] for r in (wg_ref, wu_ref, wd_ref))
    grp = pl.program_id(1)
    tm = x_ref.shape[0]
    ns, cap, sw = perm_ref.shape[1:]
    f = D_EXPERT

    @pl.when(grp == 0)
    def _():
        x = x_ref[...]
        xh = x.astype(BF16)
        xb_ref[...] = xh
        xl = (x - xh.astype(F32)).astype(BF16)
        wr = wr_ref[...]
        wh = wr.astype(BF16)
        wl = (wr - wh.astype(F32)).astype(BF16)
        lt = _dot_nt(wh, xh) + (_dot_nt(wl, xh) + _dot_nt(wh, xl)) + br_ref[...]
        g_hot, gate = _route(lt)
        row8 = lax.broadcasted_iota(jnp.int32, (SUBLANES, tm), 0)
        hot8 = jnp.zeros((SUBLANES, tm), F32)
        for g in range(N_GROUPS):
            hot8 = jnp.where(jnp.logical_and(row8 == g, g_hot[g]), 1.0, hot8)
        most = None
        for s in range(ns):
            hs = hot8[:, s * sw:(s + 1) * sw]
            cum = _dot(hs.astype(BF16), tri_ref[...])
            seg_max = jnp.max(cum[:, sw - 1:sw])
            most = seg_max if most is None else jnp.maximum(most, seg_max)
            base = lax.broadcasted_iota(jnp.int32, (SUBLANES, sw), 0).astype(F32) * cap
            code = jnp.sum(hs * (base + cum - 1.0), 0, keepdims=True).astype(jnp.int32)
            for g in range(N_GROUPS):
                r = lax.broadcasted_iota(jnp.int32, (cap, sw), 0) + g * cap
                perm_ref[g, s] = jnp.where(r == code, 1.0, 0.0).astype(BF16)
        full_ref[0] = (most > cap).astype(jnp.int32)
        ct_ref[...] = jnp.zeros_like(ct_ref)
        for j in range(EXPERTS_PER_GROUP):
            ct_ref[j:j + 1, :] = gate[j]
        ct_ref[EXPERTS_PER_GROUP:EXPERTS_PER_GROUP + 1, :] = jnp.sum(hot8 * row8.astype(F32), 0, keepdims=True)
        comb_ref[...] = ct_ref[...].T
        acc_ref[...] = jnp.zeros_like(acc_ref)

    def experts(rows, gates):
        hh = []
        for j in range(EXPERTS_PER_GROUP):
            hg = _dot(rows, wg_ref[:, j * f:(j + 1) * f])
            hu = _dot(rows, wu_ref[:, j * f:(j + 1) * f])
            hh.append((hg * jax.nn.sigmoid(hg) * hu * gates[:, j:j + 1]).astype(BF16))
        return _dot(jnp.concatenate(hh, axis=1), wd_ref[...])

    @pl.when(full_ref[0] == 0)
    def _():
        comb = comb_ref[...]
        ch = comb.astype(BF16)
        cl = (comb - ch.astype(F32)).astype(BF16)
        xb = xb_ref[...]
        perm = [perm_ref[grp, s] for s in range(ns)]
        seg = lambda a, s: a[s * sw:(s + 1) * sw]
        rows = jnp.concatenate([_dot(perm[s], seg(xb, s)) for s in range(ns)], axis=0).astype(BF16)
        gates = jnp.concatenate([_dot(perm[s], seg(ch, s)) + _dot(perm[s], seg(cl, s)) for s in range(ns)], axis=0)
        y = experts(rows, gates).astype(BF16)
        for s in range(ns):
            acc_ref[s * sw:(s + 1) * sw, :] += _dot_tn(perm[s], y[s * cap:(s + 1) * cap])
        if ride is not None:
            ride()

    @pl.when(full_ref[0] != 0)
    def _():
        comb = comb_ref[...]
        own = comb[:, EXPERTS_PER_GROUP:EXPERTS_PER_GROUP + 1] == grp.astype(F32)
        acc_ref[...] += experts(xb_ref[...], jnp.where(own, comb, 0.0))
        if ride is not None:
            ride()

    @pl.when(grp == N_GROUPS - 1)
    def _():
        o_ref[...] = _layer_norm(ALPHA * x_ref[...] + acc_ref[...], g_ref[...], b_ref[...])


def _moe(x, wr, br, wg, wu, wd, g, b, tm, attn=None):
    n, d = x.shape
    ns = max(1, tm // MOE_SEGMENT)
    sw = tm // ns
    cap = min(sw, MOE_CAP)
    tri = (lax.broadcasted_iota(jnp.int32, (sw, sw), 0) <= lax.broadcasted_iota(jnp.int32, (sw, sw), 1)).astype(BF16)
    row = pl.BlockSpec((tm, d), lambda i, e: (i, 0))
    vec = lambda a: pl.BlockSpec(a.shape, lambda i, e: (0, 0))
    grp = lambda a: pl.BlockSpec(a.shape, lambda i, e: (0, 0, 0), pipeline_mode=pl.Buffered(1))
    scratch = [pltpu.VMEM((tm, d), BF16), pltpu.VMEM((LANES, tm), F32), pltpu.VMEM((tm, LANES), F32),
               pltpu.VMEM((N_GROUPS, ns, cap, sw), BF16), pltpu.SMEM((1,), jnp.int32)]
    in_specs = [row, vec(wr), vec(br), vec(tri), grp(wg), grp(wu), grp(wd), vec(g), vec(b)]
    args = [x, wr, br, tri, wg, wu, wd, g, b]
    out_specs, out_shape = row, jax.ShapeDtypeStruct((n, d), F32)
    if attn is not None:
        q, k_all, v_all, layer = attn
        steps = (n // tm) * N_GROUPS
        bb = q.shape[0] // steps
        assert bb * steps == q.shape[0]
        qspec = pl.BlockSpec((bb, SUBLANES, CA_HEAD_DIM), lambda i, e: (i * N_GROUPS + e, 0, 0))
        kspec = pl.BlockSpec((None, bb, N_MEM, CA_HEADS, CA_HEAD_DIM), lambda i, e: (layer, i * N_GROUPS + e, 0, 0, 0))
        in_specs += [qspec, kspec, kspec]
        args += [q, k_all, v_all]
        out_specs, out_shape = (row, qspec), (out_shape, jax.ShapeDtypeStruct(q.shape, F32))
    return pl.pallas_call(
        _moe_kernel, grid=(n // tm, N_GROUPS), in_specs=in_specs, out_specs=out_specs, out_shape=out_shape,
        scratch_shapes=scratch, name="moe", compiler_params=pltpu.CompilerParams(
            dimension_semantics=("arbitrary", "arbitrary"), vmem_limit_bytes=MOE_VMEM_LIMIT))(*args)


def _pad_lanes(v):
    return jnp.zeros((1, LANES), F32).at[0, :v.shape[0]].set(v)


def _pad_rows(v):
    return jnp.zeros((SUBLANES, 1), F32).at[:v.shape[0], 0].set(v)


def _layer_weights(l, w_in, b_i, b_f, pool_w, pool_scale, mlstm_norm_g, w_out, ln1_g, ln1_b, ca_wq, ca_wo,
                   ln2_g, ln2_b, w_gr, b_gr, w_er, b_er, w_gate, w_up, w_down, ln3_g, ln3_b):
    d = D_MODEL
    w_main, w_gate_rows = _cast_inproj(w_in, l)
    w_gate_cols = w_gate_rows.T
    wg = jnp.zeros((d, 2 * LANES), F32)
    wg = wg.at[:, 0:MLSTM_HEADS].set(w_gate_cols[:, :MLSTM_HEADS])
    wg = wg.at[:, LANES:LANES + MLSTM_HEADS].set(w_gate_cols[:, MLSTM_HEADS:])
    wgt = jnp.zeros((2 * SUBLANES, d), F32)
    wgt = wgt.at[0:MLSTM_HEADS].set(w_gate_cols[:, :MLSTM_HEADS].T)
    wgt = wgt.at[SUBLANES:SUBLANES + MLSTM_HEADS].set(w_gate_cols[:, MLSTM_HEADS:].T)
    wr = jnp.zeros((ROUTER_ROWS, d), F32)
    wr = wr.at[0:N_GROUPS].set(w_gr[l].T).at[SUBLANES:SUBLANES + N_EXPERTS].set(w_er[l].T)
    br = jnp.zeros((ROUTER_ROWS, 1), F32)
    br = br.at[0:N_GROUPS, 0].set(b_gr[l]).at[SUBLANES:SUBLANES + N_EXPERTS, 0].set(b_er[l])
    row = lambda v: v.reshape(1, -1)
    return dict(
        w_main=w_main, wg=wg.astype(BF16), wgt=wgt.astype(BF16),
        bi=_pad_lanes(b_i[l]), bf=_pad_lanes(b_f[l]), bit=_pad_rows(b_i[l]), bft=_pad_rows(b_f[l]),
        pool_w=pool_w[l].astype(BF16), pool_scale=pool_scale[l], norm_g=row(mlstm_norm_g[l]),
        ln1=(row(ln1_g[l]), row(ln1_b[l])), ln2=(row(ln2_g[l]), row(ln2_b[l])), ln3=(row(ln3_g[l]), row(ln3_b[l])),
        wr=wr, br=br, raw_experts=(w_gate, w_up, w_down), raw_dense=(w_out, ca_wq, ca_wo))


def _prompt_mixers(x, nb, seq, layer, mem, p, tm, tm_big, ride=None):
    per = seq // tm
    outs = _inproj_seq(x, p["w_main"], p["wgt"], p["pool_w"], p["pool_scale"], p["raw_experts"], p["raw_dense"],
                       layer, seq, tm, first=p.get("first"), ride=ride)
    if ride is not None:
        hn_s, c_s, n_s, m_s = outs[14:18]
        p["step"] = (hn_s.reshape(-1, MLSTM_WIDTH), c_s, n_s, m_s.reshape(-1, LANES))
    pooled, tails, q, k, v, og, git, gft = outs[:8]
    p["experts"] = outs[8:11]
    p["w_out"], p["wq"], p["wo"] = outs[11:14]
    if p.get("first") is not None:
        x, p["zeros"] = outs[14:16]
    pool_buf = tails.reshape(nb, per, POOL_HALO, POOL_WIDTH)[:, per - 1, POOL_HALO - POOL_BUF:]
    r3 = lambda a: a.reshape(nb, seq, a.shape[-1])
    hn, c1, n1, m1 = _mlstm_chunk(r3(q), r3(k), r3(v), git, gft, p["bit"], p["bft"])
    x = _attn_block(pooled, hn.reshape(nb * seq, MLSTM_WIDTH), og, x, mem[0], mem[1], layer, p["norm_g"], p["w_out"],
                    p["ln1"], p["wq"], p["wo"], p["ln2"], seq, tm_big)
    return x, pool_buf, c1, n1, m1[:, :MLSTM_HEADS, 0]


def _sample_inproj(x, nb, state, p):
    u, q, k, v, og, gi, gf = _inproj(x, p["w_main"], p["wg"], nb)
    c_all, n_all, m0, c_new = state
    m0p = jnp.zeros((nb, LANES), F32).at[:, :MLSTM_HEADS].set(m0)
    return u, og, (q, k, v, gi, gf, p["bi"], p["bf"], c_all, n_all, m0p, c_new)


def _sample_mixers(x, nb, layer, pool_prev, u, og, step, p):
    ext = jnp.concatenate([pool_prev, u.reshape(nb, 1, POOL_WIDTH)], axis=1)
    period = ext.shape[1]
    pooled = _pool(ext.reshape(nb * period, POOL_WIDTH), p["pool_w"], p["pool_scale"], period)
    pooled = pooled.reshape(nb, period, POOL_WIDTH)[:, period - 1]
    hn, c1, n1, m1 = p["step"] if "step" in p else _mlstm_step(*step, layer, MLSTM_STEP_SEQS)
    x = _mixout(pooled, hn, og, x, p["norm_g"], p["w_out"], *p["ln1"], nb)
    qc = _mm(x, p["wq"], nb, F32, scale=CA_HEAD_DIM ** -0.5, name="ca_q")
    qh = jnp.zeros((nb, SUBLANES, CA_HEAD_DIM), F32).at[:, :CA_HEADS].set(qc.reshape(nb, CA_HEADS, CA_HEAD_DIM))
    return x, qh, ext[:, -POOL_BUF:], c1, n1, m1[:, :MLSTM_HEADS]


def kernel(x_prompt, x_sample, mem_prompt, cache_pool, state_mlstm_C, state_mlstm_n, state_mlstm_m,
           cache_mem_k, cache_mem_v, emb_ln_g, emb_ln_b, w_in, b_i, b_f, pool_w, pool_scale,
           mlstm_norm_g, w_out, ln1_g, ln1_b, ca_wq, ca_wk, ca_wv, ca_wo, ln2_g, ln2_b,
           w_gr, b_gr, w_er, b_er, w_gate, w_up, w_down, ln3_g, ln3_b):
    bp, tp, d = x_prompt.shape
    bs, ts, _ = x_sample.shape
    tm_p, tm_s = 512, bs * ts
    xp = x_prompt.reshape(bp * tp, d)
    xs = _ln(x_sample.reshape(bs * ts, d), emb_ln_g, emb_ln_b, tm_s)
    mk4, mv4, mk2, mv2 = _mem_proj(mem_prompt, ca_wk, ca_wv)
    outs = [[] for _ in range(7)]
    sc = None
    for l in range(DEPTH):
        p = _layer_weights(l, w_in, b_i, b_f, pool_w, pool_scale, mlstm_norm_g, w_out, ln1_g, ln1_b, ca_wq,
                           ca_wo, ln2_g, ln2_b, w_gr, b_gr, w_er, b_er, w_gate, w_up, w_down, ln3_g, ln3_b)
        if l == 0:
            p["first"] = (emb_ln_g, emb_ln_b, state_mlstm_C)
        if l > 0:
            u_s, og_s, step = _sample_inproj(xs, bs * ts, (state_mlstm_C, state_mlstm_n, state_mlstm_m[l], sc), p)
        xp, pb, c1, n1, m1 = _prompt_mixers(xp, bp, tp, l, (mk2, mv2), p, tm_p, 2 * tm_p, ride=step if l > 0 else None)
        if l == 0:
            sc = p["zeros"]
            u_s, og_s, step = _sample_inproj(xs, bs * ts, (state_mlstm_C, state_mlstm_n, state_mlstm_m[l], sc), p)
        xs, qh, sb, sc, ns, ms = _sample_mixers(xs, bs * ts, l, cache_pool[l], u_s, og_s, step, p)
        xp, ctx = _moe(xp, p["wr"], p["br"], *p["experts"], *p["ln3"], 2 * tm_p, attn=(qh, cache_mem_k, cache_mem_v, l))
        xs = _mm_res_ln(ctx[:, :CA_HEADS].reshape(bs * ts, d), xs, p["wo"], *p["ln2"], tm_s)
        xs = _moe(xs, p["wr"], p["br"], *p["experts"], *p["ln3"], tm_s)
        for lst, val in zip(outs, (pb, c1, n1, m1, sb, ns, ms)):
            lst.append(val)
    pp, pc, pn, pm, sp, sn, sm = (jnp.stack(o) for o in outs)
    return (xp.reshape(bp, tp, d), xs.reshape(bs, ts, d), pp, pc, pn, pm, mk4, mv4, sp, sc, sn, sm)
```

```python
import functools

import jax
import jax.numpy as jnp
from jax import lax
from jax.experimental import pallas as pl
from jax.experimental.pallas import tpu as pltpu

F32 = jnp.float32
BF16 = jnp.bfloat16

D_MODEL = 1024
DEPTH = 4
POOL_WIDTH = 512
POOL_GROUPS = 4
POOL_GROUP_DIM = 128
POOL_WINDOWS = (2, 4, 8, 16)
POOL_BUF = 15
MLSTM_WIDTH = 512
MLSTM_HEADS = 4
MLSTM_HEAD_DIM = 128
N_MEM = 256
CA_HEADS = 4
CA_HEAD_DIM = 256
N_GROUPS = 4
EXPERTS_PER_GROUP = 4
N_EXPERTS = 16
D_EXPERT = 256
ALPHA = (2 * DEPTH) ** 0.25
LN_EPS = 1e-5
GATE_OFF = POOL_WIDTH + 4 * MLSTM_WIDTH

LANES = 128
SUBLANES = 8
VMEM_LIMIT = 56 * 1024 * 1024
MLSTM_CHUNK = 256
POOL_ROWS = 2048
MLSTM_STEP_SEQS = 16
STEP_INPUTS = 11
POOL_HALO = 16
ROUTER_ROWS = 32
MOE_SEGMENT = 512
MOE_CAP = 160


def _params(*sem):
    return pltpu.CompilerParams(dimension_semantics=sem, vmem_limit_bytes=VMEM_LIMIT)


def _dot(a, b):
    return jnp.dot(a, b, preferred_element_type=F32)


def _dot_nt(a, b, precision=None):
    return lax.dot_general(a, b, (((1,), (1,)), ((), ())), precision=precision,
                           preferred_element_type=F32)


def _dot_tn(a, b):
    return lax.dot_general(a, b, (((0,), (0,)), ((), ())), preferred_element_type=F32)


def _layer_norm(x, g, b):
    mu = jnp.mean(x, -1, keepdims=True)
    xc = x - mu
    var = jnp.mean(xc * xc, -1, keepdims=True)
    return xc * lax.rsqrt(var + LN_EPS) * g + b


def _unit_norm(x):
    mu = jnp.mean(x, -1, keepdims=True)
    xc = x - mu
    var = jnp.mean(xc * xc, -1, keepdims=True)
    return xc * lax.rsqrt(var + LN_EPS)


def _log_sigmoid(x):
    return jnp.minimum(x, 0.0) - jnp.log1p(jnp.exp(-jnp.abs(x)))


def _scan(x, axis, op, fill):
    n = x.shape[axis]
    idx = lax.broadcasted_iota(jnp.int32, x.shape, axis)
    s = 1
    while s < n:
        x = op(x, jnp.where(idx >= s, pltpu.roll(x, s, axis), fill))
        s *= 2
    return x


def _ln_kernel(x_ref, g_ref, b_ref, o_ref):
    o_ref[...] = _layer_norm(x_ref[...], g_ref[...], b_ref[...])


def _ln(x, g, b, tm):
    n, d = x.shape
    row = pl.BlockSpec((tm, d), lambda i: (i, 0))
    vec = pl.BlockSpec((1, d), lambda i: (0, 0))
    return pl.pallas_call(
        _ln_kernel, grid=(n // tm,), in_specs=[row, vec, vec], out_specs=row,
        out_shape=jax.ShapeDtypeStruct((n, d), F32), compiler_params=_params("parallel"),
        name="emb_ln")(x, g.reshape(1, d), b.reshape(1, d))


def _qkvo(xb, w_ref, q_ref, k_ref, v_ref, og_ref):
    w = MLSTM_WIDTH
    q = _dot(xb, w_ref[:, POOL_WIDTH:POOL_WIDTH + w]) * (MLSTM_HEAD_DIM ** -0.5)
    q_ref[...] = q.astype(q_ref.dtype)
    k_ref[...] = _dot(xb, w_ref[:, POOL_WIDTH + w:POOL_WIDTH + 2 * w]).astype(k_ref.dtype)
    v_ref[...] = _dot(xb, w_ref[:, POOL_WIDTH + 2 * w:POOL_WIDTH + 3 * w]).astype(v_ref.dtype)
    og_ref[...] = _dot(xb, w_ref[:, POOL_WIDTH + 3 * w:POOL_WIDTH + 4 * w])


def _inproj_kernel(x_ref, w_ref, wg_ref, u_ref, q_ref, k_ref, v_ref, og_ref, gi_ref, gf_ref):
    xb = x_ref[...].astype(BF16)
    u_ref[...] = _dot(xb, w_ref[:, 0:POOL_WIDTH])
    _qkvo(xb, w_ref, q_ref, k_ref, v_ref, og_ref)
    g = _dot(xb, wg_ref[...])
    gi_ref[...] = g[:, 0:LANES]
    gf_ref[...] = g[:, LANES:2 * LANES]


def _inproj(x, w_main, wg, tm):
    n, d = x.shape
    row = lambda c: pl.BlockSpec((tm, c), lambda i: (i, 0))
    full = lambda a: pl.BlockSpec(a.shape, lambda i: (0, 0))
    widths = (POOL_WIDTH, MLSTM_WIDTH, MLSTM_WIDTH, MLSTM_WIDTH, MLSTM_WIDTH, LANES, LANES)
    return pl.pallas_call(
        _inproj_kernel, grid=(n // tm,), in_specs=[row(d), full(w_main), full(wg)],
        out_specs=tuple(row(c) for c in widths),
        out_shape=tuple(jax.ShapeDtypeStruct((n, c), F32) for c in widths),
        compiler_params=_params("parallel"), name="inproj")(x, w_main, wg)


def _inproj_seq_kernel(*refs, per, first, ride):
    n_in = 11 + (2 if first else 0) + (STEP_INPUTS if ride else 0)
    x_ref, w_ref, wgt_ref, pw_ref, sc_ref, wg_ref, wu_ref, wd_ref, wo_ref, wq_ref, wa_ref = refs[:11]
    outs, carry_ref = refs[n_in:-1], refs[-1]
    (po_ref, tail_ref, q_ref, k_ref, v_ref, og_ref, git_ref, gft_ref,
     cg_ref, cu_ref, cd_ref, co_ref, cq_ref, ca_ref) = outs[:14]
    tm = x_ref.shape[0]
    start = lax.rem(pl.program_id(0), per) * tm

    @pl.when(start == 0)
    def _():
        carry_ref[...] = jnp.zeros_like(carry_ref)

    for src, dst in ((wg_ref, cg_ref), (wu_ref, cu_ref), (wd_ref, cd_ref), (wo_ref, co_ref), (wq_ref, cq_ref),
                     (wa_ref, ca_ref)):
        dst[...] = src[...].astype(BF16)
    if ride:
        _mlstm_step_kernel(*refs[11:11 + STEP_INPUTS], *outs[14:18])
    x = x_ref[...]
    if first:
        x = _layer_norm(x, refs[11][...], refs[12][...])
        outs[14][...] = x
        outs[15][...] = jnp.zeros_like(outs[15])

    xb = x.astype(BF16)
    u = _dot(xb, w_ref[:, 0:POOL_WIDTH])
    halo = carry_ref.shape[0]
    ext = jnp.concatenate([carry_ref[...], u], axis=0)
    tail = ext[tm:, :]
    carry_ref[...] = tail
    tail_ref[...] = tail
    pos = start + lax.broadcasted_iota(jnp.int32, (tm, POOL_GROUP_DIM), 0)
    for g, win in enumerate(POOL_WINDOWS):
        sl = slice(g * POOL_GROUP_DIM, (g + 1) * POOL_GROUP_DIM)
        acc = ext[:, sl]
        s = 1
        while s < win:
            acc = acc + pltpu.roll(acc, s, 0)
            s *= 2
        cnt = jnp.minimum(pos + 1, win).astype(F32)
        dlt = acc[halo:, :] / cnt - u[:, sl]
        po_ref[:, sl] = (_dot(dlt.astype(BF16), pw_ref[g]) * sc_ref[:, sl]).astype(po_ref.dtype)
    _qkvo(xb, w_ref, q_ref, k_ref, v_ref, og_ref)
    gt = _dot_nt(wgt_ref[...], xb)
    git_ref[...] = gt[0:SUBLANES]
    gft_ref[...] = gt[SUBLANES:2 * SUBLANES]


def _inproj_seq(x, w_main, wgt, pw, scale, experts, dense, layer, seq, tm, first=None, ride=None):
    n, d = x.shape
    per = seq // tm
    steps = n // tm
    split = steps // N_EXPERTS
    assert split * N_EXPERTS == steps and d % steps == 0 and d % split == 0 and D_EXPERT % split == 0
    row = lambda c: pl.BlockSpec((tm, c), lambda i: (i, 0))
    full = lambda a: pl.BlockSpec(a.shape, lambda i: (0,) * a.ndim)
    gt_spec = pl.BlockSpec((None, SUBLANES, tm), lambda i: (i // per, 0, i % per))
    gt_shape = jax.ShapeDtypeStruct((n // seq, SUBLANES, seq), F32)
    mw, e, gf = MLSTM_WIDTH, EXPERTS_PER_GROUP, EXPERTS_PER_GROUP * D_EXPERT
    dr, fr, wr = d // split, D_EXPERT // split, d // steps
    col_in = pl.BlockSpec((None, None, dr, D_EXPERT), lambda i: (layer, i // split, i % split, 0))
    row_in = pl.BlockSpec((None, None, fr, d), lambda i: (layer, i // split, i % split, 0))
    col_out = pl.BlockSpec((None, dr, D_EXPERT), lambda i: (i // split // e, i % split, (i // split) % e))
    row_out = pl.BlockSpec((None, fr, d), lambda i: (i // split // e, ((i // split) % e) * split + i % split, 0))
    dense_in = pl.BlockSpec((None, wr, d), lambda i: (layer, i, 0))
    dense_out = pl.BlockSpec((wr, d), lambda i: (i, 0))
    out_shape = [
        jax.ShapeDtypeStruct((n, POOL_WIDTH), BF16),
        jax.ShapeDtypeStruct((steps, POOL_HALO, POOL_WIDTH), F32),
        jax.ShapeDtypeStruct((n, mw), BF16), jax.ShapeDtypeStruct((n, mw), BF16), jax.ShapeDtypeStruct((n, mw), BF16),
        jax.ShapeDtypeStruct((n, mw), F32), gt_shape, gt_shape,
        jax.ShapeDtypeStruct((N_GROUPS, d, gf), BF16), jax.ShapeDtypeStruct((N_GROUPS, d, gf), BF16),
        jax.ShapeDtypeStruct((N_GROUPS, gf, d), BF16),
        jax.ShapeDtypeStruct((d, d), BF16), jax.ShapeDtypeStruct((d, d), BF16), jax.ShapeDtypeStruct((d, d), BF16),
    ]
    out_specs = [row(POOL_WIDTH), pl.BlockSpec((None, POOL_HALO, POOL_WIDTH), lambda i: (i, 0, 0)),
                 row(mw), row(mw), row(mw), row(mw), gt_spec, gt_spec,
                 col_out, col_out, row_out, dense_out, dense_out, dense_out]
    extra_in, extra_args = [], []
    if first is not None:
        ln_g, ln_b, zeros_like = first
        lead = zeros_like.shape[0] * zeros_like.shape[1]
        assert lead % steps == 0 and zeros_like.shape[1] % (lead // steps) == 0
        zb = lead // steps
        zper = zeros_like.shape[1] // zb
        out_shape += [jax.ShapeDtypeStruct((n, d), F32), jax.ShapeDtypeStruct(zeros_like.shape, zeros_like.dtype)]
        out_specs += [row(d), pl.BlockSpec((None, zb) + zeros_like.shape[2:],
                                           lambda i: (i // zper, i % zper) + (0,) * (zeros_like.ndim - 2))]
        extra_args = [ln_g.reshape(1, d), ln_b.reshape(1, d)]
        extra_in = [full(a) for a in extra_args]
    aliases = {}
    if ride is not None:
        assert first is None
        rq, rk, rv, rgi, rgf, rbi, rbf, c_all, n_all, rm, c_new = ride
        n_s = rq.shape[0]
        sb = n_s // steps
        assert sb * steps == n_s
        hd = MLSTM_HEAD_DIM
        r3 = lambda a: a.reshape(steps, sb, a.shape[-1])
        s3 = lambda c: pl.BlockSpec((None, sb, c), lambda i: (i, 0, 0))
        cspec = pl.BlockSpec((None, sb, MLSTM_HEADS, hd, hd), lambda i: (layer, i, 0, 0, 0))
        nspec = pl.BlockSpec((None, sb, MLSTM_HEADS, hd), lambda i: (layer, i, 0, 0))
        extra_args = [r3(rq), r3(rk), r3(rv), r3(rgi), r3(rgf), rbi, rbf, c_all, n_all, r3(rm), c_new]
        assert len(extra_args) == STEP_INPUTS
        extra_in = [s3(mw), s3(mw), s3(mw), s3(LANES), s3(LANES), full(rbi), full(rbf), cspec, nspec, s3(LANES),
                    pl.BlockSpec(memory_space=pl.ANY)]
        aliases = {11 + STEP_INPUTS - 1: len(out_shape) + 1}
        out_shape += [jax.ShapeDtypeStruct((steps, sb, mw), F32), jax.ShapeDtypeStruct(c_new.shape, F32),
                      jax.ShapeDtypeStruct(n_all.shape[1:], F32), jax.ShapeDtypeStruct((steps, sb, LANES), F32)]
        out_specs += [s3(mw), cspec, pl.BlockSpec((sb, MLSTM_HEADS, hd), lambda i: (i, 0, 0)), s3(LANES)]
    scale = scale.reshape(1, POOL_WIDTH)
    return pl.pallas_call(
        functools.partial(_inproj_seq_kernel, per=per, first=first is not None, ride=ride is not None), grid=(steps,),
        in_specs=[row(d), full(w_main), full(wgt), full(pw), full(scale), col_in, col_in, row_in,
                  dense_in, dense_in, dense_in] + extra_in,
        out_specs=tuple(out_specs), out_shape=tuple(out_shape),
        scratch_shapes=[pltpu.VMEM((POOL_HALO, POOL_WIDTH), F32)], input_output_aliases=aliases,
        compiler_params=_params("arbitrary"), name="inproj_seq")(x, w_main, wgt, pw, scale, *experts, *dense,
                                                                 *extra_args)


def _pool_kernel(ext_ref, pw_ref, sc_ref, o_ref, *, period):
    rows = ext_ref.shape[0]
    r = lax.broadcasted_iota(jnp.int32, (rows, POOL_GROUP_DIM), 0)
    if period != rows:
        r = lax.rem(r, period)
    for g, win in enumerate(POOL_WINDOWS):
        sl = slice(g * POOL_GROUP_DIM, (g + 1) * POOL_GROUP_DIM)
        x = ext_ref[:, sl]
        acc = x
        s = 1
        while s < win:
            acc = acc + jnp.where(r >= s, pltpu.roll(acc, s, 0), 0.0)
            s *= 2
        cnt = jnp.minimum(r + 1, win).astype(F32)
        d = acc / cnt - x
        y = _dot(d.astype(BF16), pw_ref[g]) * sc_ref[:, sl]
        o_ref[:, sl] = y.astype(o_ref.dtype)


def _pool(ext2d, pw, scale, period):
    n, c = ext2d.shape
    row = pl.BlockSpec((POOL_ROWS, c), lambda i: (i, 0))
    return pl.pallas_call(
        functools.partial(_pool_kernel, period=period), grid=(n // POOL_ROWS,),
        in_specs=[row, pl.BlockSpec(pw.shape, lambda i: (0, 0, 0)), pl.BlockSpec((1, c), lambda i: (0, 0))],
        out_specs=row, out_shape=jax.ShapeDtypeStruct((n, c), BF16),
        compiler_params=_params("parallel"), name="pool")(ext2d, pw, scale.reshape(1, c))


def _mlstm_chunk_kernel(q_ref, k_ref, v_ref, git_ref, gft_ref, bit_ref, bft_ref, hn_ref, c_ref, n_ref, m_ref,
                        at_s, ws_s, dc_s, mc_s, wi_s, fl_s):
    nb, L, _ = q_ref.shape
    rows = nb * SUBLANES

    @pl.when(pl.program_id(0) == 0)
    def _():
        c_ref[...] = jnp.zeros_like(c_ref)
        n_ref[...] = jnp.zeros_like(n_ref)
        m_ref[...] = jnp.zeros_like(m_ref)

    ig = git_ref[...].reshape(rows, L) + bit_ref[...]
    lf = _log_sigmoid(gft_ref[...].reshape(rows, L) + bft_ref[...])
    tri = (lax.broadcasted_iota(jnp.int32, (L, L), 0) <= lax.broadcasted_iota(jnp.int32, (L, L), 1)).astype(BF16)
    lf_hi = lf.astype(BF16)
    lf_mid = (lf - lf_hi.astype(F32)).astype(BF16)
    lf_lo = (lf - lf_hi.astype(F32) - lf_mid.astype(F32)).astype(BF16)
    bc = _dot(lf_hi, tri) + (_dot(lf_mid, tri) + _dot(lf_lo, tri))
    at = ig - bc
    m_prev = m_ref[...].reshape(rows, LANES)[:, 0:1]
    mc = jnp.maximum(_scan(at, 1, jnp.maximum, -jnp.inf), m_prev)
    mt = bc + mc
    m_last = mc[:, L - 1:L]
    at_s[...] = at.reshape(nb, SUBLANES, L)
    ws_s[...] = jnp.exp(at - m_last).reshape(nb, SUBLANES, L)
    dc_s[...] = jnp.broadcast_to(jnp.exp(m_prev - m_last), (rows, LANES)).reshape(nb, SUBLANES, LANES)
    m_ref[...] = jnp.broadcast_to(mt[:, L - 1:L], (rows, LANES)).reshape(nb, SUBLANES, LANES)
    wi = jnp.exp(m_prev - mc)
    fl = jnp.exp(-mt)
    for b in range(nb):
        rs = slice(b * SUBLANES, (b + 1) * SUBLANES)
        mc_s[b] = mc[rs].T
        wi_s[b] = wi[rs].T
        fl_s[b] = fl[rs].T

    causal = (lax.broadcasted_iota(jnp.int32, (L, L), 0) >= lax.broadcasted_iota(jnp.int32, (L, L), 1))

    def per_batch(heads, b, carry):
        at = at_s[b]
        ws = ws_s[b]
        decay = dc_s[b]
        mc_c, wi_c, fl_c = mc_s[b], wi_s[b], fl_s[b]
        ws16 = jnp.concatenate([ws, ws], axis=0).astype(BF16)
        for h in heads:
            sl = slice(h * MLSTM_HEAD_DIM, (h + 1) * MLSTM_HEAD_DIM)
            col = slice(h, h + 1)
            qh = q_ref[b, :, sl]
            kh = k_ref[b, :, sl]
            vh = v_ref[b, :, sl]
            s = _dot_nt(qh, kh)
            p = jnp.where(causal, s * jnp.exp(at[col, :] - mc_c[:, col]), 0.0)
            c_old = c_ref[b, h]
            n_old = n_ref[b, col, :]
            wi = wi_c[:, col]
            dv = MLSTM_HEAD_DIM
            v_ext = jnp.concatenate([vh, jnp.ones_like(vh)], axis=1)
            c_ext = jnp.concatenate([c_old, jnp.broadcast_to(n_old, c_old.shape)], axis=0)
            intra = _dot(p.astype(BF16), v_ext)
            inter = _dot_nt(qh, c_ext.astype(BF16))
            num = intra[:, :dv] + wi * inter[:, :dv]
            qn = intra[:, dv:] + wi * inter[:, dv:]
            hh = num / jnp.maximum(jnp.abs(qn), fl_c[:, col])
            hn_ref[b, :, sl] = _unit_norm(hh)
            dc = decay[col, 0:1]
            vts = (vh.astype(F32).T * ws[col, :]).astype(BF16)
            c_ref[b, h] = dc * c_old + _dot(vts, kh)
            n_ref[b, col, :] = dc * n_old + _dot(ws16, kh)[col, :]
        return carry

    for h0 in range(0, MLSTM_HEADS, 2):
        lax.fori_loop(0, nb, functools.partial(per_batch, (h0, h0 + 1)), 0)


def _mlstm_chunk(q, k, v, git, gft, bit, bft):
    nb, t, w = q.shape
    L = MLSTM_CHUNK
    seq = lambda c: pl.BlockSpec((nb, L, c), lambda i: (0, i, 0))
    seqt = pl.BlockSpec((nb, SUBLANES, L), lambda i: (0, 0, i))
    vec = lambda a: pl.BlockSpec(a.shape, lambda i: (0, 0))
    out_shape = (
        jax.ShapeDtypeStruct((nb, t, w), F32),
        jax.ShapeDtypeStruct((nb, MLSTM_HEADS, MLSTM_HEAD_DIM, MLSTM_HEAD_DIM), F32),
        jax.ShapeDtypeStruct((nb, MLSTM_HEADS, MLSTM_HEAD_DIM), F32),
        jax.ShapeDtypeStruct((nb, SUBLANES, LANES), F32),
    )
    out_specs = (
        seq(w),
        pl.BlockSpec(out_shape[1].shape, lambda i: (0, 0, 0, 0)),
        pl.BlockSpec(out_shape[2].shape, lambda i: (0, 0, 0)),
        pl.BlockSpec(out_shape[3].shape, lambda i: (0, 0, 0)),
    )
    bit, bft = jnp.tile(bit, (nb, 1)), jnp.tile(bft, (nb, 1))
    scratch = [pltpu.VMEM((nb, SUBLANES, L), F32), pltpu.VMEM((nb, SUBLANES, L), F32),
               pltpu.VMEM((nb, SUBLANES, LANES), F32)] + [pltpu.VMEM((nb, L, SUBLANES), F32) for _ in range(3)]
    return pl.pallas_call(
        _mlstm_chunk_kernel, grid=(t // L,),
        in_specs=[seq(w), seq(w), seq(w), seqt, seqt, vec(bit), vec(bft)],
        out_specs=out_specs, out_shape=out_shape, scratch_shapes=scratch, compiler_params=_params("arbitrary"),
        name="mlstm_chunk")(q, k, v, git, gft, bit, bft)


def _mlstm_step_kernel(q_ref, k_ref, v_ref, gi_ref, gf_ref, bi_ref, bf_ref, c_ref, n_ref, m_ref, c_new_ref,
                       hn_ref, co_ref, no_ref, mo_ref):
    del c_new_ref
    tb = q_ref.shape[0]
    d = MLSTM_HEAD_DIM
    ig = gi_ref[...] + bi_ref[...]
    lf = _log_sigmoid(gf_ref[...] + bf_ref[...])
    m_old = m_ref[...]
    mt = jnp.maximum(lf + m_old, ig)
    mo_ref[...] = mt
    wa = jnp.exp(ig - mt)
    wi = jnp.exp(lf + m_old - mt)
    fl = jnp.exp(-mt)
    wa_t, wi_t, fl_t = wa.T, wi.T, fl.T
    lane = lax.broadcasted_iota(jnp.int32, (d, tb), 1)
    for h in range(MLSTM_HEADS):
        sl = slice(h * d, (h + 1) * d)
        qh, kh, vh = q_ref[:, sl], k_ref[:, sl], v_ref[:, sl]
        nh = n_ref[:, h, :]
        qt, kt, vt, nt = qh.T, kh.T, vh.T, nh.T
        wa_r, wi_r, fl_r = wa_t[h:h + 1, :], wi_t[h:h + 1, :], fl_t[h:h + 1, :]
        s = jnp.sum(qt * kt, 0, keepdims=True) * wa_r
        cq = jnp.zeros((d, tb), F32)
        qtb = qt.astype(BF16)
        for b in range(tb):
            cq = jnp.where(lane == b, _dot(c_ref[b, h].astype(BF16), qtb), cq)
        num = s * vt + wi_r * cq
        qn = s + wi_r * jnp.sum(nt * qt, 0, keepdims=True)
        hh = num / jnp.maximum(jnp.abs(qn), fl_r)
        mu = jnp.mean(hh, 0, keepdims=True)
        xc = hh - mu
        var = jnp.mean(xc * xc, 0, keepdims=True)
        hn_ref[:, sl] = (xc * lax.rsqrt(var + LN_EPS)).T
        wav = wa_r * vt
        for b in range(tb):
            co_ref[b, h] = wi_r[:, b:b + 1] * c_ref[b, h] + wav[:, b:b + 1] * kh[b:b + 1, :]
        no_ref[:, h, :] = wi[:, h:h + 1] * nh + wa[:, h:h + 1] * kh


def _mlstm_step(q, k, v, gi, gf, bi, bf, c_all, n_all, m0, c_new, layer, tb):
    n, w = q.shape
    d = MLSTM_HEAD_DIM
    row = lambda c: pl.BlockSpec((tb, c), lambda i: (i, 0))
    vec = lambda a: pl.BlockSpec(a.shape, lambda i: (0, 0))
    cin = pl.BlockSpec((None, tb, MLSTM_HEADS, d, d), lambda i: (layer, i, 0, 0, 0))
    nin = pl.BlockSpec((None, tb, MLSTM_HEADS, d), lambda i: (layer, i, 0, 0))
    nout = pl.BlockSpec((tb, MLSTM_HEADS, d), lambda i: (i, 0, 0))
    out_shape = (
        jax.ShapeDtypeStruct((n, w), F32),
        jax.ShapeDtypeStruct(c_new.shape, F32),
        jax.ShapeDtypeStruct(n_all.shape[1:], F32),
        jax.ShapeDtypeStruct((n, LANES), F32),
    )
    return pl.pallas_call(
        _mlstm_step_kernel, grid=(n // tb,),
        in_specs=[row(w), row(w), row(w), row(LANES), row(LANES), vec(bi), vec(bf), cin, nin, row(LANES),
                  pl.BlockSpec(memory_space=pl.ANY)],
        out_specs=(row(w), cin, nout, row(LANES)), out_shape=out_shape, input_output_aliases={10: 1},
        compiler_params=_params("parallel"), name="mlstm_step")(q, k, v, gi, gf, bi, bf, c_all, n_all, m0, c_new)


def _mixout_kernel(po_ref, hn_ref, og_ref, x_ref, ng_ref, w_ref, g_ref, b_ref, o_ref):
    mo = jax.nn.sigmoid(og_ref[...]) * hn_ref[...] * ng_ref[...]
    mix = _dot(po_ref[...], w_ref[0:POOL_WIDTH, :]) + _dot(mo.astype(BF16), w_ref[POOL_WIDTH:, :])
    o_ref[...] = _layer_norm(ALPHA * x_ref[...] + mix, g_ref[...], b_ref[...])


def _mixout(po, hn, og, x, ng, w, g, b, tm):
    n, d = x.shape
    row = lambda c: pl.BlockSpec((tm, c), lambda i: (i, 0))
    vec = lambda a: pl.BlockSpec(a.shape, lambda i: (0, 0))
    return pl.pallas_call(
        _mixout_kernel, grid=(n // tm,),
        in_specs=[row(POOL_WIDTH), row(MLSTM_WIDTH), row(MLSTM_WIDTH), row(d), vec(ng), vec(w), vec(g), vec(b)],
        out_specs=row(d), out_shape=jax.ShapeDtypeStruct((n, d), F32),
        compiler_params=_params("parallel"), name="mixout")(po, hn, og, x, ng, w, g, b)


def _mm_kernel(x_ref, w_ref, o_ref, *, scale):
    y = _dot(x_ref[...].astype(BF16), w_ref[...])
    if scale != 1.0:
        y = y * scale
    o_ref[...] = y.astype(o_ref.dtype)


def _mm(x, w, tm, out_dtype, scale=1.0, name="proj"):
    n, d = x.shape
    dout = w.shape[1]
    return pl.pallas_call(
        functools.partial(_mm_kernel, scale=scale), grid=(n // tm,),
        in_specs=[pl.BlockSpec((tm, d), lambda i: (i, 0)), pl.BlockSpec(w.shape, lambda i: (0, 0))],
        out_specs=pl.BlockSpec((tm, dout), lambda i: (i, 0)),
        out_shape=jax.ShapeDtypeStruct((n, dout), out_dtype),
        compiler_params=_params("parallel"), name=name)(x, w)


def _cast_inproj_kernel(wt_ref, o_ref, og_ref, *, n_main):
    i = pl.program_id(0)

    @pl.when(i < n_main)
    def _():
        o_ref[...] = wt_ref[...].T.astype(BF16)

    @pl.when(i == n_main)
    def _():
        og_ref[...] = wt_ref[0:2 * MLSTM_HEADS, :]


def _cast_inproj(w_in, layer, tc=256):
    wt = jnp.swapaxes(w_in, 1, 2)
    d = wt.shape[2]
    n_main = GATE_OFF // tc
    return pl.pallas_call(
        functools.partial(_cast_inproj_kernel, n_main=n_main), grid=(n_main + 1,),
        in_specs=[pl.BlockSpec((None, tc, d), lambda i: (layer, i, 0))],
        out_specs=(pl.BlockSpec((d, tc), lambda i: (0, jnp.minimum(i, n_main - 1))),
                   pl.BlockSpec((2 * MLSTM_HEADS, d), lambda i: (0, 0))),
        out_shape=(jax.ShapeDtypeStruct((d, GATE_OFF), BF16), jax.ShapeDtypeStruct((2 * MLSTM_HEADS, d), F32)),
        compiler_params=_params("arbitrary"), name="inproj_cast")(wt)


def _mem_proj_kernel(x_ref, wk_ref, wv_ref, k4_ref, v4_ref, k2_ref, v2_ref, wkb, wvb):
    @pl.when(pl.program_id(1) == 0)
    def _():
        wkb[...] = wk_ref[...].astype(BF16)
        wvb[...] = wv_ref[...].astype(BF16)

    bb, m, d = x_ref.shape
    xb = x_ref[...].reshape(bb * m, d).astype(BF16)
    k = _dot(xb, wkb[...])
    v = _dot(xb, wvb[...])
    k2_ref[...] = k.reshape(bb, m, d)
    v2_ref[...] = v.reshape(bb, m, d)
    k4_ref[...] = k.reshape(bb, m, CA_HEADS, CA_HEAD_DIM)
    v4_ref[...] = v.reshape(bb, m, CA_HEADS, CA_HEAD_DIM)


def _mem_proj(mem, wk_all, wv_all, bb=2):
    nb, m, d = mem.shape
    depth = wk_all.shape[0]
    wspec = pl.BlockSpec((None, d, d), lambda l, b: (l, 0, 0))
    o4 = pl.BlockSpec((None, bb, m, CA_HEADS, CA_HEAD_DIM), lambda l, b: (l, b, 0, 0, 0))
    o2 = pl.BlockSpec((None, bb, m, d), lambda l, b: (l, b, 0, 0))
    s4 = jax.ShapeDtypeStruct((depth, nb, m, CA_HEADS, CA_HEAD_DIM), F32)
    s2 = jax.ShapeDtypeStruct((depth, nb, m, d), F32)
    return pl.pallas_call(
        _mem_proj_kernel, grid=(depth, nb // bb),
        in_specs=[pl.BlockSpec((bb, m, d), lambda l, b: (b, 0, 0)), wspec, wspec],
        out_specs=(o4, o4, o2, o2), out_shape=(s4, s4, s2, s2),
        scratch_shapes=[pltpu.VMEM((d, d), BF16), pltpu.VMEM((d, d), BF16)],
        compiler_params=_params("arbitrary", "arbitrary"), name="mem_proj")(mem, wk_all, wv_all)


def _attn_block_kernel(po_ref, hn_ref, og_ref, x_ref, k_ref, v_ref, ng_ref, wm_ref, g1_ref, b1_ref,
                       wq_ref, wo_ref, g_ref, b_ref, o_ref):
    mo = jax.nn.sigmoid(og_ref[...]) * hn_ref[...] * ng_ref[...]
    mix = _dot(po_ref[...], wm_ref[0:POOL_WIDTH, :]) + _dot(mo.astype(BF16), wm_ref[POOL_WIDTH:, :])
    x = _layer_norm(ALPHA * x_ref[...] + mix, g1_ref[...], b1_ref[...])
    qb = (_dot(x.astype(BF16), wq_ref[...]) * (CA_HEAD_DIM ** -0.5)).astype(BF16)
    kb = k_ref[...].astype(BF16)
    vb = v_ref[...].astype(BF16)
    ctx = []
    for h in range(CA_HEADS):
        sl = slice(h * CA_HEAD_DIM, (h + 1) * CA_HEAD_DIM)
        s = _dot_nt(qb[:, sl], kb[:, sl])
        e = jnp.exp(s - jnp.max(s, -1, keepdims=True))
        ctx.append((_dot(e.astype(BF16), vb[:, sl]) / jnp.sum(e, -1, keepdims=True)).astype(BF16))
    y = _dot(jnp.concatenate(ctx, axis=1), wo_ref[...])
    o_ref[...] = _layer_norm(ALPHA * x + y, g_ref[...], b_ref[...])


def _attn_block(po, hn, og, x, k, v, layer, ng, wm, ln1, wq, wo, ln2, seq, tq):
    n, d = x.shape
    per = seq // tq
    row = lambda c: pl.BlockSpec((tq, c), lambda i: (i, 0))
    kspec = pl.BlockSpec((None, None, N_MEM, d), lambda i: (layer, i // per, 0, 0))
    vec = lambda a: pl.BlockSpec(a.shape, lambda i: (0, 0))
    return pl.pallas_call(
        _attn_block_kernel, grid=(n // tq,),
        in_specs=[row(POOL_WIDTH), row(MLSTM_WIDTH), row(MLSTM_WIDTH), row(d), kspec, kspec, vec(ng), vec(wm),
                  vec(ln1[0]), vec(ln1[1]), vec(wq), vec(wo), vec(ln2[0]), vec(ln2[1])],
        out_specs=row(d), out_shape=jax.ShapeDtypeStruct((n, d), F32),
        compiler_params=_params("parallel"), name="attn_block")(po, hn, og, x, k, v, ng, wm, *ln1, wq, wo, *ln2)


def _attn_step_kernel(q_ref, k_ref, v_ref, o_ref):
    bb = q_ref.shape[0]
    rows = N_MEM * CA_HEADS
    lane = lax.broadcasted_iota(jnp.int32, (SUBLANES, rows), 1)
    row = lax.broadcasted_iota(jnp.int32, (SUBLANES, rows), 0)
    own = lax.rem(lane, CA_HEADS) == lax.rem(row, CA_HEADS)
    for j in range(bb):
        kf = k_ref[j].reshape(rows, CA_HEAD_DIM).astype(BF16)
        vf = v_ref[j].reshape(rows, CA_HEAD_DIM).astype(BF16)
        s = jnp.where(own, _dot_nt(q_ref[j].astype(BF16), kf), -jnp.inf)
        e = jnp.exp(s - jnp.max(s, -1, keepdims=True))
        o_ref[j] = _dot(e.astype(BF16), vf) / jnp.sum(e, -1, keepdims=True)


def _mm_res_ln_kernel(a_ref, x_ref, w_ref, g_ref, b_ref, o_ref):
    y = _dot(a_ref[...].astype(BF16), w_ref[...])
    o_ref[...] = _layer_norm(ALPHA * x_ref[...] + y, g_ref[...], b_ref[...])


def _mm_res_ln(a, x, w, g, b, tm):
    n, d = x.shape
    row = lambda c: pl.BlockSpec((tm, c), lambda i: (i, 0))
    vec = lambda arr: pl.BlockSpec(arr.shape, lambda i: (0, 0))
    return pl.pallas_call(
        _mm_res_ln_kernel, grid=(n // tm,),
        in_specs=[row(a.shape[1]), row(d), vec(w), vec(g), vec(b)], out_specs=row(d),
        out_shape=jax.ShapeDtypeStruct((n, d), F32),
        compiler_params=_params("parallel"), name="proj_res_ln")(a, x, w, g, b)


def _route(lt):
    gl = [lt[g:g + 1, :] for g in range(N_GROUPS)]
    gmax = functools.reduce(jnp.maximum, gl)
    gsum = functools.reduce(jnp.add, [jnp.exp(x - gmax) for x in gl])
    pg_sel = 1.0 / gsum

    def first_max(vals):
        m = functools.reduce(jnp.maximum, vals)
        taken = jnp.zeros_like(m, dtype=jnp.bool_)
        hot = []
        for x in vals:
            h = jnp.logical_and(x == m, jnp.logical_not(taken))
            taken = jnp.logical_or(taken, h)
            hot.append(h)
        return m, hot

    _, g_hot = first_max(gl)
    el = []
    for j in range(EXPERTS_PER_GROUP):
        rows = [lt[SUBLANES + g * EXPERTS_PER_GROUP + j:SUBLANES + g * EXPERTS_PER_GROUP + j + 1, :]
                for g in range(N_GROUPS)]
        x = rows[N_GROUPS - 1]
        for g in range(N_GROUPS - 2, -1, -1):
            x = jnp.where(g_hot[g], rows[g], x)
        el.append(x)
    emax = functools.reduce(jnp.maximum, el)
    ee = [jnp.exp(x - emax) for x in el]
    esum = functools.reduce(jnp.add, ee)
    pe = [x / esum for x in ee]
    p1, hot1 = first_max(pe)
    p2, hot2 = first_max([jnp.where(h, -jnp.inf, x) for h, x in zip(hot1, pe)])
    psum = p1 + p2
    gate = [jnp.where(h1, pg_sel * p1 / psum, jnp.where(h2, pg_sel * p2 / psum, 0.0)) for h1, h2 in zip(hot1, hot2)]
    return g_hot, gate


def _moe_kernel(x_ref, wr_ref, br_ref, tri_ref, wg_ref, wu_ref, wd_ref, g_ref, b_ref, *rest):
    ride = None
    if len(rest) > 7:
        ride = functools.partial(_attn_step_kernel, rest[0], rest[1], rest[2], rest[4])
        rest = rest[3:4] + rest[5:]
    o_ref, xb_ref, ct_ref, comb_ref, perm_ref, acc_ref, full_ref = rest
    grp = pl.program_id(1)
    tm = x_ref.shape[0]
    ns, cap, sw = perm_ref.shape[1:]
    f = D_EXPERT

    @pl.when(grp == 0)
    def _():
        x = x_ref[...]
        xh = x.astype(BF16)
        xb_ref[...] = xh
        xl = (x - xh.astype(F32)).astype(BF16)
        wr = wr_ref[...]
        wh = wr.astype(BF16)
        wl = (wr - wh.astype(F32)).astype(BF16)
        lt = _dot_nt(wh, xh) + (_dot_nt(wl, xh) + _dot_nt(wh, xl)) + br_ref[...]
        g_hot, gate = _route(lt)
        row8 = lax.broadcasted_iota(jnp.int32, (SUBLANES, tm), 0)
        hot8 = jnp.zeros((SUBLANES, tm), F32)
        for g in range(N_GROUPS):
            hot8 = jnp.where(jnp.logical_and(row8 == g, g_hot[g]), 1.0, hot8)
        most = None
        for s in range(ns):
            hs = hot8[:, s * sw:(s + 1) * sw]
            cum = _dot(hs.astype(BF16), tri_ref[...])
            seg_max = jnp.max(cum[:, sw - 1:sw])
            most = seg_max if most is None else jnp.maximum(most, seg_max)
            base = lax.broadcasted_iota(jnp.int32, (SUBLANES, sw), 0).astype(F32) * cap
            code = jnp.sum(hs * (base + cum - 1.0), 0, keepdims=True).astype(jnp.int32)
            for g in range(N_GROUPS):
                r = lax.broadcasted_iota(jnp.int32, (cap, sw), 0) + g * cap
                perm_ref[g, s] = jnp.where(r == code, 1.0, 0.0).astype(BF16)
        full_ref[0] = (most > cap).astype(jnp.int32)
        ct_ref[...] = jnp.zeros_like(ct_ref)
        for j in range(EXPERTS_PER_GROUP):
            ct_ref[j:j + 1, :] = gate[j]
        ct_ref[EXPERTS_PER_GROUP:EXPERTS_PER_GROUP + 1, :] = jnp.sum(hot8 * row8.astype(F32), 0, keepdims=True)
        comb_ref[...] = ct_ref[...].T
        acc_ref[...] = jnp.zeros_like(acc_ref)

    def experts(rows, gates):
        hh = []
        for j in range(EXPERTS_PER_GROUP):
            hg = _dot(rows, wg_ref[:, j * f:(j + 1) * f])
            hu = _dot(rows, wu_ref[:, j * f:(j + 1) * f])
            hh.append((hg * jax.nn.sigmoid(hg) * hu * gates[:, j:j + 1]).astype(BF16))
        return _dot(jnp.concatenate(hh, axis=1), wd_ref[...])

    @pl.when(full_ref[0] == 0)
    def _():
        comb = comb_ref[...]
        ch = comb.astype(BF16)
        cl = (comb - ch.astype(F32)).astype(BF16)
        xb = xb_ref[...]
        perm = [perm_ref[grp, s] for s in range(ns)]
        seg = lambda a, s: a[s * sw:(s + 1) * sw]
        rows = jnp.concatenate([_dot(perm[s], seg(xb, s)) for s in range(ns)], axis=0).astype(BF16)
        gates = jnp.concatenate([_dot(perm[s], seg(ch, s)) + _dot(perm[s], seg(cl, s)) for s in range(ns)], axis=0)
        y = experts(rows, gates).astype(BF16)
        for s in range(ns):
            acc_ref[s * sw:(s + 1) * sw, :] += _dot_tn(perm[s], y[s * cap:(s + 1) * cap])
        if ride is not None:
            ride()

    @pl.when(full_ref[0] != 0)
    def _():
        comb = comb_ref[...]
        own = comb[:, EXPERTS_PER_GROUP:EXPERTS_PER_GROUP + 1] == grp.astype(F32)
        acc_ref[...] += experts(xb_ref[...], jnp.where(own, comb, 0.0))
        if ride is not None:
            ride()

    @pl.when(grp == N_GROUPS - 1)
    def _():
        o_ref[...] = _layer_norm(ALPHA * x_ref[...] + acc_ref[...], g_ref[...], b_ref[...])


def _moe(x, wr, br, wg, wu, wd, g, b, tm, attn=None):
    n, d = x.shape
    ns = max(1, tm // MOE_SEGMENT)
    sw = tm // ns
    cap = min(sw, MOE_CAP)
    tri = (lax.broadcasted_iota(jnp.int32, (sw, sw), 0) <= lax.broadcasted_iota(jnp.int32, (sw, sw), 1)).astype(BF16)
    row = pl.BlockSpec((tm, d), lambda i, e: (i, 0))
    vec = lambda a: pl.BlockSpec(a.shape, lambda i, e: (0, 0))
    grp = lambda a: pl.BlockSpec((None,) + a.shape[1:], lambda i, e: (e, 0, 0))
    scratch = [pltpu.VMEM((tm, d), BF16), pltpu.VMEM((LANES, tm), F32), pltpu.VMEM((tm, LANES), F32),
               pltpu.VMEM((N_GROUPS, ns, cap, sw), BF16), pltpu.VMEM((tm, d), F32), pltpu.SMEM((1,), jnp.int32)]
    in_specs = [row, vec(wr), vec(br), vec(tri), grp(wg), grp(wu), grp(wd), vec(g), vec(b)]
    args = [x, wr, br, tri, wg, wu, wd, g, b]
    out_specs, out_shape = row, jax.ShapeDtypeStruct((n, d), F32)
    if attn is not None:
        q, k_all, v_all, layer = attn
        steps = (n // tm) * N_GROUPS
        bb = q.shape[0] // steps
        assert bb * steps == q.shape[0]
        qspec = pl.BlockSpec((bb, SUBLANES, CA_HEAD_DIM), lambda i, e: (i * N_GROUPS + e, 0, 0))
        kspec = pl.BlockSpec((None, bb, N_MEM, CA_HEADS, CA_HEAD_DIM), lambda i, e: (layer, i * N_GROUPS + e, 0, 0, 0))
        in_specs += [qspec, kspec, kspec]
        args += [q, k_all, v_all]
        out_specs, out_shape = (row, qspec), (out_shape, jax.ShapeDtypeStruct(q.shape, F32))
    return pl.pallas_call(
        _moe_kernel, grid=(n // tm, N_GROUPS), in_specs=in_specs, out_specs=out_specs, out_shape=out_shape,
        scratch_shapes=scratch, compiler_params=_params("arbitrary", "arbitrary"), name="moe")(*args)


def _pad_lanes(v):
    return jnp.zeros((1, LANES), F32).at[0, :v.shape[0]].set(v)


def _pad_rows(v):
    return jnp.zeros((SUBLANES, 1), F32).at[:v.shape[0], 0].set(v)


def _layer_weights(l, w_in, b_i, b_f, pool_w, pool_scale, mlstm_norm_g, w_out, ln1_g, ln1_b, ca_wq, ca_wo,
                   ln2_g, ln2_b, w_gr, b_gr, w_er, b_er, w_gate, w_up, w_down, ln3_g, ln3_b):
    d = D_MODEL
    w_main, w_gate_rows = _cast_inproj(w_in, l)
    w_gate_cols = w_gate_rows.T
    wg = jnp.zeros((d, 2 * LANES), F32)
    wg = wg.at[:, 0:MLSTM_HEADS].set(w_gate_cols[:, :MLSTM_HEADS])
    wg = wg.at[:, LANES:LANES + MLSTM_HEADS].set(w_gate_cols[:, MLSTM_HEADS:])
    wgt = jnp.zeros((2 * SUBLANES, d), F32)
    wgt = wgt.at[0:MLSTM_HEADS].set(w_gate_cols[:, :MLSTM_HEADS].T)
    wgt = wgt.at[SUBLANES:SUBLANES + MLSTM_HEADS].set(w_gate_cols[:, MLSTM_HEADS:].T)
    wr = jnp.zeros((ROUTER_ROWS, d), F32)
    wr = wr.at[0:N_GROUPS].set(w_gr[l].T).at[SUBLANES:SUBLANES + N_EXPERTS].set(w_er[l].T)
    br = jnp.zeros((ROUTER_ROWS, 1), F32)
    br = br.at[0:N_GROUPS, 0].set(b_gr[l]).at[SUBLANES:SUBLANES + N_EXPERTS, 0].set(b_er[l])
    row = lambda v: v.reshape(1, -1)
    return dict(
        w_main=w_main, wg=wg.astype(BF16), wgt=wgt.astype(BF16),
        bi=_pad_lanes(b_i[l]), bf=_pad_lanes(b_f[l]), bit=_pad_rows(b_i[l]), bft=_pad_rows(b_f[l]),
        pool_w=pool_w[l].astype(BF16), pool_scale=pool_scale[l], norm_g=row(mlstm_norm_g[l]),
        ln1=(row(ln1_g[l]), row(ln1_b[l])), ln2=(row(ln2_g[l]), row(ln2_b[l])), ln3=(row(ln3_g[l]), row(ln3_b[l])),
        wr=wr, br=br, raw_experts=(w_gate, w_up, w_down), raw_dense=(w_out, ca_wq, ca_wo))


def _prompt_mixers(x, nb, seq, layer, mem, p, tm, tm_big, ride=None):
    if ride is not None:
        tm = tm_big
    per = seq // tm
    outs = _inproj_seq(x, p["w_main"], p["wgt"], p["pool_w"], p["pool_scale"], p["raw_experts"], p["raw_dense"],
                       layer, seq, tm, first=p.get("first"), ride=ride)
    if ride is not None:
        hn_s, c_s, n_s, m_s = outs[14:18]
        p["step"] = (hn_s.reshape(-1, MLSTM_WIDTH), c_s, n_s, m_s.reshape(-1, LANES))
    pooled, tails, q, k, v, og, git, gft = outs[:8]
    p["experts"] = outs[8:11]
    p["w_out"], p["wq"], p["wo"] = outs[11:14]
    if p.get("first") is not None:
        x, p["zeros"] = outs[14:16]
    pool_buf = tails.reshape(nb, per, POOL_HALO, POOL_WIDTH)[:, per - 1, POOL_HALO - POOL_BUF:]
    r3 = lambda a: a.reshape(nb, seq, a.shape[-1])
    hn, c1, n1, m1 = _mlstm_chunk(r3(q), r3(k), r3(v), git, gft, p["bit"], p["bft"])
    x = _attn_block(pooled, hn.reshape(nb * seq, MLSTM_WIDTH), og, x, mem[0], mem[1], layer, p["norm_g"], p["w_out"],
                    p["ln1"], p["wq"], p["wo"], p["ln2"], seq, tm_big)
    return x, pool_buf, c1, n1, m1[:, :MLSTM_HEADS, 0]


def _sample_inproj(x, nb, state, p):
    u, q, k, v, og, gi, gf = _inproj(x, p["w_main"], p["wg"], nb)
    c_all, n_all, m0, c_new = state
    m0p = jnp.zeros((nb, LANES), F32).at[:, :MLSTM_HEADS].set(m0)
    return u, og, (q, k, v, gi, gf, p["bi"], p["bf"], c_all, n_all, m0p, c_new)


def _sample_mixers(x, nb, layer, pool_prev, u, og, step, p):
    ext = jnp.concatenate([pool_prev, u.reshape(nb, 1, POOL_WIDTH)], axis=1)
    period = ext.shape[1]
    pooled = _pool(ext.reshape(nb * period, POOL_WIDTH), p["pool_w"], p["pool_scale"], period)
    pooled = pooled.reshape(nb, period, POOL_WIDTH)[:, period - 1]
    hn, c1, n1, m1 = p["step"] if "step" in p else _mlstm_step(*step, layer, MLSTM_STEP_SEQS)
    x = _mixout(pooled, hn, og, x, p["norm_g"], p["w_out"], *p["ln1"], nb)
    qc = _mm(x, p["wq"], nb, F32, scale=CA_HEAD_DIM ** -0.5, name="ca_q")
    qh = jnp.zeros((nb, SUBLANES, CA_HEAD_DIM), F32).at[:, :CA_HEADS].set(qc.reshape(nb, CA_HEADS, CA_HEAD_DIM))
    return x, qh, ext[:, -POOL_BUF:], c1, n1, m1[:, :MLSTM_HEADS]


def kernel(x_prompt, x_sample, mem_prompt, cache_pool, state_mlstm_C, state_mlstm_n, state_mlstm_m,
           cache_mem_k, cache_mem_v, emb_ln_g, emb_ln_b, w_in, b_i, b_f, pool_w, pool_scale,
           mlstm_norm_g, w_out, ln1_g, ln1_b, ca_wq, ca_wk, ca_wv, ca_wo, ln2_g, ln2_b,
           w_gr, b_gr, w_er, b_er, w_gate, w_up, w_down, ln3_g, ln3_b):
    bp, tp, d = x_prompt.shape
    bs, ts, _ = x_sample.shape
    tm_p, tm_s = 512, bs * ts
    xp = x_prompt.reshape(bp * tp, d)
    xs = _ln(x_sample.reshape(bs * ts, d), emb_ln_g, emb_ln_b, tm_s)
    mk4, mv4, mk2, mv2 = _mem_proj(mem_prompt, ca_wk, ca_wv)
    outs = [[] for _ in range(7)]
    sc = None
    for l in range(DEPTH):
        p = _layer_weights(l, w_in, b_i, b_f, pool_w, pool_scale, mlstm_norm_g, w_out, ln1_g, ln1_b, ca_wq,
                           ca_wo, ln2_g, ln2_b, w_gr, b_gr, w_er, b_er, w_gate, w_up, w_down, ln3_g, ln3_b)
        if l == 0:
            p["first"] = (emb_ln_g, emb_ln_b, state_mlstm_C)
        if l > 0:
            u_s, og_s, step = _sample_inproj(xs, bs * ts, (state_mlstm_C, state_mlstm_n, state_mlstm_m[l], sc), p)
        xp, pb, c1, n1, m1 = _prompt_mixers(xp, bp, tp, l, (mk2, mv2), p, tm_p, 2 * tm_p, ride=step if l > 0 else None)
        if l == 0:
            sc = p["zeros"]
            u_s, og_s, step = _sample_inproj(xs, bs * ts, (state_mlstm_C, state_mlstm_n, state_mlstm_m[l], sc), p)
        xs, qh, sb, sc, ns, ms = _sample_mixers(xs, bs * ts, l, cache_pool[l], u_s, og_s, step, p)
        xp, ctx = _moe(xp, p["wr"], p["br"], *p["experts"], *p["ln3"], 2 * tm_p, attn=(qh, cache_mem_k, cache_mem_v, l))
        xs = _mm_res_ln(ctx[:, :CA_HEADS].reshape(bs * ts, d), xs, p["wo"], *p["ln2"], tm_s)
        xs = _moe(xs, p["wr"], p["br"], *p["experts"], *p["ln3"], tm_s)
        for lst, val in zip(outs, (pb, c1, n1, m1, sb, ns, ms)):
            lst.append(val)
    pp, pc, pn, pm, sp, sn, sm = (jnp.stack(o) for o in outs)
    return (xp.reshape(bp, tp, d), xs.reshape(bs, ts, d), pp, pc, pn, pm, mk4, mv4, sp, sc, sn, sm)
```

```python
import functools

import jax
import jax.numpy as jnp
from jax import lax
from jax.experimental import pallas as pl
from jax.experimental.pallas import tpu as pltpu

F32 = jnp.float32
BF16 = jnp.bfloat16

D_MODEL = 1024
DEPTH = 4
POOL_WIDTH = 512
POOL_GROUPS = 4
POOL_GROUP_DIM = 128
POOL_WINDOWS = (2, 4, 8, 16)
POOL_BUF = 15
MLSTM_WIDTH = 512
MLSTM_HEADS = 4
MLSTM_HEAD_DIM = 128
N_MEM = 256
CA_HEADS = 4
CA_HEAD_DIM = 256
N_GROUPS = 4
EXPERTS_PER_GROUP = 4
N_EXPERTS = 16
D_EXPERT = 256
ALPHA = (2 * DEPTH) ** 0.25
LN_EPS = 1e-5
GATE_OFF = POOL_WIDTH + 4 * MLSTM_WIDTH

LANES = 128
SUBLANES = 8
VMEM_LIMIT = 56 * 1024 * 1024
MLSTM_CHUNK = 256
POOL_ROWS = 2048
MLSTM_STEP_SEQS = 16
STEP_INPUTS = 11
POOL_HALO = 16
ROUTER_ROWS = 32
MOE_SEGMENT = 512
MOE_CAP = 160


def _params(*sem):
    return pltpu.CompilerParams(dimension_semantics=sem, vmem_limit_bytes=VMEM_LIMIT)


def _dot(a, b):
    return jnp.dot(a, b, preferred_element_type=F32)


def _dot_nt(a, b, precision=None):
    return lax.dot_general(a, b, (((1,), (1,)), ((), ())), precision=precision,
                           preferred_element_type=F32)


def _dot_tn(a, b):
    return lax.dot_general(a, b, (((0,), (0,)), ((), ())), preferred_element_type=F32)


def _layer_norm(x, g, b):
    mu = jnp.mean(x, -1, keepdims=True)
    xc = x - mu
    var = jnp.mean(xc * xc, -1, keepdims=True)
    return xc * lax.rsqrt(var + LN_EPS) * g + b


def _unit_norm(x):
    mu = jnp.mean(x, -1, keepdims=True)
    xc = x - mu
    var = jnp.mean(xc * xc, -1, keepdims=True)
    return xc * lax.rsqrt(var + LN_EPS)


def _log_sigmoid(x):
    return jnp.minimum(x, 0.0) - jnp.log1p(jnp.exp(-jnp.abs(x)))


def _scan(x, axis, op, fill):
    n = x.shape[axis]
    idx = lax.broadcasted_iota(jnp.int32, x.shape, axis)
    s = 1
    while s < n:
        x = op(x, jnp.where(idx >= s, pltpu.roll(x, s, axis), fill))
        s *= 2
    return x


def _ln_kernel(x_ref, g_ref, b_ref, o_ref):
    o_ref[...] = _layer_norm(x_ref[...], g_ref[...], b_ref[...])


def _ln(x, g, b, tm):
    n, d = x.shape
    row = pl.BlockSpec((tm, d), lambda i: (i, 0))
    vec = pl.BlockSpec((1, d), lambda i: (0, 0))
    return pl.pallas_call(
        _ln_kernel, grid=(n // tm,), in_specs=[row, vec, vec], out_specs=row,
        out_shape=jax.ShapeDtypeStruct((n, d), F32), compiler_params=_params("parallel"),
        name="emb_ln")(x, g.reshape(1, d), b.reshape(1, d))


def _qkvo(xb, w_ref, q_ref, k_ref, v_ref, og_ref):
    w = MLSTM_WIDTH
    q = _dot(xb, w_ref[:, POOL_WIDTH:POOL_WIDTH + w]) * (MLSTM_HEAD_DIM ** -0.5)
    q_ref[...] = q.astype(q_ref.dtype)
    k_ref[...] = _dot(xb, w_ref[:, POOL_WIDTH + w:POOL_WIDTH + 2 * w]).astype(k_ref.dtype)
    v_ref[...] = _dot(xb, w_ref[:, POOL_WIDTH + 2 * w:POOL_WIDTH + 3 * w]).astype(v_ref.dtype)
    og_ref[...] = _dot(xb, w_ref[:, POOL_WIDTH + 3 * w:POOL_WIDTH + 4 * w])


def _inproj_kernel(x_ref, w_ref, wg_ref, u_ref, q_ref, k_ref, v_ref, og_ref, gi_ref, gf_ref):
    xb = x_ref[...].astype(BF16)
    u_ref[...] = _dot(xb, w_ref[:, 0:POOL_WIDTH])
    _qkvo(xb, w_ref, q_ref, k_ref, v_ref, og_ref)
    g = _dot(xb, wg_ref[...])
    gi_ref[...] = g[:, 0:LANES]
    gf_ref[...] = g[:, LANES:2 * LANES]


def _inproj(x, w_main, wg, tm):
    n, d = x.shape
    row = lambda c: pl.BlockSpec((tm, c), lambda i: (i, 0))
    full = lambda a: pl.BlockSpec(a.shape, lambda i: (0, 0))
    widths = (POOL_WIDTH, MLSTM_WIDTH, MLSTM_WIDTH, MLSTM_WIDTH, MLSTM_WIDTH, LANES, LANES)
    return pl.pallas_call(
        _inproj_kernel, grid=(n // tm,), in_specs=[row(d), full(w_main), full(wg)],
        out_specs=tuple(row(c) for c in widths),
        out_shape=tuple(jax.ShapeDtypeStruct((n, c), F32) for c in widths),
        compiler_params=_params("parallel"), name="inproj")(x, w_main, wg)


def _inproj_seq_kernel(*refs, per, first, ride):
    n_in = 11 + (2 if first else 0) + (STEP_INPUTS if ride else 0)
    x_ref, w_ref, wgt_ref, pw_ref, sc_ref, wg_ref, wu_ref, wd_ref, wo_ref, wq_ref, wa_ref = refs[:11]
    outs, carry_ref = refs[n_in:-1], refs[-1]
    (po_ref, tail_ref, q_ref, k_ref, v_ref, og_ref, git_ref, gft_ref,
     cg_ref, cu_ref, cd_ref, co_ref, cq_ref, ca_ref) = outs[:14]
    tm = x_ref.shape[0]
    start = lax.rem(pl.program_id(0), per) * tm

    @pl.when(start == 0)
    def _():
        carry_ref[...] = jnp.zeros_like(carry_ref)

    for src, dst in ((wg_ref, cg_ref), (wu_ref, cu_ref), (wd_ref, cd_ref), (wo_ref, co_ref), (wq_ref, cq_ref),
                     (wa_ref, ca_ref)):
        dst[...] = src[...].astype(BF16)
    if ride:
        _mlstm_step_kernel(*refs[11:11 + STEP_INPUTS], *outs[14:18])
    x = x_ref[...]
    if first:
        x = _layer_norm(x, refs[11][...], refs[12][...])
        outs[14][...] = x
        outs[15][...] = jnp.zeros_like(outs[15])

    xb = x.astype(BF16)
    u = _dot(xb, w_ref[:, 0:POOL_WIDTH])
    halo = carry_ref.shape[0]
    ext = jnp.concatenate([carry_ref[...], u], axis=0)
    tail = ext[tm:, :]
    carry_ref[...] = tail
    tail_ref[...] = tail
    pos = start + lax.broadcasted_iota(jnp.int32, (tm, POOL_GROUP_DIM), 0)
    for g, win in enumerate(POOL_WINDOWS):
        sl = slice(g * POOL_GROUP_DIM, (g + 1) * POOL_GROUP_DIM)
        acc = ext[:, sl]
        s = 1
        while s < win:
            acc = acc + pltpu.roll(acc, s, 0)
            s *= 2
        cnt = jnp.minimum(pos + 1, win).astype(F32)
        dlt = acc[halo:, :] / cnt - u[:, sl]
        po_ref[:, sl] = (_dot(dlt.astype(BF16), pw_ref[g]) * sc_ref[:, sl]).astype(po_ref.dtype)
    _qkvo(xb, w_ref, q_ref, k_ref, v_ref, og_ref)
    gt = _dot_nt(wgt_ref[...], xb)
    git_ref[...] = gt[0:SUBLANES]
    gft_ref[...] = gt[SUBLANES:2 * SUBLANES]


def _inproj_seq(x, w_main, wgt, pw, scale, experts, dense, layer, seq, tm, first=None, ride=None):
    n, d = x.shape
    per = seq // tm
    steps = n // tm
    split = steps // N_EXPERTS
    assert split * N_EXPERTS == steps and d % steps == 0 and d % split == 0 and D_EXPERT % split == 0
    row = lambda c: pl.BlockSpec((tm, c), lambda i: (i, 0))
    full = lambda a: pl.BlockSpec(a.shape, lambda i: (0,) * a.ndim)
    gt_spec = pl.BlockSpec((None, SUBLANES, tm), lambda i: (i // per, 0, i % per))
    gt_shape = jax.ShapeDtypeStruct((n // seq, SUBLANES, seq), F32)
    mw, e, gf = MLSTM_WIDTH, EXPERTS_PER_GROUP, EXPERTS_PER_GROUP * D_EXPERT
    dr, fr, wr = d // split, D_EXPERT // split, d // steps
    col_in = pl.BlockSpec((None, None, dr, D_EXPERT), lambda i: (layer, i // split, i % split, 0))
    row_in = pl.BlockSpec((None, None, fr, d), lambda i: (layer, i // split, i % split, 0))
    col_out = pl.BlockSpec((None, dr, D_EXPERT), lambda i: (i // split // e, i % split, (i // split) % e))
    row_out = pl.BlockSpec((None, fr, d), lambda i: (i // split // e, ((i // split) % e) * split + i % split, 0))
    dense_in = pl.BlockSpec((None, wr, d), lambda i: (layer, i, 0))
    dense_out = pl.BlockSpec((wr, d), lambda i: (i, 0))
    out_shape = [
        jax.ShapeDtypeStruct((n, POOL_WIDTH), BF16),
        jax.ShapeDtypeStruct((steps, POOL_HALO, POOL_WIDTH), F32),
        jax.ShapeDtypeStruct((n, mw), BF16), jax.ShapeDtypeStruct((n, mw), BF16), jax.ShapeDtypeStruct((n, mw), BF16),
        jax.ShapeDtypeStruct((n, mw), F32), gt_shape, gt_shape,
        jax.ShapeDtypeStruct((N_GROUPS, d, gf), BF16), jax.ShapeDtypeStruct((N_GROUPS, d, gf), BF16),
        jax.ShapeDtypeStruct((N_GROUPS, gf, d), BF16),
        jax.ShapeDtypeStruct((d, d), BF16), jax.ShapeDtypeStruct((d, d), BF16), jax.ShapeDtypeStruct((d, d), BF16),
    ]
    out_specs = [row(POOL_WIDTH), pl.BlockSpec((None, POOL_HALO, POOL_WIDTH), lambda i: (i, 0, 0)),
                 row(mw), row(mw), row(mw), row(mw), gt_spec, gt_spec,
                 col_out, col_out, row_out, dense_out, dense_out, dense_out]
    extra_in, extra_args = [], []
    if first is not None:
        ln_g, ln_b, zeros_like = first
        lead = zeros_like.shape[0] * zeros_like.shape[1]
        assert lead % steps == 0 and zeros_like.shape[1] % (lead // steps) == 0
        zb = lead // steps
        zper = zeros_like.shape[1] // zb
        out_shape += [jax.ShapeDtypeStruct((n, d), F32), jax.ShapeDtypeStruct(zeros_like.shape, zeros_like.dtype)]
        out_specs += [row(d), pl.BlockSpec((None, zb) + zeros_like.shape[2:],
                                           lambda i: (i // zper, i % zper) + (0,) * (zeros_like.ndim - 2))]
        extra_args = [ln_g.reshape(1, d), ln_b.reshape(1, d)]
        extra_in = [full(a) for a in extra_args]
    aliases = {}
    if ride is not None:
        assert first is None
        rq, rk, rv, rgi, rgf, rbi, rbf, c_all, n_all, rm, c_new = ride
        n_s = rq.shape[0]
        sb = n_s // steps
        assert sb * steps == n_s
        hd = MLSTM_HEAD_DIM
        r3 = lambda a: a.reshape(steps, sb, a.shape[-1])
        s3 = lambda c: pl.BlockSpec((None, sb, c), lambda i: (i, 0, 0))
        cspec = pl.BlockSpec((None, sb, MLSTM_HEADS, hd, hd), lambda i: (layer, i, 0, 0, 0))
        nspec = pl.BlockSpec((None, sb, MLSTM_HEADS, hd), lambda i: (layer, i, 0, 0))
        extra_args = [r3(rq), r3(rk), r3(rv), r3(rgi), r3(rgf), rbi, rbf, c_all, n_all, r3(rm), c_new]
        assert len(extra_args) == STEP_INPUTS
        extra_in = [s3(mw), s3(mw), s3(mw), s3(LANES), s3(LANES), full(rbi), full(rbf), cspec, nspec, s3(LANES),
                    pl.BlockSpec(memory_space=pl.ANY)]
        aliases = {11 + STEP_INPUTS - 1: len(out_shape) + 1}
        out_shape += [jax.ShapeDtypeStruct((steps, sb, mw), F32), jax.ShapeDtypeStruct(c_new.shape, F32),
                      jax.ShapeDtypeStruct(n_all.shape[1:], F32), jax.ShapeDtypeStruct((steps, sb, LANES), F32)]
        out_specs += [s3(mw), cspec, pl.BlockSpec((sb, MLSTM_HEADS, hd), lambda i: (i, 0, 0)), s3(LANES)]
    scale = scale.reshape(1, POOL_WIDTH)
    return pl.pallas_call(
        functools.partial(_inproj_seq_kernel, per=per, first=first is not None, ride=ride is not None), grid=(steps,),
        in_specs=[row(d), full(w_main), full(wgt), full(pw), full(scale), col_in, col_in, row_in,
                  dense_in, dense_in, dense_in] + extra_in,
        out_specs=tuple(out_specs), out_shape=tuple(out_shape),
        scratch_shapes=[pltpu.VMEM((POOL_HALO, POOL_WIDTH), F32)], input_output_aliases=aliases,
        compiler_params=_params("arbitrary"), name="inproj_seq")(x, w_main, wgt, pw, scale, *experts, *dense,
                                                                 *extra_args)


def _pool_kernel(ext_ref, pw_ref, sc_ref, o_ref, *, period):
    rows = ext_ref.shape[0]
    r = lax.broadcasted_iota(jnp.int32, (rows, POOL_GROUP_DIM), 0)
    if period != rows:
        r = lax.rem(r, period)
    for g, win in enumerate(POOL_WINDOWS):
        sl = slice(g * POOL_GROUP_DIM, (g + 1) * POOL_GROUP_DIM)
        x = ext_ref[:, sl]
        acc = x
        s = 1
        while s < win:
            acc = acc + jnp.where(r >= s, pltpu.roll(acc, s, 0), 0.0)
            s *= 2
        cnt = jnp.minimum(r + 1, win).astype(F32)
        d = acc / cnt - x
        y = _dot(d.astype(BF16), pw_ref[g]) * sc_ref[:, sl]
        o_ref[:, sl] = y.astype(o_ref.dtype)


def _pool(ext2d, pw, scale, period):
    n, c = ext2d.shape
    row = pl.BlockSpec((POOL_ROWS, c), lambda i: (i, 0))
    return pl.pallas_call(
        functools.partial(_pool_kernel, period=period), grid=(n // POOL_ROWS,),
        in_specs=[row, pl.BlockSpec(pw.shape, lambda i: (0, 0, 0)), pl.BlockSpec((1, c), lambda i: (0, 0))],
        out_specs=row, out_shape=jax.ShapeDtypeStruct((n, c), BF16),
        compiler_params=_params("parallel"), name="pool")(ext2d, pw, scale.reshape(1, c))


def _mlstm_chunk_kernel(q_ref, k_ref, v_ref, git_ref, gft_ref, bit_ref, bft_ref, hn_ref, c_ref, n_ref, m_ref,
                        at_s, ws_s, dc_s, mc_s, wi_s, fl_s):
    nb, L, _ = q_ref.shape
    rows = nb * SUBLANES

    @pl.when(pl.program_id(0) == 0)
    def _():
        c_ref[...] = jnp.zeros_like(c_ref)
        n_ref[...] = jnp.zeros_like(n_ref)
        m_ref[...] = jnp.zeros_like(m_ref)

    ig = git_ref[...].reshape(rows, L) + bit_ref[...]
    lf = _log_sigmoid(gft_ref[...].reshape(rows, L) + bft_ref[...])
    tri = (lax.broadcasted_iota(jnp.int32, (L, L), 0) <= lax.broadcasted_iota(jnp.int32, (L, L), 1)).astype(BF16)
    lf_hi = lf.astype(BF16)
    lf_mid = (lf - lf_hi.astype(F32)).astype(BF16)
    lf_lo = (lf - lf_hi.astype(F32) - lf_mid.astype(F32)).astype(BF16)
    bc = _dot(lf_hi, tri) + (_dot(lf_mid, tri) + _dot(lf_lo, tri))
    at = ig - bc
    m_prev = m_ref[...].reshape(rows, LANES)[:, 0:1]
    mc = jnp.maximum(_scan(at, 1, jnp.maximum, -jnp.inf), m_prev)
    mt = bc + mc
    m_last = mc[:, L - 1:L]
    at_s[...] = at.reshape(nb, SUBLANES, L)
    ws_s[...] = jnp.exp(at - m_last).reshape(nb, SUBLANES, L)
    dc_s[...] = jnp.broadcast_to(jnp.exp(m_prev - m_last), (rows, LANES)).reshape(nb, SUBLANES, LANES)
    m_ref[...] = jnp.broadcast_to(mt[:, L - 1:L], (rows, LANES)).reshape(nb, SUBLANES, LANES)
    wi = jnp.exp(m_prev - mc)
    fl = jnp.exp(-mt)
    for b in range(nb):
        rs = slice(b * SUBLANES, (b + 1) * SUBLANES)
        mc_s[b] = mc[rs].T
        wi_s[b] = wi[rs].T
        fl_s[b] = fl[rs].T

    causal = (lax.broadcasted_iota(jnp.int32, (L, L), 0) >= lax.broadcasted_iota(jnp.int32, (L, L), 1))

    def per_batch(heads, b, carry):
        at = at_s[b]
        ws = ws_s[b]
        decay = dc_s[b]
        mc_c, wi_c, fl_c = mc_s[b], wi_s[b], fl_s[b]
        ws16 = jnp.concatenate([ws, ws], axis=0).astype(BF16)
        for h in heads:
            sl = slice(h * MLSTM_HEAD_DIM, (h + 1) * MLSTM_HEAD_DIM)
            col = slice(h, h + 1)
            qh = q_ref[b, :, sl]
            kh = k_ref[b, :, sl]
            vh = v_ref[b, :, sl]
            s = _dot_nt(qh, kh)
            p = jnp.where(causal, s * jnp.exp(at[col, :] - mc_c[:, col]), 0.0)
            c_old = c_ref[b, h]
            n_old = n_ref[b, col, :]
            wi = wi_c[:, col]
            dv = MLSTM_HEAD_DIM
            v_ext = jnp.concatenate([vh, jnp.ones_like(vh)], axis=1)
            c_ext = jnp.concatenate([c_old, jnp.broadcast_to(n_old, c_old.shape)], axis=0)
            intra = _dot(p.astype(BF16), v_ext)
            inter = _dot_nt(qh, c_ext.astype(BF16))
            num = intra[:, :dv] + wi * inter[:, :dv]
            qn = intra[:, dv:] + wi * inter[:, dv:]
            hh = num / jnp.maximum(jnp.abs(qn), fl_c[:, col])
            hn_ref[b, :, sl] = _unit_norm(hh)
            dc = decay[col, 0:1]
            vts = (vh.astype(F32).T * ws[col, :]).astype(BF16)
            c_ref[b, h] = dc * c_old + _dot(vts, kh)
            n_ref[b, col, :] = dc * n_old + _dot(ws16, kh)[col, :]
        return carry

    for h0 in range(0, MLSTM_HEADS, 2):
        lax.fori_loop(0, nb, functools.partial(per_batch, (h0, h0 + 1)), 0)


def _mlstm_chunk(q, k, v, git, gft, bit, bft):
    nb, t, w = q.shape
    L = MLSTM_CHUNK
    seq = lambda c: pl.BlockSpec((nb, L, c), lambda i: (0, i, 0))
    seqt = pl.BlockSpec((nb, SUBLANES, L), lambda i: (0, 0, i))
    vec = lambda a: pl.BlockSpec(a.shape, lambda i: (0, 0))
    out_shape = (
        jax.ShapeDtypeStruct((nb, t, w), F32),
        jax.ShapeDtypeStruct((nb, MLSTM_HEADS, MLSTM_HEAD_DIM, MLSTM_HEAD_DIM), F32),
        jax.ShapeDtypeStruct((nb, MLSTM_HEADS, MLSTM_HEAD_DIM), F32),
        jax.ShapeDtypeStruct((nb, SUBLANES, LANES), F32),
    )
    out_specs = (
        seq(w),
        pl.BlockSpec(out_shape[1].shape, lambda i: (0, 0, 0, 0)),
        pl.BlockSpec(out_shape[2].shape, lambda i: (0, 0, 0)),
        pl.BlockSpec(out_shape[3].shape, lambda i: (0, 0, 0)),
    )
    bit, bft = jnp.tile(bit, (nb, 1)), jnp.tile(bft, (nb, 1))
    scratch = [pltpu.VMEM((nb, SUBLANES, L), F32), pltpu.VMEM((nb, SUBLANES, L), F32),
               pltpu.VMEM((nb, SUBLANES, LANES), F32)] + [pltpu.VMEM((nb, L, SUBLANES), F32) for _ in range(3)]
    return pl.pallas_call(
        _mlstm_chunk_kernel, grid=(t // L,),
        in_specs=[seq(w), seq(w), seq(w), seqt, seqt, vec(bit), vec(bft)],
        out_specs=out_specs, out_shape=out_shape, scratch_shapes=scratch, compiler_params=_params("arbitrary"),
        name="mlstm_chunk")(q, k, v, git, gft, bit, bft)


def _mlstm_step_kernel(q_ref, k_ref, v_ref, gi_ref, gf_ref, bi_ref, bf_ref, c_ref, n_ref, m_ref, c_new_ref,
                       hn_ref, co_ref, no_ref, mo_ref):
    del c_new_ref
    tb = q_ref.shape[0]
    d = MLSTM_HEAD_DIM
    ig = gi_ref[...] + bi_ref[...]
    lf = _log_sigmoid(gf_ref[...] + bf_ref[...])
    m_old = m_ref[...]
    mt = jnp.maximum(lf + m_old, ig)
    mo_ref[...] = mt
    wa = jnp.exp(ig - mt)
    wi = jnp.exp(lf + m_old - mt)
    fl = jnp.exp(-mt)
    wa_t, wi_t, fl_t = wa.T, wi.T, fl.T
    lane = lax.broadcasted_iota(jnp.int32, (d, tb), 1)
    for h in range(MLSTM_HEADS):
        sl = slice(h * d, (h + 1) * d)
        qh, kh, vh = q_ref[:, sl], k_ref[:, sl], v_ref[:, sl]
        nh = n_ref[:, h, :]
        qt, kt, vt, nt = qh.T, kh.T, vh.T, nh.T
        wa_r, wi_r, fl_r = wa_t[h:h + 1, :], wi_t[h:h + 1, :], fl_t[h:h + 1, :]
        s = jnp.sum(qt * kt, 0, keepdims=True) * wa_r
        cq = jnp.zeros((d, tb), F32)
        qtb = qt.astype(BF16)
        for b in range(tb):
            cq = jnp.where(lane == b, _dot(c_ref[b, h].astype(BF16), qtb), cq)
        num = s * vt + wi_r * cq
        qn = s + wi_r * jnp.sum(nt * qt, 0, keepdims=True)
        hh = num / jnp.maximum(jnp.abs(qn), fl_r)
        mu = jnp.mean(hh, 0, keepdims=True)
        xc = hh - mu
        var = jnp.mean(xc * xc, 0, keepdims=True)
        hn_ref[:, sl] = (xc * lax.rsqrt(var + LN_EPS)).T
        wav = wa_r * vt
        for b in range(tb):
            co_ref[b, h] = wi_r[:, b:b + 1] * c_ref[b, h] + wav[:, b:b + 1] * kh[b:b + 1, :]
        no_ref[:, h, :] = wi[:, h:h + 1] * nh + wa[:, h:h + 1] * kh


def _mlstm_step(q, k, v, gi, gf, bi, bf, c_all, n_all, m0, c_new, layer, tb):
    n, w = q.shape
    d = MLSTM_HEAD_DIM
    row = lambda c: pl.BlockSpec((tb, c), lambda i: (i, 0))
    vec = lambda a: pl.BlockSpec(a.shape, lambda i: (0, 0))
    cin = pl.BlockSpec((None, tb, MLSTM_HEADS, d, d), lambda i: (layer, i, 0, 0, 0))
    nin = pl.BlockSpec((None, tb, MLSTM_HEADS, d), lambda i: (layer, i, 0, 0))
    nout = pl.BlockSpec((tb, MLSTM_HEADS, d), lambda i: (i, 0, 0))
    out_shape = (
        jax.ShapeDtypeStruct((n, w), F32),
        jax.ShapeDtypeStruct(c_new.shape, F32),
        jax.ShapeDtypeStruct(n_all.shape[1:], F32),
        jax.ShapeDtypeStruct((n, LANES), F32),
    )
    return pl.pallas_call(
        _mlstm_step_kernel, grid=(n // tb,),
        in_specs=[row(w), row(w), row(w), row(LANES), row(LANES), vec(bi), vec(bf), cin, nin, row(LANES),
                  pl.BlockSpec(memory_space=pl.ANY)],
        out_specs=(row(w), cin, nout, row(LANES)), out_shape=out_shape, input_output_aliases={10: 1},
        compiler_params=_params("parallel"), name="mlstm_step")(q, k, v, gi, gf, bi, bf, c_all, n_all, m0, c_new)


def _mixout_kernel(po_ref, hn_ref, og_ref, x_ref, ng_ref, w_ref, g_ref, b_ref, o_ref):
    mo = jax.nn.sigmoid(og_ref[...]) * hn_ref[...] * ng_ref[...]
    mix = _dot(po_ref[...], w_ref[0:POOL_WIDTH, :]) + _dot(mo.astype(BF16), w_ref[POOL_WIDTH:, :])
    o_ref[...] = _layer_norm(ALPHA * x_ref[...] + mix, g_ref[...], b_ref[...])


def _mixout(po, hn, og, x, ng, w, g, b, tm):
    n, d = x.shape
    row = lambda c: pl.BlockSpec((tm, c), lambda i: (i, 0))
    vec = lambda a: pl.BlockSpec(a.shape, lambda i: (0, 0))
    return pl.pallas_call(
        _mixout_kernel, grid=(n // tm,),
        in_specs=[row(POOL_WIDTH), row(MLSTM_WIDTH), row(MLSTM_WIDTH), row(d), vec(ng), vec(w), vec(g), vec(b)],
        out_specs=row(d), out_shape=jax.ShapeDtypeStruct((n, d), F32),
        compiler_params=_params("parallel"), name="mixout")(po, hn, og, x, ng, w, g, b)


def _mm_kernel(x_ref, w_ref, o_ref, *, scale):
    y = _dot(x_ref[...].astype(BF16), w_ref[...])
    if scale != 1.0:
        y = y * scale
    o_ref[...] = y.astype(o_ref.dtype)


def _mm(x, w, tm, out_dtype, scale=1.0, name="proj"):
    n, d = x.shape
    dout = w.shape[1]
    return pl.pallas_call(
        functools.partial(_mm_kernel, scale=scale), grid=(n // tm,),
        in_specs=[pl.BlockSpec((tm, d), lambda i: (i, 0)), pl.BlockSpec(w.shape, lambda i: (0, 0))],
        out_specs=pl.BlockSpec((tm, dout), lambda i: (i, 0)),
        out_shape=jax.ShapeDtypeStruct((n, dout), out_dtype),
        compiler_params=_params("parallel"), name=name)(x, w)


def _cast_inproj_kernel(wt_ref, o_ref, og_ref, *, n_main):
    i = pl.program_id(0)

    @pl.when(i < n_main)
    def _():
        o_ref[...] = wt_ref[...].T.astype(BF16)

    @pl.when(i == n_main)
    def _():
        og_ref[...] = wt_ref[0:2 * MLSTM_HEADS, :]


def _cast_inproj(w_in, layer, tc=256):
    wt = jnp.swapaxes(w_in, 1, 2)
    d = wt.shape[2]
    n_main = GATE_OFF // tc
    return pl.pallas_call(
        functools.partial(_cast_inproj_kernel, n_main=n_main), grid=(n_main + 1,),
        in_specs=[pl.BlockSpec((None, tc, d), lambda i: (layer, i, 0))],
        out_specs=(pl.BlockSpec((d, tc), lambda i: (0, jnp.minimum(i, n_main - 1))),
                   pl.BlockSpec((2 * MLSTM_HEADS, d), lambda i: (0, 0))),
        out_shape=(jax.ShapeDtypeStruct((d, GATE_OFF), BF16), jax.ShapeDtypeStruct((2 * MLSTM_HEADS, d), F32)),
        compiler_params=_params("arbitrary"), name="inproj_cast")(wt)


def _mem_proj_kernel(x_ref, wk_ref, wv_ref, k4_ref, v4_ref, k2_ref, v2_ref, wkb, wvb):
    @pl.when(pl.program_id(1) == 0)
    def _():
        wkb[...] = wk_ref[...].astype(BF16)
        wvb[...] = wv_ref[...].astype(BF16)

    bb, m, d = x_ref.shape
    xb = x_ref[...].reshape(bb * m, d).astype(BF16)
    k = _dot(xb, wkb[...])
    v = _dot(xb, wvb[...])
    k2_ref[...] = k.reshape(bb, m, d).astype(k2_ref.dtype)
    v2_ref[...] = v.reshape(bb, m, d).astype(v2_ref.dtype)
    k4_ref[...] = k.reshape(bb, m, CA_HEADS, CA_HEAD_DIM)
    v4_ref[...] = v.reshape(bb, m, CA_HEADS, CA_HEAD_DIM)


def _mem_proj(mem, wk_all, wv_all, bb=2):
    nb, m, d = mem.shape
    depth = wk_all.shape[0]
    wspec = pl.BlockSpec((None, d, d), lambda l, b: (l, 0, 0))
    o4 = pl.BlockSpec((None, bb, m, CA_HEADS, CA_HEAD_DIM), lambda l, b: (l, b, 0, 0, 0))
    o2 = pl.BlockSpec((None, bb, m, d), lambda l, b: (l, b, 0, 0))
    s4 = jax.ShapeDtypeStruct((depth, nb, m, CA_HEADS, CA_HEAD_DIM), F32)
    s2 = jax.ShapeDtypeStruct((depth, nb, m, d), BF16)
    return pl.pallas_call(
        _mem_proj_kernel, grid=(depth, nb // bb),
        in_specs=[pl.BlockSpec((bb, m, d), lambda l, b: (b, 0, 0)), wspec, wspec],
        out_specs=(o4, o4, o2, o2), out_shape=(s4, s4, s2, s2),
        scratch_shapes=[pltpu.VMEM((d, d), BF16), pltpu.VMEM((d, d), BF16)],
        compiler_params=_params("arbitrary", "arbitrary"), name="mem_proj")(mem, wk_all, wv_all)


def _attn_block_kernel(po_ref, hn_ref, og_ref, x_ref, k_ref, v_ref, ng_ref, wm_ref, g1_ref, b1_ref,
                       wq_ref, wo_ref, g_ref, b_ref, o_ref):
    mo = jax.nn.sigmoid(og_ref[...]) * hn_ref[...] * ng_ref[...]
    mix = _dot(po_ref[...], wm_ref[0:POOL_WIDTH, :]) + _dot(mo.astype(BF16), wm_ref[POOL_WIDTH:, :])
    x = _layer_norm(ALPHA * x_ref[...] + mix, g1_ref[...], b1_ref[...])
    qb = (_dot(x.astype(BF16), wq_ref[...]) * (CA_HEAD_DIM ** -0.5)).astype(BF16)
    kb = k_ref[...].astype(BF16)
    vb = v_ref[...].astype(BF16)
    ctx = []
    for h in range(CA_HEADS):
        sl = slice(h * CA_HEAD_DIM, (h + 1) * CA_HEAD_DIM)
        s = _dot_nt(qb[:, sl], kb[:, sl])
        e = jnp.exp(s - jnp.max(s, -1, keepdims=True))
        ctx.append((_dot(e.astype(BF16), vb[:, sl]) / jnp.sum(e, -1, keepdims=True)).astype(BF16))
    y = _dot(jnp.concatenate(ctx, axis=1), wo_ref[...])
    o_ref[...] = _layer_norm(ALPHA * x + y, g_ref[...], b_ref[...])


def _attn_block(po, hn, og, x, k, v, layer, ng, wm, ln1, wq, wo, ln2, seq, tq):
    n, d = x.shape
    per = seq // tq
    row = lambda c: pl.BlockSpec((tq, c), lambda i: (i, 0))
    kspec = pl.BlockSpec((None, None, N_MEM, d), lambda i: (layer, i // per, 0, 0))
    vec = lambda a: pl.BlockSpec(a.shape, lambda i: (0, 0))
    return pl.pallas_call(
        _attn_block_kernel, grid=(n // tq,),
        in_specs=[row(POOL_WIDTH), row(MLSTM_WIDTH), row(MLSTM_WIDTH), row(d), kspec, kspec, vec(ng), vec(wm),
                  vec(ln1[0]), vec(ln1[1]), vec(wq), vec(wo), vec(ln2[0]), vec(ln2[1])],
        out_specs=row(d), out_shape=jax.ShapeDtypeStruct((n, d), F32),
        compiler_params=_params("parallel"), name="attn_block")(po, hn, og, x, k, v, ng, wm, *ln1, wq, wo, *ln2)


def _attn_step_kernel(q_ref, k_ref, v_ref, o_ref):
    bb = q_ref.shape[0]
    rows = N_MEM * CA_HEADS
    lane = lax.broadcasted_iota(jnp.int32, (SUBLANES, rows), 1)
    row = lax.broadcasted_iota(jnp.int32, (SUBLANES, rows), 0)
    own = lax.rem(lane, CA_HEADS) == lax.rem(row, CA_HEADS)
    for j in range(bb):
        kf = k_ref[j].reshape(rows, CA_HEAD_DIM).astype(BF16)
        vf = v_ref[j].reshape(rows, CA_HEAD_DIM).astype(BF16)
        s = jnp.where(own, _dot_nt(q_ref[j].astype(BF16), kf), -jnp.inf)
        e = jnp.exp(s - jnp.max(s, -1, keepdims=True))
        o_ref[j] = _dot(e.astype(BF16), vf) / jnp.sum(e, -1, keepdims=True)


def _mm_res_ln_kernel(a_ref, x_ref, w_ref, g_ref, b_ref, o_ref):
    y = _dot(a_ref[...].astype(BF16), w_ref[...])
    o_ref[...] = _layer_norm(ALPHA * x_ref[...] + y, g_ref[...], b_ref[...])


def _mm_res_ln(a, x, w, g, b, tm):
    n, d = x.shape
    row = lambda c: pl.BlockSpec((tm, c), lambda i: (i, 0))
    vec = lambda arr: pl.BlockSpec(arr.shape, lambda i: (0, 0))
    return pl.pallas_call(
        _mm_res_ln_kernel, grid=(n // tm,),
        in_specs=[row(a.shape[1]), row(d), vec(w), vec(g), vec(b)], out_specs=row(d),
        out_shape=jax.ShapeDtypeStruct((n, d), F32),
        compiler_params=_params("parallel"), name="proj_res_ln")(a, x, w, g, b)


def _route(lt):
    gl = [lt[g:g + 1, :] for g in range(N_GROUPS)]
    gmax = functools.reduce(jnp.maximum, gl)
    gsum = functools.reduce(jnp.add, [jnp.exp(x - gmax) for x in gl])
    pg_sel = 1.0 / gsum

    def first_max(vals):
        m = functools.reduce(jnp.maximum, vals)
        taken = jnp.zeros_like(m, dtype=jnp.bool_)
        hot = []
        for x in vals:
            h = jnp.logical_and(x == m, jnp.logical_not(taken))
            taken = jnp.logical_or(taken, h)
            hot.append(h)
        return m, hot

    _, g_hot = first_max(gl)
    el = []
    for j in range(EXPERTS_PER_GROUP):
        rows = [lt[SUBLANES + g * EXPERTS_PER_GROUP + j:SUBLANES + g * EXPERTS_PER_GROUP + j + 1, :]
                for g in range(N_GROUPS)]
        x = rows[N_GROUPS - 1]
        for g in range(N_GROUPS - 2, -1, -1):
            x = jnp.where(g_hot[g], rows[g], x)
        el.append(x)
    emax = functools.reduce(jnp.maximum, el)
    ee = [jnp.exp(x - emax) for x in el]
    esum = functools.reduce(jnp.add, ee)
    pe = [x / esum for x in ee]
    p1, hot1 = first_max(pe)
    p2, hot2 = first_max([jnp.where(h, -jnp.inf, x) for h, x in zip(hot1, pe)])
    psum = p1 + p2
    gate = [jnp.where(h1, pg_sel * p1 / psum, jnp.where(h2, pg_sel * p2 / psum, 0.0)) for h1, h2 in zip(hot1, hot2)]
    return g_hot, gate


def _moe_kernel(x_ref, wr_ref, br_ref, tri_ref, wg_ref, wu_ref, wd_ref, g_ref, b_ref, *rest):
    ride = None
    if len(rest) > 7:
        ride = functools.partial(_attn_step_kernel, rest[0], rest[1], rest[2], rest[4])
        rest = rest[3:4] + rest[5:]
    o_ref, xb_ref, ct_ref, comb_ref, perm_ref, acc_ref, full_ref = rest
    grp = pl.program_id(1)
    tm = x_ref.shape[0]
    ns, cap, sw = perm_ref.shape[1:]
    f = D_EXPERT

    @pl.when(grp == 0)
    def _():
        x = x_ref[...]
        xh = x.astype(BF16)
        xb_ref[...] = xh
        xl = (x - xh.astype(F32)).astype(BF16)
        wr = wr_ref[...]
        wh = wr.astype(BF16)
        wl = (wr - wh.astype(F32)).astype(BF16)
        lt = _dot_nt(wh, xh) + (_dot_nt(wl, xh) + _dot_nt(wh, xl)) + br_ref[...]
        g_hot, gate = _route(lt)
        row8 = lax.broadcasted_iota(jnp.int32, (SUBLANES, tm), 0)
        hot8 = jnp.zeros((SUBLANES, tm), F32)
        for g in range(N_GROUPS):
            hot8 = jnp.where(jnp.logical_and(row8 == g, g_hot[g]), 1.0, hot8)
        most = None
        for s in range(ns):
            hs = hot8[:, s * sw:(s + 1) * sw]
            cum = _dot(hs.astype(BF16), tri_ref[...])
            seg_max = jnp.max(cum[:, sw - 1:sw])
            most = seg_max if most is None else jnp.maximum(most, seg_max)
            base = lax.broadcasted_iota(jnp.int32, (SUBLANES, sw), 0).astype(F32) * cap
            code = jnp.sum(hs * (base + cum - 1.0), 0, keepdims=True).astype(jnp.int32)
            for g in range(N_GROUPS):
                r = lax.broadcasted_iota(jnp.int32, (cap, sw), 0) + g * cap
                perm_ref[g, s] = jnp.where(r == code, 1.0, 0.0).astype(BF16)
        full_ref[0] = (most > cap).astype(jnp.int32)
        ct_ref[...] = jnp.zeros_like(ct_ref)
        for j in range(EXPERTS_PER_GROUP):
            ct_ref[j:j + 1, :] = gate[j]
        ct_ref[EXPERTS_PER_GROUP:EXPERTS_PER_GROUP + 1, :] = jnp.sum(hot8 * row8.astype(F32), 0, keepdims=True)
        comb_ref[...] = ct_ref[...].T
        acc_ref[...] = jnp.zeros_like(acc_ref)

    def experts(rows, gates):
        hh = []
        for j in range(EXPERTS_PER_GROUP):
            hg = _dot(rows, wg_ref[:, j * f:(j + 1) * f])
            hu = _dot(rows, wu_ref[:, j * f:(j + 1) * f])
            hh.append((hg * jax.nn.sigmoid(hg) * hu * gates[:, j:j + 1]).astype(BF16))
        return _dot(jnp.concatenate(hh, axis=1), wd_ref[...])

    @pl.when(full_ref[0] == 0)
    def _():
        comb = comb_ref[...]
        ch = comb.astype(BF16)
        cl = (comb - ch.astype(F32)).astype(BF16)
        xb = xb_ref[...]
        perm = [perm_ref[grp, s] for s in range(ns)]
        seg = lambda a, s: a[s * sw:(s + 1) * sw]
        rows = jnp.concatenate([_dot(perm[s], seg(xb, s)) for s in range(ns)], axis=0).astype(BF16)
        gates = jnp.concatenate([_dot(perm[s], seg(ch, s)) + _dot(perm[s], seg(cl, s)) for s in range(ns)], axis=0)
        y = experts(rows, gates).astype(BF16)
        for s in range(ns):
            acc_ref[s * sw:(s + 1) * sw, :] += _dot_tn(perm[s], y[s * cap:(s + 1) * cap])
        if ride is not None:
            ride()

    @pl.when(full_ref[0] != 0)
    def _():
        comb = comb_ref[...]
        own = comb[:, EXPERTS_PER_GROUP:EXPERTS_PER_GROUP + 1] == grp.astype(F32)
        acc_ref[...] += experts(xb_ref[...], jnp.where(own, comb, 0.0))
        if ride is not None:
            ride()

    @pl.when(grp == N_GROUPS - 1)
    def _():
        o_ref[...] = _layer_norm(ALPHA * x_ref[...] + acc_ref[...], g_ref[...], b_ref[...])


def _moe(x, wr, br, wg, wu, wd, g, b, tm, attn=None):
    n, d = x.shape
    ns = max(1, tm // MOE_SEGMENT)
    sw = tm // ns
    cap = min(sw, MOE_CAP)
    tri = (lax.broadcasted_iota(jnp.int32, (sw, sw), 0) <= lax.broadcasted_iota(jnp.int32, (sw, sw), 1)).astype(BF16)
    row = pl.BlockSpec((tm, d), lambda i, e: (i, 0))
    vec = lambda a: pl.BlockSpec(a.shape, lambda i, e: (0, 0))
    grp = lambda a: pl.BlockSpec((None,) + a.shape[1:], lambda i, e: (e, 0, 0))
    scratch = [pltpu.VMEM((tm, d), BF16), pltpu.VMEM((LANES, tm), F32), pltpu.VMEM((tm, LANES), F32),
               pltpu.VMEM((N_GROUPS, ns, cap, sw), BF16), pltpu.VMEM((tm, d), F32), pltpu.SMEM((1,), jnp.int32)]
    in_specs = [row, vec(wr), vec(br), vec(tri), grp(wg), grp(wu), grp(wd), vec(g), vec(b)]
    args = [x, wr, br, tri, wg, wu, wd, g, b]
    out_specs, out_shape = row, jax.ShapeDtypeStruct((n, d), F32)
    if attn is not None:
        q, k_all, v_all, layer = attn
        steps = (n // tm) * N_GROUPS
        bb = q.shape[0] // steps
        assert bb * steps == q.shape[0]
        qspec = pl.BlockSpec((bb, SUBLANES, CA_HEAD_DIM), lambda i, e: (i * N_GROUPS + e, 0, 0))
        kspec = pl.BlockSpec((None, bb, N_MEM, CA_HEADS, CA_HEAD_DIM), lambda i, e: (layer, i * N_GROUPS + e, 0, 0, 0))
        in_specs += [qspec, kspec, kspec]
        args += [q, k_all, v_all]
        out_specs, out_shape = (row, qspec), (out_shape, jax.ShapeDtypeStruct(q.shape, F32))
    return pl.pallas_call(
        _moe_kernel, grid=(n // tm, N_GROUPS), in_specs=in_specs, out_specs=out_specs, out_shape=out_shape,
        scratch_shapes=scratch, compiler_params=_params("arbitrary", "arbitrary"), name="moe")(*args)


def _pad_lanes(v):
    return jnp.zeros((1, LANES), F32).at[0, :v.shape[0]].set(v)


def _pad_rows(v):
    return jnp.zeros((SUBLANES, 1), F32).at[:v.shape[0], 0].set(v)


def _layer_weights(l, w_in, b_i, b_f, pool_w, pool_scale, mlstm_norm_g, w_out, ln1_g, ln1_b, ca_wq, ca_wo,
                   ln2_g, ln2_b, w_gr, b_gr, w_er, b_er, w_gate, w_up, w_down, ln3_g, ln3_b):
    d = D_MODEL
    w_main, w_gate_rows = _cast_inproj(w_in, l)
    w_gate_cols = w_gate_rows.T
    wg = jnp.zeros((d, 2 * LANES), F32)
    wg = wg.at[:, 0:MLSTM_HEADS].set(w_gate_cols[:, :MLSTM_HEADS])
    wg = wg.at[:, LANES:LANES + MLSTM_HEADS].set(w_gate_cols[:, MLSTM_HEADS:])
    wgt = jnp.zeros((2 * SUBLANES, d), F32)
    wgt = wgt.at[0:MLSTM_HEADS].set(w_gate_cols[:, :MLSTM_HEADS].T)
    wgt = wgt.at[SUBLANES:SUBLANES + MLSTM_HEADS].set(w_gate_cols[:, MLSTM_HEADS:].T)
    wr = jnp.zeros((ROUTER_ROWS, d), F32)
    wr = wr.at[0:N_GROUPS].set(w_gr[l].T).at[SUBLANES:SUBLANES + N_EXPERTS].set(w_er[l].T)
    br = jnp.zeros((ROUTER_ROWS, 1), F32)
    br = br.at[0:N_GROUPS, 0].set(b_gr[l]).at[SUBLANES:SUBLANES + N_EXPERTS, 0].set(b_er[l])
    row = lambda v: v.reshape(1, -1)
    return dict(
        w_main=w_main, wg=wg.astype(BF16), wgt=wgt.astype(BF16),
        bi=_pad_lanes(b_i[l]), bf=_pad_lanes(b_f[l]), bit=_pad_rows(b_i[l]), bft=_pad_rows(b_f[l]),
        pool_w=pool_w[l].astype(BF16), pool_scale=pool_scale[l], norm_g=row(mlstm_norm_g[l]),
        ln1=(row(ln1_g[l]), row(ln1_b[l])), ln2=(row(ln2_g[l]), row(ln2_b[l])), ln3=(row(ln3_g[l]), row(ln3_b[l])),
        wr=wr, br=br, raw_experts=(w_gate, w_up, w_down), raw_dense=(w_out, ca_wq, ca_wo))


def _prompt_mixers(x, nb, seq, layer, mem, p, tm, tm_big, ride=None):
    if ride is not None:
        tm = tm_big
    per = seq // tm
    outs = _inproj_seq(x, p["w_main"], p["wgt"], p["pool_w"], p["pool_scale"], p["raw_experts"], p["raw_dense"],
                       layer, seq, tm, first=p.get("first"), ride=ride)
    if ride is not None:
        hn_s, c_s, n_s, m_s = outs[14:18]
        p["step"] = (hn_s.reshape(-1, MLSTM_WIDTH), c_s, n_s, m_s.reshape(-1, LANES))
    pooled, tails, q, k, v, og, git, gft = outs[:8]
    p["experts"] = outs[8:11]
    p["w_out"], p["wq"], p["wo"] = outs[11:14]
    if p.get("first") is not None:
        x, p["zeros"] = outs[14:16]
    pool_buf = tails.reshape(nb, per, POOL_HALO, POOL_WIDTH)[:, per - 1, POOL_HALO - POOL_BUF:]
    r3 = lambda a: a.reshape(nb, seq, a.shape[-1])
    hn, c1, n1, m1 = _mlstm_chunk(r3(q), r3(k), r3(v), git, gft, p["bit"], p["bft"])
    x = _attn_block(pooled, hn.reshape(nb * seq, MLSTM_WIDTH), og, x, mem[0], mem[1], layer, p["norm_g"], p["w_out"],
                    p["ln1"], p["wq"], p["wo"], p["ln2"], seq, tm_big)
    return x, pool_buf, c1, n1, m1[:, :MLSTM_HEADS, 0]


def _sample_inproj(x, nb, state, p):
    u, q, k, v, og, gi, gf = _inproj(x, p["w_main"], p["wg"], nb)
    c_all, n_all, m0, c_new = state
    m0p = jnp.zeros((nb, LANES), F32).at[:, :MLSTM_HEADS].set(m0)
    return u, og, (q, k, v, gi, gf, p["bi"], p["bf"], c_all, n_all, m0p, c_new)


def _sample_mixers(x, nb, layer, pool_prev, u, og, step, p):
    ext = jnp.concatenate([pool_prev, u.reshape(nb, 1, POOL_WIDTH)], axis=1)
    period = ext.shape[1]
    pooled = _pool(ext.reshape(nb * period, POOL_WIDTH), p["pool_w"], p["pool_scale"], period)
    pooled = pooled.reshape(nb, period, POOL_WIDTH)[:, period - 1]
    hn, c1, n1, m1 = p["step"] if "step" in p else _mlstm_step(*step, layer, MLSTM_STEP_SEQS)
    x = _mixout(pooled, hn, og, x, p["norm_g"], p["w_out"], *p["ln1"], nb)
    qc = _mm(x, p["wq"], nb, F32, scale=CA_HEAD_DIM ** -0.5, name="ca_q")
    qh = jnp.zeros((nb, SUBLANES, CA_HEAD_DIM), F32).at[:, :CA_HEADS].set(qc.reshape(nb, CA_HEADS, CA_HEAD_DIM))
    return x, qh, ext[:, -POOL_BUF:], c1, n1, m1[:, :MLSTM_HEADS]


def kernel(x_prompt, x_sample, mem_prompt, cache_pool, state_mlstm_C, state_mlstm_n, state_mlstm_m,
           cache_mem_k, cache_mem_v, emb_ln_g, emb_ln_b, w_in, b_i, b_f, pool_w, pool_scale,
           mlstm_norm_g, w_out, ln1_g, ln1_b, ca_wq, ca_wk, ca_wv, ca_wo, ln2_g, ln2_b,
           w_gr, b_gr, w_er, b_er, w_gate, w_up, w_down, ln3_g, ln3_b):
    bp, tp, d = x_prompt.shape
    bs, ts, _ = x_sample.shape
    tm_p, tm_s = 512, bs * ts
    xp = x_prompt.reshape(bp * tp, d)
    xs = _ln(x_sample.reshape(bs * ts, d), emb_ln_g, emb_ln_b, tm_s)
    mk4, mv4, mk2, mv2 = _mem_proj(mem_prompt, ca_wk, ca_wv)
    outs = [[] for _ in range(7)]
    sc = None
    for l in range(DEPTH):
        p = _layer_weights(l, w_in, b_i, b_f, pool_w, pool_scale, mlstm_norm_g, w_out, ln1_g, ln1_b, ca_wq,
                           ca_wo, ln2_g, ln2_b, w_gr, b_gr, w_er, b_er, w_gate, w_up, w_down, ln3_g, ln3_b)
        if l == 0:
            p["first"] = (emb_ln_g, emb_ln_b, state_mlstm_C)
        if l > 0:
            u_s, og_s, step = _sample_inproj(xs, bs * ts, (state_mlstm_C, state_mlstm_n, state_mlstm_m[l], sc), p)
        xp, pb, c1, n1, m1 = _prompt_mixers(xp, bp, tp, l, (mk2, mv2), p, tm_p, 2 * tm_p, ride=step if l > 0 else None)
        if l == 0:
            sc = p["zeros"]
            u_s, og_s, step = _sample_inproj(xs, bs * ts, (state_mlstm_C, state_mlstm_n, state_mlstm_m[l], sc), p)
        xs, qh, sb, sc, ns, ms = _sample_mixers(xs, bs * ts, l, cache_pool[l], u_s, og_s, step, p)
        xp, ctx = _moe(xp, p["wr"], p["br"], *p["experts"], *p["ln3"], 2 * tm_p, attn=(qh, cache_mem_k, cache_mem_v, l))
        xs = _mm_res_ln(ctx[:, :CA_HEADS].reshape(bs * ts, d), xs, p["wo"], *p["ln2"], tm_s)
        xs = _moe(xs, p["wr"], p["br"], *p["experts"], *p["ln3"], tm_s)
        for lst, val in zip(outs, (pb, c1, n1, m1, sb, ns, ms)):
            lst.append(val)
    pp, pc, pn, pm, sp, sn, sm = (jnp.stack(o) for o in outs)
    return (xp.reshape(bp, tp, d), xs.reshape(bs, ts, d), pp, pc, pn, pm, mk4, mv4, sp, sc, sn, sm)
```

```python
import functools

import jax
import jax.numpy as jnp
from jax import lax
from jax.experimental import pallas as pl
from jax.experimental.pallas import tpu as pltpu

F32 = jnp.float32
BF16 = jnp.bfloat16

D_MODEL = 1024
DEPTH = 4
POOL_WIDTH = 512
POOL_GROUPS = 4
POOL_GROUP_DIM = 128
POOL_WINDOWS = (2, 4, 8, 16)
POOL_BUF = 15
MLSTM_WIDTH = 512
MLSTM_HEADS = 4
MLSTM_HEAD_DIM = 128
N_MEM = 256
CA_HEADS = 4
CA_HEAD_DIM = 256
N_GROUPS = 4
EXPERTS_PER_GROUP = 4
N_EXPERTS = 16
D_EXPERT = 256
ALPHA = (2 * DEPTH) ** 0.25
LN_EPS = 1e-5
GATE_OFF = POOL_WIDTH + 4 * MLSTM_WIDTH

LANES = 128
SUBLANES = 8
VMEM_LIMIT = 56 * 1024 * 1024
MLSTM_CHUNK = 256
POOL_ROWS = 2048
MLSTM_STEP_SEQS = 16
STEP_INPUTS = 11
POOL_HALO = 16
ROUTER_ROWS = 32
MOE_SEGMENT = 512
MOE_CAP = 160


def _params(*sem):
    return pltpu.CompilerParams(dimension_semantics=sem, vmem_limit_bytes=VMEM_LIMIT)


def _dot(a, b):
    return jnp.dot(a, b, preferred_element_type=F32)


def _dot_nt(a, b, precision=None):
    return lax.dot_general(a, b, (((1,), (1,)), ((), ())), precision=precision,
                           preferred_element_type=F32)


def _dot_tn(a, b):
    return lax.dot_general(a, b, (((0,), (0,)), ((), ())), preferred_element_type=F32)


def _layer_norm(x, g, b):
    mu = jnp.mean(x, -1, keepdims=True)
    xc = x - mu
    var = jnp.mean(xc * xc, -1, keepdims=True)
    return xc * lax.rsqrt(var + LN_EPS) * g + b


def _unit_norm(x):
    mu = jnp.mean(x, -1, keepdims=True)
    xc = x - mu
    var = jnp.mean(xc * xc, -1, keepdims=True)
    return xc * lax.rsqrt(var + LN_EPS)


def _log_sigmoid(x):
    return jnp.minimum(x, 0.0) - jnp.log1p(jnp.exp(-jnp.abs(x)))


def _scan(x, axis, op, fill):
    n = x.shape[axis]
    idx = lax.broadcasted_iota(jnp.int32, x.shape, axis)
    s = 1
    while s < n:
        x = op(x, jnp.where(idx >= s, pltpu.roll(x, s, axis), fill))
        s *= 2
    return x


def _ln_kernel(x_ref, g_ref, b_ref, o_ref):
    o_ref[...] = _layer_norm(x_ref[...], g_ref[...], b_ref[...])


def _ln(x, g, b, tm):
    n, d = x.shape
    row = pl.BlockSpec((tm, d), lambda i: (i, 0))
    vec = pl.BlockSpec((1, d), lambda i: (0, 0))
    return pl.pallas_call(
        _ln_kernel, grid=(n // tm,), in_specs=[row, vec, vec], out_specs=row,
        out_shape=jax.ShapeDtypeStruct((n, d), F32), compiler_params=_params("parallel"),
        name="emb_ln")(x, g.reshape(1, d), b.reshape(1, d))


def _qkvo(xb, w_ref, q_ref, k_ref, v_ref, og_ref):
    w = MLSTM_WIDTH
    q = _dot(xb, w_ref[:, POOL_WIDTH:POOL_WIDTH + w]) * (MLSTM_HEAD_DIM ** -0.5)
    q_ref[...] = q.astype(q_ref.dtype)
    k_ref[...] = _dot(xb, w_ref[:, POOL_WIDTH + w:POOL_WIDTH + 2 * w]).astype(k_ref.dtype)
    v_ref[...] = _dot(xb, w_ref[:, POOL_WIDTH + 2 * w:POOL_WIDTH + 3 * w]).astype(v_ref.dtype)
    og_ref[...] = _dot(xb, w_ref[:, POOL_WIDTH + 3 * w:POOL_WIDTH + 4 * w])


def _inproj_kernel(x_ref, w_ref, wg_ref, u_ref, q_ref, k_ref, v_ref, og_ref, gi_ref, gf_ref):
    xb = x_ref[...].astype(BF16)
    u_ref[...] = _dot(xb, w_ref[:, 0:POOL_WIDTH])
    _qkvo(xb, w_ref, q_ref, k_ref, v_ref, og_ref)
    g = _dot(xb, wg_ref[...])
    gi_ref[...] = g[:, 0:LANES]
    gf_ref[...] = g[:, LANES:2 * LANES]


def _inproj(x, w_main, wg, tm):
    n, d = x.shape
    row = lambda c: pl.BlockSpec((tm, c), lambda i: (i, 0))
    full = lambda a: pl.BlockSpec(a.shape, lambda i: (0, 0))
    widths = (POOL_WIDTH, MLSTM_WIDTH, MLSTM_WIDTH, MLSTM_WIDTH, MLSTM_WIDTH, LANES, LANES)
    return pl.pallas_call(
        _inproj_kernel, grid=(n // tm,), in_specs=[row(d), full(w_main), full(wg)],
        out_specs=tuple(row(c) for c in widths),
        out_shape=tuple(jax.ShapeDtypeStruct((n, c), F32) for c in widths),
        compiler_params=_params("parallel"), name="inproj")(x, w_main, wg)


def _inproj_seq_kernel(*refs, per, first, ride):
    n_in = 11 + (2 if first else 0) + (STEP_INPUTS if ride else 0)
    x_ref, w_ref, wgt_ref, pw_ref, sc_ref, wg_ref, wu_ref, wd_ref, wo_ref, wq_ref, wa_ref = refs[:11]
    outs, carry_ref = refs[n_in:-1], refs[-1]
    (po_ref, tail_ref, q_ref, k_ref, v_ref, og_ref, git_ref, gft_ref,
     cg_ref, cu_ref, cd_ref, co_ref, cq_ref, ca_ref) = outs[:14]
    tm = x_ref.shape[0]
    start = lax.rem(pl.program_id(0), per) * tm

    @pl.when(start == 0)
    def _():
        carry_ref[...] = jnp.zeros_like(carry_ref)

    for src, dst in ((wg_ref, cg_ref), (wu_ref, cu_ref), (wd_ref, cd_ref), (wo_ref, co_ref), (wq_ref, cq_ref),
                     (wa_ref, ca_ref)):
        dst[...] = src[...].astype(BF16)
    if ride:
        _mlstm_step_kernel(*refs[11:11 + STEP_INPUTS], *outs[14:18])
    x = x_ref[...]
    if first:
        x = _layer_norm(x, refs[11][...], refs[12][...])
        outs[14][...] = x
        outs[15][...] = jnp.zeros_like(outs[15])

    xb = x.astype(BF16)
    u = _dot(xb, w_ref[:, 0:POOL_WIDTH])
    halo = carry_ref.shape[0]
    ext = jnp.concatenate([carry_ref[...], u], axis=0)
    tail = ext[tm:, :]
    carry_ref[...] = tail
    tail_ref[...] = tail
    pos = start + lax.broadcasted_iota(jnp.int32, (tm, POOL_GROUP_DIM), 0)
    for g, win in enumerate(POOL_WINDOWS):
        sl = slice(g * POOL_GROUP_DIM, (g + 1) * POOL_GROUP_DIM)
        acc = ext[:, sl]
        s = 1
        while s < win:
            acc = acc + pltpu.roll(acc, s, 0)
            s *= 2
        cnt = jnp.minimum(pos + 1, win).astype(F32)
        dlt = acc[halo:, :] / cnt - u[:, sl]
        po_ref[:, sl] = (_dot(dlt.astype(BF16), pw_ref[g]) * sc_ref[:, sl]).astype(po_ref.dtype)
    _qkvo(xb, w_ref, q_ref, k_ref, v_ref, og_ref)
    gt = _dot_nt(wgt_ref[...], xb)
    git_ref[...] = gt[0:SUBLANES]
    gft_ref[...] = gt[SUBLANES:2 * SUBLANES]


def _inproj_seq(x, w_main, wgt, pw, scale, experts, dense, layer, seq, tm, first=None, ride=None):
    n, d = x.shape
    per = seq // tm
    steps = n // tm
    split = steps // N_EXPERTS
    assert split * N_EXPERTS == steps and d % steps == 0 and d % split == 0 and D_EXPERT % split == 0
    row = lambda c: pl.BlockSpec((tm, c), lambda i: (i, 0))
    full = lambda a: pl.BlockSpec(a.shape, lambda i: (0,) * a.ndim)
    gt_spec = pl.BlockSpec((None, SUBLANES, tm), lambda i: (i // per, 0, i % per))
    gt_shape = jax.ShapeDtypeStruct((n // seq, SUBLANES, seq), F32)
    mw, e, gf = MLSTM_WIDTH, EXPERTS_PER_GROUP, EXPERTS_PER_GROUP * D_EXPERT
    dr, fr, wr = d // split, D_EXPERT // split, d // steps
    col_in = pl.BlockSpec((None, None, dr, D_EXPERT), lambda i: (layer, i // split, i % split, 0))
    row_in = pl.BlockSpec((None, None, fr, d), lambda i: (layer, i // split, i % split, 0))
    col_out = pl.BlockSpec((None, dr, D_EXPERT), lambda i: (i // split // e, i % split, (i // split) % e))
    row_out = pl.BlockSpec((None, fr, d), lambda i: (i // split // e, ((i // split) % e) * split + i % split, 0))
    dense_in = pl.BlockSpec((None, wr, d), lambda i: (layer, i, 0))
    dense_out = pl.BlockSpec((wr, d), lambda i: (i, 0))
    out_shape = [
        jax.ShapeDtypeStruct((n, POOL_WIDTH), BF16),
        jax.ShapeDtypeStruct((steps, POOL_HALO, POOL_WIDTH), F32),
        jax.ShapeDtypeStruct((n, mw), BF16), jax.ShapeDtypeStruct((n, mw), BF16), jax.ShapeDtypeStruct((n, mw), BF16),
        jax.ShapeDtypeStruct((n, mw), F32), gt_shape, gt_shape,
        jax.ShapeDtypeStruct((N_GROUPS, d, gf), BF16), jax.ShapeDtypeStruct((N_GROUPS, d, gf), BF16),
        jax.ShapeDtypeStruct((N_GROUPS, gf, d), BF16),
        jax.ShapeDtypeStruct((d, d), BF16), jax.ShapeDtypeStruct((d, d), BF16), jax.ShapeDtypeStruct((d, d), BF16),
    ]
    out_specs = [row(POOL_WIDTH), pl.BlockSpec((None, POOL_HALO, POOL_WIDTH), lambda i: (i, 0, 0)),
                 row(mw), row(mw), row(mw), row(mw), gt_spec, gt_spec,
                 col_out, col_out, row_out, dense_out, dense_out, dense_out]
    extra_in, extra_args = [], []
    if first is not None:
        ln_g, ln_b, zeros_like = first
        lead = zeros_like.shape[0] * zeros_like.shape[1]
        assert lead % steps == 0 and zeros_like.shape[1] % (lead // steps) == 0
        zb = lead // steps
        zper = zeros_like.shape[1] // zb
        out_shape += [jax.ShapeDtypeStruct((n, d), F32), jax.ShapeDtypeStruct(zeros_like.shape, zeros_like.dtype)]
        out_specs += [row(d), pl.BlockSpec((None, zb) + zeros_like.shape[2:],
                                           lambda i: (i // zper, i % zper) + (0,) * (zeros_like.ndim - 2))]
        extra_args = [ln_g.reshape(1, d), ln_b.reshape(1, d)]
        extra_in = [full(a) for a in extra_args]
    aliases = {}
    if ride is not None:
        assert first is None
        rq, rk, rv, rgi, rgf, rbi, rbf, c_all, n_all, rm, c_new = ride
        n_s = rq.shape[0]
        sb = n_s // steps
        assert sb * steps == n_s
        hd = MLSTM_HEAD_DIM
        r3 = lambda a: a.reshape(steps, sb, a.shape[-1])
        s3 = lambda c: pl.BlockSpec((None, sb, c), lambda i: (i, 0, 0))
        cspec = pl.BlockSpec((None, sb, MLSTM_HEADS, hd, hd), lambda i: (layer, i, 0, 0, 0))
        nspec = pl.BlockSpec((None, sb, MLSTM_HEADS, hd), lambda i: (layer, i, 0, 0))
        extra_args = [r3(rq), r3(rk), r3(rv), r3(rgi), r3(rgf), rbi, rbf, c_all, n_all, r3(rm), c_new]
        assert len(extra_args) == STEP_INPUTS
        extra_in = [s3(mw), s3(mw), s3(mw), s3(LANES), s3(LANES), full(rbi), full(rbf), cspec, nspec, s3(LANES),
                    pl.BlockSpec(memory_space=pl.ANY)]
        aliases = {11 + STEP_INPUTS - 1: len(out_shape) + 1}
        out_shape += [jax.ShapeDtypeStruct((steps, sb, mw), F32), jax.ShapeDtypeStruct(c_new.shape, F32),
                      jax.ShapeDtypeStruct(n_all.shape[1:], F32), jax.ShapeDtypeStruct((steps, sb, LANES), F32)]
        out_specs += [s3(mw), cspec, pl.BlockSpec((sb, MLSTM_HEADS, hd), lambda i: (i, 0, 0)), s3(LANES)]
    scale = scale.reshape(1, POOL_WIDTH)
    return pl.pallas_call(
        functools.partial(_inproj_seq_kernel, per=per, first=first is not None, ride=ride is not None), grid=(steps,),
        in_specs=[row(d), full(w_main), full(wgt), full(pw), full(scale), col_in, col_in, row_in,
                  dense_in, dense_in, dense_in] + extra_in,
        out_specs=tuple(out_specs), out_shape=tuple(out_shape),
        scratch_shapes=[pltpu.VMEM((POOL_HALO, POOL_WIDTH), F32)], input_output_aliases=aliases,
        compiler_params=_params("arbitrary"), name="inproj_seq")(x, w_main, wgt, pw, scale, *experts, *dense,
                                                                 *extra_args)


def _pool_kernel(ext_ref, pw_ref, sc_ref, o_ref, *, period):
    rows = ext_ref.shape[0]
    r = lax.broadcasted_iota(jnp.int32, (rows, POOL_GROUP_DIM), 0)
    if period != rows:
        r = lax.rem(r, period)
    for g, win in enumerate(POOL_WINDOWS):
        sl = slice(g * POOL_GROUP_DIM, (g + 1) * POOL_GROUP_DIM)
        x = ext_ref[:, sl]
        acc = x
        s = 1
        while s < win:
            acc = acc + jnp.where(r >= s, pltpu.roll(acc, s, 0), 0.0)
            s *= 2
        cnt = jnp.minimum(r + 1, win).astype(F32)
        d = acc / cnt - x
        y = _dot(d.astype(BF16), pw_ref[g]) * sc_ref[:, sl]
        o_ref[:, sl] = y.astype(o_ref.dtype)


def _pool(ext2d, pw, scale, period):
    n, c = ext2d.shape
    row = pl.BlockSpec((POOL_ROWS, c), lambda i: (i, 0))
    return pl.pallas_call(
        functools.partial(_pool_kernel, period=period), grid=(n // POOL_ROWS,),
        in_specs=[row, pl.BlockSpec(pw.shape, lambda i: (0, 0, 0)), pl.BlockSpec((1, c), lambda i: (0, 0))],
        out_specs=row, out_shape=jax.ShapeDtypeStruct((n, c), BF16),
        compiler_params=_params("parallel"), name="pool")(ext2d, pw, scale.reshape(1, c))


def _mlstm_chunk_kernel(q_ref, k_ref, v_ref, git_ref, gft_ref, bit_ref, bft_ref, hn_ref, c_ref, n_ref, m_ref,
                        at_s, ws_s, dc_s, mc_s, wi_s, fl_s):
    nb, L, _ = q_ref.shape
    rows = nb * SUBLANES

    @pl.when(pl.program_id(0) == 0)
    def _():
        c_ref[...] = jnp.zeros_like(c_ref)
        n_ref[...] = jnp.zeros_like(n_ref)
        m_ref[...] = jnp.zeros_like(m_ref)

    ig = git_ref[...].reshape(rows, L) + bit_ref[...]
    lf = _log_sigmoid(gft_ref[...].reshape(rows, L) + bft_ref[...])
    tri = (lax.broadcasted_iota(jnp.int32, (L, L), 0) <= lax.broadcasted_iota(jnp.int32, (L, L), 1)).astype(BF16)
    lf_hi = lf.astype(BF16)
    lf_mid = (lf - lf_hi.astype(F32)).astype(BF16)
    lf_lo = (lf - lf_hi.astype(F32) - lf_mid.astype(F32)).astype(BF16)
    bc = _dot(lf_hi, tri) + (_dot(lf_mid, tri) + _dot(lf_lo, tri))
    at = ig - bc
    m_prev = m_ref[...].reshape(rows, LANES)[:, 0:1]
    mc = jnp.maximum(_scan(at, 1, jnp.maximum, -jnp.inf), m_prev)
    mt = bc + mc
    m_last = mc[:, L - 1:L]
    at_s[...] = at.reshape(nb, SUBLANES, L)
    ws_s[...] = jnp.exp(at - m_last).reshape(nb, SUBLANES, L)
    dc_s[...] = jnp.broadcast_to(jnp.exp(m_prev - m_last), (rows, LANES)).reshape(nb, SUBLANES, LANES)
    m_ref[...] = jnp.broadcast_to(mt[:, L - 1:L], (rows, LANES)).reshape(nb, SUBLANES, LANES)
    wi = jnp.exp(m_prev - mc)
    fl = jnp.exp(-mt)
    for b in range(nb):
        rs = slice(b * SUBLANES, (b + 1) * SUBLANES)
        mc_s[b] = mc[rs].T
        wi_s[b] = wi[rs].T
        fl_s[b] = fl[rs].T

    causal = (lax.broadcasted_iota(jnp.int32, (L, L), 0) >= lax.broadcasted_iota(jnp.int32, (L, L), 1))

    def per_batch(heads, b, carry):
        at = at_s[b]
        ws = ws_s[b]
        decay = dc_s[b]
        mc_c, wi_c, fl_c = mc_s[b], wi_s[b], fl_s[b]
        ws16 = jnp.concatenate([ws, ws], axis=0).astype(BF16)
        for h in heads:
            sl = slice(h * MLSTM_HEAD_DIM, (h + 1) * MLSTM_HEAD_DIM)
            col = slice(h, h + 1)
            qh = q_ref[b, :, sl]
            kh = k_ref[b, :, sl]
            vh = v_ref[b, :, sl]
            s = _dot_nt(qh, kh)
            p = jnp.where(causal, s * jnp.exp(at[col, :] - mc_c[:, col]), 0.0)
            c_old = c_ref[b, h]
            n_old = n_ref[b, col, :]
            wi = wi_c[:, col]
            dv = MLSTM_HEAD_DIM
            v_ext = jnp.concatenate([vh, jnp.ones_like(vh)], axis=1)
            c_ext = jnp.concatenate([c_old, jnp.broadcast_to(n_old, c_old.shape)], axis=0)
            intra = _dot(p.astype(BF16), v_ext)
            inter = _dot_nt(qh, c_ext.astype(BF16))
            num = intra[:, :dv] + wi * inter[:, :dv]
            qn = intra[:, dv:] + wi * inter[:, dv:]
            hh = num / jnp.maximum(jnp.abs(qn), fl_c[:, col])
            hn_ref[b, :, sl] = _unit_norm(hh)
            dc = decay[col, 0:1]
            vts = (vh.astype(F32).T * ws[col, :]).astype(BF16)
            c_ref[b, h] = dc * c_old + _dot(vts, kh)
            n_ref[b, col, :] = dc * n_old + _dot(ws16, kh)[col, :]
        return carry

    for h0 in range(0, MLSTM_HEADS, 2):
        lax.fori_loop(0, nb, functools.partial(per_batch, (h0, h0 + 1)), 0)


def _mlstm_chunk(q, k, v, git, gft, bit, bft):
    nb, t, w = q.shape
    L = MLSTM_CHUNK
    seq = lambda c: pl.BlockSpec((nb, L, c), lambda i: (0, i, 0))
    seqt = pl.BlockSpec((nb, SUBLANES, L), lambda i: (0, 0, i))
    vec = lambda a: pl.BlockSpec(a.shape, lambda i: (0, 0))
    out_shape = (
        jax.ShapeDtypeStruct((nb, t, w), F32),
        jax.ShapeDtypeStruct((nb, MLSTM_HEADS, MLSTM_HEAD_DIM, MLSTM_HEAD_DIM), F32),
        jax.ShapeDtypeStruct((nb, MLSTM_HEADS, MLSTM_HEAD_DIM), F32),
        jax.ShapeDtypeStruct((nb, SUBLANES, LANES), F32),
    )
    out_specs = (
        seq(w),
        pl.BlockSpec(out_shape[1].shape, lambda i: (0, 0, 0, 0)),
        pl.BlockSpec(out_shape[2].shape, lambda i: (0, 0, 0)),
        pl.BlockSpec(out_shape[3].shape, lambda i: (0, 0, 0)),
    )
    bit, bft = jnp.tile(bit, (nb, 1)), jnp.tile(bft, (nb, 1))
    scratch = [pltpu.VMEM((nb, SUBLANES, L), F32), pltpu.VMEM((nb, SUBLANES, L), F32),
               pltpu.VMEM((nb, SUBLANES, LANES), F32)] + [pltpu.VMEM((nb, L, SUBLANES), F32) for _ in range(3)]
    return pl.pallas_call(
        _mlstm_chunk_kernel, grid=(t // L,),
        in_specs=[seq(w), seq(w), seq(w), seqt, seqt, vec(bit), vec(bft)],
        out_specs=out_specs, out_shape=out_shape, scratch_shapes=scratch, compiler_params=_params("arbitrary"),
        name="mlstm_chunk")(q, k, v, git, gft, bit, bft)


def _mlstm_step_kernel(q_ref, k_ref, v_ref, gi_ref, gf_ref, bi_ref, bf_ref, c_ref, n_ref, m_ref, c_new_ref,
                       hn_ref, co_ref, no_ref, mo_ref):
    del c_new_ref
    tb = q_ref.shape[0]
    d = MLSTM_HEAD_DIM
    ig = gi_ref[...] + bi_ref[...]
    lf = _log_sigmoid(gf_ref[...] + bf_ref[...])
    m_old = m_ref[...]
    mt = jnp.maximum(lf + m_old, ig)
    mo_ref[...] = mt
    wa = jnp.exp(ig - mt)
    wi = jnp.exp(lf + m_old - mt)
    fl = jnp.exp(-mt)
    wa_t, wi_t, fl_t = wa.T, wi.T, fl.T
    lane = lax.broadcasted_iota(jnp.int32, (d, tb), 1)
    for h in range(MLSTM_HEADS):
        sl = slice(h * d, (h + 1) * d)
        qh, kh, vh = q_ref[:, sl], k_ref[:, sl], v_ref[:, sl]
        nh = n_ref[:, h, :]
        qt, kt, vt, nt = qh.T, kh.T, vh.T, nh.T
        wa_r, wi_r, fl_r = wa_t[h:h + 1, :], wi_t[h:h + 1, :], fl_t[h:h + 1, :]
        s = jnp.sum(qt * kt, 0, keepdims=True) * wa_r
        cq = jnp.zeros((d, tb), F32)
        qtb = qt.astype(BF16)
        for b in range(tb):
            cq = jnp.where(lane == b, _dot(c_ref[b, h].astype(BF16), qtb), cq)
        num = s * vt + wi_r * cq
        qn = s + wi_r * jnp.sum(nt * qt, 0, keepdims=True)
        hh = num / jnp.maximum(jnp.abs(qn), fl_r)
        mu = jnp.mean(hh, 0, keepdims=True)
        xc = hh - mu
        var = jnp.mean(xc * xc, 0, keepdims=True)
        hn_ref[:, sl] = (xc * lax.rsqrt(var + LN_EPS)).T
        wav = wa_r * vt
        for b in range(tb):
            co_ref[b, h] = wi_r[:, b:b + 1] * c_ref[b, h] + wav[:, b:b + 1] * kh[b:b + 1, :]
        no_ref[:, h, :] = wi[:, h:h + 1] * nh + wa[:, h:h + 1] * kh


def _mlstm_step(q, k, v, gi, gf, bi, bf, c_all, n_all, m0, c_new, layer, tb):
    n, w = q.shape
    d = MLSTM_HEAD_DIM
    row = lambda c: pl.BlockSpec((tb, c), lambda i: (i, 0))
    vec = lambda a: pl.BlockSpec(a.shape, lambda i: (0, 0))
    cin = pl.BlockSpec((None, tb, MLSTM_HEADS, d, d), lambda i: (layer, i, 0, 0, 0))
    nin = pl.BlockSpec((None, tb, MLSTM_HEADS, d), lambda i: (layer, i, 0, 0))
    nout = pl.BlockSpec((tb, MLSTM_HEADS, d), lambda i: (i, 0, 0))
    out_shape = (
        jax.ShapeDtypeStruct((n, w), F32),
        jax.ShapeDtypeStruct(c_new.shape, F32),
        jax.ShapeDtypeStruct(n_all.shape[1:], F32),
        jax.ShapeDtypeStruct((n, LANES), F32),
    )
    return pl.pallas_call(
        _mlstm_step_kernel, grid=(n // tb,),
        in_specs=[row(w), row(w), row(w), row(LANES), row(LANES), vec(bi), vec(bf), cin, nin, row(LANES),
                  pl.BlockSpec(memory_space=pl.ANY)],
        out_specs=(row(w), cin, nout, row(LANES)), out_shape=out_shape, input_output_aliases={10: 1},
        compiler_params=_params("parallel"), name="mlstm_step")(q, k, v, gi, gf, bi, bf, c_all, n_all, m0, c_new)


def _mixout_kernel(po_ref, hn_ref, og_ref, x_ref, ng_ref, w_ref, g_ref, b_ref, o_ref):
    mo = jax.nn.sigmoid(og_ref[...]) * hn_ref[...] * ng_ref[...]
    mix = _dot(po_ref[...], w_ref[0:POOL_WIDTH, :]) + _dot(mo.astype(BF16), w_ref[POOL_WIDTH:, :])
    o_ref[...] = _layer_norm(ALPHA * x_ref[...] + mix, g_ref[...], b_ref[...])


def _mixout(po, hn, og, x, ng, w, g, b, tm):
    n, d = x.shape
    row = lambda c: pl.BlockSpec((tm, c), lambda i: (i, 0))
    vec = lambda a: pl.BlockSpec(a.shape, lambda i: (0, 0))
    return pl.pallas_call(
        _mixout_kernel, grid=(n // tm,),
        in_specs=[row(POOL_WIDTH), row(MLSTM_WIDTH), row(MLSTM_WIDTH), row(d), vec(ng), vec(w), vec(g), vec(b)],
        out_specs=row(d), out_shape=jax.ShapeDtypeStruct((n, d), F32),
        compiler_params=_params("parallel"), name="mixout")(po, hn, og, x, ng, w, g, b)


def _mm_kernel(x_ref, w_ref, o_ref, *, scale):
    y = _dot(x_ref[...].astype(BF16), w_ref[...])
    if scale != 1.0:
        y = y * scale
    o_ref[...] = y.astype(o_ref.dtype)


def _mm(x, w, tm, out_dtype, scale=1.0, name="proj"):
    n, d = x.shape
    dout = w.shape[1]
    return pl.pallas_call(
        functools.partial(_mm_kernel, scale=scale), grid=(n // tm,),
        in_specs=[pl.BlockSpec((tm, d), lambda i: (i, 0)), pl.BlockSpec(w.shape, lambda i: (0, 0))],
        out_specs=pl.BlockSpec((tm, dout), lambda i: (i, 0)),
        out_shape=jax.ShapeDtypeStruct((n, dout), out_dtype),
        compiler_params=_params("parallel"), name=name)(x, w)


def _cast_inproj_kernel(wt_ref, o_ref, og_ref, *, n_main):
    i = pl.program_id(0)

    @pl.when(i < n_main)
    def _():
        o_ref[...] = wt_ref[...].T.astype(BF16)

    @pl.when(i == n_main)
    def _():
        og_ref[...] = wt_ref[0:2 * MLSTM_HEADS, :]


def _cast_inproj(w_in, layer, tc=256):
    wt = jnp.swapaxes(w_in, 1, 2)
    d = wt.shape[2]
    n_main = GATE_OFF // tc
    return pl.pallas_call(
        functools.partial(_cast_inproj_kernel, n_main=n_main), grid=(n_main + 1,),
        in_specs=[pl.BlockSpec((None, tc, d), lambda i: (layer, i, 0))],
        out_specs=(pl.BlockSpec((d, tc), lambda i: (0, jnp.minimum(i, n_main - 1))),
                   pl.BlockSpec((2 * MLSTM_HEADS, d), lambda i: (0, 0))),
        out_shape=(jax.ShapeDtypeStruct((d, GATE_OFF), BF16), jax.ShapeDtypeStruct((2 * MLSTM_HEADS, d), F32)),
        compiler_params=_params("arbitrary"), name="inproj_cast")(wt)


def _mem_proj_kernel(x_ref, wk_ref, wv_ref, k4_ref, v4_ref, k2_ref, v2_ref, wkb, wvb):
    @pl.when(pl.program_id(1) == 0)
    def _():
        wkb[...] = wk_ref[...].astype(BF16)
        wvb[...] = wv_ref[...].astype(BF16)

    bb, m, d = x_ref.shape
    xb = x_ref[...].reshape(bb * m, d).astype(BF16)
    k = _dot(xb, wkb[...])
    v = _dot(xb, wvb[...])
    k2_ref[...] = k.reshape(bb, m, d).astype(k2_ref.dtype)
    v2_ref[...] = v.reshape(bb, m, d).astype(v2_ref.dtype)
    k4_ref[...] = k.reshape(bb, m, CA_HEADS, CA_HEAD_DIM)
    v4_ref[...] = v.reshape(bb, m, CA_HEADS, CA_HEAD_DIM)


def _mem_proj(mem, wk_all, wv_all, bb=2):
    nb, m, d = mem.shape
    depth = wk_all.shape[0]
    wspec = pl.BlockSpec((None, d, d), lambda l, b: (l, 0, 0))
    o4 = pl.BlockSpec((None, bb, m, CA_HEADS, CA_HEAD_DIM), lambda l, b: (l, b, 0, 0, 0))
    o2 = pl.BlockSpec((None, bb, m, d), lambda l, b: (l, b, 0, 0))
    s4 = jax.ShapeDtypeStruct((depth, nb, m, CA_HEADS, CA_HEAD_DIM), F32)
    s2 = jax.ShapeDtypeStruct((depth, nb, m, d), BF16)
    return pl.pallas_call(
        _mem_proj_kernel, grid=(depth, nb // bb),
        in_specs=[pl.BlockSpec((bb, m, d), lambda l, b: (b, 0, 0)), wspec, wspec],
        out_specs=(o4, o4, o2, o2), out_shape=(s4, s4, s2, s2),
        scratch_shapes=[pltpu.VMEM((d, d), BF16), pltpu.VMEM((d, d), BF16)],
        compiler_params=_params("arbitrary", "arbitrary"), name="mem_proj")(mem, wk_all, wv_all)


def _attn_block_kernel(po_ref, hn_ref, og_ref, x_ref, k_ref, v_ref, ng_ref, wm_ref, g1_ref, b1_ref,
                       wq_ref, wo_ref, g_ref, b_ref, o_ref):
    mo = jax.nn.sigmoid(og_ref[...]) * hn_ref[...] * ng_ref[...]
    mix = _dot(po_ref[...], wm_ref[0:POOL_WIDTH, :]) + _dot(mo.astype(BF16), wm_ref[POOL_WIDTH:, :])
    x = _layer_norm(ALPHA * x_ref[...] + mix, g1_ref[...], b1_ref[...])
    qb = (_dot(x.astype(BF16), wq_ref[...]) * (CA_HEAD_DIM ** -0.5)).astype(BF16)
    kb = k_ref[...].astype(BF16)
    vb = v_ref[...].astype(BF16)
    ctx = []
    for h in range(CA_HEADS):
        sl = slice(h * CA_HEAD_DIM, (h + 1) * CA_HEAD_DIM)
        s = _dot_nt(qb[:, sl], kb[:, sl])
        e = jnp.exp(s - jnp.max(s, -1, keepdims=True))
        ctx.append((_dot(e.astype(BF16), vb[:, sl]) / jnp.sum(e, -1, keepdims=True)).astype(BF16))
    y = _dot(jnp.concatenate(ctx, axis=1), wo_ref[...])
    o_ref[...] = _layer_norm(ALPHA * x + y, g_ref[...], b_ref[...])


def _attn_block(po, hn, og, x, k, v, layer, ng, wm, ln1, wq, wo, ln2, seq, tq):
    n, d = x.shape
    per = seq // tq
    row = lambda c: pl.BlockSpec((tq, c), lambda i: (i, 0))
    kspec = pl.BlockSpec((None, None, N_MEM, d), lambda i: (layer, i // per, 0, 0))
    vec = lambda a: pl.BlockSpec(a.shape, lambda i: (0, 0))
    return pl.pallas_call(
        _attn_block_kernel, grid=(n // tq,),
        in_specs=[row(POOL_WIDTH), row(MLSTM_WIDTH), row(MLSTM_WIDTH), row(d), kspec, kspec, vec(ng), vec(wm),
                  vec(ln1[0]), vec(ln1[1]), vec(wq), vec(wo), vec(ln2[0]), vec(ln2[1])],
        out_specs=row(d), out_shape=jax.ShapeDtypeStruct((n, d), F32),
        compiler_params=_params("parallel"), name="attn_block")(po, hn, og, x, k, v, ng, wm, *ln1, wq, wo, *ln2)


def _attn_step_kernel(q_ref, k_ref, v_ref, o_ref):
    bb = q_ref.shape[0]
    rows = N_MEM * CA_HEADS
    lane = lax.broadcasted_iota(jnp.int32, (SUBLANES, rows), 1)
    row = lax.broadcasted_iota(jnp.int32, (SUBLANES, rows), 0)
    own = lax.rem(lane, CA_HEADS) == lax.rem(row, CA_HEADS)
    for j in range(bb):
        kf = k_ref[j].reshape(rows, CA_HEAD_DIM).astype(BF16)
        vf = v_ref[j].reshape(rows, CA_HEAD_DIM).astype(BF16)
        s = jnp.where(own, _dot_nt(q_ref[j].astype(BF16), kf), -jnp.inf)
        e = jnp.exp(s - jnp.max(s, -1, keepdims=True))
        o_ref[j] = _dot(e.astype(BF16), vf) / jnp.sum(e, -1, keepdims=True)


def _mm_res_ln_kernel(a_ref, x_ref, w_ref, g_ref, b_ref, o_ref):
    y = _dot(a_ref[...].astype(BF16), w_ref[...])
    o_ref[...] = _layer_norm(ALPHA * x_ref[...] + y, g_ref[...], b_ref[...])


def _mm_res_ln(a, x, w, g, b, tm):
    n, d = x.shape
    row = lambda c: pl.BlockSpec((tm, c), lambda i: (i, 0))
    vec = lambda arr: pl.BlockSpec(arr.shape, lambda i: (0, 0))
    return pl.pallas_call(
        _mm_res_ln_kernel, grid=(n // tm,),
        in_specs=[row(a.shape[1]), row(d), vec(w), vec(g), vec(b)], out_specs=row(d),
        out_shape=jax.ShapeDtypeStruct((n, d), F32),
        compiler_params=_params("parallel"), name="proj_res_ln")(a, x, w, g, b)


def _route(lt):
    gl = [lt[g:g + 1, :] for g in range(N_GROUPS)]
    gmax = functools.reduce(jnp.maximum, gl)
    gsum = functools.reduce(jnp.add, [jnp.exp(x - gmax) for x in gl])
    pg_sel = 1.0 / gsum

    def first_max(vals):
        m = functools.reduce(jnp.maximum, vals)
        taken = jnp.zeros_like(m, dtype=jnp.bool_)
        hot = []
        for x in vals:
            h = jnp.logical_and(x == m, jnp.logical_not(taken))
            taken = jnp.logical_or(taken, h)
            hot.append(h)
        return m, hot

    _, g_hot = first_max(gl)
    el = []
    for j in range(EXPERTS_PER_GROUP):
        rows = [lt[SUBLANES + g * EXPERTS_PER_GROUP + j:SUBLANES + g * EXPERTS_PER_GROUP + j + 1, :]
                for g in range(N_GROUPS)]
        x = rows[N_GROUPS - 1]
        for g in range(N_GROUPS - 2, -1, -1):
            x = jnp.where(g_hot[g], rows[g], x)
        el.append(x)
    emax = functools.reduce(jnp.maximum, el)
    ee = [jnp.exp(x - emax) for x in el]
    esum = functools.reduce(jnp.add, ee)
    pe = [x / esum for x in ee]
    p1, hot1 = first_max(pe)
    p2, hot2 = first_max([jnp.where(h, -jnp.inf, x) for h, x in zip(hot1, pe)])
    psum = p1 + p2
    gate = [jnp.where(h1, pg_sel * p1 / psum, jnp.where(h2, pg_sel * p2 / psum, 0.0)) for h1, h2 in zip(hot1, hot2)]
    return g_hot, gate


def _moe_kernel(x_ref, wr_ref, br_ref, tri_ref, wg_ref, wu_ref, wd_ref, g_ref, b_ref, *rest):
    ride = None
    if len(rest) > 7:
        ride = functools.partial(_attn_step_kernel, rest[0], rest[1], rest[2], rest[4])
        rest = rest[3:4] + rest[5:]
    o_ref, xb_ref, ct_ref, comb_ref, perm_ref, acc_ref, full_ref = rest
    grp = pl.program_id(1)
    tm = x_ref.shape[0]
    ns, cap, sw = perm_ref.shape[1:]
    f = D_EXPERT

    @pl.when(grp == 0)
    def _():
        x = x_ref[...]
        xh = x.astype(BF16)
        xb_ref[...] = xh
        xl = (x - xh.astype(F32)).astype(BF16)
        wr = wr_ref[...]
        wh = wr.astype(BF16)
        wl = (wr - wh.astype(F32)).astype(BF16)
        lt = _dot_nt(wh, xh) + (_dot_nt(wl, xh) + _dot_nt(wh, xl)) + br_ref[...]
        g_hot, gate = _route(lt)
        row8 = lax.broadcasted_iota(jnp.int32, (SUBLANES, tm), 0)
        hot8 = jnp.zeros((SUBLANES, tm), F32)
        for g in range(N_GROUPS):
            hot8 = jnp.where(jnp.logical_and(row8 == g, g_hot[g]), 1.0, hot8)
        most = None
        for s in range(ns if cap < sw else 0):
            hs = hot8[:, s * sw:(s + 1) * sw]
            cum = _dot(hs.astype(BF16), tri_ref[...])
            seg_max = jnp.max(cum[:, sw - 1:sw])
            most = seg_max if most is None else jnp.maximum(most, seg_max)
            base = lax.broadcasted_iota(jnp.int32, (SUBLANES, sw), 0).astype(F32) * cap
            code = jnp.sum(hs * (base + cum - 1.0), 0, keepdims=True).astype(jnp.int32)
            for g in range(N_GROUPS):
                r = lax.broadcasted_iota(jnp.int32, (cap, sw), 0) + g * cap
                perm_ref[g, s] = jnp.where(r == code, 1.0, 0.0).astype(BF16)
        full_ref[0] = (most > cap).astype(jnp.int32) if most is not None else jnp.int32(1)
        ct_ref[...] = jnp.zeros_like(ct_ref)
        for j in range(EXPERTS_PER_GROUP):
            ct_ref[j:j + 1, :] = gate[j]
        ct_ref[EXPERTS_PER_GROUP:EXPERTS_PER_GROUP + 1, :] = jnp.sum(hot8 * row8.astype(F32), 0, keepdims=True)
        comb_ref[...] = ct_ref[...].T
        acc_ref[...] = jnp.zeros_like(acc_ref)

    def experts(rows, gates):
        hh = []
        for j in range(EXPERTS_PER_GROUP):
            hg = _dot(rows, wg_ref[:, j * f:(j + 1) * f])
            hu = _dot(rows, wu_ref[:, j * f:(j + 1) * f])
            hh.append((hg * jax.nn.sigmoid(hg) * hu * gates[:, j:j + 1]).astype(BF16))
        return _dot(jnp.concatenate(hh, axis=1), wd_ref[...])

    @pl.when(full_ref[0] == 0)
    def _():
        comb = comb_ref[...]
        ch = comb.astype(BF16)
        cl = (comb - ch.astype(F32)).astype(BF16)
        xb = xb_ref[...]
        perm = [perm_ref[grp, s] for s in range(ns)]
        seg = lambda a, s: a[s * sw:(s + 1) * sw]
        rows = jnp.concatenate([_dot(perm[s], seg(xb, s)) for s in range(ns)], axis=0).astype(BF16)
        gates = jnp.concatenate([_dot(perm[s], seg(ch, s)) + _dot(perm[s], seg(cl, s)) for s in range(ns)], axis=0)
        y = experts(rows, gates).astype(BF16)
        for s in range(ns):
            acc_ref[s * sw:(s + 1) * sw, :] += _dot_tn(perm[s], y[s * cap:(s + 1) * cap])
        if ride is not None:
            ride()

    @pl.when(full_ref[0] != 0)
    def _():
        comb = comb_ref[...]
        own = comb[:, EXPERTS_PER_GROUP:EXPERTS_PER_GROUP + 1] == grp.astype(F32)
        acc_ref[...] += experts(xb_ref[...], jnp.where(own, comb, 0.0))
        if ride is not None:
            ride()

    @pl.when(grp == N_GROUPS - 1)
    def _():
        o_ref[...] = _layer_norm(ALPHA * x_ref[...] + acc_ref[...], g_ref[...], b_ref[...])


def _moe(x, wr, br, wg, wu, wd, g, b, tm, attn=None):
    n, d = x.shape
    ns = max(1, tm // MOE_SEGMENT)
    sw = tm // ns
    cap = min(sw, MOE_CAP)
    tri = (lax.broadcasted_iota(jnp.int32, (sw, sw), 0) <= lax.broadcasted_iota(jnp.int32, (sw, sw), 1)).astype(BF16)
    row = pl.BlockSpec((tm, d), lambda i, e: (i, 0))
    vec = lambda a: pl.BlockSpec(a.shape, lambda i, e: (0, 0))
    grp = lambda a: pl.BlockSpec((None,) + a.shape[1:], lambda i, e: (e, 0, 0))
    scratch = [pltpu.VMEM((tm, d), BF16), pltpu.VMEM((LANES, tm), F32), pltpu.VMEM((tm, LANES), F32),
               pltpu.VMEM((N_GROUPS, ns, cap, sw), BF16), pltpu.VMEM((tm, d), F32), pltpu.SMEM((1,), jnp.int32)]
    in_specs = [row, vec(wr), vec(br), vec(tri), grp(wg), grp(wu), grp(wd), vec(g), vec(b)]
    args = [x, wr, br, tri, wg, wu, wd, g, b]
    out_specs, out_shape = row, jax.ShapeDtypeStruct((n, d), F32)
    if attn is not None:
        q, k_all, v_all, layer = attn
        steps = (n // tm) * N_GROUPS
        bb = q.shape[0] // steps
        assert bb * steps == q.shape[0]
        qspec = pl.BlockSpec((bb, SUBLANES, CA_HEAD_DIM), lambda i, e: (i * N_GROUPS + e, 0, 0))
        kspec = pl.BlockSpec((None, bb, N_MEM, CA_HEADS, CA_HEAD_DIM), lambda i, e: (layer, i * N_GROUPS + e, 0, 0, 0))
        in_specs += [qspec, kspec, kspec]
        args += [q, k_all, v_all]
        out_specs, out_shape = (row, qspec), (out_shape, jax.ShapeDtypeStruct(q.shape, F32))
    return pl.pallas_call(
        _moe_kernel, grid=(n // tm, N_GROUPS), in_specs=in_specs, out_specs=out_specs, out_shape=out_shape,
        scratch_shapes=scratch, compiler_params=_params("arbitrary", "arbitrary"), name="moe")(*args)


def _pad_lanes(v):
    return jnp.zeros((1, LANES), F32).at[0, :v.shape[0]].set(v)


def _pad_rows(v):
    return jnp.zeros((SUBLANES, 1), F32).at[:v.shape[0], 0].set(v)


def _layer_weights(l, w_in, b_i, b_f, pool_w, pool_scale, mlstm_norm_g, w_out, ln1_g, ln1_b, ca_wq, ca_wo,
                   ln2_g, ln2_b, w_gr, b_gr, w_er, b_er, w_gate, w_up, w_down, ln3_g, ln3_b):
    d = D_MODEL
    w_main, w_gate_rows = _cast_inproj(w_in, l)
    w_gate_cols = w_gate_rows.T
    wg = jnp.zeros((d, 2 * LANES), F32)
    wg = wg.at[:, 0:MLSTM_HEADS].set(w_gate_cols[:, :MLSTM_HEADS])
    wg = wg.at[:, LANES:LANES + MLSTM_HEADS].set(w_gate_cols[:, MLSTM_HEADS:])
    wgt = jnp.zeros((2 * SUBLANES, d), F32)
    wgt = wgt.at[0:MLSTM_HEADS].set(w_gate_cols[:, :MLSTM_HEADS].T)
    wgt = wgt.at[SUBLANES:SUBLANES + MLSTM_HEADS].set(w_gate_cols[:, MLSTM_HEADS:].T)
    wr = jnp.zeros((ROUTER_ROWS, d), F32)
    wr = wr.at[0:N_GROUPS].set(w_gr[l].T).at[SUBLANES:SUBLANES + N_EXPERTS].set(w_er[l].T)
    br = jnp.zeros((ROUTER_ROWS, 1), F32)
    br = br.at[0:N_GROUPS, 0].set(b_gr[l]).at[SUBLANES:SUBLANES + N_EXPERTS, 0].set(b_er[l])
    row = lambda v: v.reshape(1, -1)
    return dict(
        w_main=w_main, wg=wg.astype(BF16), wgt=wgt.astype(BF16),
        bi=_pad_lanes(b_i[l]), bf=_pad_lanes(b_f[l]), bit=_pad_rows(b_i[l]), bft=_pad_rows(b_f[l]),
        pool_w=pool_w[l].astype(BF16), pool_scale=pool_scale[l], norm_g=row(mlstm_norm_g[l]),
        ln1=(row(ln1_g[l]), row(ln1_b[l])), ln2=(row(ln2_g[l]), row(ln2_b[l])), ln3=(row(ln3_g[l]), row(ln3_b[l])),
        wr=wr, br=br, raw_experts=(w_gate, w_up, w_down), raw_dense=(w_out, ca_wq, ca_wo))


def _prompt_mixers(x, nb, seq, layer, mem, p, tm, tm_big, ride=None):
    if ride is not None:
        tm = tm_big
    per = seq // tm
    outs = _inproj_seq(x, p["w_main"], p["wgt"], p["pool_w"], p["pool_scale"], p["raw_experts"], p["raw_dense"],
                       layer, seq, tm, first=p.get("first"), ride=ride)
    if ride is not None:
        hn_s, c_s, n_s, m_s = outs[14:18]
        p["step"] = (hn_s.reshape(-1, MLSTM_WIDTH), c_s, n_s, m_s.reshape(-1, LANES))
    pooled, tails, q, k, v, og, git, gft = outs[:8]
    p["experts"] = outs[8:11]
    p["w_out"], p["wq"], p["wo"] = outs[11:14]
    if p.get("first") is not None:
        x, p["zeros"] = outs[14:16]
    pool_buf = tails.reshape(nb, per, POOL_HALO, POOL_WIDTH)[:, per - 1, POOL_HALO - POOL_BUF:]
    r3 = lambda a: a.reshape(nb, seq, a.shape[-1])
    hn, c1, n1, m1 = _mlstm_chunk(r3(q), r3(k), r3(v), git, gft, p["bit"], p["bft"])
    x = _attn_block(pooled, hn.reshape(nb * seq, MLSTM_WIDTH), og, x, mem[0], mem[1], layer, p["norm_g"], p["w_out"],
                    p["ln1"], p["wq"], p["wo"], p["ln2"], seq, tm_big)
    return x, pool_buf, c1, n1, m1[:, :MLSTM_HEADS, 0]


def _sample_inproj(x, nb, state, p):
    u, q, k, v, og, gi, gf = _inproj(x, p["w_main"], p["wg"], nb)
    c_all, n_all, m0, c_new = state
    m0p = jnp.zeros((nb, LANES), F32).at[:, :MLSTM_HEADS].set(m0)
    return u, og, (q, k, v, gi, gf, p["bi"], p["bf"], c_all, n_all, m0p, c_new)


def _sample_mixers(x, nb, layer, pool_prev, u, og, step, p):
    ext = jnp.concatenate([pool_prev, u.reshape(nb, 1, POOL_WIDTH)], axis=1)
    period = ext.shape[1]
    pooled = _pool(ext.reshape(nb * period, POOL_WIDTH), p["pool_w"], p["pool_scale"], period)
    pooled = pooled.reshape(nb, period, POOL_WIDTH)[:, period - 1]
    hn, c1, n1, m1 = p["step"] if "step" in p else _mlstm_step(*step, layer, MLSTM_STEP_SEQS)
    x = _mixout(pooled, hn, og, x, p["norm_g"], p["w_out"], *p["ln1"], nb)
    qc = _mm(x, p["wq"], nb, F32, scale=CA_HEAD_DIM ** -0.5, name="ca_q")
    qh = jnp.zeros((nb, SUBLANES, CA_HEAD_DIM), F32).at[:, :CA_HEADS].set(qc.reshape(nb, CA_HEADS, CA_HEAD_DIM))
    return x, qh, ext[:, -POOL_BUF:], c1, n1, m1[:, :MLSTM_HEADS]


def kernel(x_prompt, x_sample, mem_prompt, cache_pool, state_mlstm_C, state_mlstm_n, state_mlstm_m,
           cache_mem_k, cache_mem_v, emb_ln_g, emb_ln_b, w_in, b_i, b_f, pool_w, pool_scale,
           mlstm_norm_g, w_out, ln1_g, ln1_b, ca_wq, ca_wk, ca_wv, ca_wo, ln2_g, ln2_b,
           w_gr, b_gr, w_er, b_er, w_gate, w_up, w_down, ln3_g, ln3_b):
    bp, tp, d = x_prompt.shape
    bs, ts, _ = x_sample.shape
    tm_p, tm_s = 512, bs * ts
    xp = x_prompt.reshape(bp * tp, d)
    xs = _ln(x_sample.reshape(bs * ts, d), emb_ln_g, emb_ln_b, tm_s)
    mk4, mv4, mk2, mv2 = _mem_proj(mem_prompt, ca_wk, ca_wv)
    outs = [[] for _ in range(7)]
    sc = None
    for l in range(DEPTH):
        p = _layer_weights(l, w_in, b_i, b_f, pool_w, pool_scale, mlstm_norm_g, w_out, ln1_g, ln1_b, ca_wq,
                           ca_wo, ln2_g, ln2_b, w_gr, b_gr, w_er, b_er, w_gate, w_up, w_down, ln3_g, ln3_b)
        if l == 0:
            p["first"] = (emb_ln_g, emb_ln_b, state_mlstm_C)
        if l > 0:
            u_s, og_s, step = _sample_inproj(xs, bs * ts, (state_mlstm_C, state_mlstm_n, state_mlstm_m[l], sc), p)
        xp, pb, c1, n1, m1 = _prompt_mixers(xp, bp, tp, l, (mk2, mv2), p, tm_p, 2 * tm_p, ride=step if l > 0 else None)
        if l == 0:
            sc = p["zeros"]
            u_s, og_s, step = _sample_inproj(xs, bs * ts, (state_mlstm_C, state_mlstm_n, state_mlstm_m[l], sc), p)
        xs, qh, sb, sc, ns, ms = _sample_mixers(xs, bs * ts, l, cache_pool[l], u_s, og_s, step, p)
        xp, ctx = _moe(xp, p["wr"], p["br"], *p["experts"], *p["ln3"], 2 * tm_p, attn=(qh, cache_mem_k, cache_mem_v, l))
        xs = _mm_res_ln(ctx[:, :CA_HEADS].reshape(bs * ts, d), xs, p["wo"], *p["ln2"], tm_s)
        xs = _moe(xs, p["wr"], p["br"], *p["experts"], *p["ln3"], tm_s)
        for lst, val in zip(outs, (pb, c1, n1, m1, sb, ns, ms)):
            lst.append(val)
    pp, pc, pn, pm, sp, sn, sm = (jnp.stack(o) for o in outs)
    return (xp.reshape(bp, tp, d), xs.reshape(bs, ts, d), pp, pc, pn, pm, mk4, mv4, sp, sc, sn, sm)
```
